```python
import math
import jax, jax.numpy as jnp
from jax import lax
import numpy as np

D_MODEL = 1024
BATCH = 2
SEQ = 8192
DEPTH = 1
DEC_BATCH = 8
DEC_SEQ = 16
PAST_LEN = 2048

CHUNK = 64
MIX_WIDTH = D_MODEL
SSD_WIDTH = MIX_WIDTH // 2
SSD_HEAD_DIM = 64
SSD_HEADS = SSD_WIDTH // SSD_HEAD_DIM
SSD_GROUPS = 2
SSD_REP = SSD_HEADS // SSD_GROUPS
SSD_STATE = 128
SSD_CONV = 4
SSD_BLOCK = 64
SSD_XBC = SSD_WIDTH + 2 * SSD_GROUPS * SSD_STATE
SSD_NORM_GROUP = SSD_WIDTH // SSD_GROUPS
SSD_NORM_EPS = 1e-5
ATT_WIDTH = MIX_WIDTH - SSD_WIDTH
ATT_HEAD_DIM = 64
ATT_HEADS = ATT_WIDTH // (2 * ATT_HEAD_DIM)
ATT_V_DIM = 2 * ATT_HEAD_DIM
ATT_NORM_EPS = 1e-5
Q_BLOCK = 128
ROPE_THETA = 10000.0
MEM_LEN = 256
MEM_HEADS = 4
MEM_HEAD_DIM = D_MODEL // MEM_HEADS
FFN_HIDDEN = -(-8 * D_MODEL // (3 * 256)) * 256
NORM_EPS = 1e-6
IN_COLS = SSD_WIDTH + SSD_XBC + SSD_HEADS + 3 * ATT_WIDTH
IN_SPLITS = [SSD_WIDTH, SSD_WIDTH + SSD_XBC, SSD_WIDTH + SSD_XBC + SSD_HEADS,
             SSD_WIDTH + SSD_XBC + SSD_HEADS + ATT_WIDTH,
             SSD_WIDTH + SSD_XBC + SSD_HEADS + 2 * ATT_WIDTH]

kernel_name = 'hybrid_ssd_diffattn_stream_step'


def rms_norm(x, g, eps=NORM_EPS):
    xf = x.astype(jnp.float32)
    y = xf * lax.rsqrt(jnp.mean(xf * xf, axis=-1, keepdims=True) + eps)
    return (y * g.astype(jnp.float32)).astype(x.dtype)


def rope(x, pos):
    d = x.shape[-1]
    inv = jnp.power(ROPE_THETA, -jnp.arange(0, d, 2, dtype=jnp.float32) / d)
    ang = pos[:, None] * inv[None, :]
    cos = jnp.cos(ang)[None, :, None, None, :]
    sin = jnp.sin(ang)[None, :, None, None, :]
    xf = x.astype(jnp.float32)
    x1, x2 = xf[..., : d // 2], xf[..., d // 2:]
    return jnp.concatenate([x1 * cos - x2 * sin, x2 * cos + x1 * sin], axis=-1).astype(x.dtype)


def causal_conv(buf, u, w, b):
    ext = jnp.concatenate([buf.astype(u.dtype), u], axis=1)
    y = lax.conv_general_dilated(ext, w[:, None, :].astype(u.dtype), window_strides=(1,), padding='VALID',
                                 dimension_numbers=('NWC', 'WIO', 'NWC'), feature_group_count=u.shape[-1])
    return y + b.astype(u.dtype), ext[:, -(SSD_CONV - 1):]


def ssd_scan(x, dt, a, bmat, cmat, h0, blk):
    bsz, l, _, p = x.shape
    nc = l // blk
    xdt = (x * dt[..., None]).reshape(bsz, nc, blk, SSD_GROUPS, SSD_REP, p)
    bm = bmat.reshape(bsz, nc, blk, SSD_GROUPS, SSD_STATE)
    cm = cmat.reshape(bsz, nc, blk, SSD_GROUPS, SSD_STATE)
    ad = (dt * a).reshape(bsz, nc, blk, SSD_GROUPS, SSD_REP).transpose(0, 3, 4, 1, 2)
    acum = jnp.cumsum(ad, axis=-1)
    tri = jnp.tril(jnp.ones((blk, blk), dtype=bool))
    lmat = jnp.exp(jnp.where(tri, acum[..., :, None] - acum[..., None, :], -jnp.inf))
    cb = jnp.einsum('bclgn,bcsgn->bgcls', cm, bm)
    y_diag = jnp.einsum('bgrcls,bcsgrp->bclgrp', cb[:, :, None] * lmat, xdt)
    decay_in = jnp.exp(acum[..., -1:] - acum)
    chunk_states = jnp.einsum('bclgn,bgrcl,bclgrp->bcgrpn', bm, decay_in, xdt)
    chunk_decay = jnp.exp(acum[..., -1])

    def step(s, inp):
        st, dc = inp
        return s * dc[..., None, None] + st, s

    h0g = h0.reshape(bsz, SSD_GROUPS, SSD_REP, p, SSD_STATE)
    final, prev = lax.scan(step, h0g, (jnp.moveaxis(chunk_states, 1, 0), jnp.moveaxis(chunk_decay, -1, 0)))
    prev = jnp.moveaxis(prev, 0, 1)
    y_off = jnp.einsum('bclgn,bcgrpn,bgrcl->bclgrp', cm, prev, jnp.exp(acum))
    y = (y_diag + y_off).reshape(bsz, l, SSD_HEADS, p)
    return y, final.reshape(bsz, SSD_HEADS, p, SSD_STATE)


def diff_attn_block(q, k, v, qpos, kpos, lam):
    s = jnp.einsum('bqhcd,bkhcd->bhcqk', q.astype(jnp.float32), k.astype(jnp.float32)) / math.sqrt(ATT_HEAD_DIM)
    visible = (kpos[None, :] // CHUNK) <= (qpos[:, None] // CHUNK)
    p = jax.nn.softmax(jnp.where(visible, s, -jnp.inf), axis=-1)
    att = p[:, :, 0] - lam * p[:, :, 1]
    return jnp.einsum('bhqk,bkhe->bqhe', att, v.astype(jnp.float32))


def diff_attention(q, k, v, qpos, kpos, lam):
    bsz, sq = q.shape[0], q.shape[1]
    if sq % Q_BLOCK != 0:
        return diff_attn_block(q, k, v, qpos, kpos, lam)
    nb = sq // Q_BLOCK
    qb = jnp.moveaxis(q.reshape(bsz, nb, Q_BLOCK, ATT_HEADS, 2, ATT_HEAD_DIM), 1, 0)
    pb = qpos.reshape(nb, Q_BLOCK)
    out = lax.map(lambda t: diff_attn_block(t[0], k, v, t[1], kpos, lam), (qb, pb))
    return jnp.moveaxis(out, 0, 1).reshape(bsz, sq, ATT_HEADS, ATT_V_DIM)


def memory_kv(mem, g_mem, wk_x, wv_x):
    mn = rms_norm(mem, g_mem)
    b = mem.shape[0]
    mk = (mn @ wk_x).reshape(b, MEM_LEN, MEM_HEADS, MEM_HEAD_DIM)
    mv = (mn @ wv_x).reshape(b, MEM_LEN, MEM_HEADS, MEM_HEAD_DIM)
    return mk, mv


def layer_step(x, conv_buf, ssm0, k_past, v_past, mem_k, mem_v, lam_init,
               w_in, conv_w, conv_b, dt_bias, a_log, d_skip, ssm_norm_w,
               lam_q1, lam_k1, lam_q2, lam_k2, subln_w, w_out, wq_x, wo_x,
               g_pre_mix, g_post_mix, g_pre_x, g_post_x, g_pre_ffn, g_post_ffn,
               w_gate, w_up, w_down):
    f32 = jnp.float32
    bsz, l, _ = x.shape
    dtype = x.dtype
    past = k_past.shape[1]
    pos = past + jnp.arange(l, dtype=jnp.int32)

    hn = rms_norm(x, g_pre_mix)
    z, xbc, dt_raw, q, k, v = jnp.split(hn @ w_in, IN_SPLITS, axis=-1)

    xbc, conv_new = causal_conv(conv_buf, xbc, conv_w, conv_b)
    xbc = jax.nn.silu(xbc.astype(f32))
    xs, bm, cm = jnp.split(xbc, [SSD_WIDTH, SSD_WIDTH + SSD_GROUPS * SSD_STATE], axis=-1)
    dt = jax.nn.softplus(dt_raw.astype(f32) + dt_bias.astype(f32))
    a = -jnp.exp(a_log.astype(f32))
    blk = SSD_BLOCK if l % SSD_BLOCK == 0 else l
    xh = xs.reshape(bsz, l, SSD_HEADS, SSD_HEAD_DIM)
    y, ssm_new = ssd_scan(xh, dt, a, bm.reshape(bsz, l, SSD_GROUPS, SSD_STATE),
                          cm.reshape(bsz, l, SSD_GROUPS, SSD_STATE), ssm0.astype(f32), blk)
    y = y + d_skip.astype(f32)[:, None] * xh
    y = y.reshape(bsz, l, SSD_WIDTH) * jax.nn.silu(z.astype(f32))
    yg = y.reshape(bsz, l, SSD_GROUPS, SSD_NORM_GROUP)
    yg = yg * lax.rsqrt(jnp.mean(yg * yg, axis=-1, keepdims=True) + SSD_NORM_EPS)
    y_ssd = (yg.reshape(bsz, l, SSD_WIDTH) * ssm_norm_w.astype(f32)).astype(dtype)

    posf = pos.astype(f32)
    q = rope(q.reshape(bsz, l, ATT_HEADS, 2, ATT_HEAD_DIM), posf)
    k = rope(k.reshape(bsz, l, ATT_HEADS, 2, ATT_HEAD_DIM), posf)
    v = v.reshape(bsz, l, ATT_HEADS, ATT_V_DIM)
    k_all = jnp.concatenate([k_past.astype(dtype), k], axis=1)
    v_all = jnp.concatenate([v_past.astype(dtype), v], axis=1)
    kpos = jnp.arange(past + l, dtype=jnp.int32)
    lam = (jnp.exp(jnp.sum(lam_q1.astype(f32) * lam_k1.astype(f32)))
           - jnp.exp(jnp.sum(lam_q2.astype(f32) * lam_k2.astype(f32))) + lam_init)
    o = diff_attention(q, k_all, v_all, pos, kpos, lam)
    o = o * lax.rsqrt(jnp.mean(o * o, axis=-1, keepdims=True) + ATT_NORM_EPS) * subln_w.astype(f32) * (1.0 - lam_init)
    y_att = o.reshape(bsz, l, ATT_WIDTH).astype(dtype)

    mix = jnp.concatenate([y_ssd, y_att], axis=-1) @ w_out
    h = x + rms_norm(mix, g_post_mix)

    qx = (rms_norm(h, g_pre_x) @ wq_x).reshape(bsz, l, MEM_HEADS, MEM_HEAD_DIM)
    s = jnp.einsum('bqhd,bmhd->bhqm', qx.astype(f32), mem_k.astype(f32)) / math.sqrt(MEM_HEAD_DIM)
    p = jax.nn.softmax(s, axis=-1)
    ox = jnp.einsum('bhqm,bmhd->bqhd', p, mem_v.astype(f32)).reshape(bsz, l, D_MODEL).astype(dtype)
    h = h + rms_norm(ox @ wo_x, g_post_x)

    hn = rms_norm(h, g_pre_ffn)
    f = (jax.nn.silu(hn @ w_gate) * (hn @ w_up)) @ w_down
    h = h + rms_norm(f, g_post_ffn)
    return h, k, v, ssm_new.astype(dtype), conv_new


def setup_inputs(seed: int = 0) -> dict:
    key = jax.random.key(seed)
    ks = iter(jax.random.split(key, 48))
    L = DEPTH

    def nrm(shape, scale):
        return jax.random.normal(next(ks), shape, jnp.float32) * scale

    def gain(n):
        return 1.0 + nrm((L, n), 0.05)

    dt0 = jnp.exp(jax.random.uniform(next(ks), (L, SSD_HEADS), jnp.float32,
                                     minval=math.log(1e-3), maxval=math.log(1e-1)))
    return {
        'x_prompt': nrm((BATCH, SEQ, D_MODEL), 1.0),
        'x_sample': nrm((DEC_BATCH, DEC_SEQ, D_MODEL), 1.0),
        'cache_attn_k': nrm((L, DEC_BATCH, PAST_LEN, ATT_HEADS, 2, ATT_HEAD_DIM), 1.0),
        'cache_attn_v': nrm((L, DEC_BATCH, PAST_LEN, ATT_HEADS, ATT_V_DIM), 1.0),
        'state_ssm': nrm((L, DEC_BATCH, SSD_HEADS, SSD_HEAD_DIM, SSD_STATE), 0.1),
        'state_conv': nrm((L, DEC_BATCH, SSD_CONV - 1, SSD_XBC), 1.0),
        'cache_mem_k': nrm((L, DEC_BATCH, MEM_LEN, MEM_HEADS, MEM_HEAD_DIM), 1.0),
        'cache_mem_v': nrm((L, DEC_BATCH, MEM_LEN, MEM_HEADS, MEM_HEAD_DIM), 1.0),
        'mem_prompt': nrm((BATCH, MEM_LEN, D_MODEL), 1.0),
        'w_in': nrm((L, D_MODEL, IN_COLS), D_MODEL ** -0.5),
        'conv_w': nrm((L, SSD_CONV, SSD_XBC), SSD_CONV ** -0.5),
        'conv_b': nrm((L, SSD_XBC), 0.02),
        'dt_bias': dt0 + jnp.log(-jnp.expm1(-dt0)),
        'a_log': jnp.log(jax.random.uniform(next(ks), (L, SSD_HEADS), jnp.float32, minval=1.0, maxval=16.0)),
        'd_skip': 1.0 + nrm((L, SSD_HEADS), 0.1),
        'ssm_norm_w': gain(SSD_WIDTH),
        'lam_q1': nrm((L, ATT_HEAD_DIM), 0.1),
        'lam_k1': nrm((L, ATT_HEAD_DIM), 0.1),
        'lam_q2': nrm((L, ATT_HEAD_DIM), 0.1),
        'lam_k2': nrm((L, ATT_HEAD_DIM), 0.1),
        'subln_w': gain(ATT_V_DIM),
        'w_out': nrm((L, MIX_WIDTH, D_MODEL), MIX_WIDTH ** -0.5),
        'g_mem': gain(D_MODEL),
        'wq_x': nrm((L, D_MODEL, D_MODEL), D_MODEL ** -0.5),
        'wk_x': nrm((L, D_MODEL, D_MODEL), D_MODEL ** -0.5),
        'wv_x': nrm((L, D_MODEL, D_MODEL), D_MODEL ** -0.5),
        'wo_x': nrm((L, D_MODEL, D_MODEL), D_MODEL ** -0.5),
        'g_pre_mix': gain(D_MODEL),
        'g_post_mix': gain(D_MODEL),
        'g_pre_x': gain(D_MODEL),
        'g_post_x': gain(D_MODEL),
        'g_pre_ffn': gain(D_MODEL),
        'g_post_ffn': gain(D_MODEL),
        'w_gate': nrm((L, D_MODEL, FFN_HIDDEN), D_MODEL ** -0.5),
        'w_up': nrm((L, D_MODEL, FFN_HIDDEN), D_MODEL ** -0.5),
        'w_down': nrm((L, FFN_HIDDEN, D_MODEL), FFN_HIDDEN ** -0.5),
    }


def reference(x_prompt, x_sample, cache_attn_k, cache_attn_v, state_ssm, state_conv, cache_mem_k, cache_mem_v,
              mem_prompt, w_in, conv_w, conv_b, dt_bias, a_log, d_skip, ssm_norm_w,
              lam_q1, lam_k1, lam_q2, lam_k2, subln_w, w_out, g_mem, wq_x, wk_x, wv_x, wo_x,
              g_pre_mix, g_post_mix, g_pre_x, g_post_x, g_pre_ffn, g_post_ffn, w_gate, w_up, w_down):
    dtype = x_prompt.dtype
    bp = x_prompt.shape[0]
    hp, hs = x_prompt, x_sample
    pk, pv, pssm, pconv, pmk, pmv, sk, sv, sssm, sconv = ([] for _ in range(10))
    for i in range(DEPTH):
        lam_init = 0.8 - 0.6 * math.exp(-0.3 * i)
        lw = (w_in[i], conv_w[i], conv_b[i], dt_bias[i], a_log[i], d_skip[i], ssm_norm_w[i],
              lam_q1[i], lam_k1[i], lam_q2[i], lam_k2[i], subln_w[i], w_out[i], wq_x[i], wo_x[i],
              g_pre_mix[i], g_post_mix[i], g_pre_x[i], g_post_x[i], g_pre_ffn[i], g_post_ffn[i],
              w_gate[i], w_up[i], w_down[i])
        mk, mv = memory_kv(mem_prompt, g_mem[i], wk_x[i], wv_x[i])
        hp, k_new, v_new, ssm_new, conv_new = layer_step(
            hp,
            jnp.zeros((bp, SSD_CONV - 1, SSD_XBC), dtype),
            jnp.zeros((bp, SSD_HEADS, SSD_HEAD_DIM, SSD_STATE), jnp.float32),
            jnp.zeros((bp, 0, ATT_HEADS, 2, ATT_HEAD_DIM), dtype),
            jnp.zeros((bp, 0, ATT_HEADS, ATT_V_DIM), dtype),
            mk, mv, lam_init, *lw)
        pk.append(k_new); pv.append(v_new); pssm.append(ssm_new); pconv.append(conv_new)
        pmk.append(mk); pmv.append(mv)
        hs, k_new, v_new, ssm_new, conv_new = layer_step(
            hs, state_conv[i], state_ssm[i], cache_attn_k[i], cache_attn_v[i],
            cache_mem_k[i], cache_mem_v[i], lam_init, *lw)
        sk.append(k_new); sv.append(v_new); sssm.append(ssm_new); sconv.append(conv_new)
    return (hp, hs,
            jnp.stack(pk), jnp.stack(pv), jnp.stack(pssm), jnp.stack(pconv), jnp.stack(pmk), jnp.stack(pmv),
            jnp.stack(sk), jnp.stack(sv), jnp.stack(sssm), jnp.stack(sconv))
```

```python
import functools
import math

import jax
import jax.numpy as jnp
from jax import lax
from jax.experimental import pallas as pl
from jax.experimental.pallas import tpu as pltpu

D_MODEL = 1024
CHUNK = 64
SSD_WIDTH = 512
SSD_HEAD_DIM = 64
SSD_HEADS = 8
SSD_GROUPS = 2
SSD_REP = 4
SSD_STATE = 128
SSD_CONV = 4
SSD_XBC = 1024
SSD_NORM_GROUP = 256
SSD_NORM_EPS = 1e-5
ATT_WIDTH = 512
ATT_HEAD_DIM = 64
ATT_HEADS = 4
ATT_V_DIM = 128
ATT_NORM_EPS = 1e-5
ROPE_THETA = 10000.0
MEM_LEN = 256
MEM_HEADS = 4
MEM_HEAD_DIM = 256
FFN_HIDDEN = 2816
NORM_EPS = 1e-6
LANES = 128
DT_PAD = LANES
IN_COLS_PADDED = SSD_WIDTH + SSD_XBC + 3 * ATT_WIDTH + DT_PAD
VMEM_LIMIT = 56 * 1024 * 1024
NEG_BIG = -1e30

F32 = jnp.float32
BF16 = jnp.bfloat16


def _const_spec(shape):
    return pl.BlockSpec(shape, lambda *_: (0,) * len(shape), pipeline_mode=pl.Buffered(1))


def _rms(x, g, eps):
    return x * lax.rsqrt(jnp.mean(x * x, axis=-1, keepdims=True) + eps) * g


def _silu(x):
    return x / (1.0 + jnp.exp(-x))


def _dot(a, b):
    return jnp.dot(a, b, preferred_element_type=F32)


def _mem_kv_kernel(mem_ref, g_ref, wk_ref, wv_ref, mk_ref, mv_ref, mkb_ref, mvb_ref):
    mn = _rms(mem_ref[...], g_ref[...], NORM_EPS).astype(BF16)
    mk = _dot(mn, wk_ref[...])
    mv = _dot(mn, wv_ref[...])
    mk_ref[...] = mk
    mv_ref[...] = mv
    mkb_ref[...] = mk.astype(BF16)
    mvb_ref[...] = mv.astype(BF16)


def _mem_kv(mem, g_mem, wk, wv):
    n = mem.shape[0]
    tm = MEM_LEN
    row = pl.BlockSpec((tm, D_MODEL), lambda i: (i, 0))
    return pl.pallas_call(
        _mem_kv_kernel,
        grid=(n // tm,),
        in_specs=[row, _const_spec((1, D_MODEL)), _const_spec((D_MODEL, D_MODEL)), _const_spec((D_MODEL, D_MODEL))],
        out_specs=[row, row, row, row],
        out_shape=[jax.ShapeDtypeStruct((n, D_MODEL), F32), jax.ShapeDtypeStruct((n, D_MODEL), F32),
                   jax.ShapeDtypeStruct((n, D_MODEL), BF16), jax.ShapeDtypeStruct((n, D_MODEL), BF16)],
        compiler_params=pltpu.CompilerParams(dimension_semantics=("arbitrary",), vmem_limit_bytes=VMEM_LIMIT),
        name="mem_kv",
    )(mem, g_mem, wk, wv)


def _in_proj_kernel(x_ref, g_ref, w_ref, cos_ref, sin_ref,
                    z_ref, xbc_ref, tail_ref, dt_ref, q_ref, k_ref, kt_ref, v_ref, vb_ref, *, tm, tk):
    hn = _rms(x_ref[...], g_ref[...], NORM_EPS).astype(BF16)
    acc = _dot(hn, w_ref[...])
    o = 0
    z_ref[...] = acc[:, o:o + SSD_WIDTH].astype(BF16)
    o += SSD_WIDTH
    xbc = acc[:, o:o + SSD_XBC]
    xbc_ref[...] = xbc.astype(BF16)
    tail_ref[0] = xbc[tm - 8:, :]
    o += SSD_XBC
    q = acc[:, o:o + ATT_WIDTH]
    o += ATT_WIDTH
    k = acc[:, o:o + ATT_WIDTH]
    o += ATT_WIDTH
    v = acc[:, o:o + ATT_WIDTH]
    o += ATT_WIDTH
    dt_ref[...] = acc[:, o:o + DT_PAD]
    v_ref[...] = v
    vb_ref[...] = v.astype(BF16)

    cos = cos_ref[...]
    sin = sin_ref[...]
    first_half = (lax.broadcasted_iota(jnp.int32, (tm, LANES), 1) % ATT_HEAD_DIM) < (ATT_HEAD_DIM // 2)

    def rope(t):
        swapped = jnp.where(first_half, pltpu.roll(t, LANES - ATT_HEAD_DIM // 2, 1),
                            pltpu.roll(t, ATT_HEAD_DIM // 2, 1))
        return t * cos + swapped * sin

    q_scale = 1.0 / math.sqrt(ATT_HEAD_DIM)
    for j in range(ATT_WIDTH // LANES):
        sl = slice(j * LANES, (j + 1) * LANES)
        q_ref[:, sl] = (rope(q[:, sl]) * q_scale).astype(BF16)
        kr = rope(k[:, sl])
        k_ref[:, sl] = kr
        if kt_ref is not None:
            for c in range(tm // tk):
                kt_ref[c, sl, :] = kr[c * tk:(c + 1) * tk, :].T.astype(BF16)


def _in_proj(x, g, w, cos, sin, *, seq, tm, tk):
    n = x.shape[0]
    nt = n // tm
    tiles_per_seq = seq // tm
    row = lambda c: pl.BlockSpec((tm, c), lambda i: (i, 0))
    tab = pl.BlockSpec((tm, LANES), lambda i: (i % tiles_per_seq, 0))
    out_specs = [row(SSD_WIDTH), row(SSD_XBC), pl.BlockSpec((1, 8, SSD_XBC), lambda i: (i, 0, 0)), row(DT_PAD),
                 row(ATT_WIDTH), row(ATT_WIDTH)]
    out_shape = [jax.ShapeDtypeStruct((n, SSD_WIDTH), BF16), jax.ShapeDtypeStruct((n, SSD_XBC), BF16),
                 jax.ShapeDtypeStruct((nt, 8, SSD_XBC), F32), jax.ShapeDtypeStruct((n, DT_PAD), F32),
                 jax.ShapeDtypeStruct((n, ATT_WIDTH), BF16), jax.ShapeDtypeStruct((n, ATT_WIDTH), F32)]
    if tk:
        out_specs.append(pl.BlockSpec((tm // tk, ATT_WIDTH, tk), lambda i: (i, 0, 0)))
        out_shape.append(jax.ShapeDtypeStruct((n // tk, ATT_WIDTH, tk), BF16))
    out_specs += [row(ATT_WIDTH), row(ATT_WIDTH)]
    out_shape += [jax.ShapeDtypeStruct((n, ATT_WIDTH), F32), jax.ShapeDtypeStruct((n, ATT_WIDTH), BF16)]

    def body(x_ref, g_ref, w_ref, cos_ref, sin_ref, z_ref, xbc_ref, tail_ref, dt_ref, q_ref, k_ref, *rest):
        kt_ref = rest[0] if tk else None
        v_ref, vb_ref = rest[-2:]
        _in_proj_kernel(x_ref, g_ref, w_ref, cos_ref, sin_ref, z_ref, xbc_ref, tail_ref, dt_ref, q_ref, k_ref,
                        kt_ref, v_ref, vb_ref, tm=tm, tk=tk)

    outs = pl.pallas_call(
        body,
        grid=(nt,),
        in_specs=[row(D_MODEL), _const_spec((1, D_MODEL)), _const_spec((D_MODEL, IN_COLS_PADDED)), tab, tab],
        out_specs=out_specs,
        out_shape=out_shape,
        compiler_params=pltpu.CompilerParams(dimension_semantics=("arbitrary",), vmem_limit_bytes=VMEM_LIMIT),
        name="in_proj",
    )(x, g, w, cos, sin)
    outs = list(outs)
    if not tk:
        outs = outs[:6] + [None] + outs[6:]
    return outs


def _expand_heads(x, e2):
    hi = x.astype(BF16)
    lo = (x - hi.astype(F32)).astype(BF16)
    return _dot(jnp.concatenate([hi, lo], axis=1), e2)


def _ssd_kernel(z_ref, xbc_ref, dt_ref, cbuf_ref, h0_ref, cw_ref, cb_ref, dtb_ref, alog_ref, dsk_ref, nw_ref, e2_ref,
                y_ref, hout_ref, state, ext, *, q, valid):
    c = pl.program_id(1)

    @pl.when(c == 0)
    def _():
        state[...] = h0_ref[0]
        ext[0:8, :] = cbuf_ref[0]

    ext[8:8 + q, :] = xbc_ref[...].astype(F32)
    conv = cb_ref[...] + ext[5:5 + q, :] * cw_ref[0:1, :]
    for w in range(1, SSD_CONV):
        conv = conv + ext[5 + w:5 + w + q, :] * cw_ref[w:w + 1, :]
    ext[0:8, :] = ext[q:q + 8, :]
    u = _silu(conv)
    xs = u[:, :SSD_WIDTH]
    bm = u[:, SSD_WIDTH:SSD_WIDTH + SSD_GROUPS * SSD_STATE]
    cm = u[:, SSD_WIDTH + SSD_GROUPS * SSD_STATE:]

    dtr = dt_ref[...] + dtb_ref[...]
    dt = jnp.maximum(dtr, 0.0) + jnp.log(1.0 + jnp.exp(-jnp.abs(dtr)))
    if valid is not None:
        row = lax.broadcasted_iota(jnp.int32, (q, DT_PAD), 0) + c * q
        dt = jnp.where(row < valid, dt, 0.0)
    a = -jnp.exp(alog_ref[...])
    ad = dt * a
    ri = lax.broadcasted_iota(jnp.int32, (q, q), 0)
    ci = lax.broadcasted_iota(jnp.int32, (q, q), 1)
    tril = ri >= ci
    acum = jnp.dot(tril.astype(F32), ad, preferred_element_type=F32, precision=lax.Precision.HIGHEST)
    acum_t = acum.T
    tot = acum[q - 1:q, :]
    e2 = e2_ref[...]
    expanded = _expand_heads(jnp.concatenate([dt, dt * jnp.exp(tot - acum), jnp.exp(acum)], axis=0), e2)
    dtx = expanded[0:q]
    ddx = expanded[q:2 * q]
    eax = expanded[2 * q:3 * q]
    dsk = _expand_heads(jnp.broadcast_to(dsk_ref[...], (8, DT_PAD)), e2)[0:1]

    xdt = (xs * dtx).astype(BF16)
    xdtd = (xs * ddx).astype(BF16)
    bm_t = bm.T.astype(BF16)
    cmb = cm.astype(BF16)
    gw = SSD_REP * SSD_HEAD_DIM
    stripe = lax.broadcasted_iota(jnp.int32, (q, gw), 1) // SSD_HEAD_DIM
    ys = []
    for g in range(SSD_GROUPS):
        cm_g = cmb[:, g * SSD_STATE:(g + 1) * SSD_STATE]
        bt_g = bm_t[g * SSD_STATE:(g + 1) * SSD_STATE, :]
        cbm = _dot(cm_g, bt_g)
        ms = []
        for r in range(SSD_REP):
            h = g * SSD_REP + r
            diff = acum[:, h:h + 1] - acum_t[h:h + 1, :]
            ms.append((cbm * jnp.exp(jnp.where(tril, diff, -jnp.inf))).astype(BF16))
        ydf = _dot(jnp.concatenate(ms, axis=0), xdt[:, g * gw:(g + 1) * gw])
        yd = ydf[0:q]
        for r in range(1, SSD_REP):
            yd = jnp.where(stripe == r, ydf[r * q:(r + 1) * q], yd)
        st = state[g]
        y_off = _dot(cm_g, st.astype(BF16)) * eax[:, g * gw:(g + 1) * gw]
        state[g] = st * eax[q - 1:q, g * gw:(g + 1) * gw] + _dot(bt_g, xdtd[:, g * gw:(g + 1) * gw])
        ys.append(yd + y_off)
    y = jnp.concatenate(ys, axis=1) + dsk * xs
    y = y * _silu(z_ref[...].astype(F32))
    outs = []
    for g in range(SSD_WIDTH // SSD_NORM_GROUP):
        yg = y[:, g * SSD_NORM_GROUP:(g + 1) * SSD_NORM_GROUP]
        outs.append(yg * lax.rsqrt(jnp.mean(yg * yg, axis=-1, keepdims=True) + SSD_NORM_EPS))
    y_ref[...] = (jnp.concatenate(outs, axis=1) * nw_ref[...]).astype(BF16)

    @pl.when(c == pl.num_programs(1) - 1)
    def _():
        hout_ref[0] = state[...]


def _ssd(z, xbc, dt, cbuf, h0, cw, cb, dtb, alog, dsk, nw, e2, *, nb, seq, q, valid):
    nc = seq // q
    row = lambda c: pl.BlockSpec((q, c), lambda b, i: (b * nc + i, 0))
    gw = SSD_REP * SSD_HEAD_DIM
    st_spec = pl.BlockSpec((1, SSD_GROUPS, SSD_STATE, gw), lambda b, i: (b, 0, 0, 0))
    return pl.pallas_call(
        functools.partial(_ssd_kernel, q=q, valid=valid),
        grid=(nb, nc),
        in_specs=[row(SSD_WIDTH), row(SSD_XBC), row(DT_PAD),
                  pl.BlockSpec((1, 8, SSD_XBC), lambda b, i: (b, 0, 0)), st_spec,
                  _const_spec((SSD_CONV, SSD_XBC)), _const_spec((1, SSD_XBC)), _const_spec((1, DT_PAD)),
                  _const_spec((1, DT_PAD)), _const_spec((1, DT_PAD)), _const_spec((1, SSD_WIDTH)),
                  _const_spec((2 * DT_PAD, SSD_WIDTH))],
        out_specs=[row(SSD_WIDTH), st_spec],
        out_shape=[jax.ShapeDtypeStruct((nb * seq, SSD_WIDTH), BF16),
                   jax.ShapeDtypeStruct((nb, SSD_GROUPS, SSD_STATE, gw), F32)],
        scratch_shapes=[pltpu.VMEM((SSD_GROUPS, SSD_STATE, gw), F32), pltpu.VMEM((q + 8, SSD_XBC), F32)],
        compiler_params=pltpu.CompilerParams(dimension_semantics=("arbitrary", "arbitrary"),
                                             vmem_limit_bytes=VMEM_LIMIT),
        name="ssd",
    )(z, xbc, dt, cbuf, h0, cw, cb, dtb, alog, dsk, nw, e2)


def _diff_attn_kernel(q_ref, kt_ref, v_ref, lam_ref, sw_ref, o_ref, *, tq, tk, past, kv_len, lam_init):
    i = pl.program_id(2)
    qb = q_ref[...]
    lane = lax.broadcasted_iota(jnp.int32, (tq, LANES), 1)
    zero = jnp.zeros_like(qb)
    q2 = jnp.concatenate([jnp.where(lane < ATT_HEAD_DIM, qb, zero), jnp.where(lane >= ATT_HEAD_DIM, qb, zero)], axis=0)

    q_lo = past + i * tq
    q_hi = q_lo + tq - 1
    lim_lo = jnp.minimum((q_lo // CHUNK + 1) * CHUNK, kv_len)
    lim_hi = jnp.minimum((q_hi // CHUNK + 1) * CHUNK, kv_len)
    n_full = lim_lo // tk
    n_end = (lim_hi + tk - 1) // tk

    def step(j, carry, masked):
        m, l, acc = carry
        s = _dot(q2, kt_ref[j])
        if masked:
            qpos = q_lo + lax.broadcasted_iota(jnp.int32, (2 * tq, tk), 0) % tq
            kpos = j * tk + lax.broadcasted_iota(jnp.int32, (2 * tq, tk), 1)
            visible = (kpos // CHUNK <= qpos // CHUNK) & (kpos < kv_len)
            s = jnp.where(visible, s, NEG_BIG)
        m_new = jnp.maximum(m, jnp.max(s, axis=-1, keepdims=True))
        alpha = jnp.exp(m - m_new)
        p = jnp.exp(s - m_new)
        l = alpha * l + jnp.sum(p, axis=-1, keepdims=True)
        acc = alpha * acc + _dot(p.astype(BF16), v_ref[pl.ds(pl.multiple_of(j * tk, tk), tk), :])
        return m_new, l, acc

    init = (jnp.full((2 * tq, 1), NEG_BIG, F32), jnp.zeros((2 * tq, 1), F32), jnp.zeros((2 * tq, ATT_V_DIM), F32))
    carry = lax.fori_loop(0, n_full, functools.partial(step, masked=False), init)
    m, l, acc = lax.fori_loop(n_full, n_end, functools.partial(step, masked=True), carry)

    lam_v = lam_ref[...]
    lam = (jnp.exp(jnp.sum(lam_v[0:1] * lam_v[1:2], axis=-1, keepdims=True))
           - jnp.exp(jnp.sum(lam_v[2:3] * lam_v[3:4], axis=-1, keepdims=True)) + lam_init)
    o = acc / l
    o = o[:tq] - lam * o[tq:]
    o = o * lax.rsqrt(jnp.mean(o * o, axis=-1, keepdims=True) + ATT_NORM_EPS) * sw_ref[...] * (1.0 - lam_init)
    o_ref[...] = o.astype(BF16)


def _diff_attn(q, kt, v, lam_vecs, subln_w, *, nb, seq, tq, tk, past, kv_len, lam_init):
    nq = seq // tq
    nkt = kt.shape[0] // nb
    return pl.pallas_call(
        functools.partial(_diff_attn_kernel, tq=tq, tk=tk, past=past, kv_len=kv_len, lam_init=lam_init),
        grid=(nb, ATT_HEADS, nq),
        in_specs=[pl.BlockSpec((tq, ATT_V_DIM), lambda b, h, i: (b * nq + i, h)),
                  pl.BlockSpec((nkt, ATT_V_DIM, tk), lambda b, h, i: (b, h, 0)),
                  pl.BlockSpec((nkt * tk, ATT_V_DIM), lambda b, h, i: (b, h)),
                  _const_spec((4, ATT_HEAD_DIM)), _const_spec((1, ATT_V_DIM))],
        out_specs=pl.BlockSpec((tq, ATT_V_DIM), lambda b, h, i: (b * nq + i, h)),
        out_shape=jax.ShapeDtypeStruct((nb * seq, ATT_WIDTH), BF16),
        compiler_params=pltpu.CompilerParams(dimension_semantics=("arbitrary", "arbitrary", "arbitrary"),
                                             vmem_limit_bytes=VMEM_LIMIT),
        name="diff_attn",
    )(q, kt, v, lam_vecs, subln_w)


def _post_mix_kernel(x_ref, ys_ref, ya_ref, mk_ref, mv_ref, wo_ref, wq_ref, wox_ref,
                     g1_ref, g2_ref, g3_ref, h_ref):
    mix = _dot(ys_ref[...], wo_ref[0:SSD_WIDTH, :]) + _dot(ya_ref[...], wo_ref[SSD_WIDTH:, :])
    h = x_ref[...] + _rms(mix, g1_ref[...], NORM_EPS)
    qn = _rms(h, g2_ref[...], NORM_EPS).astype(BF16)
    qx = (_dot(qn, wq_ref[...]) * (1.0 / math.sqrt(MEM_HEAD_DIM))).astype(BF16)
    mk = mk_ref[0]
    mv = mv_ref[0]
    oxs = []
    for hd in range(MEM_HEADS):
        sl = slice(hd * MEM_HEAD_DIM, (hd + 1) * MEM_HEAD_DIM)
        s = lax.dot_general(qx[:, sl], mk[:, sl], (((1,), (1,)), ((), ())), preferred_element_type=F32)
        p = jnp.exp(s - jnp.max(s, axis=-1, keepdims=True))
        ox = _dot(p.astype(BF16), mv[:, sl]) / jnp.sum(p, axis=-1, keepdims=True)
        oxs.append(ox.astype(BF16))
    o2 = _dot(jnp.concatenate(oxs, axis=1), wox_ref[...])
    h_ref[...] = h + _rms(o2, g3_ref[...], NORM_EPS)


def _post_mix(x, ys, ya, mk, mv, w_out, wq, wox, g1, g2, g3, *, seq, tm):
    n = x.shape[0]
    tiles_per_seq = seq // tm
    row = lambda c: pl.BlockSpec((tm, c), lambda i: (i, 0))
    mem = pl.BlockSpec((1, MEM_LEN, D_MODEL), lambda i: (i // tiles_per_seq, 0, 0))
    wspec = _const_spec((D_MODEL, D_MODEL))
    gspec = _const_spec((1, D_MODEL))
    return pl.pallas_call(
        _post_mix_kernel,
        grid=(n // tm,),
        in_specs=[row(D_MODEL), row(SSD_WIDTH), row(ATT_WIDTH), mem, mem, wspec, wspec, wspec, gspec, gspec, gspec],
        out_specs=row(D_MODEL),
        out_shape=jax.ShapeDtypeStruct((n, D_MODEL), F32),
        compiler_params=pltpu.CompilerParams(dimension_semantics=("arbitrary",), vmem_limit_bytes=VMEM_LIMIT),
        name="post_mix",
    )(x, ys, ya, mk, mv, w_out, wq, wox, g1, g2, g3)


def _ffn_kernel(h_ref, wg_ref, wu_ref, wd_ref, g1_ref, g2_ref, o_ref):
    h = h_ref[...]
    hn = _rms(h, g1_ref[...], NORM_EPS).astype(BF16)
    act = (_silu(_dot(hn, wg_ref[...])) * _dot(hn, wu_ref[...])).astype(BF16)
    f = _dot(act, wd_ref[...])
    o_ref[...] = h + _rms(f, g2_ref[...], NORM_EPS)


def _ffn(h, wg, wu, wd, g1, g2, *, tm):
    n = h.shape[0]
    row = pl.BlockSpec((tm, D_MODEL), lambda i: (i, 0))
    return pl.pallas_call(
        _ffn_kernel,
        grid=(n // tm,),
        in_specs=[row, _const_spec((D_MODEL, FFN_HIDDEN)), _const_spec((D_MODEL, FFN_HIDDEN)),
                  _const_spec((FFN_HIDDEN, D_MODEL)), _const_spec((1, D_MODEL)), _const_spec((1, D_MODEL))],
        out_specs=row,
        out_shape=jax.ShapeDtypeStruct((n, D_MODEL), F32),
        compiler_params=pltpu.CompilerParams(dimension_semantics=("arbitrary",), vmem_limit_bytes=VMEM_LIMIT),
        name="ffn",
    )(h, wg, wu, wd, g1, g2)


def _rope_tables(past, seq):
    half = ATT_HEAD_DIM // 2
    inv = jnp.power(ROPE_THETA, -jnp.arange(0, ATT_HEAD_DIM, 2, dtype=F32) / ATT_HEAD_DIM)
    pos = (past + jnp.arange(seq, dtype=jnp.int32)).astype(F32)
    ang = pos[:, None] * inv[None, :]
    cos, sin = jnp.cos(ang), jnp.sin(ang)
    reps = LANES // ATT_HEAD_DIM
    assert half * 2 == ATT_HEAD_DIM
    return jnp.tile(jnp.concatenate([cos, cos], axis=-1), (1, reps)), jnp.tile(jnp.concatenate([-sin, sin], axis=-1), (1, reps))


def _state_to_kernel_layout(s):
    b = s.shape[0]
    s = s.reshape(b, SSD_GROUPS, SSD_REP, SSD_HEAD_DIM, SSD_STATE)
    return s.transpose(0, 1, 4, 2, 3).reshape(b, SSD_GROUPS, SSD_STATE, SSD_REP * SSD_HEAD_DIM)


def _state_from_kernel_layout(s):
    b = s.shape[0]
    s = s.reshape(b, SSD_GROUPS, SSD_STATE, SSD_REP, SSD_HEAD_DIM)
    return s.transpose(0, 1, 3, 4, 2).reshape(b, SSD_HEADS, SSD_HEAD_DIM, SSD_STATE)


def _layer(x, conv_buf, ssm0, k_past, v_past, mem_kb, mem_vb, lam_init, p, *, tm, tk, tq, ssd_q):
    nb, seq, _ = x.shape
    n = nb * seq
    past = 0 if k_past is None else k_past.shape[1]
    xf = x.reshape(n, D_MODEL)
    cos, sin = _rope_tables(past, seq)
    emit_kt = k_past is None
    z, xbc, tail, dt, q, k, kt, v, vb = _in_proj(xf, p["g_pre_mix"], p["w_in"], cos, sin, seq=seq, tm=tm,
                                                  tk=tk if emit_kt else 0)

    seq_pad = -(-seq // ssd_q) * ssd_q
    if seq_pad != seq:
        pad = lambda a: jnp.pad(a.reshape(nb, seq, -1), ((0, 0), (0, seq_pad - seq), (0, 0))).reshape(nb * seq_pad, -1)
        z_s, xbc_s, dt_s = pad(z), pad(xbc), pad(dt)
    else:
        z_s, xbc_s, dt_s = z, xbc, dt
    cbuf = jnp.pad(conv_buf.astype(F32), ((0, 0), (8 - (SSD_CONV - 1), 0), (0, 0)))
    y_ssd, h_new = _ssd(z_s, xbc_s, dt_s, cbuf, _state_to_kernel_layout(ssm0.astype(F32)),
                        p["conv_w"], p["conv_b"], p["dt_bias"], p["a_log"], p["d_skip"], p["ssm_norm_w"], p["e2"],
                        nb=nb, seq=seq_pad, q=ssd_q, valid=None if seq_pad == seq else seq)
    if seq_pad != seq:
        y_ssd = y_ssd.reshape(nb, seq_pad, SSD_WIDTH)[:, :seq].reshape(n, SSD_WIDTH)
    ssm_new = _state_from_kernel_layout(h_new)
    ext_tail = tail.reshape(nb, seq // tm, 8, SSD_XBC)[:, -1]
    if seq >= SSD_CONV - 1:
        conv_new = ext_tail[:, 8 - (SSD_CONV - 1):]
    else:
        conv_new = jnp.concatenate([conv_buf.astype(F32), ext_tail[:, 8 - seq:]], axis=1)[:, -(SSD_CONV - 1):]

    if emit_kt:
        kv_len = seq
        kt_all, v_all = kt, vb
    else:
        kv_len = past + seq
        kv_pad = -(-kv_len // tk) * tk
        k_all = jnp.concatenate([k_past.astype(BF16), k.reshape(nb, seq, ATT_WIDTH).astype(BF16)], axis=1)
        k_all = jnp.pad(k_all, ((0, 0), (0, kv_pad - kv_len), (0, 0)))
        kt_all = k_all.reshape(nb, kv_pad // tk, tk, ATT_WIDTH).transpose(0, 1, 3, 2).reshape(-1, ATT_WIDTH, tk)
        v_all = jnp.concatenate([v_past.astype(BF16), vb.reshape(nb, seq, ATT_WIDTH)], axis=1)
        v_all = jnp.pad(v_all, ((0, 0), (0, kv_pad - kv_len), (0, 0))).reshape(-1, ATT_WIDTH)
    y_att = _diff_attn(q, kt_all, v_all, p["lam_vecs"], p["subln_w"], nb=nb, seq=seq, tq=tq, tk=tk,
                       past=past, kv_len=kv_len, lam_init=lam_init)

    h = _post_mix(xf, y_ssd, y_att, mem_kb, mem_vb, p["w_out"], p["wq_x"], p["wo_x"],
                  p["g_post_mix"], p["g_pre_x"], p["g_post_x"], seq=seq, tm=tm)
    out = _ffn(h, p["w_gate"], p["w_up"], p["w_down"], p["g_pre_ffn"], p["g_post_ffn"], tm=tm)
    return (out.reshape(nb, seq, D_MODEL), k.reshape(nb, seq, ATT_HEADS, 2, ATT_HEAD_DIM),
            v.reshape(nb, seq, ATT_HEADS, ATT_V_DIM), ssm_new, conv_new)


def _prep_params(i, w_in, conv_w, conv_b, dt_bias, a_log, d_skip, ssm_norm_w, lam_q1, lam_k1, lam_q2, lam_k2, subln_w,
                 w_out, wq_x, wo_x, g_pre_mix, g_post_mix, g_pre_x, g_post_x, g_pre_ffn, g_post_ffn,
                 w_gate, w_up, w_down):
    w = w_in[i]
    s0 = SSD_WIDTH + SSD_XBC
    s1 = s0 + SSD_HEADS
    w_r = jnp.concatenate([w[:, :s0], w[:, s1:], w[:, s0:s1], jnp.zeros((D_MODEL, DT_PAD - SSD_HEADS), w.dtype)], axis=1)
    head_pad = lambda a: jnp.pad(a[i].astype(F32), (0, DT_PAD - SSD_HEADS)).reshape(1, DT_PAD)
    e = (jnp.arange(DT_PAD)[:, None] == (jnp.arange(SSD_WIDTH)[None, :] // SSD_HEAD_DIM)).astype(BF16)
    row = lambda a: a[i].astype(F32).reshape(1, -1)
    return {
        "w_in": w_r.astype(BF16), "conv_w": conv_w[i].astype(F32), "conv_b": row(conv_b),
        "dt_bias": head_pad(dt_bias), "a_log": head_pad(a_log), "d_skip": head_pad(d_skip),
        "ssm_norm_w": row(ssm_norm_w), "e2": jnp.concatenate([e, e], axis=0),
        "lam_vecs": jnp.stack([lam_q1[i], lam_k1[i], lam_q2[i], lam_k2[i]]).astype(F32), "subln_w": row(subln_w),
        "w_out": w_out[i].astype(BF16), "wq_x": wq_x[i].astype(BF16), "wo_x": wo_x[i].astype(BF16),
        "g_pre_mix": row(g_pre_mix), "g_post_mix": row(g_post_mix), "g_pre_x": row(g_pre_x), "g_post_x": row(g_post_x),
        "g_pre_ffn": row(g_pre_ffn), "g_post_ffn": row(g_post_ffn),
        "w_gate": w_gate[i].astype(BF16), "w_up": w_up[i].astype(BF16), "w_down": w_down[i].astype(BF16),
    }


def kernel(x_prompt, x_sample, cache_attn_k, cache_attn_v, state_ssm, state_conv, cache_mem_k, cache_mem_v, mem_prompt, w_in, conv_w, conv_b, dt_bias, a_log, d_skip, ssm_norm_w, lam_q1, lam_k1, lam_q2, lam_k2, subln_w, w_out, g_mem, wq_x, wk_x, wv_x, wo_x, g_pre_mix, g_post_mix, g_pre_x, g_post_x, g_pre_ffn, g_post_ffn, w_gate, w_up, w_down):
    depth = w_in.shape[0]
    bp, sp, _ = x_prompt.shape
    bs, ss, _ = x_sample.shape
    hp, hs = x_prompt, x_sample
    outs = [[] for _ in range(10)]
    for i in range(depth):
        lam_init = 0.8 - 0.6 * math.exp(-0.3 * i)
        p = _prep_params(i, w_in, conv_w, conv_b, dt_bias, a_log, d_skip, ssm_norm_w, lam_q1, lam_k1, lam_q2, lam_k2,
                         subln_w, w_out, wq_x, wo_x, g_pre_mix, g_post_mix, g_pre_x, g_post_x, g_pre_ffn, g_post_ffn,
                         w_gate, w_up, w_down)
        mk, mv, mkb, mvb = _mem_kv(mem_prompt.reshape(bp * MEM_LEN, D_MODEL), g_mem[i].reshape(1, D_MODEL),
                                   wk_x[i].astype(BF16), wv_x[i].astype(BF16))
        hp, k_new, v_new, ssm_new, conv_new = _layer(
            hp, jnp.zeros((bp, SSD_CONV - 1, SSD_XBC), F32), jnp.zeros((bp, SSD_HEADS, SSD_HEAD_DIM, SSD_STATE), F32),
            None, None, mkb.reshape(bp, MEM_LEN, D_MODEL), mvb.reshape(bp, MEM_LEN, D_MODEL), lam_init, p,
            tm=512, tk=256, tq=256, ssd_q=256)
        for lst, val in zip(outs[:6], (k_new, v_new, ssm_new, conv_new,
                                       mk.reshape(bp, MEM_LEN, MEM_HEADS, MEM_HEAD_DIM),
                                       mv.reshape(bp, MEM_LEN, MEM_HEADS, MEM_HEAD_DIM))):
            lst.append(val)
        past = cache_attn_k.shape[2]
        hs, k_new, v_new, ssm_new, conv_new = _layer(
            hs, state_conv[i], state_ssm[i], cache_attn_k[i].reshape(bs, past, ATT_WIDTH),
            cache_attn_v[i].reshape(bs, past, ATT_WIDTH),
            cache_mem_k[i].reshape(bs, MEM_LEN, D_MODEL).astype(BF16),
            cache_mem_v[i].reshape(bs, MEM_LEN, D_MODEL).astype(BF16), lam_init, p,
            tm=ss, tk=256, tq=ss, ssd_q=128)
        for lst, val in zip(outs[6:], (k_new, v_new, ssm_new, conv_new)):
            lst.append(val)
    return (hp, hs) + tuple(jnp.stack(o) for o in outs)
```

```python
import functools
import math

import jax
import jax.numpy as jnp
from jax import lax
from jax.experimental import pallas as pl
from jax.experimental.pallas import tpu as pltpu

D_MODEL = 1024
CHUNK = 64
SSD_WIDTH = 512
SSD_HEAD_DIM = 64
SSD_HEADS = 8
SSD_GROUPS = 2
SSD_REP = 4
SSD_STATE = 128
SSD_CONV = 4
SSD_XBC = 1024
SSD_NORM_GROUP = 256
SSD_NORM_EPS = 1e-5
ATT_WIDTH = 512
ATT_HEAD_DIM = 64
ATT_HEADS = 4
ATT_V_DIM = 128
ATT_NORM_EPS = 1e-5
ROPE_THETA = 10000.0
MEM_LEN = 256
MEM_HEADS = 4
MEM_HEAD_DIM = 256
FFN_HIDDEN = 2816
NORM_EPS = 1e-6
LANES = 128
DT_PAD = LANES
IN_COLS_PADDED = SSD_WIDTH + SSD_XBC + 3 * ATT_WIDTH + DT_PAD
VMEM_LIMIT = 56 * 1024 * 1024
NEG_BIG = -1e30
Q_SCALE = math.log2(math.e) / math.sqrt(ATT_HEAD_DIM)

F32 = jnp.float32
BF16 = jnp.bfloat16


def _const_spec(shape):
    return pl.BlockSpec(shape, lambda *_: (0,) * len(shape), pipeline_mode=pl.Buffered(1))


def _rms(x, g, eps):
    return x * lax.rsqrt(jnp.mean(x * x, axis=-1, keepdims=True) + eps) * g


def _silu(x):
    return x / (1.0 + jnp.exp(-x))


def _dot(a, b):
    return jnp.dot(a, b, preferred_element_type=F32)


def _mem_kv_kernel(mem_ref, g_ref, wk_ref, wv_ref, mk_ref, mv_ref, mkb_ref, mvb_ref):
    mn = _rms(mem_ref[...], g_ref[...], NORM_EPS).astype(BF16)
    mk = _dot(mn, wk_ref[...])
    mv = _dot(mn, wv_ref[...])
    mk_ref[...] = mk
    mv_ref[...] = mv
    mkb_ref[...] = mk.astype(BF16)
    mvb_ref[...] = mv.astype(BF16)


def _mem_kv(mem, g_mem, wk, wv):
    n = mem.shape[0]
    tm = MEM_LEN
    row = pl.BlockSpec((tm, D_MODEL), lambda i: (i, 0))
    return pl.pallas_call(
        _mem_kv_kernel,
        grid=(n // tm,),
        in_specs=[row, _const_spec((1, D_MODEL)), _const_spec((D_MODEL, D_MODEL)), _const_spec((D_MODEL, D_MODEL))],
        out_specs=[row, row, row, row],
        out_shape=[jax.ShapeDtypeStruct((n, D_MODEL), F32), jax.ShapeDtypeStruct((n, D_MODEL), F32),
                   jax.ShapeDtypeStruct((n, D_MODEL), BF16), jax.ShapeDtypeStruct((n, D_MODEL), BF16)],
        compiler_params=pltpu.CompilerParams(dimension_semantics=("arbitrary",), vmem_limit_bytes=VMEM_LIMIT),
        name="mem_kv",
    )(mem, g_mem, wk, wv)


def _in_proj_kernel(x_ref, g_ref, w_ref, cos_ref, sin_ref,
                    z_ref, xbc_ref, tail_ref, dt_ref, q_ref, k_ref, v_ref, qt_ref, kb_ref, vt_ref, *, tm, tt):
    hn = _rms(x_ref[...], g_ref[...], NORM_EPS).astype(BF16)
    acc = _dot(hn, w_ref[...])
    o = 0
    z_ref[...] = acc[:, o:o + SSD_WIDTH].astype(BF16)
    o += SSD_WIDTH
    xbc = acc[:, o:o + SSD_XBC]
    xbc_ref[...] = xbc.astype(BF16)
    tail_ref[0] = xbc[tm - 8:, :]
    o += SSD_XBC
    q = acc[:, o:o + ATT_WIDTH]
    o += ATT_WIDTH
    k = acc[:, o:o + ATT_WIDTH]
    o += ATT_WIDTH
    v = acc[:, o:o + ATT_WIDTH]
    o += ATT_WIDTH
    dt_ref[...] = acc[:, o:o + DT_PAD]
    v_ref[...] = v

    cos = cos_ref[...]
    sin = sin_ref[...]
    first_half = (lax.broadcasted_iota(jnp.int32, (tm, LANES), 1) % ATT_HEAD_DIM) < (ATT_HEAD_DIM // 2)

    def rope(t):
        swapped = jnp.where(first_half, pltpu.roll(t, LANES - ATT_HEAD_DIM // 2, 1),
                            pltpu.roll(t, ATT_HEAD_DIM // 2, 1))
        return t * cos + swapped * sin

    for j in range(ATT_WIDTH // LANES):
        sl = slice(j * LANES, (j + 1) * LANES)
        qr = rope(q[:, sl]) * Q_SCALE
        kr = rope(k[:, sl])
        q_ref[:, sl] = qr.astype(BF16)
        k_ref[:, sl] = kr
        if tt:
            kb_ref[:, sl] = kr.astype(BF16)
            for c in range(tm // tt):
                qt_ref[c, sl, :] = qr[c * tt:(c + 1) * tt, :].T.astype(BF16)
                vt_ref[c, sl, :] = v[c * tt:(c + 1) * tt, sl].T.astype(BF16)


def _in_proj(x, g, w, cos, sin, *, seq, tm, tt):
    n = x.shape[0]
    nt = n // tm
    tiles_per_seq = seq // tm
    row = lambda c: pl.BlockSpec((tm, c), lambda i: (i, 0))
    tab = pl.BlockSpec((tm, LANES), lambda i: (i % tiles_per_seq, 0))
    out_specs = [row(SSD_WIDTH), row(SSD_XBC), pl.BlockSpec((1, 8, SSD_XBC), lambda i: (i, 0, 0)), row(DT_PAD),
                 row(ATT_WIDTH), row(ATT_WIDTH), row(ATT_WIDTH)]
    out_shape = [jax.ShapeDtypeStruct((n, SSD_WIDTH), BF16), jax.ShapeDtypeStruct((n, SSD_XBC), BF16),
                 jax.ShapeDtypeStruct((nt, 8, SSD_XBC), F32), jax.ShapeDtypeStruct((n, DT_PAD), F32),
                 jax.ShapeDtypeStruct((n, ATT_WIDTH), BF16), jax.ShapeDtypeStruct((n, ATT_WIDTH), F32),
                 jax.ShapeDtypeStruct((n, ATT_WIDTH), F32)]
    if tt:
        tr = pl.BlockSpec((tm // tt, ATT_WIDTH, tt), lambda i: (i, 0, 0))
        out_specs += [tr, row(ATT_WIDTH), tr]
        out_shape += [jax.ShapeDtypeStruct((n // tt, ATT_WIDTH, tt), BF16), jax.ShapeDtypeStruct((n, ATT_WIDTH), BF16),
                      jax.ShapeDtypeStruct((n // tt, ATT_WIDTH, tt), BF16)]

    def body(*refs):
        refs = refs + (None,) * (15 - len(refs))
        _in_proj_kernel(*refs, tm=tm, tt=tt)

    outs = pl.pallas_call(
        body,
        grid=(nt,),
        in_specs=[row(D_MODEL), _const_spec((1, D_MODEL)), _const_spec((D_MODEL, IN_COLS_PADDED)), tab, tab],
        out_specs=out_specs,
        out_shape=out_shape,
        compiler_params=pltpu.CompilerParams(dimension_semantics=("arbitrary",), vmem_limit_bytes=VMEM_LIMIT),
        name="in_proj",
    )(x, g, w, cos, sin)
    return list(outs) + [None] * (10 - len(outs))


def _expand_heads(x, e2):
    hi = x.astype(BF16)
    lo = (x - hi.astype(F32)).astype(BF16)
    return _dot(jnp.concatenate([hi, lo], axis=1), e2)


def _ssd_kernel(z_ref, xbc_ref, dt_ref, cbuf_ref, h0_ref, cw_ref, cb_ref, dtb_ref, alog_ref, dsk_ref, nw_ref, e2_ref,
                y_ref, hout_ref, state, ext, *, q, valid):
    c = pl.program_id(1)

    @pl.when(c == 0)
    def _():
        state[...] = h0_ref[0]
        ext[0:8, :] = cbuf_ref[0]

    ext[8:8 + q, :] = xbc_ref[...].astype(F32)
    conv = cb_ref[...] + ext[5:5 + q, :] * cw_ref[0:1, :]
    for w in range(1, SSD_CONV):
        conv = conv + ext[5 + w:5 + w + q, :] * cw_ref[w:w + 1, :]
    ext[0:8, :] = ext[q:q + 8, :]
    u = _silu(conv)
    xs = u[:, :SSD_WIDTH]
    bm = u[:, SSD_WIDTH:SSD_WIDTH + SSD_GROUPS * SSD_STATE]
    cm = u[:, SSD_WIDTH + SSD_GROUPS * SSD_STATE:]

    dtr = dt_ref[...] + dtb_ref[...]
    dt = jnp.maximum(dtr, 0.0) + jnp.log(1.0 + jnp.exp(-jnp.abs(dtr)))
    if valid is not None:
        row = lax.broadcasted_iota(jnp.int32, (q, DT_PAD), 0) + c * q
        dt = jnp.where(row < valid, dt, 0.0)
    a = -jnp.exp(alog_ref[...])
    ad = dt * a
    ri = lax.broadcasted_iota(jnp.int32, (q, q), 0)
    ci = lax.broadcasted_iota(jnp.int32, (q, q), 1)
    tril = ri >= ci
    acum = jnp.dot(tril.astype(F32), ad, preferred_element_type=F32, precision=lax.Precision.HIGHEST)
    acum_t = acum.T
    tot = acum[q - 1:q, :]
    e2 = e2_ref[...]
    expanded = _expand_heads(jnp.concatenate([dt, dt * jnp.exp(tot - acum), jnp.exp(acum)], axis=0), e2)
    dtx = expanded[0:q]
    ddx = expanded[q:2 * q]
    eax = expanded[2 * q:3 * q]
    dsk = _expand_heads(jnp.broadcast_to(dsk_ref[...], (8, DT_PAD)), e2)[0:1]

    xdt = (xs * dtx).astype(BF16)
    xdtd = (xs * ddx).astype(BF16)
    bm_t = bm.T.astype(BF16)
    cmb = cm.astype(BF16)
    gw = SSD_REP * SSD_HEAD_DIM
    stripe = lax.broadcasted_iota(jnp.int32, (q, gw), 1) // SSD_HEAD_DIM
    ys = []
    for g in range(SSD_GROUPS):
        cm_g = cmb[:, g * SSD_STATE:(g + 1) * SSD_STATE]
        bt_g = bm_t[g * SSD_STATE:(g + 1) * SSD_STATE, :]
        cbm = _dot(cm_g, bt_g)
        ms = []
        for r in range(SSD_REP):
            h = g * SSD_REP + r
            diff = acum[:, h:h + 1] - acum_t[h:h + 1, :]
            ms.append((cbm * jnp.exp(jnp.where(tril, diff, -jnp.inf))).astype(BF16))
        ydf = _dot(jnp.concatenate(ms, axis=0), xdt[:, g * gw:(g + 1) * gw])
        yd = ydf[0:q]
        for r in range(1, SSD_REP):
            yd = jnp.where(stripe == r, ydf[r * q:(r + 1) * q], yd)
        st = state[g]
        y_off = _dot(cm_g, st.astype(BF16)) * eax[:, g * gw:(g + 1) * gw]
        state[g] = st * eax[q - 1:q, g * gw:(g + 1) * gw] + _dot(bt_g, xdtd[:, g * gw:(g + 1) * gw])
        ys.append(yd + y_off)
    y = jnp.concatenate(ys, axis=1) + dsk * xs
    y = y * _silu(z_ref[...].astype(F32))
    outs = []
    for g in range(SSD_WIDTH // SSD_NORM_GROUP):
        yg = y[:, g * SSD_NORM_GROUP:(g + 1) * SSD_NORM_GROUP]
        outs.append(yg * lax.rsqrt(jnp.mean(yg * yg, axis=-1, keepdims=True) + SSD_NORM_EPS))
    y_ref[...] = (jnp.concatenate(outs, axis=1) * nw_ref[...]).astype(BF16)

    @pl.when(c == pl.num_programs(1) - 1)
    def _():
        hout_ref[0] = state[...]


def _ssd(z, xbc, dt, cbuf, h0, cw, cb, dtb, alog, dsk, nw, e2, *, nb, seq, q, valid):
    nc = seq // q
    row = lambda c: pl.BlockSpec((q, c), lambda b, i: (b * nc + i, 0))
    gw = SSD_REP * SSD_HEAD_DIM
    st_spec = pl.BlockSpec((1, SSD_GROUPS, SSD_STATE, gw), lambda b, i: (b, 0, 0, 0))
    return pl.pallas_call(
        functools.partial(_ssd_kernel, q=q, valid=valid),
        grid=(nb, nc),
        in_specs=[row(SSD_WIDTH), row(SSD_XBC), row(DT_PAD),
                  pl.BlockSpec((1, 8, SSD_XBC), lambda b, i: (b, 0, 0)), st_spec,
                  _const_spec((SSD_CONV, SSD_XBC)), _const_spec((1, SSD_XBC)), _const_spec((1, DT_PAD)),
                  _const_spec((1, DT_PAD)), _const_spec((1, DT_PAD)), _const_spec((1, SSD_WIDTH)),
                  _const_spec((2 * DT_PAD, SSD_WIDTH))],
        out_specs=[row(SSD_WIDTH), st_spec],
        out_shape=[jax.ShapeDtypeStruct((nb * seq, SSD_WIDTH), BF16),
                   jax.ShapeDtypeStruct((nb, SSD_GROUPS, SSD_STATE, gw), F32)],
        scratch_shapes=[pltpu.VMEM((SSD_GROUPS, SSD_STATE, gw), F32), pltpu.VMEM((q + 8, SSD_XBC), F32)],
        compiler_params=pltpu.CompilerParams(dimension_semantics=("arbitrary", "arbitrary"),
                                             vmem_limit_bytes=VMEM_LIMIT),
        name="ssd",
    )(z, xbc, dt, cbuf, h0, cw, cb, dtb, alog, dsk, nw, e2)


def _attn_tile_counts(i, *, tq, tk, past, kv_len, minimum=min):
    q_lo = past + i * tq
    q_hi = q_lo + tq - 1
    lim_lo = minimum((q_lo // CHUNK + 1) * CHUNK, kv_len)
    lim_hi = minimum((q_hi // CHUNK + 1) * CHUNK, kv_len)
    return lim_lo // tk, (lim_hi + tk - 1) // tk


def _diff_attn_kernel(qt_ref, k_ref, vt_ref, lam_ref, sw_ref, o_ref,
                      q2t_ref, s_ref, p_ref, m_ref, l_ref, alpha_ref, acc_ref, *, tq, tk, past, kv_len, lam_init):
    i = pl.program_id(2)
    w = 2 * tq
    qt = qt_ref[0]
    row = lax.broadcasted_iota(jnp.int32, (ATT_V_DIM, tq), 0)
    zero = jnp.zeros_like(qt)
    q2t_ref[:, 0:tq] = jnp.where(row < ATT_HEAD_DIM, qt, zero)
    q2t_ref[:, tq:w] = jnp.where(row >= ATT_HEAD_DIM, qt, zero)

    q_lo = past + i * tq
    n_full, _ = _attn_tile_counts(i, tq=tq, tk=tk, past=past, kv_len=kv_len, minimum=jnp.minimum)

    def scores(j):
        kt = k_ref[pl.ds(pl.multiple_of(j * tk, tk), tk), :]
        return _dot(kt, q2t_ref[...])

    qpos = q_lo + lax.broadcasted_iota(jnp.int32, (tk, w), 1) % tq
    kpos = n_full * tk + lax.broadcasted_iota(jnp.int32, (tk, w), 0)
    s_ref[...] = jnp.where((kpos // CHUNK <= qpos // CHUNK) & (kpos < kv_len), scores(n_full), NEG_BIG)
    m_ref[...] = jnp.full_like(m_ref, NEG_BIG)
    l_ref[...] = jnp.zeros_like(l_ref)
    acc_ref[...] = jnp.zeros_like(acc_ref)
    p_ref[...] = jnp.zeros_like(p_ref)
    alpha_ref[...] = jnp.ones_like(alpha_ref)

    def visited_tile(t):
        return jnp.where(t == 0, n_full, jnp.maximum(t - 1, 0))

    def accumulate(t):
        acc_ref[...] = alpha_ref[...] * acc_ref[...] + _dot(vt_ref[visited_tile(t)], p_ref[...])

    def trip(t, carry):
        accumulate(t - 1)
        for cb in range(w // LANES):
            sl = slice(cb * LANES, (cb + 1) * LANES)
            s = s_ref[:, sl]
            m_old = m_ref[:, sl]
            m_new = jnp.maximum(m_old, jnp.max(s, axis=0, keepdims=True))
            alpha = jnp.exp2(m_old - m_new)
            p = jnp.exp2(s - m_new)
            l_ref[:, sl] = alpha * l_ref[:, sl] + jnp.sum(p, axis=0, keepdims=True)
            m_ref[:, sl] = m_new
            alpha_ref[:, sl] = alpha
            p_ref[:, sl] = p.astype(BF16)
        s_ref[...] = scores(t)
        return carry

    lax.fori_loop(0, n_full + 1, trip, 0)
    accumulate(n_full)

    lam_v = lam_ref[...]
    lam = (jnp.exp(jnp.sum(lam_v[0:1] * lam_v[1:2], axis=-1, keepdims=True))
           - jnp.exp(jnp.sum(lam_v[2:3] * lam_v[3:4], axis=-1, keepdims=True)) + lam_init)
    o = acc_ref[...] / l_ref[...]
    o = o[:, :tq] - lam * o[:, tq:]
    o = o * lax.rsqrt(jnp.mean(o * o, axis=0, keepdims=True) + ATT_NORM_EPS) * (sw_ref[...] * (1.0 - lam_init))
    o_ref[...] = o.T.astype(BF16)


def _diff_attn(qt, k, vt, lam_vecs, subln_col, *, nb, tq, tk, past, kv_len, lam_init):
    nq = qt.shape[0] // nb
    nkt = vt.shape[0] // nb
    for i in range(nq):
        n_full, n_end = _attn_tile_counts(i, tq=tq, tk=tk, past=past, kv_len=kv_len)
        assert n_end - n_full == 1 and n_end <= nkt, (i, n_full, n_end)
    w = 2 * tq
    return pl.pallas_call(
        functools.partial(_diff_attn_kernel, tq=tq, tk=tk, past=past, kv_len=kv_len, lam_init=lam_init),
        grid=(nb, ATT_HEADS, nq),
        in_specs=[pl.BlockSpec((1, ATT_V_DIM, tq), lambda b, h, i: (b * nq + i, h, 0)),
                  pl.BlockSpec((nkt * tk, ATT_V_DIM), lambda b, h, i: (b, h)),
                  pl.BlockSpec((nkt, ATT_V_DIM, tk), lambda b, h, i: (b, h, 0)),
                  _const_spec((4, ATT_HEAD_DIM)), _const_spec((ATT_V_DIM, 1))],
        out_specs=pl.BlockSpec((tq, ATT_V_DIM), lambda b, h, i: (b * nq + i, h)),
        out_shape=jax.ShapeDtypeStruct((nb * nq * tq, ATT_WIDTH), BF16),
        scratch_shapes=[pltpu.VMEM((ATT_V_DIM, w), BF16), pltpu.VMEM((tk, w), F32), pltpu.VMEM((tk, w), BF16),
                        pltpu.VMEM((1, w), F32), pltpu.VMEM((1, w), F32), pltpu.VMEM((1, w), F32),
                        pltpu.VMEM((ATT_V_DIM, w), F32)],
        compiler_params=pltpu.CompilerParams(dimension_semantics=("arbitrary", "arbitrary", "arbitrary"),
                                             vmem_limit_bytes=VMEM_LIMIT),
        name="diff_attn",
    )(qt, k, vt, lam_vecs, subln_col)


def _post_mix_kernel(x_ref, ys_ref, ya_ref, mk_ref, mv_ref, wo_ref, wq_ref, wox_ref,
                     g1_ref, g2_ref, g3_ref, h_ref):
    mix = _dot(ys_ref[...], wo_ref[0:SSD_WIDTH, :]) + _dot(ya_ref[...], wo_ref[SSD_WIDTH:, :])
    h = x_ref[...] + _rms(mix, g1_ref[...], NORM_EPS)
    qn = _rms(h, g2_ref[...], NORM_EPS).astype(BF16)
    qx = (_dot(qn, wq_ref[...]) * (1.0 / math.sqrt(MEM_HEAD_DIM))).astype(BF16)
    mk = mk_ref[0]
    mv = mv_ref[0]
    oxs = []
    for hd in range(MEM_HEADS):
        sl = slice(hd * MEM_HEAD_DIM, (hd + 1) * MEM_HEAD_DIM)
        s = lax.dot_general(qx[:, sl], mk[:, sl], (((1,), (1,)), ((), ())), preferred_element_type=F32)
        p = jnp.exp(s - jnp.max(s, axis=-1, keepdims=True))
        ox = _dot(p.astype(BF16), mv[:, sl]) / jnp.sum(p, axis=-1, keepdims=True)
        oxs.append(ox.astype(BF16))
    o2 = _dot(jnp.concatenate(oxs, axis=1), wox_ref[...])
    h_ref[...] = h + _rms(o2, g3_ref[...], NORM_EPS)


def _post_mix(x, ys, ya, mk, mv, w_out, wq, wox, g1, g2, g3, *, seq, tm):
    n = x.shape[0]
    tiles_per_seq = seq // tm
    row = lambda c: pl.BlockSpec((tm, c), lambda i: (i, 0))
    mem = pl.BlockSpec((1, MEM_LEN, D_MODEL), lambda i: (i // tiles_per_seq, 0, 0))
    wspec = _const_spec((D_MODEL, D_MODEL))
    gspec = _const_spec((1, D_MODEL))
    return pl.pallas_call(
        _post_mix_kernel,
        grid=(n // tm,),
        in_specs=[row(D_MODEL), row(SSD_WIDTH), row(ATT_WIDTH), mem, mem, wspec, wspec, wspec, gspec, gspec, gspec],
        out_specs=row(D_MODEL),
        out_shape=jax.ShapeDtypeStruct((n, D_MODEL), F32),
        compiler_params=pltpu.CompilerParams(dimension_semantics=("arbitrary",), vmem_limit_bytes=VMEM_LIMIT),
        name="post_mix",
    )(x, ys, ya, mk, mv, w_out, wq, wox, g1, g2, g3)


def _ffn_kernel(h_ref, wg_ref, wu_ref, wd_ref, g1_ref, g2_ref, o_ref):
    h = h_ref[...]
    hn = _rms(h, g1_ref[...], NORM_EPS).astype(BF16)
    act = (_silu(_dot(hn, wg_ref[...])) * _dot(hn, wu_ref[...])).astype(BF16)
    f = _dot(act, wd_ref[...])
    o_ref[...] = h + _rms(f, g2_ref[...], NORM_EPS)


def _ffn(h, wg, wu, wd, g1, g2, *, tm):
    n = h.shape[0]
    row = pl.BlockSpec((tm, D_MODEL), lambda i: (i, 0))
    return pl.pallas_call(
        _ffn_kernel,
        grid=(n // tm,),
        in_specs=[row, _const_spec((D_MODEL, FFN_HIDDEN)), _const_spec((D_MODEL, FFN_HIDDEN)),
                  _const_spec((FFN_HIDDEN, D_MODEL)), _const_spec((1, D_MODEL)), _const_spec((1, D_MODEL))],
        out_specs=row,
        out_shape=jax.ShapeDtypeStruct((n, D_MODEL), F32),
        compiler_params=pltpu.CompilerParams(dimension_semantics=("arbitrary",), vmem_limit_bytes=VMEM_LIMIT),
        name="ffn",
    )(h, wg, wu, wd, g1, g2)


def _rope_tables(past, seq):
    half = ATT_HEAD_DIM // 2
    inv = jnp.power(ROPE_THETA, -jnp.arange(0, ATT_HEAD_DIM, 2, dtype=F32) / ATT_HEAD_DIM)
    pos = (past + jnp.arange(seq, dtype=jnp.int32)).astype(F32)
    ang = pos[:, None] * inv[None, :]
    cos, sin = jnp.cos(ang), jnp.sin(ang)
    reps = LANES // ATT_HEAD_DIM
    assert half * 2 == ATT_HEAD_DIM
    return jnp.tile(jnp.concatenate([cos, cos], axis=-1), (1, reps)), jnp.tile(jnp.concatenate([-sin, sin], axis=-1), (1, reps))


def _state_to_kernel_layout(s):
    b = s.shape[0]
    s = s.reshape(b, SSD_GROUPS, SSD_REP, SSD_HEAD_DIM, SSD_STATE)
    return s.transpose(0, 1, 4, 2, 3).reshape(b, SSD_GROUPS, SSD_STATE, SSD_REP * SSD_HEAD_DIM)


def _state_from_kernel_layout(s):
    b = s.shape[0]
    s = s.reshape(b, SSD_GROUPS, SSD_STATE, SSD_REP, SSD_HEAD_DIM)
    return s.transpose(0, 1, 3, 4, 2).reshape(b, SSD_HEADS, SSD_HEAD_DIM, SSD_STATE)


def _layer(x, conv_buf, ssm0, k_past, v_past, mem_kb, mem_vb, lam_init, p, *, tm, tk, tq, ssd_q):
    nb, seq, _ = x.shape
    n = nb * seq
    past = 0 if k_past is None else k_past.shape[1]
    xf = x.reshape(n, D_MODEL)
    cos, sin = _rope_tables(past, seq)
    no_history = k_past is None
    assert tq == tk or not no_history
    z, xbc, tail, dt, q, k, v, qt, kb, vt = _in_proj(xf, p["g_pre_mix"], p["w_in"], cos, sin, seq=seq, tm=tm,
                                                      tt=tk if no_history else 0)

    seq_pad = -(-seq // ssd_q) * ssd_q
    if seq_pad != seq:
        pad = lambda a: jnp.pad(a.reshape(nb, seq, -1), ((0, 0), (0, seq_pad - seq), (0, 0))).reshape(nb * seq_pad, -1)
        z_s, xbc_s, dt_s = pad(z), pad(xbc), pad(dt)
    else:
        z_s, xbc_s, dt_s = z, xbc, dt
    cbuf = jnp.pad(conv_buf.astype(F32), ((0, 0), (8 - (SSD_CONV - 1), 0), (0, 0)))
    y_ssd, h_new = _ssd(z_s, xbc_s, dt_s, cbuf, _state_to_kernel_layout(ssm0.astype(F32)),
                        p["conv_w"], p["conv_b"], p["dt_bias"], p["a_log"], p["d_skip"], p["ssm_norm_w"], p["e2"],
                        nb=nb, seq=seq_pad, q=ssd_q, valid=None if seq_pad == seq else seq)
    if seq_pad != seq:
        y_ssd = y_ssd.reshape(nb, seq_pad, SSD_WIDTH)[:, :seq].reshape(n, SSD_WIDTH)
    ssm_new = _state_from_kernel_layout(h_new)
    ext_tail = tail.reshape(nb, seq // tm, 8, SSD_XBC)[:, -1]
    if seq >= SSD_CONV - 1:
        conv_new = ext_tail[:, 8 - (SSD_CONV - 1):]
    else:
        conv_new = jnp.concatenate([conv_buf.astype(F32), ext_tail[:, 8 - seq:]], axis=1)[:, -(SSD_CONV - 1):]

    if no_history:
        kv_len = seq
        y_att = _diff_attn(qt, kb, vt, p["lam_vecs"], p["subln_col"], nb=nb, tq=tq, tk=tk,
                           past=past, kv_len=kv_len, lam_init=lam_init)
    else:
        kv_len = past + seq
        kv_pad = -(-kv_len // tk) * tk
        k_all = jnp.concatenate([k_past.astype(BF16), k.reshape(nb, seq, ATT_WIDTH).astype(BF16)], axis=1)
        k_all = jnp.pad(k_all, ((0, 0), (0, kv_pad - kv_len), (0, 0))).reshape(-1, ATT_WIDTH)
        v_all = jnp.concatenate([v_past.astype(BF16), v.reshape(nb, seq, ATT_WIDTH).astype(BF16)], axis=1)
        v_all = jnp.pad(v_all, ((0, 0), (0, kv_pad - kv_len), (0, 0)))
        vt_all = v_all.reshape(nb, kv_pad // tk, tk, ATT_WIDTH).transpose(0, 1, 3, 2).reshape(-1, ATT_WIDTH, tk)
        assert seq <= tq
        q_pad = jnp.pad(q.reshape(nb, seq, ATT_WIDTH), ((0, 0), (0, tq - seq), (0, 0)))
        y_att = _diff_attn(q_pad.transpose(0, 2, 1), k_all, vt_all, p["lam_vecs"], p["subln_col"], nb=nb, tq=tq, tk=tk,
                           past=past, kv_len=kv_len, lam_init=lam_init)
        y_att = y_att.reshape(nb, tq, ATT_WIDTH)[:, :seq].reshape(n, ATT_WIDTH)

    h = _post_mix(xf, y_ssd, y_att, mem_kb, mem_vb, p["w_out"], p["wq_x"], p["wo_x"],
                  p["g_post_mix"], p["g_pre_x"], p["g_post_x"], seq=seq, tm=tm)
    out = _ffn(h, p["w_gate"], p["w_up"], p["w_down"], p["g_pre_ffn"], p["g_post_ffn"], tm=tm)
    return (out.reshape(nb, seq, D_MODEL), k.reshape(nb, seq, ATT_HEADS, 2, ATT_HEAD_DIM),
            v.reshape(nb, seq, ATT_HEADS, ATT_V_DIM), ssm_new, conv_new)


def _prep_params(i, w_in, conv_w, conv_b, dt_bias, a_log, d_skip, ssm_norm_w, lam_q1, lam_k1, lam_q2, lam_k2, subln_w,
                 w_out, wq_x, wo_x, g_pre_mix, g_post_mix, g_pre_x, g_post_x, g_pre_ffn, g_post_ffn,
                 w_gate, w_up, w_down):
    w = w_in[i]
    s0 = SSD_WIDTH + SSD_XBC
    s1 = s0 + SSD_HEADS
    w_r = jnp.concatenate([w[:, :s0], w[:, s1:], w[:, s0:s1], jnp.zeros((D_MODEL, DT_PAD - SSD_HEADS), w.dtype)], axis=1)
    head_pad = lambda a: jnp.pad(a[i].astype(F32), (0, DT_PAD - SSD_HEADS)).reshape(1, DT_PAD)
    e = (jnp.arange(DT_PAD)[:, None] == (jnp.arange(SSD_WIDTH)[None, :] // SSD_HEAD_DIM)).astype(BF16)
    row = lambda a: a[i].astype(F32).reshape(1, -1)
    return {
        "w_in": w_r.astype(BF16), "conv_w": conv_w[i].astype(F32), "conv_b": row(conv_b),
        "dt_bias": head_pad(dt_bias), "a_log": head_pad(a_log), "d_skip": head_pad(d_skip),
        "ssm_norm_w": row(ssm_norm_w), "e2": jnp.concatenate([e, e], axis=0),
        "lam_vecs": jnp.stack([lam_q1[i], lam_k1[i], lam_q2[i], lam_k2[i]]).astype(F32), "subln_col": subln_w[i].astype(F32).reshape(ATT_V_DIM, 1),
        "w_out": w_out[i].astype(BF16), "wq_x": wq_x[i].astype(BF16), "wo_x": wo_x[i].astype(BF16),
        "g_pre_mix": row(g_pre_mix), "g_post_mix": row(g_post_mix), "g_pre_x": row(g_pre_x), "g_post_x": row(g_post_x),
        "g_pre_ffn": row(g_pre_ffn), "g_post_ffn": row(g_post_ffn),
        "w_gate": w_gate[i].astype(BF16), "w_up": w_up[i].astype(BF16), "w_down": w_down[i].astype(BF16),
    }


def kernel(x_prompt, x_sample, cache_attn_k, cache_attn_v, state_ssm, state_conv, cache_mem_k, cache_mem_v, mem_prompt, w_in, conv_w, conv_b, dt_bias, a_log, d_skip, ssm_norm_w, lam_q1, lam_k1, lam_q2, lam_k2, subln_w, w_out, g_mem, wq_x, wk_x, wv_x, wo_x, g_pre_mix, g_post_mix, g_pre_x, g_post_x, g_pre_ffn, g_post_ffn, w_gate, w_up, w_down):
    depth = w_in.shape[0]
    bp, sp, _ = x_prompt.shape
    bs, ss, _ = x_sample.shape
    hp, hs = x_prompt, x_sample
    outs = [[] for _ in range(10)]
    for i in range(depth):
        lam_init = 0.8 - 0.6 * math.exp(-0.3 * i)
        p = _prep_params(i, w_in, conv_w, conv_b, dt_bias, a_log, d_skip, ssm_norm_w, lam_q1, lam_k1, lam_q2, lam_k2,
                         subln_w, w_out, wq_x, wo_x, g_pre_mix, g_post_mix, g_pre_x, g_post_x, g_pre_ffn, g_post_ffn,
                         w_gate, w_up, w_down)
        mk, mv, mkb, mvb = _mem_kv(mem_prompt.reshape(bp * MEM_LEN, D_MODEL), g_mem[i].reshape(1, D_MODEL),
                                   wk_x[i].astype(BF16), wv_x[i].astype(BF16))
        hp, k_new, v_new, ssm_new, conv_new = _layer(
            hp, jnp.zeros((bp, SSD_CONV - 1, SSD_XBC), F32), jnp.zeros((bp, SSD_HEADS, SSD_HEAD_DIM, SSD_STATE), F32),
            None, None, mkb.reshape(bp, MEM_LEN, D_MODEL), mvb.reshape(bp, MEM_LEN, D_MODEL), lam_init, p,
            tm=512, tk=256, tq=256, ssd_q=256)
        for lst, val in zip(outs[:6], (k_new, v_new, ssm_new, conv_new,
                                       mk.reshape(bp, MEM_LEN, MEM_HEADS, MEM_HEAD_DIM),
                                       mv.reshape(bp, MEM_LEN, MEM_HEADS, MEM_HEAD_DIM))):
            lst.append(val)
        past = cache_attn_k.shape[2]
        hs, k_new, v_new, ssm_new, conv_new = _layer(
            hs, state_conv[i], state_ssm[i], cache_attn_k[i].reshape(bs, past, ATT_WIDTH),
            cache_attn_v[i].reshape(bs, past, ATT_WIDTH),
            cache_mem_k[i].reshape(bs, MEM_LEN, D_MODEL).astype(BF16),
            cache_mem_v[i].reshape(bs, MEM_LEN, D_MODEL).astype(BF16), lam_init, p,
            tm=ss, tk=256, tq=128, ssd_q=128)
        for lst, val in zip(outs[6:], (k_new, v_new, ssm_new, conv_new)):
            lst.append(val)
    return (hp, hs) + tuple(jnp.stack(o) for o in outs)
```

```python
import functools
import math

import jax
import jax.numpy as jnp
from jax import lax
from jax.experimental import pallas as pl
from jax.experimental.pallas import tpu as pltpu

D_MODEL = 1024
CHUNK = 64
SSD_WIDTH = 512
SSD_HEAD_DIM = 64
SSD_HEADS = 8
SSD_GROUPS = 2
SSD_REP = 4
SSD_STATE = 128
SSD_CONV = 4
SSD_XBC = 1024
SSD_NORM_GROUP = 256
SSD_NORM_EPS = 1e-5
ATT_WIDTH = 512
ATT_HEAD_DIM = 64
ATT_HEADS = 4
ATT_V_DIM = 128
ATT_NORM_EPS = 1e-5
ROPE_THETA = 10000.0
MEM_LEN = 256
MEM_HEADS = 4
MEM_HEAD_DIM = 256
FFN_HIDDEN = 2816
NORM_EPS = 1e-6
LANES = 128
DT_PAD = LANES
IN_COLS_PADDED = SSD_WIDTH + SSD_XBC + 3 * ATT_WIDTH + DT_PAD
VMEM_LIMIT = 56 * 1024 * 1024
NEG_BIG = -1e30
Q_SCALE = math.log2(math.e) / math.sqrt(ATT_HEAD_DIM)

F32 = jnp.float32
BF16 = jnp.bfloat16


def _const_spec(shape):
    return pl.BlockSpec(shape, lambda *_: (0,) * len(shape), pipeline_mode=pl.Buffered(1))


def _rms(x, g, eps):
    return x * lax.rsqrt(jnp.mean(x * x, axis=-1, keepdims=True) + eps) * g


def _silu(x):
    return x / (1.0 + jnp.exp(-x))


def _dot(a, b):
    return jnp.dot(a, b, preferred_element_type=F32)


def _mem_kv_kernel(mem_ref, g_ref, wk_ref, wv_ref, mk_ref, mv_ref, mkb_ref, mvb_ref):
    mn = _rms(mem_ref[...], g_ref[...], NORM_EPS).astype(BF16)
    mk = _dot(mn, wk_ref[...])
    mv = _dot(mn, wv_ref[...])
    mk_ref[...] = mk
    mv_ref[...] = mv
    mkb_ref[...] = mk.astype(BF16)
    mvb_ref[...] = mv.astype(BF16)


def _mem_kv(mem, g_mem, wk, wv):
    n = mem.shape[0]
    tm = MEM_LEN
    row = pl.BlockSpec((tm, D_MODEL), lambda i: (i, 0))
    return pl.pallas_call(
        _mem_kv_kernel,
        grid=(n // tm,),
        in_specs=[row, _const_spec((1, D_MODEL)), _const_spec((D_MODEL, D_MODEL)), _const_spec((D_MODEL, D_MODEL))],
        out_specs=[row, row, row, row],
        out_shape=[jax.ShapeDtypeStruct((n, D_MODEL), F32), jax.ShapeDtypeStruct((n, D_MODEL), F32),
                   jax.ShapeDtypeStruct((n, D_MODEL), BF16), jax.ShapeDtypeStruct((n, D_MODEL), BF16)],
        compiler_params=pltpu.CompilerParams(dimension_semantics=("arbitrary",), vmem_limit_bytes=VMEM_LIMIT),
        name="mem_kv",
    )(mem, g_mem, wk, wv)


def _in_proj_kernel(x_ref, g_ref, w_ref, cos_ref, sin_ref,
                    z_ref, xbc_ref, tail_ref, dt_ref, q_ref, k_ref, v_ref, qt_ref, kb_ref, vt_ref, *, tm, tt):
    hn = _rms(x_ref[...], g_ref[...], NORM_EPS).astype(BF16)
    acc = _dot(hn, w_ref[...])
    o = 0
    z_ref[...] = acc[:, o:o + SSD_WIDTH].astype(BF16)
    o += SSD_WIDTH
    xbc = acc[:, o:o + SSD_XBC]
    xbc_ref[...] = xbc.astype(BF16)
    tail_ref[0] = xbc[tm - 8:, :]
    o += SSD_XBC
    q = acc[:, o:o + ATT_WIDTH]
    o += ATT_WIDTH
    k = acc[:, o:o + ATT_WIDTH]
    o += ATT_WIDTH
    v = acc[:, o:o + ATT_WIDTH]
    o += ATT_WIDTH
    dt_ref[...] = acc[:, o:o + DT_PAD]
    v_ref[...] = v

    cos = cos_ref[...]
    sin = sin_ref[...]
    first_half = (lax.broadcasted_iota(jnp.int32, (tm, LANES), 1) % ATT_HEAD_DIM) < (ATT_HEAD_DIM // 2)

    def rope(t):
        swapped = jnp.where(first_half, pltpu.roll(t, LANES - ATT_HEAD_DIM // 2, 1),
                            pltpu.roll(t, ATT_HEAD_DIM // 2, 1))
        return t * cos + swapped * sin

    for j in range(ATT_WIDTH // LANES):
        sl = slice(j * LANES, (j + 1) * LANES)
        qr = rope(q[:, sl]) * Q_SCALE
        kr = rope(k[:, sl])
        q_ref[:, sl] = qr.astype(BF16)
        k_ref[:, sl] = kr
        if tt:
            kb_ref[:, sl] = kr.astype(BF16)
            for c in range(tm // tt):
                qt_ref[c, sl, :] = qr[c * tt:(c + 1) * tt, :].T.astype(BF16)
                vt_ref[c, sl, :] = v[c * tt:(c + 1) * tt, sl].T.astype(BF16)


def _in_proj(x, g, w, cos, sin, *, seq, tm, tt):
    n = x.shape[0]
    nt = n // tm
    tiles_per_seq = seq // tm
    row = lambda c: pl.BlockSpec((tm, c), lambda i: (i, 0))
    tab = pl.BlockSpec((tm, LANES), lambda i: (i % tiles_per_seq, 0))
    out_specs = [row(SSD_WIDTH), row(SSD_XBC), pl.BlockSpec((1, 8, SSD_XBC), lambda i: (i, 0, 0)), row(DT_PAD),
                 row(ATT_WIDTH), row(ATT_WIDTH), row(ATT_WIDTH)]
    out_shape = [jax.ShapeDtypeStruct((n, SSD_WIDTH), BF16), jax.ShapeDtypeStruct((n, SSD_XBC), BF16),
                 jax.ShapeDtypeStruct((nt, 8, SSD_XBC), F32), jax.ShapeDtypeStruct((n, DT_PAD), F32),
                 jax.ShapeDtypeStruct((n, ATT_WIDTH), BF16), jax.ShapeDtypeStruct((n, ATT_WIDTH), F32),
                 jax.ShapeDtypeStruct((n, ATT_WIDTH), F32)]
    if tt:
        tr = pl.BlockSpec((tm // tt, ATT_WIDTH, tt), lambda i: (i, 0, 0))
        out_specs += [tr, row(ATT_WIDTH), tr]
        out_shape += [jax.ShapeDtypeStruct((n // tt, ATT_WIDTH, tt), BF16), jax.ShapeDtypeStruct((n, ATT_WIDTH), BF16),
                      jax.ShapeDtypeStruct((n // tt, ATT_WIDTH, tt), BF16)]

    def body(*refs):
        refs = refs + (None,) * (15 - len(refs))
        _in_proj_kernel(*refs, tm=tm, tt=tt)

    outs = pl.pallas_call(
        body,
        grid=(nt,),
        in_specs=[row(D_MODEL), _const_spec((1, D_MODEL)), _const_spec((D_MODEL, IN_COLS_PADDED)), tab, tab],
        out_specs=out_specs,
        out_shape=out_shape,
        compiler_params=pltpu.CompilerParams(dimension_semantics=("arbitrary",), vmem_limit_bytes=VMEM_LIMIT),
        name="in_proj",
    )(x, g, w, cos, sin)
    return list(outs) + [None] * (10 - len(outs))


def _expand_heads(x, e2):
    hi = x.astype(BF16)
    lo = (x - hi.astype(F32)).astype(BF16)
    return _dot(jnp.concatenate([hi, lo], axis=1), e2)


def _ssd_kernel(z_ref, xbc_ref, dt_ref, cbuf_ref, h0_ref, cw_ref, cb_ref, dtb_ref, alog_ref, dsk_ref, nw_ref, e2_ref,
                y_ref, hout_ref, state, ext, *, q, valid):
    c = pl.program_id(1)

    @pl.when(c == 0)
    def _():
        state[...] = h0_ref[0]
        ext[0:8, :] = cbuf_ref[0]

    ext[8:8 + q, :] = xbc_ref[...].astype(F32)
    conv = cb_ref[...] + ext[5:5 + q, :] * cw_ref[0:1, :]
    for w in range(1, SSD_CONV):
        conv = conv + ext[5 + w:5 + w + q, :] * cw_ref[w:w + 1, :]
    ext[0:8, :] = ext[q:q + 8, :]
    u = _silu(conv)
    xs = u[:, :SSD_WIDTH]
    bm = u[:, SSD_WIDTH:SSD_WIDTH + SSD_GROUPS * SSD_STATE]
    cm = u[:, SSD_WIDTH + SSD_GROUPS * SSD_STATE:]

    dtr = dt_ref[...] + dtb_ref[...]
    dt = jnp.maximum(dtr, 0.0) + jnp.log(1.0 + jnp.exp(-jnp.abs(dtr)))
    if valid is not None:
        row = lax.broadcasted_iota(jnp.int32, (q, DT_PAD), 0) + c * q
        dt = jnp.where(row < valid, dt, 0.0)
    a = -jnp.exp(alog_ref[...])
    ad = dt * a
    ri = lax.broadcasted_iota(jnp.int32, (q, q), 0)
    ci = lax.broadcasted_iota(jnp.int32, (q, q), 1)
    tril = ri >= ci
    acum = jnp.dot(tril.astype(F32), ad, preferred_element_type=F32, precision=lax.Precision.HIGHEST)
    acum_t = acum.T
    tot = acum[q - 1:q, :]
    e2 = e2_ref[...]
    expanded = _expand_heads(jnp.concatenate([dt, dt * jnp.exp(tot - acum), jnp.exp(acum)], axis=0), e2)
    dtx = expanded[0:q]
    ddx = expanded[q:2 * q]
    eax = expanded[2 * q:3 * q]
    dsk = _expand_heads(jnp.broadcast_to(dsk_ref[...], (8, DT_PAD)), e2)[0:1]

    xdt = (xs * dtx).astype(BF16)
    xdtd = (xs * ddx).astype(BF16)
    bm_t = bm.T.astype(BF16)
    cmb = cm.astype(BF16)
    gw = SSD_REP * SSD_HEAD_DIM
    stripe = lax.broadcasted_iota(jnp.int32, (q, gw), 1) // SSD_HEAD_DIM
    ys = []
    for g in range(SSD_GROUPS):
        cm_g = cmb[:, g * SSD_STATE:(g + 1) * SSD_STATE]
        bt_g = bm_t[g * SSD_STATE:(g + 1) * SSD_STATE, :]
        cbm = _dot(cm_g, bt_g)
        ms = []
        for r in range(SSD_REP):
            h = g * SSD_REP + r
            diff = acum[:, h:h + 1] - acum_t[h:h + 1, :]
            ms.append((cbm * jnp.exp(jnp.where(tril, diff, -jnp.inf))).astype(BF16))
        ydf = _dot(jnp.concatenate(ms, axis=0), xdt[:, g * gw:(g + 1) * gw])
        yd = ydf[0:q]
        for r in range(1, SSD_REP):
            yd = jnp.where(stripe == r, ydf[r * q:(r + 1) * q], yd)
        st = state[g]
        y_off = _dot(cm_g, st.astype(BF16)) * eax[:, g * gw:(g + 1) * gw]
        state[g] = st * eax[q - 1:q, g * gw:(g + 1) * gw] + _dot(bt_g, xdtd[:, g * gw:(g + 1) * gw])
        ys.append(yd + y_off)
    y = jnp.concatenate(ys, axis=1) + dsk * xs
    y = y * _silu(z_ref[...].astype(F32))
    outs = []
    for g in range(SSD_WIDTH // SSD_NORM_GROUP):
        yg = y[:, g * SSD_NORM_GROUP:(g + 1) * SSD_NORM_GROUP]
        outs.append(yg * lax.rsqrt(jnp.mean(yg * yg, axis=-1, keepdims=True) + SSD_NORM_EPS))
    y_ref[...] = (jnp.concatenate(outs, axis=1) * nw_ref[...]).astype(BF16)

    @pl.when(c == pl.num_programs(1) - 1)
    def _():
        hout_ref[0] = state[...]


def _ssd(z, xbc, dt, cbuf, h0, cw, cb, dtb, alog, dsk, nw, e2, *, nb, seq, q, valid):
    nc = seq // q
    row = lambda c: pl.BlockSpec((q, c), lambda b, i: (b * nc + i, 0))
    gw = SSD_REP * SSD_HEAD_DIM
    st_spec = pl.BlockSpec((1, SSD_GROUPS, SSD_STATE, gw), lambda b, i: (b, 0, 0, 0))
    return pl.pallas_call(
        functools.partial(_ssd_kernel, q=q, valid=valid),
        grid=(nb, nc),
        in_specs=[row(SSD_WIDTH), row(SSD_XBC), row(DT_PAD),
                  pl.BlockSpec((1, 8, SSD_XBC), lambda b, i: (b, 0, 0)), st_spec,
                  _const_spec((SSD_CONV, SSD_XBC)), _const_spec((1, SSD_XBC)), _const_spec((1, DT_PAD)),
                  _const_spec((1, DT_PAD)), _const_spec((1, DT_PAD)), _const_spec((1, SSD_WIDTH)),
                  _const_spec((2 * DT_PAD, SSD_WIDTH))],
        out_specs=[row(SSD_WIDTH), st_spec],
        out_shape=[jax.ShapeDtypeStruct((nb * seq, SSD_WIDTH), BF16),
                   jax.ShapeDtypeStruct((nb, SSD_GROUPS, SSD_STATE, gw), F32)],
        scratch_shapes=[pltpu.VMEM((SSD_GROUPS, SSD_STATE, gw), F32), pltpu.VMEM((q + 8, SSD_XBC), F32)],
        compiler_params=pltpu.CompilerParams(dimension_semantics=("arbitrary", "arbitrary"),
                                             vmem_limit_bytes=VMEM_LIMIT),
        name="ssd",
    )(z, xbc, dt, cbuf, h0, cw, cb, dtb, alog, dsk, nw, e2)


def _attn_tile_counts(i, *, tq, tk, past, kv_len, minimum=min):
    q_lo = past + i * tq
    q_hi = q_lo + tq - 1
    lim_lo = minimum((q_lo // CHUNK + 1) * CHUNK, kv_len)
    lim_hi = minimum((q_hi // CHUNK + 1) * CHUNK, kv_len)
    return lim_lo // tk, (lim_hi + tk - 1) // tk


def _diff_attn_kernel(qt_ref, k_ref, vt_ref, lam_ref, sw_ref, o_ref,
                      q2t_ref, s_ref, p_ref, m_ref, l_ref, alpha_ref, acc_ref, *, tq, tk, past, kv_len, lam_init):
    i = pl.program_id(2)
    w = 2 * tq
    qt = qt_ref[0]
    row = lax.broadcasted_iota(jnp.int32, (ATT_V_DIM, tq), 0)
    zero = jnp.zeros_like(qt)
    q2t_ref[:, 0:tq] = jnp.where(row < ATT_HEAD_DIM, qt, zero)
    q2t_ref[:, tq:w] = jnp.where(row >= ATT_HEAD_DIM, qt, zero)

    q_lo = past + i * tq
    n_full, _ = _attn_tile_counts(i, tq=tq, tk=tk, past=past, kv_len=kv_len, minimum=jnp.minimum)

    def scores(j):
        kt = k_ref[pl.ds(pl.multiple_of(j * tk, tk), tk), :]
        return _dot(kt, q2t_ref[...])

    qpos = q_lo + lax.broadcasted_iota(jnp.int32, (tk, w), 1) % tq
    kpos = n_full * tk + lax.broadcasted_iota(jnp.int32, (tk, w), 0)
    s_ref[...] = jnp.where((kpos // CHUNK <= qpos // CHUNK) & (kpos < kv_len), scores(n_full), NEG_BIG)
    m_ref[...] = jnp.full_like(m_ref, NEG_BIG)
    l_ref[...] = jnp.zeros_like(l_ref)
    acc_ref[...] = jnp.zeros_like(acc_ref)
    p_ref[...] = jnp.zeros_like(p_ref)
    alpha_ref[...] = jnp.ones_like(alpha_ref)

    def visited_tile(t):
        return jnp.where(t == 0, n_full, jnp.maximum(t - 1, 0))

    def accumulate(t):
        acc_ref[...] = alpha_ref[...] * acc_ref[...] + _dot(vt_ref[visited_tile(t)], p_ref[...])

    def trip(t, carry):
        accumulate(t - 1)
        for cb in range(w // LANES):
            sl = slice(cb * LANES, (cb + 1) * LANES)
            s = s_ref[:, sl]
            m_old = m_ref[:, sl]
            m_new = jnp.maximum(m_old, jnp.max(s, axis=0, keepdims=True))
            alpha = jnp.exp2(m_old - m_new)
            p = jnp.exp2(s - m_new)
            l_ref[:, sl] = alpha * l_ref[:, sl] + jnp.sum(p, axis=0, keepdims=True)
            m_ref[:, sl] = m_new
            alpha_ref[:, sl] = alpha
            p_ref[:, sl] = p.astype(BF16)
        s_ref[...] = scores(t)
        return carry

    lax.fori_loop(0, n_full + 1, trip, 0)
    accumulate(n_full)

    lam_v = lam_ref[...]
    lam = (jnp.exp(jnp.sum(lam_v[0:1] * lam_v[1:2], axis=-1, keepdims=True))
           - jnp.exp(jnp.sum(lam_v[2:3] * lam_v[3:4], axis=-1, keepdims=True)) + lam_init)
    o = acc_ref[...] / l_ref[...]
    o = o[:, :tq] - lam * o[:, tq:]
    o = o * lax.rsqrt(jnp.mean(o * o, axis=0, keepdims=True) + ATT_NORM_EPS) * (sw_ref[...] * (1.0 - lam_init))
    o_ref[...] = o.T.astype(BF16)


def _diff_attn(qt, k, vt, lam_vecs, subln_col, *, nb, tq, tk, past, kv_len, lam_init):
    nq = qt.shape[0] // nb
    nkt = vt.shape[0] // nb
    for i in range(nq):
        n_full, n_end = _attn_tile_counts(i, tq=tq, tk=tk, past=past, kv_len=kv_len)
        assert n_end - n_full == 1 and n_end <= nkt, (i, n_full, n_end)
    w = 2 * tq
    return pl.pallas_call(
        functools.partial(_diff_attn_kernel, tq=tq, tk=tk, past=past, kv_len=kv_len, lam_init=lam_init),
        grid=(nb, ATT_HEADS, nq),
        in_specs=[pl.BlockSpec((1, ATT_V_DIM, tq), lambda b, h, i: (b * nq + i, h, 0)),
                  pl.BlockSpec((nkt * tk, ATT_V_DIM), lambda b, h, i: (b, h)),
                  pl.BlockSpec((nkt, ATT_V_DIM, tk), lambda b, h, i: (b, h, 0)),
                  _const_spec((4, ATT_HEAD_DIM)), _const_spec((ATT_V_DIM, 1))],
        out_specs=pl.BlockSpec((tq, ATT_V_DIM), lambda b, h, i: (b * nq + i, h)),
        out_shape=jax.ShapeDtypeStruct((nb * nq * tq, ATT_WIDTH), BF16),
        scratch_shapes=[pltpu.VMEM((ATT_V_DIM, w), BF16), pltpu.VMEM((tk, w), F32), pltpu.VMEM((tk, w), BF16),
                        pltpu.VMEM((1, w), F32), pltpu.VMEM((1, w), F32), pltpu.VMEM((1, w), F32),
                        pltpu.VMEM((ATT_V_DIM, w), F32)],
        compiler_params=pltpu.CompilerParams(dimension_semantics=("arbitrary", "arbitrary", "arbitrary"),
                                             vmem_limit_bytes=VMEM_LIMIT),
        name="diff_attn",
    )(qt, k, vt, lam_vecs, subln_col)


def _post_mix_kernel(x_ref, ys_ref, ya_ref, mk_ref, mv_ref, wo_ref, wq_ref, wox_ref,
                     g1_ref, g2_ref, g3_ref, h_ref):
    mix = _dot(ys_ref[...], wo_ref[0:SSD_WIDTH, :]) + _dot(ya_ref[...], wo_ref[SSD_WIDTH:, :])
    h = x_ref[...] + _rms(mix, g1_ref[...], NORM_EPS)
    qn = _rms(h, g2_ref[...], NORM_EPS).astype(BF16)
    qx = (_dot(qn, wq_ref[...]) * (1.0 / math.sqrt(MEM_HEAD_DIM))).astype(BF16)
    mk = mk_ref[0]
    mv = mv_ref[0]
    oxs = []
    for hd in range(MEM_HEADS):
        sl = slice(hd * MEM_HEAD_DIM, (hd + 1) * MEM_HEAD_DIM)
        s = lax.dot_general(qx[:, sl], mk[:, sl], (((1,), (1,)), ((), ())), preferred_element_type=F32)
        p = jnp.exp(s - jnp.max(s, axis=-1, keepdims=True))
        ox = _dot(p.astype(BF16), mv[:, sl]) / jnp.sum(p, axis=-1, keepdims=True)
        oxs.append(ox.astype(BF16))
    o2 = _dot(jnp.concatenate(oxs, axis=1), wox_ref[...])
    h_ref[...] = h + _rms(o2, g3_ref[...], NORM_EPS)


def _post_mix(x, ys, ya, mk, mv, w_out, wq, wox, g1, g2, g3, *, seq, tm):
    n = x.shape[0]
    tiles_per_seq = seq // tm
    row = lambda c: pl.BlockSpec((tm, c), lambda i: (i, 0))
    mem = pl.BlockSpec((1, MEM_LEN, D_MODEL), lambda i: (i // tiles_per_seq, 0, 0))
    wspec = _const_spec((D_MODEL, D_MODEL))
    gspec = _const_spec((1, D_MODEL))
    return pl.pallas_call(
        _post_mix_kernel,
        grid=(n // tm,),
        in_specs=[row(D_MODEL), row(SSD_WIDTH), row(ATT_WIDTH), mem, mem, wspec, wspec, wspec, gspec, gspec, gspec],
        out_specs=row(D_MODEL),
        out_shape=jax.ShapeDtypeStruct((n, D_MODEL), F32),
        compiler_params=pltpu.CompilerParams(dimension_semantics=("arbitrary",), vmem_limit_bytes=VMEM_LIMIT),
        name="post_mix",
    )(x, ys, ya, mk, mv, w_out, wq, wox, g1, g2, g3)


def _ffn_kernel(h_ref, wg_ref, wu_ref, wd_ref, g1_ref, g2_ref, o_ref):
    h = h_ref[...]
    hn = _rms(h, g1_ref[...], NORM_EPS).astype(BF16)
    act = (_silu(_dot(hn, wg_ref[...])) * _dot(hn, wu_ref[...])).astype(BF16)
    f = _dot(act, wd_ref[...])
    o_ref[...] = h + _rms(f, g2_ref[...], NORM_EPS)


def _ffn(h, wg, wu, wd, g1, g2, *, tm):
    n = h.shape[0]
    row = pl.BlockSpec((tm, D_MODEL), lambda i: (i, 0))
    return pl.pallas_call(
        _ffn_kernel,
        grid=(n // tm,),
        in_specs=[row, _const_spec((D_MODEL, FFN_HIDDEN)), _const_spec((D_MODEL, FFN_HIDDEN)),
                  _const_spec((FFN_HIDDEN, D_MODEL)), _const_spec((1, D_MODEL)), _const_spec((1, D_MODEL))],
        out_specs=row,
        out_shape=jax.ShapeDtypeStruct((n, D_MODEL), F32),
        compiler_params=pltpu.CompilerParams(dimension_semantics=("arbitrary",), vmem_limit_bytes=VMEM_LIMIT),
        name="ffn",
    )(h, wg, wu, wd, g1, g2)


def _rope_tables(past, seq):
    half = ATT_HEAD_DIM // 2
    inv = jnp.power(ROPE_THETA, -jnp.arange(0, ATT_HEAD_DIM, 2, dtype=F32) / ATT_HEAD_DIM)
    pos = (past + jnp.arange(seq, dtype=jnp.int32)).astype(F32)
    ang = pos[:, None] * inv[None, :]
    cos, sin = jnp.cos(ang), jnp.sin(ang)
    reps = LANES // ATT_HEAD_DIM
    assert half * 2 == ATT_HEAD_DIM
    return jnp.tile(jnp.concatenate([cos, cos], axis=-1), (1, reps)), jnp.tile(jnp.concatenate([-sin, sin], axis=-1), (1, reps))


def _state_to_kernel_layout(s):
    b = s.shape[0]
    s = s.reshape(b, SSD_GROUPS, SSD_REP, SSD_HEAD_DIM, SSD_STATE)
    return s.transpose(0, 1, 4, 2, 3).reshape(b, SSD_GROUPS, SSD_STATE, SSD_REP * SSD_HEAD_DIM)


def _state_from_kernel_layout(s):
    b = s.shape[0]
    s = s.reshape(b, SSD_GROUPS, SSD_STATE, SSD_REP, SSD_HEAD_DIM)
    return s.transpose(0, 1, 3, 4, 2).reshape(b, SSD_HEADS, SSD_HEAD_DIM, SSD_STATE)


def _layer(x, conv_buf, ssm0, k_past, v_past, mem_kb, mem_vb, lam_init, p, *, tm, tk, tq, ssd_q):
    nb, seq, _ = x.shape
    n = nb * seq
    past = 0 if k_past is None else k_past.shape[1]
    xf = x.reshape(n, D_MODEL)
    cos, sin = _rope_tables(past, seq)
    no_history = k_past is None
    assert tq == tk or not no_history
    z, xbc, tail, dt, q, k, v, qt, kb, vt = _in_proj(xf, p["g_pre_mix"], p["w_in"], cos, sin, seq=seq, tm=tm,
                                                      tt=tk if no_history else 0)

    seq_pad = -(-seq // ssd_q) * ssd_q
    if seq_pad != seq:
        pad = lambda a: jnp.pad(a.reshape(nb, seq, -1), ((0, 0), (0, seq_pad - seq), (0, 0))).reshape(nb * seq_pad, -1)
        z_s, xbc_s, dt_s = pad(z), pad(xbc), pad(dt)
    else:
        z_s, xbc_s, dt_s = z, xbc, dt
    cbuf = jnp.pad(conv_buf.astype(F32), ((0, 0), (8 - (SSD_CONV - 1), 0), (0, 0)))
    y_ssd, h_new = _ssd(z_s, xbc_s, dt_s, cbuf, _state_to_kernel_layout(ssm0.astype(F32)),
                        p["conv_w"], p["conv_b"], p["dt_bias"], p["a_log"], p["d_skip"], p["ssm_norm_w"], p["e2"],
                        nb=nb, seq=seq_pad, q=ssd_q, valid=None if seq_pad == seq else seq)
    if seq_pad != seq:
        y_ssd = y_ssd.reshape(nb, seq_pad, SSD_WIDTH)[:, :seq].reshape(n, SSD_WIDTH)
    ssm_new = _state_from_kernel_layout(h_new)
    ext_tail = tail.reshape(nb, seq // tm, 8, SSD_XBC)[:, -1]
    if seq >= SSD_CONV - 1:
        conv_new = ext_tail[:, 8 - (SSD_CONV - 1):]
    else:
        conv_new = jnp.concatenate([conv_buf.astype(F32), ext_tail[:, 8 - seq:]], axis=1)[:, -(SSD_CONV - 1):]

    if no_history:
        kv_len = seq
        y_att = _diff_attn(qt, kb, vt, p["lam_vecs"], p["subln_col"], nb=nb, tq=tq, tk=tk,
                           past=past, kv_len=kv_len, lam_init=lam_init)
    else:
        kv_len = past + seq
        kv_pad = -(-kv_len // tk) * tk
        k_all = jnp.concatenate([k_past.astype(BF16), k.reshape(nb, seq, ATT_WIDTH).astype(BF16)], axis=1)
        k_all = jnp.pad(k_all, ((0, 0), (0, kv_pad - kv_len), (0, 0))).reshape(-1, ATT_WIDTH)
        v_all = jnp.concatenate([v_past.astype(BF16), v.reshape(nb, seq, ATT_WIDTH).astype(BF16)], axis=1)
        v_all = jnp.pad(v_all, ((0, 0), (0, kv_pad - kv_len), (0, 0)))
        vt_all = v_all.reshape(nb, kv_pad // tk, tk, ATT_WIDTH).transpose(0, 1, 3, 2).reshape(-1, ATT_WIDTH, tk)
        assert seq <= tq
        q_pad = jnp.pad(q.reshape(nb, seq, ATT_WIDTH), ((0, 0), (0, tq - seq), (0, 0)))
        y_att = _diff_attn(q_pad.transpose(0, 2, 1), k_all, vt_all, p["lam_vecs"], p["subln_col"], nb=nb, tq=tq, tk=tk,
                           past=past, kv_len=kv_len, lam_init=lam_init)
        y_att = y_att.reshape(nb, tq, ATT_WIDTH)[:, :seq].reshape(n, ATT_WIDTH)

    h = _post_mix(xf, y_ssd, y_att, mem_kb, mem_vb, p["w_out"], p["wq_x"], p["wo_x"],
                  p["g_post_mix"], p["g_pre_x"], p["g_post_x"], seq=seq, tm=tm)
    out = _ffn(h, p["w_gate"], p["w_up"], p["w_down"], p["g_pre_ffn"], p["g_post_ffn"], tm=tm)
    return (out.reshape(nb, seq, D_MODEL), k.reshape(nb, seq, ATT_HEADS, 2, ATT_HEAD_DIM),
            v.reshape(nb, seq, ATT_HEADS, ATT_V_DIM), ssm_new, conv_new)


def _prep_params(i, w_in, conv_w, conv_b, dt_bias, a_log, d_skip, ssm_norm_w, lam_q1, lam_k1, lam_q2, lam_k2, subln_w,
                 w_out, wq_x, wo_x, g_pre_mix, g_post_mix, g_pre_x, g_post_x, g_pre_ffn, g_post_ffn,
                 w_gate, w_up, w_down):
    w = w_in[i]
    s0 = SSD_WIDTH + SSD_XBC
    s1 = s0 + SSD_HEADS
    w_r = jnp.concatenate([w[:, :s0], w[:, s1:], w[:, s0:s1], jnp.zeros((D_MODEL, DT_PAD - SSD_HEADS), w.dtype)], axis=1)
    head_pad = lambda a: jnp.pad(a[i].astype(F32), (0, DT_PAD - SSD_HEADS)).reshape(1, DT_PAD)
    e = (jnp.arange(DT_PAD)[:, None] == (jnp.arange(SSD_WIDTH)[None, :] // SSD_HEAD_DIM)).astype(BF16)
    row = lambda a: a[i].astype(F32).reshape(1, -1)
    return {
        "w_in": w_r.astype(BF16), "conv_w": conv_w[i].astype(F32), "conv_b": row(conv_b),
        "dt_bias": head_pad(dt_bias), "a_log": head_pad(a_log), "d_skip": head_pad(d_skip),
        "ssm_norm_w": row(ssm_norm_w), "e2": jnp.concatenate([e, e], axis=0),
        "lam_vecs": jnp.stack([lam_q1[i], lam_k1[i], lam_q2[i], lam_k2[i]]).astype(F32), "subln_col": subln_w[i].astype(F32).reshape(ATT_V_DIM, 1),
        "w_out": w_out[i].astype(BF16), "wq_x": wq_x[i].astype(BF16), "wo_x": wo_x[i].astype(BF16),
        "g_pre_mix": row(g_pre_mix), "g_post_mix": row(g_post_mix), "g_pre_x": row(g_pre_x), "g_post_x": row(g_post_x),
        "g_pre_ffn": row(g_pre_ffn), "g_post_ffn": row(g_post_ffn),
        "w_gate": w_gate[i].astype(BF16), "w_up": w_up[i].astype(BF16), "w_down": w_down[i].astype(BF16),
    }


def kernel(x_prompt, x_sample, cache_attn_k, cache_attn_v, state_ssm, state_conv, cache_mem_k, cache_mem_v, mem_prompt, w_in, conv_w, conv_b, dt_bias, a_log, d_skip, ssm_norm_w, lam_q1, lam_k1, lam_q2, lam_k2, subln_w, w_out, g_mem, wq_x, wk_x, wv_x, wo_x, g_pre_mix, g_post_mix, g_pre_x, g_post_x, g_pre_ffn, g_post_ffn, w_gate, w_up, w_down):
    depth = w_in.shape[0]
    bp, sp, _ = x_prompt.shape
    bs, ss, _ = x_sample.shape
    hp, hs = x_prompt, x_sample
    outs = [[] for _ in range(10)]
    for i in range(depth):
        lam_init = 0.8 - 0.6 * math.exp(-0.3 * i)
        p = _prep_params(i, w_in, conv_w, conv_b, dt_bias, a_log, d_skip, ssm_norm_w, lam_q1, lam_k1, lam_q2, lam_k2,
                         subln_w, w_out, wq_x, wo_x, g_pre_mix, g_post_mix, g_pre_x, g_post_x, g_pre_ffn, g_post_ffn,
                         w_gate, w_up, w_down)
        mk, mv, mkb, mvb = _mem_kv(mem_prompt.reshape(bp * MEM_LEN, D_MODEL), g_mem[i].reshape(1, D_MODEL),
                                   wk_x[i].astype(BF16), wv_x[i].astype(BF16))
        hp, k_new, v_new, ssm_new, conv_new = _layer(
            hp, jnp.zeros((bp, SSD_CONV - 1, SSD_XBC), F32), jnp.zeros((bp, SSD_HEADS, SSD_HEAD_DIM, SSD_STATE), F32),
            None, None, mkb.reshape(bp, MEM_LEN, D_MODEL), mvb.reshape(bp, MEM_LEN, D_MODEL), lam_init, p,
            tm=512, tk=512, tq=512, ssd_q=256)
        for lst, val in zip(outs[:6], (k_new, v_new, ssm_new, conv_new,
                                       mk.reshape(bp, MEM_LEN, MEM_HEADS, MEM_HEAD_DIM),
                                       mv.reshape(bp, MEM_LEN, MEM_HEADS, MEM_HEAD_DIM))):
            lst.append(val)
        past = cache_attn_k.shape[2]
        hs, k_new, v_new, ssm_new, conv_new = _layer(
            hs, state_conv[i], state_ssm[i], cache_attn_k[i].reshape(bs, past, ATT_WIDTH),
            cache_attn_v[i].reshape(bs, past, ATT_WIDTH),
            cache_mem_k[i].reshape(bs, MEM_LEN, D_MODEL).astype(BF16),
            cache_mem_v[i].reshape(bs, MEM_LEN, D_MODEL).astype(BF16), lam_init, p,
            tm=ss, tk=256, tq=128, ssd_q=128)
        for lst, val in zip(outs[6:], (k_new, v_new, ssm_new, conv_new)):
            lst.append(val)
    return (hp, hs) + tuple(jnp.stack(o) for o in outs)
```

```python
import functools
import math

import jax
import jax.numpy as jnp
from jax import lax
from jax.experimental import pallas as pl
from jax.experimental.pallas import tpu as pltpu

D_MODEL = 1024
CHUNK = 64
SSD_WIDTH = 512
SSD_HEAD_DIM = 64
SSD_HEADS = 8
SSD_GROUPS = 2
SSD_REP = 4
SSD_STATE = 128
SSD_CONV = 4
SSD_XBC = 1024
SSD_NORM_GROUP = 256
SSD_NORM_EPS = 1e-5
ATT_WIDTH = 512
ATT_HEAD_DIM = 64
ATT_HEADS = 4
ATT_V_DIM = 128
ATT_NORM_EPS = 1e-5
ROPE_THETA = 10000.0
MEM_LEN = 256
MEM_HEADS = 4
MEM_HEAD_DIM = 256
FFN_HIDDEN = 2816
NORM_EPS = 1e-6
LANES = 128
DT_PAD = LANES
IN_COLS_PADDED = SSD_WIDTH + SSD_XBC + 3 * ATT_WIDTH + DT_PAD
VMEM_LIMIT = 56 * 1024 * 1024
NEG_BIG = -1e30
Q_SCALE = math.log2(math.e) / math.sqrt(ATT_HEAD_DIM)

F32 = jnp.float32
BF16 = jnp.bfloat16


def _const_spec(shape):
    return pl.BlockSpec(shape, lambda *_: (0,) * len(shape), pipeline_mode=pl.Buffered(1))


def _rms(x, g, eps):
    return x * lax.rsqrt(jnp.mean(x * x, axis=-1, keepdims=True) + eps) * g


def _silu(x):
    return x / (1.0 + jnp.exp(-x))


def _dot(a, b):
    return jnp.dot(a, b, preferred_element_type=F32)


def _mem_kv_kernel(mem_ref, g_ref, wk_ref, wv_ref, mk_ref, mv_ref, mkb_ref, mvb_ref):
    mn = _rms(mem_ref[...], g_ref[...], NORM_EPS).astype(BF16)
    mk = _dot(mn, wk_ref[...])
    mv = _dot(mn, wv_ref[...])
    mk_ref[...] = mk
    mv_ref[...] = mv
    mkb_ref[...] = mk.astype(BF16)
    mvb_ref[...] = mv.astype(BF16)


def _mem_kv(mem, g_mem, wk, wv):
    n = mem.shape[0]
    tm = MEM_LEN
    row = pl.BlockSpec((tm, D_MODEL), lambda i: (i, 0))
    return pl.pallas_call(
        _mem_kv_kernel,
        grid=(n // tm,),
        in_specs=[row, _const_spec((1, D_MODEL)), _const_spec((D_MODEL, D_MODEL)), _const_spec((D_MODEL, D_MODEL))],
        out_specs=[row, row, row, row],
        out_shape=[jax.ShapeDtypeStruct((n, D_MODEL), F32), jax.ShapeDtypeStruct((n, D_MODEL), F32),
                   jax.ShapeDtypeStruct((n, D_MODEL), BF16), jax.ShapeDtypeStruct((n, D_MODEL), BF16)],
        compiler_params=pltpu.CompilerParams(dimension_semantics=("arbitrary",), vmem_limit_bytes=VMEM_LIMIT),
        name="mem_kv",
    )(mem, g_mem, wk, wv)


def _in_proj_kernel(x_ref, g_ref, w_ref, cos_ref, sin_ref,
                    z_ref, xbc_ref, tail_ref, dt_ref, q_ref, k_ref, v_ref, qt_ref, kb_ref, vt_ref, *, tm, tt):
    hn = _rms(x_ref[...], g_ref[...], NORM_EPS).astype(BF16)
    acc = _dot(hn, w_ref[...])
    o = 0
    z_ref[...] = acc[:, o:o + SSD_WIDTH].astype(BF16)
    o += SSD_WIDTH
    xbc = acc[:, o:o + SSD_XBC]
    xbc_ref[...] = xbc.astype(BF16)
    tail_ref[0] = xbc[tm - 8:, :]
    o += SSD_XBC
    q = acc[:, o:o + ATT_WIDTH]
    o += ATT_WIDTH
    k = acc[:, o:o + ATT_WIDTH]
    o += ATT_WIDTH
    v = acc[:, o:o + ATT_WIDTH]
    o += ATT_WIDTH
    dt_ref[...] = acc[:, o:o + DT_PAD]
    v_ref[...] = v

    cos = cos_ref[...]
    sin = sin_ref[...]
    first_half = (lax.broadcasted_iota(jnp.int32, (tm, LANES), 1) % ATT_HEAD_DIM) < (ATT_HEAD_DIM // 2)

    def rope(t):
        swapped = jnp.where(first_half, pltpu.roll(t, LANES - ATT_HEAD_DIM // 2, 1),
                            pltpu.roll(t, ATT_HEAD_DIM // 2, 1))
        return t * cos + swapped * sin

    for j in range(ATT_WIDTH // LANES):
        sl = slice(j * LANES, (j + 1) * LANES)
        qr = rope(q[:, sl]) * Q_SCALE
        kr = rope(k[:, sl])
        q_ref[:, sl] = qr.astype(BF16)
        if tt:
            k_ref[0, sl, :] = kr.T
            kb_ref[:, sl] = kr.astype(BF16)
            for c in range(tm // tt):
                qt_ref[c, sl, :] = qr[c * tt:(c + 1) * tt, :].T.astype(BF16)
                vt_ref[c, sl, :] = v[c * tt:(c + 1) * tt, sl].T.astype(BF16)
        else:
            k_ref[:, sl] = kr


def _in_proj(x, g, w, cos, sin, *, seq, tm, tt):
    n = x.shape[0]
    nt = n // tm
    tiles_per_seq = seq // tm
    row = lambda c: pl.BlockSpec((tm, c), lambda i: (i, 0))
    tab = pl.BlockSpec((tm, LANES), lambda i: (i % tiles_per_seq, 0))
    if tt:
        k_spec = pl.BlockSpec((1, ATT_WIDTH, tm), lambda i: (i // tiles_per_seq, 0, i % tiles_per_seq))
        k_shape = jax.ShapeDtypeStruct((n // seq, ATT_WIDTH, seq), F32)
    else:
        k_spec, k_shape = row(ATT_WIDTH), jax.ShapeDtypeStruct((n, ATT_WIDTH), F32)
    out_specs = [row(SSD_WIDTH), row(SSD_XBC), pl.BlockSpec((1, 8, SSD_XBC), lambda i: (i, 0, 0)), row(DT_PAD),
                 row(ATT_WIDTH), k_spec, row(ATT_WIDTH)]
    out_shape = [jax.ShapeDtypeStruct((n, SSD_WIDTH), BF16), jax.ShapeDtypeStruct((n, SSD_XBC), BF16),
                 jax.ShapeDtypeStruct((nt, 8, SSD_XBC), F32), jax.ShapeDtypeStruct((n, DT_PAD), F32),
                 jax.ShapeDtypeStruct((n, ATT_WIDTH), BF16), k_shape,
                 jax.ShapeDtypeStruct((n, ATT_WIDTH), F32)]
    if tt:
        tr = pl.BlockSpec((tm // tt, ATT_WIDTH, tt), lambda i: (i, 0, 0))
        out_specs += [tr, row(ATT_WIDTH), tr]
        out_shape += [jax.ShapeDtypeStruct((n // tt, ATT_WIDTH, tt), BF16), jax.ShapeDtypeStruct((n, ATT_WIDTH), BF16),
                      jax.ShapeDtypeStruct((n // tt, ATT_WIDTH, tt), BF16)]

    def body(*refs):
        refs = refs + (None,) * (15 - len(refs))
        _in_proj_kernel(*refs, tm=tm, tt=tt)

    outs = pl.pallas_call(
        body,
        grid=(nt,),
        in_specs=[row(D_MODEL), _const_spec((1, D_MODEL)), _const_spec((D_MODEL, IN_COLS_PADDED)), tab, tab],
        out_specs=out_specs,
        out_shape=out_shape,
        compiler_params=pltpu.CompilerParams(dimension_semantics=("arbitrary",), vmem_limit_bytes=VMEM_LIMIT),
        name="in_proj",
    )(x, g, w, cos, sin)
    return list(outs) + [None] * (10 - len(outs))


def _expand_heads(x, e2):
    hi = x.astype(BF16)
    lo = (x - hi.astype(F32)).astype(BF16)
    return _dot(jnp.concatenate([hi, lo], axis=1), e2)


def _ssd_kernel(z_ref, xbc_ref, dt_ref, cbuf_ref, h0_ref, cw_ref, cb_ref, dtb_ref, alog_ref, dsk_ref, nw_ref, e2_ref,
                y_ref, hout_ref, state, ext, *, q, valid):
    c = pl.program_id(1)

    @pl.when(c == 0)
    def _():
        state[...] = h0_ref[0]
        ext[0:8, :] = cbuf_ref[0]

    ext[8:8 + q, :] = xbc_ref[...].astype(F32)
    conv = cb_ref[...] + ext[5:5 + q, :] * cw_ref[0:1, :]
    for w in range(1, SSD_CONV):
        conv = conv + ext[5 + w:5 + w + q, :] * cw_ref[w:w + 1, :]
    ext[0:8, :] = ext[q:q + 8, :]
    u = _silu(conv)
    xs = u[:, :SSD_WIDTH]
    bm = u[:, SSD_WIDTH:SSD_WIDTH + SSD_GROUPS * SSD_STATE]
    cm = u[:, SSD_WIDTH + SSD_GROUPS * SSD_STATE:]

    dtr = dt_ref[...] + dtb_ref[...]
    dt = jnp.maximum(dtr, 0.0) + jnp.log(1.0 + jnp.exp(-jnp.abs(dtr)))
    if valid is not None:
        row = lax.broadcasted_iota(jnp.int32, (q, DT_PAD), 0) + c * q
        dt = jnp.where(row < valid, dt, 0.0)
    a = -jnp.exp(alog_ref[...])
    ad = dt * a
    ri = lax.broadcasted_iota(jnp.int32, (q, q), 0)
    ci = lax.broadcasted_iota(jnp.int32, (q, q), 1)
    tril = ri >= ci
    acum = jnp.dot(tril.astype(F32), ad, preferred_element_type=F32, precision=lax.Precision.HIGHEST)
    acum_t = acum.T
    tot = acum[q - 1:q, :]
    e2 = e2_ref[...]
    expanded = _expand_heads(jnp.concatenate([dt, dt * jnp.exp(tot - acum), jnp.exp(acum)], axis=0), e2)
    dtx = expanded[0:q]
    ddx = expanded[q:2 * q]
    eax = expanded[2 * q:3 * q]
    dsk = _expand_heads(jnp.broadcast_to(dsk_ref[...], (8, DT_PAD)), e2)[0:1]

    xdt = (xs * dtx).astype(BF16)
    xdtd = (xs * ddx).astype(BF16)
    bm_t = bm.T.astype(BF16)
    cmb = cm.astype(BF16)
    gw = SSD_REP * SSD_HEAD_DIM
    stripe = lax.broadcasted_iota(jnp.int32, (q, gw), 1) // SSD_HEAD_DIM
    ys = []
    for g in range(SSD_GROUPS):
        cm_g = cmb[:, g * SSD_STATE:(g + 1) * SSD_STATE]
        bt_g = bm_t[g * SSD_STATE:(g + 1) * SSD_STATE, :]
        cbm = _dot(cm_g, bt_g)
        ms = []
        for r in range(SSD_REP):
            h = g * SSD_REP + r
            diff = acum[:, h:h + 1] - acum_t[h:h + 1, :]
            ms.append((cbm * jnp.exp(jnp.where(tril, diff, -jnp.inf))).astype(BF16))
        ydf = _dot(jnp.concatenate(ms, axis=0), xdt[:, g * gw:(g + 1) * gw])
        yd = ydf[0:q]
        for r in range(1, SSD_REP):
            yd = jnp.where(stripe == r, ydf[r * q:(r + 1) * q], yd)
        st = state[g]
        y_off = _dot(cm_g, st.astype(BF16)) * eax[:, g * gw:(g + 1) * gw]
        state[g] = st * eax[q - 1:q, g * gw:(g + 1) * gw] + _dot(bt_g, xdtd[:, g * gw:(g + 1) * gw])
        ys.append(yd + y_off)
    y = jnp.concatenate(ys, axis=1) + dsk * xs
    y = y * _silu(z_ref[...].astype(F32))
    outs = []
    for g in range(SSD_WIDTH // SSD_NORM_GROUP):
        yg = y[:, g * SSD_NORM_GROUP:(g + 1) * SSD_NORM_GROUP]
        outs.append(yg * lax.rsqrt(jnp.mean(yg * yg, axis=-1, keepdims=True) + SSD_NORM_EPS))
    y_ref[...] = (jnp.concatenate(outs, axis=1) * nw_ref[...]).astype(BF16)

    @pl.when(c == pl.num_programs(1) - 1)
    def _():
        hout_ref[0] = state[...]


def _ssd(z, xbc, dt, cbuf, h0, cw, cb, dtb, alog, dsk, nw, e2, *, nb, seq, q, valid):
    nc = seq // q
    row = lambda c: pl.BlockSpec((q, c), lambda b, i: (b * nc + i, 0))
    gw = SSD_REP * SSD_HEAD_DIM
    st_spec = pl.BlockSpec((1, SSD_GROUPS, SSD_STATE, gw), lambda b, i: (b, 0, 0, 0))
    return pl.pallas_call(
        functools.partial(_ssd_kernel, q=q, valid=valid),
        grid=(nb, nc),
        in_specs=[row(SSD_WIDTH), row(SSD_XBC), row(DT_PAD),
                  pl.BlockSpec((1, 8, SSD_XBC), lambda b, i: (b, 0, 0)), st_spec,
                  _const_spec((SSD_CONV, SSD_XBC)), _const_spec((1, SSD_XBC)), _const_spec((1, DT_PAD)),
                  _const_spec((1, DT_PAD)), _const_spec((1, DT_PAD)), _const_spec((1, SSD_WIDTH)),
                  _const_spec((2 * DT_PAD, SSD_WIDTH))],
        out_specs=[row(SSD_WIDTH), st_spec],
        out_shape=[jax.ShapeDtypeStruct((nb * seq, SSD_WIDTH), BF16),
                   jax.ShapeDtypeStruct((nb, SSD_GROUPS, SSD_STATE, gw), F32)],
        scratch_shapes=[pltpu.VMEM((SSD_GROUPS, SSD_STATE, gw), F32), pltpu.VMEM((q + 8, SSD_XBC), F32)],
        compiler_params=pltpu.CompilerParams(dimension_semantics=("arbitrary", "arbitrary"),
                                             vmem_limit_bytes=VMEM_LIMIT),
        name="ssd",
    )(z, xbc, dt, cbuf, h0, cw, cb, dtb, alog, dsk, nw, e2)


def _attn_tile_counts(i, *, tq, tk, past, kv_len, minimum=min):
    q_lo = past + i * tq
    q_hi = q_lo + tq - 1
    lim_lo = minimum((q_lo // CHUNK + 1) * CHUNK, kv_len)
    lim_hi = minimum((q_hi // CHUNK + 1) * CHUNK, kv_len)
    return lim_lo // tk, (lim_hi + tk - 1) // tk


def _diff_attn_kernel(qt_ref, k_ref, vt_ref, lam_ref, sw_ref, o_ref,
                      q2t_ref, s_ref, p_ref, m_ref, l_ref, alpha_ref, acc_ref, *, tq, tk, past, kv_len, lam_init):
    i = pl.program_id(2)
    w = 2 * tq
    qt = qt_ref[0]
    row = lax.broadcasted_iota(jnp.int32, (ATT_V_DIM, tq), 0)
    zero = jnp.zeros_like(qt)
    q2t_ref[:, 0:tq] = jnp.where(row < ATT_HEAD_DIM, qt, zero)
    q2t_ref[:, tq:w] = jnp.where(row >= ATT_HEAD_DIM, qt, zero)

    q_lo = past + i * tq
    n_full, _ = _attn_tile_counts(i, tq=tq, tk=tk, past=past, kv_len=kv_len, minimum=jnp.minimum)

    def scores(j):
        kt = k_ref[pl.ds(pl.multiple_of(j * tk, tk), tk), :]
        return _dot(kt, q2t_ref[...])

    qpos = q_lo + lax.broadcasted_iota(jnp.int32, (tk, w), 1) % tq
    kpos = n_full * tk + lax.broadcasted_iota(jnp.int32, (tk, w), 0)
    s_ref[...] = jnp.where((kpos // CHUNK <= qpos // CHUNK) & (kpos < kv_len), scores(n_full), NEG_BIG)
    m_ref[...] = jnp.full_like(m_ref, NEG_BIG)
    l_ref[...] = jnp.zeros_like(l_ref)
    acc_ref[...] = jnp.zeros_like(acc_ref)
    p_ref[...] = jnp.zeros_like(p_ref)
    alpha_ref[...] = jnp.ones_like(alpha_ref)

    def visited_tile(t):
        return jnp.where(t == 0, n_full, jnp.maximum(t - 1, 0))

    def accumulate(t):
        acc_ref[...] = alpha_ref[...] * acc_ref[...] + _dot(vt_ref[visited_tile(t)], p_ref[...])

    def trip(t, carry):
        accumulate(t - 1)
        for cb in range(w // LANES):
            sl = slice(cb * LANES, (cb + 1) * LANES)
            s = s_ref[:, sl]
            m_old = m_ref[:, sl]
            m_new = jnp.maximum(m_old, jnp.max(s, axis=0, keepdims=True))
            alpha = jnp.exp2(m_old - m_new)
            p = jnp.exp2(s - m_new)
            l_ref[:, sl] = alpha * l_ref[:, sl] + jnp.sum(p, axis=0, keepdims=True)
            m_ref[:, sl] = m_new
            alpha_ref[:, sl] = alpha
            p_ref[:, sl] = p.astype(BF16)
        s_ref[...] = scores(t)
        return carry

    lax.fori_loop(0, n_full + 1, trip, 0)
    accumulate(n_full)

    lam_v = lam_ref[...]
    lam = (jnp.exp(jnp.sum(lam_v[0:1] * lam_v[1:2], axis=-1, keepdims=True))
           - jnp.exp(jnp.sum(lam_v[2:3] * lam_v[3:4], axis=-1, keepdims=True)) + lam_init)
    o = acc_ref[...] / l_ref[...]
    o = o[:, :tq] - lam * o[:, tq:]
    o = o * lax.rsqrt(jnp.mean(o * o, axis=0, keepdims=True) + ATT_NORM_EPS) * (sw_ref[...] * (1.0 - lam_init))
    o_ref[...] = o.T.astype(BF16)


def _diff_attn(qt, k, vt, lam_vecs, subln_col, *, nb, tq, tk, past, kv_len, lam_init):
    nq = qt.shape[0] // nb
    nkt = vt.shape[0] // nb
    for i in range(nq):
        n_full, n_end = _attn_tile_counts(i, tq=tq, tk=tk, past=past, kv_len=kv_len)
        assert n_end - n_full == 1 and n_end <= nkt, (i, n_full, n_end)
    w = 2 * tq
    return pl.pallas_call(
        functools.partial(_diff_attn_kernel, tq=tq, tk=tk, past=past, kv_len=kv_len, lam_init=lam_init),
        grid=(nb, ATT_HEADS, nq),
        in_specs=[pl.BlockSpec((1, ATT_V_DIM, tq), lambda b, h, i: (b * nq + i, h, 0)),
                  pl.BlockSpec((nkt * tk, ATT_V_DIM), lambda b, h, i: (b, h)),
                  pl.BlockSpec((nkt, ATT_V_DIM, tk), lambda b, h, i: (b, h, 0)),
                  _const_spec((4, ATT_HEAD_DIM)), _const_spec((ATT_V_DIM, 1))],
        out_specs=pl.BlockSpec((tq, ATT_V_DIM), lambda b, h, i: (b * nq + i, h)),
        out_shape=jax.ShapeDtypeStruct((nb * nq * tq, ATT_WIDTH), BF16),
        scratch_shapes=[pltpu.VMEM((ATT_V_DIM, w), BF16), pltpu.VMEM((tk, w), F32), pltpu.VMEM((tk, w), BF16),
                        pltpu.VMEM((1, w), F32), pltpu.VMEM((1, w), F32), pltpu.VMEM((1, w), F32),
                        pltpu.VMEM((ATT_V_DIM, w), F32)],
        compiler_params=pltpu.CompilerParams(dimension_semantics=("arbitrary", "arbitrary", "arbitrary"),
                                             vmem_limit_bytes=VMEM_LIMIT),
        name="diff_attn",
    )(qt, k, vt, lam_vecs, subln_col)


def _post_mix_kernel(x_ref, ys_ref, ya_ref, mk_ref, mv_ref, wo_ref, wq_ref, wox_ref,
                     g1_ref, g2_ref, g3_ref, h_ref):
    mix = _dot(ys_ref[...], wo_ref[0:SSD_WIDTH, :]) + _dot(ya_ref[...], wo_ref[SSD_WIDTH:, :])
    h = x_ref[...] + _rms(mix, g1_ref[...], NORM_EPS)
    qn = _rms(h, g2_ref[...], NORM_EPS).astype(BF16)
    qx = (_dot(qn, wq_ref[...]) * (1.0 / math.sqrt(MEM_HEAD_DIM))).astype(BF16)
    mk = mk_ref[0]
    mv = mv_ref[0]
    oxs = []
    for hd in range(MEM_HEADS):
        sl = slice(hd * MEM_HEAD_DIM, (hd + 1) * MEM_HEAD_DIM)
        s = lax.dot_general(qx[:, sl], mk[:, sl], (((1,), (1,)), ((), ())), preferred_element_type=F32)
        p = jnp.exp(s - jnp.max(s, axis=-1, keepdims=True))
        ox = _dot(p.astype(BF16), mv[:, sl]) / jnp.sum(p, axis=-1, keepdims=True)
        oxs.append(ox.astype(BF16))
    o2 = _dot(jnp.concatenate(oxs, axis=1), wox_ref[...])
    h_ref[...] = h + _rms(o2, g3_ref[...], NORM_EPS)


def _post_mix(x, ys, ya, mk, mv, w_out, wq, wox, g1, g2, g3, *, seq, tm):
    n = x.shape[0]
    tiles_per_seq = seq // tm
    row = lambda c: pl.BlockSpec((tm, c), lambda i: (i, 0))
    mem = pl.BlockSpec((1, MEM_LEN, D_MODEL), lambda i: (i // tiles_per_seq, 0, 0))
    wspec = _const_spec((D_MODEL, D_MODEL))
    gspec = _const_spec((1, D_MODEL))
    return pl.pallas_call(
        _post_mix_kernel,
        grid=(n // tm,),
        in_specs=[row(D_MODEL), row(SSD_WIDTH), row(ATT_WIDTH), mem, mem, wspec, wspec, wspec, gspec, gspec, gspec],
        out_specs=row(D_MODEL),
        out_shape=jax.ShapeDtypeStruct((n, D_MODEL), F32),
        compiler_params=pltpu.CompilerParams(dimension_semantics=("arbitrary",), vmem_limit_bytes=VMEM_LIMIT),
        name="post_mix",
    )(x, ys, ya, mk, mv, w_out, wq, wox, g1, g2, g3)


def _ffn_kernel(h_ref, wg_ref, wu_ref, wd_ref, g1_ref, g2_ref, o_ref):
    h = h_ref[...]
    hn = _rms(h, g1_ref[...], NORM_EPS).astype(BF16)
    act = (_silu(_dot(hn, wg_ref[...])) * _dot(hn, wu_ref[...])).astype(BF16)
    f = _dot(act, wd_ref[...])
    o_ref[...] = h + _rms(f, g2_ref[...], NORM_EPS)


def _ffn(h, wg, wu, wd, g1, g2, *, tm):
    n = h.shape[0]
    row = pl.BlockSpec((tm, D_MODEL), lambda i: (i, 0))
    return pl.pallas_call(
        _ffn_kernel,
        grid=(n // tm,),
        in_specs=[row, _const_spec((D_MODEL, FFN_HIDDEN)), _const_spec((D_MODEL, FFN_HIDDEN)),
                  _const_spec((FFN_HIDDEN, D_MODEL)), _const_spec((1, D_MODEL)), _const_spec((1, D_MODEL))],
        out_specs=row,
        out_shape=jax.ShapeDtypeStruct((n, D_MODEL), F32),
        compiler_params=pltpu.CompilerParams(dimension_semantics=("arbitrary",), vmem_limit_bytes=VMEM_LIMIT),
        name="ffn",
    )(h, wg, wu, wd, g1, g2)


def _rope_tables(past, seq):
    half = ATT_HEAD_DIM // 2
    inv = jnp.power(ROPE_THETA, -jnp.arange(0, ATT_HEAD_DIM, 2, dtype=F32) / ATT_HEAD_DIM)
    pos = (past + jnp.arange(seq, dtype=jnp.int32)).astype(F32)
    ang = pos[:, None] * inv[None, :]
    cos, sin = jnp.cos(ang), jnp.sin(ang)
    reps = LANES // ATT_HEAD_DIM
    assert half * 2 == ATT_HEAD_DIM
    return jnp.tile(jnp.concatenate([cos, cos], axis=-1), (1, reps)), jnp.tile(jnp.concatenate([-sin, sin], axis=-1), (1, reps))


def _state_to_kernel_layout(s):
    b = s.shape[0]
    s = s.reshape(b, SSD_GROUPS, SSD_REP, SSD_HEAD_DIM, SSD_STATE)
    return s.transpose(0, 1, 4, 2, 3).reshape(b, SSD_GROUPS, SSD_STATE, SSD_REP * SSD_HEAD_DIM)


def _state_from_kernel_layout(s):
    b = s.shape[0]
    s = s.reshape(b, SSD_GROUPS, SSD_STATE, SSD_REP, SSD_HEAD_DIM)
    return s.transpose(0, 1, 3, 4, 2).reshape(b, SSD_HEADS, SSD_HEAD_DIM, SSD_STATE)


def _layer(x, conv_buf, ssm0, k_past, v_past, mem_kb, mem_vb, lam_init, p, *, tm, tk, tq, ssd_q):
    nb, seq, _ = x.shape
    n = nb * seq
    past = 0 if k_past is None else k_past.shape[1]
    xf = x.reshape(n, D_MODEL)
    cos, sin = _rope_tables(past, seq)
    no_history = k_past is None
    assert tq == tk or not no_history
    z, xbc, tail, dt, q, k, v, qt, kb, vt = _in_proj(xf, p["g_pre_mix"], p["w_in"], cos, sin, seq=seq, tm=tm,
                                                      tt=tk if no_history else 0)

    seq_pad = -(-seq // ssd_q) * ssd_q
    if seq_pad != seq:
        pad = lambda a: jnp.pad(a.reshape(nb, seq, -1), ((0, 0), (0, seq_pad - seq), (0, 0))).reshape(nb * seq_pad, -1)
        z_s, xbc_s, dt_s = pad(z), pad(xbc), pad(dt)
    else:
        z_s, xbc_s, dt_s = z, xbc, dt
    cbuf = jnp.pad(conv_buf.astype(F32), ((0, 0), (8 - (SSD_CONV - 1), 0), (0, 0)))
    y_ssd, h_new = _ssd(z_s, xbc_s, dt_s, cbuf, _state_to_kernel_layout(ssm0.astype(F32)),
                        p["conv_w"], p["conv_b"], p["dt_bias"], p["a_log"], p["d_skip"], p["ssm_norm_w"], p["e2"],
                        nb=nb, seq=seq_pad, q=ssd_q, valid=None if seq_pad == seq else seq)
    if seq_pad != seq:
        y_ssd = y_ssd.reshape(nb, seq_pad, SSD_WIDTH)[:, :seq].reshape(n, SSD_WIDTH)
    ssm_new = _state_from_kernel_layout(h_new)
    ext_tail = tail.reshape(nb, seq // tm, 8, SSD_XBC)[:, -1]
    if seq >= SSD_CONV - 1:
        conv_new = ext_tail[:, 8 - (SSD_CONV - 1):]
    else:
        conv_new = jnp.concatenate([conv_buf.astype(F32), ext_tail[:, 8 - seq:]], axis=1)[:, -(SSD_CONV - 1):]

    if no_history:
        kv_len = seq
        y_att = _diff_attn(qt, kb, vt, p["lam_vecs"], p["subln_col"], nb=nb, tq=tq, tk=tk,
                           past=past, kv_len=kv_len, lam_init=lam_init)
    else:
        kv_len = past + seq
        kv_pad = -(-kv_len // tk) * tk
        k_all = jnp.concatenate([k_past.astype(BF16), k.reshape(nb, seq, ATT_WIDTH).astype(BF16)], axis=1)
        k_all = jnp.pad(k_all, ((0, 0), (0, kv_pad - kv_len), (0, 0))).reshape(-1, ATT_WIDTH)
        v_all = jnp.concatenate([v_past.astype(BF16), v.reshape(nb, seq, ATT_WIDTH).astype(BF16)], axis=1)
        v_all = jnp.pad(v_all, ((0, 0), (0, kv_pad - kv_len), (0, 0)))
        vt_all = v_all.reshape(nb, kv_pad // tk, tk, ATT_WIDTH).transpose(0, 1, 3, 2).reshape(-1, ATT_WIDTH, tk)
        assert seq <= tq
        q_pad = jnp.pad(q.reshape(nb, seq, ATT_WIDTH), ((0, 0), (0, tq - seq), (0, 0)))
        y_att = _diff_attn(q_pad.transpose(0, 2, 1), k_all, vt_all, p["lam_vecs"], p["subln_col"], nb=nb, tq=tq, tk=tk,
                           past=past, kv_len=kv_len, lam_init=lam_init)
        y_att = y_att.reshape(nb, tq, ATT_WIDTH)[:, :seq].reshape(n, ATT_WIDTH)

    h = _post_mix(xf, y_ssd, y_att, mem_kb, mem_vb, p["w_out"], p["wq_x"], p["wo_x"],
                  p["g_post_mix"], p["g_pre_x"], p["g_post_x"], seq=seq, tm=tm)
    out = _ffn(h, p["w_gate"], p["w_up"], p["w_down"], p["g_pre_ffn"], p["g_post_ffn"], tm=tm)
    if no_history:
        k_out = k.reshape(nb, ATT_HEADS, 2, ATT_HEAD_DIM, seq).transpose(0, 4, 1, 2, 3)
    else:
        k_out = k.reshape(nb, seq, ATT_HEADS, 2, ATT_HEAD_DIM)
    return (out.reshape(nb, seq, D_MODEL), k_out,
            v.reshape(nb, seq, ATT_HEADS, ATT_V_DIM), ssm_new, conv_new)


def _prep_params(i, w_in, conv_w, conv_b, dt_bias, a_log, d_skip, ssm_norm_w, lam_q1, lam_k1, lam_q2, lam_k2, subln_w,
                 w_out, wq_x, wo_x, g_pre_mix, g_post_mix, g_pre_x, g_post_x, g_pre_ffn, g_post_ffn,
                 w_gate, w_up, w_down):
    w = w_in[i]
    s0 = SSD_WIDTH + SSD_XBC
    s1 = s0 + SSD_HEADS
    w_r = jnp.concatenate([w[:, :s0], w[:, s1:], w[:, s0:s1], jnp.zeros((D_MODEL, DT_PAD - SSD_HEADS), w.dtype)], axis=1)
    head_pad = lambda a: jnp.pad(a[i].astype(F32), (0, DT_PAD - SSD_HEADS)).reshape(1, DT_PAD)
    e = (jnp.arange(DT_PAD)[:, None] == (jnp.arange(SSD_WIDTH)[None, :] // SSD_HEAD_DIM)).astype(BF16)
    row = lambda a: a[i].astype(F32).reshape(1, -1)
    return {
        "w_in": w_r.astype(BF16), "conv_w": conv_w[i].astype(F32), "conv_b": row(conv_b),
        "dt_bias": head_pad(dt_bias), "a_log": head_pad(a_log), "d_skip": head_pad(d_skip),
        "ssm_norm_w": row(ssm_norm_w), "e2": jnp.concatenate([e, e], axis=0),
        "lam_vecs": jnp.stack([lam_q1[i], lam_k1[i], lam_q2[i], lam_k2[i]]).astype(F32), "subln_col": subln_w[i].astype(F32).reshape(ATT_V_DIM, 1),
        "w_out": w_out[i].astype(BF16), "wq_x": wq_x[i].astype(BF16), "wo_x": wo_x[i].astype(BF16),
        "g_pre_mix": row(g_pre_mix), "g_post_mix": row(g_post_mix), "g_pre_x": row(g_pre_x), "g_post_x": row(g_post_x),
        "g_pre_ffn": row(g_pre_ffn), "g_post_ffn": row(g_post_ffn),
        "w_gate": w_gate[i].astype(BF16), "w_up": w_up[i].astype(BF16), "w_down": w_down[i].astype(BF16),
    }


def kernel(x_prompt, x_sample, cache_attn_k, cache_attn_v, state_ssm, state_conv, cache_mem_k, cache_mem_v, mem_prompt, w_in, conv_w, conv_b, dt_bias, a_log, d_skip, ssm_norm_w, lam_q1, lam_k1, lam_q2, lam_k2, subln_w, w_out, g_mem, wq_x, wk_x, wv_x, wo_x, g_pre_mix, g_post_mix, g_pre_x, g_post_x, g_pre_ffn, g_post_ffn, w_gate, w_up, w_down):
    depth = w_in.shape[0]
    bp, sp, _ = x_prompt.shape
    bs, ss, _ = x_sample.shape
    hp, hs = x_prompt, x_sample
    outs = [[] for _ in range(10)]
    for i in range(depth):
        lam_init = 0.8 - 0.6 * math.exp(-0.3 * i)
        p = _prep_params(i, w_in, conv_w, conv_b, dt_bias, a_log, d_skip, ssm_norm_w, lam_q1, lam_k1, lam_q2, lam_k2,
                         subln_w, w_out, wq_x, wo_x, g_pre_mix, g_post_mix, g_pre_x, g_post_x, g_pre_ffn, g_post_ffn,
                         w_gate, w_up, w_down)
        mk, mv, mkb, mvb = _mem_kv(mem_prompt.reshape(bp * MEM_LEN, D_MODEL), g_mem[i].reshape(1, D_MODEL),
                                   wk_x[i].astype(BF16), wv_x[i].astype(BF16))
        hp, k_new, v_new, ssm_new, conv_new = _layer(
            hp, jnp.zeros((bp, SSD_CONV - 1, SSD_XBC), F32), jnp.zeros((bp, SSD_HEADS, SSD_HEAD_DIM, SSD_STATE), F32),
            None, None, mkb.reshape(bp, MEM_LEN, D_MODEL), mvb.reshape(bp, MEM_LEN, D_MODEL), lam_init, p,
            tm=512, tk=512, tq=512, ssd_q=256)
        for lst, val in zip(outs[:6], (k_new, v_new, ssm_new, conv_new,
                                       mk.reshape(bp, MEM_LEN, MEM_HEADS, MEM_HEAD_DIM),
                                       mv.reshape(bp, MEM_LEN, MEM_HEADS, MEM_HEAD_DIM))):
            lst.append(val)
        past = cache_attn_k.shape[2]
        hs, k_new, v_new, ssm_new, conv_new = _layer(
            hs, state_conv[i], state_ssm[i], cache_attn_k[i].reshape(bs, past, ATT_WIDTH),
            cache_attn_v[i].reshape(bs, past, ATT_WIDTH),
            cache_mem_k[i].reshape(bs, MEM_LEN, D_MODEL).astype(BF16),
            cache_mem_v[i].reshape(bs, MEM_LEN, D_MODEL).astype(BF16), lam_init, p,
            tm=ss, tk=256, tq=128, ssd_q=128)
        for lst, val in zip(outs[6:], (k_new, v_new, ssm_new, conv_new)):
            lst.append(val)
    return (hp, hs) + tuple(jnp.stack(o) for o in outs)
```

```python
import functools
import math

import jax
import jax.numpy as jnp
from jax import lax
from jax.experimental import pallas as pl
from jax.experimental.pallas import tpu as pltpu

D_MODEL = 1024
CHUNK = 64
SSD_WIDTH = 512
SSD_HEAD_DIM = 64
SSD_HEADS = 8
SSD_GROUPS = 2
SSD_REP = 4
SSD_STATE = 128
SSD_CONV = 4
SSD_XBC = 1024
SSD_NORM_GROUP = 256
SSD_NORM_EPS = 1e-5
ATT_WIDTH = 512
ATT_HEAD_DIM = 64
ATT_HEADS = 4
ATT_V_DIM = 128
ATT_NORM_EPS = 1e-5
ROPE_THETA = 10000.0
MEM_LEN = 256
MEM_HEADS = 4
MEM_HEAD_DIM = 256
FFN_HIDDEN = 2816
NORM_EPS = 1e-6
LANES = 128
DT_PAD = LANES
IN_COLS_PADDED = SSD_WIDTH + SSD_XBC + 3 * ATT_WIDTH + DT_PAD
VMEM_LIMIT = 56 * 1024 * 1024
NEG_BIG = -1e30
Q_SCALE = math.log2(math.e) / math.sqrt(ATT_HEAD_DIM)

F32 = jnp.float32
BF16 = jnp.bfloat16


def _const_spec(shape):
    return pl.BlockSpec(shape, lambda *_: (0,) * len(shape), pipeline_mode=pl.Buffered(1))


def _rms(x, g, eps):
    return x * lax.rsqrt(jnp.mean(x * x, axis=-1, keepdims=True) + eps) * g


def _silu(x):
    return x / (1.0 + jnp.exp(-x))


def _dot(a, b):
    return jnp.dot(a, b, preferred_element_type=F32)


def _mem_kv_kernel(mem_ref, g_ref, wk_ref, wv_ref, mk_ref, mv_ref, mkb_ref, mvb_ref):
    mn = _rms(mem_ref[...], g_ref[...], NORM_EPS).astype(BF16)
    mk = _dot(mn, wk_ref[...])
    mv = _dot(mn, wv_ref[...])
    mk_ref[...] = mk
    mv_ref[...] = mv
    mkb_ref[...] = mk.astype(BF16)
    mvb_ref[...] = mv.astype(BF16)


def _mem_kv(mem, g_mem, wk, wv):
    n = mem.shape[0]
    tm = MEM_LEN
    row = pl.BlockSpec((tm, D_MODEL), lambda i: (i, 0))
    return pl.pallas_call(
        _mem_kv_kernel,
        grid=(n // tm,),
        in_specs=[row, _const_spec((1, D_MODEL)), _const_spec((D_MODEL, D_MODEL)), _const_spec((D_MODEL, D_MODEL))],
        out_specs=[row, row, row, row],
        out_shape=[jax.ShapeDtypeStruct((n, D_MODEL), F32), jax.ShapeDtypeStruct((n, D_MODEL), F32),
                   jax.ShapeDtypeStruct((n, D_MODEL), BF16), jax.ShapeDtypeStruct((n, D_MODEL), BF16)],
        compiler_params=pltpu.CompilerParams(dimension_semantics=("arbitrary",), vmem_limit_bytes=VMEM_LIMIT),
        name="mem_kv",
    )(mem, g_mem, wk, wv)


def _in_proj_kernel(x_ref, g_ref, w_ref, cos_ref, sin_ref,
                    z_ref, xbc_ref, tail_ref, dt_ref, q_ref, k_ref, v_ref, qt_ref, kb_ref, vt_ref, *, tm, tt, spt):
    hn = _rms(x_ref[...], g_ref[...], NORM_EPS).astype(BF16)
    acc = _dot(hn, w_ref[...])
    o = 0
    z_ref[...] = acc[:, o:o + SSD_WIDTH].astype(BF16)
    o += SSD_WIDTH
    xbc = acc[:, o:o + SSD_XBC]
    xbc_ref[...] = xbc.astype(BF16)
    rows = tm // spt
    for s_i in range(spt):
        tail_ref[s_i] = xbc[(s_i + 1) * rows - 8:(s_i + 1) * rows, :]
    o += SSD_XBC
    q = acc[:, o:o + ATT_WIDTH]
    o += ATT_WIDTH
    k = acc[:, o:o + ATT_WIDTH]
    o += ATT_WIDTH
    v = acc[:, o:o + ATT_WIDTH]
    o += ATT_WIDTH
    dt_ref[...] = acc[:, o:o + DT_PAD]
    v_ref[...] = v

    cos = cos_ref[...]
    sin = sin_ref[...]
    first_half = (lax.broadcasted_iota(jnp.int32, (tm, LANES), 1) % ATT_HEAD_DIM) < (ATT_HEAD_DIM // 2)

    def rope(t):
        swapped = jnp.where(first_half, pltpu.roll(t, LANES - ATT_HEAD_DIM // 2, 1),
                            pltpu.roll(t, ATT_HEAD_DIM // 2, 1))
        return t * cos + swapped * sin

    for j in range(ATT_WIDTH // LANES):
        sl = slice(j * LANES, (j + 1) * LANES)
        qr = rope(q[:, sl]) * Q_SCALE
        kr = rope(k[:, sl])
        q_ref[:, sl] = qr.astype(BF16)
        if tt:
            k_ref[0, sl, :] = kr.T
            kb_ref[:, sl] = kr.astype(BF16)
            for c in range(tm // tt):
                qt_ref[c, sl, :] = qr[c * tt:(c + 1) * tt, :].T.astype(BF16)
                vt_ref[c, sl, :] = v[c * tt:(c + 1) * tt, sl].T.astype(BF16)
        else:
            k_ref[:, sl] = kr


def _in_proj(x, g, w, cos, sin, *, seq, tm, tt):
    n = x.shape[0]
    nt = n // tm
    spt = max(1, tm // seq)
    tiles_per_seq = max(1, seq // tm)
    if spt > 1:
        cos, sin = jnp.tile(cos, (spt, 1)), jnp.tile(sin, (spt, 1))
    row = lambda c: pl.BlockSpec((tm, c), lambda i: (i, 0))
    tab = pl.BlockSpec((tm, LANES), lambda i: (i % tiles_per_seq, 0))
    if tt:
        k_spec = pl.BlockSpec((1, ATT_WIDTH, tm), lambda i: (i // tiles_per_seq, 0, i % tiles_per_seq))
        k_shape = jax.ShapeDtypeStruct((n // seq, ATT_WIDTH, seq), F32)
    else:
        k_spec, k_shape = row(ATT_WIDTH), jax.ShapeDtypeStruct((n, ATT_WIDTH), F32)
    out_specs = [row(SSD_WIDTH), row(SSD_XBC), pl.BlockSpec((spt, 8, SSD_XBC), lambda i: (i, 0, 0)), row(DT_PAD),
                 row(ATT_WIDTH), k_spec, row(ATT_WIDTH)]
    out_shape = [jax.ShapeDtypeStruct((n, SSD_WIDTH), BF16), jax.ShapeDtypeStruct((n, SSD_XBC), BF16),
                 jax.ShapeDtypeStruct((nt * spt, 8, SSD_XBC), F32), jax.ShapeDtypeStruct((n, DT_PAD), F32),
                 jax.ShapeDtypeStruct((n, ATT_WIDTH), BF16), k_shape,
                 jax.ShapeDtypeStruct((n, ATT_WIDTH), F32)]
    if tt:
        tr = pl.BlockSpec((tm // tt, ATT_WIDTH, tt), lambda i: (i, 0, 0))
        out_specs += [tr, row(ATT_WIDTH), tr]
        out_shape += [jax.ShapeDtypeStruct((n // tt, ATT_WIDTH, tt), BF16), jax.ShapeDtypeStruct((n, ATT_WIDTH), BF16),
                      jax.ShapeDtypeStruct((n // tt, ATT_WIDTH, tt), BF16)]

    def body(*refs):
        refs = refs + (None,) * (15 - len(refs))
        _in_proj_kernel(*refs, tm=tm, tt=tt, spt=spt)

    outs = pl.pallas_call(
        body,
        grid=(nt,),
        in_specs=[row(D_MODEL), _const_spec((1, D_MODEL)), _const_spec((D_MODEL, IN_COLS_PADDED)), tab, tab],
        out_specs=out_specs,
        out_shape=out_shape,
        compiler_params=pltpu.CompilerParams(dimension_semantics=("arbitrary",), vmem_limit_bytes=VMEM_LIMIT),
        name="in_proj",
    )(x, g, w, cos, sin)
    return list(outs) + [None] * (10 - len(outs))


def _expand_heads(x, e2):
    hi = x.astype(BF16)
    lo = (x - hi.astype(F32)).astype(BF16)
    return _dot(jnp.concatenate([hi, lo], axis=1), e2)


def _ssd_kernel(z_ref, xbc_ref, dt_ref, cbuf_ref, h0_ref, cw_ref, cb_ref, dtb_ref, alog_ref, dsk_ref, nw_ref, e2_ref,
                y_ref, hout_ref, state, ext, *, q, valid):
    c = pl.program_id(1)

    @pl.when(c == 0)
    def _():
        state[...] = h0_ref[0]
        ext[0:8, :] = cbuf_ref[0]

    ext[8:8 + q, :] = xbc_ref[...].astype(F32)
    conv = cb_ref[...] + ext[5:5 + q, :] * cw_ref[0:1, :]
    for w in range(1, SSD_CONV):
        conv = conv + ext[5 + w:5 + w + q, :] * cw_ref[w:w + 1, :]
    ext[0:8, :] = ext[q:q + 8, :]
    u = _silu(conv)
    xs = u[:, :SSD_WIDTH]
    bm = u[:, SSD_WIDTH:SSD_WIDTH + SSD_GROUPS * SSD_STATE]
    cm = u[:, SSD_WIDTH + SSD_GROUPS * SSD_STATE:]

    dtr = dt_ref[...] + dtb_ref[...]
    dt = jnp.maximum(dtr, 0.0) + jnp.log(1.0 + jnp.exp(-jnp.abs(dtr)))
    if valid is not None:
        row = lax.broadcasted_iota(jnp.int32, (q, DT_PAD), 0) + c * q
        dt = jnp.where(row < valid, dt, 0.0)
    a = -jnp.exp(alog_ref[...])
    ad = dt * a
    ri = lax.broadcasted_iota(jnp.int32, (q, q), 0)
    ci = lax.broadcasted_iota(jnp.int32, (q, q), 1)
    tril = ri >= ci
    acum = jnp.dot(tril.astype(F32), ad, preferred_element_type=F32, precision=lax.Precision.HIGHEST)
    acum_t = acum.T
    tot = acum[q - 1:q, :]
    e2 = e2_ref[...]
    expanded = _expand_heads(jnp.concatenate([dt, dt * jnp.exp(tot - acum), jnp.exp(acum)], axis=0), e2)
    dtx = expanded[0:q]
    ddx = expanded[q:2 * q]
    eax = expanded[2 * q:3 * q]
    dsk = _expand_heads(jnp.broadcast_to(dsk_ref[...], (8, DT_PAD)), e2)[0:1]

    xdt = (xs * dtx).astype(BF16)
    xdtd = (xs * ddx).astype(BF16)
    bm_t = bm.T.astype(BF16)
    cmb = cm.astype(BF16)
    gw = SSD_REP * SSD_HEAD_DIM
    stripe = lax.broadcasted_iota(jnp.int32, (q, gw), 1) // SSD_HEAD_DIM
    ys = []
    for g in range(SSD_GROUPS):
        cm_g = cmb[:, g * SSD_STATE:(g + 1) * SSD_STATE]
        bt_g = bm_t[g * SSD_STATE:(g + 1) * SSD_STATE, :]
        cbm = _dot(cm_g, bt_g)
        ms = []
        for r in range(SSD_REP):
            h = g * SSD_REP + r
            diff = acum[:, h:h + 1] - acum_t[h:h + 1, :]
            ms.append((cbm * jnp.exp(jnp.where(tril, diff, -jnp.inf))).astype(BF16))
        ydf = _dot(jnp.concatenate(ms, axis=0), xdt[:, g * gw:(g + 1) * gw])
        yd = ydf[0:q]
        for r in range(1, SSD_REP):
            yd = jnp.where(stripe == r, ydf[r * q:(r + 1) * q], yd)
        st = state[g]
        y_off = _dot(cm_g, st.astype(BF16)) * eax[:, g * gw:(g + 1) * gw]
        state[g] = st * eax[q - 1:q, g * gw:(g + 1) * gw] + _dot(bt_g, xdtd[:, g * gw:(g + 1) * gw])
        ys.append(yd + y_off)
    y = jnp.concatenate(ys, axis=1) + dsk * xs
    y = y * _silu(z_ref[...].astype(F32))
    outs = []
    for g in range(SSD_WIDTH // SSD_NORM_GROUP):
        yg = y[:, g * SSD_NORM_GROUP:(g + 1) * SSD_NORM_GROUP]
        outs.append(yg * lax.rsqrt(jnp.mean(yg * yg, axis=-1, keepdims=True) + SSD_NORM_EPS))
    y_ref[...] = (jnp.concatenate(outs, axis=1) * nw_ref[...]).astype(BF16)

    @pl.when(c == pl.num_programs(1) - 1)
    def _():
        hout_ref[0] = state[...]


def _ssd(z, xbc, dt, cbuf, h0, cw, cb, dtb, alog, dsk, nw, e2, *, nb, seq, q, valid):
    nc = seq // q
    row = lambda c: pl.BlockSpec((q, c), lambda b, i: (b * nc + i, 0))
    gw = SSD_REP * SSD_HEAD_DIM
    st_spec = pl.BlockSpec((1, SSD_GROUPS, SSD_STATE, gw), lambda b, i: (b, 0, 0, 0))
    return pl.pallas_call(
        functools.partial(_ssd_kernel, q=q, valid=valid),
        grid=(nb, nc),
        in_specs=[row(SSD_WIDTH), row(SSD_XBC), row(DT_PAD),
                  pl.BlockSpec((1, 8, SSD_XBC), lambda b, i: (b, 0, 0)), st_spec,
                  _const_spec((SSD_CONV, SSD_XBC)), _const_spec((1, SSD_XBC)), _const_spec((1, DT_PAD)),
                  _const_spec((1, DT_PAD)), _const_spec((1, DT_PAD)), _const_spec((1, SSD_WIDTH)),
                  _const_spec((2 * DT_PAD, SSD_WIDTH))],
        out_specs=[row(SSD_WIDTH), st_spec],
        out_shape=[jax.ShapeDtypeStruct((nb * seq, SSD_WIDTH), BF16),
                   jax.ShapeDtypeStruct((nb, SSD_GROUPS, SSD_STATE, gw), F32)],
        scratch_shapes=[pltpu.VMEM((SSD_GROUPS, SSD_STATE, gw), F32), pltpu.VMEM((q + 8, SSD_XBC), F32)],
        compiler_params=pltpu.CompilerParams(dimension_semantics=("arbitrary", "arbitrary"),
                                             vmem_limit_bytes=VMEM_LIMIT),
        name="ssd",
    )(z, xbc, dt, cbuf, h0, cw, cb, dtb, alog, dsk, nw, e2)


def _attn_tile_counts(i, *, tq, tk, past, kv_len, minimum=min):
    q_lo = past + i * tq
    q_hi = q_lo + tq - 1
    lim_lo = minimum((q_lo // CHUNK + 1) * CHUNK, kv_len)
    lim_hi = minimum((q_hi // CHUNK + 1) * CHUNK, kv_len)
    return lim_lo // tk, (lim_hi + tk - 1) // tk


def _diff_attn_kernel(qt_ref, k_ref, vt_ref, lam_ref, sw_ref, o_ref,
                      q2t_ref, s_ref, p_ref, m_ref, l_ref, alpha_ref, acc_ref, *, tq, tk, past, kv_len, lam_init):
    i = pl.program_id(2)
    w = 2 * tq
    qt = qt_ref[0]
    row = lax.broadcasted_iota(jnp.int32, (ATT_V_DIM, tq), 0)
    zero = jnp.zeros_like(qt)
    q2t_ref[:, 0:tq] = jnp.where(row < ATT_HEAD_DIM, qt, zero)
    q2t_ref[:, tq:w] = jnp.where(row >= ATT_HEAD_DIM, qt, zero)

    q_lo = past + i * tq
    n_full, _ = _attn_tile_counts(i, tq=tq, tk=tk, past=past, kv_len=kv_len, minimum=jnp.minimum)

    def scores(j):
        kt = k_ref[pl.ds(pl.multiple_of(j * tk, tk), tk), :]
        return _dot(kt, q2t_ref[...])

    qpos = q_lo + lax.broadcasted_iota(jnp.int32, (tk, w), 1) % tq
    kpos = n_full * tk + lax.broadcasted_iota(jnp.int32, (tk, w), 0)
    s_ref[...] = jnp.where((kpos // CHUNK <= qpos // CHUNK) & (kpos < kv_len), scores(n_full), NEG_BIG)
    m_ref[...] = jnp.full_like(m_ref, NEG_BIG)
    l_ref[...] = jnp.zeros_like(l_ref)
    acc_ref[...] = jnp.zeros_like(acc_ref)
    p_ref[...] = jnp.zeros_like(p_ref)
    alpha_ref[...] = jnp.ones_like(alpha_ref)

    def visited_tile(t):
        return jnp.where(t == 0, n_full, jnp.maximum(t - 1, 0))

    def accumulate(t):
        acc_ref[...] = alpha_ref[...] * acc_ref[...] + _dot(vt_ref[visited_tile(t)], p_ref[...])

    def trip(t, carry):
        accumulate(t - 1)
        for cb in range(w // LANES):
            sl = slice(cb * LANES, (cb + 1) * LANES)
            s = s_ref[:, sl]
            m_old = m_ref[:, sl]
            m_new = jnp.maximum(m_old, jnp.max(s, axis=0, keepdims=True))
            alpha = jnp.exp2(m_old - m_new)
            p = jnp.exp2(s - m_new)
            l_ref[:, sl] = alpha * l_ref[:, sl] + jnp.sum(p, axis=0, keepdims=True)
            m_ref[:, sl] = m_new
            alpha_ref[:, sl] = alpha
            p_ref[:, sl] = p.astype(BF16)
        s_ref[...] = scores(t)
        return carry

    lax.fori_loop(0, n_full + 1, trip, 0)
    accumulate(n_full)

    lam_v = lam_ref[...]
    lam = (jnp.exp(jnp.sum(lam_v[0:1] * lam_v[1:2], axis=-1, keepdims=True))
           - jnp.exp(jnp.sum(lam_v[2:3] * lam_v[3:4], axis=-1, keepdims=True)) + lam_init)
    o = acc_ref[...] / l_ref[...]
    o = o[:, :tq] - lam * o[:, tq:]
    o = o * lax.rsqrt(jnp.mean(o * o, axis=0, keepdims=True) + ATT_NORM_EPS) * (sw_ref[...] * (1.0 - lam_init))
    o_ref[...] = o.T.astype(BF16)


def _diff_attn(qt, k, vt, lam_vecs, subln_col, *, nb, tq, tk, past, kv_len, lam_init):
    nq = qt.shape[0] // nb
    nkt = vt.shape[0] // nb
    for i in range(nq):
        n_full, n_end = _attn_tile_counts(i, tq=tq, tk=tk, past=past, kv_len=kv_len)
        assert n_end - n_full == 1 and n_end <= nkt, (i, n_full, n_end)
    w = 2 * tq
    return pl.pallas_call(
        functools.partial(_diff_attn_kernel, tq=tq, tk=tk, past=past, kv_len=kv_len, lam_init=lam_init),
        grid=(nb, ATT_HEADS, nq),
        in_specs=[pl.BlockSpec((1, ATT_V_DIM, tq), lambda b, h, i: (b * nq + i, h, 0)),
                  pl.BlockSpec((nkt * tk, ATT_V_DIM), lambda b, h, i: (b, h)),
                  pl.BlockSpec((nkt, ATT_V_DIM, tk), lambda b, h, i: (b, h, 0)),
                  _const_spec((4, ATT_HEAD_DIM)), _const_spec((ATT_V_DIM, 1))],
        out_specs=pl.BlockSpec((tq, ATT_V_DIM), lambda b, h, i: (b * nq + i, h)),
        out_shape=jax.ShapeDtypeStruct((nb * nq * tq, ATT_WIDTH), BF16),
        scratch_shapes=[pltpu.VMEM((ATT_V_DIM, w), BF16), pltpu.VMEM((tk, w), F32), pltpu.VMEM((tk, w), BF16),
                        pltpu.VMEM((1, w), F32), pltpu.VMEM((1, w), F32), pltpu.VMEM((1, w), F32),
                        pltpu.VMEM((ATT_V_DIM, w), F32)],
        compiler_params=pltpu.CompilerParams(dimension_semantics=("arbitrary", "arbitrary", "arbitrary"),
                                             vmem_limit_bytes=VMEM_LIMIT),
        name="diff_attn",
    )(qt, k, vt, lam_vecs, subln_col)


def _decode_attn_kernel(q_ref, kn_ref, vn_ref, kc_ref, vc_ref, lam_ref, sw_ref, o_ref, *, seq, past, lam_init):
    lam_v = lam_ref[...]
    lam = (jnp.exp(jnp.sum(lam_v[0:1] * lam_v[1:2], axis=-1, keepdims=True))
           - jnp.exp(jnp.sum(lam_v[2:3] * lam_v[3:4], axis=-1, keepdims=True)) + lam_init)
    npad = LANES
    lane = lax.broadcasted_iota(jnp.int32, (seq, LANES), 1)
    qchunk = (past + lax.broadcasted_iota(jnp.int32, (2 * seq, 1), 0) % seq) // CHUNK
    kpos_p = lax.broadcasted_iota(jnp.int32, (2 * seq, past), 1)
    kpos_n = lax.broadcasted_iota(jnp.int32, (2 * seq, npad), 1)
    vis_p = kpos_p // CHUNK <= qchunk
    vis_n = ((past + kpos_n) // CHUNK <= qchunk) & (kpos_n < seq)
    pad_rows = jnp.zeros((npad - seq, LANES), BF16)
    for h in range(ATT_HEADS):
        sl = slice(h * LANES, (h + 1) * LANES)
        qh = q_ref[:, sl]
        zero = jnp.zeros_like(qh)
        q2 = jnp.concatenate([jnp.where(lane < ATT_HEAD_DIM, qh, zero), jnp.where(lane >= ATT_HEAD_DIM, qh, zero)], axis=0)
        s_p = jnp.where(vis_p, _dot(q2, kc_ref[0, sl, :].astype(BF16)), NEG_BIG)
        kn = jnp.concatenate([kn_ref[:, sl].astype(BF16), pad_rows], axis=0)
        s_n = lax.dot_general(q2, kn, (((1,), (1,)), ((), ())), preferred_element_type=F32)
        s_n = jnp.where(vis_n, s_n, NEG_BIG)
        m = jnp.maximum(jnp.max(s_p, axis=-1, keepdims=True), jnp.max(s_n, axis=-1, keepdims=True))
        p_p = jnp.exp2(s_p - m)
        p_n = jnp.exp2(s_n - m)
        l = jnp.sum(p_p, axis=-1, keepdims=True) + jnp.sum(p_n, axis=-1, keepdims=True)
        vh = vc_ref[0, pl.ds(h, past, stride=ATT_HEADS), :].astype(BF16)
        vn = jnp.concatenate([vn_ref[:, sl].astype(BF16), pad_rows], axis=0)
        o = (_dot(p_p.astype(BF16), vh) + _dot(p_n.astype(BF16), vn)) / l
        o = o[:seq] - lam * o[seq:]
        o = o * lax.rsqrt(jnp.mean(o * o, axis=-1, keepdims=True) + ATT_NORM_EPS) * (sw_ref[...] * (1.0 - lam_init))
        o_ref[:, sl] = o.astype(BF16)


def _decode_attn(q, k_new, v_new, kt_cache, v_cache, lam_vecs, subln_row, *, nb, seq, past, lam_init):
    row = pl.BlockSpec((seq, ATT_WIDTH), lambda b: (b, 0))
    return pl.pallas_call(
        functools.partial(_decode_attn_kernel, seq=seq, past=past, lam_init=lam_init),
        grid=(nb,),
        in_specs=[row, row, row,
                  pl.BlockSpec((1, ATT_WIDTH, past), lambda b: (b, 0, 0)),
                  pl.BlockSpec((1, past * ATT_HEADS, ATT_V_DIM), lambda b: (b, 0, 0)),
                  _const_spec((4, ATT_HEAD_DIM)), _const_spec((1, ATT_V_DIM))],
        out_specs=row,
        out_shape=jax.ShapeDtypeStruct((nb * seq, ATT_WIDTH), BF16),
        compiler_params=pltpu.CompilerParams(dimension_semantics=("arbitrary",), vmem_limit_bytes=VMEM_LIMIT),
        name="decode_attn",
    )(q, k_new, v_new, kt_cache, v_cache, lam_vecs, subln_row)


def _post_mix_kernel(x_ref, ys_ref, ya_ref, mk_ref, mv_ref, wo_ref, wq_ref, wox_ref,
                     g1_ref, g2_ref, g3_ref, h_ref, *, spt, rows):
    mix = _dot(ys_ref[...], wo_ref[0:SSD_WIDTH, :]) + _dot(ya_ref[...], wo_ref[SSD_WIDTH:, :])
    h = x_ref[...] + _rms(mix, g1_ref[...], NORM_EPS)
    qn = _rms(h, g2_ref[...], NORM_EPS).astype(BF16)
    qx = (_dot(qn, wq_ref[...]) * (1.0 / math.sqrt(MEM_HEAD_DIM))).astype(BF16)
    ox_seqs = []
    for s_i in range(spt):
        qs = qx[s_i * rows:(s_i + 1) * rows]
        mk = mk_ref[s_i]
        mv = mv_ref[s_i]
        oxs = []
        for hd in range(MEM_HEADS):
            sl = slice(hd * MEM_HEAD_DIM, (hd + 1) * MEM_HEAD_DIM)
            s = lax.dot_general(qs[:, sl], mk[:, sl], (((1,), (1,)), ((), ())), preferred_element_type=F32)
            p = jnp.exp(s - jnp.max(s, axis=-1, keepdims=True))
            ox = _dot(p.astype(BF16), mv[:, sl]) / jnp.sum(p, axis=-1, keepdims=True)
            oxs.append(ox.astype(BF16))
        ox_seqs.append(jnp.concatenate(oxs, axis=1))
    ox_all = ox_seqs[0] if spt == 1 else jnp.concatenate(ox_seqs, axis=0)
    o2 = _dot(ox_all, wox_ref[...])
    h_ref[...] = h + _rms(o2, g3_ref[...], NORM_EPS)


def _post_mix(x, ys, ya, mk, mv, w_out, wq, wox, g1, g2, g3, *, seq, tm):
    n = x.shape[0]
    spt = max(1, tm // seq)
    tiles_per_seq = max(1, seq // tm)
    row = lambda c: pl.BlockSpec((tm, c), lambda i: (i, 0))
    mem = pl.BlockSpec((spt, MEM_LEN, D_MODEL), lambda i: (i // tiles_per_seq, 0, 0))
    wspec = _const_spec((D_MODEL, D_MODEL))
    gspec = _const_spec((1, D_MODEL))
    return pl.pallas_call(
        functools.partial(_post_mix_kernel, spt=spt, rows=tm // spt),
        grid=(n // tm,),
        in_specs=[row(D_MODEL), row(SSD_WIDTH), row(ATT_WIDTH), mem, mem, wspec, wspec, wspec, gspec, gspec, gspec],
        out_specs=row(D_MODEL),
        out_shape=jax.ShapeDtypeStruct((n, D_MODEL), F32),
        compiler_params=pltpu.CompilerParams(dimension_semantics=("arbitrary",), vmem_limit_bytes=VMEM_LIMIT),
        name="post_mix",
    )(x, ys, ya, mk, mv, w_out, wq, wox, g1, g2, g3)


def _ffn_kernel(h_ref, wg_ref, wu_ref, wd_ref, g1_ref, g2_ref, o_ref):
    h = h_ref[...]
    hn = _rms(h, g1_ref[...], NORM_EPS).astype(BF16)
    act = (_silu(_dot(hn, wg_ref[...])) * _dot(hn, wu_ref[...])).astype(BF16)
    f = _dot(act, wd_ref[...])
    o_ref[...] = h + _rms(f, g2_ref[...], NORM_EPS)


def _ffn(h, wg, wu, wd, g1, g2, *, tm):
    n = h.shape[0]
    row = pl.BlockSpec((tm, D_MODEL), lambda i: (i, 0))
    return pl.pallas_call(
        _ffn_kernel,
        grid=(n // tm,),
        in_specs=[row, _const_spec((D_MODEL, FFN_HIDDEN)), _const_spec((D_MODEL, FFN_HIDDEN)),
                  _const_spec((FFN_HIDDEN, D_MODEL)), _const_spec((1, D_MODEL)), _const_spec((1, D_MODEL))],
        out_specs=row,
        out_shape=jax.ShapeDtypeStruct((n, D_MODEL), F32),
        compiler_params=pltpu.CompilerParams(dimension_semantics=("arbitrary",), vmem_limit_bytes=VMEM_LIMIT),
        name="ffn",
    )(h, wg, wu, wd, g1, g2)


def _rope_tables(past, seq):
    half = ATT_HEAD_DIM // 2
    inv = jnp.power(ROPE_THETA, -jnp.arange(0, ATT_HEAD_DIM, 2, dtype=F32) / ATT_HEAD_DIM)
    pos = (past + jnp.arange(seq, dtype=jnp.int32)).astype(F32)
    ang = pos[:, None] * inv[None, :]
    cos, sin = jnp.cos(ang), jnp.sin(ang)
    reps = LANES // ATT_HEAD_DIM
    assert half * 2 == ATT_HEAD_DIM
    return jnp.tile(jnp.concatenate([cos, cos], axis=-1), (1, reps)), jnp.tile(jnp.concatenate([-sin, sin], axis=-1), (1, reps))


def _state_to_kernel_layout(s):
    b = s.shape[0]
    s = s.reshape(b, SSD_GROUPS, SSD_REP, SSD_HEAD_DIM, SSD_STATE)
    return s.transpose(0, 1, 4, 2, 3).reshape(b, SSD_GROUPS, SSD_STATE, SSD_REP * SSD_HEAD_DIM)


def _state_from_kernel_layout(s):
    b = s.shape[0]
    s = s.reshape(b, SSD_GROUPS, SSD_STATE, SSD_REP, SSD_HEAD_DIM)
    return s.transpose(0, 1, 3, 4, 2).reshape(b, SSD_HEADS, SSD_HEAD_DIM, SSD_STATE)


def _layer(x, conv_buf, ssm0, kt_past, v_past, mem_kb, mem_vb, lam_init, p, *, tm, tk, tq, ssd_q):
    nb, seq, _ = x.shape
    n = nb * seq
    past = 0 if kt_past is None else kt_past.shape[2]
    xf = x.reshape(n, D_MODEL)
    cos, sin = _rope_tables(past, seq)
    no_history = kt_past is None
    assert tq == tk or not no_history
    z, xbc, tail, dt, q, k, v, qt, kb, vt = _in_proj(xf, p["g_pre_mix"], p["w_in"], cos, sin, seq=seq, tm=tm,
                                                      tt=tk if no_history else 0)

    seq_pad = -(-seq // ssd_q) * ssd_q
    if seq_pad != seq:
        pad = lambda a: jnp.pad(a.reshape(nb, seq, -1), ((0, 0), (0, seq_pad - seq), (0, 0))).reshape(nb * seq_pad, -1)
        z_s, xbc_s, dt_s = pad(z), pad(xbc), pad(dt)
    else:
        z_s, xbc_s, dt_s = z, xbc, dt
    cbuf = jnp.pad(conv_buf.astype(F32), ((0, 0), (8 - (SSD_CONV - 1), 0), (0, 0)))
    y_ssd, h_new = _ssd(z_s, xbc_s, dt_s, cbuf, _state_to_kernel_layout(ssm0.astype(F32)),
                        p["conv_w"], p["conv_b"], p["dt_bias"], p["a_log"], p["d_skip"], p["ssm_norm_w"], p["e2"],
                        nb=nb, seq=seq_pad, q=ssd_q, valid=None if seq_pad == seq else seq)
    if seq_pad != seq:
        y_ssd = y_ssd.reshape(nb, seq_pad, SSD_WIDTH)[:, :seq].reshape(n, SSD_WIDTH)
    ssm_new = _state_from_kernel_layout(h_new)
    ext_tail = tail.reshape(nb, -1, 8, SSD_XBC)[:, -1]
    if seq >= SSD_CONV - 1:
        conv_new = ext_tail[:, 8 - (SSD_CONV - 1):]
    else:
        conv_new = jnp.concatenate([conv_buf.astype(F32), ext_tail[:, 8 - seq:]], axis=1)[:, -(SSD_CONV - 1):]

    if no_history:
        kv_len = seq
        y_att = _diff_attn(qt, kb, vt, p["lam_vecs"], p["subln_col"], nb=nb, tq=tq, tk=tk,
                           past=past, kv_len=kv_len, lam_init=lam_init)
    else:
        y_att = _decode_attn(q, k, v, kt_past, v_past, p["lam_vecs"], p["subln_col"].reshape(1, ATT_V_DIM),
                             nb=nb, seq=seq, past=past, lam_init=lam_init)

    h = _post_mix(xf, y_ssd, y_att, mem_kb, mem_vb, p["w_out"], p["wq_x"], p["wo_x"],
                  p["g_post_mix"], p["g_pre_x"], p["g_post_x"], seq=seq, tm=tm)
    out = _ffn(h, p["w_gate"], p["w_up"], p["w_down"], p["g_pre_ffn"], p["g_post_ffn"], tm=tm)
    if no_history:
        k_out = k.reshape(nb, ATT_HEADS, 2, ATT_HEAD_DIM, seq).transpose(0, 4, 1, 2, 3)
    else:
        k_out = k.reshape(nb, seq, ATT_HEADS, 2, ATT_HEAD_DIM)
    return (out.reshape(nb, seq, D_MODEL), k_out,
            v.reshape(nb, seq, ATT_HEADS, ATT_V_DIM), ssm_new, conv_new)


def _prep_params(i, w_in, conv_w, conv_b, dt_bias, a_log, d_skip, ssm_norm_w, lam_q1, lam_k1, lam_q2, lam_k2, subln_w,
                 w_out, wq_x, wo_x, g_pre_mix, g_post_mix, g_pre_x, g_post_x, g_pre_ffn, g_post_ffn,
                 w_gate, w_up, w_down):
    w = w_in[i]
    s0 = SSD_WIDTH + SSD_XBC
    s1 = s0 + SSD_HEADS
    w_r = jnp.concatenate([w[:, :s0], w[:, s1:], w[:, s0:s1], jnp.zeros((D_MODEL, DT_PAD - SSD_HEADS), w.dtype)], axis=1)
    head_pad = lambda a: jnp.pad(a[i].astype(F32), (0, DT_PAD - SSD_HEADS)).reshape(1, DT_PAD)
    e = (jnp.arange(DT_PAD)[:, None] == (jnp.arange(SSD_WIDTH)[None, :] // SSD_HEAD_DIM)).astype(BF16)
    row = lambda a: a[i].astype(F32).reshape(1, -1)
    return {
        "w_in": w_r.astype(BF16), "conv_w": conv_w[i].astype(F32), "conv_b": row(conv_b),
        "dt_bias": head_pad(dt_bias), "a_log": head_pad(a_log), "d_skip": head_pad(d_skip),
        "ssm_norm_w": row(ssm_norm_w), "e2": jnp.concatenate([e, e], axis=0),
        "lam_vecs": jnp.stack([lam_q1[i], lam_k1[i], lam_q2[i], lam_k2[i]]).astype(F32), "subln_col": subln_w[i].astype(F32).reshape(ATT_V_DIM, 1),
        "w_out": w_out[i].astype(BF16), "wq_x": wq_x[i].astype(BF16), "wo_x": wo_x[i].astype(BF16),
        "g_pre_mix": row(g_pre_mix), "g_post_mix": row(g_post_mix), "g_pre_x": row(g_pre_x), "g_post_x": row(g_post_x),
        "g_pre_ffn": row(g_pre_ffn), "g_post_ffn": row(g_post_ffn),
        "w_gate": w_gate[i].astype(BF16), "w_up": w_up[i].astype(BF16), "w_down": w_down[i].astype(BF16),
    }


def kernel(x_prompt, x_sample, cache_attn_k, cache_attn_v, state_ssm, state_conv, cache_mem_k, cache_mem_v, mem_prompt, w_in, conv_w, conv_b, dt_bias, a_log, d_skip, ssm_norm_w, lam_q1, lam_k1, lam_q2, lam_k2, subln_w, w_out, g_mem, wq_x, wk_x, wv_x, wo_x, g_pre_mix, g_post_mix, g_pre_x, g_post_x, g_pre_ffn, g_post_ffn, w_gate, w_up, w_down):
    depth = w_in.shape[0]
    bp, sp, _ = x_prompt.shape
    bs, ss, _ = x_sample.shape
    hp, hs = x_prompt, x_sample
    outs = [[] for _ in range(10)]
    for i in range(depth):
        lam_init = 0.8 - 0.6 * math.exp(-0.3 * i)
        p = _prep_params(i, w_in, conv_w, conv_b, dt_bias, a_log, d_skip, ssm_norm_w, lam_q1, lam_k1, lam_q2, lam_k2,
                         subln_w, w_out, wq_x, wo_x, g_pre_mix, g_post_mix, g_pre_x, g_post_x, g_pre_ffn, g_post_ffn,
                         w_gate, w_up, w_down)
        mk, mv, mkb, mvb = _mem_kv(mem_prompt.reshape(bp * MEM_LEN, D_MODEL), g_mem[i].reshape(1, D_MODEL),
                                   wk_x[i].astype(BF16), wv_x[i].astype(BF16))
        hp, k_new, v_new, ssm_new, conv_new = _layer(
            hp, jnp.zeros((bp, SSD_CONV - 1, SSD_XBC), F32), jnp.zeros((bp, SSD_HEADS, SSD_HEAD_DIM, SSD_STATE), F32),
            None, None, mkb.reshape(bp, MEM_LEN, D_MODEL), mvb.reshape(bp, MEM_LEN, D_MODEL), lam_init, p,
            tm=512, tk=512, tq=512, ssd_q=256)
        for lst, val in zip(outs[:6], (k_new, v_new, ssm_new, conv_new,
                                       mk.reshape(bp, MEM_LEN, MEM_HEADS, MEM_HEAD_DIM),
                                       mv.reshape(bp, MEM_LEN, MEM_HEADS, MEM_HEAD_DIM))):
            lst.append(val)
        past = cache_attn_k.shape[2]
        hs, k_new, v_new, ssm_new, conv_new = _layer(
            hs, state_conv[i], state_ssm[i],
            cache_attn_k[i].transpose(0, 2, 3, 4, 1).reshape(bs, ATT_WIDTH, past),
            cache_attn_v[i].reshape(bs, past * ATT_HEADS, ATT_V_DIM),
            cache_mem_k[i].reshape(bs, MEM_LEN, D_MODEL).astype(BF16),
            cache_mem_v[i].reshape(bs, MEM_LEN, D_MODEL).astype(BF16), lam_init, p,
            tm=bs * ss, tk=0, tq=0, ssd_q=128)
        for lst, val in zip(outs[6:], (k_new, v_new, ssm_new, conv_new)):
            lst.append(val)
    return (hp, hs) + tuple(jnp.stack(o) for o in outs)
```

```python
import functools
import math

import jax
import jax.numpy as jnp
from jax import lax
from jax.experimental import pallas as pl
from jax.experimental.pallas import tpu as pltpu

D_MODEL = 1024
CHUNK = 64
SSD_WIDTH = 512
SSD_HEAD_DIM = 64
SSD_HEADS = 8
SSD_GROUPS = 2
SSD_REP = 4
SSD_STATE = 128
SSD_CONV = 4
SSD_XBC = 1024
SSD_NORM_GROUP = 256
SSD_NORM_EPS = 1e-5
ATT_WIDTH = 512
ATT_HEAD_DIM = 64
ATT_HEADS = 4
ATT_V_DIM = 128
ATT_NORM_EPS = 1e-5
ROPE_THETA = 10000.0
MEM_LEN = 256
MEM_HEADS = 4
MEM_HEAD_DIM = 256
FFN_HIDDEN = 2816
NORM_EPS = 1e-6
LANES = 128
DT_PAD = LANES
IN_COLS_PADDED = SSD_WIDTH + SSD_XBC + 3 * ATT_WIDTH + DT_PAD
VMEM_LIMIT = 56 * 1024 * 1024
NEG_BIG = -1e30
BF16_SUBLANES = 16
ATT_VT_ROWS = ATT_V_DIM + BF16_SUBLANES
ATT_SOFTMAX_ROWS = 128
ATT_PIPELINE_BUFFERS = 2
Q_SCALE = math.log2(math.e) / math.sqrt(ATT_HEAD_DIM)

F32 = jnp.float32
BF16 = jnp.bfloat16


def _const_spec(shape):
    return pl.BlockSpec(shape, lambda *_: (0,) * len(shape), pipeline_mode=pl.Buffered(1))


def _rms(x, g, eps):
    return x * lax.rsqrt(jnp.mean(x * x, axis=-1, keepdims=True) + eps) * g


def _silu(x):
    return x / (1.0 + jnp.exp(-x))


def _dot(a, b):
    return jnp.dot(a, b, preferred_element_type=F32)


def _mem_kv_kernel(mem_ref, g_ref, wk_ref, wv_ref, mk_ref, mv_ref, mkb_ref, mvb_ref):
    mn = _rms(mem_ref[...], g_ref[...], NORM_EPS).astype(BF16)
    mk = _dot(mn, wk_ref[...])
    mv = _dot(mn, wv_ref[...])
    mk_ref[...] = mk
    mv_ref[...] = mv
    mkb_ref[...] = mk.astype(BF16)
    mvb_ref[...] = mv.astype(BF16)


def _mem_kv(mem, g_mem, wk, wv):
    n = mem.shape[0]
    tm = MEM_LEN
    row = pl.BlockSpec((tm, D_MODEL), lambda i: (i, 0))
    return pl.pallas_call(
        _mem_kv_kernel,
        grid=(n // tm,),
        in_specs=[row, _const_spec((1, D_MODEL)), _const_spec((D_MODEL, D_MODEL)), _const_spec((D_MODEL, D_MODEL))],
        out_specs=[row, row, row, row],
        out_shape=[jax.ShapeDtypeStruct((n, D_MODEL), F32), jax.ShapeDtypeStruct((n, D_MODEL), F32),
                   jax.ShapeDtypeStruct((n, D_MODEL), BF16), jax.ShapeDtypeStruct((n, D_MODEL), BF16)],
        compiler_params=pltpu.CompilerParams(dimension_semantics=("arbitrary",), vmem_limit_bytes=VMEM_LIMIT),
        name="mem_kv",
    )(mem, g_mem, wk, wv)


def _in_proj_kernel(x_ref, g_ref, w_ref, cos_ref, sin_ref,
                    z_ref, xbc_ref, tail_ref, dt_ref, q_ref, k_ref, v_ref, qt_ref, kb_ref, vt_ref, *, tm, tt, spt):
    hn = _rms(x_ref[...], g_ref[...], NORM_EPS).astype(BF16)
    acc = _dot(hn, w_ref[...])
    o = 0
    z_ref[...] = acc[:, o:o + SSD_WIDTH].astype(BF16)
    o += SSD_WIDTH
    xbc = acc[:, o:o + SSD_XBC]
    xbc_ref[...] = xbc.astype(BF16)
    rows = tm // spt
    for s_i in range(spt):
        tail_ref[s_i] = xbc[(s_i + 1) * rows - 8:(s_i + 1) * rows, :]
    o += SSD_XBC
    q = acc[:, o:o + ATT_WIDTH]
    o += ATT_WIDTH
    k = acc[:, o:o + ATT_WIDTH]
    o += ATT_WIDTH
    v = acc[:, o:o + ATT_WIDTH]
    o += ATT_WIDTH
    dt_ref[...] = acc[:, o:o + DT_PAD]
    v_ref[...] = v

    cos = cos_ref[...]
    sin = sin_ref[...]
    first_half = (lax.broadcasted_iota(jnp.int32, (tm, LANES), 1) % ATT_HEAD_DIM) < (ATT_HEAD_DIM // 2)

    def rope(t):
        swapped = jnp.where(first_half, pltpu.roll(t, LANES - ATT_HEAD_DIM // 2, 1),
                            pltpu.roll(t, ATT_HEAD_DIM // 2, 1))
        return t * cos + swapped * sin

    for j in range(ATT_WIDTH // LANES):
        sl = slice(j * LANES, (j + 1) * LANES)
        qr = rope(q[:, sl]) * Q_SCALE
        kr = rope(k[:, sl])
        q_ref[:, sl] = qr.astype(BF16)
        if tt:
            k_ref[0, sl, :] = kr.T
            kb_ref[:, sl] = kr.astype(BF16)
            for c in range(tm // tt):
                qt_ref[c, sl, :] = qr[c * tt:(c + 1) * tt, :].T.astype(BF16)
                vt_ref[c, j * ATT_VT_ROWS:j * ATT_VT_ROWS + LANES, :] = v[c * tt:(c + 1) * tt, sl].T.astype(BF16)
                vt_ref[c, j * ATT_VT_ROWS + LANES:(j + 1) * ATT_VT_ROWS, :] = jnp.ones((ATT_VT_ROWS - LANES, tt), BF16)
        else:
            k_ref[:, sl] = kr


def _in_proj(x, g, w, cos, sin, *, seq, tm, tt):
    n = x.shape[0]
    nt = n // tm
    spt = max(1, tm // seq)
    tiles_per_seq = max(1, seq // tm)
    if spt > 1:
        cos, sin = jnp.tile(cos, (spt, 1)), jnp.tile(sin, (spt, 1))
    row = lambda c: pl.BlockSpec((tm, c), lambda i: (i, 0))
    tab = pl.BlockSpec((tm, LANES), lambda i: (i % tiles_per_seq, 0))
    if tt:
        k_spec = pl.BlockSpec((1, ATT_WIDTH, tm), lambda i: (i // tiles_per_seq, 0, i % tiles_per_seq))
        k_shape = jax.ShapeDtypeStruct((n // seq, ATT_WIDTH, seq), F32)
    else:
        k_spec, k_shape = row(ATT_WIDTH), jax.ShapeDtypeStruct((n, ATT_WIDTH), F32)
    out_specs = [row(SSD_WIDTH), row(SSD_XBC), pl.BlockSpec((spt, 8, SSD_XBC), lambda i: (i, 0, 0)), row(DT_PAD),
                 row(ATT_WIDTH), k_spec, row(ATT_WIDTH)]
    out_shape = [jax.ShapeDtypeStruct((n, SSD_WIDTH), BF16), jax.ShapeDtypeStruct((n, SSD_XBC), BF16),
                 jax.ShapeDtypeStruct((nt * spt, 8, SSD_XBC), F32), jax.ShapeDtypeStruct((n, DT_PAD), F32),
                 jax.ShapeDtypeStruct((n, ATT_WIDTH), BF16), k_shape,
                 jax.ShapeDtypeStruct((n, ATT_WIDTH), F32)]
    if tt:
        tr = lambda r: pl.BlockSpec((tm // tt, r, tt), lambda i: (i, 0, 0))
        vt_rows = ATT_HEADS * ATT_VT_ROWS
        out_specs += [tr(ATT_WIDTH), row(ATT_WIDTH), tr(vt_rows)]
        out_shape += [jax.ShapeDtypeStruct((n // tt, ATT_WIDTH, tt), BF16), jax.ShapeDtypeStruct((n, ATT_WIDTH), BF16),
                      jax.ShapeDtypeStruct((n // tt, vt_rows, tt), BF16)]

    def body(*refs):
        refs = refs + (None,) * (15 - len(refs))
        _in_proj_kernel(*refs, tm=tm, tt=tt, spt=spt)

    outs = pl.pallas_call(
        body,
        grid=(nt,),
        in_specs=[row(D_MODEL), _const_spec((1, D_MODEL)), _const_spec((D_MODEL, IN_COLS_PADDED)), tab, tab],
        out_specs=out_specs,
        out_shape=out_shape,
        compiler_params=pltpu.CompilerParams(dimension_semantics=("arbitrary",), vmem_limit_bytes=VMEM_LIMIT),
        name="in_proj",
    )(x, g, w, cos, sin)
    return list(outs) + [None] * (10 - len(outs))


def _expand_heads(x, e2):
    hi = x.astype(BF16)
    lo = (x - hi.astype(F32)).astype(BF16)
    return _dot(jnp.concatenate([hi, lo], axis=1), e2)


def _ssd_kernel(z_ref, xbc_ref, dt_ref, cbuf_ref, h0_ref, cw_ref, cb_ref, dtb_ref, alog_ref, dsk_ref, nw_ref, e2_ref,
                y_ref, hout_ref, state, ext, *, q, valid):
    c = pl.program_id(1)

    @pl.when(c == 0)
    def _():
        state[...] = h0_ref[0]
        ext[0:8, :] = cbuf_ref[0]

    ext[8:8 + q, :] = xbc_ref[...].astype(F32)
    conv = cb_ref[...] + ext[5:5 + q, :] * cw_ref[0:1, :]
    for w in range(1, SSD_CONV):
        conv = conv + ext[5 + w:5 + w + q, :] * cw_ref[w:w + 1, :]
    ext[0:8, :] = ext[q:q + 8, :]
    u = _silu(conv)
    xs = u[:, :SSD_WIDTH]
    bm = u[:, SSD_WIDTH:SSD_WIDTH + SSD_GROUPS * SSD_STATE]
    cm = u[:, SSD_WIDTH + SSD_GROUPS * SSD_STATE:]

    dtr = dt_ref[...] + dtb_ref[...]
    dt = jnp.maximum(dtr, 0.0) + jnp.log(1.0 + jnp.exp(-jnp.abs(dtr)))
    if valid is not None:
        row = lax.broadcasted_iota(jnp.int32, (q, DT_PAD), 0) + c * q
        dt = jnp.where(row < valid, dt, 0.0)
    a = -jnp.exp(alog_ref[...])
    ad = dt * a
    ri = lax.broadcasted_iota(jnp.int32, (q, q), 0)
    ci = lax.broadcasted_iota(jnp.int32, (q, q), 1)
    tril = ri >= ci
    acum = jnp.dot(tril.astype(F32), ad, preferred_element_type=F32, precision=lax.Precision.HIGHEST)
    acum_t = acum.T
    tot = acum[q - 1:q, :]
    e2 = e2_ref[...]
    expanded = _expand_heads(jnp.concatenate([dt, dt * jnp.exp(tot - acum), jnp.exp(acum)], axis=0), e2)
    dtx = expanded[0:q]
    ddx = expanded[q:2 * q]
    eax = expanded[2 * q:3 * q]
    dsk = _expand_heads(jnp.broadcast_to(dsk_ref[...], (8, DT_PAD)), e2)[0:1]

    xdt = (xs * dtx).astype(BF16)
    xdtd = (xs * ddx).astype(BF16)
    bm_t = bm.T.astype(BF16)
    cmb = cm.astype(BF16)
    gw = SSD_REP * SSD_HEAD_DIM
    stripe = lax.broadcasted_iota(jnp.int32, (q, gw), 1) // SSD_HEAD_DIM
    ys = []
    for g in range(SSD_GROUPS):
        cm_g = cmb[:, g * SSD_STATE:(g + 1) * SSD_STATE]
        bt_g = bm_t[g * SSD_STATE:(g + 1) * SSD_STATE, :]
        cbm = _dot(cm_g, bt_g)
        ms = []
        for r in range(SSD_REP):
            h = g * SSD_REP + r
            diff = acum[:, h:h + 1] - acum_t[h:h + 1, :]
            ms.append((cbm * jnp.exp(jnp.where(tril, diff, -jnp.inf))).astype(BF16))
        ydf = _dot(jnp.concatenate(ms, axis=0), xdt[:, g * gw:(g + 1) * gw])
        yd = ydf[0:q]
        for r in range(1, SSD_REP):
            yd = jnp.where(stripe == r, ydf[r * q:(r + 1) * q], yd)
        st = state[g]
        y_off = _dot(cm_g, st.astype(BF16)) * eax[:, g * gw:(g + 1) * gw]
        state[g] = st * eax[q - 1:q, g * gw:(g + 1) * gw] + _dot(bt_g, xdtd[:, g * gw:(g + 1) * gw])
        ys.append(yd + y_off)
    y = jnp.concatenate(ys, axis=1) + dsk * xs
    y = y * _silu(z_ref[...].astype(F32))
    outs = []
    for g in range(SSD_WIDTH // SSD_NORM_GROUP):
        yg = y[:, g * SSD_NORM_GROUP:(g + 1) * SSD_NORM_GROUP]
        outs.append(yg * lax.rsqrt(jnp.mean(yg * yg, axis=-1, keepdims=True) + SSD_NORM_EPS))
    y_ref[...] = (jnp.concatenate(outs, axis=1) * nw_ref[...]).astype(BF16)

    @pl.when(c == pl.num_programs(1) - 1)
    def _():
        hout_ref[0] = state[...]


def _ssd(z, xbc, dt, cbuf, h0, cw, cb, dtb, alog, dsk, nw, e2, *, nb, seq, q, valid):
    nc = seq // q
    row = lambda c: pl.BlockSpec((q, c), lambda b, i: (b * nc + i, 0))
    gw = SSD_REP * SSD_HEAD_DIM
    st_spec = pl.BlockSpec((1, SSD_GROUPS, SSD_STATE, gw), lambda b, i: (b, 0, 0, 0))
    return pl.pallas_call(
        functools.partial(_ssd_kernel, q=q, valid=valid),
        grid=(nb, nc),
        in_specs=[row(SSD_WIDTH), row(SSD_XBC), row(DT_PAD),
                  pl.BlockSpec((1, 8, SSD_XBC), lambda b, i: (b, 0, 0)), st_spec,
                  _const_spec((SSD_CONV, SSD_XBC)), _const_spec((1, SSD_XBC)), _const_spec((1, DT_PAD)),
                  _const_spec((1, DT_PAD)), _const_spec((1, DT_PAD)), _const_spec((1, SSD_WIDTH)),
                  _const_spec((2 * DT_PAD, SSD_WIDTH))],
        out_specs=[row(SSD_WIDTH), st_spec],
        out_shape=[jax.ShapeDtypeStruct((nb * seq, SSD_WIDTH), BF16),
                   jax.ShapeDtypeStruct((nb, SSD_GROUPS, SSD_STATE, gw), F32)],
        scratch_shapes=[pltpu.VMEM((SSD_GROUPS, SSD_STATE, gw), F32), pltpu.VMEM((q + 8, SSD_XBC), F32)],
        compiler_params=pltpu.CompilerParams(dimension_semantics=("arbitrary", "arbitrary"),
                                             vmem_limit_bytes=VMEM_LIMIT),
        name="ssd",
    )(z, xbc, dt, cbuf, h0, cw, cb, dtb, alog, dsk, nw, e2)


def _column_max(x):
    while x.shape[0] > 8 and x.shape[0] % 16 == 0:
        half = x.shape[0] // 2
        x = jnp.maximum(x[:half], x[half:])
    return jnp.max(x, axis=0, keepdims=True)


def _attn_tile_counts(i, *, tq, tk, past, kv_len, minimum=min):
    q_lo = past + i * tq
    q_hi = q_lo + tq - 1
    lim_lo = minimum((q_lo // CHUNK + 1) * CHUNK, kv_len)
    lim_hi = minimum((q_hi // CHUNK + 1) * CHUNK, kv_len)
    return lim_lo // tk, (lim_hi + tk - 1) // tk


def _diff_attn_kernel(qt_ref, k_ref, vt_ref, lam_ref, sw_ref, o_ref,
                      q2t_ref, s_ref, p_ref, m_ref, alpha_ref, acc_ref, *, tq, tk, past, kv_len, lam_init, nbuf):
    i = pl.program_id(2)
    w = 2 * tq
    qt = qt_ref[0]
    row = lax.broadcasted_iota(jnp.int32, (ATT_V_DIM, tq), 0)
    zero = jnp.zeros_like(qt)
    q2t_ref[:, 0:tq] = jnp.where(row < ATT_HEAD_DIM, qt, zero)
    q2t_ref[:, tq:w] = jnp.where(row >= ATT_HEAD_DIM, qt, zero)

    q_lo = past + i * tq
    n_full, _ = _attn_tile_counts(i, tq=tq, tk=tk, past=past, kv_len=kv_len, minimum=jnp.minimum)
    n_visits = n_full + 1

    def scores(j):
        kt = k_ref[pl.ds(pl.multiple_of(j * tk, tk), tk), :]
        return _dot(kt, q2t_ref[...])

    def visited_tile(v):
        return jnp.where(v == 0, n_full, jnp.maximum(v - 1, 0))

    def stage_a(v, buf):
        s_ref[buf] = scores(jnp.minimum(v - 1, n_full))

    rows = min(tk, ATT_SOFTMAX_ROWS)

    def stage_b(buf):
        for cb in range(w // LANES):
            sl = slice(cb * LANES, (cb + 1) * LANES)
            m_old = m_ref[:, sl]
            m_new = m_old
            for r0 in range(0, tk, rows):
                m_new = jnp.maximum(m_new, _column_max(s_ref[buf, r0:r0 + rows, sl]))
            m_ref[:, sl] = m_new
            alpha_ref[buf, :, sl] = jnp.exp2(m_old - m_new)
            for r0 in range(0, tk, rows):
                p_ref[buf, r0:r0 + rows, sl] = jnp.exp2((s_ref[buf, r0:r0 + rows, sl] - m_new).astype(BF16))

    def stage_c(v, buf):
        acc_ref[...] = alpha_ref[buf] * acc_ref[...] + _dot(vt_ref[visited_tile(v)], p_ref[buf])

    qpos = q_lo + lax.broadcasted_iota(jnp.int32, (tk, w), 1) % tq
    kpos = n_full * tk + lax.broadcasted_iota(jnp.int32, (tk, w), 0)
    s_ref[0] = jnp.where((kpos // CHUNK <= qpos // CHUNK) & (kpos < kv_len), scores(n_full), NEG_BIG)
    m_ref[...] = jnp.full_like(m_ref, NEG_BIG)
    acc_ref[...] = jnp.zeros_like(acc_ref)
    p_ref[nbuf - 1] = jnp.zeros(p_ref.shape[1:], BF16)
    alpha_ref[nbuf - 1] = jnp.ones(alpha_ref.shape[1:], F32)

    def visits(v0, count, prefetch_last):
        for r in range(count):
            prefetch = r + 1 < count or prefetch_last
            if prefetch and nbuf > 1:
                stage_a(v0 + r + 1, (r + 1) % nbuf)
            stage_c(v0 + r - 1, (r - 1) % nbuf)
            stage_b(r)
            if prefetch and nbuf == 1:
                stage_a(v0 + r + 1, (r + 1) % nbuf)

    def trip(u, carry):
        visits(nbuf * u, nbuf, True)
        return carry

    lax.fori_loop(0, n_visits // nbuf, trip, 0)
    for rem in range(nbuf):

        @pl.when(n_visits % nbuf == rem)
        def _():
            visits(n_visits - rem, rem, False)
            stage_c(n_visits - 1, (rem - 1) % nbuf)

    lam_v = lam_ref[...]
    lam = (jnp.exp(jnp.sum(lam_v[0:1] * lam_v[1:2], axis=-1, keepdims=True))
           - jnp.exp(jnp.sum(lam_v[2:3] * lam_v[3:4], axis=-1, keepdims=True)) + lam_init)
    acc = acc_ref[...]
    o = acc[:ATT_V_DIM] / acc[ATT_V_DIM:ATT_V_DIM + 1]
    o = o[:, :tq] - lam * o[:, tq:]
    o = o * lax.rsqrt(jnp.mean(o * o, axis=0, keepdims=True) + ATT_NORM_EPS) * (sw_ref[...] * (1.0 - lam_init))
    o_ref[...] = o.T.astype(BF16)


def _diff_attn(qt, k, vt, lam_vecs, subln_col, *, nb, tq, tk, past, kv_len, lam_init, nbuf):
    nq = qt.shape[0] // nb
    nkt = vt.shape[0] // nb
    for i in range(nq):
        n_full, n_end = _attn_tile_counts(i, tq=tq, tk=tk, past=past, kv_len=kv_len)
        assert n_end - n_full == 1 and n_end <= nkt, (i, n_full, n_end)
    w = 2 * tq
    return pl.pallas_call(
        functools.partial(_diff_attn_kernel, tq=tq, tk=tk, past=past, kv_len=kv_len, lam_init=lam_init, nbuf=nbuf),
        grid=(nb, ATT_HEADS, nq),
        in_specs=[pl.BlockSpec((1, ATT_V_DIM, tq), lambda b, h, i: (b * nq + i, h, 0)),
                  pl.BlockSpec((nkt * tk, ATT_V_DIM), lambda b, h, i: (b, h)),
                  pl.BlockSpec((nkt, ATT_VT_ROWS, tk), lambda b, h, i: (b, h, 0)),
                  _const_spec((4, ATT_HEAD_DIM)), _const_spec((ATT_V_DIM, 1))],
        out_specs=pl.BlockSpec((tq, ATT_V_DIM), lambda b, h, i: (b * nq + i, h)),
        out_shape=jax.ShapeDtypeStruct((nb * nq * tq, ATT_WIDTH), BF16),
        scratch_shapes=[pltpu.VMEM((ATT_V_DIM, w), BF16), pltpu.VMEM((nbuf, tk, w), F32),
                        pltpu.VMEM((nbuf, tk, w), BF16), pltpu.VMEM((1, w), F32), pltpu.VMEM((nbuf, 1, w), F32),
                        pltpu.VMEM((ATT_VT_ROWS, w), F32)],
        compiler_params=pltpu.CompilerParams(dimension_semantics=("arbitrary", "arbitrary", "arbitrary"),
                                             vmem_limit_bytes=VMEM_LIMIT),
        name="diff_attn",
    )(qt, k, vt, lam_vecs, subln_col)


def _decode_attn_kernel(q_ref, kn_ref, vn_ref, kc_ref, vc_ref, lam_ref, sw_ref, o_ref, *, seq, past, lam_init):
    lam_v = lam_ref[...]
    lam = (jnp.exp(jnp.sum(lam_v[0:1] * lam_v[1:2], axis=-1, keepdims=True))
           - jnp.exp(jnp.sum(lam_v[2:3] * lam_v[3:4], axis=-1, keepdims=True)) + lam_init)
    npad = LANES
    lane = lax.broadcasted_iota(jnp.int32, (seq, LANES), 1)
    qchunk = (past + lax.broadcasted_iota(jnp.int32, (2 * seq, 1), 0) % seq) // CHUNK
    kpos_p = lax.broadcasted_iota(jnp.int32, (2 * seq, past), 1)
    kpos_n = lax.broadcasted_iota(jnp.int32, (2 * seq, npad), 1)
    vis_p = kpos_p // CHUNK <= qchunk
    vis_n = ((past + kpos_n) // CHUNK <= qchunk) & (kpos_n < seq)
    pad_rows = jnp.zeros((npad - seq, LANES), BF16)
    for h in range(ATT_HEADS):
        sl = slice(h * LANES, (h + 1) * LANES)
        qh = q_ref[:, sl]
        zero = jnp.zeros_like(qh)
        q2 = jnp.concatenate([jnp.where(lane < ATT_HEAD_DIM, qh, zero), jnp.where(lane >= ATT_HEAD_DIM, qh, zero)], axis=0)
        s_p = jnp.where(vis_p, _dot(q2, kc_ref[0, sl, :].astype(BF16)), NEG_BIG)
        kn = jnp.concatenate([kn_ref[:, sl].astype(BF16), pad_rows], axis=0)
        s_n = lax.dot_general(q2, kn, (((1,), (1,)), ((), ())), preferred_element_type=F32)
        s_n = jnp.where(vis_n, s_n, NEG_BIG)
        m = jnp.maximum(jnp.max(s_p, axis=-1, keepdims=True), jnp.max(s_n, axis=-1, keepdims=True))
        p_p = jnp.exp2(s_p - m)
        p_n = jnp.exp2(s_n - m)
        l = jnp.sum(p_p, axis=-1, keepdims=True) + jnp.sum(p_n, axis=-1, keepdims=True)
        vh = vc_ref[0, pl.ds(h, past, stride=ATT_HEADS), :].astype(BF16)
        vn = jnp.concatenate([vn_ref[:, sl].astype(BF16), pad_rows], axis=0)
        o = (_dot(p_p.astype(BF16), vh) + _dot(p_n.astype(BF16), vn)) / l
        o = o[:seq] - lam * o[seq:]
        o = o * lax.rsqrt(jnp.mean(o * o, axis=-1, keepdims=True) + ATT_NORM_EPS) * (sw_ref[...] * (1.0 - lam_init))
        o_ref[:, sl] = o.astype(BF16)


def _decode_attn(q, k_new, v_new, kt_cache, v_cache, lam_vecs, subln_row, *, nb, seq, past, lam_init):
    row = pl.BlockSpec((seq, ATT_WIDTH), lambda b: (b, 0))
    return pl.pallas_call(
        functools.partial(_decode_attn_kernel, seq=seq, past=past, lam_init=lam_init),
        grid=(nb,),
        in_specs=[row, row, row,
                  pl.BlockSpec((1, ATT_WIDTH, past), lambda b: (b, 0, 0)),
                  pl.BlockSpec((1, past * ATT_HEADS, ATT_V_DIM), lambda b: (b, 0, 0)),
                  _const_spec((4, ATT_HEAD_DIM)), _const_spec((1, ATT_V_DIM))],
        out_specs=row,
        out_shape=jax.ShapeDtypeStruct((nb * seq, ATT_WIDTH), BF16),
        compiler_params=pltpu.CompilerParams(dimension_semantics=("arbitrary",), vmem_limit_bytes=VMEM_LIMIT),
        name="decode_attn",
    )(q, k_new, v_new, kt_cache, v_cache, lam_vecs, subln_row)


def _post_mix_kernel(x_ref, ys_ref, ya_ref, mk_ref, mv_ref, wo_ref, wq_ref, wox_ref,
                     g1_ref, g2_ref, g3_ref, h_ref, *, spt, rows):
    mix = _dot(ys_ref[...], wo_ref[0:SSD_WIDTH, :]) + _dot(ya_ref[...], wo_ref[SSD_WIDTH:, :])
    h = x_ref[...] + _rms(mix, g1_ref[...], NORM_EPS)
    qn = _rms(h, g2_ref[...], NORM_EPS).astype(BF16)
    qx = (_dot(qn, wq_ref[...]) * (1.0 / math.sqrt(MEM_HEAD_DIM))).astype(BF16)
    ox_seqs = []
    for s_i in range(spt):
        qs = qx[s_i * rows:(s_i + 1) * rows]
        mk = mk_ref[s_i]
        mv = mv_ref[s_i]
        oxs = []
        for hd in range(MEM_HEADS):
            sl = slice(hd * MEM_HEAD_DIM, (hd + 1) * MEM_HEAD_DIM)
            s = lax.dot_general(qs[:, sl], mk[:, sl], (((1,), (1,)), ((), ())), preferred_element_type=F32)
            p = jnp.exp(s - jnp.max(s, axis=-1, keepdims=True))
            ox = _dot(p.astype(BF16), mv[:, sl]) / jnp.sum(p, axis=-1, keepdims=True)
            oxs.append(ox.astype(BF16))
        ox_seqs.append(jnp.concatenate(oxs, axis=1))
    ox_all = ox_seqs[0] if spt == 1 else jnp.concatenate(ox_seqs, axis=0)
    o2 = _dot(ox_all, wox_ref[...])
    h_ref[...] = h + _rms(o2, g3_ref[...], NORM_EPS)


def _post_mix(x, ys, ya, mk, mv, w_out, wq, wox, g1, g2, g3, *, seq, tm):
    n = x.shape[0]
    spt = max(1, tm // seq)
    tiles_per_seq = max(1, seq // tm)
    row = lambda c: pl.BlockSpec((tm, c), lambda i: (i, 0))
    mem = pl.BlockSpec((spt, MEM_LEN, D_MODEL), lambda i: (i // tiles_per_seq, 0, 0))
    wspec = _const_spec((D_MODEL, D_MODEL))
    gspec = _const_spec((1, D_MODEL))
    return pl.pallas_call(
        functools.partial(_post_mix_kernel, spt=spt, rows=tm // spt),
        grid=(n // tm,),
        in_specs=[row(D_MODEL), row(SSD_WIDTH), row(ATT_WIDTH), mem, mem, wspec, wspec, wspec, gspec, gspec, gspec],
        out_specs=row(D_MODEL),
        out_shape=jax.ShapeDtypeStruct((n, D_MODEL), F32),
        compiler_params=pltpu.CompilerParams(dimension_semantics=("arbitrary",), vmem_limit_bytes=VMEM_LIMIT),
        name="post_mix",
    )(x, ys, ya, mk, mv, w_out, wq, wox, g1, g2, g3)


def _ffn_kernel(h_ref, wg_ref, wu_ref, wd_ref, g1_ref, g2_ref, o_ref):
    h = h_ref[...]
    hn = _rms(h, g1_ref[...], NORM_EPS).astype(BF16)
    act = (_silu(_dot(hn, wg_ref[...])) * _dot(hn, wu_ref[...])).astype(BF16)
    f = _dot(act, wd_ref[...])
    o_ref[...] = h + _rms(f, g2_ref[...], NORM_EPS)


def _ffn(h, wg, wu, wd, g1, g2, *, tm):
    n = h.shape[0]
    row = pl.BlockSpec((tm, D_MODEL), lambda i: (i, 0))
    return pl.pallas_call(
        _ffn_kernel,
        grid=(n // tm,),
        in_specs=[row, _const_spec((D_MODEL, FFN_HIDDEN)), _const_spec((D_MODEL, FFN_HIDDEN)),
                  _const_spec((FFN_HIDDEN, D_MODEL)), _const_spec((1, D_MODEL)), _const_spec((1, D_MODEL))],
        out_specs=row,
        out_shape=jax.ShapeDtypeStruct((n, D_MODEL), F32),
        compiler_params=pltpu.CompilerParams(dimension_semantics=("arbitrary",), vmem_limit_bytes=VMEM_LIMIT),
        name="ffn",
    )(h, wg, wu, wd, g1, g2)


def _rope_tables(past, seq):
    half = ATT_HEAD_DIM // 2
    inv = jnp.power(ROPE_THETA, -jnp.arange(0, ATT_HEAD_DIM, 2, dtype=F32) / ATT_HEAD_DIM)
    pos = (past + jnp.arange(seq, dtype=jnp.int32)).astype(F32)
    ang = pos[:, None] * inv[None, :]
    cos, sin = jnp.cos(ang), jnp.sin(ang)
    reps = LANES // ATT_HEAD_DIM
    assert half * 2 == ATT_HEAD_DIM
    return jnp.tile(jnp.concatenate([cos, cos], axis=-1), (1, reps)), jnp.tile(jnp.concatenate([-sin, sin], axis=-1), (1, reps))


def _state_to_kernel_layout(s):
    b = s.shape[0]
    s = s.reshape(b, SSD_GROUPS, SSD_REP, SSD_HEAD_DIM, SSD_STATE)
    return s.transpose(0, 1, 4, 2, 3).reshape(b, SSD_GROUPS, SSD_STATE, SSD_REP * SSD_HEAD_DIM)


def _state_from_kernel_layout(s):
    b = s.shape[0]
    s = s.reshape(b, SSD_GROUPS, SSD_STATE, SSD_REP, SSD_HEAD_DIM)
    return s.transpose(0, 1, 3, 4, 2).reshape(b, SSD_HEADS, SSD_HEAD_DIM, SSD_STATE)


def _layer(x, conv_buf, ssm0, kt_past, v_past, mem_kb, mem_vb, lam_init, p, *, tm, tk, tq, ssd_q):
    nb, seq, _ = x.shape
    n = nb * seq
    past = 0 if kt_past is None else kt_past.shape[2]
    xf = x.reshape(n, D_MODEL)
    cos, sin = _rope_tables(past, seq)
    no_history = kt_past is None
    assert tq == tk or not no_history
    z, xbc, tail, dt, q, k, v, qt, kb, vt = _in_proj(xf, p["g_pre_mix"], p["w_in"], cos, sin, seq=seq, tm=tm,
                                                      tt=tk if no_history else 0)

    seq_pad = -(-seq // ssd_q) * ssd_q
    if seq_pad != seq:
        pad = lambda a: jnp.pad(a.reshape(nb, seq, -1), ((0, 0), (0, seq_pad - seq), (0, 0))).reshape(nb * seq_pad, -1)
        z_s, xbc_s, dt_s = pad(z), pad(xbc), pad(dt)
    else:
        z_s, xbc_s, dt_s = z, xbc, dt
    cbuf = jnp.pad(conv_buf.astype(F32), ((0, 0), (8 - (SSD_CONV - 1), 0), (0, 0)))
    y_ssd, h_new = _ssd(z_s, xbc_s, dt_s, cbuf, _state_to_kernel_layout(ssm0.astype(F32)),
                        p["conv_w"], p["conv_b"], p["dt_bias"], p["a_log"], p["d_skip"], p["ssm_norm_w"], p["e2"],
                        nb=nb, seq=seq_pad, q=ssd_q, valid=None if seq_pad == seq else seq)
    if seq_pad != seq:
        y_ssd = y_ssd.reshape(nb, seq_pad, SSD_WIDTH)[:, :seq].reshape(n, SSD_WIDTH)
    ssm_new = _state_from_kernel_layout(h_new)
    ext_tail = tail.reshape(nb, -1, 8, SSD_XBC)[:, -1]
    if seq >= SSD_CONV - 1:
        conv_new = ext_tail[:, 8 - (SSD_CONV - 1):]
    else:
        conv_new = jnp.concatenate([conv_buf.astype(F32), ext_tail[:, 8 - seq:]], axis=1)[:, -(SSD_CONV - 1):]

    if no_history:
        kv_len = seq
        y_att = _diff_attn(qt, kb, vt, p["lam_vecs"], p["subln_col"], nb=nb, tq=tq, tk=tk,
                           past=past, kv_len=kv_len, lam_init=lam_init, nbuf=ATT_PIPELINE_BUFFERS)
    else:
        y_att = _decode_attn(q, k, v, kt_past, v_past, p["lam_vecs"], p["subln_col"].reshape(1, ATT_V_DIM),
                             nb=nb, seq=seq, past=past, lam_init=lam_init)

    h = _post_mix(xf, y_ssd, y_att, mem_kb, mem_vb, p["w_out"], p["wq_x"], p["wo_x"],
                  p["g_post_mix"], p["g_pre_x"], p["g_post_x"], seq=seq, tm=tm)
    out = _ffn(h, p["w_gate"], p["w_up"], p["w_down"], p["g_pre_ffn"], p["g_post_ffn"], tm=tm)
    if no_history:
        k_out = k.reshape(nb, ATT_HEADS, 2, ATT_HEAD_DIM, seq).transpose(0, 4, 1, 2, 3)
    else:
        k_out = k.reshape(nb, seq, ATT_HEADS, 2, ATT_HEAD_DIM)
    return (out.reshape(nb, seq, D_MODEL), k_out,
            v.reshape(nb, seq, ATT_HEADS, ATT_V_DIM), ssm_new, conv_new)


def _prep_params(i, w_in, conv_w, conv_b, dt_bias, a_log, d_skip, ssm_norm_w, lam_q1, lam_k1, lam_q2, lam_k2, subln_w,
                 w_out, wq_x, wo_x, g_pre_mix, g_post_mix, g_pre_x, g_post_x, g_pre_ffn, g_post_ffn,
                 w_gate, w_up, w_down):
    w = w_in[i]
    s0 = SSD_WIDTH + SSD_XBC
    s1 = s0 + SSD_HEADS
    w_r = jnp.concatenate([w[:, :s0], w[:, s1:], w[:, s0:s1], jnp.zeros((D_MODEL, DT_PAD - SSD_HEADS), w.dtype)], axis=1)
    head_pad = lambda a: jnp.pad(a[i].astype(F32), (0, DT_PAD - SSD_HEADS)).reshape(1, DT_PAD)
    e = (jnp.arange(DT_PAD)[:, None] == (jnp.arange(SSD_WIDTH)[None, :] // SSD_HEAD_DIM)).astype(BF16)
    row = lambda a: a[i].astype(F32).reshape(1, -1)
    return {
        "w_in": w_r.astype(BF16), "conv_w": conv_w[i].astype(F32), "conv_b": row(conv_b),
        "dt_bias": head_pad(dt_bias), "a_log": head_pad(a_log), "d_skip": head_pad(d_skip),
        "ssm_norm_w": row(ssm_norm_w), "e2": jnp.concatenate([e, e], axis=0),
        "lam_vecs": jnp.stack([lam_q1[i], lam_k1[i], lam_q2[i], lam_k2[i]]).astype(F32), "subln_col": subln_w[i].astype(F32).reshape(ATT_V_DIM, 1),
        "w_out": w_out[i].astype(BF16), "wq_x": wq_x[i].astype(BF16), "wo_x": wo_x[i].astype(BF16),
        "g_pre_mix": row(g_pre_mix), "g_post_mix": row(g_post_mix), "g_pre_x": row(g_pre_x), "g_post_x": row(g_post_x),
        "g_pre_ffn": row(g_pre_ffn), "g_post_ffn": row(g_post_ffn),
        "w_gate": w_gate[i].astype(BF16), "w_up": w_up[i].astype(BF16), "w_down": w_down[i].astype(BF16),
    }


def kernel(x_prompt, x_sample, cache_attn_k, cache_attn_v, state_ssm, state_conv, cache_mem_k, cache_mem_v, mem_prompt, w_in, conv_w, conv_b, dt_bias, a_log, d_skip, ssm_norm_w, lam_q1, lam_k1, lam_q2, lam_k2, subln_w, w_out, g_mem, wq_x, wk_x, wv_x, wo_x, g_pre_mix, g_post_mix, g_pre_x, g_post_x, g_pre_ffn, g_post_ffn, w_gate, w_up, w_down):
    depth = w_in.shape[0]
    bp, sp, _ = x_prompt.shape
    bs, ss, _ = x_sample.shape
    hp, hs = x_prompt, x_sample
    outs = [[] for _ in range(10)]
    for i in range(depth):
        lam_init = 0.8 - 0.6 * math.exp(-0.3 * i)
        p = _prep_params(i, w_in, conv_w, conv_b, dt_bias, a_log, d_skip, ssm_norm_w, lam_q1, lam_k1, lam_q2, lam_k2,
                         subln_w, w_out, wq_x, wo_x, g_pre_mix, g_post_mix, g_pre_x, g_post_x, g_pre_ffn, g_post_ffn,
                         w_gate, w_up, w_down)
        mk, mv, mkb, mvb = _mem_kv(mem_prompt.reshape(bp * MEM_LEN, D_MODEL), g_mem[i].reshape(1, D_MODEL),
                                   wk_x[i].astype(BF16), wv_x[i].astype(BF16))
        hp, k_new, v_new, ssm_new, conv_new = _layer(
            hp, jnp.zeros((bp, SSD_CONV - 1, SSD_XBC), F32), jnp.zeros((bp, SSD_HEADS, SSD_HEAD_DIM, SSD_STATE), F32),
            None, None, mkb.reshape(bp, MEM_LEN, D_MODEL), mvb.reshape(bp, MEM_LEN, D_MODEL), lam_init, p,
            tm=512, tk=512, tq=512, ssd_q=256)
        for lst, val in zip(outs[:6], (k_new, v_new, ssm_new, conv_new,
                                       mk.reshape(bp, MEM_LEN, MEM_HEADS, MEM_HEAD_DIM),
                                       mv.reshape(bp, MEM_LEN, MEM_HEADS, MEM_HEAD_DIM))):
            lst.append(val)
        past = cache_attn_k.shape[2]
        hs, k_new, v_new, ssm_new, conv_new = _layer(
            hs, state_conv[i], state_ssm[i],
            cache_attn_k[i].transpose(0, 2, 3, 4, 1).reshape(bs, ATT_WIDTH, past),
            cache_attn_v[i].reshape(bs, past * ATT_HEADS, ATT_V_DIM),
            cache_mem_k[i].reshape(bs, MEM_LEN, D_MODEL).astype(BF16),
            cache_mem_v[i].reshape(bs, MEM_LEN, D_MODEL).astype(BF16), lam_init, p,
            tm=bs * ss, tk=0, tq=0, ssd_q=128)
        for lst, val in zip(outs[6:], (k_new, v_new, ssm_new, conv_new)):
            lst.append(val)
    return (hp, hs) + tuple(jnp.stack(o) for o in outs)
```

```python
import functools
import math

import jax
import jax.numpy as jnp
from jax import lax
from jax.experimental import pallas as pl
from jax.experimental.pallas import tpu as pltpu

D_MODEL = 1024
CHUNK = 64
SSD_WIDTH = 512
SSD_HEAD_DIM = 64
SSD_HEADS = 8
SSD_GROUPS = 2
SSD_REP = 4
SSD_STATE = 128
SSD_CONV = 4
SSD_XBC = 1024
SSD_NORM_GROUP = 256
SSD_NORM_EPS = 1e-5
ATT_WIDTH = 512
ATT_HEAD_DIM = 64
ATT_HEADS = 4
ATT_V_DIM = 128
ATT_NORM_EPS = 1e-5
ROPE_THETA = 10000.0
MEM_LEN = 256
MEM_HEADS = 4
MEM_HEAD_DIM = 256
FFN_HIDDEN = 2816
NORM_EPS = 1e-6
LANES = 128
DT_PAD = LANES
IN_COLS_PADDED = SSD_WIDTH + SSD_XBC + 3 * ATT_WIDTH + DT_PAD
VMEM_LIMIT = 56 * 1024 * 1024
NEG_BIG = -1e30
BF16_SUBLANES = 16
ATT_VT_ROWS = ATT_V_DIM + BF16_SUBLANES
ATT_SOFTMAX_ROWS = 128
ATT_PIPELINE_BUFFERS = 2
Q_SCALE = math.log2(math.e) / math.sqrt(ATT_HEAD_DIM)

F32 = jnp.float32
BF16 = jnp.bfloat16


def _const_spec(shape):
    return pl.BlockSpec(shape, lambda *_: (0,) * len(shape), pipeline_mode=pl.Buffered(1))


def _rms(x, g, eps):
    return x * lax.rsqrt(jnp.mean(x * x, axis=-1, keepdims=True) + eps) * g


def _silu(x):
    return x / (1.0 + jnp.exp(-x))


def _dot(a, b):
    return jnp.dot(a, b, preferred_element_type=F32)


def _mem_kv_kernel(mem_ref, g_ref, wk_ref, wv_ref, mk_ref, mv_ref, mkb_ref, mvb_ref):
    mn = _rms(mem_ref[...], g_ref[...], NORM_EPS).astype(BF16)
    mk = _dot(mn, wk_ref[...])
    mv = _dot(mn, wv_ref[...])
    mk_ref[...] = mk
    mv_ref[...] = mv
    mkb_ref[...] = mk.astype(BF16)
    mvb_ref[...] = mv.astype(BF16)


def _mem_kv(mem, g_mem, wk, wv):
    n = mem.shape[0]
    tm = MEM_LEN
    row = pl.BlockSpec((tm, D_MODEL), lambda i: (i, 0))
    return pl.pallas_call(
        _mem_kv_kernel,
        grid=(n // tm,),
        in_specs=[row, _const_spec((1, D_MODEL)), _const_spec((D_MODEL, D_MODEL)), _const_spec((D_MODEL, D_MODEL))],
        out_specs=[row, row, row, row],
        out_shape=[jax.ShapeDtypeStruct((n, D_MODEL), F32), jax.ShapeDtypeStruct((n, D_MODEL), F32),
                   jax.ShapeDtypeStruct((n, D_MODEL), BF16), jax.ShapeDtypeStruct((n, D_MODEL), BF16)],
        compiler_params=pltpu.CompilerParams(dimension_semantics=("arbitrary",), vmem_limit_bytes=VMEM_LIMIT),
        name="mem_kv",
    )(mem, g_mem, wk, wv)


def _in_proj_kernel(x_ref, g_ref, w_ref, cos_ref, sin_ref,
                    z_ref, xbc_ref, tail_ref, dt_ref, q_ref, k_ref, v_ref, qt_ref, kb_ref, vt_ref, *, tm, tt, spt):
    hn = _rms(x_ref[...], g_ref[...], NORM_EPS).astype(BF16)
    acc = _dot(hn, w_ref[...])
    o = 0
    z_ref[...] = acc[:, o:o + SSD_WIDTH].astype(BF16)
    o += SSD_WIDTH
    xbc = acc[:, o:o + SSD_XBC]
    xbc_ref[...] = xbc.astype(BF16)
    rows = tm // spt
    for s_i in range(spt):
        tail_ref[s_i] = xbc[(s_i + 1) * rows - 8:(s_i + 1) * rows, :]
    o += SSD_XBC
    q = acc[:, o:o + ATT_WIDTH]
    o += ATT_WIDTH
    k = acc[:, o:o + ATT_WIDTH]
    o += ATT_WIDTH
    v = acc[:, o:o + ATT_WIDTH]
    o += ATT_WIDTH
    dt_ref[...] = acc[:, o:o + DT_PAD]
    if tt:
        for j in range(ATT_HEADS):
            v_ref[pl.ds(j, tm, stride=ATT_HEADS), :] = v[:, j * ATT_V_DIM:(j + 1) * ATT_V_DIM]
    else:
        v_ref[...] = v

    cos = cos_ref[...]
    sin = sin_ref[...]
    first_half = (lax.broadcasted_iota(jnp.int32, (tm, LANES), 1) % ATT_HEAD_DIM) < (ATT_HEAD_DIM // 2)

    def rope(t):
        swapped = jnp.where(first_half, pltpu.roll(t, LANES - ATT_HEAD_DIM // 2, 1),
                            pltpu.roll(t, ATT_HEAD_DIM // 2, 1))
        return t * cos + swapped * sin

    for j in range(ATT_WIDTH // LANES):
        sl = slice(j * LANES, (j + 1) * LANES)
        qr = rope(q[:, sl]) * Q_SCALE
        kr = rope(k[:, sl])
        q_ref[:, sl] = qr.astype(BF16)
        if tt:
            k_ref[0, sl, :] = kr.T
            kb_ref[:, sl] = kr.astype(BF16)
            for c in range(tm // tt):
                qt_ref[c, sl, :] = qr[c * tt:(c + 1) * tt, :].T.astype(BF16)
                vt_ref[c, j * ATT_VT_ROWS:j * ATT_VT_ROWS + LANES, :] = v[c * tt:(c + 1) * tt, sl].T.astype(BF16)
                vt_ref[c, j * ATT_VT_ROWS + LANES:(j + 1) * ATT_VT_ROWS, :] = jnp.ones((ATT_VT_ROWS - LANES, tt), BF16)
        else:
            k_ref[:, sl] = kr


def _in_proj(x, g, w, cos, sin, *, seq, tm, tt):
    n = x.shape[0]
    nt = n // tm
    spt = max(1, tm // seq)
    tiles_per_seq = max(1, seq // tm)
    if spt > 1:
        cos, sin = jnp.tile(cos, (spt, 1)), jnp.tile(sin, (spt, 1))
    row = lambda c: pl.BlockSpec((tm, c), lambda i: (i, 0))
    tab = pl.BlockSpec((tm, LANES), lambda i: (i % tiles_per_seq, 0))
    if tt:
        k_spec = pl.BlockSpec((1, ATT_WIDTH, tm), lambda i: (i // tiles_per_seq, 0, i % tiles_per_seq))
        k_shape = jax.ShapeDtypeStruct((n // seq, ATT_WIDTH, seq), F32)
        v_spec = pl.BlockSpec((tm * ATT_HEADS, ATT_V_DIM), lambda i: (i, 0))
        v_shape = jax.ShapeDtypeStruct((n * ATT_HEADS, ATT_V_DIM), F32)
    else:
        k_spec, k_shape = row(ATT_WIDTH), jax.ShapeDtypeStruct((n, ATT_WIDTH), F32)
        v_spec, v_shape = row(ATT_WIDTH), jax.ShapeDtypeStruct((n, ATT_WIDTH), F32)
    out_specs = [row(SSD_WIDTH), row(SSD_XBC), pl.BlockSpec((spt, 8, SSD_XBC), lambda i: (i, 0, 0)), row(DT_PAD),
                 row(ATT_WIDTH), k_spec, v_spec]
    out_shape = [jax.ShapeDtypeStruct((n, SSD_WIDTH), BF16), jax.ShapeDtypeStruct((n, SSD_XBC), BF16),
                 jax.ShapeDtypeStruct((nt * spt, 8, SSD_XBC), F32), jax.ShapeDtypeStruct((n, DT_PAD), F32),
                 jax.ShapeDtypeStruct((n, ATT_WIDTH), BF16), k_shape, v_shape]
    if tt:
        tr = lambda r: pl.BlockSpec((tm // tt, r, tt), lambda i: (i, 0, 0))
        vt_rows = ATT_HEADS * ATT_VT_ROWS
        out_specs += [tr(ATT_WIDTH), row(ATT_WIDTH), tr(vt_rows)]
        out_shape += [jax.ShapeDtypeStruct((n // tt, ATT_WIDTH, tt), BF16), jax.ShapeDtypeStruct((n, ATT_WIDTH), BF16),
                      jax.ShapeDtypeStruct((n // tt, vt_rows, tt), BF16)]

    def body(*refs):
        refs = refs + (None,) * (15 - len(refs))
        _in_proj_kernel(*refs, tm=tm, tt=tt, spt=spt)

    outs = pl.pallas_call(
        body,
        grid=(nt,),
        in_specs=[row(D_MODEL), _const_spec((1, D_MODEL)), _const_spec((D_MODEL, IN_COLS_PADDED)), tab, tab],
        out_specs=out_specs,
        out_shape=out_shape,
        compiler_params=pltpu.CompilerParams(dimension_semantics=("arbitrary",), vmem_limit_bytes=VMEM_LIMIT),
        name="in_proj",
    )(x, g, w, cos, sin)
    return list(outs) + [None] * (10 - len(outs))


def _expand_heads(x, e2):
    hi = x.astype(BF16)
    lo = (x - hi.astype(F32)).astype(BF16)
    return _dot(jnp.concatenate([hi, lo], axis=1), e2)


def _ssd_kernel(z_ref, xbc_ref, dt_ref, cbuf_ref, h0_ref, cw_ref, cb_ref, dtb_ref, alog_ref, dsk_ref, nw_ref, e2_ref,
                y_ref, hout_ref, state, ext, *, q, valid):
    c = pl.program_id(1)

    @pl.when(c == 0)
    def _():
        state[...] = h0_ref[0]
        ext[0:8, :] = cbuf_ref[0]

    ext[8:8 + q, :] = xbc_ref[...].astype(F32)
    conv = cb_ref[...] + ext[5:5 + q, :] * cw_ref[0:1, :]
    for w in range(1, SSD_CONV):
        conv = conv + ext[5 + w:5 + w + q, :] * cw_ref[w:w + 1, :]
    ext[0:8, :] = ext[q:q + 8, :]
    u = _silu(conv)
    xs = u[:, :SSD_WIDTH]
    bm = u[:, SSD_WIDTH:SSD_WIDTH + SSD_GROUPS * SSD_STATE]
    cm = u[:, SSD_WIDTH + SSD_GROUPS * SSD_STATE:]

    dtr = dt_ref[...] + dtb_ref[...]
    dt = jnp.maximum(dtr, 0.0) + jnp.log(1.0 + jnp.exp(-jnp.abs(dtr)))
    if valid is not None:
        row = lax.broadcasted_iota(jnp.int32, (q, DT_PAD), 0) + c * q
        dt = jnp.where(row < valid, dt, 0.0)
    a = -jnp.exp(alog_ref[...])
    ad = dt * a
    ri = lax.broadcasted_iota(jnp.int32, (q, q), 0)
    ci = lax.broadcasted_iota(jnp.int32, (q, q), 1)
    tril = ri >= ci
    acum = jnp.dot(tril.astype(F32), ad, preferred_element_type=F32, precision=lax.Precision.HIGHEST)
    acum_t = acum.T
    tot = acum[q - 1:q, :]
    e2 = e2_ref[...]
    expanded = _expand_heads(jnp.concatenate([dt, dt * jnp.exp(tot - acum), jnp.exp(acum)], axis=0), e2)
    dtx = expanded[0:q]
    ddx = expanded[q:2 * q]
    eax = expanded[2 * q:3 * q]
    dsk = _expand_heads(jnp.broadcast_to(dsk_ref[...], (8, DT_PAD)), e2)[0:1]

    xdt = (xs * dtx).astype(BF16)
    xdtd = (xs * ddx).astype(BF16)
    bm_t = bm.T.astype(BF16)
    cmb = cm.astype(BF16)
    gw = SSD_REP * SSD_HEAD_DIM
    stripe = lax.broadcasted_iota(jnp.int32, (q, gw), 1) // SSD_HEAD_DIM
    ys = []
    for g in range(SSD_GROUPS):
        cm_g = cmb[:, g * SSD_STATE:(g + 1) * SSD_STATE]
        bt_g = bm_t[g * SSD_STATE:(g + 1) * SSD_STATE, :]
        cbm = _dot(cm_g, bt_g)
        ms = []
        for r in range(SSD_REP):
            h = g * SSD_REP + r
            diff = acum[:, h:h + 1] - acum_t[h:h + 1, :]
            ms.append((cbm * jnp.exp(jnp.where(tril, diff, -jnp.inf))).astype(BF16))
        ydf = _dot(jnp.concatenate(ms, axis=0), xdt[:, g * gw:(g + 1) * gw])
        yd = ydf[0:q]
        for r in range(1, SSD_REP):
            yd = jnp.where(stripe == r, ydf[r * q:(r + 1) * q], yd)
        st = state[g]
        y_off = _dot(cm_g, st.astype(BF16)) * eax[:, g * gw:(g + 1) * gw]
        state[g] = st * eax[q - 1:q, g * gw:(g + 1) * gw] + _dot(bt_g, xdtd[:, g * gw:(g + 1) * gw])
        ys.append(yd + y_off)
    y = jnp.concatenate(ys, axis=1) + dsk * xs
    y = y * _silu(z_ref[...].astype(F32))
    outs = []
    for g in range(SSD_WIDTH // SSD_NORM_GROUP):
        yg = y[:, g * SSD_NORM_GROUP:(g + 1) * SSD_NORM_GROUP]
        outs.append(yg * lax.rsqrt(jnp.mean(yg * yg, axis=-1, keepdims=True) + SSD_NORM_EPS))
    y_ref[...] = (jnp.concatenate(outs, axis=1) * nw_ref[...]).astype(BF16)

    @pl.when(c == pl.num_programs(1) - 1)
    def _():
        hout_ref[0] = state[...]


def _ssd(z, xbc, dt, cbuf, h0, cw, cb, dtb, alog, dsk, nw, e2, *, nb, seq, q, valid):
    nc = seq // q
    row = lambda c: pl.BlockSpec((q, c), lambda b, i: (b * nc + i, 0))
    gw = SSD_REP * SSD_HEAD_DIM
    st_spec = pl.BlockSpec((1, SSD_GROUPS, SSD_STATE, gw), lambda b, i: (b, 0, 0, 0))
    return pl.pallas_call(
        functools.partial(_ssd_kernel, q=q, valid=valid),
        grid=(nb, nc),
        in_specs=[row(SSD_WIDTH), row(SSD_XBC), row(DT_PAD),
                  pl.BlockSpec((1, 8, SSD_XBC), lambda b, i: (b, 0, 0)), st_spec,
                  _const_spec((SSD_CONV, SSD_XBC)), _const_spec((1, SSD_XBC)), _const_spec((1, DT_PAD)),
                  _const_spec((1, DT_PAD)), _const_spec((1, DT_PAD)), _const_spec((1, SSD_WIDTH)),
                  _const_spec((2 * DT_PAD, SSD_WIDTH))],
        out_specs=[row(SSD_WIDTH), st_spec],
        out_shape=[jax.ShapeDtypeStruct((nb * seq, SSD_WIDTH), BF16),
                   jax.ShapeDtypeStruct((nb, SSD_GROUPS, SSD_STATE, gw), F32)],
        scratch_shapes=[pltpu.VMEM((SSD_GROUPS, SSD_STATE, gw), F32), pltpu.VMEM((q + 8, SSD_XBC), F32)],
        compiler_params=pltpu.CompilerParams(dimension_semantics=("arbitrary", "arbitrary"),
                                             vmem_limit_bytes=VMEM_LIMIT),
        name="ssd",
    )(z, xbc, dt, cbuf, h0, cw, cb, dtb, alog, dsk, nw, e2)


def _column_max(x):
    while x.shape[0] > 8 and x.shape[0] % 16 == 0:
        half = x.shape[0] // 2
        x = jnp.maximum(x[:half], x[half:])
    return jnp.max(x, axis=0, keepdims=True)


def _attn_tile_counts(i, *, tq, tk, past, kv_len, minimum=min):
    q_lo = past + i * tq
    q_hi = q_lo + tq - 1
    lim_lo = minimum((q_lo // CHUNK + 1) * CHUNK, kv_len)
    lim_hi = minimum((q_hi // CHUNK + 1) * CHUNK, kv_len)
    return lim_lo // tk, (lim_hi + tk - 1) // tk


def _diff_attn_kernel(qt_ref, k_ref, vt_ref, lam_ref, sw_ref, o_ref,
                      q2t_ref, s_ref, p_ref, m_ref, alpha_ref, acc_ref, *, tq, tk, past, kv_len, lam_init, nbuf):
    i = pl.program_id(2)
    w = 2 * tq
    qt = qt_ref[0]
    row = lax.broadcasted_iota(jnp.int32, (ATT_V_DIM, tq), 0)
    zero = jnp.zeros_like(qt)
    q2t_ref[:, 0:tq] = jnp.where(row < ATT_HEAD_DIM, qt, zero)
    q2t_ref[:, tq:w] = jnp.where(row >= ATT_HEAD_DIM, qt, zero)

    q_lo = past + i * tq
    n_full, _ = _attn_tile_counts(i, tq=tq, tk=tk, past=past, kv_len=kv_len, minimum=jnp.minimum)
    n_visits = n_full + 1

    def scores(j):
        kt = k_ref[pl.ds(pl.multiple_of(j * tk, tk), tk), :]
        return _dot(kt, q2t_ref[...])

    def visited_tile(v):
        return jnp.where(v == 0, n_full, jnp.maximum(v - 1, 0))

    def stage_a(v, buf):
        s_ref[buf] = scores(jnp.minimum(v - 1, n_full))

    rows = min(tk, ATT_SOFTMAX_ROWS)

    def stage_b(buf):
        for cb in range(w // LANES):
            sl = slice(cb * LANES, (cb + 1) * LANES)
            m_old = m_ref[:, sl]
            m_new = m_old
            for r0 in range(0, tk, rows):
                m_new = jnp.maximum(m_new, _column_max(s_ref[buf, r0:r0 + rows, sl]))
            m_ref[:, sl] = m_new
            alpha_ref[buf, :, sl] = jnp.exp2(m_old - m_new)
            for r0 in range(0, tk, rows):
                p_ref[buf, r0:r0 + rows, sl] = jnp.exp2((s_ref[buf, r0:r0 + rows, sl] - m_new).astype(BF16))

    def stage_c(v, buf):
        acc_ref[...] = alpha_ref[buf] * acc_ref[...] + _dot(vt_ref[visited_tile(v)], p_ref[buf])

    qchunk = (q_lo + lax.broadcasted_iota(jnp.int32, (1, w), 1) % tq) // CHUNK
    s_part = scores(n_full)
    for kb in range(tk // CHUNK):
        k0 = n_full * tk + kb * CHUNK
        kchunk = jnp.where(k0 < kv_len, k0 // CHUNK, jnp.iinfo(jnp.int32).max)
        visible = kchunk <= qchunk
        s_ref[0, kb * CHUNK:(kb + 1) * CHUNK, :] = jnp.where(visible, s_part[kb * CHUNK:(kb + 1) * CHUNK], NEG_BIG)
    m_ref[...] = jnp.full_like(m_ref, NEG_BIG)
    acc_ref[...] = jnp.zeros_like(acc_ref)
    p_ref[nbuf - 1] = jnp.zeros(p_ref.shape[1:], BF16)
    alpha_ref[nbuf - 1] = jnp.ones(alpha_ref.shape[1:], F32)

    def visits(v0, count, prefetch_last):
        for r in range(count):
            prefetch = r + 1 < count or prefetch_last
            if prefetch and nbuf > 1:
                stage_a(v0 + r + 1, (r + 1) % nbuf)
            stage_c(v0 + r - 1, (r - 1) % nbuf)
            stage_b(r)
            if prefetch and nbuf == 1:
                stage_a(v0 + r + 1, (r + 1) % nbuf)

    def trip(u, carry):
        visits(nbuf * u, nbuf, True)
        return carry

    lax.fori_loop(0, n_visits // nbuf, trip, 0)
    for rem in range(nbuf):

        @pl.when(n_visits % nbuf == rem)
        def _():
            visits(n_visits - rem, rem, False)
            stage_c(n_visits - 1, (rem - 1) % nbuf)

    lam_v = lam_ref[...]
    lam = (jnp.exp(jnp.sum(lam_v[0:1] * lam_v[1:2], axis=-1, keepdims=True))
           - jnp.exp(jnp.sum(lam_v[2:3] * lam_v[3:4], axis=-1, keepdims=True)) + lam_init)
    acc = acc_ref[...]
    o = acc[:ATT_V_DIM] / acc[ATT_V_DIM:ATT_V_DIM + 1]
    o = o[:, :tq] - lam * o[:, tq:]
    o = o * lax.rsqrt(jnp.mean(o * o, axis=0, keepdims=True) + ATT_NORM_EPS) * (sw_ref[...] * (1.0 - lam_init))
    o_ref[...] = o.T.astype(BF16)


def _diff_attn(qt, k, vt, lam_vecs, subln_col, *, nb, tq, tk, past, kv_len, lam_init, nbuf):
    nq = qt.shape[0] // nb
    nkt = vt.shape[0] // nb
    assert tk % CHUNK == 0 and kv_len % CHUNK == 0
    for i in range(nq):
        n_full, n_end = _attn_tile_counts(i, tq=tq, tk=tk, past=past, kv_len=kv_len)
        assert n_end - n_full == 1 and n_end <= nkt, (i, n_full, n_end)
    w = 2 * tq
    return pl.pallas_call(
        functools.partial(_diff_attn_kernel, tq=tq, tk=tk, past=past, kv_len=kv_len, lam_init=lam_init, nbuf=nbuf),
        grid=(nb, ATT_HEADS, nq),
        in_specs=[pl.BlockSpec((1, ATT_V_DIM, tq), lambda b, h, i: (b * nq + i, h, 0)),
                  pl.BlockSpec((nkt * tk, ATT_V_DIM), lambda b, h, i: (b, h)),
                  pl.BlockSpec((nkt, ATT_VT_ROWS, tk), lambda b, h, i: (b, h, 0)),
                  _const_spec((4, ATT_HEAD_DIM)), _const_spec((ATT_V_DIM, 1))],
        out_specs=pl.BlockSpec((tq, ATT_V_DIM), lambda b, h, i: (b * nq + i, h)),
        out_shape=jax.ShapeDtypeStruct((nb * nq * tq, ATT_WIDTH), BF16),
        scratch_shapes=[pltpu.VMEM((ATT_V_DIM, w), BF16), pltpu.VMEM((nbuf, tk, w), F32),
                        pltpu.VMEM((nbuf, tk, w), BF16), pltpu.VMEM((1, w), F32), pltpu.VMEM((nbuf, 1, w), F32),
                        pltpu.VMEM((ATT_VT_ROWS, w), F32)],
        compiler_params=pltpu.CompilerParams(dimension_semantics=("arbitrary", "arbitrary", "arbitrary"),
                                             vmem_limit_bytes=VMEM_LIMIT),
        name="diff_attn",
    )(qt, k, vt, lam_vecs, subln_col)


def _decode_attn_kernel(q_ref, kn_ref, vn_ref, kc_ref, vc_ref, lam_ref, sw_ref, o_ref, *, seq, past, lam_init):
    lam_v = lam_ref[...]
    lam = (jnp.exp(jnp.sum(lam_v[0:1] * lam_v[1:2], axis=-1, keepdims=True))
           - jnp.exp(jnp.sum(lam_v[2:3] * lam_v[3:4], axis=-1, keepdims=True)) + lam_init)
    npad = LANES
    lane = lax.broadcasted_iota(jnp.int32, (seq, LANES), 1)
    qchunk = (past + lax.broadcasted_iota(jnp.int32, (2 * seq, 1), 0) % seq) // CHUNK
    kpos_p = lax.broadcasted_iota(jnp.int32, (2 * seq, past), 1)
    kpos_n = lax.broadcasted_iota(jnp.int32, (2 * seq, npad), 1)
    vis_p = kpos_p // CHUNK <= qchunk
    vis_n = ((past + kpos_n) // CHUNK <= qchunk) & (kpos_n < seq)
    pad_rows = jnp.zeros((npad - seq, LANES), BF16)
    for h in range(ATT_HEADS):
        sl = slice(h * LANES, (h + 1) * LANES)
        qh = q_ref[:, sl]
        zero = jnp.zeros_like(qh)
        q2 = jnp.concatenate([jnp.where(lane < ATT_HEAD_DIM, qh, zero), jnp.where(lane >= ATT_HEAD_DIM, qh, zero)], axis=0)
        s_p = jnp.where(vis_p, _dot(q2, kc_ref[0, sl, :].astype(BF16)), NEG_BIG)
        kn = jnp.concatenate([kn_ref[:, sl].astype(BF16), pad_rows], axis=0)
        s_n = lax.dot_general(q2, kn, (((1,), (1,)), ((), ())), preferred_element_type=F32)
        s_n = jnp.where(vis_n, s_n, NEG_BIG)
        m = jnp.maximum(jnp.max(s_p, axis=-1, keepdims=True), jnp.max(s_n, axis=-1, keepdims=True))
        p_p = jnp.exp2(s_p - m)
        p_n = jnp.exp2(s_n - m)
        l = jnp.sum(p_p, axis=-1, keepdims=True) + jnp.sum(p_n, axis=-1, keepdims=True)
        vh = vc_ref[0, pl.ds(h, past, stride=ATT_HEADS), :].astype(BF16)
        vn = jnp.concatenate([vn_ref[:, sl].astype(BF16), pad_rows], axis=0)
        o = (_dot(p_p.astype(BF16), vh) + _dot(p_n.astype(BF16), vn)) / l
        o = o[:seq] - lam * o[seq:]
        o = o * lax.rsqrt(jnp.mean(o * o, axis=-1, keepdims=True) + ATT_NORM_EPS) * (sw_ref[...] * (1.0 - lam_init))
        o_ref[:, sl] = o.astype(BF16)


def _decode_attn(q, k_new, v_new, kt_cache, v_cache, lam_vecs, subln_row, *, nb, seq, past, lam_init):
    row = pl.BlockSpec((seq, ATT_WIDTH), lambda b: (b, 0))
    return pl.pallas_call(
        functools.partial(_decode_attn_kernel, seq=seq, past=past, lam_init=lam_init),
        grid=(nb,),
        in_specs=[row, row, row,
                  pl.BlockSpec((1, ATT_WIDTH, past), lambda b: (b, 0, 0)),
                  pl.BlockSpec((1, past * ATT_HEADS, ATT_V_DIM), lambda b: (b, 0, 0)),
                  _const_spec((4, ATT_HEAD_DIM)), _const_spec((1, ATT_V_DIM))],
        out_specs=row,
        out_shape=jax.ShapeDtypeStruct((nb * seq, ATT_WIDTH), BF16),
        compiler_params=pltpu.CompilerParams(dimension_semantics=("arbitrary",), vmem_limit_bytes=VMEM_LIMIT),
        name="decode_attn",
    )(q, k_new, v_new, kt_cache, v_cache, lam_vecs, subln_row)


def _post_mix_kernel(x_ref, ys_ref, ya_ref, mk_ref, mv_ref, wo_ref, wq_ref, wox_ref,
                     g1_ref, g2_ref, g3_ref, h_ref, *, spt, rows):
    mix = _dot(ys_ref[...], wo_ref[0:SSD_WIDTH, :]) + _dot(ya_ref[...], wo_ref[SSD_WIDTH:, :])
    h = x_ref[...] + _rms(mix, g1_ref[...], NORM_EPS)
    qn = _rms(h, g2_ref[...], NORM_EPS).astype(BF16)
    qx = (_dot(qn, wq_ref[...]) * (1.0 / math.sqrt(MEM_HEAD_DIM))).astype(BF16)
    ox_seqs = []
    for s_i in range(spt):
        qs = qx[s_i * rows:(s_i + 1) * rows]
        mk = mk_ref[s_i]
        mv = mv_ref[s_i]
        oxs = []
        for hd in range(MEM_HEADS):
            sl = slice(hd * MEM_HEAD_DIM, (hd + 1) * MEM_HEAD_DIM)
            s = lax.dot_general(qs[:, sl], mk[:, sl], (((1,), (1,)), ((), ())), preferred_element_type=F32)
            p = jnp.exp(s - jnp.max(s, axis=-1, keepdims=True))
            ox = _dot(p.astype(BF16), mv[:, sl]) / jnp.sum(p, axis=-1, keepdims=True)
            oxs.append(ox.astype(BF16))
        ox_seqs.append(jnp.concatenate(oxs, axis=1))
    ox_all = ox_seqs[0] if spt == 1 else jnp.concatenate(ox_seqs, axis=0)
    o2 = _dot(ox_all, wox_ref[...])
    h_ref[...] = h + _rms(o2, g3_ref[...], NORM_EPS)


def _post_mix(x, ys, ya, mk, mv, w_out, wq, wox, g1, g2, g3, *, seq, tm):
    n = x.shape[0]
    spt = max(1, tm // seq)
    tiles_per_seq = max(1, seq // tm)
    row = lambda c: pl.BlockSpec((tm, c), lambda i: (i, 0))
    mem = pl.BlockSpec((spt, MEM_LEN, D_MODEL), lambda i: (i // tiles_per_seq, 0, 0))
    wspec = _const_spec((D_MODEL, D_MODEL))
    gspec = _const_spec((1, D_MODEL))
    return pl.pallas_call(
        functools.partial(_post_mix_kernel, spt=spt, rows=tm // spt),
        grid=(n // tm,),
        in_specs=[row(D_MODEL), row(SSD_WIDTH), row(ATT_WIDTH), mem, mem, wspec, wspec, wspec, gspec, gspec, gspec],
        out_specs=row(D_MODEL),
        out_shape=jax.ShapeDtypeStruct((n, D_MODEL), F32),
        compiler_params=pltpu.CompilerParams(dimension_semantics=("arbitrary",), vmem_limit_bytes=VMEM_LIMIT),
        name="post_mix",
    )(x, ys, ya, mk, mv, w_out, wq, wox, g1, g2, g3)


def _ffn_kernel(h_ref, wg_ref, wu_ref, wd_ref, g1_ref, g2_ref, o_ref):
    h = h_ref[...]
    hn = _rms(h, g1_ref[...], NORM_EPS).astype(BF16)
    act = (_silu(_dot(hn, wg_ref[...])) * _dot(hn, wu_ref[...])).astype(BF16)
    f = _dot(act, wd_ref[...])
    o_ref[...] = h + _rms(f, g2_ref[...], NORM_EPS)


def _ffn(h, wg, wu, wd, g1, g2, *, tm):
    n = h.shape[0]
    row = pl.BlockSpec((tm, D_MODEL), lambda i: (i, 0))
    return pl.pallas_call(
        _ffn_kernel,
        grid=(n // tm,),
        in_specs=[row, _const_spec((D_MODEL, FFN_HIDDEN)), _const_spec((D_MODEL, FFN_HIDDEN)),
                  _const_spec((FFN_HIDDEN, D_MODEL)), _const_spec((1, D_MODEL)), _const_spec((1, D_MODEL))],
        out_specs=row,
        out_shape=jax.ShapeDtypeStruct((n, D_MODEL), F32),
        compiler_params=pltpu.CompilerParams(dimension_semantics=("arbitrary",), vmem_limit_bytes=VMEM_LIMIT),
        name="ffn",
    )(h, wg, wu, wd, g1, g2)


def _rope_tables(past, seq):
    half = ATT_HEAD_DIM // 2
    inv = jnp.power(ROPE_THETA, -jnp.arange(0, ATT_HEAD_DIM, 2, dtype=F32) / ATT_HEAD_DIM)
    pos = (past + jnp.arange(seq, dtype=jnp.int32)).astype(F32)
    ang = pos[:, None] * inv[None, :]
    cos, sin = jnp.cos(ang), jnp.sin(ang)
    reps = LANES // ATT_HEAD_DIM
    assert half * 2 == ATT_HEAD_DIM
    return jnp.tile(jnp.concatenate([cos, cos], axis=-1), (1, reps)), jnp.tile(jnp.concatenate([-sin, sin], axis=-1), (1, reps))


def _state_to_kernel_layout(s):
    b = s.shape[0]
    s = s.reshape(b, SSD_GROUPS, SSD_REP, SSD_HEAD_DIM, SSD_STATE)
    return s.transpose(0, 1, 4, 2, 3).reshape(b, SSD_GROUPS, SSD_STATE, SSD_REP * SSD_HEAD_DIM)


def _state_from_kernel_layout(s):
    b = s.shape[0]
    s = s.reshape(b, SSD_GROUPS, SSD_STATE, SSD_REP, SSD_HEAD_DIM)
    return s.transpose(0, 1, 3, 4, 2).reshape(b, SSD_HEADS, SSD_HEAD_DIM, SSD_STATE)


def _layer(x, conv_buf, ssm0, kt_past, v_past, mem_kb, mem_vb, lam_init, p, *, tm, tk, tq, ssd_q):
    nb, seq, _ = x.shape
    n = nb * seq
    past = 0 if kt_past is None else kt_past.shape[2]
    xf = x.reshape(n, D_MODEL)
    cos, sin = _rope_tables(past, seq)
    no_history = kt_past is None
    assert tq == tk or not no_history
    z, xbc, tail, dt, q, k, v, qt, kb, vt = _in_proj(xf, p["g_pre_mix"], p["w_in"], cos, sin, seq=seq, tm=tm,
                                                      tt=tk if no_history else 0)

    seq_pad = -(-seq // ssd_q) * ssd_q
    if seq_pad != seq:
        pad = lambda a: jnp.pad(a.reshape(nb, seq, -1), ((0, 0), (0, seq_pad - seq), (0, 0))).reshape(nb * seq_pad, -1)
        z_s, xbc_s, dt_s = pad(z), pad(xbc), pad(dt)
    else:
        z_s, xbc_s, dt_s = z, xbc, dt
    cbuf = jnp.pad(conv_buf.astype(F32), ((0, 0), (8 - (SSD_CONV - 1), 0), (0, 0)))
    y_ssd, h_new = _ssd(z_s, xbc_s, dt_s, cbuf, _state_to_kernel_layout(ssm0.astype(F32)),
                        p["conv_w"], p["conv_b"], p["dt_bias"], p["a_log"], p["d_skip"], p["ssm_norm_w"], p["e2"],
                        nb=nb, seq=seq_pad, q=ssd_q, valid=None if seq_pad == seq else seq)
    if seq_pad != seq:
        y_ssd = y_ssd.reshape(nb, seq_pad, SSD_WIDTH)[:, :seq].reshape(n, SSD_WIDTH)
    ssm_new = _state_from_kernel_layout(h_new)
    ext_tail = tail.reshape(nb, -1, 8, SSD_XBC)[:, -1]
    if seq >= SSD_CONV - 1:
        conv_new = ext_tail[:, 8 - (SSD_CONV - 1):]
    else:
        conv_new = jnp.concatenate([conv_buf.astype(F32), ext_tail[:, 8 - seq:]], axis=1)[:, -(SSD_CONV - 1):]

    if no_history:
        kv_len = seq
        y_att = _diff_attn(qt, kb, vt, p["lam_vecs"], p["subln_col"], nb=nb, tq=tq, tk=tk,
                           past=past, kv_len=kv_len, lam_init=lam_init, nbuf=ATT_PIPELINE_BUFFERS)
    else:
        y_att = _decode_attn(q, k, v, kt_past, v_past, p["lam_vecs"], p["subln_col"].reshape(1, ATT_V_DIM),
                             nb=nb, seq=seq, past=past, lam_init=lam_init)

    h = _post_mix(xf, y_ssd, y_att, mem_kb, mem_vb, p["w_out"], p["wq_x"], p["wo_x"],
                  p["g_post_mix"], p["g_pre_x"], p["g_post_x"], seq=seq, tm=tm)
    out = _ffn(h, p["w_gate"], p["w_up"], p["w_down"], p["g_pre_ffn"], p["g_post_ffn"], tm=tm)
    if no_history:
        k_out = k.reshape(nb, ATT_HEADS, 2, ATT_HEAD_DIM, seq).transpose(0, 4, 1, 2, 3)
    else:
        k_out = k.reshape(nb, seq, ATT_HEADS, 2, ATT_HEAD_DIM)
    return (out.reshape(nb, seq, D_MODEL), k_out,
            v.reshape(nb, seq, ATT_HEADS, ATT_V_DIM), ssm_new, conv_new)


def _prep_params(i, w_in, conv_w, conv_b, dt_bias, a_log, d_skip, ssm_norm_w, lam_q1, lam_k1, lam_q2, lam_k2, subln_w,
                 w_out, wq_x, wo_x, g_pre_mix, g_post_mix, g_pre_x, g_post_x, g_pre_ffn, g_post_ffn,
                 w_gate, w_up, w_down):
    w = w_in[i]
    s0 = SSD_WIDTH + SSD_XBC
    s1 = s0 + SSD_HEADS
    w_r = jnp.concatenate([w[:, :s0], w[:, s1:], w[:, s0:s1], jnp.zeros((D_MODEL, DT_PAD - SSD_HEADS), w.dtype)], axis=1)
    head_pad = lambda a: jnp.pad(a[i].astype(F32), (0, DT_PAD - SSD_HEADS)).reshape(1, DT_PAD)
    e = (jnp.arange(DT_PAD)[:, None] == (jnp.arange(SSD_WIDTH)[None, :] // SSD_HEAD_DIM)).astype(BF16)
    row = lambda a: a[i].astype(F32).reshape(1, -1)
    return {
        "w_in": w_r.astype(BF16), "conv_w": conv_w[i].astype(F32), "conv_b": row(conv_b),
        "dt_bias": head_pad(dt_bias), "a_log": head_pad(a_log), "d_skip": head_pad(d_skip),
        "ssm_norm_w": row(ssm_norm_w), "e2": jnp.concatenate([e, e], axis=0),
        "lam_vecs": jnp.stack([lam_q1[i], lam_k1[i], lam_q2[i], lam_k2[i]]).astype(F32), "subln_col": subln_w[i].astype(F32).reshape(ATT_V_DIM, 1),
        "w_out": w_out[i].astype(BF16), "wq_x": wq_x[i].astype(BF16), "wo_x": wo_x[i].astype(BF16),
        "g_pre_mix": row(g_pre_mix), "g_post_mix": row(g_post_mix), "g_pre_x": row(g_pre_x), "g_post_x": row(g_post_x),
        "g_pre_ffn": row(g_pre_ffn), "g_post_ffn": row(g_post_ffn),
        "w_gate": w_gate[i].astype(BF16), "w_up": w_up[i].astype(BF16), "w_down": w_down[i].astype(BF16),
    }


def kernel(x_prompt, x_sample, cache_attn_k, cache_attn_v, state_ssm, state_conv, cache_mem_k, cache_mem_v, mem_prompt, w_in, conv_w, conv_b, dt_bias, a_log, d_skip, ssm_norm_w, lam_q1, lam_k1, lam_q2, lam_k2, subln_w, w_out, g_mem, wq_x, wk_x, wv_x, wo_x, g_pre_mix, g_post_mix, g_pre_x, g_post_x, g_pre_ffn, g_post_ffn, w_gate, w_up, w_down):
    depth = w_in.shape[0]
    bp, sp, _ = x_prompt.shape
    bs, ss, _ = x_sample.shape
    hp, hs = x_prompt, x_sample
    outs = [[] for _ in range(10)]
    for i in range(depth):
        lam_init = 0.8 - 0.6 * math.exp(-0.3 * i)
        p = _prep_params(i, w_in, conv_w, conv_b, dt_bias, a_log, d_skip, ssm_norm_w, lam_q1, lam_k1, lam_q2, lam_k2,
                         subln_w, w_out, wq_x, wo_x, g_pre_mix, g_post_mix, g_pre_x, g_post_x, g_pre_ffn, g_post_ffn,
                         w_gate, w_up, w_down)
        mk, mv, mkb, mvb = _mem_kv(mem_prompt.reshape(bp * MEM_LEN, D_MODEL), g_mem[i].reshape(1, D_MODEL),
                                   wk_x[i].astype(BF16), wv_x[i].astype(BF16))
        hp, k_new, v_new, ssm_new, conv_new = _layer(
            hp, jnp.zeros((bp, SSD_CONV - 1, SSD_XBC), F32), jnp.zeros((bp, SSD_HEADS, SSD_HEAD_DIM, SSD_STATE), F32),
            None, None, mkb.reshape(bp, MEM_LEN, D_MODEL), mvb.reshape(bp, MEM_LEN, D_MODEL), lam_init, p,
            tm=512, tk=512, tq=512, ssd_q=256)
        for lst, val in zip(outs[:6], (k_new, v_new, ssm_new, conv_new,
                                       mk.reshape(bp, MEM_LEN, MEM_HEADS, MEM_HEAD_DIM),
                                       mv.reshape(bp, MEM_LEN, MEM_HEADS, MEM_HEAD_DIM))):
            lst.append(val)
        past = cache_attn_k.shape[2]
        hs, k_new, v_new, ssm_new, conv_new = _layer(
            hs, state_conv[i], state_ssm[i],
            cache_attn_k[i].transpose(0, 2, 3, 4, 1).reshape(bs, ATT_WIDTH, past),
            cache_attn_v[i].reshape(bs, past * ATT_HEADS, ATT_V_DIM),
            cache_mem_k[i].reshape(bs, MEM_LEN, D_MODEL).astype(BF16),
            cache_mem_v[i].reshape(bs, MEM_LEN, D_MODEL).astype(BF16), lam_init, p,
            tm=bs * ss, tk=0, tq=0, ssd_q=128)
        for lst, val in zip(outs[6:], (k_new, v_new, ssm_new, conv_new)):
            lst.append(val)
    return (hp, hs) + tuple(jnp.stack(o) for o in outs)
```

```python
import functools
import math

import jax
import jax.numpy as jnp
from jax import lax
from jax.experimental import pallas as pl
from jax.experimental.pallas import tpu as pltpu

D_MODEL = 1024
CHUNK = 64
SSD_WIDTH = 512
SSD_HEAD_DIM = 64
SSD_HEADS = 8
SSD_GROUPS = 2
SSD_REP = 4
SSD_STATE = 128
SSD_CONV = 4
SSD_XBC = 1024
SSD_NORM_GROUP = 256
SSD_NORM_EPS = 1e-5
ATT_WIDTH = 512
ATT_HEAD_DIM = 64
ATT_HEADS = 4
ATT_V_DIM = 128
ATT_NORM_EPS = 1e-5
ROPE_THETA = 10000.0
MEM_LEN = 256
MEM_HEADS = 4
MEM_HEAD_DIM = 256
FFN_HIDDEN = 2816
NORM_EPS = 1e-6
LANES = 128
DT_PAD = LANES
IN_COLS_PADDED = SSD_WIDTH + SSD_XBC + 3 * ATT_WIDTH + DT_PAD
VMEM_LIMIT = 56 * 1024 * 1024
NEG_BIG = -1e30
MXU_TILE = 256
BF16_SUBLANES = 16
ATT_VT_ROWS = ATT_V_DIM + BF16_SUBLANES
ATT_SOFTMAX_ROWS = 128
ATT_PIPELINE_BUFFERS = 2
Q_SCALE = math.log2(math.e) / math.sqrt(ATT_HEAD_DIM)

F32 = jnp.float32
BF16 = jnp.bfloat16


def _const_spec(shape):
    return pl.BlockSpec(shape, lambda *_: (0,) * len(shape), pipeline_mode=pl.Buffered(1))


def _rms(x, g, eps):
    return x * lax.rsqrt(jnp.mean(x * x, axis=-1, keepdims=True) + eps) * g


def _silu(x):
    return x / (1.0 + jnp.exp(-x))


def _dot(a, b):
    return jnp.dot(a, b, preferred_element_type=F32)


def _mem_kv_kernel(mem_ref, g_ref, wk_ref, wv_ref, mk_ref, mv_ref, mkb_ref, mvb_ref):
    mn = _rms(mem_ref[...], g_ref[...], NORM_EPS).astype(BF16)
    mk = _dot(mn, wk_ref[...])
    mv = _dot(mn, wv_ref[...])
    mk_ref[...] = mk
    mv_ref[...] = mv
    mkb_ref[...] = mk.astype(BF16)
    mvb_ref[...] = mv.astype(BF16)


def _mem_kv(mem, g_mem, wk, wv):
    n = mem.shape[0]
    tm = MEM_LEN
    row = pl.BlockSpec((tm, D_MODEL), lambda i: (i, 0))
    return pl.pallas_call(
        _mem_kv_kernel,
        grid=(n // tm,),
        in_specs=[row, _const_spec((1, D_MODEL)), _const_spec((D_MODEL, D_MODEL)), _const_spec((D_MODEL, D_MODEL))],
        out_specs=[row, row, row, row],
        out_shape=[jax.ShapeDtypeStruct((n, D_MODEL), F32), jax.ShapeDtypeStruct((n, D_MODEL), F32),
                   jax.ShapeDtypeStruct((n, D_MODEL), BF16), jax.ShapeDtypeStruct((n, D_MODEL), BF16)],
        compiler_params=pltpu.CompilerParams(dimension_semantics=("arbitrary",), vmem_limit_bytes=VMEM_LIMIT),
        name="mem_kv",
    )(mem, g_mem, wk, wv)


def _in_proj_kernel(x_ref, g_ref, w_ref, cos_ref, sin_ref,
                    z_ref, xbc_ref, tail_ref, dt_ref, q_ref, k_ref, v_ref, qt_ref, kb_ref, vt_ref, *, tm, tt, spt):
    hn = _rms(x_ref[...], g_ref[...], NORM_EPS).astype(BF16)
    acc = _dot(hn, w_ref[...])
    o = 0
    z_ref[...] = acc[:, o:o + SSD_WIDTH].astype(BF16)
    o += SSD_WIDTH
    xbc = acc[:, o:o + SSD_XBC]
    xbc_ref[...] = xbc.astype(BF16)
    rows = tm // spt
    for s_i in range(spt):
        tail_ref[s_i] = xbc[(s_i + 1) * rows - 8:(s_i + 1) * rows, :]
    o += SSD_XBC
    q = acc[:, o:o + ATT_WIDTH]
    o += ATT_WIDTH
    k = acc[:, o:o + ATT_WIDTH]
    o += ATT_WIDTH
    v = acc[:, o:o + ATT_WIDTH]
    o += ATT_WIDTH
    dt_ref[...] = acc[:, o:o + DT_PAD]
    if tt:
        for j in range(ATT_HEADS):
            v_ref[pl.ds(j, tm, stride=ATT_HEADS), :] = v[:, j * ATT_V_DIM:(j + 1) * ATT_V_DIM]
    else:
        v_ref[...] = v

    cos = cos_ref[...]
    sin = sin_ref[...]
    first_half = (lax.broadcasted_iota(jnp.int32, (tm, LANES), 1) % ATT_HEAD_DIM) < (ATT_HEAD_DIM // 2)

    def rope(t):
        swapped = jnp.where(first_half, pltpu.roll(t, LANES - ATT_HEAD_DIM // 2, 1),
                            pltpu.roll(t, ATT_HEAD_DIM // 2, 1))
        return t * cos + swapped * sin

    for j in range(ATT_WIDTH // LANES):
        sl = slice(j * LANES, (j + 1) * LANES)
        qr = rope(q[:, sl]) * Q_SCALE
        kr = rope(k[:, sl])
        q_ref[:, sl] = qr.astype(BF16)
        if tt:
            k_ref[0, sl, :] = kr.T
            kb_ref[:, sl] = kr.astype(BF16)
            for c in range(tm // tt):
                qt_ref[c, sl, :] = qr[c * tt:(c + 1) * tt, :].T.astype(BF16)
                vt_ref[c, j * ATT_VT_ROWS:j * ATT_VT_ROWS + LANES, :] = v[c * tt:(c + 1) * tt, sl].T.astype(BF16)
                vt_ref[c, j * ATT_VT_ROWS + LANES:(j + 1) * ATT_VT_ROWS, :] = jnp.ones((ATT_VT_ROWS - LANES, tt), BF16)
        else:
            k_ref[:, sl] = kr


def _in_proj(x, g, w, cos, sin, *, seq, tm, tt):
    n = x.shape[0]
    nt = n // tm
    spt = max(1, tm // seq)
    tiles_per_seq = max(1, seq // tm)
    if spt > 1:
        cos, sin = jnp.tile(cos, (spt, 1)), jnp.tile(sin, (spt, 1))
    row = lambda c: pl.BlockSpec((tm, c), lambda i: (i, 0))
    tab = pl.BlockSpec((tm, LANES), lambda i: (i % tiles_per_seq, 0))
    if tt:
        k_spec = pl.BlockSpec((1, ATT_WIDTH, tm), lambda i: (i // tiles_per_seq, 0, i % tiles_per_seq))
        k_shape = jax.ShapeDtypeStruct((n // seq, ATT_WIDTH, seq), F32)
        v_spec = pl.BlockSpec((tm * ATT_HEADS, ATT_V_DIM), lambda i: (i, 0))
        v_shape = jax.ShapeDtypeStruct((n * ATT_HEADS, ATT_V_DIM), F32)
    else:
        k_spec, k_shape = row(ATT_WIDTH), jax.ShapeDtypeStruct((n, ATT_WIDTH), F32)
        v_spec, v_shape = row(ATT_WIDTH), jax.ShapeDtypeStruct((n, ATT_WIDTH), F32)
    out_specs = [row(SSD_WIDTH), row(SSD_XBC), pl.BlockSpec((spt, 8, SSD_XBC), lambda i: (i, 0, 0)), row(DT_PAD),
                 row(ATT_WIDTH), k_spec, v_spec]
    out_shape = [jax.ShapeDtypeStruct((n, SSD_WIDTH), BF16), jax.ShapeDtypeStruct((n, SSD_XBC), BF16),
                 jax.ShapeDtypeStruct((nt * spt, 8, SSD_XBC), F32), jax.ShapeDtypeStruct((n, DT_PAD), F32),
                 jax.ShapeDtypeStruct((n, ATT_WIDTH), BF16), k_shape, v_shape]
    if tt:
        tr = lambda r: pl.BlockSpec((tm // tt, r, tt), lambda i: (i, 0, 0))
        vt_rows = ATT_HEADS * ATT_VT_ROWS
        out_specs += [tr(ATT_WIDTH), row(ATT_WIDTH), tr(vt_rows)]
        out_shape += [jax.ShapeDtypeStruct((n // tt, ATT_WIDTH, tt), BF16), jax.ShapeDtypeStruct((n, ATT_WIDTH), BF16),
                      jax.ShapeDtypeStruct((n // tt, vt_rows, tt), BF16)]

    def body(*refs):
        refs = refs + (None,) * (15 - len(refs))
        _in_proj_kernel(*refs, tm=tm, tt=tt, spt=spt)

    outs = pl.pallas_call(
        body,
        grid=(nt,),
        in_specs=[row(D_MODEL), _const_spec((1, D_MODEL)), _const_spec((D_MODEL, IN_COLS_PADDED)), tab, tab],
        out_specs=out_specs,
        out_shape=out_shape,
        compiler_params=pltpu.CompilerParams(dimension_semantics=("arbitrary",), vmem_limit_bytes=VMEM_LIMIT),
        name="in_proj",
    )(x, g, w, cos, sin)
    return list(outs) + [None] * (10 - len(outs))


def _expand_heads(x, e2):
    hi = x.astype(BF16)
    lo = (x - hi.astype(F32)).astype(BF16)
    return _dot(jnp.concatenate([hi, lo], axis=1), e2)


def _ssd_kernel(z_ref, xbc_ref, dt_ref, cbuf_ref, h0_ref, cw_ref, cb_ref, dtb_ref, alog_ref, dsk_ref, nw_ref, e2_ref,
                y_ref, hout_ref, state, ext, *, q, valid):
    c = pl.program_id(1)

    @pl.when(c == 0)
    def _():
        state[...] = h0_ref[0]
        ext[0:8, :] = cbuf_ref[0]

    ext[8:8 + q, :] = xbc_ref[...].astype(F32)
    conv = cb_ref[...] + ext[5:5 + q, :] * cw_ref[0:1, :]
    for w in range(1, SSD_CONV):
        conv = conv + ext[5 + w:5 + w + q, :] * cw_ref[w:w + 1, :]
    ext[0:8, :] = ext[q:q + 8, :]
    u = _silu(conv)
    xs = u[:, :SSD_WIDTH]
    bm = u[:, SSD_WIDTH:SSD_WIDTH + SSD_GROUPS * SSD_STATE]
    cm = u[:, SSD_WIDTH + SSD_GROUPS * SSD_STATE:]

    dtr = dt_ref[...] + dtb_ref[...]
    dt = jnp.maximum(dtr, 0.0) + jnp.log(1.0 + jnp.exp(-jnp.abs(dtr)))
    if valid is not None:
        row = lax.broadcasted_iota(jnp.int32, (q, DT_PAD), 0) + c * q
        dt = jnp.where(row < valid, dt, 0.0)
    a = -jnp.exp(alog_ref[...])
    ad = dt * a
    ri = lax.broadcasted_iota(jnp.int32, (q, q), 0)
    ci = lax.broadcasted_iota(jnp.int32, (q, q), 1)
    tril = ri >= ci
    acum = jnp.dot(tril.astype(F32), ad, preferred_element_type=F32, precision=lax.Precision.HIGHEST)
    acum_t = acum.T
    tot = acum[q - 1:q, :]
    e2 = e2_ref[...]
    expanded = _expand_heads(jnp.concatenate([dt, dt * jnp.exp(tot - acum), jnp.exp(acum)], axis=0), e2)
    dtx = expanded[0:q]
    ddx = expanded[q:2 * q]
    eax = expanded[2 * q:3 * q]
    dsk = _expand_heads(jnp.broadcast_to(dsk_ref[...], (8, DT_PAD)), e2)[0:1]

    xdt = (xs * dtx).astype(BF16)
    xdtd = (xs * ddx).astype(BF16)
    bm_t = bm.T.astype(BF16)
    cmb = cm.astype(BF16)
    gw = SSD_REP * SSD_HEAD_DIM
    stripe = lax.broadcasted_iota(jnp.int32, (q, gw), 1) // SSD_HEAD_DIM
    ys = []
    for g in range(SSD_GROUPS):
        cm_g = cmb[:, g * SSD_STATE:(g + 1) * SSD_STATE]
        bt_g = bm_t[g * SSD_STATE:(g + 1) * SSD_STATE, :]
        cbm = _dot(cm_g, bt_g)
        ms = []
        for r in range(SSD_REP):
            h = g * SSD_REP + r
            diff = acum[:, h:h + 1] - acum_t[h:h + 1, :]
            ms.append((cbm * jnp.exp(jnp.where(tril, diff, -jnp.inf))).astype(BF16))
        ydf = _dot(jnp.concatenate(ms, axis=0), xdt[:, g * gw:(g + 1) * gw])
        yd = ydf[0:q]
        for r in range(1, SSD_REP):
            yd = jnp.where(stripe == r, ydf[r * q:(r + 1) * q], yd)
        st = state[g]
        y_off = _dot(cm_g, st.astype(BF16)) * eax[:, g * gw:(g + 1) * gw]
        state[g] = st * eax[q - 1:q, g * gw:(g + 1) * gw] + _dot(bt_g, xdtd[:, g * gw:(g + 1) * gw])
        ys.append(yd + y_off)
    y = jnp.concatenate(ys, axis=1) + dsk * xs
    y = y * _silu(z_ref[...].astype(F32))
    outs = []
    for g in range(SSD_WIDTH // SSD_NORM_GROUP):
        yg = y[:, g * SSD_NORM_GROUP:(g + 1) * SSD_NORM_GROUP]
        outs.append(yg * lax.rsqrt(jnp.mean(yg * yg, axis=-1, keepdims=True) + SSD_NORM_EPS))
    y_ref[...] = (jnp.concatenate(outs, axis=1) * nw_ref[...]).astype(BF16)

    @pl.when(c == pl.num_programs(1) - 1)
    def _():
        hout_ref[0] = state[...]


def _ssd(z, xbc, dt, cbuf, h0, cw, cb, dtb, alog, dsk, nw, e2, *, nb, seq, q, valid):
    nc = seq // q
    row = lambda c: pl.BlockSpec((q, c), lambda b, i: (b * nc + i, 0))
    gw = SSD_REP * SSD_HEAD_DIM
    st_spec = pl.BlockSpec((1, SSD_GROUPS, SSD_STATE, gw), lambda b, i: (b, 0, 0, 0))
    return pl.pallas_call(
        functools.partial(_ssd_kernel, q=q, valid=valid),
        grid=(nb, nc),
        in_specs=[row(SSD_WIDTH), row(SSD_XBC), row(DT_PAD),
                  pl.BlockSpec((1, 8, SSD_XBC), lambda b, i: (b, 0, 0)), st_spec,
                  _const_spec((SSD_CONV, SSD_XBC)), _const_spec((1, SSD_XBC)), _const_spec((1, DT_PAD)),
                  _const_spec((1, DT_PAD)), _const_spec((1, DT_PAD)), _const_spec((1, SSD_WIDTH)),
                  _const_spec((2 * DT_PAD, SSD_WIDTH))],
        out_specs=[row(SSD_WIDTH), st_spec],
        out_shape=[jax.ShapeDtypeStruct((nb * seq, SSD_WIDTH), BF16),
                   jax.ShapeDtypeStruct((nb, SSD_GROUPS, SSD_STATE, gw), F32)],
        scratch_shapes=[pltpu.VMEM((SSD_GROUPS, SSD_STATE, gw), F32), pltpu.VMEM((q + 8, SSD_XBC), F32)],
        compiler_params=pltpu.CompilerParams(dimension_semantics=("arbitrary", "arbitrary"),
                                             vmem_limit_bytes=VMEM_LIMIT),
        name="ssd",
    )(z, xbc, dt, cbuf, h0, cw, cb, dtb, alog, dsk, nw, e2)


def _column_max(x):
    while x.shape[0] > 8 and x.shape[0] % 16 == 0:
        half = x.shape[0] // 2
        x = jnp.maximum(x[:half], x[half:])
    return jnp.max(x, axis=0, keepdims=True)


def _attn_tile_counts(i, *, tq, tk, past, kv_len, minimum=min):
    q_lo = past + i * tq
    q_hi = q_lo + tq - 1
    lim_lo = minimum((q_lo // CHUNK + 1) * CHUNK, kv_len)
    lim_hi = minimum((q_hi // CHUNK + 1) * CHUNK, kv_len)
    return lim_lo // tk, (lim_hi + tk - 1) // tk


def _diff_attn_kernel(qt_ref, k_ref, vt_ref, lam_ref, sw_ref, o_ref,
                      q2t_ref, s_ref, p_ref, m_ref, alpha_ref, acc_ref, *, tq, tk, past, kv_len, lam_init, nbuf):
    i = pl.program_id(2)
    w = 2 * tq
    qt = qt_ref[0]
    row = lax.broadcasted_iota(jnp.int32, (ATT_V_DIM, tq), 0)
    zero = jnp.zeros_like(qt)
    q2t_ref[:, 0:tq] = jnp.where(row < ATT_HEAD_DIM, qt, zero)
    q2t_ref[:, tq:w] = jnp.where(row >= ATT_HEAD_DIM, qt, zero)

    q_lo = past + i * tq
    n_full, _ = _attn_tile_counts(i, tq=tq, tk=tk, past=past, kv_len=kv_len, minimum=jnp.minimum)
    n_visits = n_full + 1

    def scores(j):
        kt = k_ref[pl.ds(pl.multiple_of(j * tk, tk), tk), :]
        return _dot(kt, q2t_ref[...])

    def visited_tile(v):
        return jnp.where(v == 0, n_full, jnp.maximum(v - 1, 0))

    ncb = w // LANES
    pw = p_ref.shape[-1]

    def store_scores(buf, s):
        for cb in range(ncb):
            s_ref[buf, cb] = s[:, cb * LANES:(cb + 1) * LANES]

    def stage_a(v, buf):
        store_scores(buf, scores(jnp.minimum(v - 1, n_full)))

    rows = min(tk, ATT_SOFTMAX_ROWS)

    def stage_b(buf):
        for cb in range(ncb):
            sl = slice(cb * LANES, (cb + 1) * LANES)
            m_old = m_ref[:, sl]
            m_new = m_old
            for r0 in range(0, tk, rows):
                m_new = jnp.maximum(m_new, _column_max(s_ref[buf, cb, r0:r0 + rows, :]))
            m_ref[:, sl] = m_new
            alpha_ref[buf, :, sl] = jnp.exp2(m_old - m_new)
            pl0 = (cb * LANES) % pw
            for r0 in range(0, tk, rows):
                p_ref[buf, cb * LANES // pw, r0:r0 + rows, pl0:pl0 + LANES] = jnp.exp2(
                    (s_ref[buf, cb, r0:r0 + rows, :] - m_new).astype(BF16))

    def stage_c(v, buf):
        vt = vt_ref[visited_tile(v)]
        for pb in range(w // pw):
            sl = slice(pb * pw, (pb + 1) * pw)
            acc_ref[:, sl] = alpha_ref[buf, :, sl] * acc_ref[:, sl] + _dot(vt, p_ref[buf, pb])

    qchunk = (q_lo + lax.broadcasted_iota(jnp.int32, (1, w), 1) % tq) // CHUNK
    s_part = scores(n_full)
    s_masked = []
    for kb in range(tk // CHUNK):
        k0 = n_full * tk + kb * CHUNK
        kchunk = jnp.where(k0 < kv_len, k0 // CHUNK, jnp.iinfo(jnp.int32).max)
        s_masked.append(jnp.where(kchunk <= qchunk, s_part[kb * CHUNK:(kb + 1) * CHUNK], NEG_BIG))
    store_scores(0, jnp.concatenate(s_masked, axis=0))
    m_ref[...] = jnp.full_like(m_ref, NEG_BIG)
    acc_ref[...] = jnp.zeros_like(acc_ref)
    p_ref[nbuf - 1] = jnp.zeros(p_ref.shape[1:], BF16)
    alpha_ref[nbuf - 1] = jnp.ones(alpha_ref.shape[1:], F32)

    def visits(v0, count, prefetch_last):
        for r in range(count):
            prefetch = r + 1 < count or prefetch_last
            if prefetch and nbuf > 1:
                stage_a(v0 + r + 1, (r + 1) % nbuf)
            stage_c(v0 + r - 1, (r - 1) % nbuf)
            stage_b(r)
            if prefetch and nbuf == 1:
                stage_a(v0 + r + 1, (r + 1) % nbuf)

    def trip(u, carry):
        visits(nbuf * u, nbuf, True)
        return carry

    lax.fori_loop(0, n_visits // nbuf, trip, 0)
    for rem in range(nbuf):

        @pl.when(n_visits % nbuf == rem)
        def _():
            visits(n_visits - rem, rem, False)
            stage_c(n_visits - 1, (rem - 1) % nbuf)

    lam_v = lam_ref[...]
    lam = (jnp.exp(jnp.sum(lam_v[0:1] * lam_v[1:2], axis=-1, keepdims=True))
           - jnp.exp(jnp.sum(lam_v[2:3] * lam_v[3:4], axis=-1, keepdims=True)) + lam_init)
    acc = acc_ref[...]
    o = acc[:ATT_V_DIM] / acc[ATT_V_DIM:ATT_V_DIM + 1]
    o = o[:, :tq] - lam * o[:, tq:]
    o = o * lax.rsqrt(jnp.mean(o * o, axis=0, keepdims=True) + ATT_NORM_EPS) * (sw_ref[...] * (1.0 - lam_init))
    o_ref[...] = o.T.astype(BF16)


def _diff_attn(qt, k, vt, lam_vecs, subln_col, *, nb, tq, tk, past, kv_len, lam_init, nbuf):
    nq = qt.shape[0] // nb
    nkt = vt.shape[0] // nb
    assert tk % CHUNK == 0 and kv_len % CHUNK == 0
    for i in range(nq):
        n_full, n_end = _attn_tile_counts(i, tq=tq, tk=tk, past=past, kv_len=kv_len)
        assert n_end - n_full == 1 and n_end <= nkt, (i, n_full, n_end)
    w = 2 * tq
    return pl.pallas_call(
        functools.partial(_diff_attn_kernel, tq=tq, tk=tk, past=past, kv_len=kv_len, lam_init=lam_init, nbuf=nbuf),
        grid=(nb, ATT_HEADS, nq),
        in_specs=[pl.BlockSpec((1, ATT_V_DIM, tq), lambda b, h, i: (b * nq + i, h, 0)),
                  pl.BlockSpec((nkt * tk, ATT_V_DIM), lambda b, h, i: (b, h)),
                  pl.BlockSpec((nkt, ATT_VT_ROWS, tk), lambda b, h, i: (b, h, 0)),
                  _const_spec((4, ATT_HEAD_DIM)), _const_spec((ATT_V_DIM, 1))],
        out_specs=pl.BlockSpec((tq, ATT_V_DIM), lambda b, h, i: (b * nq + i, h)),
        out_shape=jax.ShapeDtypeStruct((nb * nq * tq, ATT_WIDTH), BF16),
        scratch_shapes=[pltpu.VMEM((ATT_V_DIM, w), BF16), pltpu.VMEM((nbuf, w // LANES, tk, LANES), F32),
                        pltpu.VMEM((nbuf, w // MXU_TILE, tk, MXU_TILE), BF16), pltpu.VMEM((1, w), F32),
                        pltpu.VMEM((nbuf, 1, w), F32),
                        pltpu.VMEM((ATT_VT_ROWS, w), F32)],
        compiler_params=pltpu.CompilerParams(dimension_semantics=("arbitrary", "arbitrary", "arbitrary"),
                                             vmem_limit_bytes=VMEM_LIMIT),
        name="diff_attn",
    )(qt, k, vt, lam_vecs, subln_col)


def _decode_attn_kernel(q_ref, kn_ref, vn_ref, kc_ref, vc_ref, lam_ref, sw_ref, o_ref, *, seq, past, lam_init):
    lam_v = lam_ref[...]
    lam = (jnp.exp(jnp.sum(lam_v[0:1] * lam_v[1:2], axis=-1, keepdims=True))
           - jnp.exp(jnp.sum(lam_v[2:3] * lam_v[3:4], axis=-1, keepdims=True)) + lam_init)
    npad = LANES
    lane = lax.broadcasted_iota(jnp.int32, (seq, LANES), 1)
    qchunk = (past + lax.broadcasted_iota(jnp.int32, (2 * seq, 1), 0) % seq) // CHUNK
    kpos_p = lax.broadcasted_iota(jnp.int32, (2 * seq, past), 1)
    kpos_n = lax.broadcasted_iota(jnp.int32, (2 * seq, npad), 1)
    vis_p = kpos_p // CHUNK <= qchunk
    vis_n = ((past + kpos_n) // CHUNK <= qchunk) & (kpos_n < seq)
    pad_rows = jnp.zeros((npad - seq, LANES), BF16)
    for h in range(ATT_HEADS):
        sl = slice(h * LANES, (h + 1) * LANES)
        qh = q_ref[:, sl]
        zero = jnp.zeros_like(qh)
        q2 = jnp.concatenate([jnp.where(lane < ATT_HEAD_DIM, qh, zero), jnp.where(lane >= ATT_HEAD_DIM, qh, zero)], axis=0)
        s_p = jnp.where(vis_p, _dot(q2, kc_ref[0, sl, :].astype(BF16)), NEG_BIG)
        kn = jnp.concatenate([kn_ref[:, sl].astype(BF16), pad_rows], axis=0)
        s_n = lax.dot_general(q2, kn, (((1,), (1,)), ((), ())), preferred_element_type=F32)
        s_n = jnp.where(vis_n, s_n, NEG_BIG)
        m = jnp.maximum(jnp.max(s_p, axis=-1, keepdims=True), jnp.max(s_n, axis=-1, keepdims=True))
        p_p = jnp.exp2(s_p - m)
        p_n = jnp.exp2(s_n - m)
        l = jnp.sum(p_p, axis=-1, keepdims=True) + jnp.sum(p_n, axis=-1, keepdims=True)
        vh = vc_ref[0, pl.ds(h, past, stride=ATT_HEADS), :].astype(BF16)
        vn = jnp.concatenate([vn_ref[:, sl].astype(BF16), pad_rows], axis=0)
        o = (_dot(p_p.astype(BF16), vh) + _dot(p_n.astype(BF16), vn)) / l
        o = o[:seq] - lam * o[seq:]
        o = o * lax.rsqrt(jnp.mean(o * o, axis=-1, keepdims=True) + ATT_NORM_EPS) * (sw_ref[...] * (1.0 - lam_init))
        o_ref[:, sl] = o.astype(BF16)


def _decode_attn(q, k_new, v_new, kt_cache, v_cache, lam_vecs, subln_row, *, nb, seq, past, lam_init):
    row = pl.BlockSpec((seq, ATT_WIDTH), lambda b: (b, 0))
    return pl.pallas_call(
        functools.partial(_decode_attn_kernel, seq=seq, past=past, lam_init=lam_init),
        grid=(nb,),
        in_specs=[row, row, row,
                  pl.BlockSpec((1, ATT_WIDTH, past), lambda b: (b, 0, 0)),
                  pl.BlockSpec((1, past * ATT_HEADS, ATT_V_DIM), lambda b: (b, 0, 0)),
                  _const_spec((4, ATT_HEAD_DIM)), _const_spec((1, ATT_V_DIM))],
        out_specs=row,
        out_shape=jax.ShapeDtypeStruct((nb * seq, ATT_WIDTH), BF16),
        compiler_params=pltpu.CompilerParams(dimension_semantics=("arbitrary",), vmem_limit_bytes=VMEM_LIMIT),
        name="decode_attn",
    )(q, k_new, v_new, kt_cache, v_cache, lam_vecs, subln_row)


def _post_mix_kernel(x_ref, ys_ref, ya_ref, mk_ref, mv_ref, wo_ref, wq_ref, wox_ref,
                     g1_ref, g2_ref, g3_ref, h_ref, *, spt, rows):
    mix = _dot(ys_ref[...], wo_ref[0:SSD_WIDTH, :]) + _dot(ya_ref[...], wo_ref[SSD_WIDTH:, :])
    h = x_ref[...] + _rms(mix, g1_ref[...], NORM_EPS)
    qn = _rms(h, g2_ref[...], NORM_EPS).astype(BF16)
    qx = (_dot(qn, wq_ref[...]) * (1.0 / math.sqrt(MEM_HEAD_DIM))).astype(BF16)
    ox_seqs = []
    for s_i in range(spt):
        qs = qx[s_i * rows:(s_i + 1) * rows]
        mk = mk_ref[s_i]
        mv = mv_ref[s_i]
        oxs = []
        for hd in range(MEM_HEADS):
            sl = slice(hd * MEM_HEAD_DIM, (hd + 1) * MEM_HEAD_DIM)
            s = lax.dot_general(qs[:, sl], mk[:, sl], (((1,), (1,)), ((), ())), preferred_element_type=F32)
            p = jnp.exp(s - jnp.max(s, axis=-1, keepdims=True))
            ox = _dot(p.astype(BF16), mv[:, sl]) / jnp.sum(p, axis=-1, keepdims=True)
            oxs.append(ox.astype(BF16))
        ox_seqs.append(jnp.concatenate(oxs, axis=1))
    ox_all = ox_seqs[0] if spt == 1 else jnp.concatenate(ox_seqs, axis=0)
    o2 = _dot(ox_all, wox_ref[...])
    h_ref[...] = h + _rms(o2, g3_ref[...], NORM_EPS)


def _post_mix(x, ys, ya, mk, mv, w_out, wq, wox, g1, g2, g3, *, seq, tm):
    n = x.shape[0]
    spt = max(1, tm // seq)
    tiles_per_seq = max(1, seq // tm)
    row = lambda c: pl.BlockSpec((tm, c), lambda i: (i, 0))
    mem = pl.BlockSpec((spt, MEM_LEN, D_MODEL), lambda i: (i // tiles_per_seq, 0, 0))
    wspec = _const_spec((D_MODEL, D_MODEL))
    gspec = _const_spec((1, D_MODEL))
    return pl.pallas_call(
        functools.partial(_post_mix_kernel, spt=spt, rows=tm // spt),
        grid=(n // tm,),
        in_specs=[row(D_MODEL), row(SSD_WIDTH), row(ATT_WIDTH), mem, mem, wspec, wspec, wspec, gspec, gspec, gspec],
        out_specs=row(D_MODEL),
        out_shape=jax.ShapeDtypeStruct((n, D_MODEL), F32),
        compiler_params=pltpu.CompilerParams(dimension_semantics=("arbitrary",), vmem_limit_bytes=VMEM_LIMIT),
        name="post_mix",
    )(x, ys, ya, mk, mv, w_out, wq, wox, g1, g2, g3)


def _ffn_kernel(h_ref, wg_ref, wu_ref, wd_ref, g1_ref, g2_ref, o_ref):
    h = h_ref[...]
    hn = _rms(h, g1_ref[...], NORM_EPS).astype(BF16)
    act = (_silu(_dot(hn, wg_ref[...])) * _dot(hn, wu_ref[...])).astype(BF16)
    f = _dot(act, wd_ref[...])
    o_ref[...] = h + _rms(f, g2_ref[...], NORM_EPS)


def _ffn(h, wg, wu, wd, g1, g2, *, tm):
    n = h.shape[0]
    row = pl.BlockSpec((tm, D_MODEL), lambda i: (i, 0))
    return pl.pallas_call(
        _ffn_kernel,
        grid=(n // tm,),
        in_specs=[row, _const_spec((D_MODEL, FFN_HIDDEN)), _const_spec((D_MODEL, FFN_HIDDEN)),
                  _const_spec((FFN_HIDDEN, D_MODEL)), _const_spec((1, D_MODEL)), _const_spec((1, D_MODEL))],
        out_specs=row,
        out_shape=jax.ShapeDtypeStruct((n, D_MODEL), F32),
        compiler_params=pltpu.CompilerParams(dimension_semantics=("arbitrary",), vmem_limit_bytes=VMEM_LIMIT),
        name="ffn",
    )(h, wg, wu, wd, g1, g2)


def _rope_tables(past, seq):
    half = ATT_HEAD_DIM // 2
    inv = jnp.power(ROPE_THETA, -jnp.arange(0, ATT_HEAD_DIM, 2, dtype=F32) / ATT_HEAD_DIM)
    pos = (past + jnp.arange(seq, dtype=jnp.int32)).astype(F32)
    ang = pos[:, None] * inv[None, :]
    cos, sin = jnp.cos(ang), jnp.sin(ang)
    reps = LANES // ATT_HEAD_DIM
    assert half * 2 == ATT_HEAD_DIM
    return jnp.tile(jnp.concatenate([cos, cos], axis=-1), (1, reps)), jnp.tile(jnp.concatenate([-sin, sin], axis=-1), (1, reps))


def _state_to_kernel_layout(s):
    b = s.shape[0]
    s = s.reshape(b, SSD_GROUPS, SSD_REP, SSD_HEAD_DIM, SSD_STATE)
    return s.transpose(0, 1, 4, 2, 3).reshape(b, SSD_GROUPS, SSD_STATE, SSD_REP * SSD_HEAD_DIM)


def _state_from_kernel_layout(s):
    b = s.shape[0]
    s = s.reshape(b, SSD_GROUPS, SSD_STATE, SSD_REP, SSD_HEAD_DIM)
    return s.transpose(0, 1, 3, 4, 2).reshape(b, SSD_HEADS, SSD_HEAD_DIM, SSD_STATE)


def _layer(x, conv_buf, ssm0, kt_past, v_past, mem_kb, mem_vb, lam_init, p, *, tm, tk, tq, ssd_q):
    nb, seq, _ = x.shape
    n = nb * seq
    past = 0 if kt_past is None else kt_past.shape[2]
    xf = x.reshape(n, D_MODEL)
    cos, sin = _rope_tables(past, seq)
    no_history = kt_past is None
    assert tq == tk or not no_history
    z, xbc, tail, dt, q, k, v, qt, kb, vt = _in_proj(xf, p["g_pre_mix"], p["w_in"], cos, sin, seq=seq, tm=tm,
                                                      tt=tk if no_history else 0)

    seq_pad = -(-seq // ssd_q) * ssd_q
    if seq_pad != seq:
        pad = lambda a: jnp.pad(a.reshape(nb, seq, -1), ((0, 0), (0, seq_pad - seq), (0, 0))).reshape(nb * seq_pad, -1)
        z_s, xbc_s, dt_s = pad(z), pad(xbc), pad(dt)
    else:
        z_s, xbc_s, dt_s = z, xbc, dt
    cbuf = jnp.pad(conv_buf.astype(F32), ((0, 0), (8 - (SSD_CONV - 1), 0), (0, 0)))
    y_ssd, h_new = _ssd(z_s, xbc_s, dt_s, cbuf, _state_to_kernel_layout(ssm0.astype(F32)),
                        p["conv_w"], p["conv_b"], p["dt_bias"], p["a_log"], p["d_skip"], p["ssm_norm_w"], p["e2"],
                        nb=nb, seq=seq_pad, q=ssd_q, valid=None if seq_pad == seq else seq)
    if seq_pad != seq:
        y_ssd = y_ssd.reshape(nb, seq_pad, SSD_WIDTH)[:, :seq].reshape(n, SSD_WIDTH)
    ssm_new = _state_from_kernel_layout(h_new)
    ext_tail = tail.reshape(nb, -1, 8, SSD_XBC)[:, -1]
    if seq >= SSD_CONV - 1:
        conv_new = ext_tail[:, 8 - (SSD_CONV - 1):]
    else:
        conv_new = jnp.concatenate([conv_buf.astype(F32), ext_tail[:, 8 - seq:]], axis=1)[:, -(SSD_CONV - 1):]

    if no_history:
        kv_len = seq
        y_att = _diff_attn(qt, kb, vt, p["lam_vecs"], p["subln_col"], nb=nb, tq=tq, tk=tk,
                           past=past, kv_len=kv_len, lam_init=lam_init, nbuf=ATT_PIPELINE_BUFFERS)
    else:
        y_att = _decode_attn(q, k, v, kt_past, v_past, p["lam_vecs"], p["subln_col"].reshape(1, ATT_V_DIM),
                             nb=nb, seq=seq, past=past, lam_init=lam_init)

    h = _post_mix(xf, y_ssd, y_att, mem_kb, mem_vb, p["w_out"], p["wq_x"], p["wo_x"],
                  p["g_post_mix"], p["g_pre_x"], p["g_post_x"], seq=seq, tm=tm)
    out = _ffn(h, p["w_gate"], p["w_up"], p["w_down"], p["g_pre_ffn"], p["g_post_ffn"], tm=tm)
    if no_history:
        k_out = k.reshape(nb, ATT_HEADS, 2, ATT_HEAD_DIM, seq).transpose(0, 4, 1, 2, 3)
    else:
        k_out = k.reshape(nb, seq, ATT_HEADS, 2, ATT_HEAD_DIM)
    return (out.reshape(nb, seq, D_MODEL), k_out,
            v.reshape(nb, seq, ATT_HEADS, ATT_V_DIM), ssm_new, conv_new)


def _prep_params(i, w_in, conv_w, conv_b, dt_bias, a_log, d_skip, ssm_norm_w, lam_q1, lam_k1, lam_q2, lam_k2, subln_w,
                 w_out, wq_x, wo_x, g_pre_mix, g_post_mix, g_pre_x, g_post_x, g_pre_ffn, g_post_ffn,
                 w_gate, w_up, w_down):
    w = w_in[i]
    s0 = SSD_WIDTH + SSD_XBC
    s1 = s0 + SSD_HEADS
    w_r = jnp.concatenate([w[:, :s0], w[:, s1:], w[:, s0:s1], jnp.zeros((D_MODEL, DT_PAD - SSD_HEADS), w.dtype)], axis=1)
    head_pad = lambda a: jnp.pad(a[i].astype(F32), (0, DT_PAD - SSD_HEADS)).reshape(1, DT_PAD)
    e = (jnp.arange(DT_PAD)[:, None] == (jnp.arange(SSD_WIDTH)[None, :] // SSD_HEAD_DIM)).astype(BF16)
    row = lambda a: a[i].astype(F32).reshape(1, -1)
    return {
        "w_in": w_r.astype(BF16), "conv_w": conv_w[i].astype(F32), "conv_b": row(conv_b),
        "dt_bias": head_pad(dt_bias), "a_log": head_pad(a_log), "d_skip": head_pad(d_skip),
        "ssm_norm_w": row(ssm_norm_w), "e2": jnp.concatenate([e, e], axis=0),
        "lam_vecs": jnp.stack([lam_q1[i], lam_k1[i], lam_q2[i], lam_k2[i]]).astype(F32), "subln_col": subln_w[i].astype(F32).reshape(ATT_V_DIM, 1),
        "w_out": w_out[i].astype(BF16), "wq_x": wq_x[i].astype(BF16), "wo_x": wo_x[i].astype(BF16),
        "g_pre_mix": row(g_pre_mix), "g_post_mix": row(g_post_mix), "g_pre_x": row(g_pre_x), "g_post_x": row(g_post_x),
        "g_pre_ffn": row(g_pre_ffn), "g_post_ffn": row(g_post_ffn),
        "w_gate": w_gate[i].astype(BF16), "w_up": w_up[i].astype(BF16), "w_down": w_down[i].astype(BF16),
    }


def kernel(x_prompt, x_sample, cache_attn_k, cache_attn_v, state_ssm, state_conv, cache_mem_k, cache_mem_v, mem_prompt, w_in, conv_w, conv_b, dt_bias, a_log, d_skip, ssm_norm_w, lam_q1, lam_k1, lam_q2, lam_k2, subln_w, w_out, g_mem, wq_x, wk_x, wv_x, wo_x, g_pre_mix, g_post_mix, g_pre_x, g_post_x, g_pre_ffn, g_post_ffn, w_gate, w_up, w_down):
    depth = w_in.shape[0]
    bp, sp, _ = x_prompt.shape
    bs, ss, _ = x_sample.shape
    hp, hs = x_prompt, x_sample
    outs = [[] for _ in range(10)]
    for i in range(depth):
        lam_init = 0.8 - 0.6 * math.exp(-0.3 * i)
        p = _prep_params(i, w_in, conv_w, conv_b, dt_bias, a_log, d_skip, ssm_norm_w, lam_q1, lam_k1, lam_q2, lam_k2,
                         subln_w, w_out, wq_x, wo_x, g_pre_mix, g_post_mix, g_pre_x, g_post_x, g_pre_ffn, g_post_ffn,
                         w_gate, w_up, w_down)
        mk, mv, mkb, mvb = _mem_kv(mem_prompt.reshape(bp * MEM_LEN, D_MODEL), g_mem[i].reshape(1, D_MODEL),
                                   wk_x[i].astype(BF16), wv_x[i].astype(BF16))
        hp, k_new, v_new, ssm_new, conv_new = _layer(
            hp, jnp.zeros((bp, SSD_CONV - 1, SSD_XBC), F32), jnp.zeros((bp, SSD_HEADS, SSD_HEAD_DIM, SSD_STATE), F32),
            None, None, mkb.reshape(bp, MEM_LEN, D_MODEL), mvb.reshape(bp, MEM_LEN, D_MODEL), lam_init, p,
            tm=512, tk=512, tq=512, ssd_q=256)
        for lst, val in zip(outs[:6], (k_new, v_new, ssm_new, conv_new,
                                       mk.reshape(bp, MEM_LEN, MEM_HEADS, MEM_HEAD_DIM),
                                       mv.reshape(bp, MEM_LEN, MEM_HEADS, MEM_HEAD_DIM))):
            lst.append(val)
        past = cache_attn_k.shape[2]
        hs, k_new, v_new, ssm_new, conv_new = _layer(
            hs, state_conv[i], state_ssm[i],
            cache_attn_k[i].transpose(0, 2, 3, 4, 1).reshape(bs, ATT_WIDTH, past),
            cache_attn_v[i].reshape(bs, past * ATT_HEADS, ATT_V_DIM),
            cache_mem_k[i].reshape(bs, MEM_LEN, D_MODEL).astype(BF16),
            cache_mem_v[i].reshape(bs, MEM_LEN, D_MODEL).astype(BF16), lam_init, p,
            tm=bs * ss, tk=0, tq=0, ssd_q=128)
        for lst, val in zip(outs[6:], (k_new, v_new, ssm_new, conv_new)):
            lst.append(val)
    return (hp, hs) + tuple(jnp.stack(o) for o in outs)
```

```python
import functools
import math

import jax
import jax.numpy as jnp
from jax import lax
from jax.experimental import pallas as pl
from jax.experimental.pallas import tpu as pltpu

D_MODEL = 1024
CHUNK = 64
SSD_WIDTH = 512
SSD_HEAD_DIM = 64
SSD_HEADS = 8
SSD_GROUPS = 2
SSD_REP = 4
SSD_STATE = 128
SSD_CONV = 4
SSD_XBC = 1024
SSD_NORM_GROUP = 256
SSD_NORM_EPS = 1e-5
ATT_WIDTH = 512
ATT_HEAD_DIM = 64
ATT_HEADS = 4
ATT_V_DIM = 128
ATT_NORM_EPS = 1e-5
ROPE_THETA = 10000.0
MEM_LEN = 256
MEM_HEADS = 4
MEM_HEAD_DIM = 256
FFN_HIDDEN = 2816
NORM_EPS = 1e-6
LANES = 128
DT_PAD = LANES
IN_COLS_PADDED = SSD_WIDTH + SSD_XBC + 3 * ATT_WIDTH + DT_PAD
VMEM_LIMIT = 56 * 1024 * 1024
NEG_BIG = -1e30
MXU_TILE = 256
BF16_SUBLANES = 16
ATT_VT_ROWS = ATT_V_DIM + BF16_SUBLANES
ATT_SOFTMAX_ROWS = 128
ATT_PIPELINE_BUFFERS = 2
Q_SCALE = math.log2(math.e) / math.sqrt(ATT_HEAD_DIM)

F32 = jnp.float32
BF16 = jnp.bfloat16


def _const_spec(shape):
    return pl.BlockSpec(shape, lambda *_: (0,) * len(shape), pipeline_mode=pl.Buffered(1))


def _rms(x, g, eps):
    return x * lax.rsqrt(jnp.mean(x * x, axis=-1, keepdims=True) + eps) * g


def _silu(x):
    return x / (1.0 + jnp.exp(-x))


def _dot(a, b):
    return jnp.dot(a, b, preferred_element_type=F32)


def _mem_kv_kernel(mem_ref, g_ref, wk_ref, wv_ref, mk_ref, mv_ref, mkb_ref, mvb_ref):
    mn = _rms(mem_ref[...], g_ref[...], NORM_EPS).astype(BF16)
    mk = _dot(mn, wk_ref[...])
    mv = _dot(mn, wv_ref[...])
    mk_ref[...] = mk
    mv_ref[...] = mv
    mkb_ref[...] = mk.astype(BF16)
    mvb_ref[...] = mv.astype(BF16)


def _mem_kv(mem, g_mem, wk, wv):
    n = mem.shape[0]
    tm = MEM_LEN
    row = pl.BlockSpec((tm, D_MODEL), lambda i: (i, 0))
    return pl.pallas_call(
        _mem_kv_kernel,
        grid=(n // tm,),
        in_specs=[row, _const_spec((1, D_MODEL)), _const_spec((D_MODEL, D_MODEL)), _const_spec((D_MODEL, D_MODEL))],
        out_specs=[row, row, row, row],
        out_shape=[jax.ShapeDtypeStruct((n, D_MODEL), F32), jax.ShapeDtypeStruct((n, D_MODEL), F32),
                   jax.ShapeDtypeStruct((n, D_MODEL), BF16), jax.ShapeDtypeStruct((n, D_MODEL), BF16)],
        compiler_params=pltpu.CompilerParams(dimension_semantics=("arbitrary",), vmem_limit_bytes=VMEM_LIMIT),
        name="mem_kv",
    )(mem, g_mem, wk, wv)


def _in_proj_kernel(x_ref, g_ref, w_ref, cos_ref, sin_ref, cbuf_ref, cw_ref, cb_ref,
                    z_ref, u_ref, tail_ref, dt_ref, q_ref, k_ref, v_ref, qt_ref, kb_ref, vt_ref, xtail,
                    *, tm, tt, spt, tiles_per_seq):
    hn = _rms(x_ref[...], g_ref[...], NORM_EPS).astype(BF16)
    n_ssd = SSD_WIDTH + SSD_XBC
    acc_ssd = _dot(hn, w_ref[:, :n_ssd])
    acc = _dot(hn, w_ref[:, n_ssd:])
    z_ref[...] = _silu(acc_ssd[:, :SSD_WIDTH]).astype(BF16)
    xbc = acc_ssd[:, SSD_WIDTH:]
    rows = tm // spt
    row8 = lax.broadcasted_iota(jnp.int32, (8, LANES), 0)
    for s_i in range(spt):
        tail_ref[s_i] = xbc[(s_i + 1) * rows - 8:(s_i + 1) * rows]
        for cb in range(SSD_XBC // LANES):
            sl = slice(cb * LANES, (cb + 1) * LANES)
            piece = xbc[s_i * rows:(s_i + 1) * rows, sl]
            if spt > 1:
                prev = cbuf_ref[s_i, :, sl]
            else:
                prev = jnp.where(pl.program_id(0) % tiles_per_seq == 0, cbuf_ref[0, :, sl], xtail[:, sl])
            conv_top = cb_ref[:, sl] + piece[0:8] * cw_ref[SSD_CONV - 1:SSD_CONV, sl]
            conv_rest = cb_ref[:, sl] + piece[8:] * cw_ref[SSD_CONV - 1:SSD_CONV, sl]
            for d in range(1, SSD_CONV):
                sh = pltpu.roll(piece, d, 0)
                tap = cw_ref[SSD_CONV - 1 - d:SSD_CONV - d, sl]
                conv_top = conv_top + jnp.where(row8 < d, pltpu.roll(prev, d, 0), sh[0:8]) * tap
                conv_rest = conv_rest + sh[8:] * tap
            u_ref[s_i * rows:(s_i + 1) * rows, sl] = _silu(jnp.concatenate([conv_top, conv_rest], axis=0)).astype(BF16)
    if spt == 1:
        xtail[...] = xbc[tm - 8:]
    o = 0
    q = acc[:, o:o + ATT_WIDTH]
    o += ATT_WIDTH
    k = acc[:, o:o + ATT_WIDTH]
    o += ATT_WIDTH
    v = acc[:, o:o + ATT_WIDTH]
    o += ATT_WIDTH
    dt_ref[...] = acc[:, o:o + DT_PAD]
    if tt:
        for j in range(ATT_HEADS):
            v_ref[pl.ds(j, tm, stride=ATT_HEADS), :] = v[:, j * ATT_V_DIM:(j + 1) * ATT_V_DIM]
    else:
        v_ref[...] = v

    cos = cos_ref[...]
    sin = sin_ref[...]
    first_half = (lax.broadcasted_iota(jnp.int32, (tm, LANES), 1) % ATT_HEAD_DIM) < (ATT_HEAD_DIM // 2)

    def rope(t):
        swapped = jnp.where(first_half, pltpu.roll(t, LANES - ATT_HEAD_DIM // 2, 1),
                            pltpu.roll(t, ATT_HEAD_DIM // 2, 1))
        return t * cos + swapped * sin

    for j in range(ATT_WIDTH // LANES):
        sl = slice(j * LANES, (j + 1) * LANES)
        qr = rope(q[:, sl]) * Q_SCALE
        kr = rope(k[:, sl])
        q_ref[:, sl] = qr.astype(BF16)
        if tt:
            k_ref[0, sl, :] = kr.T
            kb_ref[:, sl] = kr.astype(BF16)
            for c in range(tm // tt):
                qt_ref[c, sl, :] = qr[c * tt:(c + 1) * tt, :].T.astype(BF16)
                vt_ref[c, j * ATT_VT_ROWS:j * ATT_VT_ROWS + LANES, :] = v[c * tt:(c + 1) * tt, sl].T.astype(BF16)
                vt_ref[c, j * ATT_VT_ROWS + LANES:(j + 1) * ATT_VT_ROWS, :] = jnp.ones((ATT_VT_ROWS - LANES, tt), BF16)
        else:
            k_ref[:, sl] = kr


def _in_proj(x, g, w, cos, sin, cbuf, cw, cb, *, seq, tm, tt):
    n = x.shape[0]
    nt = n // tm
    spt = max(1, tm // seq)
    tiles_per_seq = max(1, seq // tm)
    if spt > 1:
        cos, sin = jnp.tile(cos, (spt, 1)), jnp.tile(sin, (spt, 1))
    row = lambda c: pl.BlockSpec((tm, c), lambda i: (i, 0))
    tab = pl.BlockSpec((tm, LANES), lambda i: (i % tiles_per_seq, 0))
    if tt:
        k_spec = pl.BlockSpec((1, ATT_WIDTH, tm), lambda i: (i // tiles_per_seq, 0, i % tiles_per_seq))
        k_shape = jax.ShapeDtypeStruct((n // seq, ATT_WIDTH, seq), F32)
        v_spec = pl.BlockSpec((tm * ATT_HEADS, ATT_V_DIM), lambda i: (i, 0))
        v_shape = jax.ShapeDtypeStruct((n * ATT_HEADS, ATT_V_DIM), F32)
    else:
        k_spec, k_shape = row(ATT_WIDTH), jax.ShapeDtypeStruct((n, ATT_WIDTH), F32)
        v_spec, v_shape = row(ATT_WIDTH), jax.ShapeDtypeStruct((n, ATT_WIDTH), F32)
    out_specs = [row(SSD_WIDTH), row(SSD_XBC), pl.BlockSpec((spt, 8, SSD_XBC), lambda i: (i, 0, 0)), row(DT_PAD),
                 row(ATT_WIDTH), k_spec, v_spec]
    out_shape = [jax.ShapeDtypeStruct((n, SSD_WIDTH), BF16), jax.ShapeDtypeStruct((n, SSD_XBC), BF16),
                 jax.ShapeDtypeStruct((nt * spt, 8, SSD_XBC), F32), jax.ShapeDtypeStruct((n, DT_PAD), F32),
                 jax.ShapeDtypeStruct((n, ATT_WIDTH), BF16), k_shape, v_shape]
    if tt:
        tr = lambda r: pl.BlockSpec((tm // tt, r, tt), lambda i: (i, 0, 0))
        vt_rows = ATT_HEADS * ATT_VT_ROWS
        out_specs += [tr(ATT_WIDTH), row(ATT_WIDTH), tr(vt_rows)]
        out_shape += [jax.ShapeDtypeStruct((n // tt, ATT_WIDTH, tt), BF16), jax.ShapeDtypeStruct((n, ATT_WIDTH), BF16),
                      jax.ShapeDtypeStruct((n // tt, vt_rows, tt), BF16)]

    def body(*refs):
        refs, xtail = refs[:-1], refs[-1]
        refs = refs + (None,) * (18 - len(refs))
        _in_proj_kernel(*refs, xtail, tm=tm, tt=tt, spt=spt, tiles_per_seq=tiles_per_seq)

    outs = pl.pallas_call(
        body,
        grid=(nt,),
        in_specs=[row(D_MODEL), _const_spec((1, D_MODEL)), _const_spec((D_MODEL, IN_COLS_PADDED)), tab, tab,
                  pl.BlockSpec((spt, 8, SSD_XBC), lambda i: (i // tiles_per_seq, 0, 0)),
                  _const_spec((SSD_CONV, SSD_XBC)), _const_spec((1, SSD_XBC))],
        out_specs=out_specs,
        out_shape=out_shape,
        scratch_shapes=[pltpu.VMEM((8, SSD_XBC), F32)],
        compiler_params=pltpu.CompilerParams(dimension_semantics=("arbitrary",), vmem_limit_bytes=VMEM_LIMIT),
        name="in_proj",
    )(x, g, w, cos, sin, cbuf, cw, cb)
    return list(outs) + [None] * (10 - len(outs))


def _expand_heads(x, e2):
    hi = x.astype(BF16)
    lo = (x - hi.astype(F32)).astype(BF16)
    return _dot(jnp.concatenate([hi, lo], axis=1), e2)


def _ssd_kernel(gate_ref, u_ref, dt_ref, h0_ref, dtb_ref, alog_ref, dsk_ref, nw_ref, e2_ref,
                y_ref, hout_ref, state, *, q, valid):
    c = pl.program_id(1)

    @pl.when(c == 0)
    def _():
        state[...] = h0_ref[0]

    xs = u_ref[:, :SSD_WIDTH].astype(F32)
    bm = u_ref[:, SSD_WIDTH:SSD_WIDTH + SSD_GROUPS * SSD_STATE]
    cmb = u_ref[:, SSD_WIDTH + SSD_GROUPS * SSD_STATE:]

    dtr = dt_ref[...] + dtb_ref[...]
    dt = jnp.maximum(dtr, 0.0) + jnp.log(1.0 + jnp.exp(-jnp.abs(dtr)))
    if valid is not None:
        row = lax.broadcasted_iota(jnp.int32, (q, DT_PAD), 0) + c * q
        dt = jnp.where(row < valid, dt, 0.0)
    a = -jnp.exp(alog_ref[...])
    ad = dt * a
    ri = lax.broadcasted_iota(jnp.int32, (q, q), 0)
    ci = lax.broadcasted_iota(jnp.int32, (q, q), 1)
    tril = ri >= ci
    tril_b = jnp.where(tril, 1.0, 0.0).astype(BF16)
    ad_hi = ad.astype(BF16)
    ad_r = ad - ad_hi.astype(F32)
    ad_mid = ad_r.astype(BF16)
    ad_lo = (ad_r - ad_mid.astype(F32)).astype(BF16)
    acum = _dot(tril_b, ad_hi) + _dot(tril_b, ad_mid) + _dot(tril_b, ad_lo)
    acum_t = acum.T
    tot = acum[q - 1:q, :]
    e2 = e2_ref[...]
    expanded = _expand_heads(jnp.concatenate([dt, dt * jnp.exp(tot - acum), jnp.exp(acum)], axis=0), e2)
    dtx = expanded[0:q]
    ddx = expanded[q:2 * q]
    eax = expanded[2 * q:3 * q]
    dsk = _expand_heads(jnp.broadcast_to(dsk_ref[...], (8, DT_PAD)), e2)[0:1]

    xdt = (xs * dtx).astype(BF16)
    xdtd = (xs * ddx).astype(BF16)
    bm_t = bm.astype(F32).T.astype(BF16)
    gw = SSD_REP * SSD_HEAD_DIM
    stripe = lax.broadcasted_iota(jnp.int32, (q, gw), 1) // SSD_HEAD_DIM
    ys = []
    for g in range(SSD_GROUPS):
        cm_g = cmb[:, g * SSD_STATE:(g + 1) * SSD_STATE]
        bt_g = bm_t[g * SSD_STATE:(g + 1) * SSD_STATE, :]
        cbm = _dot(cm_g, bt_g)
        ms = []
        for r in range(SSD_REP):
            h = g * SSD_REP + r
            diff = acum[:, h:h + 1] - acum_t[h:h + 1, :]
            ms.append((cbm * jnp.exp(jnp.where(tril, diff, -jnp.inf))).astype(BF16))
        ydf = _dot(jnp.concatenate(ms, axis=0), xdt[:, g * gw:(g + 1) * gw])
        yd = ydf[0:q]
        for r in range(1, SSD_REP):
            yd = jnp.where(stripe == r, ydf[r * q:(r + 1) * q], yd)
        st = state[g]
        y_off = _dot(cm_g, st.astype(BF16)) * eax[:, g * gw:(g + 1) * gw]
        state[g] = st * eax[q - 1:q, g * gw:(g + 1) * gw] + _dot(bt_g, xdtd[:, g * gw:(g + 1) * gw])
        ys.append(yd + y_off)
    y = jnp.concatenate(ys, axis=1) + dsk * xs
    y = y * gate_ref[...].astype(F32)
    outs = []
    for g in range(SSD_WIDTH // SSD_NORM_GROUP):
        yg = y[:, g * SSD_NORM_GROUP:(g + 1) * SSD_NORM_GROUP]
        outs.append(yg * lax.rsqrt(jnp.mean(yg * yg, axis=-1, keepdims=True) + SSD_NORM_EPS))
    y_ref[...] = (jnp.concatenate(outs, axis=1) * nw_ref[...]).astype(BF16)

    @pl.when(c == pl.num_programs(1) - 1)
    def _():
        hout_ref[0] = state[...]


def _ssd(gate, u, dt, h0, dtb, alog, dsk, nw, e2, *, nb, seq, q, valid):
    nc = seq // q
    row = lambda c: pl.BlockSpec((q, c), lambda b, i: (b * nc + i, 0))
    gw = SSD_REP * SSD_HEAD_DIM
    st_spec = pl.BlockSpec((1, SSD_GROUPS, SSD_STATE, gw), lambda b, i: (b, 0, 0, 0))
    return pl.pallas_call(
        functools.partial(_ssd_kernel, q=q, valid=valid),
        grid=(nb, nc),
        in_specs=[row(SSD_WIDTH), row(SSD_XBC), row(DT_PAD), st_spec,
                  _const_spec((1, DT_PAD)), _const_spec((1, DT_PAD)), _const_spec((1, DT_PAD)),
                  _const_spec((1, SSD_WIDTH)), _const_spec((2 * DT_PAD, SSD_WIDTH))],
        out_specs=[row(SSD_WIDTH), st_spec],
        out_shape=[jax.ShapeDtypeStruct((nb * seq, SSD_WIDTH), BF16),
                   jax.ShapeDtypeStruct((nb, SSD_GROUPS, SSD_STATE, gw), F32)],
        scratch_shapes=[pltpu.VMEM((SSD_GROUPS, SSD_STATE, gw), F32)],
        compiler_params=pltpu.CompilerParams(dimension_semantics=("arbitrary", "arbitrary"),
                                             vmem_limit_bytes=VMEM_LIMIT),
        name="ssd",
    )(gate, u, dt, h0, dtb, alog, dsk, nw, e2)


def _column_max(x):
    while x.shape[0] > 8 and x.shape[0] % 16 == 0:
        half = x.shape[0] // 2
        x = jnp.maximum(x[:half], x[half:])
    return jnp.max(x, axis=0, keepdims=True)


def _attn_tile_counts(i, *, tq, tk, past, kv_len, minimum=min):
    q_lo = past + i * tq
    q_hi = q_lo + tq - 1
    lim_lo = minimum((q_lo // CHUNK + 1) * CHUNK, kv_len)
    lim_hi = minimum((q_hi // CHUNK + 1) * CHUNK, kv_len)
    return lim_lo // tk, (lim_hi + tk - 1) // tk


def _diff_attn_kernel(qt_ref, k_ref, vt_ref, lam_ref, sw_ref, o_ref,
                      q2t_ref, s_ref, p_ref, m_ref, alpha_ref, acc_ref, *, tq, tk, past, kv_len, lam_init, nbuf):
    i = pl.program_id(2)
    w = 2 * tq
    qt = qt_ref[0]
    row = lax.broadcasted_iota(jnp.int32, (ATT_V_DIM, tq), 0)
    zero = jnp.zeros_like(qt)
    q2t_ref[:, 0:tq] = jnp.where(row < ATT_HEAD_DIM, qt, zero)
    q2t_ref[:, tq:w] = jnp.where(row >= ATT_HEAD_DIM, qt, zero)

    q_lo = past + i * tq
    n_full, _ = _attn_tile_counts(i, tq=tq, tk=tk, past=past, kv_len=kv_len, minimum=jnp.minimum)
    n_visits = n_full + 1

    def scores(j):
        kt = k_ref[pl.ds(pl.multiple_of(j * tk, tk), tk), :]
        return _dot(kt, q2t_ref[...])

    def visited_tile(v):
        return jnp.where(v == 0, n_full, jnp.maximum(v - 1, 0))

    ncb = w // LANES
    pw = p_ref.shape[-1]

    def store_scores(buf, s):
        for cb in range(ncb):
            s_ref[buf, cb] = s[:, cb * LANES:(cb + 1) * LANES]

    def stage_a(v, buf):
        store_scores(buf, scores(jnp.minimum(v - 1, n_full)))

    rows = min(tk, ATT_SOFTMAX_ROWS)

    def stage_b(buf):
        for cb in range(ncb):
            sl = slice(cb * LANES, (cb + 1) * LANES)
            m_old = m_ref[:, sl]
            m_new = m_old
            for r0 in range(0, tk, rows):
                m_new = jnp.maximum(m_new, _column_max(s_ref[buf, cb, r0:r0 + rows, :]))
            m_ref[:, sl] = m_new
            alpha_ref[buf, :, sl] = jnp.exp2(m_old - m_new)
            pl0 = (cb * LANES) % pw
            for r0 in range(0, tk, rows):
                p_ref[buf, cb * LANES // pw, r0:r0 + rows, pl0:pl0 + LANES] = jnp.exp2(
                    (s_ref[buf, cb, r0:r0 + rows, :] - m_new).astype(BF16))

    def stage_c(v, buf):
        vt = vt_ref[visited_tile(v)]
        for pb in range(w // pw):
            sl = slice(pb * pw, (pb + 1) * pw)
            acc_ref[:, sl] = alpha_ref[buf, :, sl] * acc_ref[:, sl] + _dot(vt, p_ref[buf, pb])

    qchunk = (q_lo + lax.broadcasted_iota(jnp.int32, (1, w), 1) % tq) // CHUNK
    s_part = scores(n_full)
    s_masked = []
    for kb in range(tk // CHUNK):
        k0 = n_full * tk + kb * CHUNK
        kchunk = jnp.where(k0 < kv_len, k0 // CHUNK, jnp.iinfo(jnp.int32).max)
        s_masked.append(jnp.where(kchunk <= qchunk, s_part[kb * CHUNK:(kb + 1) * CHUNK], NEG_BIG))
    store_scores(0, jnp.concatenate(s_masked, axis=0))
    m_ref[...] = jnp.full_like(m_ref, NEG_BIG)
    acc_ref[...] = jnp.zeros_like(acc_ref)
    p_ref[nbuf - 1] = jnp.zeros(p_ref.shape[1:], BF16)
    alpha_ref[nbuf - 1] = jnp.ones(alpha_ref.shape[1:], F32)

    def visits(v0, count, prefetch_last):
        for r in range(count):
            prefetch = r + 1 < count or prefetch_last
            if prefetch and nbuf > 1:
                stage_a(v0 + r + 1, (r + 1) % nbuf)
            stage_c(v0 + r - 1, (r - 1) % nbuf)
            stage_b(r)
            if prefetch and nbuf == 1:
                stage_a(v0 + r + 1, (r + 1) % nbuf)

    def trip(u, carry):
        visits(nbuf * u, nbuf, True)
        return carry

    lax.fori_loop(0, n_visits // nbuf, trip, 0)
    for rem in range(nbuf):

        @pl.when(n_visits % nbuf == rem)
        def _():
            visits(n_visits - rem, rem, False)
            stage_c(n_visits - 1, (rem - 1) % nbuf)

    lam_v = lam_ref[...]
    lam = (jnp.exp(jnp.sum(lam_v[0:1] * lam_v[1:2], axis=-1, keepdims=True))
           - jnp.exp(jnp.sum(lam_v[2:3] * lam_v[3:4], axis=-1, keepdims=True)) + lam_init)
    acc = acc_ref[...]
    o = acc[:ATT_V_DIM] / acc[ATT_V_DIM:ATT_V_DIM + 1]
    o = o[:, :tq] - lam * o[:, tq:]
    o = o * lax.rsqrt(jnp.mean(o * o, axis=0, keepdims=True) + ATT_NORM_EPS) * (sw_ref[...] * (1.0 - lam_init))
    o_ref[...] = o.T.astype(BF16)


def _diff_attn(qt, k, vt, lam_vecs, subln_col, *, nb, tq, tk, past, kv_len, lam_init, nbuf):
    nq = qt.shape[0] // nb
    nkt = vt.shape[0] // nb
    assert tk % CHUNK == 0 and kv_len % CHUNK == 0
    for i in range(nq):
        n_full, n_end = _attn_tile_counts(i, tq=tq, tk=tk, past=past, kv_len=kv_len)
        assert n_end - n_full == 1 and n_end <= nkt, (i, n_full, n_end)
    w = 2 * tq
    return pl.pallas_call(
        functools.partial(_diff_attn_kernel, tq=tq, tk=tk, past=past, kv_len=kv_len, lam_init=lam_init, nbuf=nbuf),
        grid=(nb, ATT_HEADS, nq),
        in_specs=[pl.BlockSpec((1, ATT_V_DIM, tq), lambda b, h, i: (b * nq + i, h, 0)),
                  pl.BlockSpec((nkt * tk, ATT_V_DIM), lambda b, h, i: (b, h)),
                  pl.BlockSpec((nkt, ATT_VT_ROWS, tk), lambda b, h, i: (b, h, 0)),
                  _const_spec((4, ATT_HEAD_DIM)), _const_spec((ATT_V_DIM, 1))],
        out_specs=pl.BlockSpec((tq, ATT_V_DIM), lambda b, h, i: (b * nq + i, h)),
        out_shape=jax.ShapeDtypeStruct((nb * nq * tq, ATT_WIDTH), BF16),
        scratch_shapes=[pltpu.VMEM((ATT_V_DIM, w), BF16), pltpu.VMEM((nbuf, w // LANES, tk, LANES), F32),
                        pltpu.VMEM((nbuf, w // MXU_TILE, tk, MXU_TILE), BF16), pltpu.VMEM((1, w), F32),
                        pltpu.VMEM((nbuf, 1, w), F32),
                        pltpu.VMEM((ATT_VT_ROWS, w), F32)],
        compiler_params=pltpu.CompilerParams(dimension_semantics=("arbitrary", "arbitrary", "arbitrary"),
                                             vmem_limit_bytes=VMEM_LIMIT),
        name="diff_attn",
    )(qt, k, vt, lam_vecs, subln_col)


def _decode_attn_kernel(q_ref, kn_ref, vn_ref, kc_ref, vc_ref, lam_ref, sw_ref, o_ref, *, seq, past, lam_init):
    lam_v = lam_ref[...]
    lam = (jnp.exp(jnp.sum(lam_v[0:1] * lam_v[1:2], axis=-1, keepdims=True))
           - jnp.exp(jnp.sum(lam_v[2:3] * lam_v[3:4], axis=-1, keepdims=True)) + lam_init)
    npad = LANES
    lane = lax.broadcasted_iota(jnp.int32, (seq, LANES), 1)
    qchunk = (past + lax.broadcasted_iota(jnp.int32, (2 * seq, 1), 0) % seq) // CHUNK
    kpos_p = lax.broadcasted_iota(jnp.int32, (2 * seq, past), 1)
    kpos_n = lax.broadcasted_iota(jnp.int32, (2 * seq, npad), 1)
    vis_p = kpos_p // CHUNK <= qchunk
    vis_n = ((past + kpos_n) // CHUNK <= qchunk) & (kpos_n < seq)
    pad_rows = jnp.zeros((npad - seq, LANES), BF16)
    for h in range(ATT_HEADS):
        sl = slice(h * LANES, (h + 1) * LANES)
        qh = q_ref[:, sl]
        zero = jnp.zeros_like(qh)
        q2 = jnp.concatenate([jnp.where(lane < ATT_HEAD_DIM, qh, zero), jnp.where(lane >= ATT_HEAD_DIM, qh, zero)], axis=0)
        s_p = jnp.where(vis_p, _dot(q2, kc_ref[0, sl, :].astype(BF16)), NEG_BIG)
        kn = jnp.concatenate([kn_ref[:, sl].astype(BF16), pad_rows], axis=0)
        s_n = lax.dot_general(q2, kn, (((1,), (1,)), ((), ())), preferred_element_type=F32)
        s_n = jnp.where(vis_n, s_n, NEG_BIG)
        m = jnp.maximum(jnp.max(s_p, axis=-1, keepdims=True), jnp.max(s_n, axis=-1, keepdims=True))
        p_p = jnp.exp2(s_p - m)
        p_n = jnp.exp2(s_n - m)
        l = jnp.sum(p_p, axis=-1, keepdims=True) + jnp.sum(p_n, axis=-1, keepdims=True)
        vh = vc_ref[0, pl.ds(h, past, stride=ATT_HEADS), :].astype(BF16)
        vn = jnp.concatenate([vn_ref[:, sl].astype(BF16), pad_rows], axis=0)
        o = (_dot(p_p.astype(BF16), vh) + _dot(p_n.astype(BF16), vn)) / l
        o = o[:seq] - lam * o[seq:]
        o = o * lax.rsqrt(jnp.mean(o * o, axis=-1, keepdims=True) + ATT_NORM_EPS) * (sw_ref[...] * (1.0 - lam_init))
        o_ref[:, sl] = o.astype(BF16)


def _decode_attn(q, k_new, v_new, kt_cache, v_cache, lam_vecs, subln_row, *, nb, seq, past, lam_init):
    row = pl.BlockSpec((seq, ATT_WIDTH), lambda b: (b, 0))
    return pl.pallas_call(
        functools.partial(_decode_attn_kernel, seq=seq, past=past, lam_init=lam_init),
        grid=(nb,),
        in_specs=[row, row, row,
                  pl.BlockSpec((1, ATT_WIDTH, past), lambda b: (b, 0, 0)),
                  pl.BlockSpec((1, past * ATT_HEADS, ATT_V_DIM), lambda b: (b, 0, 0)),
                  _const_spec((4, ATT_HEAD_DIM)), _const_spec((1, ATT_V_DIM))],
        out_specs=row,
        out_shape=jax.ShapeDtypeStruct((nb * seq, ATT_WIDTH), BF16),
        compiler_params=pltpu.CompilerParams(dimension_semantics=("arbitrary",), vmem_limit_bytes=VMEM_LIMIT),
        name="decode_attn",
    )(q, k_new, v_new, kt_cache, v_cache, lam_vecs, subln_row)


def _post_mix_kernel(x_ref, ys_ref, ya_ref, mk_ref, mv_ref, wo_ref, wq_ref, wox_ref,
                     g1_ref, g2_ref, g3_ref, h_ref, *, spt, rows):
    mix = _dot(ys_ref[...], wo_ref[0:SSD_WIDTH, :]) + _dot(ya_ref[...], wo_ref[SSD_WIDTH:, :])
    h = x_ref[...] + _rms(mix, g1_ref[...], NORM_EPS)
    qn = _rms(h, g2_ref[...], NORM_EPS).astype(BF16)
    qx = (_dot(qn, wq_ref[...]) * (1.0 / math.sqrt(MEM_HEAD_DIM))).astype(BF16)
    ox_seqs = []
    for s_i in range(spt):
        qs = qx[s_i * rows:(s_i + 1) * rows]
        mk = mk_ref[s_i]
        mv = mv_ref[s_i]
        oxs = []
        for hd in range(MEM_HEADS):
            sl = slice(hd * MEM_HEAD_DIM, (hd + 1) * MEM_HEAD_DIM)
            s = lax.dot_general(qs[:, sl], mk[:, sl], (((1,), (1,)), ((), ())), preferred_element_type=F32)
            p = jnp.exp(s - jnp.max(s, axis=-1, keepdims=True))
            ox = _dot(p.astype(BF16), mv[:, sl]) / jnp.sum(p, axis=-1, keepdims=True)
            oxs.append(ox.astype(BF16))
        ox_seqs.append(jnp.concatenate(oxs, axis=1))
    ox_all = ox_seqs[0] if spt == 1 else jnp.concatenate(ox_seqs, axis=0)
    o2 = _dot(ox_all, wox_ref[...])
    h_ref[...] = h + _rms(o2, g3_ref[...], NORM_EPS)


def _post_mix(x, ys, ya, mk, mv, w_out, wq, wox, g1, g2, g3, *, seq, tm):
    n = x.shape[0]
    spt = max(1, tm // seq)
    tiles_per_seq = max(1, seq // tm)
    row = lambda c: pl.BlockSpec((tm, c), lambda i: (i, 0))
    mem = pl.BlockSpec((spt, MEM_LEN, D_MODEL), lambda i: (i // tiles_per_seq, 0, 0))
    wspec = _const_spec((D_MODEL, D_MODEL))
    gspec = _const_spec((1, D_MODEL))
    return pl.pallas_call(
        functools.partial(_post_mix_kernel, spt=spt, rows=tm // spt),
        grid=(n // tm,),
        in_specs=[row(D_MODEL), row(SSD_WIDTH), row(ATT_WIDTH), mem, mem, wspec, wspec, wspec, gspec, gspec, gspec],
        out_specs=row(D_MODEL),
        out_shape=jax.ShapeDtypeStruct((n, D_MODEL), F32),
        compiler_params=pltpu.CompilerParams(dimension_semantics=("arbitrary",), vmem_limit_bytes=VMEM_LIMIT),
        name="post_mix",
    )(x, ys, ya, mk, mv, w_out, wq, wox, g1, g2, g3)


def _ffn_kernel(h_ref, wg_ref, wu_ref, wd_ref, g1_ref, g2_ref, o_ref):
    h = h_ref[...]
    hn = _rms(h, g1_ref[...], NORM_EPS).astype(BF16)
    act = (_silu(_dot(hn, wg_ref[...])) * _dot(hn, wu_ref[...])).astype(BF16)
    f = _dot(act, wd_ref[...])
    o_ref[...] = h + _rms(f, g2_ref[...], NORM_EPS)


def _ffn(h, wg, wu, wd, g1, g2, *, tm):
    n = h.shape[0]
    row = pl.BlockSpec((tm, D_MODEL), lambda i: (i, 0))
    return pl.pallas_call(
        _ffn_kernel,
        grid=(n // tm,),
        in_specs=[row, _const_spec((D_MODEL, FFN_HIDDEN)), _const_spec((D_MODEL, FFN_HIDDEN)),
                  _const_spec((FFN_HIDDEN, D_MODEL)), _const_spec((1, D_MODEL)), _const_spec((1, D_MODEL))],
        out_specs=row,
        out_shape=jax.ShapeDtypeStruct((n, D_MODEL), F32),
        compiler_params=pltpu.CompilerParams(dimension_semantics=("arbitrary",), vmem_limit_bytes=VMEM_LIMIT),
        name="ffn",
    )(h, wg, wu, wd, g1, g2)


def _rope_tables(past, seq):
    half = ATT_HEAD_DIM // 2
    inv = jnp.power(ROPE_THETA, -jnp.arange(0, ATT_HEAD_DIM, 2, dtype=F32) / ATT_HEAD_DIM)
    pos = (past + jnp.arange(seq, dtype=jnp.int32)).astype(F32)
    ang = pos[:, None] * inv[None, :]
    cos, sin = jnp.cos(ang), jnp.sin(ang)
    reps = LANES // ATT_HEAD_DIM
    assert half * 2 == ATT_HEAD_DIM
    return jnp.tile(jnp.concatenate([cos, cos], axis=-1), (1, reps)), jnp.tile(jnp.concatenate([-sin, sin], axis=-1), (1, reps))


def _state_to_kernel_layout(s):
    b = s.shape[0]
    s = s.reshape(b, SSD_GROUPS, SSD_REP, SSD_HEAD_DIM, SSD_STATE)
    return s.transpose(0, 1, 4, 2, 3).reshape(b, SSD_GROUPS, SSD_STATE, SSD_REP * SSD_HEAD_DIM)


def _state_from_kernel_layout(s):
    b = s.shape[0]
    s = s.reshape(b, SSD_GROUPS, SSD_STATE, SSD_REP, SSD_HEAD_DIM)
    return s.transpose(0, 1, 3, 4, 2).reshape(b, SSD_HEADS, SSD_HEAD_DIM, SSD_STATE)


def _layer(x, conv_buf, ssm0, kt_past, v_past, mem_kb, mem_vb, lam_init, p, *, tm, tk, tq, ssd_q):
    nb, seq, _ = x.shape
    n = nb * seq
    past = 0 if kt_past is None else kt_past.shape[2]
    xf = x.reshape(n, D_MODEL)
    cos, sin = _rope_tables(past, seq)
    no_history = kt_past is None
    assert tq == tk or not no_history
    cbuf = jnp.pad(conv_buf.astype(F32), ((0, 0), (8 - (SSD_CONV - 1), 0), (0, 0)))
    gate, u, tail, dt, q, k, v, qt, kb, vt = _in_proj(xf, p["g_pre_mix"], p["w_in"], cos, sin, cbuf, p["conv_w"],
                                                       p["conv_b"], seq=seq, tm=tm, tt=tk if no_history else 0)

    seq_pad = -(-seq // ssd_q) * ssd_q
    if seq_pad != seq:
        pad = lambda a: jnp.pad(a.reshape(nb, seq, -1), ((0, 0), (0, seq_pad - seq), (0, 0))).reshape(nb * seq_pad, -1)
        gate_s, u_s, dt_s = pad(gate), pad(u), pad(dt)
    else:
        gate_s, u_s, dt_s = gate, u, dt
    y_ssd, h_new = _ssd(gate_s, u_s, dt_s, _state_to_kernel_layout(ssm0.astype(F32)),
                        p["dt_bias"], p["a_log"], p["d_skip"], p["ssm_norm_w"], p["e2"],
                        nb=nb, seq=seq_pad, q=ssd_q, valid=None if seq_pad == seq else seq)
    if seq_pad != seq:
        y_ssd = y_ssd.reshape(nb, seq_pad, SSD_WIDTH)[:, :seq].reshape(n, SSD_WIDTH)
    ssm_new = _state_from_kernel_layout(h_new)
    ext_tail = tail.reshape(nb, -1, 8, SSD_XBC)[:, -1]
    if seq >= SSD_CONV - 1:
        conv_new = ext_tail[:, 8 - (SSD_CONV - 1):]
    else:
        conv_new = jnp.concatenate([conv_buf.astype(F32), ext_tail[:, 8 - seq:]], axis=1)[:, -(SSD_CONV - 1):]

    if no_history:
        kv_len = seq
        y_att = _diff_attn(qt, kb, vt, p["lam_vecs"], p["subln_col"], nb=nb, tq=tq, tk=tk,
                           past=past, kv_len=kv_len, lam_init=lam_init, nbuf=ATT_PIPELINE_BUFFERS)
    else:
        y_att = _decode_attn(q, k, v, kt_past, v_past, p["lam_vecs"], p["subln_col"].reshape(1, ATT_V_DIM),
                             nb=nb, seq=seq, past=past, lam_init=lam_init)

    h = _post_mix(xf, y_ssd, y_att, mem_kb, mem_vb, p["w_out"], p["wq_x"], p["wo_x"],
                  p["g_post_mix"], p["g_pre_x"], p["g_post_x"], seq=seq, tm=tm)
    out = _ffn(h, p["w_gate"], p["w_up"], p["w_down"], p["g_pre_ffn"], p["g_post_ffn"], tm=tm)
    if no_history:
        k_out = k.reshape(nb, ATT_HEADS, 2, ATT_HEAD_DIM, seq).transpose(0, 4, 1, 2, 3)
    else:
        k_out = k.reshape(nb, seq, ATT_HEADS, 2, ATT_HEAD_DIM)
    return (out.reshape(nb, seq, D_MODEL), k_out,
            v.reshape(nb, seq, ATT_HEADS, ATT_V_DIM), ssm_new, conv_new)


def _prep_params(i, w_in, conv_w, conv_b, dt_bias, a_log, d_skip, ssm_norm_w, lam_q1, lam_k1, lam_q2, lam_k2, subln_w,
                 w_out, wq_x, wo_x, g_pre_mix, g_post_mix, g_pre_x, g_post_x, g_pre_ffn, g_post_ffn,
                 w_gate, w_up, w_down):
    w = w_in[i]
    s0 = SSD_WIDTH + SSD_XBC
    s1 = s0 + SSD_HEADS
    w_r = jnp.concatenate([w[:, :s0], w[:, s1:], w[:, s0:s1], jnp.zeros((D_MODEL, DT_PAD - SSD_HEADS), w.dtype)], axis=1)
    head_pad = lambda a: jnp.pad(a[i].astype(F32), (0, DT_PAD - SSD_HEADS)).reshape(1, DT_PAD)
    e = (jnp.arange(DT_PAD)[:, None] == (jnp.arange(SSD_WIDTH)[None, :] // SSD_HEAD_DIM)).astype(BF16)
    row = lambda a: a[i].astype(F32).reshape(1, -1)
    return {
        "w_in": w_r.astype(BF16), "conv_w": conv_w[i].astype(F32), "conv_b": row(conv_b),
        "dt_bias": head_pad(dt_bias), "a_log": head_pad(a_log), "d_skip": head_pad(d_skip),
        "ssm_norm_w": row(ssm_norm_w), "e2": jnp.concatenate([e, e], axis=0),
        "lam_vecs": jnp.stack([lam_q1[i], lam_k1[i], lam_q2[i], lam_k2[i]]).astype(F32), "subln_col": subln_w[i].astype(F32).reshape(ATT_V_DIM, 1),
        "w_out": w_out[i].astype(BF16), "wq_x": wq_x[i].astype(BF16), "wo_x": wo_x[i].astype(BF16),
        "g_pre_mix": row(g_pre_mix), "g_post_mix": row(g_post_mix), "g_pre_x": row(g_pre_x), "g_post_x": row(g_post_x),
        "g_pre_ffn": row(g_pre_ffn), "g_post_ffn": row(g_post_ffn),
        "w_gate": w_gate[i].astype(BF16), "w_up": w_up[i].astype(BF16), "w_down": w_down[i].astype(BF16),
    }


def kernel(x_prompt, x_sample, cache_attn_k, cache_attn_v, state_ssm, state_conv, cache_mem_k, cache_mem_v, mem_prompt, w_in, conv_w, conv_b, dt_bias, a_log, d_skip, ssm_norm_w, lam_q1, lam_k1, lam_q2, lam_k2, subln_w, w_out, g_mem, wq_x, wk_x, wv_x, wo_x, g_pre_mix, g_post_mix, g_pre_x, g_post_x, g_pre_ffn, g_post_ffn, w_gate, w_up, w_down):
    depth = w_in.shape[0]
    bp, sp, _ = x_prompt.shape
    bs, ss, _ = x_sample.shape
    hp, hs = x_prompt, x_sample
    outs = [[] for _ in range(10)]
    for i in range(depth):
        lam_init = 0.8 - 0.6 * math.exp(-0.3 * i)
        p = _prep_params(i, w_in, conv_w, conv_b, dt_bias, a_log, d_skip, ssm_norm_w, lam_q1, lam_k1, lam_q2, lam_k2,
                         subln_w, w_out, wq_x, wo_x, g_pre_mix, g_post_mix, g_pre_x, g_post_x, g_pre_ffn, g_post_ffn,
                         w_gate, w_up, w_down)
        mk, mv, mkb, mvb = _mem_kv(mem_prompt.reshape(bp * MEM_LEN, D_MODEL), g_mem[i].reshape(1, D_MODEL),
                                   wk_x[i].astype(BF16), wv_x[i].astype(BF16))
        hp, k_new, v_new, ssm_new, conv_new = _layer(
            hp, jnp.zeros((bp, SSD_CONV - 1, SSD_XBC), F32), jnp.zeros((bp, SSD_HEADS, SSD_HEAD_DIM, SSD_STATE), F32),
            None, None, mkb.reshape(bp, MEM_LEN, D_MODEL), mvb.reshape(bp, MEM_LEN, D_MODEL), lam_init, p,
            tm=512, tk=512, tq=512, ssd_q=256)
        for lst, val in zip(outs[:6], (k_new, v_new, ssm_new, conv_new,
                                       mk.reshape(bp, MEM_LEN, MEM_HEADS, MEM_HEAD_DIM),
                                       mv.reshape(bp, MEM_LEN, MEM_HEADS, MEM_HEAD_DIM))):
            lst.append(val)
        past = cache_attn_k.shape[2]
        hs, k_new, v_new, ssm_new, conv_new = _layer(
            hs, state_conv[i], state_ssm[i],
            cache_attn_k[i].transpose(0, 2, 3, 4, 1).reshape(bs, ATT_WIDTH, past),
            cache_attn_v[i].reshape(bs, past * ATT_HEADS, ATT_V_DIM),
            cache_mem_k[i].reshape(bs, MEM_LEN, D_MODEL).astype(BF16),
            cache_mem_v[i].reshape(bs, MEM_LEN, D_MODEL).astype(BF16), lam_init, p,
            tm=bs * ss, tk=0, tq=0, ssd_q=128)
        for lst, val in zip(outs[6:], (k_new, v_new, ssm_new, conv_new)):
            lst.append(val)
    return (hp, hs) + tuple(jnp.stack(o) for o in outs)
```

```python
import functools
import math

import jax
import jax.numpy as jnp
from jax import lax
from jax.experimental import pallas as pl
from jax.experimental.pallas import tpu as pltpu

D_MODEL = 1024
CHUNK = 64
SSD_WIDTH = 512
SSD_HEAD_DIM = 64
SSD_HEADS = 8
SSD_GROUPS = 2
SSD_REP = 4
SSD_STATE = 128
SSD_CONV = 4
SSD_XBC = 1024
SSD_NORM_GROUP = 256
SSD_NORM_EPS = 1e-5
ATT_WIDTH = 512
ATT_HEAD_DIM = 64
ATT_HEADS = 4
ATT_V_DIM = 128
ATT_NORM_EPS = 1e-5
ROPE_THETA = 10000.0
MEM_LEN = 256
MEM_HEADS = 4
MEM_HEAD_DIM = 256
FFN_HIDDEN = 2816
NORM_EPS = 1e-6
LANES = 128
DT_PAD = LANES
IN_COLS_PADDED = SSD_WIDTH + SSD_XBC + 3 * ATT_WIDTH + DT_PAD
VMEM_LIMIT = 56 * 1024 * 1024
NEG_BIG = -1e30
MXU_TILE = 256
BF16_SUBLANES = 16
ATT_VT_ROWS = ATT_V_DIM + BF16_SUBLANES
IN_PROJ_SUB_ROWS = 256
ATT_SOFTMAX_ROWS = 128
ATT_PIPELINE_BUFFERS = 2
Q_SCALE = math.log2(math.e) / math.sqrt(ATT_HEAD_DIM)

F32 = jnp.float32
BF16 = jnp.bfloat16


def _const_spec(shape):
    return pl.BlockSpec(shape, lambda *_: (0,) * len(shape), pipeline_mode=pl.Buffered(1))


def _rms(x, g, eps):
    return x * lax.rsqrt(jnp.mean(x * x, axis=-1, keepdims=True) + eps) * g


def _silu(x):
    h = 0.5 * x
    return h + h * jnp.tanh(h)


def _dot(a, b):
    return jnp.dot(a, b, preferred_element_type=F32)


def _mem_kv_kernel(mem_ref, g_ref, wk_ref, wv_ref, mk_ref, mv_ref, mkb_ref, mvb_ref):
    mn = _rms(mem_ref[...], g_ref[...], NORM_EPS).astype(BF16)
    mk = _dot(mn, wk_ref[...])
    mv = _dot(mn, wv_ref[...])
    mk_ref[...] = mk
    mv_ref[...] = mv
    mkb_ref[...] = mk.astype(BF16)
    mvb_ref[...] = mv.astype(BF16)


def _mem_kv(mem, g_mem, wk, wv):
    n = mem.shape[0]
    tm = MEM_LEN
    row = pl.BlockSpec((tm, D_MODEL), lambda i: (i, 0))
    return pl.pallas_call(
        _mem_kv_kernel,
        grid=(n // tm,),
        in_specs=[row, _const_spec((1, D_MODEL)), _const_spec((D_MODEL, D_MODEL)), _const_spec((D_MODEL, D_MODEL))],
        out_specs=[row, row, row, row],
        out_shape=[jax.ShapeDtypeStruct((n, D_MODEL), F32), jax.ShapeDtypeStruct((n, D_MODEL), F32),
                   jax.ShapeDtypeStruct((n, D_MODEL), BF16), jax.ShapeDtypeStruct((n, D_MODEL), BF16)],
        compiler_params=pltpu.CompilerParams(dimension_semantics=("arbitrary",), vmem_limit_bytes=VMEM_LIMIT),
        name="mem_kv",
    )(mem, g_mem, wk, wv)


def _in_proj_kernel(x_ref, g_ref, w_ref, cos_ref, sin_ref, cbuf_ref, cw_ref, cb_ref,
                    z_ref, u_ref, tail_ref, dt_ref, q_ref, k_ref, v_ref, qt_ref, kb_ref, vt_ref, xtail,
                    *, tm, tt, spt, tiles_per_seq):
    rows = tm // spt
    sub = min(tm, IN_PROJ_SUB_ROWS) if spt == 1 else tm
    row8 = lax.broadcasted_iota(jnp.int32, (8, LANES), 0)
    first_half = (lax.broadcasted_iota(jnp.int32, (sub, LANES), 1) % ATT_HEAD_DIM) < (ATT_HEAD_DIM // 2)
    n_ssd = SSD_WIDTH + SSD_XBC
    xbc_prev = None
    for r0 in range(0, tm, sub):
        rs = slice(r0, r0 + sub)
        hn = _rms(x_ref[rs, :], g_ref[...], NORM_EPS).astype(BF16)
        acc_ssd = _dot(hn, w_ref[:, :n_ssd])
        acc = _dot(hn, w_ref[:, n_ssd:])
        z_ref[rs, :] = _silu(acc_ssd[:, :SSD_WIDTH]).astype(BF16)
        xbc = acc_ssd[:, SSD_WIDTH:]
        prows = sub // spt if spt > 1 else sub
        for s_i in range(max(spt, 1)):
            p0 = s_i * prows
            if spt > 1 or r0 + sub == tm:
                tail_ref[s_i] = xbc[p0 + prows - 8:p0 + prows]
            for cb in range(SSD_XBC // LANES):
                sl = slice(cb * LANES, (cb + 1) * LANES)
                piece = xbc[p0:p0 + prows, sl]
                if spt > 1:
                    prev = cbuf_ref[s_i, :, sl]
                elif r0 == 0:
                    prev = jnp.where(pl.program_id(0) % tiles_per_seq == 0, cbuf_ref[0, :, sl], xtail[:, sl])
                else:
                    prev = xbc_prev[sub - 8:, sl]
                conv_top = cb_ref[:, sl] + piece[0:8] * cw_ref[SSD_CONV - 1:SSD_CONV, sl]
                conv_rest = cb_ref[:, sl] + piece[8:] * cw_ref[SSD_CONV - 1:SSD_CONV, sl]
                for d in range(1, SSD_CONV):
                    sh = pltpu.roll(piece, d, 0)
                    tap = cw_ref[SSD_CONV - 1 - d:SSD_CONV - d, sl]
                    conv_top = conv_top + jnp.where(row8 < d, pltpu.roll(prev, d, 0), sh[0:8]) * tap
                    conv_rest = conv_rest + sh[8:] * tap
                u_ref[r0 + p0:r0 + p0 + prows, sl] = _silu(jnp.concatenate([conv_top, conv_rest], axis=0)).astype(BF16)
        xbc_prev = xbc
        if spt == 1 and r0 + sub == tm:
            xtail[...] = xbc[sub - 8:]
        o = 0
        q = acc[:, o:o + ATT_WIDTH]
        o += ATT_WIDTH
        k = acc[:, o:o + ATT_WIDTH]
        o += ATT_WIDTH
        v = acc[:, o:o + ATT_WIDTH]
        o += ATT_WIDTH
        dt_ref[rs, :] = acc[:, o:o + DT_PAD]
        if tt:
            for j in range(ATT_HEADS):
                v_ref[pl.ds(r0 * ATT_HEADS + j, sub, stride=ATT_HEADS), :] = v[:, j * ATT_V_DIM:(j + 1) * ATT_V_DIM]
        else:
            v_ref[rs, :] = v

        cos = cos_ref[rs, :]
        sin = sin_ref[rs, :]

        def rope(t):
            swapped = jnp.where(first_half, pltpu.roll(t, LANES - ATT_HEAD_DIM // 2, 1),
                                pltpu.roll(t, ATT_HEAD_DIM // 2, 1))
            return t * cos + swapped * sin

        for j in range(ATT_WIDTH // LANES):
            sl = slice(j * LANES, (j + 1) * LANES)
            qr = rope(q[:, sl]) * Q_SCALE
            kr = rope(k[:, sl])
            q_ref[rs, sl] = qr.astype(BF16)
            if tt:
                assert tt % sub == 0 or sub % tt == 0
                k_ref[0, sl, rs] = kr.T
                kb_ref[rs, sl] = kr.astype(BF16)
                step = min(tt, sub)
                for c0 in range(r0, r0 + sub, step):
                    c, off = c0 // tt, c0 % tt
                    loc = slice(c0 - r0, c0 - r0 + step)
                    qt_ref[c, sl, off:off + step] = qr[loc, :].T.astype(BF16)
                    vt_ref[c, j * ATT_VT_ROWS:j * ATT_VT_ROWS + LANES, off:off + step] = v[loc, sl].T.astype(BF16)
                    vt_ref[c, j * ATT_VT_ROWS + LANES:(j + 1) * ATT_VT_ROWS, off:off + step] = jnp.ones(
                        (ATT_VT_ROWS - LANES, step), BF16)
            else:
                k_ref[rs, sl] = kr


def _in_proj(x, g, w, cos, sin, cbuf, cw, cb, *, seq, tm, tt):
    n = x.shape[0]
    nt = n // tm
    spt = max(1, tm // seq)
    tiles_per_seq = max(1, seq // tm)
    if spt > 1:
        cos, sin = jnp.tile(cos, (spt, 1)), jnp.tile(sin, (spt, 1))
    row = lambda c: pl.BlockSpec((tm, c), lambda i: (i, 0))
    tab = pl.BlockSpec((tm, LANES), lambda i: (i % tiles_per_seq, 0))
    if tt:
        k_spec = pl.BlockSpec((1, ATT_WIDTH, tm), lambda i: (i // tiles_per_seq, 0, i % tiles_per_seq))
        k_shape = jax.ShapeDtypeStruct((n // seq, ATT_WIDTH, seq), F32)
        v_spec = pl.BlockSpec((tm * ATT_HEADS, ATT_V_DIM), lambda i: (i, 0))
        v_shape = jax.ShapeDtypeStruct((n * ATT_HEADS, ATT_V_DIM), F32)
    else:
        k_spec, k_shape = row(ATT_WIDTH), jax.ShapeDtypeStruct((n, ATT_WIDTH), F32)
        v_spec, v_shape = row(ATT_WIDTH), jax.ShapeDtypeStruct((n, ATT_WIDTH), F32)
    out_specs = [row(SSD_WIDTH), row(SSD_XBC), pl.BlockSpec((spt, 8, SSD_XBC), lambda i: (i, 0, 0)), row(DT_PAD),
                 row(ATT_WIDTH), k_spec, v_spec]
    out_shape = [jax.ShapeDtypeStruct((n, SSD_WIDTH), BF16), jax.ShapeDtypeStruct((n, SSD_XBC), BF16),
                 jax.ShapeDtypeStruct((nt * spt, 8, SSD_XBC), F32), jax.ShapeDtypeStruct((n, DT_PAD), F32),
                 jax.ShapeDtypeStruct((n, ATT_WIDTH), BF16), k_shape, v_shape]
    if tt:
        tr = lambda r: pl.BlockSpec((tm // tt, r, tt), lambda i: (i, 0, 0))
        vt_rows = ATT_HEADS * ATT_VT_ROWS
        out_specs += [tr(ATT_WIDTH), row(ATT_WIDTH), tr(vt_rows)]
        out_shape += [jax.ShapeDtypeStruct((n // tt, ATT_WIDTH, tt), BF16), jax.ShapeDtypeStruct((n, ATT_WIDTH), BF16),
                      jax.ShapeDtypeStruct((n // tt, vt_rows, tt), BF16)]

    def body(*refs):
        refs, xtail = refs[:-1], refs[-1]
        refs = refs + (None,) * (18 - len(refs))
        _in_proj_kernel(*refs, xtail, tm=tm, tt=tt, spt=spt, tiles_per_seq=tiles_per_seq)

    outs = pl.pallas_call(
        body,
        grid=(nt,),
        in_specs=[row(D_MODEL), _const_spec((1, D_MODEL)), _const_spec((D_MODEL, IN_COLS_PADDED)), tab, tab,
                  pl.BlockSpec((spt, 8, SSD_XBC), lambda i: (i // tiles_per_seq, 0, 0)),
                  _const_spec((SSD_CONV, SSD_XBC)), _const_spec((1, SSD_XBC))],
        out_specs=out_specs,
        out_shape=out_shape,
        scratch_shapes=[pltpu.VMEM((8, SSD_XBC), F32)],
        compiler_params=pltpu.CompilerParams(dimension_semantics=("arbitrary",), vmem_limit_bytes=VMEM_LIMIT),
        name="in_proj",
    )(x, g, w, cos, sin, cbuf, cw, cb)
    return list(outs) + [None] * (10 - len(outs))


def _expand_heads(x, e2):
    hi = x.astype(BF16)
    lo = (x - hi.astype(F32)).astype(BF16)
    return _dot(jnp.concatenate([hi, lo], axis=1), e2)


def _ssd_kernel(gate_ref, u_ref, dt_ref, h0_ref, dtb_ref, alog_ref, dsk_ref, nw_ref, e2_ref,
                y_ref, hout_ref, state, *, q, valid):
    c = pl.program_id(1)

    @pl.when(c == 0)
    def _():
        state[...] = h0_ref[0]

    xs = u_ref[:, :SSD_WIDTH].astype(F32)
    bm = u_ref[:, SSD_WIDTH:SSD_WIDTH + SSD_GROUPS * SSD_STATE]
    cmb = u_ref[:, SSD_WIDTH + SSD_GROUPS * SSD_STATE:]

    dtr = dt_ref[...] + dtb_ref[...]
    dt = jnp.maximum(dtr, 0.0) + jnp.log(1.0 + jnp.exp(-jnp.abs(dtr)))
    if valid is not None:
        row = lax.broadcasted_iota(jnp.int32, (q, DT_PAD), 0) + c * q
        dt = jnp.where(row < valid, dt, 0.0)
    a = -jnp.exp(alog_ref[...])
    ad = dt * a
    ri = lax.broadcasted_iota(jnp.int32, (q, q), 0)
    ci = lax.broadcasted_iota(jnp.int32, (q, q), 1)
    tril = ri >= ci
    tril_b = jnp.where(tril, 1.0, 0.0).astype(BF16)
    ad_hi = ad.astype(BF16)
    ad_r = ad - ad_hi.astype(F32)
    ad_mid = ad_r.astype(BF16)
    ad_lo = (ad_r - ad_mid.astype(F32)).astype(BF16)
    acum = _dot(tril_b, ad_hi) + _dot(tril_b, ad_mid) + _dot(tril_b, ad_lo)
    acum_t = acum.T
    tot = acum[q - 1:q, :]
    e2 = e2_ref[...]
    expanded = _expand_heads(jnp.concatenate([dt, dt * jnp.exp(tot - acum), jnp.exp(acum)], axis=0), e2)
    dtx = expanded[0:q]
    ddx = expanded[q:2 * q]
    eax = expanded[2 * q:3 * q]
    dsk = _expand_heads(jnp.broadcast_to(dsk_ref[...], (8, DT_PAD)), e2)[0:1]

    xdt = (xs * dtx).astype(BF16)
    xdtd = (xs * ddx).astype(BF16)
    bm_t = bm.astype(F32).T.astype(BF16)
    gw = SSD_REP * SSD_HEAD_DIM
    stripe = lax.broadcasted_iota(jnp.int32, (q, gw), 1) // SSD_HEAD_DIM
    ys = []
    for g in range(SSD_GROUPS):
        cm_g = cmb[:, g * SSD_STATE:(g + 1) * SSD_STATE]
        bt_g = bm_t[g * SSD_STATE:(g + 1) * SSD_STATE, :]
        cbm = _dot(cm_g, bt_g)
        ms = []
        for r in range(SSD_REP):
            h = g * SSD_REP + r
            diff = acum[:, h:h + 1] - acum_t[h:h + 1, :]
            ms.append((cbm * jnp.exp(jnp.where(tril, diff, -jnp.inf))).astype(BF16))
        ydf = _dot(jnp.concatenate(ms, axis=0), xdt[:, g * gw:(g + 1) * gw])
        yd = ydf[0:q]
        for r in range(1, SSD_REP):
            yd = jnp.where(stripe == r, ydf[r * q:(r + 1) * q], yd)
        st = state[g]
        y_off = _dot(cm_g, st.astype(BF16)) * eax[:, g * gw:(g + 1) * gw]
        state[g] = st * eax[q - 1:q, g * gw:(g + 1) * gw] + _dot(bt_g, xdtd[:, g * gw:(g + 1) * gw])
        ys.append(yd + y_off)
    y = jnp.concatenate(ys, axis=1) + dsk * xs
    y = y * gate_ref[...].astype(F32)
    outs = []
    for g in range(SSD_WIDTH // SSD_NORM_GROUP):
        yg = y[:, g * SSD_NORM_GROUP:(g + 1) * SSD_NORM_GROUP]
        outs.append(yg * lax.rsqrt(jnp.mean(yg * yg, axis=-1, keepdims=True) + SSD_NORM_EPS))
    y_ref[...] = (jnp.concatenate(outs, axis=1) * nw_ref[...]).astype(BF16)

    @pl.when(c == pl.num_programs(1) - 1)
    def _():
        hout_ref[0] = state[...]


def _ssd(gate, u, dt, h0, dtb, alog, dsk, nw, e2, *, nb, seq, q, valid):
    nc = seq // q
    row = lambda c: pl.BlockSpec((q, c), lambda b, i: (b * nc + i, 0))
    gw = SSD_REP * SSD_HEAD_DIM
    st_spec = pl.BlockSpec((1, SSD_GROUPS, SSD_STATE, gw), lambda b, i: (b, 0, 0, 0))
    return pl.pallas_call(
        functools.partial(_ssd_kernel, q=q, valid=valid),
        grid=(nb, nc),
        in_specs=[row(SSD_WIDTH), row(SSD_XBC), row(DT_PAD), st_spec,
                  _const_spec((1, DT_PAD)), _const_spec((1, DT_PAD)), _const_spec((1, DT_PAD)),
                  _const_spec((1, SSD_WIDTH)), _const_spec((2 * DT_PAD, SSD_WIDTH))],
        out_specs=[row(SSD_WIDTH), st_spec],
        out_shape=[jax.ShapeDtypeStruct((nb * seq, SSD_WIDTH), BF16),
                   jax.ShapeDtypeStruct((nb, SSD_GROUPS, SSD_STATE, gw), F32)],
        scratch_shapes=[pltpu.VMEM((SSD_GROUPS, SSD_STATE, gw), F32)],
        compiler_params=pltpu.CompilerParams(dimension_semantics=("arbitrary", "arbitrary"),
                                             vmem_limit_bytes=VMEM_LIMIT),
        name="ssd",
    )(gate, u, dt, h0, dtb, alog, dsk, nw, e2)


def _column_max(x):
    while x.shape[0] > 8 and x.shape[0] % 16 == 0:
        half = x.shape[0] // 2
        x = jnp.maximum(x[:half], x[half:])
    return jnp.max(x, axis=0, keepdims=True)


def _attn_tile_counts(i, *, tq, tk, past, kv_len, minimum=min):
    q_lo = past + i * tq
    q_hi = q_lo + tq - 1
    lim_lo = minimum((q_lo // CHUNK + 1) * CHUNK, kv_len)
    lim_hi = minimum((q_hi // CHUNK + 1) * CHUNK, kv_len)
    return lim_lo // tk, (lim_hi + tk - 1) // tk


def _diff_attn_kernel(qt_ref, k_ref, vt_ref, lam_ref, sw_ref, o_ref,
                      q2t_ref, s_ref, p_ref, m_ref, alpha_ref, acc_ref, *, tq, tk, past, kv_len, lam_init, nbuf):
    i = pl.program_id(2)
    w = 2 * tq
    qt = qt_ref[0]
    row = lax.broadcasted_iota(jnp.int32, (ATT_V_DIM, tq), 0)
    zero = jnp.zeros_like(qt)
    q2t_ref[:, 0:tq] = jnp.where(row < ATT_HEAD_DIM, qt, zero)
    q2t_ref[:, tq:w] = jnp.where(row >= ATT_HEAD_DIM, qt, zero)

    q_lo = past + i * tq
    n_full, _ = _attn_tile_counts(i, tq=tq, tk=tk, past=past, kv_len=kv_len, minimum=jnp.minimum)
    n_visits = n_full + 1

    def scores(j):
        kt = k_ref[pl.ds(pl.multiple_of(j * tk, tk), tk), :]
        return _dot(kt, q2t_ref[...])

    def visited_tile(v):
        return jnp.where(v == 0, n_full, jnp.maximum(v - 1, 0))

    ncb = w // LANES
    pw = p_ref.shape[-1]

    def store_scores(buf, s):
        for cb in range(ncb):
            s_ref[buf, cb] = s[:, cb * LANES:(cb + 1) * LANES]

    def stage_a(v, buf):
        store_scores(buf, scores(jnp.minimum(v - 1, n_full)))

    rows = min(tk, ATT_SOFTMAX_ROWS)

    def stage_b(buf):
        for cb in range(ncb):
            sl = slice(cb * LANES, (cb + 1) * LANES)
            m_old = m_ref[:, sl]
            m_new = m_old
            for r0 in range(0, tk, rows):
                m_new = jnp.maximum(m_new, _column_max(s_ref[buf, cb, r0:r0 + rows, :]))
            m_ref[:, sl] = m_new
            alpha_ref[buf, :, sl] = jnp.exp2(m_old - m_new)
            pl0 = (cb * LANES) % pw
            for r0 in range(0, tk, rows):
                p_ref[buf, cb * LANES // pw, r0:r0 + rows, pl0:pl0 + LANES] = jnp.exp2(
                    (s_ref[buf, cb, r0:r0 + rows, :] - m_new).astype(BF16))

    def stage_c(v, buf):
        vt = vt_ref[visited_tile(v)]
        for pb in range(w // pw):
            sl = slice(pb * pw, (pb + 1) * pw)
            acc_ref[:, sl] = alpha_ref[buf, :, sl] * acc_ref[:, sl] + _dot(vt, p_ref[buf, pb])

    qchunk = (q_lo + lax.broadcasted_iota(jnp.int32, (1, w), 1) % tq) // CHUNK
    s_part = scores(n_full)
    s_masked = []
    for kb in range(tk // CHUNK):
        k0 = n_full * tk + kb * CHUNK
        kchunk = jnp.where(k0 < kv_len, k0 // CHUNK, jnp.iinfo(jnp.int32).max)
        s_masked.append(jnp.where(kchunk <= qchunk, s_part[kb * CHUNK:(kb + 1) * CHUNK], NEG_BIG))
    store_scores(0, jnp.concatenate(s_masked, axis=0))
    m_ref[...] = jnp.full_like(m_ref, NEG_BIG)
    acc_ref[...] = jnp.zeros_like(acc_ref)
    p_ref[nbuf - 1] = jnp.zeros(p_ref.shape[1:], BF16)
    alpha_ref[nbuf - 1] = jnp.ones(alpha_ref.shape[1:], F32)

    def visits(v0, count, prefetch_last):
        for r in range(count):
            prefetch = r + 1 < count or prefetch_last
            if prefetch and nbuf > 1:
                stage_a(v0 + r + 1, (r + 1) % nbuf)
            stage_c(v0 + r - 1, (r - 1) % nbuf)
            stage_b(r)
            if prefetch and nbuf == 1:
                stage_a(v0 + r + 1, (r + 1) % nbuf)

    def trip(u, carry):
        visits(nbuf * u, nbuf, True)
        return carry

    lax.fori_loop(0, n_visits // nbuf, trip, 0)
    for rem in range(nbuf):

        @pl.when(n_visits % nbuf == rem)
        def _():
            visits(n_visits - rem, rem, False)
            stage_c(n_visits - 1, (rem - 1) % nbuf)

    lam_v = lam_ref[...]
    lam = (jnp.exp(jnp.sum(lam_v[0:1] * lam_v[1:2], axis=-1, keepdims=True))
           - jnp.exp(jnp.sum(lam_v[2:3] * lam_v[3:4], axis=-1, keepdims=True)) + lam_init)
    acc = acc_ref[...]
    o = acc[:ATT_V_DIM] / acc[ATT_V_DIM:ATT_V_DIM + 1]
    o = o[:, :tq] - lam * o[:, tq:]
    o = o * lax.rsqrt(jnp.mean(o * o, axis=0, keepdims=True) + ATT_NORM_EPS) * (sw_ref[...] * (1.0 - lam_init))
    o_ref[...] = o.T.astype(BF16)


def _diff_attn(qt, k, vt, lam_vecs, subln_col, *, nb, tq, tk, past, kv_len, lam_init, nbuf):
    nq = qt.shape[0] // nb
    nkt = vt.shape[0] // nb
    assert tk % CHUNK == 0 and kv_len % CHUNK == 0
    for i in range(nq):
        n_full, n_end = _attn_tile_counts(i, tq=tq, tk=tk, past=past, kv_len=kv_len)
        assert n_end - n_full == 1 and n_end <= nkt, (i, n_full, n_end)
    w = 2 * tq
    return pl.pallas_call(
        functools.partial(_diff_attn_kernel, tq=tq, tk=tk, past=past, kv_len=kv_len, lam_init=lam_init, nbuf=nbuf),
        grid=(nb, ATT_HEADS, nq),
        in_specs=[pl.BlockSpec((1, ATT_V_DIM, tq), lambda b, h, i: (b * nq + i, h, 0)),
                  pl.BlockSpec((nkt * tk, ATT_V_DIM), lambda b, h, i: (b, h)),
                  pl.BlockSpec((nkt, ATT_VT_ROWS, tk), lambda b, h, i: (b, h, 0)),
                  _const_spec((4, ATT_HEAD_DIM)), _const_spec((ATT_V_DIM, 1))],
        out_specs=pl.BlockSpec((tq, ATT_V_DIM), lambda b, h, i: (b * nq + i, h)),
        out_shape=jax.ShapeDtypeStruct((nb * nq * tq, ATT_WIDTH), BF16),
        scratch_shapes=[pltpu.VMEM((ATT_V_DIM, w), BF16), pltpu.VMEM((nbuf, w // LANES, tk, LANES), F32),
                        pltpu.VMEM((nbuf, w // MXU_TILE, tk, MXU_TILE), BF16), pltpu.VMEM((1, w), F32),
                        pltpu.VMEM((nbuf, 1, w), F32),
                        pltpu.VMEM((ATT_VT_ROWS, w), F32)],
        compiler_params=pltpu.CompilerParams(dimension_semantics=("arbitrary", "arbitrary", "arbitrary"),
                                             vmem_limit_bytes=VMEM_LIMIT),
        name="diff_attn",
    )(qt, k, vt, lam_vecs, subln_col)


def _decode_attn_kernel(q_ref, kn_ref, vn_ref, kc_ref, vc_ref, lam_ref, sw_ref, o_ref, *, seq, past, lam_init):
    lam_v = lam_ref[...]
    lam = (jnp.exp(jnp.sum(lam_v[0:1] * lam_v[1:2], axis=-1, keepdims=True))
           - jnp.exp(jnp.sum(lam_v[2:3] * lam_v[3:4], axis=-1, keepdims=True)) + lam_init)
    npad = LANES
    lane = lax.broadcasted_iota(jnp.int32, (seq, LANES), 1)
    qchunk = (past + lax.broadcasted_iota(jnp.int32, (2 * seq, 1), 0) % seq) // CHUNK
    kpos_p = lax.broadcasted_iota(jnp.int32, (2 * seq, past), 1)
    kpos_n = lax.broadcasted_iota(jnp.int32, (2 * seq, npad), 1)
    vis_p = kpos_p // CHUNK <= qchunk
    vis_n = ((past + kpos_n) // CHUNK <= qchunk) & (kpos_n < seq)
    pad_rows = jnp.zeros((npad - seq, LANES), BF16)
    for h in range(ATT_HEADS):
        sl = slice(h * LANES, (h + 1) * LANES)
        qh = q_ref[:, sl]
        zero = jnp.zeros_like(qh)
        q2 = jnp.concatenate([jnp.where(lane < ATT_HEAD_DIM, qh, zero), jnp.where(lane >= ATT_HEAD_DIM, qh, zero)], axis=0)
        s_p = jnp.where(vis_p, _dot(q2, kc_ref[0, sl, :].astype(BF16)), NEG_BIG)
        kn = jnp.concatenate([kn_ref[:, sl].astype(BF16), pad_rows], axis=0)
        s_n = lax.dot_general(q2, kn, (((1,), (1,)), ((), ())), preferred_element_type=F32)
        s_n = jnp.where(vis_n, s_n, NEG_BIG)
        m = jnp.maximum(jnp.max(s_p, axis=-1, keepdims=True), jnp.max(s_n, axis=-1, keepdims=True))
        p_p = jnp.exp2(s_p - m)
        p_n = jnp.exp2(s_n - m)
        l = jnp.sum(p_p, axis=-1, keepdims=True) + jnp.sum(p_n, axis=-1, keepdims=True)
        vh = vc_ref[0, pl.ds(h, past, stride=ATT_HEADS), :].astype(BF16)
        vn = jnp.concatenate([vn_ref[:, sl].astype(BF16), pad_rows], axis=0)
        o = (_dot(p_p.astype(BF16), vh) + _dot(p_n.astype(BF16), vn)) / l
        o = o[:seq] - lam * o[seq:]
        o = o * lax.rsqrt(jnp.mean(o * o, axis=-1, keepdims=True) + ATT_NORM_EPS) * (sw_ref[...] * (1.0 - lam_init))
        o_ref[:, sl] = o.astype(BF16)


def _decode_attn(q, k_new, v_new, kt_cache, v_cache, lam_vecs, subln_row, *, nb, seq, past, lam_init):
    row = pl.BlockSpec((seq, ATT_WIDTH), lambda b: (b, 0))
    return pl.pallas_call(
        functools.partial(_decode_attn_kernel, seq=seq, past=past, lam_init=lam_init),
        grid=(nb,),
        in_specs=[row, row, row,
                  pl.BlockSpec((1, ATT_WIDTH, past), lambda b: (b, 0, 0)),
                  pl.BlockSpec((1, past * ATT_HEADS, ATT_V_DIM), lambda b: (b, 0, 0)),
                  _const_spec((4, ATT_HEAD_DIM)), _const_spec((1, ATT_V_DIM))],
        out_specs=row,
        out_shape=jax.ShapeDtypeStruct((nb * seq, ATT_WIDTH), BF16),
        compiler_params=pltpu.CompilerParams(dimension_semantics=("arbitrary",), vmem_limit_bytes=VMEM_LIMIT),
        name="decode_attn",
    )(q, k_new, v_new, kt_cache, v_cache, lam_vecs, subln_row)


def _post_mix_kernel(x_ref, ys_ref, ya_ref, mk_ref, mv_ref, wo_ref, wq_ref, wox_ref,
                     g1_ref, g2_ref, g3_ref, h_ref, *, spt, rows):
    mix = _dot(ys_ref[...], wo_ref[0:SSD_WIDTH, :]) + _dot(ya_ref[...], wo_ref[SSD_WIDTH:, :])
    h = x_ref[...] + _rms(mix, g1_ref[...], NORM_EPS)
    qn = _rms(h, g2_ref[...], NORM_EPS).astype(BF16)
    qx = (_dot(qn, wq_ref[...]) * (1.0 / math.sqrt(MEM_HEAD_DIM))).astype(BF16)
    ox_seqs = []
    for s_i in range(spt):
        qs = qx[s_i * rows:(s_i + 1) * rows]
        mk = mk_ref[s_i]
        mv = mv_ref[s_i]
        oxs = []
        for hd in range(MEM_HEADS):
            sl = slice(hd * MEM_HEAD_DIM, (hd + 1) * MEM_HEAD_DIM)
            s = lax.dot_general(qs[:, sl], mk[:, sl], (((1,), (1,)), ((), ())), preferred_element_type=F32)
            p = jnp.exp(s - jnp.max(s, axis=-1, keepdims=True))
            ox = _dot(p.astype(BF16), mv[:, sl]) / jnp.sum(p, axis=-1, keepdims=True)
            oxs.append(ox.astype(BF16))
        ox_seqs.append(jnp.concatenate(oxs, axis=1))
    ox_all = ox_seqs[0] if spt == 1 else jnp.concatenate(ox_seqs, axis=0)
    o2 = _dot(ox_all, wox_ref[...])
    h_ref[...] = h + _rms(o2, g3_ref[...], NORM_EPS)


def _post_mix(x, ys, ya, mk, mv, w_out, wq, wox, g1, g2, g3, *, seq, tm):
    n = x.shape[0]
    spt = max(1, tm // seq)
    tiles_per_seq = max(1, seq // tm)
    row = lambda c: pl.BlockSpec((tm, c), lambda i: (i, 0))
    mem = pl.BlockSpec((spt, MEM_LEN, D_MODEL), lambda i: (i // tiles_per_seq, 0, 0))
    wspec = _const_spec((D_MODEL, D_MODEL))
    gspec = _const_spec((1, D_MODEL))
    return pl.pallas_call(
        functools.partial(_post_mix_kernel, spt=spt, rows=tm // spt),
        grid=(n // tm,),
        in_specs=[row(D_MODEL), row(SSD_WIDTH), row(ATT_WIDTH), mem, mem, wspec, wspec, wspec, gspec, gspec, gspec],
        out_specs=row(D_MODEL),
        out_shape=jax.ShapeDtypeStruct((n, D_MODEL), F32),
        compiler_params=pltpu.CompilerParams(dimension_semantics=("arbitrary",), vmem_limit_bytes=VMEM_LIMIT),
        name="post_mix",
    )(x, ys, ya, mk, mv, w_out, wq, wox, g1, g2, g3)


def _ffn_kernel(h_ref, wg_ref, wu_ref, wd_ref, g1_ref, g2_ref, o_ref):
    h = h_ref[...]
    hn = _rms(h, g1_ref[...], NORM_EPS).astype(BF16)
    act = (_silu(_dot(hn, wg_ref[...])) * _dot(hn, wu_ref[...])).astype(BF16)
    f = _dot(act, wd_ref[...])
    o_ref[...] = h + _rms(f, g2_ref[...], NORM_EPS)


def _ffn(h, wg, wu, wd, g1, g2, *, tm):
    n = h.shape[0]
    row = pl.BlockSpec((tm, D_MODEL), lambda i: (i, 0))
    return pl.pallas_call(
        _ffn_kernel,
        grid=(n // tm,),
        in_specs=[row, _const_spec((D_MODEL, FFN_HIDDEN)), _const_spec((D_MODEL, FFN_HIDDEN)),
                  _const_spec((FFN_HIDDEN, D_MODEL)), _const_spec((1, D_MODEL)), _const_spec((1, D_MODEL))],
        out_specs=row,
        out_shape=jax.ShapeDtypeStruct((n, D_MODEL), F32),
        compiler_params=pltpu.CompilerParams(dimension_semantics=("arbitrary",), vmem_limit_bytes=VMEM_LIMIT),
        name="ffn",
    )(h, wg, wu, wd, g1, g2)


def _rope_tables(past, seq):
    half = ATT_HEAD_DIM // 2
    inv = jnp.power(ROPE_THETA, -jnp.arange(0, ATT_HEAD_DIM, 2, dtype=F32) / ATT_HEAD_DIM)
    pos = (past + jnp.arange(seq, dtype=jnp.int32)).astype(F32)
    ang = pos[:, None] * inv[None, :]
    cos, sin = jnp.cos(ang), jnp.sin(ang)
    reps = LANES // ATT_HEAD_DIM
    assert half * 2 == ATT_HEAD_DIM
    return jnp.tile(jnp.concatenate([cos, cos], axis=-1), (1, reps)), jnp.tile(jnp.concatenate([-sin, sin], axis=-1), (1, reps))


def _state_to_kernel_layout(s):
    b = s.shape[0]
    s = s.reshape(b, SSD_GROUPS, SSD_REP, SSD_HEAD_DIM, SSD_STATE)
    return s.transpose(0, 1, 4, 2, 3).reshape(b, SSD_GROUPS, SSD_STATE, SSD_REP * SSD_HEAD_DIM)


def _state_from_kernel_layout(s):
    b = s.shape[0]
    s = s.reshape(b, SSD_GROUPS, SSD_STATE, SSD_REP, SSD_HEAD_DIM)
    return s.transpose(0, 1, 3, 4, 2).reshape(b, SSD_HEADS, SSD_HEAD_DIM, SSD_STATE)


def _layer(x, conv_buf, ssm0, kt_past, v_past, mem_kb, mem_vb, lam_init, p, *, tm, tk, tq, ssd_q):
    nb, seq, _ = x.shape
    n = nb * seq
    past = 0 if kt_past is None else kt_past.shape[2]
    xf = x.reshape(n, D_MODEL)
    cos, sin = _rope_tables(past, seq)
    no_history = kt_past is None
    assert tq == tk or not no_history
    cbuf = jnp.pad(conv_buf.astype(F32), ((0, 0), (8 - (SSD_CONV - 1), 0), (0, 0)))
    gate, u, tail, dt, q, k, v, qt, kb, vt = _in_proj(xf, p["g_pre_mix"], p["w_in"], cos, sin, cbuf, p["conv_w"],
                                                       p["conv_b"], seq=seq, tm=tm, tt=tk if no_history else 0)

    seq_pad = -(-seq // ssd_q) * ssd_q
    if seq_pad != seq:
        pad = lambda a: jnp.pad(a.reshape(nb, seq, -1), ((0, 0), (0, seq_pad - seq), (0, 0))).reshape(nb * seq_pad, -1)
        gate_s, u_s, dt_s = pad(gate), pad(u), pad(dt)
    else:
        gate_s, u_s, dt_s = gate, u, dt
    y_ssd, h_new = _ssd(gate_s, u_s, dt_s, _state_to_kernel_layout(ssm0.astype(F32)),
                        p["dt_bias"], p["a_log"], p["d_skip"], p["ssm_norm_w"], p["e2"],
                        nb=nb, seq=seq_pad, q=ssd_q, valid=None if seq_pad == seq else seq)
    if seq_pad != seq:
        y_ssd = y_ssd.reshape(nb, seq_pad, SSD_WIDTH)[:, :seq].reshape(n, SSD_WIDTH)
    ssm_new = _state_from_kernel_layout(h_new)
    ext_tail = tail.reshape(nb, -1, 8, SSD_XBC)[:, -1]
    if seq >= SSD_CONV - 1:
        conv_new = ext_tail[:, 8 - (SSD_CONV - 1):]
    else:
        conv_new = jnp.concatenate([conv_buf.astype(F32), ext_tail[:, 8 - seq:]], axis=1)[:, -(SSD_CONV - 1):]

    if no_history:
        kv_len = seq
        y_att = _diff_attn(qt, kb, vt, p["lam_vecs"], p["subln_col"], nb=nb, tq=tq, tk=tk,
                           past=past, kv_len=kv_len, lam_init=lam_init, nbuf=ATT_PIPELINE_BUFFERS)
    else:
        y_att = _decode_attn(q, k, v, kt_past, v_past, p["lam_vecs"], p["subln_col"].reshape(1, ATT_V_DIM),
                             nb=nb, seq=seq, past=past, lam_init=lam_init)

    h = _post_mix(xf, y_ssd, y_att, mem_kb, mem_vb, p["w_out"], p["wq_x"], p["wo_x"],
                  p["g_post_mix"], p["g_pre_x"], p["g_post_x"], seq=seq, tm=tm)
    out = _ffn(h, p["w_gate"], p["w_up"], p["w_down"], p["g_pre_ffn"], p["g_post_ffn"], tm=tm)
    if no_history:
        k_out = k.reshape(nb, ATT_HEADS, 2, ATT_HEAD_DIM, seq).transpose(0, 4, 1, 2, 3)
    else:
        k_out = k.reshape(nb, seq, ATT_HEADS, 2, ATT_HEAD_DIM)
    return (out.reshape(nb, seq, D_MODEL), k_out,
            v.reshape(nb, seq, ATT_HEADS, ATT_V_DIM), ssm_new, conv_new)


def _prep_params(i, w_in, conv_w, conv_b, dt_bias, a_log, d_skip, ssm_norm_w, lam_q1, lam_k1, lam_q2, lam_k2, subln_w,
                 w_out, wq_x, wo_x, g_pre_mix, g_post_mix, g_pre_x, g_post_x, g_pre_ffn, g_post_ffn,
                 w_gate, w_up, w_down):
    w = w_in[i]
    s0 = SSD_WIDTH + SSD_XBC
    s1 = s0 + SSD_HEADS
    w_r = jnp.concatenate([w[:, :s0], w[:, s1:], w[:, s0:s1], jnp.zeros((D_MODEL, DT_PAD - SSD_HEADS), w.dtype)], axis=1)
    head_pad = lambda a: jnp.pad(a[i].astype(F32), (0, DT_PAD - SSD_HEADS)).reshape(1, DT_PAD)
    e = (jnp.arange(DT_PAD)[:, None] == (jnp.arange(SSD_WIDTH)[None, :] // SSD_HEAD_DIM)).astype(BF16)
    row = lambda a: a[i].astype(F32).reshape(1, -1)
    return {
        "w_in": w_r.astype(BF16), "conv_w": conv_w[i].astype(F32), "conv_b": row(conv_b),
        "dt_bias": head_pad(dt_bias), "a_log": head_pad(a_log), "d_skip": head_pad(d_skip),
        "ssm_norm_w": row(ssm_norm_w), "e2": jnp.concatenate([e, e], axis=0),
        "lam_vecs": jnp.stack([lam_q1[i], lam_k1[i], lam_q2[i], lam_k2[i]]).astype(F32), "subln_col": subln_w[i].astype(F32).reshape(ATT_V_DIM, 1),
        "w_out": w_out[i].astype(BF16), "wq_x": wq_x[i].astype(BF16), "wo_x": wo_x[i].astype(BF16),
        "g_pre_mix": row(g_pre_mix), "g_post_mix": row(g_post_mix), "g_pre_x": row(g_pre_x), "g_post_x": row(g_post_x),
        "g_pre_ffn": row(g_pre_ffn), "g_post_ffn": row(g_post_ffn),
        "w_gate": w_gate[i].astype(BF16), "w_up": w_up[i].astype(BF16), "w_down": w_down[i].astype(BF16),
    }


def kernel(x_prompt, x_sample, cache_attn_k, cache_attn_v, state_ssm, state_conv, cache_mem_k, cache_mem_v, mem_prompt, w_in, conv_w, conv_b, dt_bias, a_log, d_skip, ssm_norm_w, lam_q1, lam_k1, lam_q2, lam_k2, subln_w, w_out, g_mem, wq_x, wk_x, wv_x, wo_x, g_pre_mix, g_post_mix, g_pre_x, g_post_x, g_pre_ffn, g_post_ffn, w_gate, w_up, w_down):
    depth = w_in.shape[0]
    bp, sp, _ = x_prompt.shape
    bs, ss, _ = x_sample.shape
    hp, hs = x_prompt, x_sample
    outs = [[] for _ in range(10)]
    for i in range(depth):
        lam_init = 0.8 - 0.6 * math.exp(-0.3 * i)
        p = _prep_params(i, w_in, conv_w, conv_b, dt_bias, a_log, d_skip, ssm_norm_w, lam_q1, lam_k1, lam_q2, lam_k2,
                         subln_w, w_out, wq_x, wo_x, g_pre_mix, g_post_mix, g_pre_x, g_post_x, g_pre_ffn, g_post_ffn,
                         w_gate, w_up, w_down)
        mk, mv, mkb, mvb = _mem_kv(mem_prompt.reshape(bp * MEM_LEN, D_MODEL), g_mem[i].reshape(1, D_MODEL),
                                   wk_x[i].astype(BF16), wv_x[i].astype(BF16))
        hp, k_new, v_new, ssm_new, conv_new = _layer(
            hp, jnp.zeros((bp, SSD_CONV - 1, SSD_XBC), F32), jnp.zeros((bp, SSD_HEADS, SSD_HEAD_DIM, SSD_STATE), F32),
            None, None, mkb.reshape(bp, MEM_LEN, D_MODEL), mvb.reshape(bp, MEM_LEN, D_MODEL), lam_init, p,
            tm=512, tk=512, tq=512, ssd_q=256)
        for lst, val in zip(outs[:6], (k_new, v_new, ssm_new, conv_new,
                                       mk.reshape(bp, MEM_LEN, MEM_HEADS, MEM_HEAD_DIM),
                                       mv.reshape(bp, MEM_LEN, MEM_HEADS, MEM_HEAD_DIM))):
            lst.append(val)
        past = cache_attn_k.shape[2]
        hs, k_new, v_new, ssm_new, conv_new = _layer(
            hs, state_conv[i], state_ssm[i],
            cache_attn_k[i].transpose(0, 2, 3, 4, 1).reshape(bs, ATT_WIDTH, past),
            cache_attn_v[i].reshape(bs, past * ATT_HEADS, ATT_V_DIM),
            cache_mem_k[i].reshape(bs, MEM_LEN, D_MODEL).astype(BF16),
            cache_mem_v[i].reshape(bs, MEM_LEN, D_MODEL).astype(BF16), lam_init, p,
            tm=bs * ss, tk=0, tq=0, ssd_q=128)
        for lst, val in zip(outs[6:], (k_new, v_new, ssm_new, conv_new)):
            lst.append(val)
    return (hp, hs) + tuple(jnp.stack(o) for o in outs)
```

```python
import functools
import math

import jax
import jax.numpy as jnp
from jax import lax
from jax.experimental import pallas as pl
from jax.experimental.pallas import tpu as pltpu

D_MODEL = 1024
CHUNK = 64
SSD_WIDTH = 512
SSD_HEAD_DIM = 64
SSD_HEADS = 8
SSD_GROUPS = 2
SSD_REP = 4
SSD_STATE = 128
SSD_CONV = 4
SSD_XBC = 1024
SSD_NORM_GROUP = 256
SSD_NORM_EPS = 1e-5
ATT_WIDTH = 512
ATT_HEAD_DIM = 64
ATT_HEADS = 4
ATT_V_DIM = 128
ATT_NORM_EPS = 1e-5
ROPE_THETA = 10000.0
MEM_LEN = 256
MEM_HEADS = 4
MEM_HEAD_DIM = 256
FFN_HIDDEN = 2816
NORM_EPS = 1e-6
LANES = 128
DT_PAD = LANES
IN_COLS_PADDED = SSD_WIDTH + SSD_XBC + 3 * ATT_WIDTH + DT_PAD
VMEM_LIMIT = 56 * 1024 * 1024
NEG_BIG = -1e30
MXU_TILE = 256
BF16_SUBLANES = 16
ATT_VT_ROWS = ATT_V_DIM + BF16_SUBLANES
IN_PROJ_SUB_ROWS = 512
ATT_SOFTMAX_ROWS = 128
ATT_PIPELINE_BUFFERS = 4
Q_SCALE = math.log2(math.e) / math.sqrt(ATT_HEAD_DIM)

F32 = jnp.float32
BF16 = jnp.bfloat16


def _const_spec(shape):
    return pl.BlockSpec(shape, lambda *_: (0,) * len(shape), pipeline_mode=pl.Buffered(1))


def _rms(x, g, eps):
    return x * lax.rsqrt(jnp.mean(x * x, axis=-1, keepdims=True) + eps) * g


def _silu(x):
    h = 0.5 * x
    return h + h * jnp.tanh(h)


def _dot(a, b):
    return jnp.dot(a, b, preferred_element_type=F32)


def _mem_kv_kernel(mem_ref, g_ref, wk_ref, wv_ref, mk_ref, mv_ref, mkb_ref, mvb_ref):
    mn = _rms(mem_ref[...], g_ref[...], NORM_EPS).astype(BF16)
    mk = _dot(mn, wk_ref[...])
    mv = _dot(mn, wv_ref[...])
    mk_ref[...] = mk
    mv_ref[...] = mv
    mkb_ref[...] = mk.astype(BF16)
    mvb_ref[...] = mv.astype(BF16)


def _mem_kv(mem, g_mem, wk, wv):
    n = mem.shape[0]
    tm = MEM_LEN
    row = pl.BlockSpec((tm, D_MODEL), lambda i: (i, 0))
    return pl.pallas_call(
        _mem_kv_kernel,
        grid=(n // tm,),
        in_specs=[row, _const_spec((1, D_MODEL)), _const_spec((D_MODEL, D_MODEL)), _const_spec((D_MODEL, D_MODEL))],
        out_specs=[row, row, row, row],
        out_shape=[jax.ShapeDtypeStruct((n, D_MODEL), F32), jax.ShapeDtypeStruct((n, D_MODEL), F32),
                   jax.ShapeDtypeStruct((n, D_MODEL), BF16), jax.ShapeDtypeStruct((n, D_MODEL), BF16)],
        compiler_params=pltpu.CompilerParams(dimension_semantics=("arbitrary",), vmem_limit_bytes=VMEM_LIMIT),
        name="mem_kv",
    )(mem, g_mem, wk, wv)


def _in_proj_kernel(x_ref, g_ref, w_ref, cos_ref, sin_ref, cbuf_ref, cw_ref, cb_ref,
                    z_ref, u_ref, tail_ref, dt_ref, q_ref, k_ref, v_ref, qt_ref, kb_ref, vt_ref, xtail,
                    *, tm, tt, spt, tiles_per_seq):
    rows = tm // spt
    sub = min(tm, IN_PROJ_SUB_ROWS) if spt == 1 else tm
    row8 = lax.broadcasted_iota(jnp.int32, (8, LANES), 0)
    first_half = (lax.broadcasted_iota(jnp.int32, (sub, LANES), 1) % ATT_HEAD_DIM) < (ATT_HEAD_DIM // 2)
    n_ssd = SSD_WIDTH + SSD_XBC
    xbc_prev = None
    for r0 in range(0, tm, sub):
        rs = slice(r0, r0 + sub)
        hn = _rms(x_ref[rs, :], g_ref[...], NORM_EPS).astype(BF16)
        acc_ssd = _dot(hn, w_ref[:, :n_ssd])
        acc = _dot(hn, w_ref[:, n_ssd:])
        z_ref[rs, :] = _silu(acc_ssd[:, :SSD_WIDTH]).astype(BF16)
        xbc = acc_ssd[:, SSD_WIDTH:]
        prows = sub // spt if spt > 1 else sub
        for s_i in range(max(spt, 1)):
            p0 = s_i * prows
            if spt > 1 or r0 + sub == tm:
                tail_ref[s_i] = xbc[p0 + prows - 8:p0 + prows]
            for cb in range(SSD_XBC // LANES):
                sl = slice(cb * LANES, (cb + 1) * LANES)
                piece = xbc[p0:p0 + prows, sl]
                if spt > 1:
                    prev = cbuf_ref[s_i, :, sl]
                elif r0 == 0:
                    prev = jnp.where(pl.program_id(0) % tiles_per_seq == 0, cbuf_ref[0, :, sl], xtail[:, sl])
                else:
                    prev = xbc_prev[sub - 8:, sl]
                conv_top = cb_ref[:, sl] + piece[0:8] * cw_ref[SSD_CONV - 1:SSD_CONV, sl]
                conv_rest = cb_ref[:, sl] + piece[8:] * cw_ref[SSD_CONV - 1:SSD_CONV, sl]
                for d in range(1, SSD_CONV):
                    sh = pltpu.roll(piece, d, 0)
                    tap = cw_ref[SSD_CONV - 1 - d:SSD_CONV - d, sl]
                    conv_top = conv_top + jnp.where(row8 < d, pltpu.roll(prev, d, 0), sh[0:8]) * tap
                    conv_rest = conv_rest + sh[8:] * tap
                u_ref[r0 + p0:r0 + p0 + prows, sl] = _silu(jnp.concatenate([conv_top, conv_rest], axis=0)).astype(BF16)
        xbc_prev = xbc
        if spt == 1 and r0 + sub == tm:
            xtail[...] = xbc[sub - 8:]
        o = 0
        q = acc[:, o:o + ATT_WIDTH]
        o += ATT_WIDTH
        k = acc[:, o:o + ATT_WIDTH]
        o += ATT_WIDTH
        v = acc[:, o:o + ATT_WIDTH]
        o += ATT_WIDTH
        dt_ref[rs, :] = acc[:, o:o + DT_PAD]
        if tt:
            for j in range(ATT_HEADS):
                v_ref[pl.ds(r0 * ATT_HEADS + j, sub, stride=ATT_HEADS), :] = v[:, j * ATT_V_DIM:(j + 1) * ATT_V_DIM]
        else:
            v_ref[rs, :] = v

        cos = cos_ref[rs, :]
        sin = sin_ref[rs, :]

        def rope(t):
            swapped = jnp.where(first_half, pltpu.roll(t, LANES - ATT_HEAD_DIM // 2, 1),
                                pltpu.roll(t, ATT_HEAD_DIM // 2, 1))
            return t * cos + swapped * sin

        for j in range(ATT_WIDTH // LANES):
            sl = slice(j * LANES, (j + 1) * LANES)
            qr = rope(q[:, sl]) * Q_SCALE
            kr = rope(k[:, sl])
            q_ref[rs, sl] = qr.astype(BF16)
            if tt:
                assert tt % sub == 0 or sub % tt == 0
                k_ref[0, sl, rs] = kr.T
                kb_ref[rs, sl] = kr.astype(BF16)
                step = min(tt, sub)
                for c0 in range(r0, r0 + sub, step):
                    c, off = c0 // tt, c0 % tt
                    loc = slice(c0 - r0, c0 - r0 + step)
                    qt_ref[c, sl, off:off + step] = qr[loc, :].T.astype(BF16)
                    vt_ref[c, j * ATT_VT_ROWS:j * ATT_VT_ROWS + LANES, off:off + step] = v[loc, sl].T.astype(BF16)
                    vt_ref[c, j * ATT_VT_ROWS + LANES:(j + 1) * ATT_VT_ROWS, off:off + step] = jnp.ones(
                        (ATT_VT_ROWS - LANES, step), BF16)
            else:
                k_ref[rs, sl] = kr


def _in_proj(x, g, w, cos, sin, cbuf, cw, cb, *, seq, tm, tt):
    n = x.shape[0]
    nt = n // tm
    spt = max(1, tm // seq)
    tiles_per_seq = max(1, seq // tm)
    if spt > 1:
        cos, sin = jnp.tile(cos, (spt, 1)), jnp.tile(sin, (spt, 1))
    row = lambda c: pl.BlockSpec((tm, c), lambda i: (i, 0))
    tab = pl.BlockSpec((tm, LANES), lambda i: (i % tiles_per_seq, 0))
    if tt:
        k_spec = pl.BlockSpec((1, ATT_WIDTH, tm), lambda i: (i // tiles_per_seq, 0, i % tiles_per_seq))
        k_shape = jax.ShapeDtypeStruct((n // seq, ATT_WIDTH, seq), F32)
        v_spec = pl.BlockSpec((tm * ATT_HEADS, ATT_V_DIM), lambda i: (i, 0))
        v_shape = jax.ShapeDtypeStruct((n * ATT_HEADS, ATT_V_DIM), F32)
    else:
        k_spec, k_shape = row(ATT_WIDTH), jax.ShapeDtypeStruct((n, ATT_WIDTH), F32)
        v_spec, v_shape = row(ATT_WIDTH), jax.ShapeDtypeStruct((n, ATT_WIDTH), F32)
    out_specs = [row(SSD_WIDTH), row(SSD_XBC), pl.BlockSpec((spt, 8, SSD_XBC), lambda i: (i, 0, 0)), row(DT_PAD),
                 row(ATT_WIDTH), k_spec, v_spec]
    out_shape = [jax.ShapeDtypeStruct((n, SSD_WIDTH), BF16), jax.ShapeDtypeStruct((n, SSD_XBC), BF16),
                 jax.ShapeDtypeStruct((nt * spt, 8, SSD_XBC), F32), jax.ShapeDtypeStruct((n, DT_PAD), F32),
                 jax.ShapeDtypeStruct((n, ATT_WIDTH), BF16), k_shape, v_shape]
    if tt:
        tr = lambda r: pl.BlockSpec((tm // tt, r, tt), lambda i: (i, 0, 0))
        vt_rows = ATT_HEADS * ATT_VT_ROWS
        out_specs += [tr(ATT_WIDTH), row(ATT_WIDTH), tr(vt_rows)]
        out_shape += [jax.ShapeDtypeStruct((n // tt, ATT_WIDTH, tt), BF16), jax.ShapeDtypeStruct((n, ATT_WIDTH), BF16),
                      jax.ShapeDtypeStruct((n // tt, vt_rows, tt), BF16)]

    def body(*refs):
        refs, xtail = refs[:-1], refs[-1]
        refs = refs + (None,) * (18 - len(refs))
        _in_proj_kernel(*refs, xtail, tm=tm, tt=tt, spt=spt, tiles_per_seq=tiles_per_seq)

    outs = pl.pallas_call(
        body,
        grid=(nt,),
        in_specs=[row(D_MODEL), _const_spec((1, D_MODEL)), _const_spec((D_MODEL, IN_COLS_PADDED)), tab, tab,
                  pl.BlockSpec((spt, 8, SSD_XBC), lambda i: (i // tiles_per_seq, 0, 0)),
                  _const_spec((SSD_CONV, SSD_XBC)), _const_spec((1, SSD_XBC))],
        out_specs=out_specs,
        out_shape=out_shape,
        scratch_shapes=[pltpu.VMEM((8, SSD_XBC), F32)],
        compiler_params=pltpu.CompilerParams(dimension_semantics=("arbitrary",), vmem_limit_bytes=VMEM_LIMIT),
        name="in_proj",
    )(x, g, w, cos, sin, cbuf, cw, cb)
    return list(outs) + [None] * (10 - len(outs))


def _expand_heads(x, e2):
    hi = x.astype(BF16)
    lo = (x - hi.astype(F32)).astype(BF16)
    return _dot(jnp.concatenate([hi, lo], axis=1), e2)


def _ssd_kernel(gate_ref, u_ref, dt_ref, h0_ref, dtb_ref, alog_ref, dsk_ref, nw_ref, e2_ref,
                y_ref, hout_ref, state, *, q, valid):
    c = pl.program_id(1)

    @pl.when(c == 0)
    def _():
        state[...] = h0_ref[0]

    xs = u_ref[:, :SSD_WIDTH].astype(F32)
    bm = u_ref[:, SSD_WIDTH:SSD_WIDTH + SSD_GROUPS * SSD_STATE]
    cmb = u_ref[:, SSD_WIDTH + SSD_GROUPS * SSD_STATE:]

    dtr = dt_ref[...] + dtb_ref[...]
    dt = jnp.maximum(dtr, 0.0) + jnp.log(1.0 + jnp.exp(-jnp.abs(dtr)))
    if valid is not None:
        row = lax.broadcasted_iota(jnp.int32, (q, DT_PAD), 0) + c * q
        dt = jnp.where(row < valid, dt, 0.0)
    a = -jnp.exp(alog_ref[...])
    ad = dt * a
    ri = lax.broadcasted_iota(jnp.int32, (q, q), 0)
    ci = lax.broadcasted_iota(jnp.int32, (q, q), 1)
    tril = ri >= ci
    tril_b = jnp.where(tril, 1.0, 0.0).astype(BF16)
    ad_hi = ad.astype(BF16)
    ad_r = ad - ad_hi.astype(F32)
    ad_mid = ad_r.astype(BF16)
    ad_lo = (ad_r - ad_mid.astype(F32)).astype(BF16)
    acum = _dot(tril_b, ad_hi) + _dot(tril_b, ad_mid) + _dot(tril_b, ad_lo)
    acum_t = acum.T
    tot = acum[q - 1:q, :]
    e2 = e2_ref[...]
    expanded = _expand_heads(jnp.concatenate([dt, dt * jnp.exp(tot - acum), jnp.exp(acum)], axis=0), e2)
    dtx = expanded[0:q]
    ddx = expanded[q:2 * q]
    eax = expanded[2 * q:3 * q]
    dsk = _expand_heads(jnp.broadcast_to(dsk_ref[...], (8, DT_PAD)), e2)[0:1]

    xdt = (xs * dtx).astype(BF16)
    xdtd = (xs * ddx).astype(BF16)
    bm_t = bm.astype(F32).T.astype(BF16)
    gw = SSD_REP * SSD_HEAD_DIM
    stripe = lax.broadcasted_iota(jnp.int32, (q, gw), 1) // SSD_HEAD_DIM
    ys = []
    for g in range(SSD_GROUPS):
        cm_g = cmb[:, g * SSD_STATE:(g + 1) * SSD_STATE]
        bt_g = bm_t[g * SSD_STATE:(g + 1) * SSD_STATE, :]
        cbm = _dot(cm_g, bt_g)
        ms = []
        for r in range(SSD_REP):
            h = g * SSD_REP + r
            diff = acum[:, h:h + 1] - acum_t[h:h + 1, :]
            ms.append((cbm * jnp.exp(jnp.where(tril, diff, -jnp.inf))).astype(BF16))
        ydf = _dot(jnp.concatenate(ms, axis=0), xdt[:, g * gw:(g + 1) * gw])
        yd = ydf[0:q]
        for r in range(1, SSD_REP):
            yd = jnp.where(stripe == r, ydf[r * q:(r + 1) * q], yd)
        st = state[g]
        y_off = _dot(cm_g, st.astype(BF16)) * eax[:, g * gw:(g + 1) * gw]
        state[g] = st * eax[q - 1:q, g * gw:(g + 1) * gw] + _dot(bt_g, xdtd[:, g * gw:(g + 1) * gw])
        ys.append(yd + y_off)
    y = jnp.concatenate(ys, axis=1) + dsk * xs
    y = y * gate_ref[...].astype(F32)
    outs = []
    for g in range(SSD_WIDTH // SSD_NORM_GROUP):
        yg = y[:, g * SSD_NORM_GROUP:(g + 1) * SSD_NORM_GROUP]
        outs.append(yg * lax.rsqrt(jnp.mean(yg * yg, axis=-1, keepdims=True) + SSD_NORM_EPS))
    y_ref[...] = (jnp.concatenate(outs, axis=1) * nw_ref[...]).astype(BF16)

    @pl.when(c == pl.num_programs(1) - 1)
    def _():
        hout_ref[0] = state[...]


def _ssd(gate, u, dt, h0, dtb, alog, dsk, nw, e2, *, nb, seq, q, valid):
    nc = seq // q
    row = lambda c: pl.BlockSpec((q, c), lambda b, i: (b * nc + i, 0))
    gw = SSD_REP * SSD_HEAD_DIM
    st_spec = pl.BlockSpec((1, SSD_GROUPS, SSD_STATE, gw), lambda b, i: (b, 0, 0, 0))
    return pl.pallas_call(
        functools.partial(_ssd_kernel, q=q, valid=valid),
        grid=(nb, nc),
        in_specs=[row(SSD_WIDTH), row(SSD_XBC), row(DT_PAD), st_spec,
                  _const_spec((1, DT_PAD)), _const_spec((1, DT_PAD)), _const_spec((1, DT_PAD)),
                  _const_spec((1, SSD_WIDTH)), _const_spec((2 * DT_PAD, SSD_WIDTH))],
        out_specs=[row(SSD_WIDTH), st_spec],
        out_shape=[jax.ShapeDtypeStruct((nb * seq, SSD_WIDTH), BF16),
                   jax.ShapeDtypeStruct((nb, SSD_GROUPS, SSD_STATE, gw), F32)],
        scratch_shapes=[pltpu.VMEM((SSD_GROUPS, SSD_STATE, gw), F32)],
        compiler_params=pltpu.CompilerParams(dimension_semantics=("arbitrary", "arbitrary"),
                                             vmem_limit_bytes=VMEM_LIMIT),
        name="ssd",
    )(gate, u, dt, h0, dtb, alog, dsk, nw, e2)


def _column_max(x):
    while x.shape[0] > 8 and x.shape[0] % 16 == 0:
        half = x.shape[0] // 2
        x = jnp.maximum(x[:half], x[half:])
    return jnp.max(x, axis=0, keepdims=True)


def _attn_tile_counts(i, *, tq, tk, past, kv_len, minimum=min):
    q_lo = past + i * tq
    q_hi = q_lo + tq - 1
    lim_lo = minimum((q_lo // CHUNK + 1) * CHUNK, kv_len)
    lim_hi = minimum((q_hi // CHUNK + 1) * CHUNK, kv_len)
    return lim_lo // tk, (lim_hi + tk - 1) // tk


def _diff_attn_kernel(qt_ref, k_ref, vt_ref, lam_ref, sw_ref, o_ref,
                      q2t_ref, s_ref, p_ref, m_ref, alpha_ref, acc_ref, *, tq, tk, past, kv_len, lam_init, nbuf):
    i = pl.program_id(2)
    w = 2 * tq
    qt = qt_ref[0]
    row = lax.broadcasted_iota(jnp.int32, (ATT_V_DIM, tq), 0)
    zero = jnp.zeros_like(qt)
    q2t_ref[:, 0:tq] = jnp.where(row < ATT_HEAD_DIM, qt, zero)
    q2t_ref[:, tq:w] = jnp.where(row >= ATT_HEAD_DIM, qt, zero)

    q_lo = past + i * tq
    n_full, _ = _attn_tile_counts(i, tq=tq, tk=tk, past=past, kv_len=kv_len, minimum=jnp.minimum)
    n_visits = n_full + 1

    def scores(j):
        kt = k_ref[pl.ds(pl.multiple_of(j * tk, tk), tk), :]
        return _dot(kt, q2t_ref[...])

    def visited_tile(v):
        return jnp.where(v == 0, n_full, jnp.maximum(v - 1, 0))

    ncb = w // LANES
    pw = p_ref.shape[-1]

    def store_scores(buf, s):
        for cb in range(ncb):
            s_ref[buf, cb] = s[:, cb * LANES:(cb + 1) * LANES]

    def stage_a(v, buf):
        store_scores(buf, scores(jnp.minimum(v - 1, n_full)))

    rows = min(tk, ATT_SOFTMAX_ROWS)

    def stage_b(buf):
        for cb in range(ncb):
            sl = slice(cb * LANES, (cb + 1) * LANES)
            m_old = m_ref[:, sl]
            m_new = m_old
            for r0 in range(0, tk, rows):
                m_new = jnp.maximum(m_new, _column_max(s_ref[buf, cb, r0:r0 + rows, :]))
            m_ref[:, sl] = m_new
            alpha_ref[buf, :, sl] = jnp.exp2(m_old - m_new)
            pl0 = (cb * LANES) % pw
            for r0 in range(0, tk, rows):
                p_ref[buf, cb * LANES // pw, r0:r0 + rows, pl0:pl0 + LANES] = jnp.exp2(
                    (s_ref[buf, cb, r0:r0 + rows, :] - m_new).astype(BF16))

    def stage_c(v, buf):
        vt = vt_ref[visited_tile(v)]
        for pb in range(w // pw):
            sl = slice(pb * pw, (pb + 1) * pw)
            acc_ref[:, sl] = alpha_ref[buf, :, sl] * acc_ref[:, sl] + _dot(vt, p_ref[buf, pb])

    qchunk = (q_lo + lax.broadcasted_iota(jnp.int32, (1, w), 1) % tq) // CHUNK
    s_part = scores(n_full)
    s_masked = []
    for kb in range(tk // CHUNK):
        k0 = n_full * tk + kb * CHUNK
        kchunk = jnp.where(k0 < kv_len, k0 // CHUNK, jnp.iinfo(jnp.int32).max)
        s_masked.append(jnp.where(kchunk <= qchunk, s_part[kb * CHUNK:(kb + 1) * CHUNK], NEG_BIG))
    store_scores(0, jnp.concatenate(s_masked, axis=0))
    m_ref[...] = jnp.full_like(m_ref, NEG_BIG)
    acc_ref[...] = jnp.zeros_like(acc_ref)
    p_ref[nbuf - 1] = jnp.zeros(p_ref.shape[1:], BF16)
    alpha_ref[nbuf - 1] = jnp.ones(alpha_ref.shape[1:], F32)

    def visits(v0, count, prefetch_last):
        for r in range(count):
            prefetch = r + 1 < count or prefetch_last
            if prefetch and nbuf > 1:
                stage_a(v0 + r + 1, (r + 1) % nbuf)
            stage_c(v0 + r - 1, (r - 1) % nbuf)
            stage_b(r)
            if prefetch and nbuf == 1:
                stage_a(v0 + r + 1, (r + 1) % nbuf)

    def trip(u, carry):
        visits(nbuf * u, nbuf, True)
        return carry

    lax.fori_loop(0, n_visits // nbuf, trip, 0)
    for rem in range(nbuf):

        @pl.when(n_visits % nbuf == rem)
        def _():
            visits(n_visits - rem, rem, False)
            stage_c(n_visits - 1, (rem - 1) % nbuf)

    lam_v = lam_ref[...]
    lam = (jnp.exp(jnp.sum(lam_v[0:1] * lam_v[1:2], axis=-1, keepdims=True))
           - jnp.exp(jnp.sum(lam_v[2:3] * lam_v[3:4], axis=-1, keepdims=True)) + lam_init)
    acc = acc_ref[...]
    o = acc[:ATT_V_DIM] / acc[ATT_V_DIM:ATT_V_DIM + 1]
    o = o[:, :tq] - lam * o[:, tq:]
    o = o * lax.rsqrt(jnp.mean(o * o, axis=0, keepdims=True) + ATT_NORM_EPS) * (sw_ref[...] * (1.0 - lam_init))
    o_ref[...] = o.T.astype(BF16)


def _diff_attn(qt, k, vt, lam_vecs, subln_col, *, nb, tq, tk, past, kv_len, lam_init, nbuf):
    nq = qt.shape[0] // nb
    nkt = vt.shape[0] // nb
    assert tk % CHUNK == 0 and kv_len % CHUNK == 0
    for i in range(nq):
        n_full, n_end = _attn_tile_counts(i, tq=tq, tk=tk, past=past, kv_len=kv_len)
        assert n_end - n_full == 1 and n_end <= nkt, (i, n_full, n_end)
    w = 2 * tq
    return pl.pallas_call(
        functools.partial(_diff_attn_kernel, tq=tq, tk=tk, past=past, kv_len=kv_len, lam_init=lam_init, nbuf=nbuf),
        grid=(nb, ATT_HEADS, nq),
        in_specs=[pl.BlockSpec((1, ATT_V_DIM, tq), lambda b, h, i: (b * nq + i, h, 0)),
                  pl.BlockSpec((nkt * tk, ATT_V_DIM), lambda b, h, i: (b, h)),
                  pl.BlockSpec((nkt, ATT_VT_ROWS, tk), lambda b, h, i: (b, h, 0)),
                  _const_spec((4, ATT_HEAD_DIM)), _const_spec((ATT_V_DIM, 1))],
        out_specs=pl.BlockSpec((tq, ATT_V_DIM), lambda b, h, i: (b * nq + i, h)),
        out_shape=jax.ShapeDtypeStruct((nb * nq * tq, ATT_WIDTH), BF16),
        scratch_shapes=[pltpu.VMEM((ATT_V_DIM, w), BF16), pltpu.VMEM((nbuf, w // LANES, tk, LANES), F32),
                        pltpu.VMEM((nbuf, w // MXU_TILE, tk, MXU_TILE), BF16), pltpu.VMEM((1, w), F32),
                        pltpu.VMEM((nbuf, 1, w), F32),
                        pltpu.VMEM((ATT_VT_ROWS, w), F32)],
        compiler_params=pltpu.CompilerParams(dimension_semantics=("arbitrary", "arbitrary", "arbitrary"),
                                             vmem_limit_bytes=VMEM_LIMIT),
        name="diff_attn",
    )(qt, k, vt, lam_vecs, subln_col)


def _decode_attn_kernel(q_ref, kn_ref, vn_ref, kc_ref, vc_ref, lam_ref, sw_ref, o_ref, *, seq, past, lam_init):
    lam_v = lam_ref[...]
    lam = (jnp.exp(jnp.sum(lam_v[0:1] * lam_v[1:2], axis=-1, keepdims=True))
           - jnp.exp(jnp.sum(lam_v[2:3] * lam_v[3:4], axis=-1, keepdims=True)) + lam_init)
    npad = LANES
    lane = lax.broadcasted_iota(jnp.int32, (seq, LANES), 1)
    qchunk = (past + lax.broadcasted_iota(jnp.int32, (2 * seq, 1), 0) % seq) // CHUNK
    kpos_p = lax.broadcasted_iota(jnp.int32, (2 * seq, past), 1)
    kpos_n = lax.broadcasted_iota(jnp.int32, (2 * seq, npad), 1)
    vis_p = kpos_p // CHUNK <= qchunk
    vis_n = ((past + kpos_n) // CHUNK <= qchunk) & (kpos_n < seq)
    pad_rows = jnp.zeros((npad - seq, LANES), BF16)
    for h in range(ATT_HEADS):
        sl = slice(h * LANES, (h + 1) * LANES)
        qh = q_ref[:, sl]
        zero = jnp.zeros_like(qh)
        q2 = jnp.concatenate([jnp.where(lane < ATT_HEAD_DIM, qh, zero), jnp.where(lane >= ATT_HEAD_DIM, qh, zero)], axis=0)
        s_p = jnp.where(vis_p, _dot(q2, kc_ref[0, sl, :].astype(BF16)), NEG_BIG)
        kn = jnp.concatenate([kn_ref[:, sl].astype(BF16), pad_rows], axis=0)
        s_n = lax.dot_general(q2, kn, (((1,), (1,)), ((), ())), preferred_element_type=F32)
        s_n = jnp.where(vis_n, s_n, NEG_BIG)
        m = jnp.maximum(jnp.max(s_p, axis=-1, keepdims=True), jnp.max(s_n, axis=-1, keepdims=True))
        p_p = jnp.exp2(s_p - m)
        p_n = jnp.exp2(s_n - m)
        l = jnp.sum(p_p, axis=-1, keepdims=True) + jnp.sum(p_n, axis=-1, keepdims=True)
        vh = vc_ref[0, pl.ds(h, past, stride=ATT_HEADS), :].astype(BF16)
        vn = jnp.concatenate([vn_ref[:, sl].astype(BF16), pad_rows], axis=0)
        o = (_dot(p_p.astype(BF16), vh) + _dot(p_n.astype(BF16), vn)) / l
        o = o[:seq] - lam * o[seq:]
        o = o * lax.rsqrt(jnp.mean(o * o, axis=-1, keepdims=True) + ATT_NORM_EPS) * (sw_ref[...] * (1.0 - lam_init))
        o_ref[:, sl] = o.astype(BF16)


def _decode_attn(q, k_new, v_new, kt_cache, v_cache, lam_vecs, subln_row, *, nb, seq, past, lam_init):
    row = pl.BlockSpec((seq, ATT_WIDTH), lambda b: (b, 0))
    return pl.pallas_call(
        functools.partial(_decode_attn_kernel, seq=seq, past=past, lam_init=lam_init),
        grid=(nb,),
        in_specs=[row, row, row,
                  pl.BlockSpec((1, ATT_WIDTH, past), lambda b: (b, 0, 0)),
                  pl.BlockSpec((1, past * ATT_HEADS, ATT_V_DIM), lambda b: (b, 0, 0)),
                  _const_spec((4, ATT_HEAD_DIM)), _const_spec((1, ATT_V_DIM))],
        out_specs=row,
        out_shape=jax.ShapeDtypeStruct((nb * seq, ATT_WIDTH), BF16),
        compiler_params=pltpu.CompilerParams(dimension_semantics=("arbitrary",), vmem_limit_bytes=VMEM_LIMIT),
        name="decode_attn",
    )(q, k_new, v_new, kt_cache, v_cache, lam_vecs, subln_row)


def _post_mix_kernel(x_ref, ys_ref, ya_ref, mk_ref, mv_ref, wo_ref, wq_ref, wox_ref,
                     g1_ref, g2_ref, g3_ref, h_ref, *, spt, rows):
    mix = _dot(ys_ref[...], wo_ref[0:SSD_WIDTH, :]) + _dot(ya_ref[...], wo_ref[SSD_WIDTH:, :])
    h = x_ref[...] + _rms(mix, g1_ref[...], NORM_EPS)
    qn = _rms(h, g2_ref[...], NORM_EPS).astype(BF16)
    qx = (_dot(qn, wq_ref[...]) * (1.0 / math.sqrt(MEM_HEAD_DIM))).astype(BF16)
    ox_seqs = []
    for s_i in range(spt):
        qs = qx[s_i * rows:(s_i + 1) * rows]
        mk = mk_ref[s_i]
        mv = mv_ref[s_i]
        oxs = []
        for hd in range(MEM_HEADS):
            sl = slice(hd * MEM_HEAD_DIM, (hd + 1) * MEM_HEAD_DIM)
            s = lax.dot_general(qs[:, sl], mk[:, sl], (((1,), (1,)), ((), ())), preferred_element_type=F32)
            p = jnp.exp(s - jnp.max(s, axis=-1, keepdims=True))
            ox = _dot(p.astype(BF16), mv[:, sl]) / jnp.sum(p, axis=-1, keepdims=True)
            oxs.append(ox.astype(BF16))
        ox_seqs.append(jnp.concatenate(oxs, axis=1))
    ox_all = ox_seqs[0] if spt == 1 else jnp.concatenate(ox_seqs, axis=0)
    o2 = _dot(ox_all, wox_ref[...])
    h_ref[...] = h + _rms(o2, g3_ref[...], NORM_EPS)


def _post_mix(x, ys, ya, mk, mv, w_out, wq, wox, g1, g2, g3, *, seq, tm):
    n = x.shape[0]
    spt = max(1, tm // seq)
    tiles_per_seq = max(1, seq // tm)
    row = lambda c: pl.BlockSpec((tm, c), lambda i: (i, 0))
    mem = pl.BlockSpec((spt, MEM_LEN, D_MODEL), lambda i: (i // tiles_per_seq, 0, 0))
    wspec = _const_spec((D_MODEL, D_MODEL))
    gspec = _const_spec((1, D_MODEL))
    return pl.pallas_call(
        functools.partial(_post_mix_kernel, spt=spt, rows=tm // spt),
        grid=(n // tm,),
        in_specs=[row(D_MODEL), row(SSD_WIDTH), row(ATT_WIDTH), mem, mem, wspec, wspec, wspec, gspec, gspec, gspec],
        out_specs=row(D_MODEL),
        out_shape=jax.ShapeDtypeStruct((n, D_MODEL), F32),
        compiler_params=pltpu.CompilerParams(dimension_semantics=("arbitrary",), vmem_limit_bytes=VMEM_LIMIT),
        name="post_mix",
    )(x, ys, ya, mk, mv, w_out, wq, wox, g1, g2, g3)


def _ffn_kernel(h_ref, wg_ref, wu_ref, wd_ref, g1_ref, g2_ref, o_ref):
    h = h_ref[...]
    hn = _rms(h, g1_ref[...], NORM_EPS).astype(BF16)
    act = (_silu(_dot(hn, wg_ref[...])) * _dot(hn, wu_ref[...])).astype(BF16)
    f = _dot(act, wd_ref[...])
    o_ref[...] = h + _rms(f, g2_ref[...], NORM_EPS)


def _ffn(h, wg, wu, wd, g1, g2, *, tm):
    n = h.shape[0]
    row = pl.BlockSpec((tm, D_MODEL), lambda i: (i, 0))
    return pl.pallas_call(
        _ffn_kernel,
        grid=(n // tm,),
        in_specs=[row, _const_spec((D_MODEL, FFN_HIDDEN)), _const_spec((D_MODEL, FFN_HIDDEN)),
                  _const_spec((FFN_HIDDEN, D_MODEL)), _const_spec((1, D_MODEL)), _const_spec((1, D_MODEL))],
        out_specs=row,
        out_shape=jax.ShapeDtypeStruct((n, D_MODEL), F32),
        compiler_params=pltpu.CompilerParams(dimension_semantics=("arbitrary",), vmem_limit_bytes=VMEM_LIMIT),
        name="ffn",
    )(h, wg, wu, wd, g1, g2)


def _rope_tables(past, seq):
    half = ATT_HEAD_DIM // 2
    inv = jnp.power(ROPE_THETA, -jnp.arange(0, ATT_HEAD_DIM, 2, dtype=F32) / ATT_HEAD_DIM)
    pos = (past + jnp.arange(seq, dtype=jnp.int32)).astype(F32)
    ang = pos[:, None] * inv[None, :]
    cos, sin = jnp.cos(ang), jnp.sin(ang)
    reps = LANES // ATT_HEAD_DIM
    assert half * 2 == ATT_HEAD_DIM
    return jnp.tile(jnp.concatenate([cos, cos], axis=-1), (1, reps)), jnp.tile(jnp.concatenate([-sin, sin], axis=-1), (1, reps))


def _state_to_kernel_layout(s):
    b = s.shape[0]
    s = s.reshape(b, SSD_GROUPS, SSD_REP, SSD_HEAD_DIM, SSD_STATE)
    return s.transpose(0, 1, 4, 2, 3).reshape(b, SSD_GROUPS, SSD_STATE, SSD_REP * SSD_HEAD_DIM)


def _state_from_kernel_layout(s):
    b = s.shape[0]
    s = s.reshape(b, SSD_GROUPS, SSD_STATE, SSD_REP, SSD_HEAD_DIM)
    return s.transpose(0, 1, 3, 4, 2).reshape(b, SSD_HEADS, SSD_HEAD_DIM, SSD_STATE)


def _layer(x, conv_buf, ssm0, kt_past, v_past, mem_kb, mem_vb, lam_init, p, *, tm, tk, tq, ssd_q):
    nb, seq, _ = x.shape
    n = nb * seq
    past = 0 if kt_past is None else kt_past.shape[2]
    xf = x.reshape(n, D_MODEL)
    cos, sin = _rope_tables(past, seq)
    no_history = kt_past is None
    assert tq == tk or not no_history
    cbuf = jnp.pad(conv_buf.astype(F32), ((0, 0), (8 - (SSD_CONV - 1), 0), (0, 0)))
    gate, u, tail, dt, q, k, v, qt, kb, vt = _in_proj(xf, p["g_pre_mix"], p["w_in"], cos, sin, cbuf, p["conv_w"],
                                                       p["conv_b"], seq=seq, tm=tm, tt=tk if no_history else 0)

    seq_pad = -(-seq // ssd_q) * ssd_q
    if seq_pad != seq:
        pad = lambda a: jnp.pad(a.reshape(nb, seq, -1), ((0, 0), (0, seq_pad - seq), (0, 0))).reshape(nb * seq_pad, -1)
        gate_s, u_s, dt_s = pad(gate), pad(u), pad(dt)
    else:
        gate_s, u_s, dt_s = gate, u, dt
    y_ssd, h_new = _ssd(gate_s, u_s, dt_s, _state_to_kernel_layout(ssm0.astype(F32)),
                        p["dt_bias"], p["a_log"], p["d_skip"], p["ssm_norm_w"], p["e2"],
                        nb=nb, seq=seq_pad, q=ssd_q, valid=None if seq_pad == seq else seq)
    if seq_pad != seq:
        y_ssd = y_ssd.reshape(nb, seq_pad, SSD_WIDTH)[:, :seq].reshape(n, SSD_WIDTH)
    ssm_new = _state_from_kernel_layout(h_new)
    ext_tail = tail.reshape(nb, -1, 8, SSD_XBC)[:, -1]
    if seq >= SSD_CONV - 1:
        conv_new = ext_tail[:, 8 - (SSD_CONV - 1):]
    else:
        conv_new = jnp.concatenate([conv_buf.astype(F32), ext_tail[:, 8 - seq:]], axis=1)[:, -(SSD_CONV - 1):]

    if no_history:
        kv_len = seq
        y_att = _diff_attn(qt, kb, vt, p["lam_vecs"], p["subln_col"], nb=nb, tq=tq, tk=tk,
                           past=past, kv_len=kv_len, lam_init=lam_init, nbuf=ATT_PIPELINE_BUFFERS)
    else:
        y_att = _decode_attn(q, k, v, kt_past, v_past, p["lam_vecs"], p["subln_col"].reshape(1, ATT_V_DIM),
                             nb=nb, seq=seq, past=past, lam_init=lam_init)

    h = _post_mix(xf, y_ssd, y_att, mem_kb, mem_vb, p["w_out"], p["wq_x"], p["wo_x"],
                  p["g_post_mix"], p["g_pre_x"], p["g_post_x"], seq=seq, tm=tm)
    out = _ffn(h, p["w_gate"], p["w_up"], p["w_down"], p["g_pre_ffn"], p["g_post_ffn"], tm=tm)
    if no_history:
        k_out = k.reshape(nb, ATT_HEADS, 2, ATT_HEAD_DIM, seq).transpose(0, 4, 1, 2, 3)
    else:
        k_out = k.reshape(nb, seq, ATT_HEADS, 2, ATT_HEAD_DIM)
    return (out.reshape(nb, seq, D_MODEL), k_out,
            v.reshape(nb, seq, ATT_HEADS, ATT_V_DIM), ssm_new, conv_new)


def _prep_params(i, w_in, conv_w, conv_b, dt_bias, a_log, d_skip, ssm_norm_w, lam_q1, lam_k1, lam_q2, lam_k2, subln_w,
                 w_out, wq_x, wo_x, g_pre_mix, g_post_mix, g_pre_x, g_post_x, g_pre_ffn, g_post_ffn,
                 w_gate, w_up, w_down):
    w = w_in[i]
    s0 = SSD_WIDTH + SSD_XBC
    s1 = s0 + SSD_HEADS
    w_r = jnp.concatenate([w[:, :s0], w[:, s1:], w[:, s0:s1], jnp.zeros((D_MODEL, DT_PAD - SSD_HEADS), w.dtype)], axis=1)
    head_pad = lambda a: jnp.pad(a[i].astype(F32), (0, DT_PAD - SSD_HEADS)).reshape(1, DT_PAD)
    e = (jnp.arange(DT_PAD)[:, None] == (jnp.arange(SSD_WIDTH)[None, :] // SSD_HEAD_DIM)).astype(BF16)
    row = lambda a: a[i].astype(F32).reshape(1, -1)
    return {
        "w_in": w_r.astype(BF16), "conv_w": conv_w[i].astype(F32), "conv_b": row(conv_b),
        "dt_bias": head_pad(dt_bias), "a_log": head_pad(a_log), "d_skip": head_pad(d_skip),
        "ssm_norm_w": row(ssm_norm_w), "e2": jnp.concatenate([e, e], axis=0),
        "lam_vecs": jnp.stack([lam_q1[i], lam_k1[i], lam_q2[i], lam_k2[i]]).astype(F32), "subln_col": subln_w[i].astype(F32).reshape(ATT_V_DIM, 1),
        "w_out": w_out[i].astype(BF16), "wq_x": wq_x[i].astype(BF16), "wo_x": wo_x[i].astype(BF16),
        "g_pre_mix": row(g_pre_mix), "g_post_mix": row(g_post_mix), "g_pre_x": row(g_pre_x), "g_post_x": row(g_post_x),
        "g_pre_ffn": row(g_pre_ffn), "g_post_ffn": row(g_post_ffn),
        "w_gate": w_gate[i].astype(BF16), "w_up": w_up[i].astype(BF16), "w_down": w_down[i].astype(BF16),
    }


def kernel(x_prompt, x_sample, cache_attn_k, cache_attn_v, state_ssm, state_conv, cache_mem_k, cache_mem_v, mem_prompt, w_in, conv_w, conv_b, dt_bias, a_log, d_skip, ssm_norm_w, lam_q1, lam_k1, lam_q2, lam_k2, subln_w, w_out, g_mem, wq_x, wk_x, wv_x, wo_x, g_pre_mix, g_post_mix, g_pre_x, g_post_x, g_pre_ffn, g_post_ffn, w_gate, w_up, w_down):
    depth = w_in.shape[0]
    bp, sp, _ = x_prompt.shape
    bs, ss, _ = x_sample.shape
    hp, hs = x_prompt, x_sample
    outs = [[] for _ in range(10)]
    for i in range(depth):
        lam_init = 0.8 - 0.6 * math.exp(-0.3 * i)
        p = _prep_params(i, w_in, conv_w, conv_b, dt_bias, a_log, d_skip, ssm_norm_w, lam_q1, lam_k1, lam_q2, lam_k2,
                         subln_w, w_out, wq_x, wo_x, g_pre_mix, g_post_mix, g_pre_x, g_post_x, g_pre_ffn, g_post_ffn,
                         w_gate, w_up, w_down)
        mk, mv, mkb, mvb = _mem_kv(mem_prompt.reshape(bp * MEM_LEN, D_MODEL), g_mem[i].reshape(1, D_MODEL),
                                   wk_x[i].astype(BF16), wv_x[i].astype(BF16))
        hp, k_new, v_new, ssm_new, conv_new = _layer(
            hp, jnp.zeros((bp, SSD_CONV - 1, SSD_XBC), F32), jnp.zeros((bp, SSD_HEADS, SSD_HEAD_DIM, SSD_STATE), F32),
            None, None, mkb.reshape(bp, MEM_LEN, D_MODEL), mvb.reshape(bp, MEM_LEN, D_MODEL), lam_init, p,
            tm=512, tk=256, tq=256, ssd_q=256)
        for lst, val in zip(outs[:6], (k_new, v_new, ssm_new, conv_new,
                                       mk.reshape(bp, MEM_LEN, MEM_HEADS, MEM_HEAD_DIM),
                                       mv.reshape(bp, MEM_LEN, MEM_HEADS, MEM_HEAD_DIM))):
            lst.append(val)
        past = cache_attn_k.shape[2]
        hs, k_new, v_new, ssm_new, conv_new = _layer(
            hs, state_conv[i], state_ssm[i],
            cache_attn_k[i].transpose(0, 2, 3, 4, 1).reshape(bs, ATT_WIDTH, past),
            cache_attn_v[i].reshape(bs, past * ATT_HEADS, ATT_V_DIM),
            cache_mem_k[i].reshape(bs, MEM_LEN, D_MODEL).astype(BF16),
            cache_mem_v[i].reshape(bs, MEM_LEN, D_MODEL).astype(BF16), lam_init, p,
            tm=bs * ss, tk=0, tq=0, ssd_q=128)
        for lst, val in zip(outs[6:], (k_new, v_new, ssm_new, conv_new)):
            lst.append(val)
    return (hp, hs) + tuple(jnp.stack(o) for o in outs)
```

```python
import functools
import math

import jax
import jax.numpy as jnp
from jax import lax
from jax.experimental import pallas as pl
from jax.experimental.pallas import tpu as pltpu

D_MODEL = 1024
CHUNK = 64
SSD_WIDTH = 512
SSD_HEAD_DIM = 64
SSD_HEADS = 8
SSD_GROUPS = 2
SSD_REP = 4
SSD_STATE = 128
SSD_CONV = 4
SSD_XBC = 1024
SSD_NORM_GROUP = 256
SSD_NORM_EPS = 1e-5
ATT_WIDTH = 512
ATT_HEAD_DIM = 64
ATT_HEADS = 4
ATT_V_DIM = 128
ATT_NORM_EPS = 1e-5
ROPE_THETA = 10000.0
MEM_LEN = 256
MEM_HEADS = 4
MEM_HEAD_DIM = 256
FFN_HIDDEN = 2816
NORM_EPS = 1e-6
LANES = 128
DT_PAD = LANES
IN_COLS_PADDED = SSD_WIDTH + SSD_XBC + 3 * ATT_WIDTH + DT_PAD
VMEM_LIMIT = 56 * 1024 * 1024
NEG_BIG = -1e30
MXU_TILE = 256
BF16_SUBLANES = 16
ATT_VT_ROWS = ATT_V_DIM + BF16_SUBLANES
ATT_SOFTMAX_ROWS = 128
ATT_PIPELINE_BUFFERS = 2
Q_SCALE = math.log2(math.e) / math.sqrt(ATT_HEAD_DIM)

F32 = jnp.float32
BF16 = jnp.bfloat16


def _const_spec(shape):
    return pl.BlockSpec(shape, lambda *_: (0,) * len(shape), pipeline_mode=pl.Buffered(1))


def _rms(x, g, eps):
    return x * lax.rsqrt(jnp.mean(x * x, axis=-1, keepdims=True) + eps) * g


def _silu(x):
    h = 0.5 * x
    return h + h * jnp.tanh(h)


def _dot(a, b):
    return jnp.dot(a, b, preferred_element_type=F32)


def _mem_kv_kernel(mem_ref, g_ref, wk_ref, wv_ref, mk_ref, mv_ref, mkb_ref, mvb_ref):
    mn = _rms(mem_ref[...], g_ref[...], NORM_EPS).astype(BF16)
    mk = _dot(mn, wk_ref[...])
    mv = _dot(mn, wv_ref[...])
    mk_ref[...] = mk
    mv_ref[...] = mv
    mkb_ref[...] = mk.astype(BF16)
    mvb_ref[...] = mv.astype(BF16)


def _mem_kv(mem, g_mem, wk, wv):
    n = mem.shape[0]
    tm = MEM_LEN
    row = pl.BlockSpec((tm, D_MODEL), lambda i: (i, 0))
    return pl.pallas_call(
        _mem_kv_kernel,
        grid=(n // tm,),
        in_specs=[row, _const_spec((1, D_MODEL)), _const_spec((D_MODEL, D_MODEL)), _const_spec((D_MODEL, D_MODEL))],
        out_specs=[row, row, row, row],
        out_shape=[jax.ShapeDtypeStruct((n, D_MODEL), F32), jax.ShapeDtypeStruct((n, D_MODEL), F32),
                   jax.ShapeDtypeStruct((n, D_MODEL), BF16), jax.ShapeDtypeStruct((n, D_MODEL), BF16)],
        compiler_params=pltpu.CompilerParams(dimension_semantics=("arbitrary",), vmem_limit_bytes=VMEM_LIMIT),
        name="mem_kv",
    )(mem, g_mem, wk, wv)


def _in_proj_kernel(x_ref, g_ref, w_ref, cos_ref, sin_ref, cbuf_ref, cw_ref, cb_ref,
                    z_ref, u_ref, tail_ref, dt_ref, q_ref, k_ref, v_ref, qt_ref, kb_ref, vt_ref, acc_ref, xtail,
                    *, tm, tt, spt, tiles_per_seq):
    step = pl.program_id(0)
    tile = step - 1

    @pl.when(step == 0)
    def _():
        acc_ref[...] = jnp.zeros_like(acc_ref)
        xtail[...] = jnp.zeros_like(xtail)

    c_xbc, c_q = SSD_WIDTH, SSD_WIDTH + SSD_XBC
    c_k, c_v, c_dt = c_q + ATT_WIDTH, c_q + 2 * ATT_WIDTH, c_q + 3 * ATT_WIDTH
    z_ref[...] = _silu(acc_ref[:, :SSD_WIDTH]).astype(BF16)
    rows = tm // spt
    row8 = lax.broadcasted_iota(jnp.int32, (8, LANES), 0)
    first_tile_of_seq = tile % tiles_per_seq == 0
    for s_i in range(spt):
        r0 = s_i * rows
        tail_ref[s_i] = acc_ref[r0 + rows - 8:r0 + rows, c_xbc:c_q]
        for cb in range(SSD_XBC // LANES):
            sl = slice(cb * LANES, (cb + 1) * LANES)
            piece = acc_ref[r0:r0 + rows, c_xbc + cb * LANES:c_xbc + (cb + 1) * LANES]
            if spt > 1:
                prev = cbuf_ref[s_i, :, sl]
            else:
                prev = jnp.where(first_tile_of_seq, cbuf_ref[0, :, sl], xtail[:, sl])
            conv_top = cb_ref[:, sl] + piece[0:8] * cw_ref[SSD_CONV - 1:SSD_CONV, sl]
            conv_rest = cb_ref[:, sl] + piece[8:] * cw_ref[SSD_CONV - 1:SSD_CONV, sl]
            for d in range(1, SSD_CONV):
                sh = pltpu.roll(piece, d, 0)
                tap = cw_ref[SSD_CONV - 1 - d:SSD_CONV - d, sl]
                conv_top = conv_top + jnp.where(row8 < d, pltpu.roll(prev, d, 0), sh[0:8]) * tap
                conv_rest = conv_rest + sh[8:] * tap
            u_ref[r0:r0 + rows, sl] = _silu(jnp.concatenate([conv_top, conv_rest], axis=0)).astype(BF16)
    if spt == 1:
        xtail[...] = acc_ref[tm - 8:, c_xbc:c_q]
    dt_ref[...] = acc_ref[:, c_dt:c_dt + DT_PAD]
    if tt:
        for j in range(ATT_HEADS):
            v_ref[pl.ds(j, tm, stride=ATT_HEADS), :] = acc_ref[:, c_v + j * ATT_V_DIM:c_v + (j + 1) * ATT_V_DIM]
    else:
        v_ref[...] = acc_ref[:, c_v:c_v + ATT_WIDTH]

    cos = cos_ref[...]
    sin = sin_ref[...]
    first_half = (lax.broadcasted_iota(jnp.int32, (tm, LANES), 1) % ATT_HEAD_DIM) < (ATT_HEAD_DIM // 2)

    def rope(t):
        swapped = jnp.where(first_half, pltpu.roll(t, LANES - ATT_HEAD_DIM // 2, 1),
                            pltpu.roll(t, ATT_HEAD_DIM // 2, 1))
        return t * cos + swapped * sin

    for j in range(ATT_WIDTH // LANES):
        sl = slice(j * LANES, (j + 1) * LANES)
        qr = rope(acc_ref[:, c_q + j * LANES:c_q + (j + 1) * LANES]) * Q_SCALE
        kr = rope(acc_ref[:, c_k + j * LANES:c_k + (j + 1) * LANES])
        q_ref[:, sl] = qr.astype(BF16)
        if tt:
            k_ref[0, sl, :] = kr.T
            kb_ref[:, sl] = kr.astype(BF16)
            for c in range(tm // tt):
                qt_ref[c, sl, :] = qr[c * tt:(c + 1) * tt, :].T.astype(BF16)
                vt_ref[c, j * ATT_VT_ROWS:j * ATT_VT_ROWS + LANES, :] = acc_ref[
                    c * tt:(c + 1) * tt, c_v + j * LANES:c_v + (j + 1) * LANES].T.astype(BF16)
                vt_ref[c, j * ATT_VT_ROWS + LANES:(j + 1) * ATT_VT_ROWS, :] = jnp.ones((ATT_VT_ROWS - LANES, tt), BF16)
        else:
            k_ref[:, sl] = kr

    hn = _rms(x_ref[...], g_ref[...], NORM_EPS).astype(BF16)
    acc_ref[...] = _dot(hn, w_ref[...])


def _in_proj(x, g, w, cos, sin, cbuf, cw, cb, *, seq, tm, tt):
    n = x.shape[0]
    nt = n // tm
    spt = max(1, tm // seq)
    tiles_per_seq = max(1, seq // tm)
    if spt > 1:
        cos, sin = jnp.tile(cos, (spt, 1)), jnp.tile(sin, (spt, 1))
    done = lambda i: jnp.maximum(i - 1, 0)
    row = lambda c: pl.BlockSpec((tm, c), lambda i: (done(i), 0))
    tab = pl.BlockSpec((tm, LANES), lambda i: (done(i) % tiles_per_seq, 0))
    if tt:
        k_spec = pl.BlockSpec((1, ATT_WIDTH, tm), lambda i: (done(i) // tiles_per_seq, 0, done(i) % tiles_per_seq))
        k_shape = jax.ShapeDtypeStruct((n // seq, ATT_WIDTH, seq), F32)
        v_spec = pl.BlockSpec((tm * ATT_HEADS, ATT_V_DIM), lambda i: (done(i), 0))
        v_shape = jax.ShapeDtypeStruct((n * ATT_HEADS, ATT_V_DIM), F32)
    else:
        k_spec, k_shape = row(ATT_WIDTH), jax.ShapeDtypeStruct((n, ATT_WIDTH), F32)
        v_spec, v_shape = row(ATT_WIDTH), jax.ShapeDtypeStruct((n, ATT_WIDTH), F32)
    out_specs = [row(SSD_WIDTH), row(SSD_XBC), pl.BlockSpec((spt, 8, SSD_XBC), lambda i: (done(i), 0, 0)),
                 row(DT_PAD), row(ATT_WIDTH), k_spec, v_spec]
    out_shape = [jax.ShapeDtypeStruct((n, SSD_WIDTH), BF16), jax.ShapeDtypeStruct((n, SSD_XBC), BF16),
                 jax.ShapeDtypeStruct((nt * spt, 8, SSD_XBC), F32), jax.ShapeDtypeStruct((n, DT_PAD), F32),
                 jax.ShapeDtypeStruct((n, ATT_WIDTH), BF16), k_shape, v_shape]
    if tt:
        tr = lambda r: pl.BlockSpec((tm // tt, r, tt), lambda i: (done(i), 0, 0))
        vt_rows = ATT_HEADS * ATT_VT_ROWS
        out_specs += [tr(ATT_WIDTH), row(ATT_WIDTH), tr(vt_rows)]
        out_shape += [jax.ShapeDtypeStruct((n // tt, ATT_WIDTH, tt), BF16), jax.ShapeDtypeStruct((n, ATT_WIDTH), BF16),
                      jax.ShapeDtypeStruct((n // tt, vt_rows, tt), BF16)]

    def body(*refs):
        refs, scratch = refs[:-2], refs[-2:]
        refs = refs + (None,) * (18 - len(refs))
        _in_proj_kernel(*refs, *scratch, tm=tm, tt=tt, spt=spt, tiles_per_seq=tiles_per_seq)

    outs = pl.pallas_call(
        body,
        grid=(nt + 1,),
        in_specs=[pl.BlockSpec((tm, D_MODEL), lambda i: (jnp.minimum(i, nt - 1), 0)),
                  _const_spec((1, D_MODEL)), _const_spec((D_MODEL, IN_COLS_PADDED)), tab, tab,
                  pl.BlockSpec((spt, 8, SSD_XBC), lambda i: (done(i) // tiles_per_seq, 0, 0)),
                  _const_spec((SSD_CONV, SSD_XBC)), _const_spec((1, SSD_XBC))],
        out_specs=out_specs,
        out_shape=out_shape,
        scratch_shapes=[pltpu.VMEM((tm, IN_COLS_PADDED), F32), pltpu.VMEM((8, SSD_XBC), F32)],
        compiler_params=pltpu.CompilerParams(dimension_semantics=("arbitrary",), vmem_limit_bytes=VMEM_LIMIT),
        name="in_proj",
    )(x, g, w, cos, sin, cbuf, cw, cb)
    return list(outs) + [None] * (10 - len(outs))


def _expand_heads(x, e2):
    hi = x.astype(BF16)
    lo = (x - hi.astype(F32)).astype(BF16)
    return _dot(jnp.concatenate([hi, lo], axis=1), e2)


def _ssd_kernel(gate_ref, u_ref, dt_ref, h0_ref, dtb_ref, alog_ref, dsk_ref, nw_ref, e2_ref,
                y_ref, hout_ref, state, *, q, valid):
    c = pl.program_id(1)

    @pl.when(c == 0)
    def _():
        state[...] = h0_ref[0]

    xs = u_ref[:, :SSD_WIDTH].astype(F32)
    bm = u_ref[:, SSD_WIDTH:SSD_WIDTH + SSD_GROUPS * SSD_STATE]
    cmb = u_ref[:, SSD_WIDTH + SSD_GROUPS * SSD_STATE:]

    dtr = dt_ref[...] + dtb_ref[...]
    dt = jnp.maximum(dtr, 0.0) + jnp.log(1.0 + jnp.exp(-jnp.abs(dtr)))
    if valid is not None:
        row = lax.broadcasted_iota(jnp.int32, (q, DT_PAD), 0) + c * q
        dt = jnp.where(row < valid, dt, 0.0)
    a = -jnp.exp(alog_ref[...])
    ad = dt * a
    ri = lax.broadcasted_iota(jnp.int32, (q, q), 0)
    ci = lax.broadcasted_iota(jnp.int32, (q, q), 1)
    tril = ri >= ci
    tril_b = jnp.where(tril, 1.0, 0.0).astype(BF16)
    ad_hi = ad.astype(BF16)
    ad_r = ad - ad_hi.astype(F32)
    ad_mid = ad_r.astype(BF16)
    ad_lo = (ad_r - ad_mid.astype(F32)).astype(BF16)
    acum = _dot(tril_b, ad_hi) + _dot(tril_b, ad_mid) + _dot(tril_b, ad_lo)
    acum_t = acum.T
    tot = acum[q - 1:q, :]
    e2 = e2_ref[...]
    expanded = _expand_heads(jnp.concatenate([dt, dt * jnp.exp(tot - acum), jnp.exp(acum)], axis=0), e2)
    dtx = expanded[0:q]
    ddx = expanded[q:2 * q]
    eax = expanded[2 * q:3 * q]
    dsk = _expand_heads(jnp.broadcast_to(dsk_ref[...], (8, DT_PAD)), e2)[0:1]

    xdt = (xs * dtx).astype(BF16)
    xdtd = (xs * ddx).astype(BF16)
    bm_t = bm.astype(F32).T.astype(BF16)
    gw = SSD_REP * SSD_HEAD_DIM
    stripe = lax.broadcasted_iota(jnp.int32, (q, gw), 1) // SSD_HEAD_DIM
    ys = []
    for g in range(SSD_GROUPS):
        cm_g = cmb[:, g * SSD_STATE:(g + 1) * SSD_STATE]
        bt_g = bm_t[g * SSD_STATE:(g + 1) * SSD_STATE, :]
        cbm = _dot(cm_g, bt_g)
        ms = []
        for r in range(SSD_REP):
            h = g * SSD_REP + r
            diff = acum[:, h:h + 1] - acum_t[h:h + 1, :]
            ms.append((cbm * jnp.exp(jnp.where(tril, diff, -jnp.inf))).astype(BF16))
        ydf = _dot(jnp.concatenate(ms, axis=0), xdt[:, g * gw:(g + 1) * gw])
        yd = ydf[0:q]
        for r in range(1, SSD_REP):
            yd = jnp.where(stripe == r, ydf[r * q:(r + 1) * q], yd)
        st = state[g]
        y_off = _dot(cm_g, st.astype(BF16)) * eax[:, g * gw:(g + 1) * gw]
        state[g] = st * eax[q - 1:q, g * gw:(g + 1) * gw] + _dot(bt_g, xdtd[:, g * gw:(g + 1) * gw])
        ys.append(yd + y_off)
    y = jnp.concatenate(ys, axis=1) + dsk * xs
    y = y * gate_ref[...].astype(F32)
    outs = []
    for g in range(SSD_WIDTH // SSD_NORM_GROUP):
        yg = y[:, g * SSD_NORM_GROUP:(g + 1) * SSD_NORM_GROUP]
        outs.append(yg * lax.rsqrt(jnp.mean(yg * yg, axis=-1, keepdims=True) + SSD_NORM_EPS))
    y_ref[...] = (jnp.concatenate(outs, axis=1) * nw_ref[...]).astype(BF16)

    @pl.when(c == pl.num_programs(1) - 1)
    def _():
        hout_ref[0] = state[...]


def _ssd(gate, u, dt, h0, dtb, alog, dsk, nw, e2, *, nb, seq, q, valid):
    nc = seq // q
    row = lambda c: pl.BlockSpec((q, c), lambda b, i: (b * nc + i, 0))
    gw = SSD_REP * SSD_HEAD_DIM
    st_spec = pl.BlockSpec((1, SSD_GROUPS, SSD_STATE, gw), lambda b, i: (b, 0, 0, 0))
    return pl.pallas_call(
        functools.partial(_ssd_kernel, q=q, valid=valid),
        grid=(nb, nc),
        in_specs=[row(SSD_WIDTH), row(SSD_XBC), row(DT_PAD), st_spec,
                  _const_spec((1, DT_PAD)), _const_spec((1, DT_PAD)), _const_spec((1, DT_PAD)),
                  _const_spec((1, SSD_WIDTH)), _const_spec((2 * DT_PAD, SSD_WIDTH))],
        out_specs=[row(SSD_WIDTH), st_spec],
        out_shape=[jax.ShapeDtypeStruct((nb * seq, SSD_WIDTH), BF16),
                   jax.ShapeDtypeStruct((nb, SSD_GROUPS, SSD_STATE, gw), F32)],
        scratch_shapes=[pltpu.VMEM((SSD_GROUPS, SSD_STATE, gw), F32)],
        compiler_params=pltpu.CompilerParams(dimension_semantics=("arbitrary", "arbitrary"),
                                             vmem_limit_bytes=VMEM_LIMIT),
        name="ssd",
    )(gate, u, dt, h0, dtb, alog, dsk, nw, e2)


def _column_max(x):
    while x.shape[0] > 8 and x.shape[0] % 16 == 0:
        half = x.shape[0] // 2
        x = jnp.maximum(x[:half], x[half:])
    return jnp.max(x, axis=0, keepdims=True)


def _attn_tile_counts(i, *, tq, tk, past, kv_len, minimum=min):
    q_lo = past + i * tq
    q_hi = q_lo + tq - 1
    lim_lo = minimum((q_lo // CHUNK + 1) * CHUNK, kv_len)
    lim_hi = minimum((q_hi // CHUNK + 1) * CHUNK, kv_len)
    return lim_lo // tk, (lim_hi + tk - 1) // tk


def _diff_attn_kernel(qt_ref, k_ref, vt_ref, lam_ref, sw_ref, o_ref,
                      q2t_ref, s_ref, p_ref, m_ref, alpha_ref, acc_ref, *, tq, tk, past, kv_len, lam_init, nbuf):
    i = pl.program_id(2)
    w = 2 * tq
    qt = qt_ref[0]
    row = lax.broadcasted_iota(jnp.int32, (ATT_V_DIM, tq), 0)
    zero = jnp.zeros_like(qt)
    q2t_ref[:, 0:tq] = jnp.where(row < ATT_HEAD_DIM, qt, zero)
    q2t_ref[:, tq:w] = jnp.where(row >= ATT_HEAD_DIM, qt, zero)

    q_lo = past + i * tq
    n_full, _ = _attn_tile_counts(i, tq=tq, tk=tk, past=past, kv_len=kv_len, minimum=jnp.minimum)
    n_visits = n_full + 1

    def scores(j):
        kt = k_ref[pl.ds(pl.multiple_of(j * tk, tk), tk), :]
        return _dot(kt, q2t_ref[...])

    def visited_tile(v):
        return jnp.where(v == 0, n_full, jnp.maximum(v - 1, 0))

    ncb = w // LANES
    pw = p_ref.shape[-1]

    def store_scores(buf, s):
        for cb in range(ncb):
            s_ref[buf, cb] = s[:, cb * LANES:(cb + 1) * LANES]

    def stage_a(v, buf):
        store_scores(buf, scores(jnp.minimum(v - 1, n_full)))

    rows = min(tk, ATT_SOFTMAX_ROWS)

    def stage_b(buf):
        for cb in range(ncb):
            sl = slice(cb * LANES, (cb + 1) * LANES)
            m_old = m_ref[:, sl]
            m_new = m_old
            for r0 in range(0, tk, rows):
                m_new = jnp.maximum(m_new, _column_max(s_ref[buf, cb, r0:r0 + rows, :]))
            m_ref[:, sl] = m_new
            alpha_ref[buf, :, sl] = jnp.exp2(m_old - m_new)
            pl0 = (cb * LANES) % pw
            for r0 in range(0, tk, rows):
                p_ref[buf, cb * LANES // pw, r0:r0 + rows, pl0:pl0 + LANES] = jnp.exp2(
                    (s_ref[buf, cb, r0:r0 + rows, :] - m_new).astype(BF16))

    def stage_c(v, buf):
        vt = vt_ref[visited_tile(v)]
        for pb in range(w // pw):
            sl = slice(pb * pw, (pb + 1) * pw)
            acc_ref[:, sl] = alpha_ref[buf, :, sl] * acc_ref[:, sl] + _dot(vt, p_ref[buf, pb])

    qchunk = (q_lo + lax.broadcasted_iota(jnp.int32, (1, w), 1) % tq) // CHUNK
    s_part = scores(n_full)
    s_masked = []
    for kb in range(tk // CHUNK):
        k0 = n_full * tk + kb * CHUNK
        kchunk = jnp.where(k0 < kv_len, k0 // CHUNK, jnp.iinfo(jnp.int32).max)
        s_masked.append(jnp.where(kchunk <= qchunk, s_part[kb * CHUNK:(kb + 1) * CHUNK], NEG_BIG))
    store_scores(0, jnp.concatenate(s_masked, axis=0))
    m_ref[...] = jnp.full_like(m_ref, NEG_BIG)
    acc_ref[...] = jnp.zeros_like(acc_ref)
    p_ref[nbuf - 1] = jnp.zeros(p_ref.shape[1:], BF16)
    alpha_ref[nbuf - 1] = jnp.ones(alpha_ref.shape[1:], F32)

    def visits(v0, count, prefetch_last):
        for r in range(count):
            prefetch = r + 1 < count or prefetch_last
            if prefetch and nbuf > 1:
                stage_a(v0 + r + 1, (r + 1) % nbuf)
            stage_c(v0 + r - 1, (r - 1) % nbuf)
            stage_b(r)
            if prefetch and nbuf == 1:
                stage_a(v0 + r + 1, (r + 1) % nbuf)

    def trip(u, carry):
        visits(nbuf * u, nbuf, True)
        return carry

    lax.fori_loop(0, n_visits // nbuf, trip, 0)
    for rem in range(nbuf):

        @pl.when(n_visits % nbuf == rem)
        def _():
            visits(n_visits - rem, rem, False)
            stage_c(n_visits - 1, (rem - 1) % nbuf)

    lam_v = lam_ref[...]
    lam = (jnp.exp(jnp.sum(lam_v[0:1] * lam_v[1:2], axis=-1, keepdims=True))
           - jnp.exp(jnp.sum(lam_v[2:3] * lam_v[3:4], axis=-1, keepdims=True)) + lam_init)
    acc = acc_ref[...]
    o = acc[:ATT_V_DIM] / acc[ATT_V_DIM:ATT_V_DIM + 1]
    o = o[:, :tq] - lam * o[:, tq:]
    o = o * lax.rsqrt(jnp.mean(o * o, axis=0, keepdims=True) + ATT_NORM_EPS) * (sw_ref[...] * (1.0 - lam_init))
    o_ref[...] = o.T.astype(BF16)


def _diff_attn(qt, k, vt, lam_vecs, subln_col, *, nb, tq, tk, past, kv_len, lam_init, nbuf):
    nq = qt.shape[0] // nb
    nkt = vt.shape[0] // nb
    assert tk % CHUNK == 0 and kv_len % CHUNK == 0
    for i in range(nq):
        n_full, n_end = _attn_tile_counts(i, tq=tq, tk=tk, past=past, kv_len=kv_len)
        assert n_end - n_full == 1 and n_end <= nkt, (i, n_full, n_end)
    w = 2 * tq
    return pl.pallas_call(
        functools.partial(_diff_attn_kernel, tq=tq, tk=tk, past=past, kv_len=kv_len, lam_init=lam_init, nbuf=nbuf),
        grid=(nb, ATT_HEADS, nq),
        in_specs=[pl.BlockSpec((1, ATT_V_DIM, tq), lambda b, h, i: (b * nq + i, h, 0)),
                  pl.BlockSpec((nkt * tk, ATT_V_DIM), lambda b, h, i: (b, h)),
                  pl.BlockSpec((nkt, ATT_VT_ROWS, tk), lambda b, h, i: (b, h, 0)),
                  _const_spec((4, ATT_HEAD_DIM)), _const_spec((ATT_V_DIM, 1))],
        out_specs=pl.BlockSpec((tq, ATT_V_DIM), lambda b, h, i: (b * nq + i, h)),
        out_shape=jax.ShapeDtypeStruct((nb * nq * tq, ATT_WIDTH), BF16),
        scratch_shapes=[pltpu.VMEM((ATT_V_DIM, w), BF16), pltpu.VMEM((nbuf, w // LANES, tk, LANES), F32),
                        pltpu.VMEM((nbuf, w // MXU_TILE, tk, MXU_TILE), BF16), pltpu.VMEM((1, w), F32),
                        pltpu.VMEM((nbuf, 1, w), F32),
                        pltpu.VMEM((ATT_VT_ROWS, w), F32)],
        compiler_params=pltpu.CompilerParams(dimension_semantics=("arbitrary", "arbitrary", "arbitrary"),
                                             vmem_limit_bytes=VMEM_LIMIT),
        name="diff_attn",
    )(qt, k, vt, lam_vecs, subln_col)


def _decode_attn_kernel(q_ref, kn_ref, vn_ref, kc_ref, vc_ref, lam_ref, sw_ref, o_ref, *, seq, past, lam_init):
    lam_v = lam_ref[...]
    lam = (jnp.exp(jnp.sum(lam_v[0:1] * lam_v[1:2], axis=-1, keepdims=True))
           - jnp.exp(jnp.sum(lam_v[2:3] * lam_v[3:4], axis=-1, keepdims=True)) + lam_init)
    npad = LANES
    lane = lax.broadcasted_iota(jnp.int32, (seq, LANES), 1)
    qchunk = (past + lax.broadcasted_iota(jnp.int32, (2 * seq, 1), 0) % seq) // CHUNK
    kpos_p = lax.broadcasted_iota(jnp.int32, (2 * seq, past), 1)
    kpos_n = lax.broadcasted_iota(jnp.int32, (2 * seq, npad), 1)
    vis_p = kpos_p // CHUNK <= qchunk
    vis_n = ((past + kpos_n) // CHUNK <= qchunk) & (kpos_n < seq)
    pad_rows = jnp.zeros((npad - seq, LANES), BF16)
    for h in range(ATT_HEADS):
        sl = slice(h * LANES, (h + 1) * LANES)
        qh = q_ref[:, sl]
        zero = jnp.zeros_like(qh)
        q2 = jnp.concatenate([jnp.where(lane < ATT_HEAD_DIM, qh, zero), jnp.where(lane >= ATT_HEAD_DIM, qh, zero)], axis=0)
        s_p = jnp.where(vis_p, _dot(q2, kc_ref[0, sl, :].astype(BF16)), NEG_BIG)
        kn = jnp.concatenate([kn_ref[:, sl].astype(BF16), pad_rows], axis=0)
        s_n = lax.dot_general(q2, kn, (((1,), (1,)), ((), ())), preferred_element_type=F32)
        s_n = jnp.where(vis_n, s_n, NEG_BIG)
        m = jnp.maximum(jnp.max(s_p, axis=-1, keepdims=True), jnp.max(s_n, axis=-1, keepdims=True))
        p_p = jnp.exp2(s_p - m)
        p_n = jnp.exp2(s_n - m)
        l = jnp.sum(p_p, axis=-1, keepdims=True) + jnp.sum(p_n, axis=-1, keepdims=True)
        vh = vc_ref[0, pl.ds(h, past, stride=ATT_HEADS), :].astype(BF16)
        vn = jnp.concatenate([vn_ref[:, sl].astype(BF16), pad_rows], axis=0)
        o = (_dot(p_p.astype(BF16), vh) + _dot(p_n.astype(BF16), vn)) / l
        o = o[:seq] - lam * o[seq:]
        o = o * lax.rsqrt(jnp.mean(o * o, axis=-1, keepdims=True) + ATT_NORM_EPS) * (sw_ref[...] * (1.0 - lam_init))
        o_ref[:, sl] = o.astype(BF16)


def _decode_attn(q, k_new, v_new, kt_cache, v_cache, lam_vecs, subln_row, *, nb, seq, past, lam_init):
    row = pl.BlockSpec((seq, ATT_WIDTH), lambda b: (b, 0))
    return pl.pallas_call(
        functools.partial(_decode_attn_kernel, seq=seq, past=past, lam_init=lam_init),
        grid=(nb,),
        in_specs=[row, row, row,
                  pl.BlockSpec((1, ATT_WIDTH, past), lambda b: (b, 0, 0)),
                  pl.BlockSpec((1, past * ATT_HEADS, ATT_V_DIM), lambda b: (b, 0, 0)),
                  _const_spec((4, ATT_HEAD_DIM)), _const_spec((1, ATT_V_DIM))],
        out_specs=row,
        out_shape=jax.ShapeDtypeStruct((nb * seq, ATT_WIDTH), BF16),
        compiler_params=pltpu.CompilerParams(dimension_semantics=("arbitrary",), vmem_limit_bytes=VMEM_LIMIT),
        name="decode_attn",
    )(q, k_new, v_new, kt_cache, v_cache, lam_vecs, subln_row)


def _post_mix_kernel(x_ref, ys_ref, ya_ref, mk_ref, mv_ref, wo_ref, wq_ref, wox_ref,
                     g1_ref, g2_ref, g3_ref, h_ref, *, spt, rows):
    mix = _dot(ys_ref[...], wo_ref[0:SSD_WIDTH, :]) + _dot(ya_ref[...], wo_ref[SSD_WIDTH:, :])
    h = x_ref[...] + _rms(mix, g1_ref[...], NORM_EPS)
    qn = _rms(h, g2_ref[...], NORM_EPS).astype(BF16)
    qx = (_dot(qn, wq_ref[...]) * (1.0 / math.sqrt(MEM_HEAD_DIM))).astype(BF16)
    ox_seqs = []
    for s_i in range(spt):
        qs = qx[s_i * rows:(s_i + 1) * rows]
        mk = mk_ref[s_i]
        mv = mv_ref[s_i]
        oxs = []
        for hd in range(MEM_HEADS):
            sl = slice(hd * MEM_HEAD_DIM, (hd + 1) * MEM_HEAD_DIM)
            s = lax.dot_general(qs[:, sl], mk[:, sl], (((1,), (1,)), ((), ())), preferred_element_type=F32)
            p = jnp.exp(s - jnp.max(s, axis=-1, keepdims=True))
            ox = _dot(p.astype(BF16), mv[:, sl]) / jnp.sum(p, axis=-1, keepdims=True)
            oxs.append(ox.astype(BF16))
        ox_seqs.append(jnp.concatenate(oxs, axis=1))
    ox_all = ox_seqs[0] if spt == 1 else jnp.concatenate(ox_seqs, axis=0)
    o2 = _dot(ox_all, wox_ref[...])
    h_ref[...] = h + _rms(o2, g3_ref[...], NORM_EPS)


def _post_mix(x, ys, ya, mk, mv, w_out, wq, wox, g1, g2, g3, *, seq, tm):
    n = x.shape[0]
    spt = max(1, tm // seq)
    tiles_per_seq = max(1, seq // tm)
    row = lambda c: pl.BlockSpec((tm, c), lambda i: (i, 0))
    mem = pl.BlockSpec((spt, MEM_LEN, D_MODEL), lambda i: (i // tiles_per_seq, 0, 0))
    wspec = _const_spec((D_MODEL, D_MODEL))
    gspec = _const_spec((1, D_MODEL))
    return pl.pallas_call(
        functools.partial(_post_mix_kernel, spt=spt, rows=tm // spt),
        grid=(n // tm,),
        in_specs=[row(D_MODEL), row(SSD_WIDTH), row(ATT_WIDTH), mem, mem, wspec, wspec, wspec, gspec, gspec, gspec],
        out_specs=row(D_MODEL),
        out_shape=jax.ShapeDtypeStruct((n, D_MODEL), F32),
        compiler_params=pltpu.CompilerParams(dimension_semantics=("arbitrary",), vmem_limit_bytes=VMEM_LIMIT),
        name="post_mix",
    )(x, ys, ya, mk, mv, w_out, wq, wox, g1, g2, g3)


def _ffn_kernel(h_ref, wg_ref, wu_ref, wd_ref, g1_ref, g2_ref, o_ref):
    h = h_ref[...]
    hn = _rms(h, g1_ref[...], NORM_EPS).astype(BF16)
    act = (_silu(_dot(hn, wg_ref[...])) * _dot(hn, wu_ref[...])).astype(BF16)
    f = _dot(act, wd_ref[...])
    o_ref[...] = h + _rms(f, g2_ref[...], NORM_EPS)


def _ffn(h, wg, wu, wd, g1, g2, *, tm):
    n = h.shape[0]
    row = pl.BlockSpec((tm, D_MODEL), lambda i: (i, 0))
    return pl.pallas_call(
        _ffn_kernel,
        grid=(n // tm,),
        in_specs=[row, _const_spec((D_MODEL, FFN_HIDDEN)), _const_spec((D_MODEL, FFN_HIDDEN)),
                  _const_spec((FFN_HIDDEN, D_MODEL)), _const_spec((1, D_MODEL)), _const_spec((1, D_MODEL))],
        out_specs=row,
        out_shape=jax.ShapeDtypeStruct((n, D_MODEL), F32),
        compiler_params=pltpu.CompilerParams(dimension_semantics=("arbitrary",), vmem_limit_bytes=VMEM_LIMIT),
        name="ffn",
    )(h, wg, wu, wd, g1, g2)


def _rope_tables(past, seq):
    half = ATT_HEAD_DIM // 2
    inv = jnp.power(ROPE_THETA, -jnp.arange(0, ATT_HEAD_DIM, 2, dtype=F32) / ATT_HEAD_DIM)
    pos = (past + jnp.arange(seq, dtype=jnp.int32)).astype(F32)
    ang = pos[:, None] * inv[None, :]
    cos, sin = jnp.cos(ang), jnp.sin(ang)
    reps = LANES // ATT_HEAD_DIM
    assert half * 2 == ATT_HEAD_DIM
    return jnp.tile(jnp.concatenate([cos, cos], axis=-1), (1, reps)), jnp.tile(jnp.concatenate([-sin, sin], axis=-1), (1, reps))


def _state_to_kernel_layout(s):
    b = s.shape[0]
    s = s.reshape(b, SSD_GROUPS, SSD_REP, SSD_HEAD_DIM, SSD_STATE)
    return s.transpose(0, 1, 4, 2, 3).reshape(b, SSD_GROUPS, SSD_STATE, SSD_REP * SSD_HEAD_DIM)


def _state_from_kernel_layout(s):
    b = s.shape[0]
    s = s.reshape(b, SSD_GROUPS, SSD_STATE, SSD_REP, SSD_HEAD_DIM)
    return s.transpose(0, 1, 3, 4, 2).reshape(b, SSD_HEADS, SSD_HEAD_DIM, SSD_STATE)


def _layer(x, conv_buf, ssm0, kt_past, v_past, mem_kb, mem_vb, lam_init, p, *, tm, tk, tq, ssd_q):
    nb, seq, _ = x.shape
    n = nb * seq
    past = 0 if kt_past is None else kt_past.shape[2]
    xf = x.reshape(n, D_MODEL)
    cos, sin = _rope_tables(past, seq)
    no_history = kt_past is None
    assert tq == tk or not no_history
    cbuf = jnp.pad(conv_buf.astype(F32), ((0, 0), (8 - (SSD_CONV - 1), 0), (0, 0)))
    gate, u, tail, dt, q, k, v, qt, kb, vt = _in_proj(xf, p["g_pre_mix"], p["w_in"], cos, sin, cbuf, p["conv_w"],
                                                       p["conv_b"], seq=seq, tm=tm, tt=tk if no_history else 0)

    seq_pad = -(-seq // ssd_q) * ssd_q
    if seq_pad != seq:
        pad = lambda a: jnp.pad(a.reshape(nb, seq, -1), ((0, 0), (0, seq_pad - seq), (0, 0))).reshape(nb * seq_pad, -1)
        gate_s, u_s, dt_s = pad(gate), pad(u), pad(dt)
    else:
        gate_s, u_s, dt_s = gate, u, dt
    y_ssd, h_new = _ssd(gate_s, u_s, dt_s, _state_to_kernel_layout(ssm0.astype(F32)),
                        p["dt_bias"], p["a_log"], p["d_skip"], p["ssm_norm_w"], p["e2"],
                        nb=nb, seq=seq_pad, q=ssd_q, valid=None if seq_pad == seq else seq)
    if seq_pad != seq:
        y_ssd = y_ssd.reshape(nb, seq_pad, SSD_WIDTH)[:, :seq].reshape(n, SSD_WIDTH)
    ssm_new = _state_from_kernel_layout(h_new)
    ext_tail = tail.reshape(nb, -1, 8, SSD_XBC)[:, -1]
    if seq >= SSD_CONV - 1:
        conv_new = ext_tail[:, 8 - (SSD_CONV - 1):]
    else:
        conv_new = jnp.concatenate([conv_buf.astype(F32), ext_tail[:, 8 - seq:]], axis=1)[:, -(SSD_CONV - 1):]

    if no_history:
        kv_len = seq
        y_att = _diff_attn(qt, kb, vt, p["lam_vecs"], p["subln_col"], nb=nb, tq=tq, tk=tk,
                           past=past, kv_len=kv_len, lam_init=lam_init, nbuf=ATT_PIPELINE_BUFFERS)
    else:
        y_att = _decode_attn(q, k, v, kt_past, v_past, p["lam_vecs"], p["subln_col"].reshape(1, ATT_V_DIM),
                             nb=nb, seq=seq, past=past, lam_init=lam_init)

    h = _post_mix(xf, y_ssd, y_att, mem_kb, mem_vb, p["w_out"], p["wq_x"], p["wo_x"],
                  p["g_post_mix"], p["g_pre_x"], p["g_post_x"], seq=seq, tm=tm)
    out = _ffn(h, p["w_gate"], p["w_up"], p["w_down"], p["g_pre_ffn"], p["g_post_ffn"], tm=tm)
    if no_history:
        k_out = k.reshape(nb, ATT_HEADS, 2, ATT_HEAD_DIM, seq).transpose(0, 4, 1, 2, 3)
    else:
        k_out = k.reshape(nb, seq, ATT_HEADS, 2, ATT_HEAD_DIM)
    return (out.reshape(nb, seq, D_MODEL), k_out,
            v.reshape(nb, seq, ATT_HEADS, ATT_V_DIM), ssm_new, conv_new)


def _prep_params(i, w_in, conv_w, conv_b, dt_bias, a_log, d_skip, ssm_norm_w, lam_q1, lam_k1, lam_q2, lam_k2, subln_w,
                 w_out, wq_x, wo_x, g_pre_mix, g_post_mix, g_pre_x, g_post_x, g_pre_ffn, g_post_ffn,
                 w_gate, w_up, w_down):
    w = w_in[i]
    s0 = SSD_WIDTH + SSD_XBC
    s1 = s0 + SSD_HEADS
    w_r = jnp.concatenate([w[:, :s0], w[:, s1:], w[:, s0:s1], jnp.zeros((D_MODEL, DT_PAD - SSD_HEADS), w.dtype)], axis=1)
    head_pad = lambda a: jnp.pad(a[i].astype(F32), (0, DT_PAD - SSD_HEADS)).reshape(1, DT_PAD)
    e = (jnp.arange(DT_PAD)[:, None] == (jnp.arange(SSD_WIDTH)[None, :] // SSD_HEAD_DIM)).astype(BF16)
    row = lambda a: a[i].astype(F32).reshape(1, -1)
    return {
        "w_in": w_r.astype(BF16), "conv_w": conv_w[i].astype(F32), "conv_b": row(conv_b),
        "dt_bias": head_pad(dt_bias), "a_log": head_pad(a_log), "d_skip": head_pad(d_skip),
        "ssm_norm_w": row(ssm_norm_w), "e2": jnp.concatenate([e, e], axis=0),
        "lam_vecs": jnp.stack([lam_q1[i], lam_k1[i], lam_q2[i], lam_k2[i]]).astype(F32), "subln_col": subln_w[i].astype(F32).reshape(ATT_V_DIM, 1),
        "w_out": w_out[i].astype(BF16), "wq_x": wq_x[i].astype(BF16), "wo_x": wo_x[i].astype(BF16),
        "g_pre_mix": row(g_pre_mix), "g_post_mix": row(g_post_mix), "g_pre_x": row(g_pre_x), "g_post_x": row(g_post_x),
        "g_pre_ffn": row(g_pre_ffn), "g_post_ffn": row(g_post_ffn),
        "w_gate": w_gate[i].astype(BF16), "w_up": w_up[i].astype(BF16), "w_down": w_down[i].astype(BF16),
    }


def kernel(x_prompt, x_sample, cache_attn_k, cache_attn_v, state_ssm, state_conv, cache_mem_k, cache_mem_v, mem_prompt, w_in, conv_w, conv_b, dt_bias, a_log, d_skip, ssm_norm_w, lam_q1, lam_k1, lam_q2, lam_k2, subln_w, w_out, g_mem, wq_x, wk_x, wv_x, wo_x, g_pre_mix, g_post_mix, g_pre_x, g_post_x, g_pre_ffn, g_post_ffn, w_gate, w_up, w_down):
    depth = w_in.shape[0]
    bp, sp, _ = x_prompt.shape
    bs, ss, _ = x_sample.shape
    hp, hs = x_prompt, x_sample
    outs = [[] for _ in range(10)]
    for i in range(depth):
        lam_init = 0.8 - 0.6 * math.exp(-0.3 * i)
        p = _prep_params(i, w_in, conv_w, conv_b, dt_bias, a_log, d_skip, ssm_norm_w, lam_q1, lam_k1, lam_q2, lam_k2,
                         subln_w, w_out, wq_x, wo_x, g_pre_mix, g_post_mix, g_pre_x, g_post_x, g_pre_ffn, g_post_ffn,
                         w_gate, w_up, w_down)
        mk, mv, mkb, mvb = _mem_kv(mem_prompt.reshape(bp * MEM_LEN, D_MODEL), g_mem[i].reshape(1, D_MODEL),
                                   wk_x[i].astype(BF16), wv_x[i].astype(BF16))
        hp, k_new, v_new, ssm_new, conv_new = _layer(
            hp, jnp.zeros((bp, SSD_CONV - 1, SSD_XBC), F32), jnp.zeros((bp, SSD_HEADS, SSD_HEAD_DIM, SSD_STATE), F32),
            None, None, mkb.reshape(bp, MEM_LEN, D_MODEL), mvb.reshape(bp, MEM_LEN, D_MODEL), lam_init, p,
            tm=512, tk=512, tq=512, ssd_q=256)
        for lst, val in zip(outs[:6], (k_new, v_new, ssm_new, conv_new,
                                       mk.reshape(bp, MEM_LEN, MEM_HEADS, MEM_HEAD_DIM),
                                       mv.reshape(bp, MEM_LEN, MEM_HEADS, MEM_HEAD_DIM))):
            lst.append(val)
        past = cache_attn_k.shape[2]
        hs, k_new, v_new, ssm_new, conv_new = _layer(
            hs, state_conv[i], state_ssm[i],
            cache_attn_k[i].transpose(0, 2, 3, 4, 1).reshape(bs, ATT_WIDTH, past),
            cache_attn_v[i].reshape(bs, past * ATT_HEADS, ATT_V_DIM),
            cache_mem_k[i].reshape(bs, MEM_LEN, D_MODEL).astype(BF16),
            cache_mem_v[i].reshape(bs, MEM_LEN, D_MODEL).astype(BF16), lam_init, p,
            tm=bs * ss, tk=0, tq=0, ssd_q=128)
        for lst, val in zip(outs[6:], (k_new, v_new, ssm_new, conv_new)):
            lst.append(val)
    return (hp, hs) + tuple(jnp.stack(o) for o in outs)
```

```python
import functools
import math

import jax
import jax.numpy as jnp
from jax import lax
from jax.experimental import pallas as pl
from jax.experimental.pallas import tpu as pltpu

D_MODEL = 1024
CHUNK = 64
SSD_WIDTH = 512
SSD_HEAD_DIM = 64
SSD_HEADS = 8
SSD_GROUPS = 2
SSD_REP = 4
SSD_STATE = 128
SSD_CONV = 4
SSD_XBC = 1024
SSD_NORM_GROUP = 256
SSD_NORM_EPS = 1e-5
ATT_WIDTH = 512
ATT_HEAD_DIM = 64
ATT_HEADS = 4
ATT_V_DIM = 128
ATT_NORM_EPS = 1e-5
ROPE_THETA = 10000.0
MEM_LEN = 256
MEM_HEADS = 4
MEM_HEAD_DIM = 256
FFN_HIDDEN = 2816
NORM_EPS = 1e-6
LANES = 128
DT_PAD = LANES
IN_COLS_PADDED = SSD_WIDTH + SSD_XBC + 3 * ATT_WIDTH + DT_PAD
VMEM_LIMIT = 56 * 1024 * 1024
NEG_BIG = -1e30
MXU_TILE = 256
BF16_SUBLANES = 16
ATT_VT_ROWS = ATT_V_DIM + BF16_SUBLANES
ATT_SOFTMAX_ROWS = 128
ATT_PIPELINE_BUFFERS = 2
Q_SCALE = math.log2(math.e) / math.sqrt(ATT_HEAD_DIM)

F32 = jnp.float32
BF16 = jnp.bfloat16


def _const_spec(shape):
    return pl.BlockSpec(shape, lambda *_: (0,) * len(shape), pipeline_mode=pl.Buffered(1))


def _rms(x, g, eps):
    return x * lax.rsqrt(jnp.mean(x * x, axis=-1, keepdims=True) + eps) * g


def _silu(x):
    h = 0.5 * x
    return h + h * jnp.tanh(h)


def _dot(a, b):
    return jnp.dot(a, b, preferred_element_type=F32)


def _mem_kv_kernel(mem_ref, g_ref, wk_ref, wv_ref, mk_ref, mv_ref, mkb_ref, mvb_ref):
    mn = _rms(mem_ref[...], g_ref[...], NORM_EPS).astype(BF16)
    mk = _dot(mn, wk_ref[...])
    mv = _dot(mn, wv_ref[...])
    mk_ref[...] = mk
    mv_ref[...] = mv
    mkb_ref[...] = mk.astype(BF16)
    mvb_ref[...] = mv.astype(BF16)


def _mem_kv(mem, g_mem, wk, wv):
    n = mem.shape[0]
    tm = MEM_LEN
    row = pl.BlockSpec((tm, D_MODEL), lambda i: (i, 0))
    return pl.pallas_call(
        _mem_kv_kernel,
        grid=(n // tm,),
        in_specs=[row, _const_spec((1, D_MODEL)), _const_spec((D_MODEL, D_MODEL)), _const_spec((D_MODEL, D_MODEL))],
        out_specs=[row, row, row, row],
        out_shape=[jax.ShapeDtypeStruct((n, D_MODEL), F32), jax.ShapeDtypeStruct((n, D_MODEL), F32),
                   jax.ShapeDtypeStruct((n, D_MODEL), BF16), jax.ShapeDtypeStruct((n, D_MODEL), BF16)],
        compiler_params=pltpu.CompilerParams(dimension_semantics=("arbitrary",), vmem_limit_bytes=VMEM_LIMIT),
        name="mem_kv",
    )(mem, g_mem, wk, wv)


def _in_proj_kernel(x_ref, g_ref, w_ref, cos_ref, sin_ref, cbuf_ref, cw_ref, cb_ref,
                    z_ref, u_ref, tail_ref, dt_ref, q_ref, k_ref, v_ref, qt_ref, kb_ref, vt_ref, acc_ref, xtail,
                    *, tm, tt, spt, tiles_per_seq):
    step = pl.program_id(0)
    tile = step - 1

    @pl.when(step == 0)
    def _():
        acc_ref[...] = jnp.zeros_like(acc_ref)
        xtail[...] = jnp.zeros_like(xtail)

    c_xbc, c_q = SSD_WIDTH, SSD_WIDTH + SSD_XBC
    c_k, c_v, c_dt = c_q + ATT_WIDTH, c_q + 2 * ATT_WIDTH, c_q + 3 * ATT_WIDTH
    z_ref[...] = _silu(acc_ref[:, :SSD_WIDTH]).astype(BF16)
    rows = tm // spt
    row8 = lax.broadcasted_iota(jnp.int32, (8, LANES), 0)
    first_tile_of_seq = tile % tiles_per_seq == 0
    for s_i in range(spt):
        r0 = s_i * rows
        tail_ref[s_i] = acc_ref[r0 + rows - 8:r0 + rows, c_xbc:c_q]
        for cb in range(SSD_XBC // LANES):
            sl = slice(cb * LANES, (cb + 1) * LANES)
            piece = acc_ref[r0:r0 + rows, c_xbc + cb * LANES:c_xbc + (cb + 1) * LANES]
            if spt > 1:
                prev = cbuf_ref[s_i, :, sl]
            else:
                prev = jnp.where(first_tile_of_seq, cbuf_ref[0, :, sl], xtail[:, sl])
            conv_top = cb_ref[:, sl] + piece[0:8] * cw_ref[SSD_CONV - 1:SSD_CONV, sl]
            conv_rest = cb_ref[:, sl] + piece[8:] * cw_ref[SSD_CONV - 1:SSD_CONV, sl]
            for d in range(1, SSD_CONV):
                sh = pltpu.roll(piece, d, 0)
                tap = cw_ref[SSD_CONV - 1 - d:SSD_CONV - d, sl]
                conv_top = conv_top + jnp.where(row8 < d, pltpu.roll(prev, d, 0), sh[0:8]) * tap
                conv_rest = conv_rest + sh[8:] * tap
            u_ref[r0:r0 + rows, sl] = _silu(jnp.concatenate([conv_top, conv_rest], axis=0)).astype(BF16)
    if spt == 1:
        xtail[...] = acc_ref[tm - 8:, c_xbc:c_q]
    dt_ref[...] = acc_ref[:, c_dt:c_dt + DT_PAD]
    if tt:
        for j in range(ATT_HEADS):
            v_ref[pl.ds(j, tm, stride=ATT_HEADS), :] = acc_ref[:, c_v + j * ATT_V_DIM:c_v + (j + 1) * ATT_V_DIM]
    else:
        v_ref[...] = acc_ref[:, c_v:c_v + ATT_WIDTH]

    cos = cos_ref[...]
    sin = sin_ref[...]
    first_half = (lax.broadcasted_iota(jnp.int32, (tm, LANES), 1) % ATT_HEAD_DIM) < (ATT_HEAD_DIM // 2)

    def rope(t):
        swapped = jnp.where(first_half, pltpu.roll(t, LANES - ATT_HEAD_DIM // 2, 1),
                            pltpu.roll(t, ATT_HEAD_DIM // 2, 1))
        return t * cos + swapped * sin

    for j in range(ATT_WIDTH // LANES):
        sl = slice(j * LANES, (j + 1) * LANES)
        qr = rope(acc_ref[:, c_q + j * LANES:c_q + (j + 1) * LANES]) * Q_SCALE
        kr = rope(acc_ref[:, c_k + j * LANES:c_k + (j + 1) * LANES])
        q_ref[:, sl] = qr.astype(BF16)
        if tt:
            k_ref[0, sl, :] = kr.T
            kb_ref[:, sl] = kr.astype(BF16)
            for c in range(tm // tt):
                qt_ref[c, sl, :] = qr[c * tt:(c + 1) * tt, :].T.astype(BF16)
                vt_ref[c, j * ATT_VT_ROWS:j * ATT_VT_ROWS + LANES, :] = acc_ref[
                    c * tt:(c + 1) * tt, c_v + j * LANES:c_v + (j + 1) * LANES].T.astype(BF16)
                vt_ref[c, j * ATT_VT_ROWS + LANES:(j + 1) * ATT_VT_ROWS, :] = jnp.ones((ATT_VT_ROWS - LANES, tt), BF16)
        else:
            k_ref[:, sl] = kr

    hn = _rms(x_ref[...], g_ref[...], NORM_EPS).astype(BF16)
    acc_ref[...] = _dot(hn, w_ref[...])


def _in_proj(x, g, w, cos, sin, cbuf, cw, cb, *, seq, tm, tt):
    n = x.shape[0]
    nt = n // tm
    spt = max(1, tm // seq)
    tiles_per_seq = max(1, seq // tm)
    if spt > 1:
        cos, sin = jnp.tile(cos, (spt, 1)), jnp.tile(sin, (spt, 1))
    done = lambda i: jnp.maximum(i - 1, 0)
    row = lambda c: pl.BlockSpec((tm, c), lambda i: (done(i), 0))
    tab = pl.BlockSpec((tm, LANES), lambda i: (done(i) % tiles_per_seq, 0))
    if tt:
        k_spec = pl.BlockSpec((1, ATT_WIDTH, tm), lambda i: (done(i) // tiles_per_seq, 0, done(i) % tiles_per_seq))
        k_shape = jax.ShapeDtypeStruct((n // seq, ATT_WIDTH, seq), F32)
        v_spec = pl.BlockSpec((tm * ATT_HEADS, ATT_V_DIM), lambda i: (done(i), 0))
        v_shape = jax.ShapeDtypeStruct((n * ATT_HEADS, ATT_V_DIM), F32)
    else:
        k_spec, k_shape = row(ATT_WIDTH), jax.ShapeDtypeStruct((n, ATT_WIDTH), F32)
        v_spec, v_shape = row(ATT_WIDTH), jax.ShapeDtypeStruct((n, ATT_WIDTH), F32)
    out_specs = [row(SSD_WIDTH), row(SSD_XBC), pl.BlockSpec((spt, 8, SSD_XBC), lambda i: (done(i), 0, 0)),
                 row(DT_PAD), row(ATT_WIDTH), k_spec, v_spec]
    out_shape = [jax.ShapeDtypeStruct((n, SSD_WIDTH), BF16), jax.ShapeDtypeStruct((n, SSD_XBC), BF16),
                 jax.ShapeDtypeStruct((nt * spt, 8, SSD_XBC), F32), jax.ShapeDtypeStruct((n, DT_PAD), F32),
                 jax.ShapeDtypeStruct((n, ATT_WIDTH), BF16), k_shape, v_shape]
    if tt:
        tr = lambda r: pl.BlockSpec((tm // tt, r, tt), lambda i: (done(i), 0, 0))
        vt_rows = ATT_HEADS * ATT_VT_ROWS
        out_specs += [tr(ATT_WIDTH), row(ATT_WIDTH), tr(vt_rows)]
        out_shape += [jax.ShapeDtypeStruct((n // tt, ATT_WIDTH, tt), BF16), jax.ShapeDtypeStruct((n, ATT_WIDTH), BF16),
                      jax.ShapeDtypeStruct((n // tt, vt_rows, tt), BF16)]

    def body(*refs):
        refs, scratch = refs[:-2], refs[-2:]
        refs = refs + (None,) * (18 - len(refs))
        _in_proj_kernel(*refs, *scratch, tm=tm, tt=tt, spt=spt, tiles_per_seq=tiles_per_seq)

    outs = pl.pallas_call(
        body,
        grid=(nt + 1,),
        in_specs=[pl.BlockSpec((tm, D_MODEL), lambda i: (jnp.minimum(i, nt - 1), 0)),
                  _const_spec((1, D_MODEL)), _const_spec((D_MODEL, IN_COLS_PADDED)), tab, tab,
                  pl.BlockSpec((spt, 8, SSD_XBC), lambda i: (done(i) // tiles_per_seq, 0, 0)),
                  _const_spec((SSD_CONV, SSD_XBC)), _const_spec((1, SSD_XBC))],
        out_specs=out_specs,
        out_shape=out_shape,
        scratch_shapes=[pltpu.VMEM((tm, IN_COLS_PADDED), F32), pltpu.VMEM((8, SSD_XBC), F32)],
        compiler_params=pltpu.CompilerParams(dimension_semantics=("arbitrary",), vmem_limit_bytes=VMEM_LIMIT),
        name="in_proj",
    )(x, g, w, cos, sin, cbuf, cw, cb)
    return list(outs) + [None] * (10 - len(outs))


def _expand_heads(x, e2):
    hi = x.astype(BF16)
    lo = (x - hi.astype(F32)).astype(BF16)
    return _dot(jnp.concatenate([hi, lo], axis=1), e2)


def _ssd_kernel(gate_ref, u_ref, dt_ref, h0_ref, dtb_ref, alog_ref, dsk_ref, nw_ref, e2_ref,
                y_ref, hout_ref, state, *, q, valid):
    c = pl.program_id(1)

    @pl.when(c == 0)
    def _():
        state[...] = h0_ref[0]

    xs = u_ref[:, :SSD_WIDTH].astype(F32)
    bm = u_ref[:, SSD_WIDTH:SSD_WIDTH + SSD_GROUPS * SSD_STATE]
    cmb = u_ref[:, SSD_WIDTH + SSD_GROUPS * SSD_STATE:]

    dtr = dt_ref[...] + dtb_ref[...]
    dt = jnp.maximum(dtr, 0.0) + jnp.log(1.0 + jnp.exp(-jnp.abs(dtr)))
    if valid is not None:
        row = lax.broadcasted_iota(jnp.int32, (q, DT_PAD), 0) + c * q
        dt = jnp.where(row < valid, dt, 0.0)
    a = -jnp.exp(alog_ref[...])
    ad = dt * a
    ri = lax.broadcasted_iota(jnp.int32, (q, q), 0)
    ci = lax.broadcasted_iota(jnp.int32, (q, q), 1)
    tril = ri >= ci
    tril_b = jnp.where(tril, 1.0, 0.0).astype(BF16)
    ad_hi = ad.astype(BF16)
    ad_r = ad - ad_hi.astype(F32)
    ad_mid = ad_r.astype(BF16)
    ad_lo = (ad_r - ad_mid.astype(F32)).astype(BF16)
    acum = _dot(tril_b, ad_hi) + _dot(tril_b, ad_mid) + _dot(tril_b, ad_lo)
    acum_t = acum.T
    tot = acum[q - 1:q, :]
    e2 = e2_ref[...]
    expanded = _expand_heads(jnp.concatenate([dt, dt * jnp.exp(tot - acum), jnp.exp(acum)], axis=0), e2)
    dtx = expanded[0:q]
    ddx = expanded[q:2 * q]
    eax = expanded[2 * q:3 * q]
    dsk = _expand_heads(jnp.broadcast_to(dsk_ref[...], (8, DT_PAD)), e2)[0:1]

    xdt = (xs * dtx).astype(BF16)
    xdtd = (xs * ddx).astype(BF16)
    bm_t = bm.astype(F32).T.astype(BF16)
    gw = SSD_REP * SSD_HEAD_DIM
    stripe = lax.broadcasted_iota(jnp.int32, (q, gw), 1) // SSD_HEAD_DIM
    ys = []
    for g in range(SSD_GROUPS):
        cm_g = cmb[:, g * SSD_STATE:(g + 1) * SSD_STATE]
        bt_g = bm_t[g * SSD_STATE:(g + 1) * SSD_STATE, :]
        cbm = _dot(cm_g, bt_g)
        ms = []
        for r in range(SSD_REP):
            h = g * SSD_REP + r
            diff = acum[:, h:h + 1] - acum_t[h:h + 1, :]
            ms.append((cbm * jnp.exp(jnp.where(tril, diff, -jnp.inf))).astype(BF16))
        ydf = _dot(jnp.concatenate(ms, axis=0), xdt[:, g * gw:(g + 1) * gw])
        yd = ydf[0:q]
        for r in range(1, SSD_REP):
            yd = jnp.where(stripe == r, ydf[r * q:(r + 1) * q], yd)
        st = state[g]
        y_off = _dot(cm_g, st.astype(BF16)) * eax[:, g * gw:(g + 1) * gw]
        state[g] = st * eax[q - 1:q, g * gw:(g + 1) * gw] + _dot(bt_g, xdtd[:, g * gw:(g + 1) * gw])
        ys.append(yd + y_off)
    y = jnp.concatenate(ys, axis=1) + dsk * xs
    y = y * gate_ref[...].astype(F32)
    outs = []
    for g in range(SSD_WIDTH // SSD_NORM_GROUP):
        yg = y[:, g * SSD_NORM_GROUP:(g + 1) * SSD_NORM_GROUP]
        outs.append(yg * lax.rsqrt(jnp.mean(yg * yg, axis=-1, keepdims=True) + SSD_NORM_EPS))
    y_ref[...] = (jnp.concatenate(outs, axis=1) * nw_ref[...]).astype(BF16)

    @pl.when(c == pl.num_programs(1) - 1)
    def _():
        hout_ref[0] = state[...]


def _ssd(gate, u, dt, h0, dtb, alog, dsk, nw, e2, *, nb, seq, q, valid):
    nc = seq // q
    row = lambda c: pl.BlockSpec((q, c), lambda b, i: (b * nc + i, 0))
    gw = SSD_REP * SSD_HEAD_DIM
    st_spec = pl.BlockSpec((1, SSD_GROUPS, SSD_STATE, gw), lambda b, i: (b, 0, 0, 0))
    return pl.pallas_call(
        functools.partial(_ssd_kernel, q=q, valid=valid),
        grid=(nb, nc),
        in_specs=[row(SSD_WIDTH), row(SSD_XBC), row(DT_PAD), st_spec,
                  _const_spec((1, DT_PAD)), _const_spec((1, DT_PAD)), _const_spec((1, DT_PAD)),
                  _const_spec((1, SSD_WIDTH)), _const_spec((2 * DT_PAD, SSD_WIDTH))],
        out_specs=[row(SSD_WIDTH), st_spec],
        out_shape=[jax.ShapeDtypeStruct((nb * seq, SSD_WIDTH), BF16),
                   jax.ShapeDtypeStruct((nb, SSD_GROUPS, SSD_STATE, gw), F32)],
        scratch_shapes=[pltpu.VMEM((SSD_GROUPS, SSD_STATE, gw), F32)],
        compiler_params=pltpu.CompilerParams(dimension_semantics=("arbitrary", "arbitrary"),
                                             vmem_limit_bytes=VMEM_LIMIT),
        name="ssd",
    )(gate, u, dt, h0, dtb, alog, dsk, nw, e2)


def _column_max(x):
    while x.shape[0] > 8 and x.shape[0] % 16 == 0:
        half = x.shape[0] // 2
        x = jnp.maximum(x[:half], x[half:])
    return jnp.max(x, axis=0, keepdims=True)


def _attn_tile_counts(i, *, tq, tk, past, kv_len, minimum=min):
    q_lo = past + i * tq
    q_hi = q_lo + tq - 1
    lim_lo = minimum((q_lo // CHUNK + 1) * CHUNK, kv_len)
    lim_hi = minimum((q_hi // CHUNK + 1) * CHUNK, kv_len)
    return lim_lo // tk, (lim_hi + tk - 1) // tk


def _diff_attn_kernel(qt_ref, qt_next_ref, k_ref, vt_ref, lam_ref, sw_ref, o_ref,
                      q2t_ref, s_ref, p_ref, m_ref, alpha_ref, acc_ref, *, tq, tk, past, kv_len, lam_init, nbuf, nkt):
    i = pl.program_id(2)
    w = 2 * tq
    row = lax.broadcasted_iota(jnp.int32, (ATT_V_DIM, tq), 0)
    n_full, _ = _attn_tile_counts(i, tq=tq, tk=tk, past=past, kv_len=kv_len, minimum=jnp.minimum)
    n_visits = n_full + 1

    def scores(j):
        kt = k_ref[pl.ds(pl.multiple_of(j * tk, tk), tk), :]
        return _dot(kt, q2t_ref[...])

    def visited_tile(v):
        return jnp.where(v == 0, n_full, jnp.maximum(v - 1, 0))

    ncb = w // LANES
    pw = p_ref.shape[-1]

    def store_scores(buf, s):
        for cb in range(ncb):
            s_ref[buf, cb] = s[:, cb * LANES:(cb + 1) * LANES]

    def stage_a(v, buf):
        store_scores(buf, scores(jnp.minimum(v - 1, n_full)))

    rows = min(tk, ATT_SOFTMAX_ROWS)

    def stage_b(buf):
        for cb in range(ncb):
            sl = slice(cb * LANES, (cb + 1) * LANES)
            m_old = m_ref[:, sl]
            m_new = m_old
            for r0 in range(0, tk, rows):
                m_new = jnp.maximum(m_new, _column_max(s_ref[buf, cb, r0:r0 + rows, :]))
            m_ref[:, sl] = m_new
            alpha_ref[buf, :, sl] = jnp.exp2(m_old - m_new)
            pl0 = (cb * LANES) % pw
            for r0 in range(0, tk, rows):
                p_ref[buf, cb * LANES // pw, r0:r0 + rows, pl0:pl0 + LANES] = jnp.exp2(
                    (s_ref[buf, cb, r0:r0 + rows, :] - m_new).astype(BF16))

    def stage_c(v, buf):
        vt = vt_ref[visited_tile(v)]
        for pb in range(w // pw):
            sl = slice(pb * pw, (pb + 1) * pw)
            acc_ref[:, sl] = alpha_ref[buf, :, sl] * acc_ref[:, sl] + _dot(vt, p_ref[buf, pb])

    def first_visit(qt, tile):
        zero = jnp.zeros_like(qt)
        q2t_ref[:, 0:tq] = jnp.where(row < ATT_HEAD_DIM, qt, zero)
        q2t_ref[:, tq:w] = jnp.where(row >= ATT_HEAD_DIM, qt, zero)
        nf, _ = _attn_tile_counts(tile, tq=tq, tk=tk, past=past, kv_len=kv_len, minimum=jnp.minimum)
        nf = jnp.minimum(nf, nkt - 1)
        qchunk = (past + tile * tq + lax.broadcasted_iota(jnp.int32, (1, w), 1) % tq) // CHUNK
        s_part = scores(nf)
        s_masked = []
        for kb in range(tk // CHUNK):
            k0 = nf * tk + kb * CHUNK
            kchunk = jnp.where(k0 < kv_len, k0 // CHUNK, jnp.iinfo(jnp.int32).max)
            s_masked.append(jnp.where(kchunk <= qchunk, s_part[kb * CHUNK:(kb + 1) * CHUNK], NEG_BIG))
        store_scores(0, jnp.concatenate(s_masked, axis=0))

    @pl.when(i == 0)
    def _():
        first_visit(qt_ref[0], i)

    m_ref[...] = jnp.full_like(m_ref, NEG_BIG)
    acc_ref[...] = jnp.zeros_like(acc_ref)
    p_ref[nbuf - 1] = jnp.zeros(p_ref.shape[1:], BF16)
    alpha_ref[nbuf - 1] = jnp.ones(alpha_ref.shape[1:], F32)

    def visits(v0, count, prefetch_last):
        for r in range(count):
            prefetch = r + 1 < count or prefetch_last
            if prefetch and nbuf > 1:
                stage_a(v0 + r + 1, (r + 1) % nbuf)
            stage_c(v0 + r - 1, (r - 1) % nbuf)
            stage_b(r)
            if prefetch and nbuf == 1:
                stage_a(v0 + r + 1, (r + 1) % nbuf)

    def trip(u, carry):
        visits(nbuf * u, nbuf, True)
        return carry

    lax.fori_loop(0, n_visits // nbuf, trip, 0)
    for rem in range(nbuf):

        @pl.when(n_visits % nbuf == rem)
        def _():
            visits(n_visits - rem, rem, False)
            stage_c(n_visits - 1, (rem - 1) % nbuf)

    first_visit(qt_next_ref[0], i + 1)

    lam_v = lam_ref[...]
    lam = (jnp.exp(jnp.sum(lam_v[0:1] * lam_v[1:2], axis=-1, keepdims=True))
           - jnp.exp(jnp.sum(lam_v[2:3] * lam_v[3:4], axis=-1, keepdims=True)) + lam_init)
    acc = acc_ref[...]
    o = acc[:ATT_V_DIM] / acc[ATT_V_DIM:ATT_V_DIM + 1]
    o = o[:, :tq] - lam * o[:, tq:]
    o = o * lax.rsqrt(jnp.mean(o * o, axis=0, keepdims=True) + ATT_NORM_EPS) * (sw_ref[...] * (1.0 - lam_init))
    o_ref[...] = o.T.astype(BF16)


def _diff_attn(qt, k, vt, lam_vecs, subln_col, *, nb, tq, tk, past, kv_len, lam_init, nbuf):
    nq = qt.shape[0] // nb
    nkt = vt.shape[0] // nb
    assert tk % CHUNK == 0 and kv_len % CHUNK == 0
    for i in range(nq):
        n_full, n_end = _attn_tile_counts(i, tq=tq, tk=tk, past=past, kv_len=kv_len)
        assert n_end - n_full == 1 and n_end <= nkt, (i, n_full, n_end)
    w = 2 * tq
    return pl.pallas_call(
        functools.partial(_diff_attn_kernel, tq=tq, tk=tk, past=past, kv_len=kv_len, lam_init=lam_init, nbuf=nbuf,
                          nkt=nkt),
        grid=(nb, ATT_HEADS, nq),
        in_specs=[pl.BlockSpec((1, ATT_V_DIM, tq), lambda b, h, i: (b * nq + i, h, 0)),
                  pl.BlockSpec((1, ATT_V_DIM, tq), lambda b, h, i: (b * nq + jnp.minimum(i + 1, nq - 1), h, 0)),
                  pl.BlockSpec((nkt * tk, ATT_V_DIM), lambda b, h, i: (b, h)),
                  pl.BlockSpec((nkt, ATT_VT_ROWS, tk), lambda b, h, i: (b, h, 0)),
                  _const_spec((4, ATT_HEAD_DIM)), _const_spec((ATT_V_DIM, 1))],
        out_specs=pl.BlockSpec((tq, ATT_V_DIM), lambda b, h, i: (b * nq + i, h)),
        out_shape=jax.ShapeDtypeStruct((nb * nq * tq, ATT_WIDTH), BF16),
        scratch_shapes=[pltpu.VMEM((ATT_V_DIM, w), BF16), pltpu.VMEM((nbuf, w // LANES, tk, LANES), F32),
                        pltpu.VMEM((nbuf, w // MXU_TILE, tk, MXU_TILE), BF16), pltpu.VMEM((1, w), F32),
                        pltpu.VMEM((nbuf, 1, w), F32),
                        pltpu.VMEM((ATT_VT_ROWS, w), F32)],
        compiler_params=pltpu.CompilerParams(dimension_semantics=("arbitrary", "arbitrary", "arbitrary"),
                                             vmem_limit_bytes=VMEM_LIMIT),
        name="diff_attn",
    )(qt, qt, k, vt, lam_vecs, subln_col)


def _decode_attn_kernel(q_ref, kn_ref, vn_ref, kc_ref, vc_ref, lam_ref, sw_ref, o_ref, *, seq, past, lam_init):
    lam_v = lam_ref[...]
    lam = (jnp.exp(jnp.sum(lam_v[0:1] * lam_v[1:2], axis=-1, keepdims=True))
           - jnp.exp(jnp.sum(lam_v[2:3] * lam_v[3:4], axis=-1, keepdims=True)) + lam_init)
    npad = LANES
    lane = lax.broadcasted_iota(jnp.int32, (seq, LANES), 1)
    qchunk = (past + lax.broadcasted_iota(jnp.int32, (2 * seq, 1), 0) % seq) // CHUNK
    kpos_p = lax.broadcasted_iota(jnp.int32, (2 * seq, past), 1)
    kpos_n = lax.broadcasted_iota(jnp.int32, (2 * seq, npad), 1)
    vis_p = kpos_p // CHUNK <= qchunk
    vis_n = ((past + kpos_n) // CHUNK <= qchunk) & (kpos_n < seq)
    pad_rows = jnp.zeros((npad - seq, LANES), BF16)
    for h in range(ATT_HEADS):
        sl = slice(h * LANES, (h + 1) * LANES)
        qh = q_ref[:, sl]
        zero = jnp.zeros_like(qh)
        q2 = jnp.concatenate([jnp.where(lane < ATT_HEAD_DIM, qh, zero), jnp.where(lane >= ATT_HEAD_DIM, qh, zero)], axis=0)
        s_p = jnp.where(vis_p, _dot(q2, kc_ref[0, sl, :].astype(BF16)), NEG_BIG)
        kn = jnp.concatenate([kn_ref[:, sl].astype(BF16), pad_rows], axis=0)
        s_n = lax.dot_general(q2, kn, (((1,), (1,)), ((), ())), preferred_element_type=F32)
        s_n = jnp.where(vis_n, s_n, NEG_BIG)
        m = jnp.maximum(jnp.max(s_p, axis=-1, keepdims=True), jnp.max(s_n, axis=-1, keepdims=True))
        p_p = jnp.exp2(s_p - m)
        p_n = jnp.exp2(s_n - m)
        l = jnp.sum(p_p, axis=-1, keepdims=True) + jnp.sum(p_n, axis=-1, keepdims=True)
        vh = vc_ref[0, pl.ds(h, past, stride=ATT_HEADS), :].astype(BF16)
        vn = jnp.concatenate([vn_ref[:, sl].astype(BF16), pad_rows], axis=0)
        o = (_dot(p_p.astype(BF16), vh) + _dot(p_n.astype(BF16), vn)) / l
        o = o[:seq] - lam * o[seq:]
        o = o * lax.rsqrt(jnp.mean(o * o, axis=-1, keepdims=True) + ATT_NORM_EPS) * (sw_ref[...] * (1.0 - lam_init))
        o_ref[:, sl] = o.astype(BF16)


def _decode_attn(q, k_new, v_new, kt_cache, v_cache, lam_vecs, subln_row, *, nb, seq, past, lam_init):
    row = pl.BlockSpec((seq, ATT_WIDTH), lambda b: (b, 0))
    return pl.pallas_call(
        functools.partial(_decode_attn_kernel, seq=seq, past=past, lam_init=lam_init),
        grid=(nb,),
        in_specs=[row, row, row,
                  pl.BlockSpec((1, ATT_WIDTH, past), lambda b: (b, 0, 0)),
                  pl.BlockSpec((1, past * ATT_HEADS, ATT_V_DIM), lambda b: (b, 0, 0)),
                  _const_spec((4, ATT_HEAD_DIM)), _const_spec((1, ATT_V_DIM))],
        out_specs=row,
        out_shape=jax.ShapeDtypeStruct((nb * seq, ATT_WIDTH), BF16),
        compiler_params=pltpu.CompilerParams(dimension_semantics=("arbitrary",), vmem_limit_bytes=VMEM_LIMIT),
        name="decode_attn",
    )(q, k_new, v_new, kt_cache, v_cache, lam_vecs, subln_row)


def _post_mix_kernel(x_ref, ys_ref, ya_ref, mk_ref, mv_ref, wo_ref, wq_ref, wox_ref,
                     g1_ref, g2_ref, g3_ref, h_ref, *, spt, rows):
    mix = _dot(ys_ref[...], wo_ref[0:SSD_WIDTH, :]) + _dot(ya_ref[...], wo_ref[SSD_WIDTH:, :])
    h = x_ref[...] + _rms(mix, g1_ref[...], NORM_EPS)
    qn = _rms(h, g2_ref[...], NORM_EPS).astype(BF16)
    qx = (_dot(qn, wq_ref[...]) * (1.0 / math.sqrt(MEM_HEAD_DIM))).astype(BF16)
    ox_seqs = []
    for s_i in range(spt):
        qs = qx[s_i * rows:(s_i + 1) * rows]
        mk = mk_ref[s_i]
        mv = mv_ref[s_i]
        oxs = []
        for hd in range(MEM_HEADS):
            sl = slice(hd * MEM_HEAD_DIM, (hd + 1) * MEM_HEAD_DIM)
            s = lax.dot_general(qs[:, sl], mk[:, sl], (((1,), (1,)), ((), ())), preferred_element_type=F32)
            p = jnp.exp(s - jnp.max(s, axis=-1, keepdims=True))
            ox = _dot(p.astype(BF16), mv[:, sl]) / jnp.sum(p, axis=-1, keepdims=True)
            oxs.append(ox.astype(BF16))
        ox_seqs.append(jnp.concatenate(oxs, axis=1))
    ox_all = ox_seqs[0] if spt == 1 else jnp.concatenate(ox_seqs, axis=0)
    o2 = _dot(ox_all, wox_ref[...])
    h_ref[...] = h + _rms(o2, g3_ref[...], NORM_EPS)


def _post_mix(x, ys, ya, mk, mv, w_out, wq, wox, g1, g2, g3, *, seq, tm):
    n = x.shape[0]
    spt = max(1, tm // seq)
    tiles_per_seq = max(1, seq // tm)
    row = lambda c: pl.BlockSpec((tm, c), lambda i: (i, 0))
    mem = pl.BlockSpec((spt, MEM_LEN, D_MODEL), lambda i: (i // tiles_per_seq, 0, 0))
    wspec = _const_spec((D_MODEL, D_MODEL))
    gspec = _const_spec((1, D_MODEL))
    return pl.pallas_call(
        functools.partial(_post_mix_kernel, spt=spt, rows=tm // spt),
        grid=(n // tm,),
        in_specs=[row(D_MODEL), row(SSD_WIDTH), row(ATT_WIDTH), mem, mem, wspec, wspec, wspec, gspec, gspec, gspec],
        out_specs=row(D_MODEL),
        out_shape=jax.ShapeDtypeStruct((n, D_MODEL), F32),
        compiler_params=pltpu.CompilerParams(dimension_semantics=("arbitrary",), vmem_limit_bytes=VMEM_LIMIT),
        name="post_mix",
    )(x, ys, ya, mk, mv, w_out, wq, wox, g1, g2, g3)


def _ffn_kernel(h_ref, wg_ref, wu_ref, wd_ref, g1_ref, g2_ref, o_ref):
    h = h_ref[...]
    hn = _rms(h, g1_ref[...], NORM_EPS).astype(BF16)
    act = (_silu(_dot(hn, wg_ref[...])) * _dot(hn, wu_ref[...])).astype(BF16)
    f = _dot(act, wd_ref[...])
    o_ref[...] = h + _rms(f, g2_ref[...], NORM_EPS)


def _ffn(h, wg, wu, wd, g1, g2, *, tm):
    n = h.shape[0]
    row = pl.BlockSpec((tm, D_MODEL), lambda i: (i, 0))
    return pl.pallas_call(
        _ffn_kernel,
        grid=(n // tm,),
        in_specs=[row, _const_spec((D_MODEL, FFN_HIDDEN)), _const_spec((D_MODEL, FFN_HIDDEN)),
                  _const_spec((FFN_HIDDEN, D_MODEL)), _const_spec((1, D_MODEL)), _const_spec((1, D_MODEL))],
        out_specs=row,
        out_shape=jax.ShapeDtypeStruct((n, D_MODEL), F32),
        compiler_params=pltpu.CompilerParams(dimension_semantics=("arbitrary",), vmem_limit_bytes=VMEM_LIMIT),
        name="ffn",
    )(h, wg, wu, wd, g1, g2)


def _rope_tables(past, seq):
    half = ATT_HEAD_DIM // 2
    inv = jnp.power(ROPE_THETA, -jnp.arange(0, ATT_HEAD_DIM, 2, dtype=F32) / ATT_HEAD_DIM)
    pos = (past + jnp.arange(seq, dtype=jnp.int32)).astype(F32)
    ang = pos[:, None] * inv[None, :]
    cos, sin = jnp.cos(ang), jnp.sin(ang)
    reps = LANES // ATT_HEAD_DIM
    assert half * 2 == ATT_HEAD_DIM
    return jnp.tile(jnp.concatenate([cos, cos], axis=-1), (1, reps)), jnp.tile(jnp.concatenate([-sin, sin], axis=-1), (1, reps))


def _state_to_kernel_layout(s):
    b = s.shape[0]
    s = s.reshape(b, SSD_GROUPS, SSD_REP, SSD_HEAD_DIM, SSD_STATE)
    return s.transpose(0, 1, 4, 2, 3).reshape(b, SSD_GROUPS, SSD_STATE, SSD_REP * SSD_HEAD_DIM)


def _state_from_kernel_layout(s):
    b = s.shape[0]
    s = s.reshape(b, SSD_GROUPS, SSD_STATE, SSD_REP, SSD_HEAD_DIM)
    return s.transpose(0, 1, 3, 4, 2).reshape(b, SSD_HEADS, SSD_HEAD_DIM, SSD_STATE)


def _layer(x, conv_buf, ssm0, kt_past, v_past, mem_kb, mem_vb, lam_init, p, *, tm, tk, tq, ssd_q):
    nb, seq, _ = x.shape
    n = nb * seq
    past = 0 if kt_past is None else kt_past.shape[2]
    xf = x.reshape(n, D_MODEL)
    cos, sin = _rope_tables(past, seq)
    no_history = kt_past is None
    assert tq == tk or not no_history
    cbuf = jnp.pad(conv_buf.astype(F32), ((0, 0), (8 - (SSD_CONV - 1), 0), (0, 0)))
    gate, u, tail, dt, q, k, v, qt, kb, vt = _in_proj(xf, p["g_pre_mix"], p["w_in"], cos, sin, cbuf, p["conv_w"],
                                                       p["conv_b"], seq=seq, tm=tm, tt=tk if no_history else 0)

    seq_pad = -(-seq // ssd_q) * ssd_q
    if seq_pad != seq:
        pad = lambda a: jnp.pad(a.reshape(nb, seq, -1), ((0, 0), (0, seq_pad - seq), (0, 0))).reshape(nb * seq_pad, -1)
        gate_s, u_s, dt_s = pad(gate), pad(u), pad(dt)
    else:
        gate_s, u_s, dt_s = gate, u, dt
    y_ssd, h_new = _ssd(gate_s, u_s, dt_s, _state_to_kernel_layout(ssm0.astype(F32)),
                        p["dt_bias"], p["a_log"], p["d_skip"], p["ssm_norm_w"], p["e2"],
                        nb=nb, seq=seq_pad, q=ssd_q, valid=None if seq_pad == seq else seq)
    if seq_pad != seq:
        y_ssd = y_ssd.reshape(nb, seq_pad, SSD_WIDTH)[:, :seq].reshape(n, SSD_WIDTH)
    ssm_new = _state_from_kernel_layout(h_new)
    ext_tail = tail.reshape(nb, -1, 8, SSD_XBC)[:, -1]
    if seq >= SSD_CONV - 1:
        conv_new = ext_tail[:, 8 - (SSD_CONV - 1):]
    else:
        conv_new = jnp.concatenate([conv_buf.astype(F32), ext_tail[:, 8 - seq:]], axis=1)[:, -(SSD_CONV - 1):]

    if no_history:
        kv_len = seq
        y_att = _diff_attn(qt, kb, vt, p["lam_vecs"], p["subln_col"], nb=nb, tq=tq, tk=tk,
                           past=past, kv_len=kv_len, lam_init=lam_init, nbuf=ATT_PIPELINE_BUFFERS)
    else:
        y_att = _decode_attn(q, k, v, kt_past, v_past, p["lam_vecs"], p["subln_col"].reshape(1, ATT_V_DIM),
                             nb=nb, seq=seq, past=past, lam_init=lam_init)

    h = _post_mix(xf, y_ssd, y_att, mem_kb, mem_vb, p["w_out"], p["wq_x"], p["wo_x"],
                  p["g_post_mix"], p["g_pre_x"], p["g_post_x"], seq=seq, tm=tm)
    out = _ffn(h, p["w_gate"], p["w_up"], p["w_down"], p["g_pre_ffn"], p["g_post_ffn"], tm=tm)
    if no_history:
        k_out = k.reshape(nb, ATT_HEADS, 2, ATT_HEAD_DIM, seq).transpose(0, 4, 1, 2, 3)
    else:
        k_out = k.reshape(nb, seq, ATT_HEADS, 2, ATT_HEAD_DIM)
    return (out.reshape(nb, seq, D_MODEL), k_out,
            v.reshape(nb, seq, ATT_HEADS, ATT_V_DIM), ssm_new, conv_new)


def _prep_params(i, w_in, conv_w, conv_b, dt_bias, a_log, d_skip, ssm_norm_w, lam_q1, lam_k1, lam_q2, lam_k2, subln_w,
                 w_out, wq_x, wo_x, g_pre_mix, g_post_mix, g_pre_x, g_post_x, g_pre_ffn, g_post_ffn,
                 w_gate, w_up, w_down):
    w = w_in[i]
    s0 = SSD_WIDTH + SSD_XBC
    s1 = s0 + SSD_HEADS
    w_r = jnp.concatenate([w[:, :s0], w[:, s1:], w[:, s0:s1], jnp.zeros((D_MODEL, DT_PAD - SSD_HEADS), w.dtype)], axis=1)
    head_pad = lambda a: jnp.pad(a[i].astype(F32), (0, DT_PAD - SSD_HEADS)).reshape(1, DT_PAD)
    e = (jnp.arange(DT_PAD)[:, None] == (jnp.arange(SSD_WIDTH)[None, :] // SSD_HEAD_DIM)).astype(BF16)
    row = lambda a: a[i].astype(F32).reshape(1, -1)
    return {
        "w_in": w_r.astype(BF16), "conv_w": conv_w[i].astype(F32), "conv_b": row(conv_b),
        "dt_bias": head_pad(dt_bias), "a_log": head_pad(a_log), "d_skip": head_pad(d_skip),
        "ssm_norm_w": row(ssm_norm_w), "e2": jnp.concatenate([e, e], axis=0),
        "lam_vecs": jnp.stack([lam_q1[i], lam_k1[i], lam_q2[i], lam_k2[i]]).astype(F32), "subln_col": subln_w[i].astype(F32).reshape(ATT_V_DIM, 1),
        "w_out": w_out[i].astype(BF16), "wq_x": wq_x[i].astype(BF16), "wo_x": wo_x[i].astype(BF16),
        "g_pre_mix": row(g_pre_mix), "g_post_mix": row(g_post_mix), "g_pre_x": row(g_pre_x), "g_post_x": row(g_post_x),
        "g_pre_ffn": row(g_pre_ffn), "g_post_ffn": row(g_post_ffn),
        "w_gate": w_gate[i].astype(BF16), "w_up": w_up[i].astype(BF16), "w_down": w_down[i].astype(BF16),
    }


def kernel(x_prompt, x_sample, cache_attn_k, cache_attn_v, state_ssm, state_conv, cache_mem_k, cache_mem_v, mem_prompt, w_in, conv_w, conv_b, dt_bias, a_log, d_skip, ssm_norm_w, lam_q1, lam_k1, lam_q2, lam_k2, subln_w, w_out, g_mem, wq_x, wk_x, wv_x, wo_x, g_pre_mix, g_post_mix, g_pre_x, g_post_x, g_pre_ffn, g_post_ffn, w_gate, w_up, w_down):
    depth = w_in.shape[0]
    bp, sp, _ = x_prompt.shape
    bs, ss, _ = x_sample.shape
    hp, hs = x_prompt, x_sample
    outs = [[] for _ in range(10)]
    for i in range(depth):
        lam_init = 0.8 - 0.6 * math.exp(-0.3 * i)
        p = _prep_params(i, w_in, conv_w, conv_b, dt_bias, a_log, d_skip, ssm_norm_w, lam_q1, lam_k1, lam_q2, lam_k2,
                         subln_w, w_out, wq_x, wo_x, g_pre_mix, g_post_mix, g_pre_x, g_post_x, g_pre_ffn, g_post_ffn,
                         w_gate, w_up, w_down)
        mk, mv, mkb, mvb = _mem_kv(mem_prompt.reshape(bp * MEM_LEN, D_MODEL), g_mem[i].reshape(1, D_MODEL),
                                   wk_x[i].astype(BF16), wv_x[i].astype(BF16))
        hp, k_new, v_new, ssm_new, conv_new = _layer(
            hp, jnp.zeros((bp, SSD_CONV - 1, SSD_XBC), F32), jnp.zeros((bp, SSD_HEADS, SSD_HEAD_DIM, SSD_STATE), F32),
            None, None, mkb.reshape(bp, MEM_LEN, D_MODEL), mvb.reshape(bp, MEM_LEN, D_MODEL), lam_init, p,
            tm=512, tk=512, tq=512, ssd_q=256)
        for lst, val in zip(outs[:6], (k_new, v_new, ssm_new, conv_new,
                                       mk.reshape(bp, MEM_LEN, MEM_HEADS, MEM_HEAD_DIM),
                                       mv.reshape(bp, MEM_LEN, MEM_HEADS, MEM_HEAD_DIM))):
            lst.append(val)
        past = cache_attn_k.shape[2]
        hs, k_new, v_new, ssm_new, conv_new = _layer(
            hs, state_conv[i], state_ssm[i],
            cache_attn_k[i].transpose(0, 2, 3, 4, 1).reshape(bs, ATT_WIDTH, past),
            cache_attn_v[i].reshape(bs, past * ATT_HEADS, ATT_V_DIM),
            cache_mem_k[i].reshape(bs, MEM_LEN, D_MODEL).astype(BF16),
            cache_mem_v[i].reshape(bs, MEM_LEN, D_MODEL).astype(BF16), lam_init, p,
            tm=bs * ss, tk=0, tq=0, ssd_q=128)
        for lst, val in zip(outs[6:], (k_new, v_new, ssm_new, conv_new)):
            lst.append(val)
    return (hp, hs) + tuple(jnp.stack(o) for o in outs)
```

```python
import functools
import math

import jax
import jax.numpy as jnp
from jax import lax
from jax.experimental import pallas as pl
from jax.experimental.pallas import tpu as pltpu

D_MODEL = 1024
CHUNK = 64
SSD_WIDTH = 512
SSD_HEAD_DIM = 64
SSD_HEADS = 8
SSD_GROUPS = 2
SSD_REP = 4
SSD_STATE = 128
SSD_CONV = 4
SSD_XBC = 1024
SSD_NORM_GROUP = 256
SSD_NORM_EPS = 1e-5
ATT_WIDTH = 512
ATT_HEAD_DIM = 64
ATT_HEADS = 4
ATT_V_DIM = 128
ATT_NORM_EPS = 1e-5
ROPE_THETA = 10000.0
MEM_LEN = 256
MEM_HEADS = 4
MEM_HEAD_DIM = 256
FFN_HIDDEN = 2816
NORM_EPS = 1e-6
LANES = 128
DT_PAD = LANES
IN_COLS_PADDED = SSD_WIDTH + SSD_XBC + 3 * ATT_WIDTH + DT_PAD
VMEM_LIMIT = 56 * 1024 * 1024
NEG_BIG = -1e30
MXU_TILE = 256
BF16_SUBLANES = 16
ATT_VT_ROWS = ATT_V_DIM + BF16_SUBLANES
ATT_SOFTMAX_ROWS = 128
ATT_PIPELINE_BUFFERS = 2
Q_SCALE = math.log2(math.e) / math.sqrt(ATT_HEAD_DIM)

F32 = jnp.float32
BF16 = jnp.bfloat16


def _const_spec(shape):
    return pl.BlockSpec(shape, lambda *_: (0,) * len(shape), pipeline_mode=pl.Buffered(1))


def _rms(x, g, eps):
    return x * lax.rsqrt(jnp.mean(x * x, axis=-1, keepdims=True) + eps) * g


def _silu(x):
    h = 0.5 * x
    return h + h * jnp.tanh(h)


def _dot(a, b):
    return jnp.dot(a, b, preferred_element_type=F32)


def _cast_kernel(*refs):
    n = len(refs) // 2
    for x_ref, o_ref in zip(refs[:n], refs[n:]):
        o_ref[...] = x_ref[...].astype(BF16)


def _cast_bf16(arrays, steps):
    specs = [pl.BlockSpec((a.shape[0] // steps, a.shape[1]), lambda i: (i, 0)) for a in arrays]
    assert all(a.shape[0] % (steps * BF16_SUBLANES) == 0 for a in arrays)
    return pl.pallas_call(
        _cast_kernel,
        grid=(steps,),
        in_specs=specs,
        out_specs=specs,
        out_shape=[jax.ShapeDtypeStruct(a.shape, BF16) for a in arrays],
        compiler_params=pltpu.CompilerParams(dimension_semantics=("arbitrary",), vmem_limit_bytes=VMEM_LIMIT),
        name="cast_bf16",
    )(*arrays)


def _mem_kv_kernel(mem_ref, g_ref, wk_ref, wv_ref, mk_ref, mv_ref, mkb_ref, mvb_ref):
    mn = _rms(mem_ref[...], g_ref[...], NORM_EPS).astype(BF16)
    mk = _dot(mn, wk_ref[...].astype(BF16))
    mv = _dot(mn, wv_ref[...].astype(BF16))
    mk_ref[...] = mk
    mv_ref[...] = mv
    mkb_ref[...] = mk.astype(BF16)
    mvb_ref[...] = mv.astype(BF16)


def _mem_kv(mem, g_mem, wk, wv):
    n = mem.shape[0]
    tm = MEM_LEN
    row = pl.BlockSpec((tm, D_MODEL), lambda i: (i, 0))
    return pl.pallas_call(
        _mem_kv_kernel,
        grid=(n // tm,),
        in_specs=[row, _const_spec((1, D_MODEL)), _const_spec((D_MODEL, D_MODEL)), _const_spec((D_MODEL, D_MODEL))],
        out_specs=[row, row, row, row],
        out_shape=[jax.ShapeDtypeStruct((n, D_MODEL), F32), jax.ShapeDtypeStruct((n, D_MODEL), F32),
                   jax.ShapeDtypeStruct((n, D_MODEL), BF16), jax.ShapeDtypeStruct((n, D_MODEL), BF16)],
        compiler_params=pltpu.CompilerParams(dimension_semantics=("arbitrary",), vmem_limit_bytes=VMEM_LIMIT),
        name="mem_kv",
    )(mem, g_mem, wk, wv)


def _in_proj_kernel(x_ref, g_ref, w_ref, cos_ref, sin_ref, cbuf_ref, cw_ref, cb_ref,
                    z_ref, u_ref, tail_ref, dt_ref, q_ref, k_ref, v_ref, qt_ref, kb_ref, vt_ref, acc_ref, xtail,
                    *, tm, tt, spt, tiles_per_seq):
    step = pl.program_id(0)
    tile = step - 1

    @pl.when(step == 0)
    def _():
        acc_ref[...] = jnp.zeros_like(acc_ref)
        xtail[...] = jnp.zeros_like(xtail)

    c_xbc, c_q = SSD_WIDTH, SSD_WIDTH + SSD_XBC
    c_k, c_v, c_dt = c_q + ATT_WIDTH, c_q + 2 * ATT_WIDTH, c_q + 3 * ATT_WIDTH
    z_ref[...] = _silu(acc_ref[:, :SSD_WIDTH]).astype(BF16)
    rows = tm // spt
    row8 = lax.broadcasted_iota(jnp.int32, (8, LANES), 0)
    first_tile_of_seq = tile % tiles_per_seq == 0
    for s_i in range(spt):
        r0 = s_i * rows
        tail_ref[s_i] = acc_ref[r0 + rows - 8:r0 + rows, c_xbc:c_q]
        for cb in range(SSD_XBC // LANES):
            sl = slice(cb * LANES, (cb + 1) * LANES)
            piece = acc_ref[r0:r0 + rows, c_xbc + cb * LANES:c_xbc + (cb + 1) * LANES]
            if spt > 1:
                prev = cbuf_ref[s_i, :, sl]
            else:
                prev = jnp.where(first_tile_of_seq, cbuf_ref[0, :, sl], xtail[:, sl])
            conv_top = cb_ref[:, sl] + piece[0:8] * cw_ref[SSD_CONV - 1:SSD_CONV, sl]
            conv_rest = cb_ref[:, sl] + piece[8:] * cw_ref[SSD_CONV - 1:SSD_CONV, sl]
            for d in range(1, SSD_CONV):
                sh = pltpu.roll(piece, d, 0)
                tap = cw_ref[SSD_CONV - 1 - d:SSD_CONV - d, sl]
                conv_top = conv_top + jnp.where(row8 < d, pltpu.roll(prev, d, 0), sh[0:8]) * tap
                conv_rest = conv_rest + sh[8:] * tap
            u_ref[r0:r0 + rows, sl] = _silu(jnp.concatenate([conv_top, conv_rest], axis=0)).astype(BF16)
    if spt == 1:
        xtail[...] = acc_ref[tm - 8:, c_xbc:c_q]
    dt_ref[...] = acc_ref[:, c_dt:c_dt + DT_PAD]
    if tt:
        for j in range(ATT_HEADS):
            v_ref[pl.ds(j, tm, stride=ATT_HEADS), :] = acc_ref[:, c_v + j * ATT_V_DIM:c_v + (j + 1) * ATT_V_DIM]
    else:
        v_ref[...] = acc_ref[:, c_v:c_v + ATT_WIDTH]

    cos = cos_ref[...]
    sin = sin_ref[...]
    first_half = (lax.broadcasted_iota(jnp.int32, (tm, LANES), 1) % ATT_HEAD_DIM) < (ATT_HEAD_DIM // 2)

    def rope(t):
        swapped = jnp.where(first_half, pltpu.roll(t, LANES - ATT_HEAD_DIM // 2, 1),
                            pltpu.roll(t, ATT_HEAD_DIM // 2, 1))
        return t * cos + swapped * sin

    for j in range(ATT_WIDTH // LANES):
        sl = slice(j * LANES, (j + 1) * LANES)
        qr = rope(acc_ref[:, c_q + j * LANES:c_q + (j + 1) * LANES]) * Q_SCALE
        kr = rope(acc_ref[:, c_k + j * LANES:c_k + (j + 1) * LANES])
        q_ref[:, sl] = qr.astype(BF16)
        if tt:
            k_ref[0, sl, :] = kr.T
            kb_ref[:, sl] = kr.astype(BF16)
            for c in range(tm // tt):
                qt_ref[c, sl, :] = qr[c * tt:(c + 1) * tt, :].T.astype(BF16)
                vt_ref[c, j * ATT_VT_ROWS:j * ATT_VT_ROWS + LANES, :] = acc_ref[
                    c * tt:(c + 1) * tt, c_v + j * LANES:c_v + (j + 1) * LANES].T.astype(BF16)
                vt_ref[c, j * ATT_VT_ROWS + LANES:(j + 1) * ATT_VT_ROWS, :] = jnp.ones((ATT_VT_ROWS - LANES, tt), BF16)
        else:
            k_ref[:, sl] = kr

    hn = _rms(x_ref[...], g_ref[...], NORM_EPS).astype(BF16)
    acc_ref[...] = _dot(hn, w_ref[...])


def _in_proj(x, g, w, cos, sin, cbuf, cw, cb, *, seq, tm, tt):
    n = x.shape[0]
    nt = n // tm
    spt = max(1, tm // seq)
    tiles_per_seq = max(1, seq // tm)
    if spt > 1:
        cos, sin = jnp.tile(cos, (spt, 1)), jnp.tile(sin, (spt, 1))
    done = lambda i: jnp.maximum(i - 1, 0)
    row = lambda c: pl.BlockSpec((tm, c), lambda i: (done(i), 0))
    tab = pl.BlockSpec((tm, LANES), lambda i: (done(i) % tiles_per_seq, 0))
    if tt:
        k_spec = pl.BlockSpec((1, ATT_WIDTH, tm), lambda i: (done(i) // tiles_per_seq, 0, done(i) % tiles_per_seq))
        k_shape = jax.ShapeDtypeStruct((n // seq, ATT_WIDTH, seq), F32)
        v_spec = pl.BlockSpec((tm * ATT_HEADS, ATT_V_DIM), lambda i: (done(i), 0))
        v_shape = jax.ShapeDtypeStruct((n * ATT_HEADS, ATT_V_DIM), F32)
    else:
        k_spec, k_shape = row(ATT_WIDTH), jax.ShapeDtypeStruct((n, ATT_WIDTH), F32)
        v_spec, v_shape = row(ATT_WIDTH), jax.ShapeDtypeStruct((n, ATT_WIDTH), F32)
    out_specs = [row(SSD_WIDTH), row(SSD_XBC), pl.BlockSpec((spt, 8, SSD_XBC), lambda i: (done(i), 0, 0)),
                 row(DT_PAD), row(ATT_WIDTH), k_spec, v_spec]
    out_shape = [jax.ShapeDtypeStruct((n, SSD_WIDTH), BF16), jax.ShapeDtypeStruct((n, SSD_XBC), BF16),
                 jax.ShapeDtypeStruct((nt * spt, 8, SSD_XBC), F32), jax.ShapeDtypeStruct((n, DT_PAD), F32),
                 jax.ShapeDtypeStruct((n, ATT_WIDTH), BF16), k_shape, v_shape]
    if tt:
        tr = lambda r: pl.BlockSpec((tm // tt, r, tt), lambda i: (done(i), 0, 0))
        vt_rows = ATT_HEADS * ATT_VT_ROWS
        out_specs += [tr(ATT_WIDTH), row(ATT_WIDTH), tr(vt_rows)]
        out_shape += [jax.ShapeDtypeStruct((n // tt, ATT_WIDTH, tt), BF16), jax.ShapeDtypeStruct((n, ATT_WIDTH), BF16),
                      jax.ShapeDtypeStruct((n // tt, vt_rows, tt), BF16)]

    def body(*refs):
        refs, scratch = refs[:-2], refs[-2:]
        refs = refs + (None,) * (18 - len(refs))
        _in_proj_kernel(*refs, *scratch, tm=tm, tt=tt, spt=spt, tiles_per_seq=tiles_per_seq)

    outs = pl.pallas_call(
        body,
        grid=(nt + 1,),
        in_specs=[pl.BlockSpec((tm, D_MODEL), lambda i: (jnp.minimum(i, nt - 1), 0)),
                  _const_spec((1, D_MODEL)), _const_spec((D_MODEL, IN_COLS_PADDED)), tab, tab,
                  pl.BlockSpec((spt, 8, SSD_XBC), lambda i: (done(i) // tiles_per_seq, 0, 0)),
                  _const_spec((SSD_CONV, SSD_XBC)), _const_spec((1, SSD_XBC))],
        out_specs=out_specs,
        out_shape=out_shape,
        scratch_shapes=[pltpu.VMEM((tm, IN_COLS_PADDED), F32), pltpu.VMEM((8, SSD_XBC), F32)],
        compiler_params=pltpu.CompilerParams(dimension_semantics=("arbitrary",), vmem_limit_bytes=VMEM_LIMIT),
        name="in_proj",
    )(x, g, w, cos, sin, cbuf, cw, cb)
    return list(outs) + [None] * (10 - len(outs))


def _expand_heads(x, e2):
    hi = x.astype(BF16)
    lo = (x - hi.astype(F32)).astype(BF16)
    return _dot(jnp.concatenate([hi, lo], axis=1), e2)


def _ssd_kernel(gate_ref, u_ref, dt_ref, h0_ref, dtb_ref, alog_ref, dsk_ref, nw_ref, e2_ref,
                y_ref, hout_ref, state, *, q, valid):
    c = pl.program_id(1)

    @pl.when(c == 0)
    def _():
        state[...] = h0_ref[0]

    xs = u_ref[:, :SSD_WIDTH].astype(F32)
    bm = u_ref[:, SSD_WIDTH:SSD_WIDTH + SSD_GROUPS * SSD_STATE]
    cmb = u_ref[:, SSD_WIDTH + SSD_GROUPS * SSD_STATE:]

    dtr = dt_ref[...] + dtb_ref[...]
    dt = jnp.maximum(dtr, 0.0) + jnp.log(1.0 + jnp.exp(-jnp.abs(dtr)))
    if valid is not None:
        row = lax.broadcasted_iota(jnp.int32, (q, DT_PAD), 0) + c * q
        dt = jnp.where(row < valid, dt, 0.0)
    a = -jnp.exp(alog_ref[...])
    ad = dt * a
    ri = lax.broadcasted_iota(jnp.int32, (q, q), 0)
    ci = lax.broadcasted_iota(jnp.int32, (q, q), 1)
    tril = ri >= ci
    tril_b = jnp.where(tril, 1.0, 0.0).astype(BF16)
    ad_hi = ad.astype(BF16)
    ad_r = ad - ad_hi.astype(F32)
    ad_mid = ad_r.astype(BF16)
    ad_lo = (ad_r - ad_mid.astype(F32)).astype(BF16)
    acum = _dot(tril_b, ad_hi) + _dot(tril_b, ad_mid) + _dot(tril_b, ad_lo)
    acum_t = acum.T
    tot = acum[q - 1:q, :]
    e2 = e2_ref[...]
    expanded = _expand_heads(jnp.concatenate([dt, dt * jnp.exp(tot - acum), jnp.exp(acum)], axis=0), e2)
    dtx = expanded[0:q]
    ddx = expanded[q:2 * q]
    eax = expanded[2 * q:3 * q]
    dsk = _expand_heads(jnp.broadcast_to(dsk_ref[...], (8, DT_PAD)), e2)[0:1]

    xdt = (xs * dtx).astype(BF16)
    xdtd = (xs * ddx).astype(BF16)
    bm_t = bm.astype(F32).T.astype(BF16)
    gw = SSD_REP * SSD_HEAD_DIM
    stripe = lax.broadcasted_iota(jnp.int32, (q, gw), 1) // SSD_HEAD_DIM
    ys = []
    for g in range(SSD_GROUPS):
        cm_g = cmb[:, g * SSD_STATE:(g + 1) * SSD_STATE]
        bt_g = bm_t[g * SSD_STATE:(g + 1) * SSD_STATE, :]
        cbm = _dot(cm_g, bt_g)
        ms = []
        for r in range(SSD_REP):
            h = g * SSD_REP + r
            diff = acum[:, h:h + 1] - acum_t[h:h + 1, :]
            ms.append((cbm * jnp.exp(jnp.where(tril, diff, -jnp.inf))).astype(BF16))
        ydf = _dot(jnp.concatenate(ms, axis=0), xdt[:, g * gw:(g + 1) * gw])
        yd = ydf[0:q]
        for r in range(1, SSD_REP):
            yd = jnp.where(stripe == r, ydf[r * q:(r + 1) * q], yd)
        st = state[g]
        y_off = _dot(cm_g, st.astype(BF16)) * eax[:, g * gw:(g + 1) * gw]
        state[g] = st * eax[q - 1:q, g * gw:(g + 1) * gw] + _dot(bt_g, xdtd[:, g * gw:(g + 1) * gw])
        ys.append(yd + y_off)
    y = jnp.concatenate(ys, axis=1) + dsk * xs
    y = y * gate_ref[...].astype(F32)
    outs = []
    for g in range(SSD_WIDTH // SSD_NORM_GROUP):
        yg = y[:, g * SSD_NORM_GROUP:(g + 1) * SSD_NORM_GROUP]
        outs.append(yg * lax.rsqrt(jnp.mean(yg * yg, axis=-1, keepdims=True) + SSD_NORM_EPS))
    y_ref[...] = (jnp.concatenate(outs, axis=1) * nw_ref[...]).astype(BF16)

    @pl.when(c == pl.num_programs(1) - 1)
    def _():
        hout_ref[0] = state[...]


def _ssd(gate, u, dt, h0, dtb, alog, dsk, nw, e2, *, nb, seq, q, valid):
    nc = seq // q
    row = lambda c: pl.BlockSpec((q, c), lambda b, i: (b * nc + i, 0))
    gw = SSD_REP * SSD_HEAD_DIM
    st_spec = pl.BlockSpec((1, SSD_GROUPS, SSD_STATE, gw), lambda b, i: (b, 0, 0, 0))
    return pl.pallas_call(
        functools.partial(_ssd_kernel, q=q, valid=valid),
        grid=(nb, nc),
        in_specs=[row(SSD_WIDTH), row(SSD_XBC), row(DT_PAD), st_spec,
                  _const_spec((1, DT_PAD)), _const_spec((1, DT_PAD)), _const_spec((1, DT_PAD)),
                  _const_spec((1, SSD_WIDTH)), _const_spec((2 * DT_PAD, SSD_WIDTH))],
        out_specs=[row(SSD_WIDTH), st_spec],
        out_shape=[jax.ShapeDtypeStruct((nb * seq, SSD_WIDTH), BF16),
                   jax.ShapeDtypeStruct((nb, SSD_GROUPS, SSD_STATE, gw), F32)],
        scratch_shapes=[pltpu.VMEM((SSD_GROUPS, SSD_STATE, gw), F32)],
        compiler_params=pltpu.CompilerParams(dimension_semantics=("arbitrary", "arbitrary"),
                                             vmem_limit_bytes=VMEM_LIMIT),
        name="ssd",
    )(gate, u, dt, h0, dtb, alog, dsk, nw, e2)


def _column_max(x):
    while x.shape[0] > 8 and x.shape[0] % 16 == 0:
        half = x.shape[0] // 2
        x = jnp.maximum(x[:half], x[half:])
    return jnp.max(x, axis=0, keepdims=True)


def _attn_tile_counts(i, *, tq, tk, past, kv_len, minimum=min):
    q_lo = past + i * tq
    q_hi = q_lo + tq - 1
    lim_lo = minimum((q_lo // CHUNK + 1) * CHUNK, kv_len)
    lim_hi = minimum((q_hi // CHUNK + 1) * CHUNK, kv_len)
    return lim_lo // tk, (lim_hi + tk - 1) // tk


def _diff_attn_kernel(qt_ref, qt_next_ref, k_ref, vt_ref, lam_ref, sw_ref, o_ref,
                      q2t_ref, s_ref, p_ref, m_ref, alpha_ref, acc_ref, *, tq, tk, past, kv_len, lam_init, nbuf, nkt):
    i = pl.program_id(2)
    w = 2 * tq
    row = lax.broadcasted_iota(jnp.int32, (ATT_V_DIM, tq), 0)
    n_full, _ = _attn_tile_counts(i, tq=tq, tk=tk, past=past, kv_len=kv_len, minimum=jnp.minimum)
    n_visits = n_full + 1

    def scores(j):
        kt = k_ref[pl.ds(pl.multiple_of(j * tk, tk), tk), :]
        return _dot(kt, q2t_ref[...])

    def visited_tile(v):
        return jnp.where(v == 0, n_full, jnp.maximum(v - 1, 0))

    ncb = w // LANES
    pw = p_ref.shape[-1]

    def store_scores(buf, s):
        for cb in range(ncb):
            s_ref[buf, cb] = s[:, cb * LANES:(cb + 1) * LANES]

    def stage_a(v, buf):
        store_scores(buf, scores(jnp.minimum(v - 1, n_full)))

    rows = min(tk, ATT_SOFTMAX_ROWS)

    def stage_b(buf):
        for cb in range(ncb):
            sl = slice(cb * LANES, (cb + 1) * LANES)
            m_old = m_ref[:, sl]
            m_new = m_old
            for r0 in range(0, tk, rows):
                m_new = jnp.maximum(m_new, _column_max(s_ref[buf, cb, r0:r0 + rows, :]))
            m_ref[:, sl] = m_new
            alpha_ref[buf, :, sl] = jnp.exp2(m_old - m_new)
            pl0 = (cb * LANES) % pw
            for r0 in range(0, tk, rows):
                p_ref[buf, cb * LANES // pw, r0:r0 + rows, pl0:pl0 + LANES] = jnp.exp2(
                    (s_ref[buf, cb, r0:r0 + rows, :] - m_new).astype(BF16))

    def stage_c(v, buf):
        vt = vt_ref[visited_tile(v)]
        for pb in range(w // pw):
            sl = slice(pb * pw, (pb + 1) * pw)
            acc_ref[:, sl] = alpha_ref[buf, :, sl] * acc_ref[:, sl] + _dot(vt, p_ref[buf, pb])

    def first_visit(qt, tile):
        zero = jnp.zeros_like(qt)
        q2t_ref[:, 0:tq] = jnp.where(row < ATT_HEAD_DIM, qt, zero)
        q2t_ref[:, tq:w] = jnp.where(row >= ATT_HEAD_DIM, qt, zero)
        nf, _ = _attn_tile_counts(tile, tq=tq, tk=tk, past=past, kv_len=kv_len, minimum=jnp.minimum)
        nf = jnp.minimum(nf, nkt - 1)
        qchunk = (past + tile * tq + lax.broadcasted_iota(jnp.int32, (1, w), 1) % tq) // CHUNK
        s_part = scores(nf)
        s_masked = []
        for kb in range(tk // CHUNK):
            k0 = nf * tk + kb * CHUNK
            kchunk = jnp.where(k0 < kv_len, k0 // CHUNK, jnp.iinfo(jnp.int32).max)
            s_masked.append(jnp.where(kchunk <= qchunk, s_part[kb * CHUNK:(kb + 1) * CHUNK], NEG_BIG))
        store_scores(0, jnp.concatenate(s_masked, axis=0))

    @pl.when(i == 0)
    def _():
        first_visit(qt_ref[0], i)

    m_ref[...] = jnp.full_like(m_ref, NEG_BIG)
    acc_ref[...] = jnp.zeros_like(acc_ref)
    p_ref[nbuf - 1] = jnp.zeros(p_ref.shape[1:], BF16)
    alpha_ref[nbuf - 1] = jnp.ones(alpha_ref.shape[1:], F32)

    def visits(v0, count, prefetch_last):
        for r in range(count):
            prefetch = r + 1 < count or prefetch_last
            if prefetch and nbuf > 1:
                stage_a(v0 + r + 1, (r + 1) % nbuf)
            stage_c(v0 + r - 1, (r - 1) % nbuf)
            stage_b(r)
            if prefetch and nbuf == 1:
                stage_a(v0 + r + 1, (r + 1) % nbuf)

    def trip(u, carry):
        visits(nbuf * u, nbuf, True)
        return carry

    lax.fori_loop(0, n_visits // nbuf, trip, 0)
    for rem in range(nbuf):

        @pl.when(n_visits % nbuf == rem)
        def _():
            visits(n_visits - rem, rem, False)
            stage_c(n_visits - 1, (rem - 1) % nbuf)

    first_visit(qt_next_ref[0], i + 1)

    lam_v = lam_ref[...]
    lam = (jnp.exp(jnp.sum(lam_v[0:1] * lam_v[1:2], axis=-1, keepdims=True))
           - jnp.exp(jnp.sum(lam_v[2:3] * lam_v[3:4], axis=-1, keepdims=True)) + lam_init)
    acc = acc_ref[...]
    o = acc[:ATT_V_DIM] / acc[ATT_V_DIM:ATT_V_DIM + 1]
    o = o[:, :tq] - lam * o[:, tq:]
    o = o * lax.rsqrt(jnp.mean(o * o, axis=0, keepdims=True) + ATT_NORM_EPS) * (sw_ref[...] * (1.0 - lam_init))
    o_ref[...] = o.T.astype(BF16)


def _diff_attn(qt, k, vt, lam_vecs, subln_col, *, nb, tq, tk, past, kv_len, lam_init, nbuf):
    nq = qt.shape[0] // nb
    nkt = vt.shape[0] // nb
    assert tk % CHUNK == 0 and kv_len % CHUNK == 0
    for i in range(nq):
        n_full, n_end = _attn_tile_counts(i, tq=tq, tk=tk, past=past, kv_len=kv_len)
        assert n_end - n_full == 1 and n_end <= nkt, (i, n_full, n_end)
    w = 2 * tq
    return pl.pallas_call(
        functools.partial(_diff_attn_kernel, tq=tq, tk=tk, past=past, kv_len=kv_len, lam_init=lam_init, nbuf=nbuf,
                          nkt=nkt),
        grid=(nb, ATT_HEADS, nq),
        in_specs=[pl.BlockSpec((1, ATT_V_DIM, tq), lambda b, h, i: (b * nq + i, h, 0)),
                  pl.BlockSpec((1, ATT_V_DIM, tq), lambda b, h, i: (b * nq + jnp.minimum(i + 1, nq - 1), h, 0)),
                  pl.BlockSpec((nkt * tk, ATT_V_DIM), lambda b, h, i: (b, h)),
                  pl.BlockSpec((nkt, ATT_VT_ROWS, tk), lambda b, h, i: (b, h, 0)),
                  _const_spec((4, ATT_HEAD_DIM)), _const_spec((ATT_V_DIM, 1))],
        out_specs=pl.BlockSpec((tq, ATT_V_DIM), lambda b, h, i: (b * nq + i, h)),
        out_shape=jax.ShapeDtypeStruct((nb * nq * tq, ATT_WIDTH), BF16),
        scratch_shapes=[pltpu.VMEM((ATT_V_DIM, w), BF16), pltpu.VMEM((nbuf, w // LANES, tk, LANES), F32),
                        pltpu.VMEM((nbuf, w // MXU_TILE, tk, MXU_TILE), BF16), pltpu.VMEM((1, w), F32),
                        pltpu.VMEM((nbuf, 1, w), F32),
                        pltpu.VMEM((ATT_VT_ROWS, w), F32)],
        compiler_params=pltpu.CompilerParams(dimension_semantics=("arbitrary", "arbitrary", "arbitrary"),
                                             vmem_limit_bytes=VMEM_LIMIT),
        name="diff_attn",
    )(qt, qt, k, vt, lam_vecs, subln_col)


def _decode_attn_kernel(q_ref, kn_ref, vn_ref, kc_ref, vc_ref, lam_ref, sw_ref, o_ref, *, seq, past, lam_init):
    lam_v = lam_ref[...]
    lam = (jnp.exp(jnp.sum(lam_v[0:1] * lam_v[1:2], axis=-1, keepdims=True))
           - jnp.exp(jnp.sum(lam_v[2:3] * lam_v[3:4], axis=-1, keepdims=True)) + lam_init)
    npad = LANES
    lane = lax.broadcasted_iota(jnp.int32, (seq, LANES), 1)
    qchunk = (past + lax.broadcasted_iota(jnp.int32, (2 * seq, 1), 0) % seq) // CHUNK
    kpos_p = lax.broadcasted_iota(jnp.int32, (2 * seq, past), 1)
    kpos_n = lax.broadcasted_iota(jnp.int32, (2 * seq, npad), 1)
    vis_p = kpos_p // CHUNK <= qchunk
    vis_n = ((past + kpos_n) // CHUNK <= qchunk) & (kpos_n < seq)
    pad_rows = jnp.zeros((npad - seq, LANES), BF16)
    for h in range(ATT_HEADS):
        sl = slice(h * LANES, (h + 1) * LANES)
        qh = q_ref[:, sl]
        zero = jnp.zeros_like(qh)
        q2 = jnp.concatenate([jnp.where(lane < ATT_HEAD_DIM, qh, zero), jnp.where(lane >= ATT_HEAD_DIM, qh, zero)], axis=0)
        s_p = jnp.where(vis_p, _dot(q2, kc_ref[0, sl, :].astype(BF16)), NEG_BIG)
        kn = jnp.concatenate([kn_ref[:, sl].astype(BF16), pad_rows], axis=0)
        s_n = lax.dot_general(q2, kn, (((1,), (1,)), ((), ())), preferred_element_type=F32)
        s_n = jnp.where(vis_n, s_n, NEG_BIG)
        m = jnp.maximum(jnp.max(s_p, axis=-1, keepdims=True), jnp.max(s_n, axis=-1, keepdims=True))
        p_p = jnp.exp2(s_p - m)
        p_n = jnp.exp2(s_n - m)
        l = jnp.sum(p_p, axis=-1, keepdims=True) + jnp.sum(p_n, axis=-1, keepdims=True)
        vh = vc_ref[0, pl.ds(h, past, stride=ATT_HEADS), :].astype(BF16)
        vn = jnp.concatenate([vn_ref[:, sl].astype(BF16), pad_rows], axis=0)
        o = (_dot(p_p.astype(BF16), vh) + _dot(p_n.astype(BF16), vn)) / l
        o = o[:seq] - lam * o[seq:]
        o = o * lax.rsqrt(jnp.mean(o * o, axis=-1, keepdims=True) + ATT_NORM_EPS) * (sw_ref[...] * (1.0 - lam_init))
        o_ref[:, sl] = o.astype(BF16)


def _decode_attn(q, k_new, v_new, kt_cache, v_cache, lam_vecs, subln_row, *, nb, seq, past, lam_init):
    row = pl.BlockSpec((seq, ATT_WIDTH), lambda b: (b, 0))
    return pl.pallas_call(
        functools.partial(_decode_attn_kernel, seq=seq, past=past, lam_init=lam_init),
        grid=(nb,),
        in_specs=[row, row, row,
                  pl.BlockSpec((1, ATT_WIDTH, past), lambda b: (b, 0, 0)),
                  pl.BlockSpec((1, past * ATT_HEADS, ATT_V_DIM), lambda b: (b, 0, 0)),
                  _const_spec((4, ATT_HEAD_DIM)), _const_spec((1, ATT_V_DIM))],
        out_specs=row,
        out_shape=jax.ShapeDtypeStruct((nb * seq, ATT_WIDTH), BF16),
        compiler_params=pltpu.CompilerParams(dimension_semantics=("arbitrary",), vmem_limit_bytes=VMEM_LIMIT),
        name="decode_attn",
    )(q, k_new, v_new, kt_cache, v_cache, lam_vecs, subln_row)


def _post_mix_kernel(x_ref, ys_ref, ya_ref, mk_ref, mv_ref, wo_ref, wq_ref, wox_ref,
                     g1_ref, g2_ref, g3_ref, h_ref, *, spt, rows):
    mix = _dot(ys_ref[...], wo_ref[0:SSD_WIDTH, :]) + _dot(ya_ref[...], wo_ref[SSD_WIDTH:, :])
    h = x_ref[...] + _rms(mix, g1_ref[...], NORM_EPS)
    qn = _rms(h, g2_ref[...], NORM_EPS).astype(BF16)
    qx = (_dot(qn, wq_ref[...]) * (1.0 / math.sqrt(MEM_HEAD_DIM))).astype(BF16)
    ox_seqs = []
    for s_i in range(spt):
        qs = qx[s_i * rows:(s_i + 1) * rows]
        mk = mk_ref[s_i]
        mv = mv_ref[s_i]
        oxs = []
        for hd in range(MEM_HEADS):
            sl = slice(hd * MEM_HEAD_DIM, (hd + 1) * MEM_HEAD_DIM)
            s = lax.dot_general(qs[:, sl], mk[:, sl], (((1,), (1,)), ((), ())), preferred_element_type=F32)
            p = jnp.exp(s - jnp.max(s, axis=-1, keepdims=True))
            ox = _dot(p.astype(BF16), mv[:, sl]) / jnp.sum(p, axis=-1, keepdims=True)
            oxs.append(ox.astype(BF16))
        ox_seqs.append(jnp.concatenate(oxs, axis=1))
    ox_all = ox_seqs[0] if spt == 1 else jnp.concatenate(ox_seqs, axis=0)
    o2 = _dot(ox_all, wox_ref[...])
    h_ref[...] = h + _rms(o2, g3_ref[...], NORM_EPS)


def _post_mix(x, ys, ya, mk, mv, w_out, wq, wox, g1, g2, g3, *, seq, tm):
    n = x.shape[0]
    spt = max(1, tm // seq)
    tiles_per_seq = max(1, seq // tm)
    row = lambda c: pl.BlockSpec((tm, c), lambda i: (i, 0))
    mem = pl.BlockSpec((spt, MEM_LEN, D_MODEL), lambda i: (i // tiles_per_seq, 0, 0))
    wspec = _const_spec((D_MODEL, D_MODEL))
    gspec = _const_spec((1, D_MODEL))
    return pl.pallas_call(
        functools.partial(_post_mix_kernel, spt=spt, rows=tm // spt),
        grid=(n // tm,),
        in_specs=[row(D_MODEL), row(SSD_WIDTH), row(ATT_WIDTH), mem, mem, wspec, wspec, wspec, gspec, gspec, gspec],
        out_specs=row(D_MODEL),
        out_shape=jax.ShapeDtypeStruct((n, D_MODEL), F32),
        compiler_params=pltpu.CompilerParams(dimension_semantics=("arbitrary",), vmem_limit_bytes=VMEM_LIMIT),
        name="post_mix",
    )(x, ys, ya, mk, mv, w_out, wq, wox, g1, g2, g3)


def _ffn_kernel(h_ref, wg_ref, wu_ref, wd_ref, g1_ref, g2_ref, o_ref):
    h = h_ref[...]
    hn = _rms(h, g1_ref[...], NORM_EPS).astype(BF16)
    act = (_silu(_dot(hn, wg_ref[...])) * _dot(hn, wu_ref[...])).astype(BF16)
    f = _dot(act, wd_ref[...])
    o_ref[...] = h + _rms(f, g2_ref[...], NORM_EPS)


def _ffn(h, wg, wu, wd, g1, g2, *, tm):
    n = h.shape[0]
    row = pl.BlockSpec((tm, D_MODEL), lambda i: (i, 0))
    return pl.pallas_call(
        _ffn_kernel,
        grid=(n // tm,),
        in_specs=[row, _const_spec((D_MODEL, FFN_HIDDEN)), _const_spec((D_MODEL, FFN_HIDDEN)),
                  _const_spec((FFN_HIDDEN, D_MODEL)), _const_spec((1, D_MODEL)), _const_spec((1, D_MODEL))],
        out_specs=row,
        out_shape=jax.ShapeDtypeStruct((n, D_MODEL), F32),
        compiler_params=pltpu.CompilerParams(dimension_semantics=("arbitrary",), vmem_limit_bytes=VMEM_LIMIT),
        name="ffn",
    )(h, wg, wu, wd, g1, g2)


def _rope_tables(past, seq):
    half = ATT_HEAD_DIM // 2
    inv = jnp.power(ROPE_THETA, -jnp.arange(0, ATT_HEAD_DIM, 2, dtype=F32) / ATT_HEAD_DIM)
    pos = (past + jnp.arange(seq, dtype=jnp.int32)).astype(F32)
    ang = pos[:, None] * inv[None, :]
    cos, sin = jnp.cos(ang), jnp.sin(ang)
    reps = LANES // ATT_HEAD_DIM
    assert half * 2 == ATT_HEAD_DIM
    return jnp.tile(jnp.concatenate([cos, cos], axis=-1), (1, reps)), jnp.tile(jnp.concatenate([-sin, sin], axis=-1), (1, reps))


def _state_to_kernel_layout(s):
    b = s.shape[0]
    s = s.reshape(b, SSD_GROUPS, SSD_REP, SSD_HEAD_DIM, SSD_STATE)
    return s.transpose(0, 1, 4, 2, 3).reshape(b, SSD_GROUPS, SSD_STATE, SSD_REP * SSD_HEAD_DIM)


def _state_from_kernel_layout(s):
    b = s.shape[0]
    s = s.reshape(b, SSD_GROUPS, SSD_STATE, SSD_REP, SSD_HEAD_DIM)
    return s.transpose(0, 1, 3, 4, 2).reshape(b, SSD_HEADS, SSD_HEAD_DIM, SSD_STATE)


def _layer(x, conv_buf, ssm0, kt_past, v_past, mem_kb, mem_vb, lam_init, p, *, tm, tk, tq, ssd_q):
    nb, seq, _ = x.shape
    n = nb * seq
    past = 0 if kt_past is None else kt_past.shape[2]
    xf = x.reshape(n, D_MODEL)
    cos, sin = _rope_tables(past, seq)
    no_history = kt_past is None
    assert tq == tk or not no_history
    cbuf = jnp.pad(conv_buf.astype(F32), ((0, 0), (8 - (SSD_CONV - 1), 0), (0, 0)))
    gate, u, tail, dt, q, k, v, qt, kb, vt = _in_proj(xf, p["g_pre_mix"], p["w_in"], cos, sin, cbuf, p["conv_w"],
                                                       p["conv_b"], seq=seq, tm=tm, tt=tk if no_history else 0)

    seq_pad = -(-seq // ssd_q) * ssd_q
    if seq_pad != seq:
        pad = lambda a: jnp.pad(a.reshape(nb, seq, -1), ((0, 0), (0, seq_pad - seq), (0, 0))).reshape(nb * seq_pad, -1)
        gate_s, u_s, dt_s = pad(gate), pad(u), pad(dt)
    else:
        gate_s, u_s, dt_s = gate, u, dt
    y_ssd, h_new = _ssd(gate_s, u_s, dt_s, _state_to_kernel_layout(ssm0.astype(F32)),
                        p["dt_bias"], p["a_log"], p["d_skip"], p["ssm_norm_w"], p["e2"],
                        nb=nb, seq=seq_pad, q=ssd_q, valid=None if seq_pad == seq else seq)
    if seq_pad != seq:
        y_ssd = y_ssd.reshape(nb, seq_pad, SSD_WIDTH)[:, :seq].reshape(n, SSD_WIDTH)
    ssm_new = _state_from_kernel_layout(h_new)
    ext_tail = tail.reshape(nb, -1, 8, SSD_XBC)[:, -1]
    if seq >= SSD_CONV - 1:
        conv_new = ext_tail[:, 8 - (SSD_CONV - 1):]
    else:
        conv_new = jnp.concatenate([conv_buf.astype(F32), ext_tail[:, 8 - seq:]], axis=1)[:, -(SSD_CONV - 1):]

    if no_history:
        kv_len = seq
        y_att = _diff_attn(qt, kb, vt, p["lam_vecs"], p["subln_col"], nb=nb, tq=tq, tk=tk,
                           past=past, kv_len=kv_len, lam_init=lam_init, nbuf=ATT_PIPELINE_BUFFERS)
    else:
        y_att = _decode_attn(q, k, v, kt_past, v_past, p["lam_vecs"], p["subln_col"].reshape(1, ATT_V_DIM),
                             nb=nb, seq=seq, past=past, lam_init=lam_init)

    h = _post_mix(xf, y_ssd, y_att, mem_kb, mem_vb, p["w_out"], p["wq_x"], p["wo_x"],
                  p["g_post_mix"], p["g_pre_x"], p["g_post_x"], seq=seq, tm=tm)
    out = _ffn(h, p["w_gate"], p["w_up"], p["w_down"], p["g_pre_ffn"], p["g_post_ffn"], tm=tm)
    if no_history:
        k_out = k.reshape(nb, ATT_HEADS, 2, ATT_HEAD_DIM, seq).transpose(0, 4, 1, 2, 3)
    else:
        k_out = k.reshape(nb, seq, ATT_HEADS, 2, ATT_HEAD_DIM)
    return (out.reshape(nb, seq, D_MODEL), k_out,
            v.reshape(nb, seq, ATT_HEADS, ATT_V_DIM), ssm_new, conv_new)


def _prep_params(i, w_in, conv_w, conv_b, dt_bias, a_log, d_skip, ssm_norm_w, lam_q1, lam_k1, lam_q2, lam_k2, subln_w,
                 w_out, wq_x, wo_x, g_pre_mix, g_post_mix, g_pre_x, g_post_x, g_pre_ffn, g_post_ffn,
                 w_gate, w_up, w_down):
    w = w_in[i]
    s0 = SSD_WIDTH + SSD_XBC
    s1 = s0 + SSD_HEADS
    w_r = jnp.concatenate([w[:, :s0], w[:, s1:], w[:, s0:s1], jnp.zeros((D_MODEL, DT_PAD - SSD_HEADS), w.dtype)], axis=1)
    head_pad = lambda a: jnp.pad(a[i].astype(F32), (0, DT_PAD - SSD_HEADS)).reshape(1, DT_PAD)
    e = (jnp.arange(DT_PAD)[:, None] == (jnp.arange(SSD_WIDTH)[None, :] // SSD_HEAD_DIM)).astype(BF16)
    row = lambda a: a[i].astype(F32).reshape(1, -1)
    w_out_b, wq_b, wo_b = _cast_bf16([w_out[i], wq_x[i], wo_x[i]], steps=4)
    w_gate_b, w_up_b, w_down_b = _cast_bf16([w_gate[i], w_up[i], w_down[i]], steps=8)
    return {
        "w_in": w_r.astype(BF16), "conv_w": conv_w[i].astype(F32), "conv_b": row(conv_b),
        "dt_bias": head_pad(dt_bias), "a_log": head_pad(a_log), "d_skip": head_pad(d_skip),
        "ssm_norm_w": row(ssm_norm_w), "e2": jnp.concatenate([e, e], axis=0),
        "lam_vecs": jnp.stack([lam_q1[i], lam_k1[i], lam_q2[i], lam_k2[i]]).astype(F32), "subln_col": subln_w[i].astype(F32).reshape(ATT_V_DIM, 1),
        "w_out": w_out_b, "wq_x": wq_b, "wo_x": wo_b,
        "g_pre_mix": row(g_pre_mix), "g_post_mix": row(g_post_mix), "g_pre_x": row(g_pre_x), "g_post_x": row(g_post_x),
        "g_pre_ffn": row(g_pre_ffn), "g_post_ffn": row(g_post_ffn),
        "w_gate": w_gate_b, "w_up": w_up_b, "w_down": w_down_b,
    }


def kernel(x_prompt, x_sample, cache_attn_k, cache_attn_v, state_ssm, state_conv, cache_mem_k, cache_mem_v, mem_prompt, w_in, conv_w, conv_b, dt_bias, a_log, d_skip, ssm_norm_w, lam_q1, lam_k1, lam_q2, lam_k2, subln_w, w_out, g_mem, wq_x, wk_x, wv_x, wo_x, g_pre_mix, g_post_mix, g_pre_x, g_post_x, g_pre_ffn, g_post_ffn, w_gate, w_up, w_down):
    depth = w_in.shape[0]
    bp, sp, _ = x_prompt.shape
    bs, ss, _ = x_sample.shape
    hp, hs = x_prompt, x_sample
    outs = [[] for _ in range(10)]
    for i in range(depth):
        lam_init = 0.8 - 0.6 * math.exp(-0.3 * i)
        p = _prep_params(i, w_in, conv_w, conv_b, dt_bias, a_log, d_skip, ssm_norm_w, lam_q1, lam_k1, lam_q2, lam_k2,
                         subln_w, w_out, wq_x, wo_x, g_pre_mix, g_post_mix, g_pre_x, g_post_x, g_pre_ffn, g_post_ffn,
                         w_gate, w_up, w_down)
        mk, mv, mkb, mvb = _mem_kv(mem_prompt.reshape(bp * MEM_LEN, D_MODEL), g_mem[i].reshape(1, D_MODEL),
                                   wk_x[i], wv_x[i])
        hp, k_new, v_new, ssm_new, conv_new = _layer(
            hp, jnp.zeros((bp, SSD_CONV - 1, SSD_XBC), F32), jnp.zeros((bp, SSD_HEADS, SSD_HEAD_DIM, SSD_STATE), F32),
            None, None, mkb.reshape(bp, MEM_LEN, D_MODEL), mvb.reshape(bp, MEM_LEN, D_MODEL), lam_init, p,
            tm=512, tk=512, tq=512, ssd_q=256)
        for lst, val in zip(outs[:6], (k_new, v_new, ssm_new, conv_new,
                                       mk.reshape(bp, MEM_LEN, MEM_HEADS, MEM_HEAD_DIM),
                                       mv.reshape(bp, MEM_LEN, MEM_HEADS, MEM_HEAD_DIM))):
            lst.append(val)
        past = cache_attn_k.shape[2]
        hs, k_new, v_new, ssm_new, conv_new = _layer(
            hs, state_conv[i], state_ssm[i],
            cache_attn_k[i].transpose(0, 2, 3, 4, 1).reshape(bs, ATT_WIDTH, past),
            cache_attn_v[i].reshape(bs, past * ATT_HEADS, ATT_V_DIM),
            cache_mem_k[i].reshape(bs, MEM_LEN, D_MODEL).astype(BF16),
            cache_mem_v[i].reshape(bs, MEM_LEN, D_MODEL).astype(BF16), lam_init, p,
            tm=bs * ss, tk=0, tq=0, ssd_q=128)
        for lst, val in zip(outs[6:], (k_new, v_new, ssm_new, conv_new)):
            lst.append(val)
    return (hp, hs) + tuple(jnp.stack(o) for o in outs)
```

```python
import functools
import math

import jax
import jax.numpy as jnp
from jax import lax
from jax.experimental import pallas as pl
from jax.experimental.pallas import tpu as pltpu

D_MODEL = 1024
CHUNK = 64
SSD_WIDTH = 512
SSD_HEAD_DIM = 64
SSD_HEADS = 8
SSD_GROUPS = 2
SSD_REP = 4
SSD_STATE = 128
SSD_CONV = 4
SSD_XBC = 1024
SSD_NORM_GROUP = 256
SSD_NORM_EPS = 1e-5
ATT_WIDTH = 512
ATT_HEAD_DIM = 64
ATT_HEADS = 4
ATT_V_DIM = 128
ATT_NORM_EPS = 1e-5
ROPE_THETA = 10000.0
MEM_LEN = 256
MEM_HEADS = 4
MEM_HEAD_DIM = 256
FFN_HIDDEN = 2816
NORM_EPS = 1e-6
LANES = 128
SUBLANES = 8
DT_PAD = LANES
IN_COLS_PADDED = SSD_WIDTH + SSD_XBC + 3 * ATT_WIDTH + DT_PAD
VMEM_LIMIT = 56 * 1024 * 1024
NEG_BIG = -1e30
MXU_TILE = 256
BF16_SUBLANES = 16
ATT_VT_ROWS = ATT_V_DIM + BF16_SUBLANES
ATT_SOFTMAX_ROWS = 128
ATT_PIPELINE_BUFFERS = 2
Q_SCALE = math.log2(math.e) / math.sqrt(ATT_HEAD_DIM)

F32 = jnp.float32
BF16 = jnp.bfloat16


def _const_spec(shape):
    return pl.BlockSpec(shape, lambda *_: (0,) * len(shape), pipeline_mode=pl.Buffered(1))


def _rms(x, g, eps):
    return x * lax.rsqrt(jnp.mean(x * x, axis=-1, keepdims=True) + eps) * g


def _silu(x):
    h = 0.5 * x
    return h + h * jnp.tanh(h)


def _dot(a, b):
    return jnp.dot(a, b, preferred_element_type=F32)


def _cast_kernel(*refs):
    n = len(refs) // 2
    for x_ref, o_ref in zip(refs[:n], refs[n:]):
        o_ref[...] = x_ref[...].astype(BF16)


def _cast_bf16(arrays, steps):
    specs = [pl.BlockSpec((a.shape[0] // steps, a.shape[1]), lambda i: (i, 0)) for a in arrays]
    assert all(a.shape[0] % (steps * BF16_SUBLANES) == 0 for a in arrays)
    return pl.pallas_call(
        _cast_kernel,
        grid=(steps,),
        in_specs=specs,
        out_specs=specs,
        out_shape=[jax.ShapeDtypeStruct(a.shape, BF16) for a in arrays],
        compiler_params=pltpu.CompilerParams(dimension_semantics=("arbitrary",), vmem_limit_bytes=VMEM_LIMIT),
        name="cast_bf16",
    )(*arrays)


def _mem_kv_kernel(mem_ref, g_ref, wk_ref, wv_ref, mk_ref, mv_ref, mkb_ref, mvb_ref):
    mn = _rms(mem_ref[...], g_ref[...], NORM_EPS).astype(BF16)
    mk = _dot(mn, wk_ref[...].astype(BF16))
    mv = _dot(mn, wv_ref[...].astype(BF16))
    mk_ref[...] = mk
    mv_ref[...] = mv
    mkb_ref[...] = mk.astype(BF16)
    mvb_ref[...] = mv.astype(BF16)


def _mem_kv(mem, g_mem, wk, wv):
    n = mem.shape[0]
    tm = MEM_LEN
    row = pl.BlockSpec((tm, D_MODEL), lambda i: (i, 0))
    return pl.pallas_call(
        _mem_kv_kernel,
        grid=(n // tm,),
        in_specs=[row, _const_spec((1, D_MODEL)), _const_spec((D_MODEL, D_MODEL)), _const_spec((D_MODEL, D_MODEL))],
        out_specs=[row, row, row, row],
        out_shape=[jax.ShapeDtypeStruct((n, D_MODEL), F32), jax.ShapeDtypeStruct((n, D_MODEL), F32),
                   jax.ShapeDtypeStruct((n, D_MODEL), BF16), jax.ShapeDtypeStruct((n, D_MODEL), BF16)],
        compiler_params=pltpu.CompilerParams(dimension_semantics=("arbitrary",), vmem_limit_bytes=VMEM_LIMIT),
        name="mem_kv",
    )(mem, g_mem, wk, wv)


def _in_proj_kernel(x_ref, g_ref, w_ref, cos_ref, sin_ref, cbuf_ref, cw_ref, cb_ref,
                    z_ref, u_ref, tail_ref, dt_ref, q_ref, k_ref, v_ref, qt_ref, kb_ref, vt_ref, acc_ref, xtail,
                    *, tm, tt, spt, tiles_per_seq):
    step = pl.program_id(0)
    tile = step - 1

    @pl.when(step == 0)
    def _():
        acc_ref[...] = jnp.zeros_like(acc_ref)
        xtail[...] = jnp.zeros_like(xtail)

    c_xbc, c_q = SSD_WIDTH, SSD_WIDTH + SSD_XBC
    c_k, c_v, c_dt = c_q + ATT_WIDTH, c_q + 2 * ATT_WIDTH, c_q + 3 * ATT_WIDTH
    z_ref[...] = _silu(acc_ref[:, :SSD_WIDTH]).astype(BF16)
    rows = tm // spt
    row8 = lax.broadcasted_iota(jnp.int32, (SUBLANES, LANES), 0)
    first_tile_of_seq = tile % tiles_per_seq == 0
    for s_i in range(spt):
        r0 = s_i * rows
        tail_ref[s_i] = acc_ref[r0 + rows - SUBLANES:r0 + rows, c_xbc:c_q]
        for cb in range(SSD_XBC // LANES):
            sl = slice(cb * LANES, (cb + 1) * LANES)
            piece = acc_ref[r0:r0 + rows, c_xbc + cb * LANES:c_xbc + (cb + 1) * LANES]
            if spt > 1:
                prev = cbuf_ref[s_i, :, sl]
            else:
                prev = jnp.where(first_tile_of_seq, cbuf_ref[0, :, sl], xtail[:, sl])
            conv_top = cb_ref[:, sl] + piece[0:SUBLANES] * cw_ref[SSD_CONV - 1:SSD_CONV, sl]
            conv_rest = cb_ref[:, sl] + piece[SUBLANES:] * cw_ref[SSD_CONV - 1:SSD_CONV, sl]
            for d in range(1, SSD_CONV):
                sh = pltpu.roll(piece, d, 0)
                tap = cw_ref[SSD_CONV - 1 - d:SSD_CONV - d, sl]
                conv_top = conv_top + jnp.where(row8 < d, pltpu.roll(prev, d, 0), sh[0:SUBLANES]) * tap
                conv_rest = conv_rest + sh[SUBLANES:] * tap
            u_ref[r0:r0 + rows, sl] = _silu(jnp.concatenate([conv_top, conv_rest], axis=0)).astype(BF16)
    if spt == 1:
        xtail[...] = acc_ref[tm - SUBLANES:, c_xbc:c_q]
    dt_ref[...] = acc_ref[:, c_dt:c_dt + DT_PAD]
    if tt:
        for j in range(ATT_HEADS):
            v_ref[pl.ds(j, tm, stride=ATT_HEADS), :] = acc_ref[:, c_v + j * ATT_V_DIM:c_v + (j + 1) * ATT_V_DIM]
    else:
        v_ref[...] = acc_ref[:, c_v:c_v + ATT_WIDTH]

    cos = cos_ref[...]
    sin = sin_ref[...]
    first_half = (lax.broadcasted_iota(jnp.int32, (tm, LANES), 1) % ATT_HEAD_DIM) < (ATT_HEAD_DIM // 2)

    def rope(t):
        swapped = jnp.where(first_half, pltpu.roll(t, LANES - ATT_HEAD_DIM // 2, 1),
                            pltpu.roll(t, ATT_HEAD_DIM // 2, 1))
        return t * cos + swapped * sin

    for j in range(ATT_WIDTH // LANES):
        sl = slice(j * LANES, (j + 1) * LANES)
        qr = rope(acc_ref[:, c_q + j * LANES:c_q + (j + 1) * LANES]) * Q_SCALE
        kr = rope(acc_ref[:, c_k + j * LANES:c_k + (j + 1) * LANES])
        q_ref[:, sl] = qr.astype(BF16)
        if tt:
            k_ref[0, sl, :] = kr.T
            kb_ref[:, sl] = kr.astype(BF16)
            for c in range(tm // tt):
                qt_ref[c, sl, :] = qr[c * tt:(c + 1) * tt, :].T.astype(BF16)
                vt_ref[c, j * ATT_VT_ROWS:j * ATT_VT_ROWS + LANES, :] = acc_ref[
                    c * tt:(c + 1) * tt, c_v + j * LANES:c_v + (j + 1) * LANES].T.astype(BF16)
                vt_ref[c, j * ATT_VT_ROWS + LANES:(j + 1) * ATT_VT_ROWS, :] = jnp.ones((ATT_VT_ROWS - LANES, tt), BF16)
        else:
            k_ref[:, sl] = kr

    hn = _rms(x_ref[...], g_ref[...], NORM_EPS).astype(BF16)
    acc_ref[...] = _dot(hn, w_ref[...])


def _in_proj(x, g, w, cos, sin, cbuf, cw, cb, *, seq, tm, tt):
    n = x.shape[0]
    nt = n // tm
    spt = max(1, tm // seq)
    tiles_per_seq = max(1, seq // tm)
    if spt > 1:
        cos, sin = jnp.tile(cos, (spt, 1)), jnp.tile(sin, (spt, 1))
    done = lambda i: jnp.maximum(i - 1, 0)
    row = lambda c: pl.BlockSpec((tm, c), lambda i: (done(i), 0))
    tab = pl.BlockSpec((tm, LANES), lambda i: (done(i) % tiles_per_seq, 0))
    if tt:
        k_spec = pl.BlockSpec((1, ATT_WIDTH, tm), lambda i: (done(i) // tiles_per_seq, 0, done(i) % tiles_per_seq))
        k_shape = jax.ShapeDtypeStruct((n // seq, ATT_WIDTH, seq), F32)
        v_spec = pl.BlockSpec((tm * ATT_HEADS, ATT_V_DIM), lambda i: (done(i), 0))
        v_shape = jax.ShapeDtypeStruct((n * ATT_HEADS, ATT_V_DIM), F32)
    else:
        k_spec, k_shape = row(ATT_WIDTH), jax.ShapeDtypeStruct((n, ATT_WIDTH), F32)
        v_spec, v_shape = row(ATT_WIDTH), jax.ShapeDtypeStruct((n, ATT_WIDTH), F32)
    out_specs = [row(SSD_WIDTH), row(SSD_XBC), pl.BlockSpec((spt, SUBLANES, SSD_XBC), lambda i: (done(i), 0, 0)),
                 row(DT_PAD), row(ATT_WIDTH), k_spec, v_spec]
    out_shape = [jax.ShapeDtypeStruct((n, SSD_WIDTH), BF16), jax.ShapeDtypeStruct((n, SSD_XBC), BF16),
                 jax.ShapeDtypeStruct((nt * spt, SUBLANES, SSD_XBC), F32), jax.ShapeDtypeStruct((n, DT_PAD), F32),
                 jax.ShapeDtypeStruct((n, ATT_WIDTH), BF16), k_shape, v_shape]
    if tt:
        tr = lambda r: pl.BlockSpec((tm // tt, r, tt), lambda i: (done(i), 0, 0))
        vt_rows = ATT_HEADS * ATT_VT_ROWS
        out_specs += [tr(ATT_WIDTH), row(ATT_WIDTH), tr(vt_rows)]
        out_shape += [jax.ShapeDtypeStruct((n // tt, ATT_WIDTH, tt), BF16), jax.ShapeDtypeStruct((n, ATT_WIDTH), BF16),
                      jax.ShapeDtypeStruct((n // tt, vt_rows, tt), BF16)]

    def body(*refs):
        refs, scratch = refs[:-2], refs[-2:]
        refs = refs + (None,) * (18 - len(refs))
        _in_proj_kernel(*refs, *scratch, tm=tm, tt=tt, spt=spt, tiles_per_seq=tiles_per_seq)

    outs = pl.pallas_call(
        body,
        grid=(nt + 1,),
        in_specs=[pl.BlockSpec((tm, D_MODEL), lambda i: (jnp.minimum(i, nt - 1), 0)),
                  _const_spec((1, D_MODEL)), _const_spec((D_MODEL, IN_COLS_PADDED)), tab, tab,
                  pl.BlockSpec((spt, SUBLANES, SSD_XBC), lambda i: (done(i) // tiles_per_seq, 0, 0)),
                  _const_spec((SSD_CONV, SSD_XBC)), _const_spec((1, SSD_XBC))],
        out_specs=out_specs,
        out_shape=out_shape,
        scratch_shapes=[pltpu.VMEM((tm, IN_COLS_PADDED), F32), pltpu.VMEM((SUBLANES, SSD_XBC), F32)],
        compiler_params=pltpu.CompilerParams(dimension_semantics=("arbitrary",), vmem_limit_bytes=VMEM_LIMIT),
        name="in_proj",
    )(x, g, w, cos, sin, cbuf, cw, cb)
    return list(outs) + [None] * (10 - len(outs))


def _expand_heads(x, e2):
    hi = x.astype(BF16)
    lo = (x - hi.astype(F32)).astype(BF16)
    return _dot(jnp.concatenate([hi, lo], axis=1), e2)


def _ssd_kernel(gate_ref, u_ref, dt_ref, h0_ref, dtb_ref, alog_ref, dsk_ref, nw_ref, e2_ref,
                y_ref, hout_ref, state, *, q, valid):
    c = pl.program_id(1)

    @pl.when(c == 0)
    def _():
        state[...] = h0_ref[0]

    xs = u_ref[:, :SSD_WIDTH].astype(F32)
    bm = u_ref[:, SSD_WIDTH:SSD_WIDTH + SSD_GROUPS * SSD_STATE]
    cmb = u_ref[:, SSD_WIDTH + SSD_GROUPS * SSD_STATE:]

    dtr = dt_ref[...] + dtb_ref[...]
    dt = jnp.maximum(dtr, 0.0) + jnp.log(1.0 + jnp.exp(-jnp.abs(dtr)))
    if valid is not None:
        row = lax.broadcasted_iota(jnp.int32, (q, DT_PAD), 0) + c * q
        dt = jnp.where(row < valid, dt, 0.0)
    a = -jnp.exp(alog_ref[...])
    ad = dt * a
    ri = lax.broadcasted_iota(jnp.int32, (q, q), 0)
    ci = lax.broadcasted_iota(jnp.int32, (q, q), 1)
    tril = ri >= ci
    tril_b = jnp.where(tril, 1.0, 0.0).astype(BF16)
    ad_hi = ad.astype(BF16)
    ad_r = ad - ad_hi.astype(F32)
    ad_mid = ad_r.astype(BF16)
    ad_lo = (ad_r - ad_mid.astype(F32)).astype(BF16)
    acum = _dot(tril_b, ad_hi) + _dot(tril_b, ad_mid) + _dot(tril_b, ad_lo)
    acum_t = acum.T
    tot = acum[q - 1:q, :]
    e2 = e2_ref[...]
    expanded = _expand_heads(jnp.concatenate([dt, dt * jnp.exp(tot - acum), jnp.exp(acum)], axis=0), e2)
    dtx = expanded[0:q]
    ddx = expanded[q:2 * q]
    eax = expanded[2 * q:3 * q]
    dsk = _expand_heads(jnp.broadcast_to(dsk_ref[...], (SUBLANES, DT_PAD)), e2)[0:1]

    xdt = (xs * dtx).astype(BF16)
    xdtd = (xs * ddx).astype(BF16)
    bm_t = bm.astype(F32).T.astype(BF16)
    gw = SSD_REP * SSD_HEAD_DIM
    stripe = lax.broadcasted_iota(jnp.int32, (q, gw), 1) // SSD_HEAD_DIM
    ys = []
    for g in range(SSD_GROUPS):
        cm_g = cmb[:, g * SSD_STATE:(g + 1) * SSD_STATE]
        bt_g = bm_t[g * SSD_STATE:(g + 1) * SSD_STATE, :]
        cbm = _dot(cm_g, bt_g)
        ms = []
        for r in range(SSD_REP):
            h = g * SSD_REP + r
            diff = acum[:, h:h + 1] - acum_t[h:h + 1, :]
            ms.append((cbm * jnp.exp(jnp.where(tril, diff, -jnp.inf))).astype(BF16))
        ydf = _dot(jnp.concatenate(ms, axis=0), xdt[:, g * gw:(g + 1) * gw])
        yd = ydf[0:q]
        for r in range(1, SSD_REP):
            yd = jnp.where(stripe == r, ydf[r * q:(r + 1) * q], yd)
        st = state[g]
        y_off = _dot(cm_g, st.astype(BF16)) * eax[:, g * gw:(g + 1) * gw]
        state[g] = st * eax[q - 1:q, g * gw:(g + 1) * gw] + _dot(bt_g, xdtd[:, g * gw:(g + 1) * gw])
        ys.append(yd + y_off)
    y = jnp.concatenate(ys, axis=1) + dsk * xs
    y = y * gate_ref[...].astype(F32)
    outs = []
    for g in range(SSD_WIDTH // SSD_NORM_GROUP):
        yg = y[:, g * SSD_NORM_GROUP:(g + 1) * SSD_NORM_GROUP]
        outs.append(yg * lax.rsqrt(jnp.mean(yg * yg, axis=-1, keepdims=True) + SSD_NORM_EPS))
    y_ref[...] = (jnp.concatenate(outs, axis=1) * nw_ref[...]).astype(BF16)

    @pl.when(c == pl.num_programs(1) - 1)
    def _():
        hout_ref[0] = state[...]


def _ssd(gate, u, dt, h0, dtb, alog, dsk, nw, e2, *, nb, seq, q, valid):
    nc = seq // q
    row = lambda c: pl.BlockSpec((q, c), lambda b, i: (b * nc + i, 0))
    gw = SSD_REP * SSD_HEAD_DIM
    st_spec = pl.BlockSpec((1, SSD_GROUPS, SSD_STATE, gw), lambda b, i: (b, 0, 0, 0))
    return pl.pallas_call(
        functools.partial(_ssd_kernel, q=q, valid=valid),
        grid=(nb, nc),
        in_specs=[row(SSD_WIDTH), row(SSD_XBC), row(DT_PAD), st_spec,
                  _const_spec((1, DT_PAD)), _const_spec((1, DT_PAD)), _const_spec((1, DT_PAD)),
                  _const_spec((1, SSD_WIDTH)), _const_spec((2 * DT_PAD, SSD_WIDTH))],
        out_specs=[row(SSD_WIDTH), st_spec],
        out_shape=[jax.ShapeDtypeStruct((nb * seq, SSD_WIDTH), BF16),
                   jax.ShapeDtypeStruct((nb, SSD_GROUPS, SSD_STATE, gw), F32)],
        scratch_shapes=[pltpu.VMEM((SSD_GROUPS, SSD_STATE, gw), F32)],
        compiler_params=pltpu.CompilerParams(dimension_semantics=("arbitrary", "arbitrary"),
                                             vmem_limit_bytes=VMEM_LIMIT),
        name="ssd",
    )(gate, u, dt, h0, dtb, alog, dsk, nw, e2)


def _column_max(x):
    while x.shape[0] > SUBLANES and x.shape[0] % (2 * SUBLANES) == 0:
        half = x.shape[0] // 2
        x = jnp.maximum(x[:half], x[half:])
    return jnp.max(x, axis=0, keepdims=True)


def _attn_tile_counts(i, *, tq, tk, past, kv_len, minimum=min):
    q_lo = past + i * tq
    q_hi = q_lo + tq - 1
    lim_lo = minimum((q_lo // CHUNK + 1) * CHUNK, kv_len)
    lim_hi = minimum((q_hi // CHUNK + 1) * CHUNK, kv_len)
    return lim_lo // tk, (lim_hi + tk - 1) // tk


def _diff_attn_kernel(qt_ref, qt_next_ref, k_ref, vt_ref, lam_ref, sw_ref, o_ref,
                      q2t_ref, s_ref, p_ref, m_ref, alpha_ref, acc_ref, *, tq, tk, past, kv_len, lam_init, nbuf, nkt):
    i = pl.program_id(2)
    w = 2 * tq
    row = lax.broadcasted_iota(jnp.int32, (ATT_V_DIM, tq), 0)
    n_full, _ = _attn_tile_counts(i, tq=tq, tk=tk, past=past, kv_len=kv_len, minimum=jnp.minimum)
    n_visits = n_full + 1

    def scores(j):
        kt = k_ref[pl.ds(pl.multiple_of(j * tk, tk), tk), :]
        return _dot(kt, q2t_ref[...])

    def visited_tile(v):
        return jnp.where(v == 0, n_full, jnp.maximum(v - 1, 0))

    ncb = w // LANES
    pw = p_ref.shape[-1]

    def store_scores(buf, s):
        for cb in range(ncb):
            s_ref[buf, cb] = s[:, cb * LANES:(cb + 1) * LANES]

    def stage_a(v, buf):
        store_scores(buf, scores(jnp.minimum(v - 1, n_full)))

    rows = min(tk, ATT_SOFTMAX_ROWS)

    def stage_b(buf):
        for cb in range(ncb):
            sl = slice(cb * LANES, (cb + 1) * LANES)
            m_old = m_ref[:, sl]
            m_new = m_old
            for r0 in range(0, tk, rows):
                m_new = jnp.maximum(m_new, _column_max(s_ref[buf, cb, r0:r0 + rows, :]))
            m_ref[:, sl] = m_new
            alpha_ref[buf, :, sl] = jnp.exp2(m_old - m_new)
            pl0 = (cb * LANES) % pw
            for r0 in range(0, tk, rows):
                p_ref[buf, cb * LANES // pw, r0:r0 + rows, pl0:pl0 + LANES] = jnp.exp2(
                    (s_ref[buf, cb, r0:r0 + rows, :] - m_new).astype(BF16))

    def stage_c(v, buf):
        vt = vt_ref[visited_tile(v)]
        for pb in range(w // pw):
            sl = slice(pb * pw, (pb + 1) * pw)
            acc_ref[:, sl] = alpha_ref[buf, :, sl] * acc_ref[:, sl] + _dot(vt, p_ref[buf, pb])

    def first_visit(qt, tile):
        zero = jnp.zeros_like(qt)
        q2t_ref[:, 0:tq] = jnp.where(row < ATT_HEAD_DIM, qt, zero)
        q2t_ref[:, tq:w] = jnp.where(row >= ATT_HEAD_DIM, qt, zero)
        nf, _ = _attn_tile_counts(tile, tq=tq, tk=tk, past=past, kv_len=kv_len, minimum=jnp.minimum)
        nf = jnp.minimum(nf, nkt - 1)
        qchunk = (past + tile * tq + lax.broadcasted_iota(jnp.int32, (1, w), 1) % tq) // CHUNK
        s_part = scores(nf)
        s_masked = []
        for kb in range(tk // CHUNK):
            k0 = nf * tk + kb * CHUNK
            kchunk = jnp.where(k0 < kv_len, k0 // CHUNK, jnp.iinfo(jnp.int32).max)
            s_masked.append(jnp.where(kchunk <= qchunk, s_part[kb * CHUNK:(kb + 1) * CHUNK], NEG_BIG))
        store_scores(0, jnp.concatenate(s_masked, axis=0))

    @pl.when(i == 0)
    def _():
        first_visit(qt_ref[0], i)

    m_ref[...] = jnp.full_like(m_ref, NEG_BIG)
    acc_ref[...] = jnp.zeros_like(acc_ref)
    p_ref[nbuf - 1] = jnp.zeros(p_ref.shape[1:], BF16)
    alpha_ref[nbuf - 1] = jnp.ones(alpha_ref.shape[1:], F32)

    def visits(v0, count, prefetch_last):
        for r in range(count):
            prefetch = r + 1 < count or prefetch_last
            if prefetch and nbuf > 1:
                stage_a(v0 + r + 1, (r + 1) % nbuf)
            stage_c(v0 + r - 1, (r - 1) % nbuf)
            stage_b(r)
            if prefetch and nbuf == 1:
                stage_a(v0 + r + 1, (r + 1) % nbuf)

    def trip(u, carry):
        visits(nbuf * u, nbuf, True)
        return carry

    lax.fori_loop(0, n_visits // nbuf, trip, 0)
    for rem in range(nbuf):

        @pl.when(n_visits % nbuf == rem)
        def _():
            visits(n_visits - rem, rem, False)
            stage_c(n_visits - 1, (rem - 1) % nbuf)

    first_visit(qt_next_ref[0], i + 1)

    lam_v = lam_ref[...]
    lam = (jnp.exp(jnp.sum(lam_v[0:1] * lam_v[1:2], axis=-1, keepdims=True))
           - jnp.exp(jnp.sum(lam_v[2:3] * lam_v[3:4], axis=-1, keepdims=True)) + lam_init)
    acc = acc_ref[...]
    o = acc[:ATT_V_DIM] * (1.0 / acc[ATT_V_DIM:ATT_V_DIM + 1])
    o = o[:, :tq] - lam * o[:, tq:]
    o = o * lax.rsqrt(jnp.mean(o * o, axis=0, keepdims=True) + ATT_NORM_EPS) * (sw_ref[...] * (1.0 - lam_init))
    o_ref[...] = o.T.astype(BF16)


def _diff_attn(qt, k, vt, lam_vecs, subln_col, *, nb, tq, tk, past, kv_len, lam_init, nbuf):
    nq = qt.shape[0] // nb
    nkt = vt.shape[0] // nb
    assert tk % CHUNK == 0 and kv_len % CHUNK == 0
    for i in range(nq):
        n_full, n_end = _attn_tile_counts(i, tq=tq, tk=tk, past=past, kv_len=kv_len)
        assert n_end - n_full == 1 and n_end <= nkt, (i, n_full, n_end)
    w = 2 * tq
    return pl.pallas_call(
        functools.partial(_diff_attn_kernel, tq=tq, tk=tk, past=past, kv_len=kv_len, lam_init=lam_init, nbuf=nbuf,
                          nkt=nkt),
        grid=(nb, ATT_HEADS, nq),
        in_specs=[pl.BlockSpec((1, ATT_V_DIM, tq), lambda b, h, i: (b * nq + i, h, 0)),
                  pl.BlockSpec((1, ATT_V_DIM, tq), lambda b, h, i: (b * nq + jnp.minimum(i + 1, nq - 1), h, 0)),
                  pl.BlockSpec((nkt * tk, ATT_V_DIM), lambda b, h, i: (b, h)),
                  pl.BlockSpec((nkt, ATT_VT_ROWS, tk), lambda b, h, i: (b, h, 0)),
                  _const_spec((4, ATT_HEAD_DIM)), _const_spec((ATT_V_DIM, 1))],
        out_specs=pl.BlockSpec((tq, ATT_V_DIM), lambda b, h, i: (b * nq + i, h)),
        out_shape=jax.ShapeDtypeStruct((nb * nq * tq, ATT_WIDTH), BF16),
        scratch_shapes=[pltpu.VMEM((ATT_V_DIM, w), BF16), pltpu.VMEM((nbuf, w // LANES, tk, LANES), F32),
                        pltpu.VMEM((nbuf, w // MXU_TILE, tk, MXU_TILE), BF16), pltpu.VMEM((1, w), F32),
                        pltpu.VMEM((nbuf, 1, w), F32),
                        pltpu.VMEM((ATT_VT_ROWS, w), F32)],
        compiler_params=pltpu.CompilerParams(dimension_semantics=("arbitrary", "arbitrary", "arbitrary"),
                                             vmem_limit_bytes=VMEM_LIMIT),
        name="diff_attn",
    )(qt, qt, k, vt, lam_vecs, subln_col)


def _decode_attn_kernel(q_ref, kn_ref, vn_ref, kc_ref, vc_ref, lam_ref, sw_ref, o_ref, *, seq, past, lam_init):
    lam_v = lam_ref[...]
    lam = (jnp.exp(jnp.sum(lam_v[0:1] * lam_v[1:2], axis=-1, keepdims=True))
           - jnp.exp(jnp.sum(lam_v[2:3] * lam_v[3:4], axis=-1, keepdims=True)) + lam_init)
    npad = LANES
    lane = lax.broadcasted_iota(jnp.int32, (seq, LANES), 1)
    qchunk = (past + lax.broadcasted_iota(jnp.int32, (2 * seq, 1), 0) % seq) // CHUNK
    kpos_p = lax.broadcasted_iota(jnp.int32, (2 * seq, past), 1)
    kpos_n = lax.broadcasted_iota(jnp.int32, (2 * seq, npad), 1)
    vis_p = kpos_p // CHUNK <= qchunk
    vis_n = ((past + kpos_n) // CHUNK <= qchunk) & (kpos_n < seq)
    pad_rows = jnp.zeros((npad - seq, LANES), BF16)
    for h in range(ATT_HEADS):
        sl = slice(h * LANES, (h + 1) * LANES)
        qh = q_ref[:, sl]
        zero = jnp.zeros_like(qh)
        q2 = jnp.concatenate([jnp.where(lane < ATT_HEAD_DIM, qh, zero), jnp.where(lane >= ATT_HEAD_DIM, qh, zero)], axis=0)
        s_p = jnp.where(vis_p, _dot(q2, kc_ref[0, sl, :].astype(BF16)), NEG_BIG)
        kn = jnp.concatenate([kn_ref[:, sl].astype(BF16), pad_rows], axis=0)
        s_n = lax.dot_general(q2, kn, (((1,), (1,)), ((), ())), preferred_element_type=F32)
        s_n = jnp.where(vis_n, s_n, NEG_BIG)
        m = jnp.maximum(jnp.max(s_p, axis=-1, keepdims=True), jnp.max(s_n, axis=-1, keepdims=True))
        p_p = jnp.exp2(s_p - m)
        p_n = jnp.exp2(s_n - m)
        l = jnp.sum(p_p, axis=-1, keepdims=True) + jnp.sum(p_n, axis=-1, keepdims=True)
        vh = vc_ref[0, pl.ds(h, past, stride=ATT_HEADS), :].astype(BF16)
        vn = jnp.concatenate([vn_ref[:, sl].astype(BF16), pad_rows], axis=0)
        o = (_dot(p_p.astype(BF16), vh) + _dot(p_n.astype(BF16), vn)) / l
        o = o[:seq] - lam * o[seq:]
        o = o * lax.rsqrt(jnp.mean(o * o, axis=-1, keepdims=True) + ATT_NORM_EPS) * (sw_ref[...] * (1.0 - lam_init))
        o_ref[:, sl] = o.astype(BF16)


def _decode_attn(q, k_new, v_new, kt_cache, v_cache, lam_vecs, subln_row, *, nb, seq, past, lam_init):
    row = pl.BlockSpec((seq, ATT_WIDTH), lambda b: (b, 0))
    return pl.pallas_call(
        functools.partial(_decode_attn_kernel, seq=seq, past=past, lam_init=lam_init),
        grid=(nb,),
        in_specs=[row, row, row,
                  pl.BlockSpec((1, ATT_WIDTH, past), lambda b: (b, 0, 0)),
                  pl.BlockSpec((1, past * ATT_HEADS, ATT_V_DIM), lambda b: (b, 0, 0)),
                  _const_spec((4, ATT_HEAD_DIM)), _const_spec((1, ATT_V_DIM))],
        out_specs=row,
        out_shape=jax.ShapeDtypeStruct((nb * seq, ATT_WIDTH), BF16),
        compiler_params=pltpu.CompilerParams(dimension_semantics=("arbitrary",), vmem_limit_bytes=VMEM_LIMIT),
        name="decode_attn",
    )(q, k_new, v_new, kt_cache, v_cache, lam_vecs, subln_row)


def _post_mix_kernel(x_ref, ys_ref, ya_ref, mk_ref, mv_ref, wo_ref, wq_ref, wox_ref,
                     g1_ref, g2_ref, g3_ref, h_ref, *, spt, rows):
    mix = _dot(ys_ref[...], wo_ref[0:SSD_WIDTH, :]) + _dot(ya_ref[...], wo_ref[SSD_WIDTH:, :])
    h = x_ref[...] + _rms(mix, g1_ref[...], NORM_EPS)
    qn = _rms(h, g2_ref[...], NORM_EPS).astype(BF16)
    qx = (_dot(qn, wq_ref[...]) * (math.log2(math.e) / math.sqrt(MEM_HEAD_DIM))).astype(BF16)
    ox_seqs = []
    for s_i in range(spt):
        qs = qx[s_i * rows:(s_i + 1) * rows]
        mk = mk_ref[s_i]
        mv = mv_ref[s_i]
        oxs = []
        for hd in range(MEM_HEADS):
            sl = slice(hd * MEM_HEAD_DIM, (hd + 1) * MEM_HEAD_DIM)
            s = lax.dot_general(qs[:, sl], mk[:, sl], (((1,), (1,)), ((), ())), preferred_element_type=F32)
            p = jnp.exp2(s - jnp.max(s, axis=-1, keepdims=True))
            ox = _dot(p.astype(BF16), mv[:, sl]) * (1.0 / jnp.sum(p, axis=-1, keepdims=True))
            oxs.append(ox.astype(BF16))
        ox_seqs.append(jnp.concatenate(oxs, axis=1))
    ox_all = ox_seqs[0] if spt == 1 else jnp.concatenate(ox_seqs, axis=0)
    o2 = _dot(ox_all, wox_ref[...])
    h_ref[...] = h + _rms(o2, g3_ref[...], NORM_EPS)


def _post_mix(x, ys, ya, mk, mv, w_out, wq, wox, g1, g2, g3, *, seq, tm):
    n = x.shape[0]
    spt = max(1, tm // seq)
    tiles_per_seq = max(1, seq // tm)
    row = lambda c: pl.BlockSpec((tm, c), lambda i: (i, 0))
    mem = pl.BlockSpec((spt, MEM_LEN, D_MODEL), lambda i: (i // tiles_per_seq, 0, 0))
    wspec = _const_spec((D_MODEL, D_MODEL))
    gspec = _const_spec((1, D_MODEL))
    return pl.pallas_call(
        functools.partial(_post_mix_kernel, spt=spt, rows=tm // spt),
        grid=(n // tm,),
        in_specs=[row(D_MODEL), row(SSD_WIDTH), row(ATT_WIDTH), mem, mem, wspec, wspec, wspec, gspec, gspec, gspec],
        out_specs=row(D_MODEL),
        out_shape=jax.ShapeDtypeStruct((n, D_MODEL), F32),
        compiler_params=pltpu.CompilerParams(dimension_semantics=("arbitrary",), vmem_limit_bytes=VMEM_LIMIT),
        name="post_mix",
    )(x, ys, ya, mk, mv, w_out, wq, wox, g1, g2, g3)


def _ffn_kernel(h_ref, wg_ref, wu_ref, wd_ref, g1_ref, g2_ref, o_ref):
    h = h_ref[...]
    hn = _rms(h, g1_ref[...], NORM_EPS).astype(BF16)
    act = (_silu(_dot(hn, wg_ref[...])) * _dot(hn, wu_ref[...])).astype(BF16)
    f = _dot(act, wd_ref[...])
    o_ref[...] = h + _rms(f, g2_ref[...], NORM_EPS)


def _ffn(h, wg, wu, wd, g1, g2, *, tm):
    n = h.shape[0]
    row = pl.BlockSpec((tm, D_MODEL), lambda i: (i, 0))
    return pl.pallas_call(
        _ffn_kernel,
        grid=(n // tm,),
        in_specs=[row, _const_spec((D_MODEL, FFN_HIDDEN)), _const_spec((D_MODEL, FFN_HIDDEN)),
                  _const_spec((FFN_HIDDEN, D_MODEL)), _const_spec((1, D_MODEL)), _const_spec((1, D_MODEL))],
        out_specs=row,
        out_shape=jax.ShapeDtypeStruct((n, D_MODEL), F32),
        compiler_params=pltpu.CompilerParams(dimension_semantics=("arbitrary",), vmem_limit_bytes=VMEM_LIMIT),
        name="ffn",
    )(h, wg, wu, wd, g1, g2)


def _rope_tables(past, seq):
    half = ATT_HEAD_DIM // 2
    inv = jnp.power(ROPE_THETA, -jnp.arange(0, ATT_HEAD_DIM, 2, dtype=F32) / ATT_HEAD_DIM)
    pos = (past + jnp.arange(seq, dtype=jnp.int32)).astype(F32)
    ang = pos[:, None] * inv[None, :]
    cos, sin = jnp.cos(ang), jnp.sin(ang)
    reps = LANES // ATT_HEAD_DIM
    assert half * 2 == ATT_HEAD_DIM
    return jnp.tile(jnp.concatenate([cos, cos], axis=-1), (1, reps)), jnp.tile(jnp.concatenate([-sin, sin], axis=-1), (1, reps))


def _state_to_kernel_layout(s):
    b = s.shape[0]
    s = s.reshape(b, SSD_GROUPS, SSD_REP, SSD_HEAD_DIM, SSD_STATE)
    return s.transpose(0, 1, 4, 2, 3).reshape(b, SSD_GROUPS, SSD_STATE, SSD_REP * SSD_HEAD_DIM)


def _state_from_kernel_layout(s):
    b = s.shape[0]
    s = s.reshape(b, SSD_GROUPS, SSD_STATE, SSD_REP, SSD_HEAD_DIM)
    return s.transpose(0, 1, 3, 4, 2).reshape(b, SSD_HEADS, SSD_HEAD_DIM, SSD_STATE)


def _layer(x, conv_buf, ssm0, kt_past, v_past, mem_kb, mem_vb, lam_init, p, *, tm, tk, tq, ssd_q):
    nb, seq, _ = x.shape
    n = nb * seq
    past = 0 if kt_past is None else kt_past.shape[2]
    xf = x.reshape(n, D_MODEL)
    cos, sin = _rope_tables(past, seq)
    no_history = kt_past is None
    assert tq == tk or not no_history
    cbuf = jnp.pad(conv_buf.astype(F32), ((0, 0), (SUBLANES - (SSD_CONV - 1), 0), (0, 0)))
    gate, u, tail, dt, q, k, v, qt, kb, vt = _in_proj(xf, p["g_pre_mix"], p["w_in"], cos, sin, cbuf, p["conv_w"],
                                                       p["conv_b"], seq=seq, tm=tm, tt=tk if no_history else 0)

    seq_pad = -(-seq // ssd_q) * ssd_q
    if seq_pad != seq:
        pad = lambda a: jnp.pad(a.reshape(nb, seq, -1), ((0, 0), (0, seq_pad - seq), (0, 0))).reshape(nb * seq_pad, -1)
        gate_s, u_s, dt_s = pad(gate), pad(u), pad(dt)
    else:
        gate_s, u_s, dt_s = gate, u, dt
    y_ssd, h_new = _ssd(gate_s, u_s, dt_s, _state_to_kernel_layout(ssm0.astype(F32)),
                        p["dt_bias"], p["a_log"], p["d_skip"], p["ssm_norm_w"], p["e2"],
                        nb=nb, seq=seq_pad, q=ssd_q, valid=None if seq_pad == seq else seq)
    if seq_pad != seq:
        y_ssd = y_ssd.reshape(nb, seq_pad, SSD_WIDTH)[:, :seq].reshape(n, SSD_WIDTH)
    ssm_new = _state_from_kernel_layout(h_new)
    ext_tail = tail.reshape(nb, -1, SUBLANES, SSD_XBC)[:, -1]
    if seq >= SSD_CONV - 1:
        conv_new = ext_tail[:, SUBLANES - (SSD_CONV - 1):]
    else:
        conv_new = jnp.concatenate([conv_buf.astype(F32), ext_tail[:, SUBLANES - seq:]], axis=1)[:, -(SSD_CONV - 1):]

    if no_history:
        kv_len = seq
        y_att = _diff_attn(qt, kb, vt, p["lam_vecs"], p["subln_col"], nb=nb, tq=tq, tk=tk,
                           past=past, kv_len=kv_len, lam_init=lam_init, nbuf=ATT_PIPELINE_BUFFERS)
    else:
        y_att = _decode_attn(q, k, v, kt_past, v_past, p["lam_vecs"], p["subln_col"].reshape(1, ATT_V_DIM),
                             nb=nb, seq=seq, past=past, lam_init=lam_init)

    h = _post_mix(xf, y_ssd, y_att, mem_kb, mem_vb, p["w_out"], p["wq_x"], p["wo_x"],
                  p["g_post_mix"], p["g_pre_x"], p["g_post_x"], seq=seq, tm=tm)
    out = _ffn(h, p["w_gate"], p["w_up"], p["w_down"], p["g_pre_ffn"], p["g_post_ffn"], tm=tm)
    if no_history:
        k_out = k.reshape(nb, ATT_HEADS, 2, ATT_HEAD_DIM, seq).transpose(0, 4, 1, 2, 3)
    else:
        k_out = k.reshape(nb, seq, ATT_HEADS, 2, ATT_HEAD_DIM)
    return (out.reshape(nb, seq, D_MODEL), k_out,
            v.reshape(nb, seq, ATT_HEADS, ATT_V_DIM), ssm_new, conv_new)


def _prep_params(i, w_in, conv_w, conv_b, dt_bias, a_log, d_skip, ssm_norm_w, lam_q1, lam_k1, lam_q2, lam_k2, subln_w,
                 w_out, wq_x, wo_x, g_pre_mix, g_post_mix, g_pre_x, g_post_x, g_pre_ffn, g_post_ffn,
                 w_gate, w_up, w_down):
    w = w_in[i]
    s0 = SSD_WIDTH + SSD_XBC
    s1 = s0 + SSD_HEADS
    w_r = jnp.concatenate([w[:, :s0], w[:, s1:], w[:, s0:s1], jnp.zeros((D_MODEL, DT_PAD - SSD_HEADS), w.dtype)], axis=1)
    head_pad = lambda a: jnp.pad(a[i].astype(F32), (0, DT_PAD - SSD_HEADS)).reshape(1, DT_PAD)
    e = (jnp.arange(DT_PAD)[:, None] == (jnp.arange(SSD_WIDTH)[None, :] // SSD_HEAD_DIM)).astype(BF16)
    row = lambda a: a[i].astype(F32).reshape(1, -1)
    w_out_b, wq_b, wo_b = _cast_bf16([w_out[i], wq_x[i], wo_x[i]], steps=4)
    w_gate_b, w_up_b, w_down_b = _cast_bf16([w_gate[i], w_up[i], w_down[i]], steps=8)
    return {
        "w_in": w_r.astype(BF16), "conv_w": conv_w[i].astype(F32), "conv_b": row(conv_b),
        "dt_bias": head_pad(dt_bias), "a_log": head_pad(a_log), "d_skip": head_pad(d_skip),
        "ssm_norm_w": row(ssm_norm_w), "e2": jnp.concatenate([e, e], axis=0),
        "lam_vecs": jnp.stack([lam_q1[i], lam_k1[i], lam_q2[i], lam_k2[i]]).astype(F32), "subln_col": subln_w[i].astype(F32).reshape(ATT_V_DIM, 1),
        "w_out": w_out_b, "wq_x": wq_b, "wo_x": wo_b,
        "g_pre_mix": row(g_pre_mix), "g_post_mix": row(g_post_mix), "g_pre_x": row(g_pre_x), "g_post_x": row(g_post_x),
        "g_pre_ffn": row(g_pre_ffn), "g_post_ffn": row(g_post_ffn),
        "w_gate": w_gate_b, "w_up": w_up_b, "w_down": w_down_b,
    }


def kernel(x_prompt, x_sample, cache_attn_k, cache_attn_v, state_ssm, state_conv, cache_mem_k, cache_mem_v, mem_prompt, w_in, conv_w, conv_b, dt_bias, a_log, d_skip, ssm_norm_w, lam_q1, lam_k1, lam_q2, lam_k2, subln_w, w_out, g_mem, wq_x, wk_x, wv_x, wo_x, g_pre_mix, g_post_mix, g_pre_x, g_post_x, g_pre_ffn, g_post_ffn, w_gate, w_up, w_down):
    depth = w_in.shape[0]
    bp, sp, _ = x_prompt.shape
    bs, ss, _ = x_sample.shape
    hp, hs = x_prompt, x_sample
    outs = [[] for _ in range(10)]
    for i in range(depth):
        lam_init = 0.8 - 0.6 * math.exp(-0.3 * i)
        p = _prep_params(i, w_in, conv_w, conv_b, dt_bias, a_log, d_skip, ssm_norm_w, lam_q1, lam_k1, lam_q2, lam_k2,
                         subln_w, w_out, wq_x, wo_x, g_pre_mix, g_post_mix, g_pre_x, g_post_x, g_pre_ffn, g_post_ffn,
                         w_gate, w_up, w_down)
        mk, mv, mkb, mvb = _mem_kv(mem_prompt.reshape(bp * MEM_LEN, D_MODEL), g_mem[i].reshape(1, D_MODEL),
                                   wk_x[i], wv_x[i])
        hp, k_new, v_new, ssm_new, conv_new = _layer(
            hp, jnp.zeros((bp, SSD_CONV - 1, SSD_XBC), F32), jnp.zeros((bp, SSD_HEADS, SSD_HEAD_DIM, SSD_STATE), F32),
            None, None, mkb.reshape(bp, MEM_LEN, D_MODEL), mvb.reshape(bp, MEM_LEN, D_MODEL), lam_init, p,
            tm=512, tk=512, tq=512, ssd_q=256)
        for lst, val in zip(outs[:6], (k_new, v_new, ssm_new, conv_new,
                                       mk.reshape(bp, MEM_LEN, MEM_HEADS, MEM_HEAD_DIM),
                                       mv.reshape(bp, MEM_LEN, MEM_HEADS, MEM_HEAD_DIM))):
            lst.append(val)
        past = cache_attn_k.shape[2]
        hs, k_new, v_new, ssm_new, conv_new = _layer(
            hs, state_conv[i], state_ssm[i],
            cache_attn_k[i].transpose(0, 2, 3, 4, 1).reshape(bs, ATT_WIDTH, past),
            cache_attn_v[i].reshape(bs, past * ATT_HEADS, ATT_V_DIM),
            cache_mem_k[i].reshape(bs, MEM_LEN, D_MODEL).astype(BF16),
            cache_mem_v[i].reshape(bs, MEM_LEN, D_MODEL).astype(BF16), lam_init, p,
            tm=bs * ss, tk=0, tq=0, ssd_q=128)
        for lst, val in zip(outs[6:], (k_new, v_new, ssm_new, conv_new)):
            lst.append(val)
    return (hp, hs) + tuple(jnp.stack(o) for o in outs)
```

```python
import functools
import math

import jax
import jax.numpy as jnp
from jax import lax
from jax.experimental import pallas as pl
from jax.experimental.pallas import tpu as pltpu

D_MODEL = 1024
CHUNK = 64
SSD_WIDTH = 512
SSD_HEAD_DIM = 64
SSD_HEADS = 8
SSD_GROUPS = 2
SSD_REP = 4
SSD_STATE = 128
SSD_CONV = 4
SSD_XBC = 1024
SSD_NORM_GROUP = 256
SSD_NORM_EPS = 1e-5
ATT_WIDTH = 512
ATT_HEAD_DIM = 64
ATT_HEADS = 4
ATT_V_DIM = 128
ATT_NORM_EPS = 1e-5
ROPE_THETA = 10000.0
MEM_LEN = 256
MEM_HEADS = 4
MEM_HEAD_DIM = 256
FFN_HIDDEN = 2816
NORM_EPS = 1e-6
LANES = 128
SUBLANES = 8
DT_PAD = LANES
IN_COLS_PADDED = SSD_WIDTH + SSD_XBC + 3 * ATT_WIDTH + DT_PAD
VMEM_LIMIT = 56 * 1024 * 1024
NEG_BIG = -1e30
MXU_TILE = 256
BF16_SUBLANES = 16
ATT_VT_ROWS = ATT_V_DIM + BF16_SUBLANES
ATT_SOFTMAX_ROWS = 128
ATT_PIPELINE_BUFFERS = 2
Q_SCALE = math.log2(math.e) / math.sqrt(ATT_HEAD_DIM)

F32 = jnp.float32
BF16 = jnp.bfloat16


def _const_spec(shape):
    return pl.BlockSpec(shape, lambda *_: (0,) * len(shape), pipeline_mode=pl.Buffered(1))


def _rms(x, g, eps):
    return x * lax.rsqrt(jnp.mean(x * x, axis=-1, keepdims=True) + eps) * g


def _silu(x):
    h = 0.5 * x
    return h + h * jnp.tanh(h)


def _dot(a, b):
    return jnp.dot(a, b, preferred_element_type=F32)


def _cast_kernel(*refs):
    n = len(refs) // 2
    for x_ref, o_ref in zip(refs[:n], refs[n:]):
        o_ref[...] = x_ref[...].astype(BF16)


def _cast_bf16(arrays, steps):
    specs = [pl.BlockSpec((a.shape[0] // steps, a.shape[1]), lambda i: (i, 0)) for a in arrays]
    assert all(a.shape[0] % (steps * BF16_SUBLANES) == 0 for a in arrays)
    return pl.pallas_call(
        _cast_kernel,
        grid=(steps,),
        in_specs=specs,
        out_specs=specs,
        out_shape=[jax.ShapeDtypeStruct(a.shape, BF16) for a in arrays],
        compiler_params=pltpu.CompilerParams(dimension_semantics=("arbitrary",), vmem_limit_bytes=VMEM_LIMIT),
        name="cast_bf16",
    )(*arrays)


def _mem_kv_kernel(mem_ref, g_ref, wk_ref, wv_ref, mk_ref, mv_ref, mkb_ref, mvb_ref):
    mn = _rms(mem_ref[...], g_ref[...], NORM_EPS).astype(BF16)
    mk = _dot(mn, wk_ref[...].astype(BF16))
    mv = _dot(mn, wv_ref[...].astype(BF16))
    mk_ref[...] = mk
    mv_ref[...] = mv
    mkb_ref[...] = mk.astype(BF16)
    mvb_ref[...] = mv.astype(BF16)


def _mem_kv(mem, g_mem, wk, wv):
    n = mem.shape[0]
    tm = MEM_LEN
    row = pl.BlockSpec((tm, D_MODEL), lambda i: (i, 0))
    return pl.pallas_call(
        _mem_kv_kernel,
        grid=(n // tm,),
        in_specs=[row, _const_spec((1, D_MODEL)), _const_spec((D_MODEL, D_MODEL)), _const_spec((D_MODEL, D_MODEL))],
        out_specs=[row, row, row, row],
        out_shape=[jax.ShapeDtypeStruct((n, D_MODEL), F32), jax.ShapeDtypeStruct((n, D_MODEL), F32),
                   jax.ShapeDtypeStruct((n, D_MODEL), BF16), jax.ShapeDtypeStruct((n, D_MODEL), BF16)],
        compiler_params=pltpu.CompilerParams(dimension_semantics=("arbitrary",), vmem_limit_bytes=VMEM_LIMIT),
        name="mem_kv",
    )(mem, g_mem, wk, wv)


def _in_proj_kernel(x_ref, g_ref, w_ref, cos_ref, sin_ref, cbuf_ref, cw_ref, cb_ref,
                    z_ref, u_ref, tail_ref, dt_ref, q_ref, k_ref, v_ref, qt_ref, kb_ref, vt_ref, acc_ref, xtail,
                    *, tm, tt, spt, tiles_per_seq):
    step = pl.program_id(0)
    tile = step - 1

    @pl.when(step == 0)
    def _():
        acc_ref[...] = jnp.zeros_like(acc_ref)
        xtail[...] = jnp.zeros_like(xtail)

    c_xbc, c_q = SSD_WIDTH, SSD_WIDTH + SSD_XBC
    c_k, c_v, c_dt = c_q + ATT_WIDTH, c_q + 2 * ATT_WIDTH, c_q + 3 * ATT_WIDTH
    z_ref[...] = _silu(acc_ref[:, :SSD_WIDTH]).astype(BF16)
    rows = tm // spt
    row8 = lax.broadcasted_iota(jnp.int32, (SUBLANES, LANES), 0)
    first_tile_of_seq = tile % tiles_per_seq == 0
    for s_i in range(spt):
        r0 = s_i * rows
        tail_ref[s_i] = acc_ref[r0 + rows - SUBLANES:r0 + rows, c_xbc:c_q]
        for cb in range(SSD_XBC // LANES):
            sl = slice(cb * LANES, (cb + 1) * LANES)
            piece = acc_ref[r0:r0 + rows, c_xbc + cb * LANES:c_xbc + (cb + 1) * LANES]
            if spt > 1:
                prev = cbuf_ref[s_i, :, sl]
            else:
                prev = jnp.where(first_tile_of_seq, cbuf_ref[0, :, sl], xtail[:, sl])
            conv_top = cb_ref[:, sl] + piece[0:SUBLANES] * cw_ref[SSD_CONV - 1:SSD_CONV, sl]
            conv_rest = cb_ref[:, sl] + piece[SUBLANES:] * cw_ref[SSD_CONV - 1:SSD_CONV, sl]
            for d in range(1, SSD_CONV):
                sh = pltpu.roll(piece, d, 0)
                tap = cw_ref[SSD_CONV - 1 - d:SSD_CONV - d, sl]
                conv_top = conv_top + jnp.where(row8 < d, pltpu.roll(prev, d, 0), sh[0:SUBLANES]) * tap
                conv_rest = conv_rest + sh[SUBLANES:] * tap
            u_ref[r0:r0 + rows, sl] = _silu(jnp.concatenate([conv_top, conv_rest], axis=0)).astype(BF16)
    if spt == 1:
        xtail[...] = acc_ref[tm - SUBLANES:, c_xbc:c_q]
    dt_ref[...] = acc_ref[:, c_dt:c_dt + DT_PAD]
    if tt:
        for j in range(ATT_HEADS):
            v_ref[pl.ds(j, tm, stride=ATT_HEADS), :] = acc_ref[:, c_v + j * ATT_V_DIM:c_v + (j + 1) * ATT_V_DIM]
    else:
        v_ref[...] = acc_ref[:, c_v:c_v + ATT_WIDTH]

    cos = cos_ref[...]
    sin = sin_ref[...]
    first_half = (lax.broadcasted_iota(jnp.int32, (tm, LANES), 1) % ATT_HEAD_DIM) < (ATT_HEAD_DIM // 2)

    def rope(t):
        swapped = jnp.where(first_half, pltpu.roll(t, LANES - ATT_HEAD_DIM // 2, 1),
                            pltpu.roll(t, ATT_HEAD_DIM // 2, 1))
        return t * cos + swapped * sin

    for j in range(ATT_WIDTH // LANES):
        sl = slice(j * LANES, (j + 1) * LANES)
        qr = rope(acc_ref[:, c_q + j * LANES:c_q + (j + 1) * LANES]) * Q_SCALE
        kr = rope(acc_ref[:, c_k + j * LANES:c_k + (j + 1) * LANES])
        q_ref[:, sl] = qr.astype(BF16)
        if tt:
            k_ref[0, sl, :] = kr.T
            kb_ref[:, sl] = kr.astype(BF16)
            for c in range(tm // tt):
                qt_ref[c, sl, :] = qr[c * tt:(c + 1) * tt, :].T.astype(BF16)
                vt_ref[c, j * ATT_VT_ROWS:j * ATT_VT_ROWS + LANES, :] = acc_ref[
                    c * tt:(c + 1) * tt, c_v + j * LANES:c_v + (j + 1) * LANES].T.astype(BF16)
                vt_ref[c, j * ATT_VT_ROWS + LANES:(j + 1) * ATT_VT_ROWS, :] = jnp.ones((ATT_VT_ROWS - LANES, tt), BF16)
        else:
            k_ref[:, sl] = kr

    hn = _rms(x_ref[...], g_ref[...], NORM_EPS).astype(BF16)
    acc_ref[...] = _dot(hn, w_ref[...])


def _in_proj(x, g, w, cos, sin, cbuf, cw, cb, *, seq, tm, tt):
    n = x.shape[0]
    nt = n // tm
    spt = max(1, tm // seq)
    tiles_per_seq = max(1, seq // tm)
    if spt > 1:
        cos, sin = jnp.tile(cos, (spt, 1)), jnp.tile(sin, (spt, 1))
    done = lambda i: jnp.maximum(i - 1, 0)
    row = lambda c: pl.BlockSpec((tm, c), lambda i: (done(i), 0))
    tab = pl.BlockSpec((tm, LANES), lambda i: (done(i) % tiles_per_seq, 0))
    if tt:
        k_spec = pl.BlockSpec((1, ATT_WIDTH, tm), lambda i: (done(i) // tiles_per_seq, 0, done(i) % tiles_per_seq))
        k_shape = jax.ShapeDtypeStruct((n // seq, ATT_WIDTH, seq), F32)
        v_spec = pl.BlockSpec((tm * ATT_HEADS, ATT_V_DIM), lambda i: (done(i), 0))
        v_shape = jax.ShapeDtypeStruct((n * ATT_HEADS, ATT_V_DIM), F32)
    else:
        k_spec, k_shape = row(ATT_WIDTH), jax.ShapeDtypeStruct((n, ATT_WIDTH), F32)
        v_spec, v_shape = row(ATT_WIDTH), jax.ShapeDtypeStruct((n, ATT_WIDTH), F32)
    out_specs = [row(SSD_WIDTH), row(SSD_XBC), pl.BlockSpec((spt, SUBLANES, SSD_XBC), lambda i: (done(i), 0, 0)),
                 row(DT_PAD), row(ATT_WIDTH), k_spec, v_spec]
    out_shape = [jax.ShapeDtypeStruct((n, SSD_WIDTH), BF16), jax.ShapeDtypeStruct((n, SSD_XBC), BF16),
                 jax.ShapeDtypeStruct((nt * spt, SUBLANES, SSD_XBC), F32), jax.ShapeDtypeStruct((n, DT_PAD), F32),
                 jax.ShapeDtypeStruct((n, ATT_WIDTH), BF16), k_shape, v_shape]
    if tt:
        tr = lambda r: pl.BlockSpec((tm // tt, r, tt), lambda i: (done(i), 0, 0))
        vt_rows = ATT_HEADS * ATT_VT_ROWS
        out_specs += [tr(ATT_WIDTH), row(ATT_WIDTH), tr(vt_rows)]
        out_shape += [jax.ShapeDtypeStruct((n // tt, ATT_WIDTH, tt), BF16), jax.ShapeDtypeStruct((n, ATT_WIDTH), BF16),
                      jax.ShapeDtypeStruct((n // tt, vt_rows, tt), BF16)]

    def body(*refs):
        refs, scratch = refs[:-2], refs[-2:]
        refs = refs + (None,) * (18 - len(refs))
        _in_proj_kernel(*refs, *scratch, tm=tm, tt=tt, spt=spt, tiles_per_seq=tiles_per_seq)

    outs = pl.pallas_call(
        body,
        grid=(nt + 1,),
        in_specs=[pl.BlockSpec((tm, D_MODEL), lambda i: (jnp.minimum(i, nt - 1), 0)),
                  _const_spec((1, D_MODEL)), _const_spec((D_MODEL, IN_COLS_PADDED)), tab, tab,
                  pl.BlockSpec((spt, SUBLANES, SSD_XBC), lambda i: (done(i) // tiles_per_seq, 0, 0)),
                  _const_spec((SSD_CONV, SSD_XBC)), _const_spec((1, SSD_XBC))],
        out_specs=out_specs,
        out_shape=out_shape,
        scratch_shapes=[pltpu.VMEM((tm, IN_COLS_PADDED), F32), pltpu.VMEM((SUBLANES, SSD_XBC), F32)],
        compiler_params=pltpu.CompilerParams(dimension_semantics=("arbitrary",), vmem_limit_bytes=VMEM_LIMIT),
        name="in_proj",
    )(x, g, w, cos, sin, cbuf, cw, cb)
    return list(outs) + [None] * (10 - len(outs))


def _expand_heads(x, e2):
    hi = x.astype(BF16)
    lo = (x - hi.astype(F32)).astype(BF16)
    return _dot(jnp.concatenate([hi, lo], axis=1), e2)


def _ssd_kernel(gate_ref, u_ref, dt_ref, h0_ref, dtb_ref, alog_ref, dsk_ref, nw_ref, e2_ref,
                y_ref, hout_ref, state, *, q, valid):
    c = pl.program_id(1)

    @pl.when(c == 0)
    def _():
        state[...] = h0_ref[0]

    xs = u_ref[:, :SSD_WIDTH].astype(F32)
    bm = u_ref[:, SSD_WIDTH:SSD_WIDTH + SSD_GROUPS * SSD_STATE]
    cmb = u_ref[:, SSD_WIDTH + SSD_GROUPS * SSD_STATE:]

    dtr = dt_ref[...] + dtb_ref[...]
    dt = jnp.maximum(dtr, 0.0) + jnp.log(1.0 + jnp.exp(-jnp.abs(dtr)))
    if valid is not None:
        row = lax.broadcasted_iota(jnp.int32, (q, DT_PAD), 0) + c * q
        dt = jnp.where(row < valid, dt, 0.0)
    a = -jnp.exp(alog_ref[...])
    ad = dt * a
    ri = lax.broadcasted_iota(jnp.int32, (q, q), 0)
    ci = lax.broadcasted_iota(jnp.int32, (q, q), 1)
    tril = ri >= ci
    tril_b = jnp.where(tril, 1.0, 0.0).astype(BF16)
    ad_hi = ad.astype(BF16)
    ad_r = ad - ad_hi.astype(F32)
    ad_mid = ad_r.astype(BF16)
    ad_lo = (ad_r - ad_mid.astype(F32)).astype(BF16)
    acum = _dot(tril_b, ad_hi) + _dot(tril_b, ad_mid) + _dot(tril_b, ad_lo)
    acum_t = acum.T
    tot = acum[q - 1:q, :]
    e2 = e2_ref[...]
    expanded = _expand_heads(jnp.concatenate([dt, dt * jnp.exp(tot - acum), jnp.exp(acum)], axis=0), e2)
    dtx = expanded[0:q]
    ddx = expanded[q:2 * q]
    eax = expanded[2 * q:3 * q]
    dsk = _expand_heads(jnp.broadcast_to(dsk_ref[...], (SUBLANES, DT_PAD)), e2)[0:1]

    xdt = (xs * dtx).astype(BF16)
    xdtd = (xs * ddx).astype(BF16)
    bm_t = bm.astype(F32).T.astype(BF16)
    gw = SSD_REP * SSD_HEAD_DIM
    stripe = lax.broadcasted_iota(jnp.int32, (q, gw), 1) // SSD_HEAD_DIM
    ys = []
    for g in range(SSD_GROUPS):
        cm_g = cmb[:, g * SSD_STATE:(g + 1) * SSD_STATE]
        bt_g = bm_t[g * SSD_STATE:(g + 1) * SSD_STATE, :]
        cbm = _dot(cm_g, bt_g)
        ms = []
        for r in range(SSD_REP):
            h = g * SSD_REP + r
            diff = acum[:, h:h + 1] - acum_t[h:h + 1, :]
            ms.append((cbm * jnp.exp(jnp.where(tril, diff, -jnp.inf))).astype(BF16))
        ydf = _dot(jnp.concatenate(ms, axis=0), xdt[:, g * gw:(g + 1) * gw])
        yd = ydf[0:q]
        for r in range(1, SSD_REP):
            yd = jnp.where(stripe == r, ydf[r * q:(r + 1) * q], yd)
        st = state[g]
        y_off = _dot(cm_g, st.astype(BF16)) * eax[:, g * gw:(g + 1) * gw]
        state[g] = st * eax[q - 1:q, g * gw:(g + 1) * gw] + _dot(bt_g, xdtd[:, g * gw:(g + 1) * gw])
        ys.append(yd + y_off)
    y = jnp.concatenate(ys, axis=1) + dsk * xs
    y = y * gate_ref[...].astype(F32)
    outs = []
    for g in range(SSD_WIDTH // SSD_NORM_GROUP):
        yg = y[:, g * SSD_NORM_GROUP:(g + 1) * SSD_NORM_GROUP]
        outs.append(yg * lax.rsqrt(jnp.mean(yg * yg, axis=-1, keepdims=True) + SSD_NORM_EPS))
    y_ref[...] = (jnp.concatenate(outs, axis=1) * nw_ref[...]).astype(BF16)

    @pl.when(c == pl.num_programs(1) - 1)
    def _():
        hout_ref[0] = state[...]


def _ssd(gate, u, dt, h0, dtb, alog, dsk, nw, e2, *, nb, seq, q, valid):
    nc = seq // q
    row = lambda c: pl.BlockSpec((q, c), lambda b, i: (b * nc + i, 0))
    gw = SSD_REP * SSD_HEAD_DIM
    st_spec = pl.BlockSpec((1, SSD_GROUPS, SSD_STATE, gw), lambda b, i: (b, 0, 0, 0))
    return pl.pallas_call(
        functools.partial(_ssd_kernel, q=q, valid=valid),
        grid=(nb, nc),
        in_specs=[row(SSD_WIDTH), row(SSD_XBC), row(DT_PAD), st_spec,
                  _const_spec((1, DT_PAD)), _const_spec((1, DT_PAD)), _const_spec((1, DT_PAD)),
                  _const_spec((1, SSD_WIDTH)), _const_spec((2 * DT_PAD, SSD_WIDTH))],
        out_specs=[row(SSD_WIDTH), st_spec],
        out_shape=[jax.ShapeDtypeStruct((nb * seq, SSD_WIDTH), BF16),
                   jax.ShapeDtypeStruct((nb, SSD_GROUPS, SSD_STATE, gw), F32)],
        scratch_shapes=[pltpu.VMEM((SSD_GROUPS, SSD_STATE, gw), F32)],
        compiler_params=pltpu.CompilerParams(dimension_semantics=("arbitrary", "arbitrary"),
                                             vmem_limit_bytes=VMEM_LIMIT),
        name="ssd",
    )(gate, u, dt, h0, dtb, alog, dsk, nw, e2)


def _column_max(x):
    while x.shape[0] > SUBLANES and x.shape[0] % (2 * SUBLANES) == 0:
        half = x.shape[0] // 2
        x = jnp.maximum(x[:half], x[half:])
    return jnp.max(x, axis=0, keepdims=True)


def _attn_tile_counts(i, *, tq, tk, past, kv_len, minimum=min):
    q_lo = past + i * tq
    q_hi = q_lo + tq - 1
    lim_lo = minimum((q_lo // CHUNK + 1) * CHUNK, kv_len)
    lim_hi = minimum((q_hi // CHUNK + 1) * CHUNK, kv_len)
    return lim_lo // tk, (lim_hi + tk - 1) // tk


def _diff_attn_kernel(qt_ref, qt_next_ref, k_ref, vt_ref, lam_ref, sw_ref, o_ref,
                      q2t_ref, s_ref, p_ref, m_ref, alpha_ref, acc_ref, *, tq, tk, past, kv_len, lam_init, nbuf, nkt):
    i = pl.program_id(2)
    w = 2 * tq
    row = lax.broadcasted_iota(jnp.int32, (ATT_V_DIM, tq), 0)
    n_full, _ = _attn_tile_counts(i, tq=tq, tk=tk, past=past, kv_len=kv_len, minimum=jnp.minimum)
    n_visits = n_full + 1

    def scores(j):
        kt = k_ref[pl.ds(pl.multiple_of(j * tk, tk), tk), :]
        return _dot(kt, q2t_ref[...])

    def visited_tile(v):
        return jnp.where(v == 0, n_full, jnp.maximum(v - 1, 0))

    ncb = w // LANES
    pw = p_ref.shape[-1]

    def store_scores(buf, s):
        for cb in range(ncb):
            s_ref[buf, cb] = s[:, cb * LANES:(cb + 1) * LANES]

    def stage_a(v, buf):
        store_scores(buf, scores(jnp.minimum(v - 1, n_full)))

    rows = min(tk, ATT_SOFTMAX_ROWS)

    def stage_b(buf):
        for cb in range(ncb):
            sl = slice(cb * LANES, (cb + 1) * LANES)
            m_old = m_ref[:, sl]
            m_new = m_old
            for r0 in range(0, tk, rows):
                m_new = jnp.maximum(m_new, _column_max(s_ref[buf, cb, r0:r0 + rows, :]))
            m_ref[:, sl] = m_new
            alpha_ref[buf, :, sl] = jnp.exp2(m_old - m_new)
            pl0 = (cb * LANES) % pw
            for r0 in range(0, tk, rows):
                p_ref[buf, cb * LANES // pw, r0:r0 + rows, pl0:pl0 + LANES] = jnp.exp2(
                    (s_ref[buf, cb, r0:r0 + rows, :] - m_new).astype(BF16))

    def stage_c(v, buf):
        vt = vt_ref[visited_tile(v)]
        for pb in range(w // pw):
            sl = slice(pb * pw, (pb + 1) * pw)
            acc_ref[:, sl] = alpha_ref[buf, :, sl] * acc_ref[:, sl] + _dot(vt, p_ref[buf, pb])

    def first_visit(qt, tile):
        zero = jnp.zeros_like(qt)
        q2t_ref[:, 0:tq] = jnp.where(row < ATT_HEAD_DIM, qt, zero)
        q2t_ref[:, tq:w] = jnp.where(row >= ATT_HEAD_DIM, qt, zero)
        nf, _ = _attn_tile_counts(tile, tq=tq, tk=tk, past=past, kv_len=kv_len, minimum=jnp.minimum)
        nf = jnp.minimum(nf, nkt - 1)
        qchunk = (past + tile * tq + lax.broadcasted_iota(jnp.int32, (1, w), 1) % tq) // CHUNK
        s_part = scores(nf)
        s_masked = []
        for kb in range(tk // CHUNK):
            k0 = nf * tk + kb * CHUNK
            kchunk = jnp.where(k0 < kv_len, k0 // CHUNK, jnp.iinfo(jnp.int32).max)
            s_masked.append(jnp.where(kchunk <= qchunk, s_part[kb * CHUNK:(kb + 1) * CHUNK], NEG_BIG))
        store_scores(0, jnp.concatenate(s_masked, axis=0))

    @pl.when(i == 0)
    def _():
        first_visit(qt_ref[0], i)

    m_ref[...] = jnp.full_like(m_ref, NEG_BIG)
    acc_ref[...] = jnp.zeros_like(acc_ref)
    p_ref[nbuf - 1] = jnp.zeros(p_ref.shape[1:], BF16)
    alpha_ref[nbuf - 1] = jnp.ones(alpha_ref.shape[1:], F32)

    def visits(v0, count, prefetch_last):
        for r in range(count):
            prefetch = r + 1 < count or prefetch_last
            if prefetch and nbuf > 1:
                stage_a(v0 + r + 1, (r + 1) % nbuf)
            stage_c(v0 + r - 1, (r - 1) % nbuf)
            stage_b(r)
            if prefetch and nbuf == 1:
                stage_a(v0 + r + 1, (r + 1) % nbuf)

    def trip(u, carry):
        visits(nbuf * u, nbuf, True)
        return carry

    lax.fori_loop(0, n_visits // nbuf, trip, 0)
    for rem in range(nbuf):

        @pl.when(n_visits % nbuf == rem)
        def _():
            visits(n_visits - rem, rem, False)
            stage_c(n_visits - 1, (rem - 1) % nbuf)

    first_visit(qt_next_ref[0], i + 1)

    lam_v = lam_ref[...]
    lam = (jnp.exp(jnp.sum(lam_v[0:1] * lam_v[1:2], axis=-1, keepdims=True))
           - jnp.exp(jnp.sum(lam_v[2:3] * lam_v[3:4], axis=-1, keepdims=True)) + lam_init)
    acc = acc_ref[...]
    o = acc[:ATT_V_DIM] * (1.0 / acc[ATT_V_DIM:ATT_V_DIM + 1])
    o = o[:, :tq] - lam * o[:, tq:]
    o = o * lax.rsqrt(jnp.mean(o * o, axis=0, keepdims=True) + ATT_NORM_EPS) * (sw_ref[...] * (1.0 - lam_init))
    o_ref[...] = o.T.astype(BF16)


def _diff_attn(qt, k, vt, lam_vecs, subln_col, *, nb, tq, tk, past, kv_len, lam_init, nbuf):
    nq = qt.shape[0] // nb
    nkt = vt.shape[0] // nb
    assert tk % CHUNK == 0 and kv_len % CHUNK == 0
    for i in range(nq):
        n_full, n_end = _attn_tile_counts(i, tq=tq, tk=tk, past=past, kv_len=kv_len)
        assert n_end - n_full == 1 and n_end <= nkt, (i, n_full, n_end)
    w = 2 * tq
    return pl.pallas_call(
        functools.partial(_diff_attn_kernel, tq=tq, tk=tk, past=past, kv_len=kv_len, lam_init=lam_init, nbuf=nbuf,
                          nkt=nkt),
        grid=(nb, ATT_HEADS, nq),
        in_specs=[pl.BlockSpec((1, ATT_V_DIM, tq), lambda b, h, i: (b * nq + i, h, 0)),
                  pl.BlockSpec((1, ATT_V_DIM, tq), lambda b, h, i: (b * nq + jnp.minimum(i + 1, nq - 1), h, 0)),
                  pl.BlockSpec((nkt * tk, ATT_V_DIM), lambda b, h, i: (b, h)),
                  pl.BlockSpec((nkt, ATT_VT_ROWS, tk), lambda b, h, i: (b, h, 0)),
                  _const_spec((4, ATT_HEAD_DIM)), _const_spec((ATT_V_DIM, 1))],
        out_specs=pl.BlockSpec((tq, ATT_V_DIM), lambda b, h, i: (b * nq + i, h)),
        out_shape=jax.ShapeDtypeStruct((nb * nq * tq, ATT_WIDTH), BF16),
        scratch_shapes=[pltpu.VMEM((ATT_V_DIM, w), BF16), pltpu.VMEM((nbuf, w // LANES, tk, LANES), F32),
                        pltpu.VMEM((nbuf, w // MXU_TILE, tk, MXU_TILE), BF16), pltpu.VMEM((1, w), F32),
                        pltpu.VMEM((nbuf, 1, w), F32),
                        pltpu.VMEM((ATT_VT_ROWS, w), F32)],
        compiler_params=pltpu.CompilerParams(dimension_semantics=("arbitrary", "arbitrary", "arbitrary"),
                                             vmem_limit_bytes=VMEM_LIMIT),
        name="diff_attn",
    )(qt, qt, k, vt, lam_vecs, subln_col)


def _decode_attn_kernel(q_ref, kn_ref, vn_ref, kc_ref, vc_ref, lam_ref, sw_ref, o_ref, *, seq, past, lam_init):
    lam_v = lam_ref[...]
    lam = (jnp.exp(jnp.sum(lam_v[0:1] * lam_v[1:2], axis=-1, keepdims=True))
           - jnp.exp(jnp.sum(lam_v[2:3] * lam_v[3:4], axis=-1, keepdims=True)) + lam_init)
    npad = LANES
    lane = lax.broadcasted_iota(jnp.int32, (seq, LANES), 1)
    qchunk = (past + lax.broadcasted_iota(jnp.int32, (2 * seq, 1), 0) % seq) // CHUNK
    kpos_p = lax.broadcasted_iota(jnp.int32, (2 * seq, past), 1)
    kpos_n = lax.broadcasted_iota(jnp.int32, (2 * seq, npad), 1)
    vis_p = kpos_p // CHUNK <= qchunk
    vis_n = ((past + kpos_n) // CHUNK <= qchunk) & (kpos_n < seq)
    pad_rows = jnp.zeros((npad - seq, LANES), BF16)
    for h in range(ATT_HEADS):
        sl = slice(h * LANES, (h + 1) * LANES)
        qh = q_ref[:, sl]
        zero = jnp.zeros_like(qh)
        q2 = jnp.concatenate([jnp.where(lane < ATT_HEAD_DIM, qh, zero), jnp.where(lane >= ATT_HEAD_DIM, qh, zero)], axis=0)
        s_p = jnp.where(vis_p, _dot(q2, kc_ref[0, sl, :].astype(BF16)), NEG_BIG)
        kn = jnp.concatenate([kn_ref[:, sl].astype(BF16), pad_rows], axis=0)
        s_n = lax.dot_general(q2, kn, (((1,), (1,)), ((), ())), preferred_element_type=F32)
        s_n = jnp.where(vis_n, s_n, NEG_BIG)
        m = jnp.maximum(jnp.max(s_p, axis=-1, keepdims=True), jnp.max(s_n, axis=-1, keepdims=True))
        p_p = jnp.exp2(s_p - m)
        p_n = jnp.exp2(s_n - m)
        l = jnp.sum(p_p, axis=-1, keepdims=True) + jnp.sum(p_n, axis=-1, keepdims=True)
        vh = vc_ref[0, pl.ds(h, past, stride=ATT_HEADS), :].astype(BF16)
        vn = jnp.concatenate([vn_ref[:, sl].astype(BF16), pad_rows], axis=0)
        o = (_dot(p_p.astype(BF16), vh) + _dot(p_n.astype(BF16), vn)) / l
        o = o[:seq] - lam * o[seq:]
        o = o * lax.rsqrt(jnp.mean(o * o, axis=-1, keepdims=True) + ATT_NORM_EPS) * (sw_ref[...] * (1.0 - lam_init))
        o_ref[:, sl] = o.astype(BF16)


def _decode_attn(q, k_new, v_new, kt_cache, v_cache, lam_vecs, subln_row, *, nb, seq, past, lam_init):
    row = pl.BlockSpec((seq, ATT_WIDTH), lambda b: (b, 0))
    return pl.pallas_call(
        functools.partial(_decode_attn_kernel, seq=seq, past=past, lam_init=lam_init),
        grid=(nb,),
        in_specs=[row, row, row,
                  pl.BlockSpec((1, ATT_WIDTH, past), lambda b: (b, 0, 0)),
                  pl.BlockSpec((1, past * ATT_HEADS, ATT_V_DIM), lambda b: (b, 0, 0)),
                  _const_spec((4, ATT_HEAD_DIM)), _const_spec((1, ATT_V_DIM))],
        out_specs=row,
        out_shape=jax.ShapeDtypeStruct((nb * seq, ATT_WIDTH), BF16),
        compiler_params=pltpu.CompilerParams(dimension_semantics=("arbitrary",), vmem_limit_bytes=VMEM_LIMIT),
        name="decode_attn",
    )(q, k_new, v_new, kt_cache, v_cache, lam_vecs, subln_row)


def _post_mix_kernel(x_ref, ys_ref, ya_ref, mk_ref, mv_ref, wo_ref, wq_ref, wox_ref,
                     g1_ref, g2_ref, g3_ref, h_ref, *, spt, rows, cached_mem):
    lane_blocks = MEM_HEAD_DIM // LANES

    def mem_head(ref, s_i, hd):
        if not cached_mem:
            return ref[s_i, :, hd * MEM_HEAD_DIM:(hd + 1) * MEM_HEAD_DIM]
        parts = [ref[s_i, pl.ds(cb * MEM_HEADS + hd, MEM_LEN, stride=lane_blocks * MEM_HEADS), :].astype(BF16)
                 for cb in range(lane_blocks)]
        return jnp.concatenate(parts, axis=1)

    mix = _dot(ys_ref[...], wo_ref[0:SSD_WIDTH, :]) + _dot(ya_ref[...], wo_ref[SSD_WIDTH:, :])
    h = x_ref[...] + _rms(mix, g1_ref[...], NORM_EPS)
    qn = _rms(h, g2_ref[...], NORM_EPS).astype(BF16)
    qx = (_dot(qn, wq_ref[...]) * (math.log2(math.e) / math.sqrt(MEM_HEAD_DIM))).astype(BF16)
    ox_seqs = []
    for s_i in range(spt):
        qs = qx[s_i * rows:(s_i + 1) * rows]
        oxs = []
        for hd in range(MEM_HEADS):
            sl = slice(hd * MEM_HEAD_DIM, (hd + 1) * MEM_HEAD_DIM)
            s = lax.dot_general(qs[:, sl], mem_head(mk_ref, s_i, hd), (((1,), (1,)), ((), ())),
                                preferred_element_type=F32)
            p = jnp.exp2(s - jnp.max(s, axis=-1, keepdims=True))
            ox = _dot(p.astype(BF16), mem_head(mv_ref, s_i, hd)) * (1.0 / jnp.sum(p, axis=-1, keepdims=True))
            oxs.append(ox.astype(BF16))
        ox_seqs.append(jnp.concatenate(oxs, axis=1))
    ox_all = ox_seqs[0] if spt == 1 else jnp.concatenate(ox_seqs, axis=0)
    o2 = _dot(ox_all, wox_ref[...])
    h_ref[...] = h + _rms(o2, g3_ref[...], NORM_EPS)


def _post_mix(x, ys, ya, mk, mv, w_out, wq, wox, g1, g2, g3, *, seq, tm):
    n = x.shape[0]
    spt = max(1, tm // seq)
    tiles_per_seq = max(1, seq // tm)
    row = lambda c: pl.BlockSpec((tm, c), lambda i: (i, 0))
    cached_mem = mk.dtype == F32
    mem = pl.BlockSpec((spt,) + mk.shape[1:], lambda i: (i // tiles_per_seq, 0, 0))
    wspec = _const_spec((D_MODEL, D_MODEL))
    gspec = _const_spec((1, D_MODEL))
    return pl.pallas_call(
        functools.partial(_post_mix_kernel, spt=spt, rows=tm // spt, cached_mem=cached_mem),
        grid=(n // tm,),
        in_specs=[row(D_MODEL), row(SSD_WIDTH), row(ATT_WIDTH), mem, mem, wspec, wspec, wspec, gspec, gspec, gspec],
        out_specs=row(D_MODEL),
        out_shape=jax.ShapeDtypeStruct((n, D_MODEL), F32),
        compiler_params=pltpu.CompilerParams(dimension_semantics=("arbitrary",), vmem_limit_bytes=VMEM_LIMIT),
        name="post_mix",
    )(x, ys, ya, mk, mv, w_out, wq, wox, g1, g2, g3)


def _ffn_kernel(h_ref, wg_ref, wu_ref, wd_ref, g1_ref, g2_ref, o_ref):
    h = h_ref[...]
    hn = _rms(h, g1_ref[...], NORM_EPS).astype(BF16)
    act = (_silu(_dot(hn, wg_ref[...])) * _dot(hn, wu_ref[...])).astype(BF16)
    f = _dot(act, wd_ref[...])
    o_ref[...] = h + _rms(f, g2_ref[...], NORM_EPS)


def _ffn(h, wg, wu, wd, g1, g2, *, tm):
    n = h.shape[0]
    row = pl.BlockSpec((tm, D_MODEL), lambda i: (i, 0))
    return pl.pallas_call(
        _ffn_kernel,
        grid=(n // tm,),
        in_specs=[row, _const_spec((D_MODEL, FFN_HIDDEN)), _const_spec((D_MODEL, FFN_HIDDEN)),
                  _const_spec((FFN_HIDDEN, D_MODEL)), _const_spec((1, D_MODEL)), _const_spec((1, D_MODEL))],
        out_specs=row,
        out_shape=jax.ShapeDtypeStruct((n, D_MODEL), F32),
        compiler_params=pltpu.CompilerParams(dimension_semantics=("arbitrary",), vmem_limit_bytes=VMEM_LIMIT),
        name="ffn",
    )(h, wg, wu, wd, g1, g2)


def _rope_tables(past, seq):
    half = ATT_HEAD_DIM // 2
    inv = jnp.power(ROPE_THETA, -jnp.arange(0, ATT_HEAD_DIM, 2, dtype=F32) / ATT_HEAD_DIM)
    pos = (past + jnp.arange(seq, dtype=jnp.int32)).astype(F32)
    ang = pos[:, None] * inv[None, :]
    cos, sin = jnp.cos(ang), jnp.sin(ang)
    reps = LANES // ATT_HEAD_DIM
    assert half * 2 == ATT_HEAD_DIM
    return jnp.tile(jnp.concatenate([cos, cos], axis=-1), (1, reps)), jnp.tile(jnp.concatenate([-sin, sin], axis=-1), (1, reps))


def _mem_cache_rows(c):
    b = c.shape[0]
    c = c.reshape(b, MEM_LEN, MEM_HEADS, MEM_HEAD_DIM // LANES, LANES)
    return c.transpose(0, 1, 3, 2, 4).reshape(b, MEM_LEN * (MEM_HEAD_DIM // LANES) * MEM_HEADS, LANES)


def _state_to_kernel_layout(s):
    b = s.shape[0]
    s = s.reshape(b, SSD_GROUPS, SSD_REP, SSD_HEAD_DIM, SSD_STATE)
    return s.transpose(0, 1, 4, 2, 3).reshape(b, SSD_GROUPS, SSD_STATE, SSD_REP * SSD_HEAD_DIM)


def _state_from_kernel_layout(s):
    b = s.shape[0]
    s = s.reshape(b, SSD_GROUPS, SSD_STATE, SSD_REP, SSD_HEAD_DIM)
    return s.transpose(0, 1, 3, 4, 2).reshape(b, SSD_HEADS, SSD_HEAD_DIM, SSD_STATE)


def _layer(x, conv_buf, ssm0, kt_past, v_past, mem_kb, mem_vb, lam_init, p, *, tm, tk, tq, ssd_q):
    nb, seq, _ = x.shape
    n = nb * seq
    past = 0 if kt_past is None else kt_past.shape[2]
    xf = x.reshape(n, D_MODEL)
    cos, sin = _rope_tables(past, seq)
    no_history = kt_past is None
    assert tq == tk or not no_history
    cbuf = jnp.pad(conv_buf.astype(F32), ((0, 0), (SUBLANES - (SSD_CONV - 1), 0), (0, 0)))
    gate, u, tail, dt, q, k, v, qt, kb, vt = _in_proj(xf, p["g_pre_mix"], p["w_in"], cos, sin, cbuf, p["conv_w"],
                                                       p["conv_b"], seq=seq, tm=tm, tt=tk if no_history else 0)

    seq_pad = -(-seq // ssd_q) * ssd_q
    if seq_pad != seq:
        pad = lambda a: jnp.pad(a.reshape(nb, seq, -1), ((0, 0), (0, seq_pad - seq), (0, 0))).reshape(nb * seq_pad, -1)
        gate_s, u_s, dt_s = pad(gate), pad(u), pad(dt)
    else:
        gate_s, u_s, dt_s = gate, u, dt
    y_ssd, h_new = _ssd(gate_s, u_s, dt_s, _state_to_kernel_layout(ssm0.astype(F32)),
                        p["dt_bias"], p["a_log"], p["d_skip"], p["ssm_norm_w"], p["e2"],
                        nb=nb, seq=seq_pad, q=ssd_q, valid=None if seq_pad == seq else seq)
    if seq_pad != seq:
        y_ssd = y_ssd.reshape(nb, seq_pad, SSD_WIDTH)[:, :seq].reshape(n, SSD_WIDTH)
    ssm_new = _state_from_kernel_layout(h_new)
    ext_tail = tail.reshape(nb, -1, SUBLANES, SSD_XBC)[:, -1]
    if seq >= SSD_CONV - 1:
        conv_new = ext_tail[:, SUBLANES - (SSD_CONV - 1):]
    else:
        conv_new = jnp.concatenate([conv_buf.astype(F32), ext_tail[:, SUBLANES - seq:]], axis=1)[:, -(SSD_CONV - 1):]

    if no_history:
        kv_len = seq
        y_att = _diff_attn(qt, kb, vt, p["lam_vecs"], p["subln_col"], nb=nb, tq=tq, tk=tk,
                           past=past, kv_len=kv_len, lam_init=lam_init, nbuf=ATT_PIPELINE_BUFFERS)
    else:
        y_att = _decode_attn(q, k, v, kt_past, v_past, p["lam_vecs"], p["subln_col"].reshape(1, ATT_V_DIM),
                             nb=nb, seq=seq, past=past, lam_init=lam_init)

    h = _post_mix(xf, y_ssd, y_att, mem_kb, mem_vb, p["w_out"], p["wq_x"], p["wo_x"],
                  p["g_post_mix"], p["g_pre_x"], p["g_post_x"], seq=seq, tm=tm)
    out = _ffn(h, p["w_gate"], p["w_up"], p["w_down"], p["g_pre_ffn"], p["g_post_ffn"], tm=tm)
    if no_history:
        k_out = k.reshape(nb, ATT_HEADS, 2, ATT_HEAD_DIM, seq).transpose(0, 4, 1, 2, 3)
    else:
        k_out = k.reshape(nb, seq, ATT_HEADS, 2, ATT_HEAD_DIM)
    return (out.reshape(nb, seq, D_MODEL), k_out,
            v.reshape(nb, seq, ATT_HEADS, ATT_V_DIM), ssm_new, conv_new)


def _prep_params(i, w_in, conv_w, conv_b, dt_bias, a_log, d_skip, ssm_norm_w, lam_q1, lam_k1, lam_q2, lam_k2, subln_w,
                 w_out, wq_x, wo_x, g_pre_mix, g_post_mix, g_pre_x, g_post_x, g_pre_ffn, g_post_ffn,
                 w_gate, w_up, w_down):
    w = w_in[i]
    s0 = SSD_WIDTH + SSD_XBC
    s1 = s0 + SSD_HEADS
    w_r = jnp.concatenate([w[:, :s0], w[:, s1:], w[:, s0:s1], jnp.zeros((D_MODEL, DT_PAD - SSD_HEADS), w.dtype)], axis=1)
    head_pad = lambda a: jnp.pad(a[i].astype(F32), (0, DT_PAD - SSD_HEADS)).reshape(1, DT_PAD)
    e = (jnp.arange(DT_PAD)[:, None] == (jnp.arange(SSD_WIDTH)[None, :] // SSD_HEAD_DIM)).astype(BF16)
    row = lambda a: a[i].astype(F32).reshape(1, -1)
    w_out_b, wq_b, wo_b = _cast_bf16([w_out[i], wq_x[i], wo_x[i]], steps=4)
    w_gate_b, w_up_b, w_down_b = _cast_bf16([w_gate[i], w_up[i], w_down[i]], steps=8)
    return {
        "w_in": w_r.astype(BF16), "conv_w": conv_w[i].astype(F32), "conv_b": row(conv_b),
        "dt_bias": head_pad(dt_bias), "a_log": head_pad(a_log), "d_skip": head_pad(d_skip),
        "ssm_norm_w": row(ssm_norm_w), "e2": jnp.concatenate([e, e], axis=0),
        "lam_vecs": jnp.stack([lam_q1[i], lam_k1[i], lam_q2[i], lam_k2[i]]).astype(F32), "subln_col": subln_w[i].astype(F32).reshape(ATT_V_DIM, 1),
        "w_out": w_out_b, "wq_x": wq_b, "wo_x": wo_b,
        "g_pre_mix": row(g_pre_mix), "g_post_mix": row(g_post_mix), "g_pre_x": row(g_pre_x), "g_post_x": row(g_post_x),
        "g_pre_ffn": row(g_pre_ffn), "g_post_ffn": row(g_post_ffn),
        "w_gate": w_gate_b, "w_up": w_up_b, "w_down": w_down_b,
    }


def kernel(x_prompt, x_sample, cache_attn_k, cache_attn_v, state_ssm, state_conv, cache_mem_k, cache_mem_v, mem_prompt, w_in, conv_w, conv_b, dt_bias, a_log, d_skip, ssm_norm_w, lam_q1, lam_k1, lam_q2, lam_k2, subln_w, w_out, g_mem, wq_x, wk_x, wv_x, wo_x, g_pre_mix, g_post_mix, g_pre_x, g_post_x, g_pre_ffn, g_post_ffn, w_gate, w_up, w_down):
    depth = w_in.shape[0]
    bp, sp, _ = x_prompt.shape
    bs, ss, _ = x_sample.shape
    hp, hs = x_prompt, x_sample
    outs = [[] for _ in range(10)]
    for i in range(depth):
        lam_init = 0.8 - 0.6 * math.exp(-0.3 * i)
        p = _prep_params(i, w_in, conv_w, conv_b, dt_bias, a_log, d_skip, ssm_norm_w, lam_q1, lam_k1, lam_q2, lam_k2,
                         subln_w, w_out, wq_x, wo_x, g_pre_mix, g_post_mix, g_pre_x, g_post_x, g_pre_ffn, g_post_ffn,
                         w_gate, w_up, w_down)
        mk, mv, mkb, mvb = _mem_kv(mem_prompt.reshape(bp * MEM_LEN, D_MODEL), g_mem[i].reshape(1, D_MODEL),
                                   wk_x[i], wv_x[i])
        hp, k_new, v_new, ssm_new, conv_new = _layer(
            hp, jnp.zeros((bp, SSD_CONV - 1, SSD_XBC), F32), jnp.zeros((bp, SSD_HEADS, SSD_HEAD_DIM, SSD_STATE), F32),
            None, None, mkb.reshape(bp, MEM_LEN, D_MODEL), mvb.reshape(bp, MEM_LEN, D_MODEL), lam_init, p,
            tm=512, tk=512, tq=512, ssd_q=256)
        for lst, val in zip(outs[:6], (k_new, v_new, ssm_new, conv_new,
                                       mk.reshape(bp, MEM_LEN, MEM_HEADS, MEM_HEAD_DIM),
                                       mv.reshape(bp, MEM_LEN, MEM_HEADS, MEM_HEAD_DIM))):
            lst.append(val)
        past = cache_attn_k.shape[2]
        hs, k_new, v_new, ssm_new, conv_new = _layer(
            hs, state_conv[i], state_ssm[i],
            cache_attn_k[i].transpose(0, 2, 3, 4, 1).reshape(bs, ATT_WIDTH, past),
            cache_attn_v[i].reshape(bs, past * ATT_HEADS, ATT_V_DIM),
            _mem_cache_rows(cache_mem_k[i]), _mem_cache_rows(cache_mem_v[i]), lam_init, p,
            tm=bs * ss, tk=0, tq=0, ssd_q=128)
        for lst, val in zip(outs[6:], (k_new, v_new, ssm_new, conv_new)):
            lst.append(val)
    return (hp, hs) + tuple(jnp.stack(o) for o in outs)
```

```python
import functools
import math

import jax
import jax.numpy as jnp
from jax import lax
from jax.experimental import pallas as pl
from jax.experimental.pallas import tpu as pltpu

D_MODEL = 1024
CHUNK = 64
SSD_WIDTH = 512
SSD_HEAD_DIM = 64
SSD_HEADS = 8
SSD_GROUPS = 2
SSD_REP = 4
SSD_STATE = 128
SSD_CONV = 4
SSD_XBC = 1024
SSD_NORM_GROUP = 256
SSD_NORM_EPS = 1e-5
ATT_WIDTH = 512
ATT_HEAD_DIM = 64
ATT_HEADS = 4
ATT_V_DIM = 128
ATT_NORM_EPS = 1e-5
ROPE_THETA = 10000.0
MEM_LEN = 256
MEM_HEADS = 4
MEM_HEAD_DIM = 256
FFN_HIDDEN = 2816
NORM_EPS = 1e-6
LANES = 128
SUBLANES = 8
DT_PAD = LANES
IN_COLS_PADDED = SSD_WIDTH + SSD_XBC + 3 * ATT_WIDTH + DT_PAD
VMEM_LIMIT = 56 * 1024 * 1024
NEG_BIG = -1e30
MXU_TILE = 256
BF16_SUBLANES = 16
ATT_VT_ROWS = ATT_V_DIM + BF16_SUBLANES
ATT_SOFTMAX_ROWS = 128
ATT_PIPELINE_BUFFERS = 2
Q_SCALE = math.log2(math.e) / math.sqrt(ATT_HEAD_DIM)

F32 = jnp.float32
BF16 = jnp.bfloat16


def _const_spec(shape):
    return pl.BlockSpec(shape, lambda *_: (0,) * len(shape), pipeline_mode=pl.Buffered(1))


def _rms(x, g, eps):
    return x * lax.rsqrt(jnp.mean(x * x, axis=-1, keepdims=True) + eps) * g


def _silu(x):
    h = 0.5 * x
    return h + h * jnp.tanh(h)


def _dot(a, b):
    return jnp.dot(a, b, preferred_element_type=F32)


def _cast_kernel(*refs):
    n = len(refs) // 2
    for x_ref, o_ref in zip(refs[:n], refs[n:]):
        o_ref[...] = x_ref[...].astype(BF16)


def _cast_bf16(arrays, steps):
    specs = [pl.BlockSpec((a.shape[0] // steps, a.shape[1]), lambda i: (i, 0)) for a in arrays]
    assert all(a.shape[0] % (steps * BF16_SUBLANES) == 0 for a in arrays)
    return pl.pallas_call(
        _cast_kernel,
        grid=(steps,),
        in_specs=specs,
        out_specs=specs,
        out_shape=[jax.ShapeDtypeStruct(a.shape, BF16) for a in arrays],
        compiler_params=pltpu.CompilerParams(dimension_semantics=("arbitrary",), vmem_limit_bytes=VMEM_LIMIT),
        name="cast_bf16",
    )(*arrays)


def _mem_kv_kernel(mem_ref, g_ref, wk_ref, wv_ref, mk_ref, mv_ref, mkb_ref, mvb_ref):
    mn = _rms(mem_ref[...], g_ref[...], NORM_EPS).astype(BF16)
    mk = _dot(mn, wk_ref[...].astype(BF16))
    mv = _dot(mn, wv_ref[...].astype(BF16))
    mk_ref[...] = mk
    mv_ref[...] = mv
    mkb_ref[...] = mk.astype(BF16)
    mvb_ref[...] = mv.astype(BF16)


def _mem_kv(mem, g_mem, wk, wv):
    n = mem.shape[0]
    tm = MEM_LEN
    row = pl.BlockSpec((tm, D_MODEL), lambda i: (i, 0))
    return pl.pallas_call(
        _mem_kv_kernel,
        grid=(n // tm,),
        in_specs=[row, _const_spec((1, D_MODEL)), _const_spec((D_MODEL, D_MODEL)), _const_spec((D_MODEL, D_MODEL))],
        out_specs=[row, row, row, row],
        out_shape=[jax.ShapeDtypeStruct((n, D_MODEL), F32), jax.ShapeDtypeStruct((n, D_MODEL), F32),
                   jax.ShapeDtypeStruct((n, D_MODEL), BF16), jax.ShapeDtypeStruct((n, D_MODEL), BF16)],
        compiler_params=pltpu.CompilerParams(dimension_semantics=("arbitrary",), vmem_limit_bytes=VMEM_LIMIT),
        name="mem_kv",
    )(mem, g_mem, wk, wv)


def _in_proj_kernel(x_ref, g_ref, w_ref, wqkv_ref, cos_ref, sin_ref, cbuf_ref, cw_ref, cb_ref,
                    z_ref, u_ref, tail_ref, dt_ref, q_ref, k_ref, v_ref, qt_ref, kb_ref, vt_ref, acc_ref, xtail,
                    *, tm, tt, spt, tiles_per_seq):
    step = pl.program_id(0)
    tile = step - 1

    @pl.when(step == 0)
    def _():
        acc_ref[...] = jnp.zeros_like(acc_ref)
        xtail[...] = jnp.zeros_like(xtail)

    c_xbc, c_dt = SSD_WIDTH, SSD_WIDTH + SSD_XBC
    c_q = c_dt + DT_PAD
    c_k, c_v = c_q + ATT_WIDTH, c_q + 2 * ATT_WIDTH
    z_ref[...] = _silu(acc_ref[:, :SSD_WIDTH]).astype(BF16)
    rows = tm // spt
    row8 = lax.broadcasted_iota(jnp.int32, (SUBLANES, LANES), 0)
    first_tile_of_seq = tile % tiles_per_seq == 0
    for s_i in range(spt):
        r0 = s_i * rows
        tail_ref[s_i] = acc_ref[r0 + rows - SUBLANES:r0 + rows, c_xbc:c_dt]
        for cb in range(SSD_XBC // LANES):
            sl = slice(cb * LANES, (cb + 1) * LANES)
            piece = acc_ref[r0:r0 + rows, c_xbc + cb * LANES:c_xbc + (cb + 1) * LANES]
            if spt > 1:
                prev = cbuf_ref[s_i, :, sl]
            else:
                prev = jnp.where(first_tile_of_seq, cbuf_ref[0, :, sl], xtail[:, sl])
            conv_top = cb_ref[:, sl] + piece[0:SUBLANES] * cw_ref[SSD_CONV - 1:SSD_CONV, sl]
            conv_rest = cb_ref[:, sl] + piece[SUBLANES:] * cw_ref[SSD_CONV - 1:SSD_CONV, sl]
            for d in range(1, SSD_CONV):
                sh = pltpu.roll(piece, d, 0)
                tap = cw_ref[SSD_CONV - 1 - d:SSD_CONV - d, sl]
                conv_top = conv_top + jnp.where(row8 < d, pltpu.roll(prev, d, 0), sh[0:SUBLANES]) * tap
                conv_rest = conv_rest + sh[SUBLANES:] * tap
            u_ref[r0:r0 + rows, sl] = _silu(jnp.concatenate([conv_top, conv_rest], axis=0)).astype(BF16)
    if spt == 1:
        xtail[...] = acc_ref[tm - SUBLANES:, c_xbc:c_dt]
    dt_ref[...] = acc_ref[:, c_dt:c_dt + DT_PAD]
    if tt:
        for j in range(ATT_HEADS):
            v_ref[pl.ds(j, tm, stride=ATT_HEADS), :] = acc_ref[:, c_v + j * ATT_V_DIM:c_v + (j + 1) * ATT_V_DIM]
    else:
        v_ref[...] = acc_ref[:, c_v:c_v + ATT_WIDTH]

    cos = cos_ref[...]
    sin = sin_ref[...]
    first_half = (lax.broadcasted_iota(jnp.int32, (tm, LANES), 1) % ATT_HEAD_DIM) < (ATT_HEAD_DIM // 2)

    def rope(t):
        swapped = jnp.where(first_half, pltpu.roll(t, LANES - ATT_HEAD_DIM // 2, 1),
                            pltpu.roll(t, ATT_HEAD_DIM // 2, 1))
        return t * cos + swapped * sin

    for j in range(ATT_WIDTH // LANES):
        sl = slice(j * LANES, (j + 1) * LANES)
        qr = rope(acc_ref[:, c_q + j * LANES:c_q + (j + 1) * LANES]) * Q_SCALE
        kr = rope(acc_ref[:, c_k + j * LANES:c_k + (j + 1) * LANES])
        q_ref[:, sl] = qr.astype(BF16)
        if tt:
            k_ref[0, sl, :] = kr.T
            kb_ref[:, sl] = kr.astype(BF16)
            for c in range(tm // tt):
                qt_ref[c, sl, :] = qr[c * tt:(c + 1) * tt, :].T.astype(BF16)
                vt_ref[c, j * ATT_VT_ROWS:j * ATT_VT_ROWS + LANES, :] = acc_ref[
                    c * tt:(c + 1) * tt, c_v + j * LANES:c_v + (j + 1) * LANES].T.astype(BF16)
                vt_ref[c, j * ATT_VT_ROWS + LANES:(j + 1) * ATT_VT_ROWS, :] = jnp.ones((ATT_VT_ROWS - LANES, tt), BF16)
        else:
            k_ref[:, sl] = kr

    hn = _rms(x_ref[...], g_ref[...], NORM_EPS).astype(BF16)
    acc_ref[:, :c_q] = _dot(hn, w_ref[...])
    acc_ref[:, c_q:] = _dot(hn, wqkv_ref[...])


def _in_proj(x, g, w, wqkv, cos, sin, cbuf, cw, cb, *, seq, tm, tt):
    n = x.shape[0]
    nt = n // tm
    spt = max(1, tm // seq)
    tiles_per_seq = max(1, seq // tm)
    if spt > 1:
        cos, sin = jnp.tile(cos, (spt, 1)), jnp.tile(sin, (spt, 1))
    done = lambda i: jnp.maximum(i - 1, 0)
    row = lambda c: pl.BlockSpec((tm, c), lambda i: (done(i), 0))
    tab = pl.BlockSpec((tm, LANES), lambda i: (done(i) % tiles_per_seq, 0))
    if tt:
        k_spec = pl.BlockSpec((1, ATT_WIDTH, tm), lambda i: (done(i) // tiles_per_seq, 0, done(i) % tiles_per_seq))
        k_shape = jax.ShapeDtypeStruct((n // seq, ATT_WIDTH, seq), F32)
        v_spec = pl.BlockSpec((tm * ATT_HEADS, ATT_V_DIM), lambda i: (done(i), 0))
        v_shape = jax.ShapeDtypeStruct((n * ATT_HEADS, ATT_V_DIM), F32)
    else:
        k_spec, k_shape = row(ATT_WIDTH), jax.ShapeDtypeStruct((n, ATT_WIDTH), F32)
        v_spec, v_shape = row(ATT_WIDTH), jax.ShapeDtypeStruct((n, ATT_WIDTH), F32)
    out_specs = [row(SSD_WIDTH), row(SSD_XBC), pl.BlockSpec((spt, SUBLANES, SSD_XBC), lambda i: (done(i), 0, 0)),
                 row(DT_PAD), row(ATT_WIDTH), k_spec, v_spec]
    out_shape = [jax.ShapeDtypeStruct((n, SSD_WIDTH), BF16), jax.ShapeDtypeStruct((n, SSD_XBC), BF16),
                 jax.ShapeDtypeStruct((nt * spt, SUBLANES, SSD_XBC), F32), jax.ShapeDtypeStruct((n, DT_PAD), F32),
                 jax.ShapeDtypeStruct((n, ATT_WIDTH), BF16), k_shape, v_shape]
    if tt:
        tr = lambda r: pl.BlockSpec((tm // tt, r, tt), lambda i: (done(i), 0, 0))
        vt_rows = ATT_HEADS * ATT_VT_ROWS
        out_specs += [tr(ATT_WIDTH), row(ATT_WIDTH), tr(vt_rows)]
        out_shape += [jax.ShapeDtypeStruct((n // tt, ATT_WIDTH, tt), BF16), jax.ShapeDtypeStruct((n, ATT_WIDTH), BF16),
                      jax.ShapeDtypeStruct((n // tt, vt_rows, tt), BF16)]

    def body(*refs):
        refs, scratch = refs[:-2], refs[-2:]
        refs = refs + (None,) * (19 - len(refs))
        _in_proj_kernel(*refs, *scratch, tm=tm, tt=tt, spt=spt, tiles_per_seq=tiles_per_seq)

    outs = pl.pallas_call(
        body,
        grid=(nt + 1,),
        in_specs=[pl.BlockSpec((tm, D_MODEL), lambda i: (jnp.minimum(i, nt - 1), 0)),
                  _const_spec((1, D_MODEL)), _const_spec(w.shape), _const_spec(wqkv.shape), tab, tab,
                  pl.BlockSpec((spt, SUBLANES, SSD_XBC), lambda i: (done(i) // tiles_per_seq, 0, 0)),
                  _const_spec((SSD_CONV, SSD_XBC)), _const_spec((1, SSD_XBC))],
        out_specs=out_specs,
        out_shape=out_shape,
        scratch_shapes=[pltpu.VMEM((tm, IN_COLS_PADDED), F32), pltpu.VMEM((SUBLANES, SSD_XBC), F32)],
        compiler_params=pltpu.CompilerParams(dimension_semantics=("arbitrary",), vmem_limit_bytes=VMEM_LIMIT),
        name="in_proj",
    )(x, g, w, wqkv, cos, sin, cbuf, cw, cb)
    return list(outs) + [None] * (10 - len(outs))


def _expand_heads(x, e2):
    hi = x.astype(BF16)
    lo = (x - hi.astype(F32)).astype(BF16)
    return _dot(jnp.concatenate([hi, lo], axis=1), e2)


def _ssd_kernel(gate_ref, u_ref, dt_ref, h0_ref, dtb_ref, alog_ref, dsk_ref, nw_ref, e2_ref,
                y_ref, hout_ref, state, *, q, valid):
    c = pl.program_id(1)

    @pl.when(c == 0)
    def _():
        state[...] = h0_ref[0]

    xs = u_ref[:, :SSD_WIDTH].astype(F32)
    bm = u_ref[:, SSD_WIDTH:SSD_WIDTH + SSD_GROUPS * SSD_STATE]
    cmb = u_ref[:, SSD_WIDTH + SSD_GROUPS * SSD_STATE:]

    dtr = dt_ref[...] + dtb_ref[...]
    dt = jnp.maximum(dtr, 0.0) + jnp.log(1.0 + jnp.exp(-jnp.abs(dtr)))
    if valid is not None:
        row = lax.broadcasted_iota(jnp.int32, (q, DT_PAD), 0) + c * q
        dt = jnp.where(row < valid, dt, 0.0)
    a = -jnp.exp(alog_ref[...])
    ad = dt * a
    ri = lax.broadcasted_iota(jnp.int32, (q, q), 0)
    ci = lax.broadcasted_iota(jnp.int32, (q, q), 1)
    tril = ri >= ci
    tril_b = jnp.where(tril, 1.0, 0.0).astype(BF16)
    ad_hi = ad.astype(BF16)
    ad_r = ad - ad_hi.astype(F32)
    ad_mid = ad_r.astype(BF16)
    ad_lo = (ad_r - ad_mid.astype(F32)).astype(BF16)
    acum = _dot(tril_b, ad_hi) + _dot(tril_b, ad_mid) + _dot(tril_b, ad_lo)
    acum_t = acum.T
    tot = acum[q - 1:q, :]
    e2 = e2_ref[...]
    expanded = _expand_heads(jnp.concatenate([dt, dt * jnp.exp(tot - acum), jnp.exp(acum)], axis=0), e2)
    dtx = expanded[0:q]
    ddx = expanded[q:2 * q]
    eax = expanded[2 * q:3 * q]
    dsk = _expand_heads(jnp.broadcast_to(dsk_ref[...], (SUBLANES, DT_PAD)), e2)[0:1]

    xdt = (xs * dtx).astype(BF16)
    xdtd = (xs * ddx).astype(BF16)
    bm_t = bm.astype(F32).T.astype(BF16)
    gw = SSD_REP * SSD_HEAD_DIM
    stripe = lax.broadcasted_iota(jnp.int32, (q, gw), 1) // SSD_HEAD_DIM
    ys = []
    for g in range(SSD_GROUPS):
        cm_g = cmb[:, g * SSD_STATE:(g + 1) * SSD_STATE]
        bt_g = bm_t[g * SSD_STATE:(g + 1) * SSD_STATE, :]
        cbm = _dot(cm_g, bt_g)
        ms = []
        for r in range(SSD_REP):
            h = g * SSD_REP + r
            diff = acum[:, h:h + 1] - acum_t[h:h + 1, :]
            ms.append((cbm * jnp.exp(jnp.where(tril, diff, -jnp.inf))).astype(BF16))
        ydf = _dot(jnp.concatenate(ms, axis=0), xdt[:, g * gw:(g + 1) * gw])
        yd = ydf[0:q]
        for r in range(1, SSD_REP):
            yd = jnp.where(stripe == r, ydf[r * q:(r + 1) * q], yd)
        st = state[g]
        y_off = _dot(cm_g, st.astype(BF16)) * eax[:, g * gw:(g + 1) * gw]
        state[g] = st * eax[q - 1:q, g * gw:(g + 1) * gw] + _dot(bt_g, xdtd[:, g * gw:(g + 1) * gw])
        ys.append(yd + y_off)
    y = jnp.concatenate(ys, axis=1) + dsk * xs
    y = y * gate_ref[...].astype(F32)
    outs = []
    for g in range(SSD_WIDTH // SSD_NORM_GROUP):
        yg = y[:, g * SSD_NORM_GROUP:(g + 1) * SSD_NORM_GROUP]
        outs.append(yg * lax.rsqrt(jnp.mean(yg * yg, axis=-1, keepdims=True) + SSD_NORM_EPS))
    y_ref[...] = (jnp.concatenate(outs, axis=1) * nw_ref[...]).astype(BF16)

    @pl.when(c == pl.num_programs(1) - 1)
    def _():
        hout_ref[0] = state[...]


def _ssd(gate, u, dt, h0, dtb, alog, dsk, nw, e2, *, nb, seq, q, valid):
    nc = seq // q
    row = lambda c: pl.BlockSpec((q, c), lambda b, i: (b * nc + i, 0))
    gw = SSD_REP * SSD_HEAD_DIM
    st_spec = pl.BlockSpec((1, SSD_GROUPS, SSD_STATE, gw), lambda b, i: (b, 0, 0, 0))
    return pl.pallas_call(
        functools.partial(_ssd_kernel, q=q, valid=valid),
        grid=(nb, nc),
        in_specs=[row(SSD_WIDTH), row(SSD_XBC), row(DT_PAD), st_spec,
                  _const_spec((1, DT_PAD)), _const_spec((1, DT_PAD)), _const_spec((1, DT_PAD)),
                  _const_spec((1, SSD_WIDTH)), _const_spec((2 * DT_PAD, SSD_WIDTH))],
        out_specs=[row(SSD_WIDTH), st_spec],
        out_shape=[jax.ShapeDtypeStruct((nb * seq, SSD_WIDTH), BF16),
                   jax.ShapeDtypeStruct((nb, SSD_GROUPS, SSD_STATE, gw), F32)],
        scratch_shapes=[pltpu.VMEM((SSD_GROUPS, SSD_STATE, gw), F32)],
        compiler_params=pltpu.CompilerParams(dimension_semantics=("arbitrary", "arbitrary"),
                                             vmem_limit_bytes=VMEM_LIMIT),
        name="ssd",
    )(gate, u, dt, h0, dtb, alog, dsk, nw, e2)


def _column_max(x):
    while x.shape[0] > SUBLANES and x.shape[0] % (2 * SUBLANES) == 0:
        half = x.shape[0] // 2
        x = jnp.maximum(x[:half], x[half:])
    return jnp.max(x, axis=0, keepdims=True)


def _attn_tile_counts(i, *, tq, tk, past, kv_len, minimum=min):
    q_lo = past + i * tq
    q_hi = q_lo + tq - 1
    lim_lo = minimum((q_lo // CHUNK + 1) * CHUNK, kv_len)
    lim_hi = minimum((q_hi // CHUNK + 1) * CHUNK, kv_len)
    return lim_lo // tk, (lim_hi + tk - 1) // tk


def _diff_attn_kernel(qt_ref, qt_next_ref, k_ref, vt_ref, lam_ref, sw_ref, o_ref,
                      q2t_ref, s_ref, p_ref, m_ref, alpha_ref, acc_ref, *, tq, tk, past, kv_len, lam_init, nbuf, nkt):
    i = pl.program_id(2)
    w = 2 * tq
    row = lax.broadcasted_iota(jnp.int32, (ATT_V_DIM, tq), 0)
    n_full, _ = _attn_tile_counts(i, tq=tq, tk=tk, past=past, kv_len=kv_len, minimum=jnp.minimum)
    n_visits = n_full + 1

    def scores(j):
        kt = k_ref[pl.ds(pl.multiple_of(j * tk, tk), tk), :]
        return _dot(kt, q2t_ref[...])

    def visited_tile(v):
        return jnp.where(v == 0, n_full, jnp.maximum(v - 1, 0))

    ncb = w // LANES
    pw = p_ref.shape[-1]

    def store_scores(buf, s):
        for cb in range(ncb):
            s_ref[buf, cb] = s[:, cb * LANES:(cb + 1) * LANES]

    def stage_a(v, buf):
        store_scores(buf, scores(jnp.minimum(v - 1, n_full)))

    rows = min(tk, ATT_SOFTMAX_ROWS)

    def stage_b(buf):
        for cb in range(ncb):
            sl = slice(cb * LANES, (cb + 1) * LANES)
            m_old = m_ref[:, sl]
            m_new = m_old
            for r0 in range(0, tk, rows):
                m_new = jnp.maximum(m_new, _column_max(s_ref[buf, cb, r0:r0 + rows, :]))
            m_ref[:, sl] = m_new
            alpha_ref[buf, :, sl] = jnp.exp2(m_old - m_new)
            pl0 = (cb * LANES) % pw
            for r0 in range(0, tk, rows):
                p_ref[buf, cb * LANES // pw, r0:r0 + rows, pl0:pl0 + LANES] = jnp.exp2(
                    (s_ref[buf, cb, r0:r0 + rows, :] - m_new).astype(BF16))

    def stage_c(v, buf):
        vt = vt_ref[visited_tile(v)]
        for pb in range(w // pw):
            sl = slice(pb * pw, (pb + 1) * pw)
            acc_ref[:, sl] = alpha_ref[buf, :, sl] * acc_ref[:, sl] + _dot(vt, p_ref[buf, pb])

    def first_visit(qt, tile):
        zero = jnp.zeros_like(qt)
        q2t_ref[:, 0:tq] = jnp.where(row < ATT_HEAD_DIM, qt, zero)
        q2t_ref[:, tq:w] = jnp.where(row >= ATT_HEAD_DIM, qt, zero)
        nf, _ = _attn_tile_counts(tile, tq=tq, tk=tk, past=past, kv_len=kv_len, minimum=jnp.minimum)
        nf = jnp.minimum(nf, nkt - 1)
        qchunk = (past + tile * tq + lax.broadcasted_iota(jnp.int32, (1, w), 1) % tq) // CHUNK
        s_part = scores(nf)
        s_masked = []
        for kb in range(tk // CHUNK):
            k0 = nf * tk + kb * CHUNK
            kchunk = jnp.where(k0 < kv_len, k0 // CHUNK, jnp.iinfo(jnp.int32).max)
            s_masked.append(jnp.where(kchunk <= qchunk, s_part[kb * CHUNK:(kb + 1) * CHUNK], NEG_BIG))
        store_scores(0, jnp.concatenate(s_masked, axis=0))

    @pl.when(i == 0)
    def _():
        first_visit(qt_ref[0], i)

    m_ref[...] = jnp.full_like(m_ref, NEG_BIG)
    acc_ref[...] = jnp.zeros_like(acc_ref)
    p_ref[nbuf - 1] = jnp.zeros(p_ref.shape[1:], BF16)
    alpha_ref[nbuf - 1] = jnp.ones(alpha_ref.shape[1:], F32)

    def visits(v0, count, prefetch_last):
        for r in range(count):
            prefetch = r + 1 < count or prefetch_last
            if prefetch and nbuf > 1:
                stage_a(v0 + r + 1, (r + 1) % nbuf)
            stage_c(v0 + r - 1, (r - 1) % nbuf)
            stage_b(r)
            if prefetch and nbuf == 1:
                stage_a(v0 + r + 1, (r + 1) % nbuf)

    def trip(u, carry):
        visits(nbuf * u, nbuf, True)
        return carry

    lax.fori_loop(0, n_visits // nbuf, trip, 0)
    for rem in range(nbuf):

        @pl.when(n_visits % nbuf == rem)
        def _():
            visits(n_visits - rem, rem, False)
            stage_c(n_visits - 1, (rem - 1) % nbuf)

    first_visit(qt_next_ref[0], i + 1)

    lam_v = lam_ref[...]
    lam = (jnp.exp(jnp.sum(lam_v[0:1] * lam_v[1:2], axis=-1, keepdims=True))
           - jnp.exp(jnp.sum(lam_v[2:3] * lam_v[3:4], axis=-1, keepdims=True)) + lam_init)
    acc = acc_ref[...]
    o = acc[:ATT_V_DIM] * (1.0 / acc[ATT_V_DIM:ATT_V_DIM + 1])
    o = o[:, :tq] - lam * o[:, tq:]
    o = o * lax.rsqrt(jnp.mean(o * o, axis=0, keepdims=True) + ATT_NORM_EPS) * (sw_ref[...] * (1.0 - lam_init))
    o_ref[...] = o.T.astype(BF16)


def _diff_attn(qt, k, vt, lam_vecs, subln_col, *, nb, tq, tk, past, kv_len, lam_init, nbuf):
    nq = qt.shape[0] // nb
    nkt = vt.shape[0] // nb
    assert tk % CHUNK == 0 and kv_len % CHUNK == 0
    for i in range(nq):
        n_full, n_end = _attn_tile_counts(i, tq=tq, tk=tk, past=past, kv_len=kv_len)
        assert n_end - n_full == 1 and n_end <= nkt, (i, n_full, n_end)
    w = 2 * tq
    return pl.pallas_call(
        functools.partial(_diff_attn_kernel, tq=tq, tk=tk, past=past, kv_len=kv_len, lam_init=lam_init, nbuf=nbuf,
                          nkt=nkt),
        grid=(nb, ATT_HEADS, nq),
        in_specs=[pl.BlockSpec((1, ATT_V_DIM, tq), lambda b, h, i: (b * nq + i, h, 0)),
                  pl.BlockSpec((1, ATT_V_DIM, tq), lambda b, h, i: (b * nq + jnp.minimum(i + 1, nq - 1), h, 0)),
                  pl.BlockSpec((nkt * tk, ATT_V_DIM), lambda b, h, i: (b, h)),
                  pl.BlockSpec((nkt, ATT_VT_ROWS, tk), lambda b, h, i: (b, h, 0)),
                  _const_spec((4, ATT_HEAD_DIM)), _const_spec((ATT_V_DIM, 1))],
        out_specs=pl.BlockSpec((tq, ATT_V_DIM), lambda b, h, i: (b * nq + i, h)),
        out_shape=jax.ShapeDtypeStruct((nb * nq * tq, ATT_WIDTH), BF16),
        scratch_shapes=[pltpu.VMEM((ATT_V_DIM, w), BF16), pltpu.VMEM((nbuf, w // LANES, tk, LANES), F32),
                        pltpu.VMEM((nbuf, w // MXU_TILE, tk, MXU_TILE), BF16), pltpu.VMEM((1, w), F32),
                        pltpu.VMEM((nbuf, 1, w), F32),
                        pltpu.VMEM((ATT_VT_ROWS, w), F32)],
        compiler_params=pltpu.CompilerParams(dimension_semantics=("arbitrary", "arbitrary", "arbitrary"),
                                             vmem_limit_bytes=VMEM_LIMIT),
        name="diff_attn",
    )(qt, qt, k, vt, lam_vecs, subln_col)


def _decode_attn_kernel(q_ref, kn_ref, vn_ref, kc_ref, vc_ref, lam_ref, sw_ref, o_ref, *, seq, past, lam_init):
    lam_v = lam_ref[...]
    lam = (jnp.exp(jnp.sum(lam_v[0:1] * lam_v[1:2], axis=-1, keepdims=True))
           - jnp.exp(jnp.sum(lam_v[2:3] * lam_v[3:4], axis=-1, keepdims=True)) + lam_init)
    npad = LANES
    lane = lax.broadcasted_iota(jnp.int32, (seq, LANES), 1)
    qchunk = (past + lax.broadcasted_iota(jnp.int32, (2 * seq, 1), 0) % seq) // CHUNK
    kpos_p = lax.broadcasted_iota(jnp.int32, (2 * seq, past), 1)
    kpos_n = lax.broadcasted_iota(jnp.int32, (2 * seq, npad), 1)
    vis_p = kpos_p // CHUNK <= qchunk
    vis_n = ((past + kpos_n) // CHUNK <= qchunk) & (kpos_n < seq)
    pad_rows = jnp.zeros((npad - seq, LANES), BF16)
    for h in range(ATT_HEADS):
        sl = slice(h * LANES, (h + 1) * LANES)
        qh = q_ref[:, sl]
        zero = jnp.zeros_like(qh)
        q2 = jnp.concatenate([jnp.where(lane < ATT_HEAD_DIM, qh, zero), jnp.where(lane >= ATT_HEAD_DIM, qh, zero)], axis=0)
        s_p = jnp.where(vis_p, _dot(q2, kc_ref[0, sl, :].astype(BF16)), NEG_BIG)
        kn = jnp.concatenate([kn_ref[:, sl].astype(BF16), pad_rows], axis=0)
        s_n = lax.dot_general(q2, kn, (((1,), (1,)), ((), ())), preferred_element_type=F32)
        s_n = jnp.where(vis_n, s_n, NEG_BIG)
        m = jnp.maximum(jnp.max(s_p, axis=-1, keepdims=True), jnp.max(s_n, axis=-1, keepdims=True))
        p_p = jnp.exp2(s_p - m)
        p_n = jnp.exp2(s_n - m)
        l = jnp.sum(p_p, axis=-1, keepdims=True) + jnp.sum(p_n, axis=-1, keepdims=True)
        vh = vc_ref[0, pl.ds(h, past, stride=ATT_HEADS), :].astype(BF16)
        vn = jnp.concatenate([vn_ref[:, sl].astype(BF16), pad_rows], axis=0)
        o = (_dot(p_p.astype(BF16), vh) + _dot(p_n.astype(BF16), vn)) / l
        o = o[:seq] - lam * o[seq:]
        o = o * lax.rsqrt(jnp.mean(o * o, axis=-1, keepdims=True) + ATT_NORM_EPS) * (sw_ref[...] * (1.0 - lam_init))
        o_ref[:, sl] = o.astype(BF16)


def _decode_attn(q, k_new, v_new, kt_cache, v_cache, lam_vecs, subln_row, *, nb, seq, past, lam_init):
    row = pl.BlockSpec((seq, ATT_WIDTH), lambda b: (b, 0))
    return pl.pallas_call(
        functools.partial(_decode_attn_kernel, seq=seq, past=past, lam_init=lam_init),
        grid=(nb,),
        in_specs=[row, row, row,
                  pl.BlockSpec((1, ATT_WIDTH, past), lambda b: (b, 0, 0)),
                  pl.BlockSpec((1, past * ATT_HEADS, ATT_V_DIM), lambda b: (b, 0, 0)),
                  _const_spec((4, ATT_HEAD_DIM)), _const_spec((1, ATT_V_DIM))],
        out_specs=row,
        out_shape=jax.ShapeDtypeStruct((nb * seq, ATT_WIDTH), BF16),
        compiler_params=pltpu.CompilerParams(dimension_semantics=("arbitrary",), vmem_limit_bytes=VMEM_LIMIT),
        name="decode_attn",
    )(q, k_new, v_new, kt_cache, v_cache, lam_vecs, subln_row)


def _post_mix_kernel(x_ref, ys_ref, ya_ref, mk_ref, mv_ref, wo_ref, wq_ref, wox_ref,
                     g1_ref, g2_ref, g3_ref, h_ref, *, spt, rows, cached_mem):
    lane_blocks = MEM_HEAD_DIM // LANES

    def mem_head(ref, s_i, hd):
        if not cached_mem:
            return ref[s_i, :, hd * MEM_HEAD_DIM:(hd + 1) * MEM_HEAD_DIM]
        parts = [ref[s_i, pl.ds(cb * MEM_HEADS + hd, MEM_LEN, stride=lane_blocks * MEM_HEADS), :].astype(BF16)
                 for cb in range(lane_blocks)]
        return jnp.concatenate(parts, axis=1)

    mix = _dot(ys_ref[...], wo_ref[0:SSD_WIDTH, :]) + _dot(ya_ref[...], wo_ref[SSD_WIDTH:, :])
    h = x_ref[...] + _rms(mix, g1_ref[...], NORM_EPS)
    qn = _rms(h, g2_ref[...], NORM_EPS).astype(BF16)
    qx = (_dot(qn, wq_ref[...]) * (math.log2(math.e) / math.sqrt(MEM_HEAD_DIM))).astype(BF16)
    ox_seqs = []
    for s_i in range(spt):
        qs = qx[s_i * rows:(s_i + 1) * rows]
        oxs = []
        for hd in range(MEM_HEADS):
            sl = slice(hd * MEM_HEAD_DIM, (hd + 1) * MEM_HEAD_DIM)
            s = lax.dot_general(qs[:, sl], mem_head(mk_ref, s_i, hd), (((1,), (1,)), ((), ())),
                                preferred_element_type=F32)
            p = jnp.exp2(s - jnp.max(s, axis=-1, keepdims=True))
            ox = _dot(p.astype(BF16), mem_head(mv_ref, s_i, hd)) * (1.0 / jnp.sum(p, axis=-1, keepdims=True))
            oxs.append(ox.astype(BF16))
        ox_seqs.append(jnp.concatenate(oxs, axis=1))
    ox_all = ox_seqs[0] if spt == 1 else jnp.concatenate(ox_seqs, axis=0)
    o2 = _dot(ox_all, wox_ref[...])
    h_ref[...] = h + _rms(o2, g3_ref[...], NORM_EPS)


def _post_mix(x, ys, ya, mk, mv, w_out, wq, wox, g1, g2, g3, *, seq, tm):
    n = x.shape[0]
    spt = max(1, tm // seq)
    tiles_per_seq = max(1, seq // tm)
    row = lambda c: pl.BlockSpec((tm, c), lambda i: (i, 0))
    cached_mem = mk.dtype == F32
    mem = pl.BlockSpec((spt,) + mk.shape[1:], lambda i: (i // tiles_per_seq, 0, 0))
    wspec = _const_spec((D_MODEL, D_MODEL))
    gspec = _const_spec((1, D_MODEL))
    return pl.pallas_call(
        functools.partial(_post_mix_kernel, spt=spt, rows=tm // spt, cached_mem=cached_mem),
        grid=(n // tm,),
        in_specs=[row(D_MODEL), row(SSD_WIDTH), row(ATT_WIDTH), mem, mem, wspec, wspec, wspec, gspec, gspec, gspec],
        out_specs=row(D_MODEL),
        out_shape=jax.ShapeDtypeStruct((n, D_MODEL), F32),
        compiler_params=pltpu.CompilerParams(dimension_semantics=("arbitrary",), vmem_limit_bytes=VMEM_LIMIT),
        name="post_mix",
    )(x, ys, ya, mk, mv, w_out, wq, wox, g1, g2, g3)


def _ffn_kernel(h_ref, wg_ref, wu_ref, wd_ref, g1_ref, g2_ref, o_ref):
    h = h_ref[...]
    hn = _rms(h, g1_ref[...], NORM_EPS).astype(BF16)
    act = (_silu(_dot(hn, wg_ref[...])) * _dot(hn, wu_ref[...])).astype(BF16)
    f = _dot(act, wd_ref[...])
    o_ref[...] = h + _rms(f, g2_ref[...], NORM_EPS)


def _ffn(h, wg, wu, wd, g1, g2, *, tm):
    n = h.shape[0]
    row = pl.BlockSpec((tm, D_MODEL), lambda i: (i, 0))
    return pl.pallas_call(
        _ffn_kernel,
        grid=(n // tm,),
        in_specs=[row, _const_spec((D_MODEL, FFN_HIDDEN)), _const_spec((D_MODEL, FFN_HIDDEN)),
                  _const_spec((FFN_HIDDEN, D_MODEL)), _const_spec((1, D_MODEL)), _const_spec((1, D_MODEL))],
        out_specs=row,
        out_shape=jax.ShapeDtypeStruct((n, D_MODEL), F32),
        compiler_params=pltpu.CompilerParams(dimension_semantics=("arbitrary",), vmem_limit_bytes=VMEM_LIMIT),
        name="ffn",
    )(h, wg, wu, wd, g1, g2)


def _rope_tables(past, seq):
    half = ATT_HEAD_DIM // 2
    inv = jnp.power(ROPE_THETA, -jnp.arange(0, ATT_HEAD_DIM, 2, dtype=F32) / ATT_HEAD_DIM)
    pos = (past + jnp.arange(seq, dtype=jnp.int32)).astype(F32)
    ang = pos[:, None] * inv[None, :]
    cos, sin = jnp.cos(ang), jnp.sin(ang)
    reps = LANES // ATT_HEAD_DIM
    assert half * 2 == ATT_HEAD_DIM
    return jnp.tile(jnp.concatenate([cos, cos], axis=-1), (1, reps)), jnp.tile(jnp.concatenate([-sin, sin], axis=-1), (1, reps))


def _mem_cache_rows(c):
    b = c.shape[0]
    c = c.reshape(b, MEM_LEN, MEM_HEADS, MEM_HEAD_DIM // LANES, LANES)
    return c.transpose(0, 1, 3, 2, 4).reshape(b, MEM_LEN * (MEM_HEAD_DIM // LANES) * MEM_HEADS, LANES)


def _state_to_kernel_layout(s):
    b = s.shape[0]
    s = s.reshape(b, SSD_GROUPS, SSD_REP, SSD_HEAD_DIM, SSD_STATE)
    return s.transpose(0, 1, 4, 2, 3).reshape(b, SSD_GROUPS, SSD_STATE, SSD_REP * SSD_HEAD_DIM)


def _state_from_kernel_layout(s):
    b = s.shape[0]
    s = s.reshape(b, SSD_GROUPS, SSD_STATE, SSD_REP, SSD_HEAD_DIM)
    return s.transpose(0, 1, 3, 4, 2).reshape(b, SSD_HEADS, SSD_HEAD_DIM, SSD_STATE)


def _layer(x, conv_buf, ssm0, kt_past, v_past, mem_kb, mem_vb, lam_init, p, *, tm, tk, tq, ssd_q):
    nb, seq, _ = x.shape
    n = nb * seq
    past = 0 if kt_past is None else kt_past.shape[2]
    xf = x.reshape(n, D_MODEL)
    cos, sin = _rope_tables(past, seq)
    no_history = kt_past is None
    assert tq == tk or not no_history
    cbuf = jnp.pad(conv_buf.astype(F32), ((0, 0), (SUBLANES - (SSD_CONV - 1), 0), (0, 0)))
    gate, u, tail, dt, q, k, v, qt, kb, vt = _in_proj(xf, p["g_pre_mix"], p["w_in_ssd"], p["w_in_qkv"], cos, sin, cbuf,
                                                       p["conv_w"], p["conv_b"], seq=seq, tm=tm,
                                                       tt=tk if no_history else 0)

    seq_pad = -(-seq // ssd_q) * ssd_q
    if seq_pad != seq:
        pad = lambda a: jnp.pad(a.reshape(nb, seq, -1), ((0, 0), (0, seq_pad - seq), (0, 0))).reshape(nb * seq_pad, -1)
        gate_s, u_s, dt_s = pad(gate), pad(u), pad(dt)
    else:
        gate_s, u_s, dt_s = gate, u, dt
    y_ssd, h_new = _ssd(gate_s, u_s, dt_s, _state_to_kernel_layout(ssm0.astype(F32)),
                        p["dt_bias"], p["a_log"], p["d_skip"], p["ssm_norm_w"], p["e2"],
                        nb=nb, seq=seq_pad, q=ssd_q, valid=None if seq_pad == seq else seq)
    if seq_pad != seq:
        y_ssd = y_ssd.reshape(nb, seq_pad, SSD_WIDTH)[:, :seq].reshape(n, SSD_WIDTH)
    ssm_new = _state_from_kernel_layout(h_new)
    ext_tail = tail.reshape(nb, -1, SUBLANES, SSD_XBC)[:, -1]
    if seq >= SSD_CONV - 1:
        conv_new = ext_tail[:, SUBLANES - (SSD_CONV - 1):]
    else:
        conv_new = jnp.concatenate([conv_buf.astype(F32), ext_tail[:, SUBLANES - seq:]], axis=1)[:, -(SSD_CONV - 1):]

    if no_history:
        kv_len = seq
        y_att = _diff_attn(qt, kb, vt, p["lam_vecs"], p["subln_col"], nb=nb, tq=tq, tk=tk,
                           past=past, kv_len=kv_len, lam_init=lam_init, nbuf=ATT_PIPELINE_BUFFERS)
    else:
        y_att = _decode_attn(q, k, v, kt_past, v_past, p["lam_vecs"], p["subln_col"].reshape(1, ATT_V_DIM),
                             nb=nb, seq=seq, past=past, lam_init=lam_init)

    h = _post_mix(xf, y_ssd, y_att, mem_kb, mem_vb, p["w_out"], p["wq_x"], p["wo_x"],
                  p["g_post_mix"], p["g_pre_x"], p["g_post_x"], seq=seq, tm=tm)
    out = _ffn(h, p["w_gate"], p["w_up"], p["w_down"], p["g_pre_ffn"], p["g_post_ffn"], tm=tm)
    if no_history:
        k_out = k.reshape(nb, ATT_HEADS, 2, ATT_HEAD_DIM, seq).transpose(0, 4, 1, 2, 3)
    else:
        k_out = k.reshape(nb, seq, ATT_HEADS, 2, ATT_HEAD_DIM)
    return (out.reshape(nb, seq, D_MODEL), k_out,
            v.reshape(nb, seq, ATT_HEADS, ATT_V_DIM), ssm_new, conv_new)


def _prep_params(i, w_in, conv_w, conv_b, dt_bias, a_log, d_skip, ssm_norm_w, lam_q1, lam_k1, lam_q2, lam_k2, subln_w,
                 w_out, wq_x, wo_x, g_pre_mix, g_post_mix, g_pre_x, g_post_x, g_pre_ffn, g_post_ffn,
                 w_gate, w_up, w_down):
    w = w_in[i]
    s1 = SSD_WIDTH + SSD_XBC + SSD_HEADS
    w_ssd = jnp.pad(w[:, :s1].astype(BF16), ((0, 0), (0, DT_PAD - SSD_HEADS)))
    w_qkv = w[:, s1:].astype(BF16)
    head_pad = lambda a: jnp.pad(a[i].astype(F32), (0, DT_PAD - SSD_HEADS)).reshape(1, DT_PAD)
    e = (jnp.arange(DT_PAD)[:, None] == (jnp.arange(SSD_WIDTH)[None, :] // SSD_HEAD_DIM)).astype(BF16)
    row = lambda a: a[i].astype(F32).reshape(1, -1)
    w_out_b, wq_b, wo_b = _cast_bf16([w_out[i], wq_x[i], wo_x[i]], steps=4)
    w_gate_b, w_up_b, w_down_b = _cast_bf16([w_gate[i], w_up[i], w_down[i]], steps=8)
    return {
        "w_in_ssd": w_ssd, "w_in_qkv": w_qkv, "conv_w": conv_w[i].astype(F32), "conv_b": row(conv_b),
        "dt_bias": head_pad(dt_bias), "a_log": head_pad(a_log), "d_skip": head_pad(d_skip),
        "ssm_norm_w": row(ssm_norm_w), "e2": jnp.concatenate([e, e], axis=0),
        "lam_vecs": jnp.stack([lam_q1[i], lam_k1[i], lam_q2[i], lam_k2[i]]).astype(F32), "subln_col": subln_w[i].astype(F32).reshape(ATT_V_DIM, 1),
        "w_out": w_out_b, "wq_x": wq_b, "wo_x": wo_b,
        "g_pre_mix": row(g_pre_mix), "g_post_mix": row(g_post_mix), "g_pre_x": row(g_pre_x), "g_post_x": row(g_post_x),
        "g_pre_ffn": row(g_pre_ffn), "g_post_ffn": row(g_post_ffn),
        "w_gate": w_gate_b, "w_up": w_up_b, "w_down": w_down_b,
    }


def kernel(x_prompt, x_sample, cache_attn_k, cache_attn_v, state_ssm, state_conv, cache_mem_k, cache_mem_v, mem_prompt, w_in, conv_w, conv_b, dt_bias, a_log, d_skip, ssm_norm_w, lam_q1, lam_k1, lam_q2, lam_k2, subln_w, w_out, g_mem, wq_x, wk_x, wv_x, wo_x, g_pre_mix, g_post_mix, g_pre_x, g_post_x, g_pre_ffn, g_post_ffn, w_gate, w_up, w_down):
    depth = w_in.shape[0]
    bp, sp, _ = x_prompt.shape
    bs, ss, _ = x_sample.shape
    hp, hs = x_prompt, x_sample
    outs = [[] for _ in range(10)]
    for i in range(depth):
        lam_init = 0.8 - 0.6 * math.exp(-0.3 * i)
        p = _prep_params(i, w_in, conv_w, conv_b, dt_bias, a_log, d_skip, ssm_norm_w, lam_q1, lam_k1, lam_q2, lam_k2,
                         subln_w, w_out, wq_x, wo_x, g_pre_mix, g_post_mix, g_pre_x, g_post_x, g_pre_ffn, g_post_ffn,
                         w_gate, w_up, w_down)
        mk, mv, mkb, mvb = _mem_kv(mem_prompt.reshape(bp * MEM_LEN, D_MODEL), g_mem[i].reshape(1, D_MODEL),
                                   wk_x[i], wv_x[i])
        hp, k_new, v_new, ssm_new, conv_new = _layer(
            hp, jnp.zeros((bp, SSD_CONV - 1, SSD_XBC), F32), jnp.zeros((bp, SSD_HEADS, SSD_HEAD_DIM, SSD_STATE), F32),
            None, None, mkb.reshape(bp, MEM_LEN, D_MODEL), mvb.reshape(bp, MEM_LEN, D_MODEL), lam_init, p,
            tm=512, tk=512, tq=512, ssd_q=256)
        for lst, val in zip(outs[:6], (k_new, v_new, ssm_new, conv_new,
                                       mk.reshape(bp, MEM_LEN, MEM_HEADS, MEM_HEAD_DIM),
                                       mv.reshape(bp, MEM_LEN, MEM_HEADS, MEM_HEAD_DIM))):
            lst.append(val)
        past = cache_attn_k.shape[2]
        hs, k_new, v_new, ssm_new, conv_new = _layer(
            hs, state_conv[i], state_ssm[i],
            cache_attn_k[i].transpose(0, 2, 3, 4, 1).reshape(bs, ATT_WIDTH, past),
            cache_attn_v[i].reshape(bs, past * ATT_HEADS, ATT_V_DIM),
            _mem_cache_rows(cache_mem_k[i]), _mem_cache_rows(cache_mem_v[i]), lam_init, p,
            tm=bs * ss, tk=0, tq=0, ssd_q=128)
        for lst, val in zip(outs[6:], (k_new, v_new, ssm_new, conv_new)):
            lst.append(val)
    return (hp, hs) + tuple(jnp.stack(o) for o in outs)
```

```python
import functools
import math

import jax
import jax.numpy as jnp
import numpy as np
from jax import lax
from jax.experimental import pallas as pl
from jax.experimental.pallas import tpu as pltpu

D_MODEL = 1024
CHUNK = 64
SSD_WIDTH = 512
SSD_HEAD_DIM = 64
SSD_HEADS = 8
SSD_GROUPS = 2
SSD_REP = 4
SSD_STATE = 128
SSD_CONV = 4
SSD_XBC = 1024
SSD_NORM_GROUP = 256
SSD_NORM_EPS = 1e-5
ATT_WIDTH = 512
ATT_HEAD_DIM = 64
ATT_HEADS = 4
ATT_V_DIM = 128
ATT_NORM_EPS = 1e-5
ROPE_THETA = 10000.0
MEM_LEN = 256
MEM_HEADS = 4
MEM_HEAD_DIM = 256
FFN_HIDDEN = 2816
NORM_EPS = 1e-6
LANES = 128
SUBLANES = 8
DT_PAD = LANES
IN_COLS_PADDED = SSD_WIDTH + SSD_XBC + 3 * ATT_WIDTH + DT_PAD
VMEM_LIMIT = 56 * 1024 * 1024
NEG_BIG = -1e30
MXU_TILE = 256
BF16_SUBLANES = 16
ATT_VT_ROWS = ATT_V_DIM + BF16_SUBLANES
ATT_SOFTMAX_ROWS = 128
ATT_PIPELINE_BUFFERS = 2
Q_SCALE = math.log2(math.e) / math.sqrt(ATT_HEAD_DIM)

F32 = jnp.float32
BF16 = jnp.bfloat16


def _const_spec(shape):
    return pl.BlockSpec(shape, lambda *_: (0,) * len(shape), pipeline_mode=pl.Buffered(1))


def _rms(x, g, eps):
    return x * lax.rsqrt(jnp.mean(x * x, axis=-1, keepdims=True) + eps) * g


def _silu(x):
    h = 0.5 * x
    return h + h * jnp.tanh(h)


def _dot(a, b):
    return jnp.dot(a, b, preferred_element_type=F32)


def _cast_kernel(*refs):
    n = len(refs) // 2
    for x_ref, o_ref in zip(refs[:n], refs[n:]):
        o_ref[...] = x_ref[...].astype(BF16)


def _cast_bf16(arrays, steps):
    specs = [pl.BlockSpec((a.shape[0] // steps, a.shape[1]), lambda i: (i, 0)) for a in arrays]
    assert all(a.shape[0] % (steps * BF16_SUBLANES) == 0 for a in arrays)
    return pl.pallas_call(
        _cast_kernel,
        grid=(steps,),
        in_specs=specs,
        out_specs=specs,
        out_shape=[jax.ShapeDtypeStruct(a.shape, BF16) for a in arrays],
        compiler_params=pltpu.CompilerParams(dimension_semantics=("arbitrary",), vmem_limit_bytes=VMEM_LIMIT),
        name="cast_bf16",
    )(*arrays)


def _mem_kv_kernel(mem_ref, g_ref, wk_ref, wv_ref, mk_ref, mv_ref, mkb_ref, mvb_ref):
    mn = _rms(mem_ref[...], g_ref[...], NORM_EPS).astype(BF16)
    mk = _dot(mn, wk_ref[...].astype(BF16))
    mv = _dot(mn, wv_ref[...].astype(BF16))
    mk_ref[...] = mk
    mv_ref[...] = mv
    mkb_ref[...] = mk.astype(BF16)
    mvb_ref[...] = mv.astype(BF16)


def _mem_kv(mem, g_mem, wk, wv):
    n = mem.shape[0]
    tm = MEM_LEN
    row = pl.BlockSpec((tm, D_MODEL), lambda i: (i, 0))
    return pl.pallas_call(
        _mem_kv_kernel,
        grid=(n // tm,),
        in_specs=[row, _const_spec((1, D_MODEL)), _const_spec((D_MODEL, D_MODEL)), _const_spec((D_MODEL, D_MODEL))],
        out_specs=[row, row, row, row],
        out_shape=[jax.ShapeDtypeStruct((n, D_MODEL), F32), jax.ShapeDtypeStruct((n, D_MODEL), F32),
                   jax.ShapeDtypeStruct((n, D_MODEL), BF16), jax.ShapeDtypeStruct((n, D_MODEL), BF16)],
        compiler_params=pltpu.CompilerParams(dimension_semantics=("arbitrary",), vmem_limit_bytes=VMEM_LIMIT),
        name="mem_kv",
    )(mem, g_mem, wk, wv)


def _in_proj_kernel(x_ref, g_ref, w_ref, wqkv_ref, cos_ref, sin_ref, cbuf_ref, cw_ref, cb_ref,
                    z_ref, u_ref, tail_ref, dt_ref, q_ref, k_ref, v_ref, qt_ref, kb_ref, vt_ref, acc_ref, xtail,
                    *, tm, tt, spt, tiles_per_seq):
    step = pl.program_id(0)
    tile = step - 1

    @pl.when(step == 0)
    def _():
        acc_ref[...] = jnp.zeros_like(acc_ref)
        xtail[...] = jnp.zeros_like(xtail)

    c_xbc, c_dt = SSD_WIDTH, SSD_WIDTH + SSD_XBC
    c_q = c_dt + DT_PAD
    c_k, c_v = c_q + ATT_WIDTH, c_q + 2 * ATT_WIDTH
    z_ref[...] = _silu(acc_ref[:, :SSD_WIDTH]).astype(BF16)
    rows = tm // spt
    row8 = lax.broadcasted_iota(jnp.int32, (SUBLANES, LANES), 0)
    first_tile_of_seq = tile % tiles_per_seq == 0
    for s_i in range(spt):
        r0 = s_i * rows
        tail_ref[s_i] = acc_ref[r0 + rows - SUBLANES:r0 + rows, c_xbc:c_dt]
        for cb in range(SSD_XBC // LANES):
            sl = slice(cb * LANES, (cb + 1) * LANES)
            piece = acc_ref[r0:r0 + rows, c_xbc + cb * LANES:c_xbc + (cb + 1) * LANES]
            if spt > 1:
                prev = cbuf_ref[s_i, :, sl]
            else:
                prev = jnp.where(first_tile_of_seq, cbuf_ref[0, :, sl], xtail[:, sl])
            conv_top = cb_ref[:, sl] + piece[0:SUBLANES] * cw_ref[SSD_CONV - 1:SSD_CONV, sl]
            conv_rest = cb_ref[:, sl] + piece[SUBLANES:] * cw_ref[SSD_CONV - 1:SSD_CONV, sl]
            for d in range(1, SSD_CONV):
                sh = pltpu.roll(piece, d, 0)
                tap = cw_ref[SSD_CONV - 1 - d:SSD_CONV - d, sl]
                conv_top = conv_top + jnp.where(row8 < d, pltpu.roll(prev, d, 0), sh[0:SUBLANES]) * tap
                conv_rest = conv_rest + sh[SUBLANES:] * tap
            u_ref[r0:r0 + rows, sl] = _silu(jnp.concatenate([conv_top, conv_rest], axis=0)).astype(BF16)
    if spt == 1:
        xtail[...] = acc_ref[tm - SUBLANES:, c_xbc:c_dt]
    dt_ref[...] = acc_ref[:, c_dt:c_dt + DT_PAD]
    if tt:
        for j in range(ATT_HEADS):
            v_ref[pl.ds(j, tm, stride=ATT_HEADS), :] = acc_ref[:, c_v + j * ATT_V_DIM:c_v + (j + 1) * ATT_V_DIM]
    else:
        v_ref[...] = acc_ref[:, c_v:c_v + ATT_WIDTH]

    cos = cos_ref[...]
    sin = sin_ref[...]
    first_half = (lax.broadcasted_iota(jnp.int32, (tm, LANES), 1) % ATT_HEAD_DIM) < (ATT_HEAD_DIM // 2)

    def rope(t):
        swapped = jnp.where(first_half, pltpu.roll(t, LANES - ATT_HEAD_DIM // 2, 1),
                            pltpu.roll(t, ATT_HEAD_DIM // 2, 1))
        return t * cos + swapped * sin

    for j in range(ATT_WIDTH // LANES):
        sl = slice(j * LANES, (j + 1) * LANES)
        qr = rope(acc_ref[:, c_q + j * LANES:c_q + (j + 1) * LANES]) * Q_SCALE
        kr = rope(acc_ref[:, c_k + j * LANES:c_k + (j + 1) * LANES])
        q_ref[:, sl] = qr.astype(BF16)
        if tt:
            k_ref[0, sl, :] = kr.T
            kb_ref[:, sl] = kr.astype(BF16)
            for c in range(tm // tt):
                qt_ref[c, sl, :] = qr[c * tt:(c + 1) * tt, :].T.astype(BF16)
                vt_ref[c, j * ATT_VT_ROWS:j * ATT_VT_ROWS + LANES, :] = acc_ref[
                    c * tt:(c + 1) * tt, c_v + j * LANES:c_v + (j + 1) * LANES].T.astype(BF16)
                vt_ref[c, j * ATT_VT_ROWS + LANES:(j + 1) * ATT_VT_ROWS, :] = jnp.ones((ATT_VT_ROWS - LANES, tt), BF16)
        else:
            k_ref[:, sl] = kr

    hn = _rms(x_ref[...], g_ref[...], NORM_EPS).astype(BF16)
    acc_ref[:, :c_q] = _dot(hn, w_ref[...])
    acc_ref[:, c_q:] = _dot(hn, wqkv_ref[...])


def _in_proj(x, g, w, wqkv, cos, sin, cbuf, cw, cb, *, seq, tm, tt):
    n = x.shape[0]
    nt = n // tm
    spt = max(1, tm // seq)
    tiles_per_seq = max(1, seq // tm)
    if spt > 1:
        cos, sin = jnp.tile(cos, (spt, 1)), jnp.tile(sin, (spt, 1))
    done = lambda i: jnp.maximum(i - 1, 0)
    row = lambda c: pl.BlockSpec((tm, c), lambda i: (done(i), 0))
    tab = pl.BlockSpec((tm, LANES), lambda i: (done(i) % tiles_per_seq, 0))
    if tt:
        k_spec = pl.BlockSpec((1, ATT_WIDTH, tm), lambda i: (done(i) // tiles_per_seq, 0, done(i) % tiles_per_seq))
        k_shape = jax.ShapeDtypeStruct((n // seq, ATT_WIDTH, seq), F32)
        v_spec = pl.BlockSpec((tm * ATT_HEADS, ATT_V_DIM), lambda i: (done(i), 0))
        v_shape = jax.ShapeDtypeStruct((n * ATT_HEADS, ATT_V_DIM), F32)
    else:
        k_spec, k_shape = row(ATT_WIDTH), jax.ShapeDtypeStruct((n, ATT_WIDTH), F32)
        v_spec, v_shape = row(ATT_WIDTH), jax.ShapeDtypeStruct((n, ATT_WIDTH), F32)
    out_specs = [row(SSD_WIDTH), row(SSD_XBC), pl.BlockSpec((spt, SUBLANES, SSD_XBC), lambda i: (done(i), 0, 0)),
                 row(DT_PAD), row(ATT_WIDTH), k_spec, v_spec]
    out_shape = [jax.ShapeDtypeStruct((n, SSD_WIDTH), BF16), jax.ShapeDtypeStruct((n, SSD_XBC), BF16),
                 jax.ShapeDtypeStruct((nt * spt, SUBLANES, SSD_XBC), F32), jax.ShapeDtypeStruct((n, DT_PAD), F32),
                 jax.ShapeDtypeStruct((n, ATT_WIDTH), BF16), k_shape, v_shape]
    if tt:
        tr = lambda r: pl.BlockSpec((tm // tt, r, tt), lambda i: (done(i), 0, 0))
        vt_rows = ATT_HEADS * ATT_VT_ROWS
        out_specs += [tr(ATT_WIDTH), row(ATT_WIDTH), tr(vt_rows)]
        out_shape += [jax.ShapeDtypeStruct((n // tt, ATT_WIDTH, tt), BF16), jax.ShapeDtypeStruct((n, ATT_WIDTH), BF16),
                      jax.ShapeDtypeStruct((n // tt, vt_rows, tt), BF16)]

    def body(*refs):
        refs, scratch = refs[:-2], refs[-2:]
        refs = refs + (None,) * (19 - len(refs))
        _in_proj_kernel(*refs, *scratch, tm=tm, tt=tt, spt=spt, tiles_per_seq=tiles_per_seq)

    outs = pl.pallas_call(
        body,
        grid=(nt + 1,),
        in_specs=[pl.BlockSpec((tm, D_MODEL), lambda i: (jnp.minimum(i, nt - 1), 0)),
                  _const_spec((1, D_MODEL)), _const_spec(w.shape), _const_spec(wqkv.shape), tab, tab,
                  pl.BlockSpec((spt, SUBLANES, SSD_XBC), lambda i: (done(i) // tiles_per_seq, 0, 0)),
                  _const_spec((SSD_CONV, SSD_XBC)), _const_spec((1, SSD_XBC))],
        out_specs=out_specs,
        out_shape=out_shape,
        scratch_shapes=[pltpu.VMEM((tm, IN_COLS_PADDED), F32), pltpu.VMEM((SUBLANES, SSD_XBC), F32)],
        compiler_params=pltpu.CompilerParams(dimension_semantics=("arbitrary",), vmem_limit_bytes=VMEM_LIMIT),
        name="in_proj",
    )(x, g, w, wqkv, cos, sin, cbuf, cw, cb)
    return list(outs) + [None] * (10 - len(outs))


def _expand_heads(x, e2):
    hi = x.astype(BF16)
    lo = (x - hi.astype(F32)).astype(BF16)
    return _dot(jnp.concatenate([hi, lo], axis=1), e2)


def _ssd_kernel(gate_ref, u_ref, dt_ref, h0_ref, dtb_ref, alog_ref, dsk_ref, nw_ref, e2_ref,
                y_ref, hout_ref, state, *, q, valid):
    c = pl.program_id(1)

    @pl.when(c == 0)
    def _():
        state[...] = h0_ref[0]

    xs = u_ref[:, :SSD_WIDTH].astype(F32)
    bm = u_ref[:, SSD_WIDTH:SSD_WIDTH + SSD_GROUPS * SSD_STATE]
    cmb = u_ref[:, SSD_WIDTH + SSD_GROUPS * SSD_STATE:]

    dtr = dt_ref[...] + dtb_ref[...]
    dt = jnp.maximum(dtr, 0.0) + jnp.log(1.0 + jnp.exp(-jnp.abs(dtr)))
    if valid is not None:
        row = lax.broadcasted_iota(jnp.int32, (q, DT_PAD), 0) + c * q
        dt = jnp.where(row < valid, dt, 0.0)
    a = -jnp.exp(alog_ref[...])
    ad = dt * a
    ri = lax.broadcasted_iota(jnp.int32, (q, q), 0)
    ci = lax.broadcasted_iota(jnp.int32, (q, q), 1)
    tril = ri >= ci
    tril_b = jnp.where(tril, 1.0, 0.0).astype(BF16)
    ad_hi = ad.astype(BF16)
    ad_r = ad - ad_hi.astype(F32)
    ad_mid = ad_r.astype(BF16)
    ad_lo = (ad_r - ad_mid.astype(F32)).astype(BF16)
    acum = _dot(tril_b, ad_hi) + _dot(tril_b, ad_mid) + _dot(tril_b, ad_lo)
    acum_t = acum.T
    tot = acum[q - 1:q, :]
    e2 = e2_ref[...]
    expanded = _expand_heads(jnp.concatenate([dt, dt * jnp.exp(tot - acum), jnp.exp(acum)], axis=0), e2)
    dtx = expanded[0:q]
    ddx = expanded[q:2 * q]
    eax = expanded[2 * q:3 * q]
    dsk = _expand_heads(jnp.broadcast_to(dsk_ref[...], (SUBLANES, DT_PAD)), e2)[0:1]

    xdt = (xs * dtx).astype(BF16)
    xdtd = (xs * ddx).astype(BF16)
    bm_t = bm.astype(F32).T.astype(BF16)
    gw = SSD_REP * SSD_HEAD_DIM
    stripe = lax.broadcasted_iota(jnp.int32, (q, gw), 1) // SSD_HEAD_DIM
    ys = []
    for g in range(SSD_GROUPS):
        cm_g = cmb[:, g * SSD_STATE:(g + 1) * SSD_STATE]
        bt_g = bm_t[g * SSD_STATE:(g + 1) * SSD_STATE, :]
        cbm = _dot(cm_g, bt_g)
        ms = []
        for r in range(SSD_REP):
            h = g * SSD_REP + r
            diff = acum[:, h:h + 1] - acum_t[h:h + 1, :]
            ms.append((cbm * jnp.exp(jnp.where(tril, diff, -jnp.inf))).astype(BF16))
        ydf = _dot(jnp.concatenate(ms, axis=0), xdt[:, g * gw:(g + 1) * gw])
        yd = ydf[0:q]
        for r in range(1, SSD_REP):
            yd = jnp.where(stripe == r, ydf[r * q:(r + 1) * q], yd)
        st = state[g]
        y_off = _dot(cm_g, st.astype(BF16)) * eax[:, g * gw:(g + 1) * gw]
        state[g] = st * eax[q - 1:q, g * gw:(g + 1) * gw] + _dot(bt_g, xdtd[:, g * gw:(g + 1) * gw])
        ys.append(yd + y_off)
    y = jnp.concatenate(ys, axis=1) + dsk * xs
    y = y * gate_ref[...].astype(F32)
    outs = []
    for g in range(SSD_WIDTH // SSD_NORM_GROUP):
        yg = y[:, g * SSD_NORM_GROUP:(g + 1) * SSD_NORM_GROUP]
        outs.append(yg * lax.rsqrt(jnp.mean(yg * yg, axis=-1, keepdims=True) + SSD_NORM_EPS))
    y_ref[...] = (jnp.concatenate(outs, axis=1) * nw_ref[...]).astype(BF16)

    @pl.when(c == pl.num_programs(1) - 1)
    def _():
        hout_ref[0] = state[...]


def _ssd(gate, u, dt, h0, dtb, alog, dsk, nw, e2, *, nb, seq, q, valid):
    nc = seq // q
    row = lambda c: pl.BlockSpec((q, c), lambda b, i: (b * nc + i, 0))
    gw = SSD_REP * SSD_HEAD_DIM
    st_spec = pl.BlockSpec((1, SSD_GROUPS, SSD_STATE, gw), lambda b, i: (b, 0, 0, 0))
    return pl.pallas_call(
        functools.partial(_ssd_kernel, q=q, valid=valid),
        grid=(nb, nc),
        in_specs=[row(SSD_WIDTH), row(SSD_XBC), row(DT_PAD), st_spec,
                  _const_spec((1, DT_PAD)), _const_spec((1, DT_PAD)), _const_spec((1, DT_PAD)),
                  _const_spec((1, SSD_WIDTH)), _const_spec((2 * DT_PAD, SSD_WIDTH))],
        out_specs=[row(SSD_WIDTH), st_spec],
        out_shape=[jax.ShapeDtypeStruct((nb * seq, SSD_WIDTH), BF16),
                   jax.ShapeDtypeStruct((nb, SSD_GROUPS, SSD_STATE, gw), F32)],
        scratch_shapes=[pltpu.VMEM((SSD_GROUPS, SSD_STATE, gw), F32)],
        compiler_params=pltpu.CompilerParams(dimension_semantics=("arbitrary", "arbitrary"),
                                             vmem_limit_bytes=VMEM_LIMIT),
        name="ssd",
    )(gate, u, dt, h0, dtb, alog, dsk, nw, e2)


def _column_max(x):
    while x.shape[0] > SUBLANES and x.shape[0] % (2 * SUBLANES) == 0:
        half = x.shape[0] // 2
        x = jnp.maximum(x[:half], x[half:])
    return jnp.max(x, axis=0, keepdims=True)


def _attn_tile_counts(i, *, tq, tk, past, kv_len, minimum=min):
    q_lo = past + i * tq
    q_hi = q_lo + tq - 1
    lim_lo = minimum((q_lo // CHUNK + 1) * CHUNK, kv_len)
    lim_hi = minimum((q_hi // CHUNK + 1) * CHUNK, kv_len)
    return lim_lo // tk, (lim_hi + tk - 1) // tk


def _diff_attn_kernel(qt_ref, qt_next_ref, k_ref, vt_ref, lam_ref, sw_ref, o_ref,
                      q2t_ref, s_ref, p_ref, m_ref, alpha_ref, acc_ref, *, tq, tk, past, kv_len, lam_init, nbuf, nkt):
    i = pl.program_id(2)
    w = 2 * tq
    row = lax.broadcasted_iota(jnp.int32, (ATT_V_DIM, tq), 0)
    n_full, _ = _attn_tile_counts(i, tq=tq, tk=tk, past=past, kv_len=kv_len, minimum=jnp.minimum)
    n_visits = n_full + 1

    def scores(j):
        kt = k_ref[pl.ds(pl.multiple_of(j * tk, tk), tk), :]
        return _dot(kt, q2t_ref[...])

    def visited_tile(v):
        return jnp.where(v == 0, n_full, jnp.maximum(v - 1, 0))

    ncb = w // LANES
    pw = p_ref.shape[-1]

    def store_scores(buf, s):
        for cb in range(ncb):
            s_ref[buf, cb] = s[:, cb * LANES:(cb + 1) * LANES]

    def stage_a(v, buf):
        store_scores(buf, scores(jnp.minimum(v - 1, n_full)))

    rows = min(tk, ATT_SOFTMAX_ROWS)

    def stage_b(buf):
        for cb in range(ncb):
            sl = slice(cb * LANES, (cb + 1) * LANES)
            m_old = m_ref[:, sl]
            m_new = m_old
            for r0 in range(0, tk, rows):
                m_new = jnp.maximum(m_new, _column_max(s_ref[buf, cb, r0:r0 + rows, :]))
            m_ref[:, sl] = m_new
            alpha_ref[buf, :, sl] = jnp.exp2(m_old - m_new)
            pl0 = (cb * LANES) % pw
            for r0 in range(0, tk, rows):
                p_ref[buf, cb * LANES // pw, r0:r0 + rows, pl0:pl0 + LANES] = jnp.exp2(
                    (s_ref[buf, cb, r0:r0 + rows, :] - m_new).astype(BF16))

    def stage_c(v, buf):
        vt = vt_ref[visited_tile(v)]
        for pb in range(w // pw):
            sl = slice(pb * pw, (pb + 1) * pw)
            acc_ref[:, sl] = alpha_ref[buf, :, sl] * acc_ref[:, sl] + _dot(vt, p_ref[buf, pb])

    def first_visit(qt, tile):
        zero = jnp.zeros_like(qt)
        q2t_ref[:, 0:tq] = jnp.where(row < ATT_HEAD_DIM, qt, zero)
        q2t_ref[:, tq:w] = jnp.where(row >= ATT_HEAD_DIM, qt, zero)
        nf, _ = _attn_tile_counts(tile, tq=tq, tk=tk, past=past, kv_len=kv_len, minimum=jnp.minimum)
        nf = jnp.minimum(nf, nkt - 1)
        qchunk = (past + tile * tq + lax.broadcasted_iota(jnp.int32, (1, w), 1) % tq) // CHUNK
        s_part = scores(nf)
        s_masked = []
        for kb in range(tk // CHUNK):
            k0 = nf * tk + kb * CHUNK
            kchunk = jnp.where(k0 < kv_len, k0 // CHUNK, jnp.iinfo(jnp.int32).max)
            s_masked.append(jnp.where(kchunk <= qchunk, s_part[kb * CHUNK:(kb + 1) * CHUNK], NEG_BIG))
        store_scores(0, jnp.concatenate(s_masked, axis=0))

    @pl.when(i == 0)
    def _():
        first_visit(qt_ref[0], i)

    m_ref[...] = jnp.full_like(m_ref, NEG_BIG)
    acc_ref[...] = jnp.zeros_like(acc_ref)
    p_ref[nbuf - 1] = jnp.zeros(p_ref.shape[1:], BF16)
    alpha_ref[nbuf - 1] = jnp.ones(alpha_ref.shape[1:], F32)

    def visits(v0, count, prefetch_last):
        for r in range(count):
            prefetch = r + 1 < count or prefetch_last
            if prefetch and nbuf > 1:
                stage_a(v0 + r + 1, (r + 1) % nbuf)
            stage_c(v0 + r - 1, (r - 1) % nbuf)
            stage_b(r)
            if prefetch and nbuf == 1:
                stage_a(v0 + r + 1, (r + 1) % nbuf)

    def trip(u, carry):
        visits(nbuf * u, nbuf, True)
        return carry

    lax.fori_loop(0, n_visits // nbuf, trip, 0)
    for rem in range(nbuf):

        @pl.when(n_visits % nbuf == rem)
        def _():
            visits(n_visits - rem, rem, False)
            stage_c(n_visits - 1, (rem - 1) % nbuf)

    first_visit(qt_next_ref[0], i + 1)

    lam_v = lam_ref[...]
    lam = (jnp.exp(jnp.sum(lam_v[0:1] * lam_v[1:2], axis=-1, keepdims=True))
           - jnp.exp(jnp.sum(lam_v[2:3] * lam_v[3:4], axis=-1, keepdims=True)) + lam_init)
    acc = acc_ref[...]
    o = acc[:ATT_V_DIM] * (1.0 / acc[ATT_V_DIM:ATT_V_DIM + 1])
    o = o[:, :tq] - lam * o[:, tq:]
    o = o * lax.rsqrt(jnp.mean(o * o, axis=0, keepdims=True) + ATT_NORM_EPS) * (sw_ref[...] * (1.0 - lam_init))
    o_ref[...] = o.T.astype(BF16)


def _diff_attn(qt, k, vt, lam_vecs, subln_col, *, nb, tq, tk, past, kv_len, lam_init, nbuf):
    nq = qt.shape[0] // nb
    nkt = vt.shape[0] // nb
    assert tk % CHUNK == 0 and kv_len % CHUNK == 0
    for i in range(nq):
        n_full, n_end = _attn_tile_counts(i, tq=tq, tk=tk, past=past, kv_len=kv_len)
        assert n_end - n_full == 1 and n_end <= nkt, (i, n_full, n_end)
    w = 2 * tq
    return pl.pallas_call(
        functools.partial(_diff_attn_kernel, tq=tq, tk=tk, past=past, kv_len=kv_len, lam_init=lam_init, nbuf=nbuf,
                          nkt=nkt),
        grid=(nb, ATT_HEADS, nq),
        in_specs=[pl.BlockSpec((1, ATT_V_DIM, tq), lambda b, h, i: (b * nq + i, h, 0)),
                  pl.BlockSpec((1, ATT_V_DIM, tq), lambda b, h, i: (b * nq + jnp.minimum(i + 1, nq - 1), h, 0)),
                  pl.BlockSpec((nkt * tk, ATT_V_DIM), lambda b, h, i: (b, h)),
                  pl.BlockSpec((nkt, ATT_VT_ROWS, tk), lambda b, h, i: (b, h, 0)),
                  _const_spec((4, ATT_HEAD_DIM)), _const_spec((ATT_V_DIM, 1))],
        out_specs=pl.BlockSpec((tq, ATT_V_DIM), lambda b, h, i: (b * nq + i, h)),
        out_shape=jax.ShapeDtypeStruct((nb * nq * tq, ATT_WIDTH), BF16),
        scratch_shapes=[pltpu.VMEM((ATT_V_DIM, w), BF16), pltpu.VMEM((nbuf, w // LANES, tk, LANES), F32),
                        pltpu.VMEM((nbuf, w // MXU_TILE, tk, MXU_TILE), BF16), pltpu.VMEM((1, w), F32),
                        pltpu.VMEM((nbuf, 1, w), F32),
                        pltpu.VMEM((ATT_VT_ROWS, w), F32)],
        compiler_params=pltpu.CompilerParams(dimension_semantics=("arbitrary", "arbitrary", "arbitrary"),
                                             vmem_limit_bytes=VMEM_LIMIT),
        name="diff_attn",
    )(qt, qt, k, vt, lam_vecs, subln_col)


def _decode_attn_kernel(q_ref, kn_ref, vn_ref, kc_ref, vc_ref, lam_ref, sw_ref, o_ref, *, seq, past, lam_init):
    lam_v = lam_ref[...]
    lam = (jnp.exp(jnp.sum(lam_v[0:1] * lam_v[1:2], axis=-1, keepdims=True))
           - jnp.exp(jnp.sum(lam_v[2:3] * lam_v[3:4], axis=-1, keepdims=True)) + lam_init)
    npad = LANES
    lane = lax.broadcasted_iota(jnp.int32, (seq, LANES), 1)
    qchunk = (past + lax.broadcasted_iota(jnp.int32, (2 * seq, 1), 0) % seq) // CHUNK
    kpos_p = lax.broadcasted_iota(jnp.int32, (2 * seq, past), 1)
    kpos_n = lax.broadcasted_iota(jnp.int32, (2 * seq, npad), 1)
    vis_p = kpos_p // CHUNK <= qchunk
    vis_n = ((past + kpos_n) // CHUNK <= qchunk) & (kpos_n < seq)
    pad_rows = jnp.zeros((npad - seq, LANES), BF16)
    for h in range(ATT_HEADS):
        sl = slice(h * LANES, (h + 1) * LANES)
        qh = q_ref[:, sl]
        zero = jnp.zeros_like(qh)
        q2 = jnp.concatenate([jnp.where(lane < ATT_HEAD_DIM, qh, zero), jnp.where(lane >= ATT_HEAD_DIM, qh, zero)], axis=0)
        s_p = jnp.where(vis_p, _dot(q2, kc_ref[0, sl, :].astype(BF16)), NEG_BIG)
        kn = jnp.concatenate([kn_ref[:, sl].astype(BF16), pad_rows], axis=0)
        s_n = lax.dot_general(q2, kn, (((1,), (1,)), ((), ())), preferred_element_type=F32)
        s_n = jnp.where(vis_n, s_n, NEG_BIG)
        m = jnp.maximum(jnp.max(s_p, axis=-1, keepdims=True), jnp.max(s_n, axis=-1, keepdims=True))
        p_p = jnp.exp2(s_p - m)
        p_n = jnp.exp2(s_n - m)
        l = jnp.sum(p_p, axis=-1, keepdims=True) + jnp.sum(p_n, axis=-1, keepdims=True)
        vh = vc_ref[0, pl.ds(h, past, stride=ATT_HEADS), :].astype(BF16)
        vn = jnp.concatenate([vn_ref[:, sl].astype(BF16), pad_rows], axis=0)
        o = (_dot(p_p.astype(BF16), vh) + _dot(p_n.astype(BF16), vn)) / l
        o = o[:seq] - lam * o[seq:]
        o = o * lax.rsqrt(jnp.mean(o * o, axis=-1, keepdims=True) + ATT_NORM_EPS) * (sw_ref[...] * (1.0 - lam_init))
        o_ref[:, sl] = o.astype(BF16)


def _decode_attn(q, k_new, v_new, kt_cache, v_cache, lam_vecs, subln_row, *, nb, seq, past, lam_init):
    row = pl.BlockSpec((seq, ATT_WIDTH), lambda b: (b, 0))
    return pl.pallas_call(
        functools.partial(_decode_attn_kernel, seq=seq, past=past, lam_init=lam_init),
        grid=(nb,),
        in_specs=[row, row, row,
                  pl.BlockSpec((1, ATT_WIDTH, past), lambda b: (b, 0, 0)),
                  pl.BlockSpec((1, past * ATT_HEADS, ATT_V_DIM), lambda b: (b, 0, 0)),
                  _const_spec((4, ATT_HEAD_DIM)), _const_spec((1, ATT_V_DIM))],
        out_specs=row,
        out_shape=jax.ShapeDtypeStruct((nb * seq, ATT_WIDTH), BF16),
        compiler_params=pltpu.CompilerParams(dimension_semantics=("arbitrary",), vmem_limit_bytes=VMEM_LIMIT),
        name="decode_attn",
    )(q, k_new, v_new, kt_cache, v_cache, lam_vecs, subln_row)


def _post_mix_kernel(x_ref, ys_ref, ya_ref, mk_ref, mv_ref, wo_ref, wq_ref, wox_ref,
                     g1_ref, g2_ref, g3_ref, h_ref, *, spt, rows, cached_mem):
    lane_blocks = MEM_HEAD_DIM // LANES

    def mem_head(ref, s_i, hd):
        if not cached_mem:
            return ref[s_i, :, hd * MEM_HEAD_DIM:(hd + 1) * MEM_HEAD_DIM]
        parts = [ref[s_i, pl.ds(cb * MEM_HEADS + hd, MEM_LEN, stride=lane_blocks * MEM_HEADS), :].astype(BF16)
                 for cb in range(lane_blocks)]
        return jnp.concatenate(parts, axis=1)

    mix = _dot(ys_ref[...], wo_ref[0:SSD_WIDTH, :]) + _dot(ya_ref[...], wo_ref[SSD_WIDTH:, :])
    h = x_ref[...] + _rms(mix, g1_ref[...], NORM_EPS)
    qn = _rms(h, g2_ref[...], NORM_EPS).astype(BF16)
    qx = (_dot(qn, wq_ref[...]) * (math.log2(math.e) / math.sqrt(MEM_HEAD_DIM))).astype(BF16)
    ox_seqs = []
    for s_i in range(spt):
        qs = qx[s_i * rows:(s_i + 1) * rows]
        oxs = []
        for hd in range(MEM_HEADS):
            sl = slice(hd * MEM_HEAD_DIM, (hd + 1) * MEM_HEAD_DIM)
            s = lax.dot_general(qs[:, sl], mem_head(mk_ref, s_i, hd), (((1,), (1,)), ((), ())),
                                preferred_element_type=F32)
            p = jnp.exp2(s - jnp.max(s, axis=-1, keepdims=True))
            ox = _dot(p.astype(BF16), mem_head(mv_ref, s_i, hd)) * (1.0 / jnp.sum(p, axis=-1, keepdims=True))
            oxs.append(ox.astype(BF16))
        ox_seqs.append(jnp.concatenate(oxs, axis=1))
    ox_all = ox_seqs[0] if spt == 1 else jnp.concatenate(ox_seqs, axis=0)
    o2 = _dot(ox_all, wox_ref[...])
    h_ref[...] = h + _rms(o2, g3_ref[...], NORM_EPS)


def _post_mix(x, ys, ya, mk, mv, w_out, wq, wox, g1, g2, g3, *, seq, tm):
    n = x.shape[0]
    spt = max(1, tm // seq)
    tiles_per_seq = max(1, seq // tm)
    row = lambda c: pl.BlockSpec((tm, c), lambda i: (i, 0))
    cached_mem = mk.dtype == F32
    mem = pl.BlockSpec((spt,) + mk.shape[1:], lambda i: (i // tiles_per_seq, 0, 0))
    wspec = _const_spec((D_MODEL, D_MODEL))
    gspec = _const_spec((1, D_MODEL))
    return pl.pallas_call(
        functools.partial(_post_mix_kernel, spt=spt, rows=tm // spt, cached_mem=cached_mem),
        grid=(n // tm,),
        in_specs=[row(D_MODEL), row(SSD_WIDTH), row(ATT_WIDTH), mem, mem, wspec, wspec, wspec, gspec, gspec, gspec],
        out_specs=row(D_MODEL),
        out_shape=jax.ShapeDtypeStruct((n, D_MODEL), F32),
        compiler_params=pltpu.CompilerParams(dimension_semantics=("arbitrary",), vmem_limit_bytes=VMEM_LIMIT),
        name="post_mix",
    )(x, ys, ya, mk, mv, w_out, wq, wox, g1, g2, g3)


def _ffn_kernel(h_ref, wg_ref, wu_ref, wd_ref, g1_ref, g2_ref, o_ref):
    h = h_ref[...]
    hn = _rms(h, g1_ref[...], NORM_EPS).astype(BF16)
    act = (_silu(_dot(hn, wg_ref[...])) * _dot(hn, wu_ref[...])).astype(BF16)
    f = _dot(act, wd_ref[...])
    o_ref[...] = h + _rms(f, g2_ref[...], NORM_EPS)


def _ffn(h, wg, wu, wd, g1, g2, *, tm):
    n = h.shape[0]
    row = pl.BlockSpec((tm, D_MODEL), lambda i: (i, 0))
    return pl.pallas_call(
        _ffn_kernel,
        grid=(n // tm,),
        in_specs=[row, _const_spec((D_MODEL, FFN_HIDDEN)), _const_spec((D_MODEL, FFN_HIDDEN)),
                  _const_spec((FFN_HIDDEN, D_MODEL)), _const_spec((1, D_MODEL)), _const_spec((1, D_MODEL))],
        out_specs=row,
        out_shape=jax.ShapeDtypeStruct((n, D_MODEL), F32),
        compiler_params=pltpu.CompilerParams(dimension_semantics=("arbitrary",), vmem_limit_bytes=VMEM_LIMIT),
        name="ffn",
    )(h, wg, wu, wd, g1, g2)


def _rope_tables(past, seq):
    inv = np.power(ROPE_THETA, -np.arange(0, ATT_HEAD_DIM, 2, dtype=np.float64) / ATT_HEAD_DIM)
    ang = (past + np.arange(seq, dtype=np.float64))[:, None] * inv[None, :]
    cos, sin = np.cos(ang), np.sin(ang)
    reps = LANES // ATT_HEAD_DIM
    cos_t = np.tile(np.concatenate([cos, cos], axis=-1), (1, reps)).astype(np.float32)
    sin_t = np.tile(np.concatenate([-sin, sin], axis=-1), (1, reps)).astype(np.float32)
    return jnp.asarray(cos_t), jnp.asarray(sin_t)


def _mem_cache_rows(c):
    b = c.shape[0]
    c = c.reshape(b, MEM_LEN, MEM_HEADS, MEM_HEAD_DIM // LANES, LANES)
    return c.transpose(0, 1, 3, 2, 4).reshape(b, MEM_LEN * (MEM_HEAD_DIM // LANES) * MEM_HEADS, LANES)


def _state_to_kernel_layout(s):
    b = s.shape[0]
    s = s.reshape(b, SSD_GROUPS, SSD_REP, SSD_HEAD_DIM, SSD_STATE)
    return s.transpose(0, 1, 4, 2, 3).reshape(b, SSD_GROUPS, SSD_STATE, SSD_REP * SSD_HEAD_DIM)


def _state_from_kernel_layout(s):
    b = s.shape[0]
    s = s.reshape(b, SSD_GROUPS, SSD_STATE, SSD_REP, SSD_HEAD_DIM)
    return s.transpose(0, 1, 3, 4, 2).reshape(b, SSD_HEADS, SSD_HEAD_DIM, SSD_STATE)


def _layer(x, conv_buf, ssm0, kt_past, v_past, mem_kb, mem_vb, lam_init, p, *, tm, tk, tq, ssd_q):
    nb, seq, _ = x.shape
    n = nb * seq
    past = 0 if kt_past is None else kt_past.shape[2]
    xf = x.reshape(n, D_MODEL)
    cos, sin = _rope_tables(past, seq)
    no_history = kt_past is None
    assert tq == tk or not no_history
    cbuf = jnp.pad(conv_buf.astype(F32), ((0, 0), (SUBLANES - (SSD_CONV - 1), 0), (0, 0)))
    gate, u, tail, dt, q, k, v, qt, kb, vt = _in_proj(xf, p["g_pre_mix"], p["w_in_ssd"], p["w_in_qkv"], cos, sin, cbuf,
                                                       p["conv_w"], p["conv_b"], seq=seq, tm=tm,
                                                       tt=tk if no_history else 0)

    seq_pad = -(-seq // ssd_q) * ssd_q
    if seq_pad != seq:
        pad = lambda a: jnp.pad(a.reshape(nb, seq, -1), ((0, 0), (0, seq_pad - seq), (0, 0))).reshape(nb * seq_pad, -1)
        gate_s, u_s, dt_s = pad(gate), pad(u), pad(dt)
    else:
        gate_s, u_s, dt_s = gate, u, dt
    y_ssd, h_new = _ssd(gate_s, u_s, dt_s, _state_to_kernel_layout(ssm0.astype(F32)),
                        p["dt_bias"], p["a_log"], p["d_skip"], p["ssm_norm_w"], p["e2"],
                        nb=nb, seq=seq_pad, q=ssd_q, valid=None if seq_pad == seq else seq)
    if seq_pad != seq:
        y_ssd = y_ssd.reshape(nb, seq_pad, SSD_WIDTH)[:, :seq].reshape(n, SSD_WIDTH)
    ssm_new = _state_from_kernel_layout(h_new)
    ext_tail = tail.reshape(nb, -1, SUBLANES, SSD_XBC)[:, -1]
    if seq >= SSD_CONV - 1:
        conv_new = ext_tail[:, SUBLANES - (SSD_CONV - 1):]
    else:
        conv_new = jnp.concatenate([conv_buf.astype(F32), ext_tail[:, SUBLANES - seq:]], axis=1)[:, -(SSD_CONV - 1):]

    if no_history:
        kv_len = seq
        y_att = _diff_attn(qt, kb, vt, p["lam_vecs"], p["subln_col"], nb=nb, tq=tq, tk=tk,
                           past=past, kv_len=kv_len, lam_init=lam_init, nbuf=ATT_PIPELINE_BUFFERS)
    else:
        y_att = _decode_attn(q, k, v, kt_past, v_past, p["lam_vecs"], p["subln_col"].reshape(1, ATT_V_DIM),
                             nb=nb, seq=seq, past=past, lam_init=lam_init)

    h = _post_mix(xf, y_ssd, y_att, mem_kb, mem_vb, p["w_out"], p["wq_x"], p["wo_x"],
                  p["g_post_mix"], p["g_pre_x"], p["g_post_x"], seq=seq, tm=tm)
    out = _ffn(h, p["w_gate"], p["w_up"], p["w_down"], p["g_pre_ffn"], p["g_post_ffn"], tm=tm)
    if no_history:
        k_out = k.reshape(nb, ATT_HEADS, 2, ATT_HEAD_DIM, seq).transpose(0, 4, 1, 2, 3)
    else:
        k_out = k.reshape(nb, seq, ATT_HEADS, 2, ATT_HEAD_DIM)
    return (out.reshape(nb, seq, D_MODEL), k_out,
            v.reshape(nb, seq, ATT_HEADS, ATT_V_DIM), ssm_new, conv_new)


def _prep_params(i, w_in, conv_w, conv_b, dt_bias, a_log, d_skip, ssm_norm_w, lam_q1, lam_k1, lam_q2, lam_k2, subln_w,
                 w_out, wq_x, wo_x, g_pre_mix, g_post_mix, g_pre_x, g_post_x, g_pre_ffn, g_post_ffn,
                 w_gate, w_up, w_down):
    w = w_in[i]
    s1 = SSD_WIDTH + SSD_XBC + SSD_HEADS
    w_ssd = jnp.pad(w[:, :s1].astype(BF16), ((0, 0), (0, DT_PAD - SSD_HEADS)))
    w_qkv = w[:, s1:].astype(BF16)
    head_pad = lambda a: jnp.pad(a[i].astype(F32), (0, DT_PAD - SSD_HEADS)).reshape(1, DT_PAD)
    e = (jnp.arange(DT_PAD)[:, None] == (jnp.arange(SSD_WIDTH)[None, :] // SSD_HEAD_DIM)).astype(BF16)
    row = lambda a: a[i].astype(F32).reshape(1, -1)
    w_out_b, wq_b, wo_b = _cast_bf16([w_out[i], wq_x[i], wo_x[i]], steps=4)
    w_gate_b, w_up_b, w_down_b = _cast_bf16([w_gate[i], w_up[i], w_down[i]], steps=8)
    return {
        "w_in_ssd": w_ssd, "w_in_qkv": w_qkv, "conv_w": conv_w[i].astype(F32), "conv_b": row(conv_b),
        "dt_bias": head_pad(dt_bias), "a_log": head_pad(a_log), "d_skip": head_pad(d_skip),
        "ssm_norm_w": row(ssm_norm_w), "e2": jnp.concatenate([e, e], axis=0),
        "lam_vecs": jnp.stack([lam_q1[i], lam_k1[i], lam_q2[i], lam_k2[i]]).astype(F32), "subln_col": subln_w[i].astype(F32).reshape(ATT_V_DIM, 1),
        "w_out": w_out_b, "wq_x": wq_b, "wo_x": wo_b,
        "g_pre_mix": row(g_pre_mix), "g_post_mix": row(g_post_mix), "g_pre_x": row(g_pre_x), "g_post_x": row(g_post_x),
        "g_pre_ffn": row(g_pre_ffn), "g_post_ffn": row(g_post_ffn),
        "w_gate": w_gate_b, "w_up": w_up_b, "w_down": w_down_b,
    }


def kernel(x_prompt, x_sample, cache_attn_k, cache_attn_v, state_ssm, state_conv, cache_mem_k, cache_mem_v, mem_prompt, w_in, conv_w, conv_b, dt_bias, a_log, d_skip, ssm_norm_w, lam_q1, lam_k1, lam_q2, lam_k2, subln_w, w_out, g_mem, wq_x, wk_x, wv_x, wo_x, g_pre_mix, g_post_mix, g_pre_x, g_post_x, g_pre_ffn, g_post_ffn, w_gate, w_up, w_down):
    depth = w_in.shape[0]
    bp, sp, _ = x_prompt.shape
    bs, ss, _ = x_sample.shape
    hp, hs = x_prompt, x_sample
    outs = [[] for _ in range(10)]
    for i in range(depth):
        lam_init = 0.8 - 0.6 * math.exp(-0.3 * i)
        p = _prep_params(i, w_in, conv_w, conv_b, dt_bias, a_log, d_skip, ssm_norm_w, lam_q1, lam_k1, lam_q2, lam_k2,
                         subln_w, w_out, wq_x, wo_x, g_pre_mix, g_post_mix, g_pre_x, g_post_x, g_pre_ffn, g_post_ffn,
                         w_gate, w_up, w_down)
        mk, mv, mkb, mvb = _mem_kv(mem_prompt.reshape(bp * MEM_LEN, D_MODEL), g_mem[i].reshape(1, D_MODEL),
                                   wk_x[i], wv_x[i])
        hp, k_new, v_new, ssm_new, conv_new = _layer(
            hp, jnp.zeros((bp, SSD_CONV - 1, SSD_XBC), F32), jnp.zeros((bp, SSD_HEADS, SSD_HEAD_DIM, SSD_STATE), F32),
            None, None, mkb.reshape(bp, MEM_LEN, D_MODEL), mvb.reshape(bp, MEM_LEN, D_MODEL), lam_init, p,
            tm=512, tk=512, tq=512, ssd_q=256)
        for lst, val in zip(outs[:6], (k_new, v_new, ssm_new, conv_new,
                                       mk.reshape(bp, MEM_LEN, MEM_HEADS, MEM_HEAD_DIM),
                                       mv.reshape(bp, MEM_LEN, MEM_HEADS, MEM_HEAD_DIM))):
            lst.append(val)
        past = cache_attn_k.shape[2]
        hs, k_new, v_new, ssm_new, conv_new = _layer(
            hs, state_conv[i], state_ssm[i],
            cache_attn_k[i].transpose(0, 2, 3, 4, 1).reshape(bs, ATT_WIDTH, past),
            cache_attn_v[i].reshape(bs, past * ATT_HEADS, ATT_V_DIM),
            _mem_cache_rows(cache_mem_k[i]), _mem_cache_rows(cache_mem_v[i]), lam_init, p,
            tm=bs * ss, tk=0, tq=0, ssd_q=128)
        for lst, val in zip(outs[6:], (k_new, v_new, ssm_new, conv_new)):
            lst.append(val)
    return (hp, hs) + tuple(jnp.stack(o) for o in outs)
```

```python
import functools
import math

import jax
import jax.numpy as jnp
import numpy as np
from jax import lax
from jax.experimental import pallas as pl
from jax.experimental.pallas import tpu as pltpu

D_MODEL = 1024
CHUNK = 64
SSD_WIDTH = 512
SSD_HEAD_DIM = 64
SSD_HEADS = 8
SSD_GROUPS = 2
SSD_REP = 4
SSD_STATE = 128
SSD_CONV = 4
SSD_XBC = 1024
SSD_NORM_GROUP = 256
SSD_NORM_EPS = 1e-5
ATT_WIDTH = 512
ATT_HEAD_DIM = 64
ATT_HEADS = 4
ATT_V_DIM = 128
ATT_NORM_EPS = 1e-5
ROPE_THETA = 10000.0
MEM_LEN = 256
MEM_HEADS = 4
MEM_HEAD_DIM = 256
FFN_HIDDEN = 2816
NORM_EPS = 1e-6
LANES = 128
SUBLANES = 8
DT_PAD = LANES
IN_COLS_PADDED = SSD_WIDTH + SSD_XBC + 3 * ATT_WIDTH + DT_PAD
VMEM_LIMIT = 56 * 1024 * 1024
NEG_BIG = -1e30
MXU_TILE = 256
BF16_SUBLANES = 16
ATT_VT_ROWS = ATT_V_DIM + BF16_SUBLANES
SSD_CHUNKS_PER_STEP = 4
ATT_SOFTMAX_ROWS = 128
ATT_TILES_PER_STEP = 2
ATT_PIPELINE_BUFFERS = 2
Q_SCALE = math.log2(math.e) / math.sqrt(ATT_HEAD_DIM)

F32 = jnp.float32
BF16 = jnp.bfloat16


def _const_spec(shape):
    return pl.BlockSpec(shape, lambda *_: (0,) * len(shape), pipeline_mode=pl.Buffered(1))


def _rms(x, g, eps):
    return x * lax.rsqrt(jnp.mean(x * x, axis=-1, keepdims=True) + eps) * g


def _silu(x):
    h = 0.5 * x
    return h + h * jnp.tanh(h)


def _dot(a, b):
    return jnp.dot(a, b, preferred_element_type=F32)


def _cast_kernel(*refs):
    n = len(refs) // 2
    for x_ref, o_ref in zip(refs[:n], refs[n:]):
        o_ref[...] = x_ref[...].astype(BF16)


def _cast_bf16(arrays, steps):
    specs = [pl.BlockSpec((a.shape[0] // steps, a.shape[1]), lambda i: (i, 0)) for a in arrays]
    assert all(a.shape[0] % (steps * BF16_SUBLANES) == 0 for a in arrays)
    return pl.pallas_call(
        _cast_kernel,
        grid=(steps,),
        in_specs=specs,
        out_specs=specs,
        out_shape=[jax.ShapeDtypeStruct(a.shape, BF16) for a in arrays],
        compiler_params=pltpu.CompilerParams(dimension_semantics=("arbitrary",), vmem_limit_bytes=VMEM_LIMIT),
        name="cast_bf16",
    )(*arrays)


def _mem_kv_kernel(mem_ref, g_ref, wk_ref, wv_ref, mk_ref, mv_ref, mkb_ref, mvb_ref):
    mn = _rms(mem_ref[...], g_ref[...], NORM_EPS).astype(BF16)
    mk = _dot(mn, wk_ref[...].astype(BF16))
    mv = _dot(mn, wv_ref[...].astype(BF16))
    mk_ref[...] = mk
    mv_ref[...] = mv
    mkb_ref[...] = mk.astype(BF16)
    mvb_ref[...] = mv.astype(BF16)


def _mem_kv(mem, g_mem, wk, wv):
    n = mem.shape[0]
    tm = MEM_LEN
    row = pl.BlockSpec((tm, D_MODEL), lambda i: (i, 0))
    return pl.pallas_call(
        _mem_kv_kernel,
        grid=(n // tm,),
        in_specs=[row, _const_spec((1, D_MODEL)), _const_spec((D_MODEL, D_MODEL)), _const_spec((D_MODEL, D_MODEL))],
        out_specs=[row, row, row, row],
        out_shape=[jax.ShapeDtypeStruct((n, D_MODEL), F32), jax.ShapeDtypeStruct((n, D_MODEL), F32),
                   jax.ShapeDtypeStruct((n, D_MODEL), BF16), jax.ShapeDtypeStruct((n, D_MODEL), BF16)],
        compiler_params=pltpu.CompilerParams(dimension_semantics=("arbitrary",), vmem_limit_bytes=VMEM_LIMIT),
        name="mem_kv",
    )(mem, g_mem, wk, wv)


def _in_proj_kernel(x_ref, g_ref, w_ref, wqkv_ref, cos_ref, sin_ref, cbuf_ref, cw_ref, cb_ref,
                    z_ref, u_ref, tail_ref, dt_ref, q_ref, k_ref, v_ref, qt_ref, kb_ref, vt_ref, acc_ref, xtail,
                    *, tm, tt, spt, tiles_per_seq):
    step = pl.program_id(0)
    tile = step - 1

    @pl.when(step == 0)
    def _():
        acc_ref[...] = jnp.zeros_like(acc_ref)
        xtail[...] = jnp.zeros_like(xtail)

    c_xbc, c_dt = SSD_WIDTH, SSD_WIDTH + SSD_XBC
    c_q = c_dt + DT_PAD
    c_k, c_v = c_q + ATT_WIDTH, c_q + 2 * ATT_WIDTH
    z_ref[...] = _silu(acc_ref[:, :SSD_WIDTH]).astype(BF16)
    rows = tm // spt
    row8 = lax.broadcasted_iota(jnp.int32, (SUBLANES, LANES), 0)
    first_tile_of_seq = tile % tiles_per_seq == 0
    for s_i in range(spt):
        r0 = s_i * rows
        tail_ref[s_i] = acc_ref[r0 + rows - SUBLANES:r0 + rows, c_xbc:c_dt]
        for cb in range(SSD_XBC // LANES):
            sl = slice(cb * LANES, (cb + 1) * LANES)
            piece = acc_ref[r0:r0 + rows, c_xbc + cb * LANES:c_xbc + (cb + 1) * LANES]
            if spt > 1:
                prev = cbuf_ref[s_i, :, sl]
            else:
                prev = jnp.where(first_tile_of_seq, cbuf_ref[0, :, sl], xtail[:, sl])
            conv_top = cb_ref[:, sl] + piece[0:SUBLANES] * cw_ref[SSD_CONV - 1:SSD_CONV, sl]
            conv_rest = cb_ref[:, sl] + piece[SUBLANES:] * cw_ref[SSD_CONV - 1:SSD_CONV, sl]
            for d in range(1, SSD_CONV):
                sh = pltpu.roll(piece, d, 0)
                tap = cw_ref[SSD_CONV - 1 - d:SSD_CONV - d, sl]
                conv_top = conv_top + jnp.where(row8 < d, pltpu.roll(prev, d, 0), sh[0:SUBLANES]) * tap
                conv_rest = conv_rest + sh[SUBLANES:] * tap
            u_ref[r0:r0 + rows, sl] = _silu(jnp.concatenate([conv_top, conv_rest], axis=0)).astype(BF16)
    if spt == 1:
        xtail[...] = acc_ref[tm - SUBLANES:, c_xbc:c_dt]
    dt_ref[...] = acc_ref[:, c_dt:c_dt + DT_PAD]
    if tt:
        for j in range(ATT_HEADS):
            v_ref[pl.ds(j, tm, stride=ATT_HEADS), :] = acc_ref[:, c_v + j * ATT_V_DIM:c_v + (j + 1) * ATT_V_DIM]
    else:
        v_ref[...] = acc_ref[:, c_v:c_v + ATT_WIDTH]

    cos = cos_ref[...]
    sin = sin_ref[...]
    first_half = (lax.broadcasted_iota(jnp.int32, (tm, LANES), 1) % ATT_HEAD_DIM) < (ATT_HEAD_DIM // 2)

    def rope(t):
        swapped = jnp.where(first_half, pltpu.roll(t, LANES - ATT_HEAD_DIM // 2, 1),
                            pltpu.roll(t, ATT_HEAD_DIM // 2, 1))
        return t * cos + swapped * sin

    for j in range(ATT_WIDTH // LANES):
        sl = slice(j * LANES, (j + 1) * LANES)
        qr = rope(acc_ref[:, c_q + j * LANES:c_q + (j + 1) * LANES]) * Q_SCALE
        kr = rope(acc_ref[:, c_k + j * LANES:c_k + (j + 1) * LANES])
        q_ref[:, sl] = qr.astype(BF16)
        if tt:
            k_ref[0, sl, :] = kr.T
            kb_ref[:, sl] = kr.astype(BF16)
            for c in range(tm // tt):
                qt_ref[c, sl, :] = qr[c * tt:(c + 1) * tt, :].T.astype(BF16)
                vt_ref[c, j * ATT_VT_ROWS:j * ATT_VT_ROWS + LANES, :] = acc_ref[
                    c * tt:(c + 1) * tt, c_v + j * LANES:c_v + (j + 1) * LANES].T.astype(BF16)
                vt_ref[c, j * ATT_VT_ROWS + LANES:(j + 1) * ATT_VT_ROWS, :] = jnp.ones((ATT_VT_ROWS - LANES, tt), BF16)
        else:
            k_ref[:, sl] = kr

    hn = _rms(x_ref[...], g_ref[...], NORM_EPS).astype(BF16)
    acc_ref[:, :c_q] = _dot(hn, w_ref[...])
    acc_ref[:, c_q:] = _dot(hn, wqkv_ref[...])


def _in_proj(x, g, w, wqkv, cos, sin, cbuf, cw, cb, *, seq, tm, tt):
    n = x.shape[0]
    nt = n // tm
    spt = max(1, tm // seq)
    tiles_per_seq = max(1, seq // tm)
    if spt > 1:
        cos, sin = jnp.tile(cos, (spt, 1)), jnp.tile(sin, (spt, 1))
    done = lambda i: jnp.maximum(i - 1, 0)
    row = lambda c: pl.BlockSpec((tm, c), lambda i: (done(i), 0))
    tab = pl.BlockSpec((tm, LANES), lambda i: (done(i) % tiles_per_seq, 0))
    if tt:
        k_spec = pl.BlockSpec((1, ATT_WIDTH, tm), lambda i: (done(i) // tiles_per_seq, 0, done(i) % tiles_per_seq))
        k_shape = jax.ShapeDtypeStruct((n // seq, ATT_WIDTH, seq), F32)
        v_spec = pl.BlockSpec((tm * ATT_HEADS, ATT_V_DIM), lambda i: (done(i), 0))
        v_shape = jax.ShapeDtypeStruct((n * ATT_HEADS, ATT_V_DIM), F32)
    else:
        k_spec, k_shape = row(ATT_WIDTH), jax.ShapeDtypeStruct((n, ATT_WIDTH), F32)
        v_spec, v_shape = row(ATT_WIDTH), jax.ShapeDtypeStruct((n, ATT_WIDTH), F32)
    out_specs = [row(SSD_WIDTH), row(SSD_XBC), pl.BlockSpec((spt, SUBLANES, SSD_XBC), lambda i: (done(i), 0, 0)),
                 row(DT_PAD), row(ATT_WIDTH), k_spec, v_spec]
    out_shape = [jax.ShapeDtypeStruct((n, SSD_WIDTH), BF16), jax.ShapeDtypeStruct((n, SSD_XBC), BF16),
                 jax.ShapeDtypeStruct((nt * spt, SUBLANES, SSD_XBC), F32), jax.ShapeDtypeStruct((n, DT_PAD), F32),
                 jax.ShapeDtypeStruct((n, ATT_WIDTH), BF16), k_shape, v_shape]
    if tt:
        tr = lambda r: pl.BlockSpec((tm // tt, r, tt), lambda i: (done(i), 0, 0))
        vt_rows = ATT_HEADS * ATT_VT_ROWS
        out_specs += [tr(ATT_WIDTH), row(ATT_WIDTH), tr(vt_rows)]
        out_shape += [jax.ShapeDtypeStruct((n // tt, ATT_WIDTH, tt), BF16), jax.ShapeDtypeStruct((n, ATT_WIDTH), BF16),
                      jax.ShapeDtypeStruct((n // tt, vt_rows, tt), BF16)]

    def body(*refs):
        refs, scratch = refs[:-2], refs[-2:]
        refs = refs + (None,) * (19 - len(refs))
        _in_proj_kernel(*refs, *scratch, tm=tm, tt=tt, spt=spt, tiles_per_seq=tiles_per_seq)

    outs = pl.pallas_call(
        body,
        grid=(nt + 1,),
        in_specs=[pl.BlockSpec((tm, D_MODEL), lambda i: (jnp.minimum(i, nt - 1), 0)),
                  _const_spec((1, D_MODEL)), _const_spec(w.shape), _const_spec(wqkv.shape), tab, tab,
                  pl.BlockSpec((spt, SUBLANES, SSD_XBC), lambda i: (done(i) // tiles_per_seq, 0, 0)),
                  _const_spec((SSD_CONV, SSD_XBC)), _const_spec((1, SSD_XBC))],
        out_specs=out_specs,
        out_shape=out_shape,
        scratch_shapes=[pltpu.VMEM((tm, IN_COLS_PADDED), F32), pltpu.VMEM((SUBLANES, SSD_XBC), F32)],
        compiler_params=pltpu.CompilerParams(dimension_semantics=("arbitrary",), vmem_limit_bytes=VMEM_LIMIT),
        name="in_proj",
    )(x, g, w, wqkv, cos, sin, cbuf, cw, cb)
    return list(outs) + [None] * (10 - len(outs))


def _expand_heads(x, e2):
    hi = x.astype(BF16)
    lo = (x - hi.astype(F32)).astype(BF16)
    return _dot(jnp.concatenate([hi, lo], axis=1), e2)


def _ssd_kernel(gate_ref, u_ref, dt_ref, h0_ref, dtb_ref, alog_ref, dsk_ref, nw_ref, e2_ref,
                y_ref, hout_ref, state, *, q, valid, chunks):
    step = pl.program_id(1)

    @pl.when(step == 0)
    def _():
        state[...] = h0_ref[0]

    for sc in range(chunks):
        rs = pl.ds(sc * q, q)
        _ssd_chunk(gate_ref.at[rs], u_ref.at[rs], dt_ref.at[rs], dtb_ref, alog_ref, dsk_ref, nw_ref, e2_ref,
                   y_ref.at[rs], state, q=q, valid=valid, c=step * chunks + sc)

    @pl.when(step == pl.num_programs(1) - 1)
    def _():
        hout_ref[0] = state[...]


def _ssd_chunk(gate_ref, u_ref, dt_ref, dtb_ref, alog_ref, dsk_ref, nw_ref, e2_ref, y_ref, state, *, q, valid, c):
    xs = u_ref[:, :SSD_WIDTH].astype(F32)
    bm = u_ref[:, SSD_WIDTH:SSD_WIDTH + SSD_GROUPS * SSD_STATE]
    cmb = u_ref[:, SSD_WIDTH + SSD_GROUPS * SSD_STATE:]

    dtr = dt_ref[...] + dtb_ref[...]
    dt = jnp.maximum(dtr, 0.0) + jnp.log(1.0 + jnp.exp(-jnp.abs(dtr)))
    if valid is not None:
        row = lax.broadcasted_iota(jnp.int32, (q, DT_PAD), 0) + c * q
        dt = jnp.where(row < valid, dt, 0.0)
    a = -jnp.exp(alog_ref[...])
    ad = dt * a
    ri = lax.broadcasted_iota(jnp.int32, (q, q), 0)
    ci = lax.broadcasted_iota(jnp.int32, (q, q), 1)
    tril = ri >= ci
    tril_b = jnp.where(tril, 1.0, 0.0).astype(BF16)
    ad_hi = ad.astype(BF16)
    ad_r = ad - ad_hi.astype(F32)
    ad_mid = ad_r.astype(BF16)
    ad_lo = (ad_r - ad_mid.astype(F32)).astype(BF16)
    acum = _dot(tril_b, ad_hi) + _dot(tril_b, ad_mid) + _dot(tril_b, ad_lo)
    acum_t = acum.T
    tot = acum[q - 1:q, :]
    e2 = e2_ref[...]
    expanded = _expand_heads(jnp.concatenate([dt, dt * jnp.exp(tot - acum), jnp.exp(acum)], axis=0), e2)
    dtx = expanded[0:q]
    ddx = expanded[q:2 * q]
    eax = expanded[2 * q:3 * q]
    dsk = _expand_heads(jnp.broadcast_to(dsk_ref[...], (SUBLANES, DT_PAD)), e2)[0:1]

    xdt = (xs * dtx).astype(BF16)
    xdtd = (xs * ddx).astype(BF16)
    bm_t = bm.astype(F32).T.astype(BF16)
    gw = SSD_REP * SSD_HEAD_DIM
    stripe = lax.broadcasted_iota(jnp.int32, (q, gw), 1) // SSD_HEAD_DIM
    ys = []
    for g in range(SSD_GROUPS):
        cm_g = cmb[:, g * SSD_STATE:(g + 1) * SSD_STATE]
        bt_g = bm_t[g * SSD_STATE:(g + 1) * SSD_STATE, :]
        cbm = _dot(cm_g, bt_g)
        ms = []
        for r in range(SSD_REP):
            h = g * SSD_REP + r
            diff = acum[:, h:h + 1] - acum_t[h:h + 1, :]
            ms.append((cbm * jnp.exp(jnp.where(tril, diff, -jnp.inf))).astype(BF16))
        ydf = _dot(jnp.concatenate(ms, axis=0), xdt[:, g * gw:(g + 1) * gw])
        yd = ydf[0:q]
        for r in range(1, SSD_REP):
            yd = jnp.where(stripe == r, ydf[r * q:(r + 1) * q], yd)
        st = state[g]
        y_off = _dot(cm_g, st.astype(BF16)) * eax[:, g * gw:(g + 1) * gw]
        state[g] = st * eax[q - 1:q, g * gw:(g + 1) * gw] + _dot(bt_g, xdtd[:, g * gw:(g + 1) * gw])
        ys.append(yd + y_off)
    y = jnp.concatenate(ys, axis=1) + dsk * xs
    y = y * gate_ref[...].astype(F32)
    outs = []
    for g in range(SSD_WIDTH // SSD_NORM_GROUP):
        yg = y[:, g * SSD_NORM_GROUP:(g + 1) * SSD_NORM_GROUP]
        outs.append(yg * lax.rsqrt(jnp.mean(yg * yg, axis=-1, keepdims=True) + SSD_NORM_EPS))
    y_ref[...] = (jnp.concatenate(outs, axis=1) * nw_ref[...]).astype(BF16)


def _ssd(gate, u, dt, h0, dtb, alog, dsk, nw, e2, *, nb, seq, q, valid):
    chunks = SSD_CHUNKS_PER_STEP if (seq // q) % SSD_CHUNKS_PER_STEP == 0 else 1
    nc = seq // (q * chunks)
    row = lambda c: pl.BlockSpec((q * chunks, c), lambda b, i: (b * nc + i, 0))
    gw = SSD_REP * SSD_HEAD_DIM
    st_spec = pl.BlockSpec((1, SSD_GROUPS, SSD_STATE, gw), lambda b, i: (b, 0, 0, 0))
    return pl.pallas_call(
        functools.partial(_ssd_kernel, q=q, valid=valid, chunks=chunks),
        grid=(nb, nc),
        in_specs=[row(SSD_WIDTH), row(SSD_XBC), row(DT_PAD), st_spec,
                  _const_spec((1, DT_PAD)), _const_spec((1, DT_PAD)), _const_spec((1, DT_PAD)),
                  _const_spec((1, SSD_WIDTH)), _const_spec((2 * DT_PAD, SSD_WIDTH))],
        out_specs=[row(SSD_WIDTH), st_spec],
        out_shape=[jax.ShapeDtypeStruct((nb * seq, SSD_WIDTH), BF16),
                   jax.ShapeDtypeStruct((nb, SSD_GROUPS, SSD_STATE, gw), F32)],
        scratch_shapes=[pltpu.VMEM((SSD_GROUPS, SSD_STATE, gw), F32)],
        compiler_params=pltpu.CompilerParams(dimension_semantics=("arbitrary", "arbitrary"),
                                             vmem_limit_bytes=VMEM_LIMIT),
        name="ssd",
    )(gate, u, dt, h0, dtb, alog, dsk, nw, e2)


def _column_max(x):
    while x.shape[0] > SUBLANES and x.shape[0] % (2 * SUBLANES) == 0:
        half = x.shape[0] // 2
        x = jnp.maximum(x[:half], x[half:])
    return jnp.max(x, axis=0, keepdims=True)


def _attn_tile_counts(i, *, tq, tk, past, kv_len, minimum=min):
    q_lo = past + i * tq
    q_hi = q_lo + tq - 1
    lim_lo = minimum((q_lo // CHUNK + 1) * CHUNK, kv_len)
    lim_hi = minimum((q_hi // CHUNK + 1) * CHUNK, kv_len)
    return lim_lo // tk, (lim_hi + tk - 1) // tk


def _diff_attn_kernel(qt_ref, qt_next_ref, k_ref, vt_ref, lam_ref, sw_ref, o_ref, *scratch, tq, tiles, **kw):
    step = pl.program_id(2)
    for t in range(tiles):
        qt_next = qt_ref[t + 1] if t + 1 < tiles else qt_next_ref[0]
        _diff_attn_tile(step * tiles + t, qt_ref[t], qt_next, t == 0, k_ref, vt_ref, lam_ref, sw_ref,
                        o_ref.at[pl.ds(t * tq, tq)], *scratch, tq=tq, **kw)


def _diff_attn_tile(i, qt_own, qt_next, may_be_first, k_ref, vt_ref, lam_ref, sw_ref, o_ref,
                    q2t_ref, s_ref, p_ref, m_ref, alpha_ref, acc_ref, *, tq, tk, past, kv_len, lam_init, nbuf, nkt):
    w = 2 * tq
    row = lax.broadcasted_iota(jnp.int32, (ATT_V_DIM, tq), 0)
    n_full, _ = _attn_tile_counts(i, tq=tq, tk=tk, past=past, kv_len=kv_len, minimum=jnp.minimum)
    n_visits = n_full + 1

    def scores(j):
        kt = k_ref[pl.ds(pl.multiple_of(j * tk, tk), tk), :]
        return _dot(kt, q2t_ref[...])

    def visited_tile(v):
        return jnp.where(v == 0, n_full, jnp.maximum(v - 1, 0))

    ncb = w // LANES
    pw = p_ref.shape[-1]

    def store_scores(buf, s):
        for cb in range(ncb):
            s_ref[buf, cb] = s[:, cb * LANES:(cb + 1) * LANES]

    def stage_a(v, buf):
        store_scores(buf, scores(jnp.minimum(v - 1, n_full)))

    rows = min(tk, ATT_SOFTMAX_ROWS)

    def stage_b(buf):
        for cb in range(ncb):
            sl = slice(cb * LANES, (cb + 1) * LANES)
            m_old = m_ref[:, sl]
            m_new = m_old
            for r0 in range(0, tk, rows):
                m_new = jnp.maximum(m_new, _column_max(s_ref[buf, cb, r0:r0 + rows, :]))
            m_ref[:, sl] = m_new
            alpha_ref[buf, :, sl] = jnp.exp2(m_old - m_new)
            pl0 = (cb * LANES) % pw
            for r0 in range(0, tk, rows):
                p_ref[buf, cb * LANES // pw, r0:r0 + rows, pl0:pl0 + LANES] = jnp.exp2(
                    (s_ref[buf, cb, r0:r0 + rows, :] - m_new).astype(BF16))

    def stage_c(v, buf):
        vt = vt_ref[visited_tile(v)]
        for pb in range(w // pw):
            sl = slice(pb * pw, (pb + 1) * pw)
            acc_ref[:, sl] = alpha_ref[buf, :, sl] * acc_ref[:, sl] + _dot(vt, p_ref[buf, pb])

    def first_visit(qt, tile):
        zero = jnp.zeros_like(qt)
        q2t_ref[:, 0:tq] = jnp.where(row < ATT_HEAD_DIM, qt, zero)
        q2t_ref[:, tq:w] = jnp.where(row >= ATT_HEAD_DIM, qt, zero)
        nf, _ = _attn_tile_counts(tile, tq=tq, tk=tk, past=past, kv_len=kv_len, minimum=jnp.minimum)
        nf = jnp.minimum(nf, nkt - 1)
        qchunk = (past + tile * tq + lax.broadcasted_iota(jnp.int32, (1, w), 1) % tq) // CHUNK
        s_part = scores(nf)
        s_masked = []
        for kb in range(tk // CHUNK):
            k0 = nf * tk + kb * CHUNK
            kchunk = jnp.where(k0 < kv_len, k0 // CHUNK, jnp.iinfo(jnp.int32).max)
            s_masked.append(jnp.where(kchunk <= qchunk, s_part[kb * CHUNK:(kb + 1) * CHUNK], NEG_BIG))
        store_scores(0, jnp.concatenate(s_masked, axis=0))

    if may_be_first:

        @pl.when(i == 0)
        def _():
            first_visit(qt_own, i)

    m_ref[...] = jnp.full_like(m_ref, NEG_BIG)
    acc_ref[...] = jnp.zeros_like(acc_ref)
    p_ref[nbuf - 1] = jnp.zeros(p_ref.shape[1:], BF16)
    alpha_ref[nbuf - 1] = jnp.ones(alpha_ref.shape[1:], F32)

    def visits(v0, count, prefetch_last):
        for r in range(count):
            prefetch = r + 1 < count or prefetch_last
            if prefetch and nbuf > 1:
                stage_a(v0 + r + 1, (r + 1) % nbuf)
            stage_c(v0 + r - 1, (r - 1) % nbuf)
            stage_b(r)
            if prefetch and nbuf == 1:
                stage_a(v0 + r + 1, (r + 1) % nbuf)

    def trip(u, carry):
        visits(nbuf * u, nbuf, True)
        return carry

    lax.fori_loop(0, n_visits // nbuf, trip, 0)
    for rem in range(nbuf):

        @pl.when(n_visits % nbuf == rem)
        def _():
            visits(n_visits - rem, rem, False)
            stage_c(n_visits - 1, (rem - 1) % nbuf)

    first_visit(qt_next, i + 1)

    lam_v = lam_ref[...]
    lam = (jnp.exp(jnp.sum(lam_v[0:1] * lam_v[1:2], axis=-1, keepdims=True))
           - jnp.exp(jnp.sum(lam_v[2:3] * lam_v[3:4], axis=-1, keepdims=True)) + lam_init)
    acc = acc_ref[...]
    o = acc[:ATT_V_DIM] * (1.0 / acc[ATT_V_DIM:ATT_V_DIM + 1])
    o = o[:, :tq] - lam * o[:, tq:]
    o = o * lax.rsqrt(jnp.mean(o * o, axis=0, keepdims=True) + ATT_NORM_EPS) * (sw_ref[...] * (1.0 - lam_init))
    o_ref[...] = o.T.astype(BF16)


def _diff_attn(qt, k, vt, lam_vecs, subln_col, *, nb, tq, tk, past, kv_len, lam_init, nbuf):
    nq = qt.shape[0] // nb
    nkt = vt.shape[0] // nb
    assert tk % CHUNK == 0 and kv_len % CHUNK == 0
    for i in range(nq):
        n_full, n_end = _attn_tile_counts(i, tq=tq, tk=tk, past=past, kv_len=kv_len)
        assert n_end - n_full == 1 and n_end <= nkt, (i, n_full, n_end)
    w = 2 * tq
    tiles = ATT_TILES_PER_STEP if nq % ATT_TILES_PER_STEP == 0 else 1
    ns = nq // tiles
    return pl.pallas_call(
        functools.partial(_diff_attn_kernel, tq=tq, tiles=tiles, tk=tk, past=past, kv_len=kv_len, lam_init=lam_init,
                          nbuf=nbuf, nkt=nkt),
        grid=(nb, ATT_HEADS, ns),
        in_specs=[pl.BlockSpec((tiles, ATT_V_DIM, tq), lambda b, h, i: (b * ns + i, h, 0)),
                  pl.BlockSpec((1, ATT_V_DIM, tq),
                               lambda b, h, i: (b * nq + jnp.minimum((i + 1) * tiles, nq - 1), h, 0)),
                  pl.BlockSpec((nkt * tk, ATT_V_DIM), lambda b, h, i: (b, h)),
                  pl.BlockSpec((nkt, ATT_VT_ROWS, tk), lambda b, h, i: (b, h, 0)),
                  _const_spec((4, ATT_HEAD_DIM)), _const_spec((ATT_V_DIM, 1))],
        out_specs=pl.BlockSpec((tiles * tq, ATT_V_DIM), lambda b, h, i: (b * ns + i, h)),
        out_shape=jax.ShapeDtypeStruct((nb * nq * tq, ATT_WIDTH), BF16),
        scratch_shapes=[pltpu.VMEM((ATT_V_DIM, w), BF16), pltpu.VMEM((nbuf, w // LANES, tk, LANES), F32),
                        pltpu.VMEM((nbuf, w // MXU_TILE, tk, MXU_TILE), BF16), pltpu.VMEM((1, w), F32),
                        pltpu.VMEM((nbuf, 1, w), F32),
                        pltpu.VMEM((ATT_VT_ROWS, w), F32)],
        compiler_params=pltpu.CompilerParams(dimension_semantics=("arbitrary", "arbitrary", "arbitrary"),
                                             vmem_limit_bytes=VMEM_LIMIT),
        name="diff_attn",
    )(qt, qt, k, vt, lam_vecs, subln_col)


def _decode_attn_kernel(q_ref, kn_ref, vn_ref, kc_ref, vc_ref, lam_ref, sw_ref, o_ref, *, seq, past, lam_init):
    lam_v = lam_ref[...]
    lam = (jnp.exp(jnp.sum(lam_v[0:1] * lam_v[1:2], axis=-1, keepdims=True))
           - jnp.exp(jnp.sum(lam_v[2:3] * lam_v[3:4], axis=-1, keepdims=True)) + lam_init)
    npad = LANES
    lane = lax.broadcasted_iota(jnp.int32, (seq, LANES), 1)
    qchunk = (past + lax.broadcasted_iota(jnp.int32, (2 * seq, 1), 0) % seq) // CHUNK
    kpos_p = lax.broadcasted_iota(jnp.int32, (2 * seq, past), 1)
    kpos_n = lax.broadcasted_iota(jnp.int32, (2 * seq, npad), 1)
    vis_p = kpos_p // CHUNK <= qchunk
    vis_n = ((past + kpos_n) // CHUNK <= qchunk) & (kpos_n < seq)
    pad_rows = jnp.zeros((npad - seq, LANES), BF16)
    for h in range(ATT_HEADS):
        sl = slice(h * LANES, (h + 1) * LANES)
        qh = q_ref[:, sl]
        zero = jnp.zeros_like(qh)
        q2 = jnp.concatenate([jnp.where(lane < ATT_HEAD_DIM, qh, zero), jnp.where(lane >= ATT_HEAD_DIM, qh, zero)], axis=0)
        s_p = jnp.where(vis_p, _dot(q2, kc_ref[0, sl, :].astype(BF16)), NEG_BIG)
        kn = jnp.concatenate([kn_ref[:, sl].astype(BF16), pad_rows], axis=0)
        s_n = lax.dot_general(q2, kn, (((1,), (1,)), ((), ())), preferred_element_type=F32)
        s_n = jnp.where(vis_n, s_n, NEG_BIG)
        m = jnp.maximum(jnp.max(s_p, axis=-1, keepdims=True), jnp.max(s_n, axis=-1, keepdims=True))
        p_p = jnp.exp2(s_p - m)
        p_n = jnp.exp2(s_n - m)
        l = jnp.sum(p_p, axis=-1, keepdims=True) + jnp.sum(p_n, axis=-1, keepdims=True)
        vh = vc_ref[0, pl.ds(h, past, stride=ATT_HEADS), :].astype(BF16)
        vn = jnp.concatenate([vn_ref[:, sl].astype(BF16), pad_rows], axis=0)
        o = (_dot(p_p.astype(BF16), vh) + _dot(p_n.astype(BF16), vn)) / l
        o = o[:seq] - lam * o[seq:]
        o = o * lax.rsqrt(jnp.mean(o * o, axis=-1, keepdims=True) + ATT_NORM_EPS) * (sw_ref[...] * (1.0 - lam_init))
        o_ref[:, sl] = o.astype(BF16)


def _decode_attn(q, k_new, v_new, kt_cache, v_cache, lam_vecs, subln_row, *, nb, seq, past, lam_init):
    row = pl.BlockSpec((seq, ATT_WIDTH), lambda b: (b, 0))
    return pl.pallas_call(
        functools.partial(_decode_attn_kernel, seq=seq, past=past, lam_init=lam_init),
        grid=(nb,),
        in_specs=[row, row, row,
                  pl.BlockSpec((1, ATT_WIDTH, past), lambda b: (b, 0, 0)),
                  pl.BlockSpec((1, past * ATT_HEADS, ATT_V_DIM), lambda b: (b, 0, 0)),
                  _const_spec((4, ATT_HEAD_DIM)), _const_spec((1, ATT_V_DIM))],
        out_specs=row,
        out_shape=jax.ShapeDtypeStruct((nb * seq, ATT_WIDTH), BF16),
        compiler_params=pltpu.CompilerParams(dimension_semantics=("arbitrary",), vmem_limit_bytes=VMEM_LIMIT),
        name="decode_attn",
    )(q, k_new, v_new, kt_cache, v_cache, lam_vecs, subln_row)


def _post_mix_kernel(x_ref, ys_ref, ya_ref, mk_ref, mv_ref, wo_ref, wq_ref, wox_ref,
                     g1_ref, g2_ref, g3_ref, h_ref, *, spt, rows, cached_mem):
    lane_blocks = MEM_HEAD_DIM // LANES

    def mem_head(ref, s_i, hd):
        if not cached_mem:
            return ref[s_i, :, hd * MEM_HEAD_DIM:(hd + 1) * MEM_HEAD_DIM]
        parts = [ref[s_i, pl.ds(cb * MEM_HEADS + hd, MEM_LEN, stride=lane_blocks * MEM_HEADS), :].astype(BF16)
                 for cb in range(lane_blocks)]
        return jnp.concatenate(parts, axis=1)

    mix = _dot(ys_ref[...], wo_ref[0:SSD_WIDTH, :]) + _dot(ya_ref[...], wo_ref[SSD_WIDTH:, :])
    h = x_ref[...] + _rms(mix, g1_ref[...], NORM_EPS)
    qn = _rms(h, g2_ref[...], NORM_EPS).astype(BF16)
    qx = (_dot(qn, wq_ref[...]) * (math.log2(math.e) / math.sqrt(MEM_HEAD_DIM))).astype(BF16)
    ox_seqs = []
    for s_i in range(spt):
        qs = qx[s_i * rows:(s_i + 1) * rows]
        oxs = []
        for hd in range(MEM_HEADS):
            sl = slice(hd * MEM_HEAD_DIM, (hd + 1) * MEM_HEAD_DIM)
            s = lax.dot_general(qs[:, sl], mem_head(mk_ref, s_i, hd), (((1,), (1,)), ((), ())),
                                preferred_element_type=F32)
            p = jnp.exp2(s - jnp.max(s, axis=-1, keepdims=True))
            ox = _dot(p.astype(BF16), mem_head(mv_ref, s_i, hd)) * (1.0 / jnp.sum(p, axis=-1, keepdims=True))
            oxs.append(ox.astype(BF16))
        ox_seqs.append(jnp.concatenate(oxs, axis=1))
    ox_all = ox_seqs[0] if spt == 1 else jnp.concatenate(ox_seqs, axis=0)
    o2 = _dot(ox_all, wox_ref[...])
    h_ref[...] = h + _rms(o2, g3_ref[...], NORM_EPS)


def _post_mix(x, ys, ya, mk, mv, w_out, wq, wox, g1, g2, g3, *, seq, tm):
    n = x.shape[0]
    spt = max(1, tm // seq)
    tiles_per_seq = max(1, seq // tm)
    row = lambda c: pl.BlockSpec((tm, c), lambda i: (i, 0))
    cached_mem = mk.dtype == F32
    mem = pl.BlockSpec((spt,) + mk.shape[1:], lambda i: (i // tiles_per_seq, 0, 0))
    wspec = _const_spec((D_MODEL, D_MODEL))
    gspec = _const_spec((1, D_MODEL))
    return pl.pallas_call(
        functools.partial(_post_mix_kernel, spt=spt, rows=tm // spt, cached_mem=cached_mem),
        grid=(n // tm,),
        in_specs=[row(D_MODEL), row(SSD_WIDTH), row(ATT_WIDTH), mem, mem, wspec, wspec, wspec, gspec, gspec, gspec],
        out_specs=row(D_MODEL),
        out_shape=jax.ShapeDtypeStruct((n, D_MODEL), F32),
        compiler_params=pltpu.CompilerParams(dimension_semantics=("arbitrary",), vmem_limit_bytes=VMEM_LIMIT),
        name="post_mix",
    )(x, ys, ya, mk, mv, w_out, wq, wox, g1, g2, g3)


def _ffn_kernel(h_ref, wg_ref, wu_ref, wd_ref, g1_ref, g2_ref, o_ref):
    h = h_ref[...]
    hn = _rms(h, g1_ref[...], NORM_EPS).astype(BF16)
    act = (_silu(_dot(hn, wg_ref[...])) * _dot(hn, wu_ref[...])).astype(BF16)
    f = _dot(act, wd_ref[...])
    o_ref[...] = h + _rms(f, g2_ref[...], NORM_EPS)


def _ffn(h, wg, wu, wd, g1, g2, *, tm):
    n = h.shape[0]
    row = pl.BlockSpec((tm, D_MODEL), lambda i: (i, 0))
    return pl.pallas_call(
        _ffn_kernel,
        grid=(n // tm,),
        in_specs=[row, _const_spec((D_MODEL, FFN_HIDDEN)), _const_spec((D_MODEL, FFN_HIDDEN)),
                  _const_spec((FFN_HIDDEN, D_MODEL)), _const_spec((1, D_MODEL)), _const_spec((1, D_MODEL))],
        out_specs=row,
        out_shape=jax.ShapeDtypeStruct((n, D_MODEL), F32),
        compiler_params=pltpu.CompilerParams(dimension_semantics=("arbitrary",), vmem_limit_bytes=VMEM_LIMIT),
        name="ffn",
    )(h, wg, wu, wd, g1, g2)


def _rope_tables(past, seq):
    inv = np.power(ROPE_THETA, -np.arange(0, ATT_HEAD_DIM, 2, dtype=np.float64) / ATT_HEAD_DIM)
    ang = (past + np.arange(seq, dtype=np.float64))[:, None] * inv[None, :]
    cos, sin = np.cos(ang), np.sin(ang)
    reps = LANES // ATT_HEAD_DIM
    cos_t = np.tile(np.concatenate([cos, cos], axis=-1), (1, reps)).astype(np.float32)
    sin_t = np.tile(np.concatenate([-sin, sin], axis=-1), (1, reps)).astype(np.float32)
    return jnp.asarray(cos_t), jnp.asarray(sin_t)


def _mem_cache_rows(c):
    b = c.shape[0]
    c = c.reshape(b, MEM_LEN, MEM_HEADS, MEM_HEAD_DIM // LANES, LANES)
    return c.transpose(0, 1, 3, 2, 4).reshape(b, MEM_LEN * (MEM_HEAD_DIM // LANES) * MEM_HEADS, LANES)


def _state_to_kernel_layout(s):
    b = s.shape[0]
    s = s.reshape(b, SSD_GROUPS, SSD_REP, SSD_HEAD_DIM, SSD_STATE)
    return s.transpose(0, 1, 4, 2, 3).reshape(b, SSD_GROUPS, SSD_STATE, SSD_REP * SSD_HEAD_DIM)


def _state_from_kernel_layout(s):
    b = s.shape[0]
    s = s.reshape(b, SSD_GROUPS, SSD_STATE, SSD_REP, SSD_HEAD_DIM)
    return s.transpose(0, 1, 3, 4, 2).reshape(b, SSD_HEADS, SSD_HEAD_DIM, SSD_STATE)


def _layer(x, conv_buf, ssm0, kt_past, v_past, mem_kb, mem_vb, lam_init, p, *, tm, tk, tq, ssd_q):
    nb, seq, _ = x.shape
    n = nb * seq
    past = 0 if kt_past is None else kt_past.shape[2]
    xf = x.reshape(n, D_MODEL)
    cos, sin = _rope_tables(past, seq)
    no_history = kt_past is None
    assert tq == tk or not no_history
    cbuf = jnp.pad(conv_buf.astype(F32), ((0, 0), (SUBLANES - (SSD_CONV - 1), 0), (0, 0)))
    gate, u, tail, dt, q, k, v, qt, kb, vt = _in_proj(xf, p["g_pre_mix"], p["w_in_ssd"], p["w_in_qkv"], cos, sin, cbuf,
                                                       p["conv_w"], p["conv_b"], seq=seq, tm=tm,
                                                       tt=tk if no_history else 0)

    seq_pad = -(-seq // ssd_q) * ssd_q
    if seq_pad != seq:
        pad = lambda a: jnp.pad(a.reshape(nb, seq, -1), ((0, 0), (0, seq_pad - seq), (0, 0))).reshape(nb * seq_pad, -1)
        gate_s, u_s, dt_s = pad(gate), pad(u), pad(dt)
    else:
        gate_s, u_s, dt_s = gate, u, dt
    y_ssd, h_new = _ssd(gate_s, u_s, dt_s, _state_to_kernel_layout(ssm0.astype(F32)),
                        p["dt_bias"], p["a_log"], p["d_skip"], p["ssm_norm_w"], p["e2"],
                        nb=nb, seq=seq_pad, q=ssd_q, valid=None if seq_pad == seq else seq)
    if seq_pad != seq:
        y_ssd = y_ssd.reshape(nb, seq_pad, SSD_WIDTH)[:, :seq].reshape(n, SSD_WIDTH)
    ssm_new = _state_from_kernel_layout(h_new)
    ext_tail = tail.reshape(nb, -1, SUBLANES, SSD_XBC)[:, -1]
    if seq >= SSD_CONV - 1:
        conv_new = ext_tail[:, SUBLANES - (SSD_CONV - 1):]
    else:
        conv_new = jnp.concatenate([conv_buf.astype(F32), ext_tail[:, SUBLANES - seq:]], axis=1)[:, -(SSD_CONV - 1):]

    if no_history:
        kv_len = seq
        y_att = _diff_attn(qt, kb, vt, p["lam_vecs"], p["subln_col"], nb=nb, tq=tq, tk=tk,
                           past=past, kv_len=kv_len, lam_init=lam_init, nbuf=ATT_PIPELINE_BUFFERS)
    else:
        y_att = _decode_attn(q, k, v, kt_past, v_past, p["lam_vecs"], p["subln_col"].reshape(1, ATT_V_DIM),
                             nb=nb, seq=seq, past=past, lam_init=lam_init)

    h = _post_mix(xf, y_ssd, y_att, mem_kb, mem_vb, p["w_out"], p["wq_x"], p["wo_x"],
                  p["g_post_mix"], p["g_pre_x"], p["g_post_x"], seq=seq, tm=tm)
    out = _ffn(h, p["w_gate"], p["w_up"], p["w_down"], p["g_pre_ffn"], p["g_post_ffn"], tm=tm)
    if no_history:
        k_out = k.reshape(nb, ATT_HEADS, 2, ATT_HEAD_DIM, seq).transpose(0, 4, 1, 2, 3)
    else:
        k_out = k.reshape(nb, seq, ATT_HEADS, 2, ATT_HEAD_DIM)
    return (out.reshape(nb, seq, D_MODEL), k_out,
            v.reshape(nb, seq, ATT_HEADS, ATT_V_DIM), ssm_new, conv_new)


def _prep_params(i, w_in, conv_w, conv_b, dt_bias, a_log, d_skip, ssm_norm_w, lam_q1, lam_k1, lam_q2, lam_k2, subln_w,
                 w_out, wq_x, wo_x, g_pre_mix, g_post_mix, g_pre_x, g_post_x, g_pre_ffn, g_post_ffn,
                 w_gate, w_up, w_down):
    w = w_in[i]
    s1 = SSD_WIDTH + SSD_XBC + SSD_HEADS
    w_ssd = jnp.pad(w[:, :s1].astype(BF16), ((0, 0), (0, DT_PAD - SSD_HEADS)))
    w_qkv = w[:, s1:].astype(BF16)
    head_pad = lambda a: jnp.pad(a[i].astype(F32), (0, DT_PAD - SSD_HEADS)).reshape(1, DT_PAD)
    e = (jnp.arange(DT_PAD)[:, None] == (jnp.arange(SSD_WIDTH)[None, :] // SSD_HEAD_DIM)).astype(BF16)
    row = lambda a: a[i].astype(F32).reshape(1, -1)
    w_out_b, wq_b, wo_b = _cast_bf16([w_out[i], wq_x[i], wo_x[i]], steps=4)
    w_gate_b, w_up_b, w_down_b = _cast_bf16([w_gate[i], w_up[i], w_down[i]], steps=8)
    return {
        "w_in_ssd": w_ssd, "w_in_qkv": w_qkv, "conv_w": conv_w[i].astype(F32), "conv_b": row(conv_b),
        "dt_bias": head_pad(dt_bias), "a_log": head_pad(a_log), "d_skip": head_pad(d_skip),
        "ssm_norm_w": row(ssm_norm_w), "e2": jnp.concatenate([e, e], axis=0),
        "lam_vecs": jnp.stack([lam_q1[i], lam_k1[i], lam_q2[i], lam_k2[i]]).astype(F32), "subln_col": subln_w[i].astype(F32).reshape(ATT_V_DIM, 1),
        "w_out": w_out_b, "wq_x": wq_b, "wo_x": wo_b,
        "g_pre_mix": row(g_pre_mix), "g_post_mix": row(g_post_mix), "g_pre_x": row(g_pre_x), "g_post_x": row(g_post_x),
        "g_pre_ffn": row(g_pre_ffn), "g_post_ffn": row(g_post_ffn),
        "w_gate": w_gate_b, "w_up": w_up_b, "w_down": w_down_b,
    }


def kernel(x_prompt, x_sample, cache_attn_k, cache_attn_v, state_ssm, state_conv, cache_mem_k, cache_mem_v, mem_prompt, w_in, conv_w, conv_b, dt_bias, a_log, d_skip, ssm_norm_w, lam_q1, lam_k1, lam_q2, lam_k2, subln_w, w_out, g_mem, wq_x, wk_x, wv_x, wo_x, g_pre_mix, g_post_mix, g_pre_x, g_post_x, g_pre_ffn, g_post_ffn, w_gate, w_up, w_down):
    depth = w_in.shape[0]
    bp, sp, _ = x_prompt.shape
    bs, ss, _ = x_sample.shape
    hp, hs = x_prompt, x_sample
    outs = [[] for _ in range(10)]
    for i in range(depth):
        lam_init = 0.8 - 0.6 * math.exp(-0.3 * i)
        p = _prep_params(i, w_in, conv_w, conv_b, dt_bias, a_log, d_skip, ssm_norm_w, lam_q1, lam_k1, lam_q2, lam_k2,
                         subln_w, w_out, wq_x, wo_x, g_pre_mix, g_post_mix, g_pre_x, g_post_x, g_pre_ffn, g_post_ffn,
                         w_gate, w_up, w_down)
        mk, mv, mkb, mvb = _mem_kv(mem_prompt.reshape(bp * MEM_LEN, D_MODEL), g_mem[i].reshape(1, D_MODEL),
                                   wk_x[i], wv_x[i])
        hp, k_new, v_new, ssm_new, conv_new = _layer(
            hp, jnp.zeros((bp, SSD_CONV - 1, SSD_XBC), F32), jnp.zeros((bp, SSD_HEADS, SSD_HEAD_DIM, SSD_STATE), F32),
            None, None, mkb.reshape(bp, MEM_LEN, D_MODEL), mvb.reshape(bp, MEM_LEN, D_MODEL), lam_init, p,
            tm=512, tk=512, tq=512, ssd_q=256)
        for lst, val in zip(outs[:6], (k_new, v_new, ssm_new, conv_new,
                                       mk.reshape(bp, MEM_LEN, MEM_HEADS, MEM_HEAD_DIM),
                                       mv.reshape(bp, MEM_LEN, MEM_HEADS, MEM_HEAD_DIM))):
            lst.append(val)
        past = cache_attn_k.shape[2]
        hs, k_new, v_new, ssm_new, conv_new = _layer(
            hs, state_conv[i], state_ssm[i],
            cache_attn_k[i].transpose(0, 2, 3, 4, 1).reshape(bs, ATT_WIDTH, past),
            cache_attn_v[i].reshape(bs, past * ATT_HEADS, ATT_V_DIM),
            _mem_cache_rows(cache_mem_k[i]), _mem_cache_rows(cache_mem_v[i]), lam_init, p,
            tm=bs * ss, tk=0, tq=0, ssd_q=128)
        for lst, val in zip(outs[6:], (k_new, v_new, ssm_new, conv_new)):
            lst.append(val)
    return (hp, hs) + tuple(jnp.stack(o) for o in outs)
```

```python
import functools
import math

import jax
import jax.numpy as jnp
import numpy as np
from jax import lax
from jax.experimental import pallas as pl
from jax.experimental.pallas import tpu as pltpu

D_MODEL = 1024
CHUNK = 64
SSD_WIDTH = 512
SSD_HEAD_DIM = 64
SSD_HEADS = 8
SSD_GROUPS = 2
SSD_REP = 4
SSD_STATE = 128
SSD_CONV = 4
SSD_XBC = 1024
SSD_NORM_GROUP = 256
SSD_NORM_EPS = 1e-5
ATT_WIDTH = 512
ATT_HEAD_DIM = 64
ATT_HEADS = 4
ATT_V_DIM = 128
ATT_NORM_EPS = 1e-5
ROPE_THETA = 10000.0
MEM_LEN = 256
MEM_HEADS = 4
MEM_HEAD_DIM = 256
FFN_HIDDEN = 2816
NORM_EPS = 1e-6
LANES = 128
SUBLANES = 8
DT_PAD = LANES
IN_COLS_PADDED = SSD_WIDTH + SSD_XBC + 3 * ATT_WIDTH + DT_PAD
VMEM_LIMIT = 56 * 1024 * 1024
NEG_BIG = -1e30
MXU_TILE = 256
BF16_SUBLANES = 16
ATT_VT_ROWS = ATT_V_DIM + BF16_SUBLANES
SSD_CHUNKS_PER_STEP = 4
ATT_SOFTMAX_ROWS = 128
ATT_PIPELINE_BUFFERS = 2
Q_SCALE = math.log2(math.e) / math.sqrt(ATT_HEAD_DIM)

F32 = jnp.float32
BF16 = jnp.bfloat16


def _const_spec(shape):
    return pl.BlockSpec(shape, lambda *_: (0,) * len(shape), pipeline_mode=pl.Buffered(1))


def _rms(x, g, eps):
    return x * lax.rsqrt(jnp.mean(x * x, axis=-1, keepdims=True) + eps) * g


def _silu(x):
    h = 0.5 * x
    return h + h * jnp.tanh(h)


def _dot(a, b):
    return jnp.dot(a, b, preferred_element_type=F32)


def _cast_kernel(*refs):
    n = len(refs) // 2
    for x_ref, o_ref in zip(refs[:n], refs[n:]):
        o_ref[...] = x_ref[...].astype(BF16)


def _cast_bf16(arrays, steps):
    specs = [pl.BlockSpec((a.shape[0] // steps, a.shape[1]), lambda i: (i, 0)) for a in arrays]
    assert all(a.shape[0] % (steps * BF16_SUBLANES) == 0 for a in arrays)
    return pl.pallas_call(
        _cast_kernel,
        grid=(steps,),
        in_specs=specs,
        out_specs=specs,
        out_shape=[jax.ShapeDtypeStruct(a.shape, BF16) for a in arrays],
        compiler_params=pltpu.CompilerParams(dimension_semantics=("arbitrary",), vmem_limit_bytes=VMEM_LIMIT),
        name="cast_bf16",
    )(*arrays)


def _mem_kv_kernel(mem_ref, g_ref, wk_ref, wv_ref, mk_ref, mv_ref, mkb_ref, mvb_ref):
    mn = _rms(mem_ref[...], g_ref[...], NORM_EPS).astype(BF16)
    mk = _dot(mn, wk_ref[...].astype(BF16))
    mv = _dot(mn, wv_ref[...].astype(BF16))
    mk_ref[...] = mk
    mv_ref[...] = mv
    mkb_ref[...] = mk.astype(BF16)
    mvb_ref[...] = mv.astype(BF16)


def _mem_kv(mem, g_mem, wk, wv):
    n = mem.shape[0]
    tm = MEM_LEN
    row = pl.BlockSpec((tm, D_MODEL), lambda i: (i, 0))
    return pl.pallas_call(
        _mem_kv_kernel,
        grid=(n // tm,),
        in_specs=[row, _const_spec((1, D_MODEL)), _const_spec((D_MODEL, D_MODEL)), _const_spec((D_MODEL, D_MODEL))],
        out_specs=[row, row, row, row],
        out_shape=[jax.ShapeDtypeStruct((n, D_MODEL), F32), jax.ShapeDtypeStruct((n, D_MODEL), F32),
                   jax.ShapeDtypeStruct((n, D_MODEL), BF16), jax.ShapeDtypeStruct((n, D_MODEL), BF16)],
        compiler_params=pltpu.CompilerParams(dimension_semantics=("arbitrary",), vmem_limit_bytes=VMEM_LIMIT),
        name="mem_kv",
    )(mem, g_mem, wk, wv)


def _in_proj_kernel(x_ref, g_ref, w_ref, wqkv_ref, cos_ref, sin_ref, cbuf_ref, cw_ref, cb_ref,
                    z_ref, u_ref, tail_ref, dt_ref, q_ref, k_ref, v_ref, qt_ref, kb_ref, vt_ref, acc_ref, xtail,
                    *, tm, tt, spt, tiles_per_seq):
    step = pl.program_id(0)
    tile = step - 1

    @pl.when(step == 0)
    def _():
        acc_ref[...] = jnp.zeros_like(acc_ref)
        xtail[...] = jnp.zeros_like(xtail)

    c_xbc, c_dt = SSD_WIDTH, SSD_WIDTH + SSD_XBC
    c_q = c_dt + DT_PAD
    c_k, c_v = c_q + ATT_WIDTH, c_q + 2 * ATT_WIDTH
    z_ref[...] = _silu(acc_ref[:, :SSD_WIDTH]).astype(BF16)
    rows = tm // spt
    row8 = lax.broadcasted_iota(jnp.int32, (SUBLANES, LANES), 0)
    first_tile_of_seq = tile % tiles_per_seq == 0
    for s_i in range(spt):
        r0 = s_i * rows
        tail_ref[s_i] = acc_ref[r0 + rows - SUBLANES:r0 + rows, c_xbc:c_dt]
        for cb in range(SSD_XBC // LANES):
            sl = slice(cb * LANES, (cb + 1) * LANES)
            piece = acc_ref[r0:r0 + rows, c_xbc + cb * LANES:c_xbc + (cb + 1) * LANES]
            if spt > 1:
                prev = cbuf_ref[s_i, :, sl]
            else:
                prev = jnp.where(first_tile_of_seq, cbuf_ref[0, :, sl], xtail[:, sl])
            conv_top = cb_ref[:, sl] + piece[0:SUBLANES] * cw_ref[SSD_CONV - 1:SSD_CONV, sl]
            conv_rest = cb_ref[:, sl] + piece[SUBLANES:] * cw_ref[SSD_CONV - 1:SSD_CONV, sl]
            for d in range(1, SSD_CONV):
                sh = pltpu.roll(piece, d, 0)
                tap = cw_ref[SSD_CONV - 1 - d:SSD_CONV - d, sl]
                conv_top = conv_top + jnp.where(row8 < d, pltpu.roll(prev, d, 0), sh[0:SUBLANES]) * tap
                conv_rest = conv_rest + sh[SUBLANES:] * tap
            u_ref[r0:r0 + rows, sl] = _silu(jnp.concatenate([conv_top, conv_rest], axis=0)).astype(BF16)
    if spt == 1:
        xtail[...] = acc_ref[tm - SUBLANES:, c_xbc:c_dt]
    dt_ref[...] = acc_ref[:, c_dt:c_dt + DT_PAD]
    if tt:
        for j in range(ATT_HEADS):
            v_ref[pl.ds(j, tm, stride=ATT_HEADS), :] = acc_ref[:, c_v + j * ATT_V_DIM:c_v + (j + 1) * ATT_V_DIM]
    else:
        v_ref[...] = acc_ref[:, c_v:c_v + ATT_WIDTH]

    cos = cos_ref[...]
    sin = sin_ref[...]
    first_half = (lax.broadcasted_iota(jnp.int32, (tm, LANES), 1) % ATT_HEAD_DIM) < (ATT_HEAD_DIM // 2)

    def rope(t):
        swapped = jnp.where(first_half, pltpu.roll(t, LANES - ATT_HEAD_DIM // 2, 1),
                            pltpu.roll(t, ATT_HEAD_DIM // 2, 1))
        return t * cos + swapped * sin

    for j in range(ATT_WIDTH // LANES):
        sl = slice(j * LANES, (j + 1) * LANES)
        qr = rope(acc_ref[:, c_q + j * LANES:c_q + (j + 1) * LANES]) * Q_SCALE
        kr = rope(acc_ref[:, c_k + j * LANES:c_k + (j + 1) * LANES])
        q_ref[:, sl] = qr.astype(BF16)
        if tt:
            k_ref[0, sl, :] = kr.T
            kb_ref[:, sl] = kr.astype(BF16)
            for c in range(tm // tt):
                qt_ref[c, sl, :] = qr[c * tt:(c + 1) * tt, :].T.astype(BF16)
                vt_ref[c, j * ATT_VT_ROWS:j * ATT_VT_ROWS + LANES, :] = acc_ref[
                    c * tt:(c + 1) * tt, c_v + j * LANES:c_v + (j + 1) * LANES].T.astype(BF16)
                vt_ref[c, j * ATT_VT_ROWS + LANES:(j + 1) * ATT_VT_ROWS, :] = jnp.ones((ATT_VT_ROWS - LANES, tt), BF16)
        else:
            k_ref[:, sl] = kr

    hn = _rms(x_ref[...], g_ref[...], NORM_EPS).astype(BF16)
    acc_ref[:, :c_q] = _dot(hn, w_ref[...])
    acc_ref[:, c_q:] = _dot(hn, wqkv_ref[...])


def _in_proj(x, g, w, wqkv, cos, sin, cbuf, cw, cb, *, seq, tm, tt):
    n = x.shape[0]
    nt = n // tm
    spt = max(1, tm // seq)
    tiles_per_seq = max(1, seq // tm)
    if spt > 1:
        cos, sin = jnp.tile(cos, (spt, 1)), jnp.tile(sin, (spt, 1))
    done = lambda i: jnp.maximum(i - 1, 0)
    row = lambda c: pl.BlockSpec((tm, c), lambda i: (done(i), 0))
    tab = pl.BlockSpec((tm, LANES), lambda i: (done(i) % tiles_per_seq, 0))
    if tt:
        k_spec = pl.BlockSpec((1, ATT_WIDTH, tm), lambda i: (done(i) // tiles_per_seq, 0, done(i) % tiles_per_seq))
        k_shape = jax.ShapeDtypeStruct((n // seq, ATT_WIDTH, seq), F32)
        v_spec = pl.BlockSpec((tm * ATT_HEADS, ATT_V_DIM), lambda i: (done(i), 0))
        v_shape = jax.ShapeDtypeStruct((n * ATT_HEADS, ATT_V_DIM), F32)
    else:
        k_spec, k_shape = row(ATT_WIDTH), jax.ShapeDtypeStruct((n, ATT_WIDTH), F32)
        v_spec, v_shape = row(ATT_WIDTH), jax.ShapeDtypeStruct((n, ATT_WIDTH), F32)
    out_specs = [row(SSD_WIDTH), row(SSD_XBC), pl.BlockSpec((spt, SUBLANES, SSD_XBC), lambda i: (done(i), 0, 0)),
                 row(DT_PAD), row(ATT_WIDTH), k_spec, v_spec]
    out_shape = [jax.ShapeDtypeStruct((n, SSD_WIDTH), BF16), jax.ShapeDtypeStruct((n, SSD_XBC), BF16),
                 jax.ShapeDtypeStruct((nt * spt, SUBLANES, SSD_XBC), F32), jax.ShapeDtypeStruct((n, DT_PAD), F32),
                 jax.ShapeDtypeStruct((n, ATT_WIDTH), BF16), k_shape, v_shape]
    if tt:
        tr = lambda r: pl.BlockSpec((tm // tt, r, tt), lambda i: (done(i), 0, 0))
        vt_rows = ATT_HEADS * ATT_VT_ROWS
        out_specs += [tr(ATT_WIDTH), row(ATT_WIDTH), tr(vt_rows)]
        out_shape += [jax.ShapeDtypeStruct((n // tt, ATT_WIDTH, tt), BF16), jax.ShapeDtypeStruct((n, ATT_WIDTH), BF16),
                      jax.ShapeDtypeStruct((n // tt, vt_rows, tt), BF16)]

    def body(*refs):
        refs, scratch = refs[:-2], refs[-2:]
        refs = refs + (None,) * (19 - len(refs))
        _in_proj_kernel(*refs, *scratch, tm=tm, tt=tt, spt=spt, tiles_per_seq=tiles_per_seq)

    outs = pl.pallas_call(
        body,
        grid=(nt + 1,),
        in_specs=[pl.BlockSpec((tm, D_MODEL), lambda i: (jnp.minimum(i, nt - 1), 0)),
                  _const_spec((1, D_MODEL)), _const_spec(w.shape), _const_spec(wqkv.shape), tab, tab,
                  pl.BlockSpec((spt, SUBLANES, SSD_XBC), lambda i: (done(i) // tiles_per_seq, 0, 0)),
                  _const_spec((SSD_CONV, SSD_XBC)), _const_spec((1, SSD_XBC))],
        out_specs=out_specs,
        out_shape=out_shape,
        scratch_shapes=[pltpu.VMEM((tm, IN_COLS_PADDED), F32), pltpu.VMEM((SUBLANES, SSD_XBC), F32)],
        compiler_params=pltpu.CompilerParams(dimension_semantics=("arbitrary",), vmem_limit_bytes=VMEM_LIMIT),
        name="in_proj",
    )(x, g, w, wqkv, cos, sin, cbuf, cw, cb)
    return list(outs) + [None] * (10 - len(outs))


def _expand_heads(x, e2):
    hi = x.astype(BF16)
    lo = (x - hi.astype(F32)).astype(BF16)
    return _dot(jnp.concatenate([hi, lo], axis=1), e2)


def _ssd_kernel(gate_ref, u_ref, dt_ref, h0_ref, dtb_ref, alog_ref, dsk_ref, nw_ref, e2_ref,
                y_ref, hout_ref, state, *, q, valid, chunks):
    step = pl.program_id(1)

    @pl.when(step == 0)
    def _():
        state[...] = h0_ref[0]

    for sc in range(chunks):
        rs = pl.ds(sc * q, q)
        _ssd_chunk(gate_ref.at[rs], u_ref.at[rs], dt_ref.at[rs], dtb_ref, alog_ref, dsk_ref, nw_ref, e2_ref,
                   y_ref.at[rs], state, q=q, valid=valid, c=step * chunks + sc)

    @pl.when(step == pl.num_programs(1) - 1)
    def _():
        hout_ref[0] = state[...]


def _ssd_chunk(gate_ref, u_ref, dt_ref, dtb_ref, alog_ref, dsk_ref, nw_ref, e2_ref, y_ref, state, *, q, valid, c):
    xs = u_ref[:, :SSD_WIDTH].astype(F32)
    bm = u_ref[:, SSD_WIDTH:SSD_WIDTH + SSD_GROUPS * SSD_STATE]
    cmb = u_ref[:, SSD_WIDTH + SSD_GROUPS * SSD_STATE:]

    dtr = dt_ref[...] + dtb_ref[...]
    dt = jnp.maximum(dtr, 0.0) + jnp.log(1.0 + jnp.exp(-jnp.abs(dtr)))
    if valid is not None:
        row = lax.broadcasted_iota(jnp.int32, (q, DT_PAD), 0) + c * q
        dt = jnp.where(row < valid, dt, 0.0)
    a = -jnp.exp(alog_ref[...])
    ad = dt * a
    ri = lax.broadcasted_iota(jnp.int32, (q, q), 0)
    ci = lax.broadcasted_iota(jnp.int32, (q, q), 1)
    tril = ri >= ci
    tril_b = jnp.where(tril, 1.0, 0.0).astype(BF16)
    ad_hi = ad.astype(BF16)
    ad_r = ad - ad_hi.astype(F32)
    ad_mid = ad_r.astype(BF16)
    ad_lo = (ad_r - ad_mid.astype(F32)).astype(BF16)
    acum = _dot(tril_b, ad_hi) + _dot(tril_b, ad_mid) + _dot(tril_b, ad_lo)
    acum_t = acum.T
    tot = acum[q - 1:q, :]
    e2 = e2_ref[...]
    expanded = _expand_heads(jnp.concatenate([dt, dt * jnp.exp(tot - acum), jnp.exp(acum)], axis=0), e2)
    dtx = expanded[0:q]
    ddx = expanded[q:2 * q]
    eax = expanded[2 * q:3 * q]
    dsk = _expand_heads(jnp.broadcast_to(dsk_ref[...], (SUBLANES, DT_PAD)), e2)[0:1]

    xdt = (xs * dtx).astype(BF16)
    xdtd = (xs * ddx).astype(BF16)
    bm_t = bm.astype(F32).T.astype(BF16)
    gw = SSD_REP * SSD_HEAD_DIM
    stripe = lax.broadcasted_iota(jnp.int32, (q, gw), 1) // SSD_HEAD_DIM
    ys = []
    for g in range(SSD_GROUPS):
        cm_g = cmb[:, g * SSD_STATE:(g + 1) * SSD_STATE]
        bt_g = bm_t[g * SSD_STATE:(g + 1) * SSD_STATE, :]
        cbm = _dot(cm_g, bt_g)
        ms = []
        for r in range(SSD_REP):
            h = g * SSD_REP + r
            diff = acum[:, h:h + 1] - acum_t[h:h + 1, :]
            ms.append((cbm * jnp.exp(jnp.where(tril, diff, -jnp.inf))).astype(BF16))
        ydf = _dot(jnp.concatenate(ms, axis=0), xdt[:, g * gw:(g + 1) * gw])
        yd = ydf[0:q]
        for r in range(1, SSD_REP):
            yd = jnp.where(stripe == r, ydf[r * q:(r + 1) * q], yd)
        st = state[g]
        y_off = _dot(cm_g, st.astype(BF16)) * eax[:, g * gw:(g + 1) * gw]
        state[g] = st * eax[q - 1:q, g * gw:(g + 1) * gw] + _dot(bt_g, xdtd[:, g * gw:(g + 1) * gw])
        ys.append(yd + y_off)
    y = jnp.concatenate(ys, axis=1) + dsk * xs
    y = y * gate_ref[...].astype(F32)
    outs = []
    for g in range(SSD_WIDTH // SSD_NORM_GROUP):
        yg = y[:, g * SSD_NORM_GROUP:(g + 1) * SSD_NORM_GROUP]
        outs.append(yg * lax.rsqrt(jnp.mean(yg * yg, axis=-1, keepdims=True) + SSD_NORM_EPS))
    y_ref[...] = (jnp.concatenate(outs, axis=1) * nw_ref[...]).astype(BF16)


def _ssd(gate, u, dt, h0, dtb, alog, dsk, nw, e2, *, nb, seq, q, valid):
    chunks = SSD_CHUNKS_PER_STEP if (seq // q) % SSD_CHUNKS_PER_STEP == 0 else 1
    nc = seq // (q * chunks)
    row = lambda c: pl.BlockSpec((q * chunks, c), lambda b, i: (b * nc + i, 0))
    gw = SSD_REP * SSD_HEAD_DIM
    st_spec = pl.BlockSpec((1, SSD_GROUPS, SSD_STATE, gw), lambda b, i: (b, 0, 0, 0))
    return pl.pallas_call(
        functools.partial(_ssd_kernel, q=q, valid=valid, chunks=chunks),
        grid=(nb, nc),
        in_specs=[row(SSD_WIDTH), row(SSD_XBC), row(DT_PAD), st_spec,
                  _const_spec((1, DT_PAD)), _const_spec((1, DT_PAD)), _const_spec((1, DT_PAD)),
                  _const_spec((1, SSD_WIDTH)), _const_spec((2 * DT_PAD, SSD_WIDTH))],
        out_specs=[row(SSD_WIDTH), st_spec],
        out_shape=[jax.ShapeDtypeStruct((nb * seq, SSD_WIDTH), BF16),
                   jax.ShapeDtypeStruct((nb, SSD_GROUPS, SSD_STATE, gw), F32)],
        scratch_shapes=[pltpu.VMEM((SSD_GROUPS, SSD_STATE, gw), F32)],
        compiler_params=pltpu.CompilerParams(dimension_semantics=("arbitrary", "arbitrary"),
                                             vmem_limit_bytes=VMEM_LIMIT),
        name="ssd",
    )(gate, u, dt, h0, dtb, alog, dsk, nw, e2)


def _column_max(x):
    while x.shape[0] > SUBLANES and x.shape[0] % (2 * SUBLANES) == 0:
        half = x.shape[0] // 2
        x = jnp.maximum(x[:half], x[half:])
    return jnp.max(x, axis=0, keepdims=True)


def _attn_tile_counts(i, *, tq, tk, past, kv_len, minimum=min):
    q_lo = past + i * tq
    q_hi = q_lo + tq - 1
    lim_lo = minimum((q_lo // CHUNK + 1) * CHUNK, kv_len)
    lim_hi = minimum((q_hi // CHUNK + 1) * CHUNK, kv_len)
    return lim_lo // tk, (lim_hi + tk - 1) // tk


def _diff_attn_kernel(qt_ref, qt_next_ref, k_ref, vt_ref, lam_ref, sw_ref, o_ref,
                      q2t_ref, s_ref, p_ref, m_ref, alpha_ref, acc_ref, *, tq, tk, past, kv_len, lam_init, nbuf, nkt):
    i = pl.program_id(2)
    w = 2 * tq
    row = lax.broadcasted_iota(jnp.int32, (ATT_V_DIM, tq), 0)
    n_full, _ = _attn_tile_counts(i, tq=tq, tk=tk, past=past, kv_len=kv_len, minimum=jnp.minimum)
    n_visits = n_full + 1

    def scores(j):
        kt = k_ref[pl.ds(pl.multiple_of(j * tk, tk), tk), :]
        return _dot(kt, q2t_ref[...])

    def visited_tile(v):
        return jnp.where(v == 0, n_full, jnp.maximum(v - 1, 0))

    ncb = w // LANES
    pw = p_ref.shape[-1]

    def store_scores(buf, s):
        for cb in range(ncb):
            s_ref[buf, cb] = s[:, cb * LANES:(cb + 1) * LANES]

    def stage_a(v, buf):
        store_scores(buf, scores(jnp.minimum(v - 1, n_full)))

    rows = min(tk, ATT_SOFTMAX_ROWS)

    def stage_b(buf):
        for cb in range(ncb):
            sl = slice(cb * LANES, (cb + 1) * LANES)
            m_old = m_ref[:, sl]
            m_new = m_old
            for r0 in range(0, tk, rows):
                m_new = jnp.maximum(m_new, _column_max(s_ref[buf, cb, r0:r0 + rows, :]))
            m_ref[:, sl] = m_new
            alpha_ref[buf, :, sl] = jnp.exp2(m_old - m_new)
            pl0 = (cb * LANES) % pw
            for r0 in range(0, tk, rows):
                p_ref[buf, cb * LANES // pw, r0:r0 + rows, pl0:pl0 + LANES] = jnp.exp2(
                    (s_ref[buf, cb, r0:r0 + rows, :] - m_new).astype(BF16))

    def stage_c(v, buf):
        vt = vt_ref[visited_tile(v)]
        for pb in range(w // pw):
            sl = slice(pb * pw, (pb + 1) * pw)
            acc_ref[:, sl] = alpha_ref[buf, :, sl] * acc_ref[:, sl] + _dot(vt, p_ref[buf, pb])

    def first_visit(qt, tile):
        zero = jnp.zeros_like(qt)
        q2t_ref[:, 0:tq] = jnp.where(row < ATT_HEAD_DIM, qt, zero)
        q2t_ref[:, tq:w] = jnp.where(row >= ATT_HEAD_DIM, qt, zero)
        nf, _ = _attn_tile_counts(tile, tq=tq, tk=tk, past=past, kv_len=kv_len, minimum=jnp.minimum)
        nf = jnp.minimum(nf, nkt - 1)
        qchunk = (past + tile * tq + lax.broadcasted_iota(jnp.int32, (1, w), 1) % tq) // CHUNK
        s_part = scores(nf)
        s_masked = []
        for kb in range(tk // CHUNK):
            k0 = nf * tk + kb * CHUNK
            kchunk = jnp.where(k0 < kv_len, k0 // CHUNK, jnp.iinfo(jnp.int32).max)
            s_masked.append(jnp.where(kchunk <= qchunk, s_part[kb * CHUNK:(kb + 1) * CHUNK], NEG_BIG))
        store_scores(0, jnp.concatenate(s_masked, axis=0))

    @pl.when(i == 0)
    def _():
        first_visit(qt_ref[0], i)

    m_ref[...] = jnp.full_like(m_ref, NEG_BIG)
    acc_ref[...] = jnp.zeros_like(acc_ref)
    p_ref[nbuf - 1] = jnp.zeros(p_ref.shape[1:], BF16)
    alpha_ref[nbuf - 1] = jnp.ones(alpha_ref.shape[1:], F32)

    def visits(v0, count, prefetch_last):
        for r in range(count):
            prefetch = r + 1 < count or prefetch_last
            if prefetch and nbuf > 1:
                stage_a(v0 + r + 1, (r + 1) % nbuf)
            stage_c(v0 + r - 1, (r - 1) % nbuf)
            stage_b(r)
            if prefetch and nbuf == 1:
                stage_a(v0 + r + 1, (r + 1) % nbuf)

    def trip(u, carry):
        visits(nbuf * u, nbuf, True)
        return carry

    lax.fori_loop(0, n_visits // nbuf, trip, 0)
    for rem in range(nbuf):

        @pl.when(n_visits % nbuf == rem)
        def _():
            visits(n_visits - rem, rem, False)
            stage_c(n_visits - 1, (rem - 1) % nbuf)

    first_visit(qt_next_ref[0], i + 1)

    lam_v = lam_ref[...]
    lam = (jnp.exp(jnp.sum(lam_v[0:1] * lam_v[1:2], axis=-1, keepdims=True))
           - jnp.exp(jnp.sum(lam_v[2:3] * lam_v[3:4], axis=-1, keepdims=True)) + lam_init)
    acc = acc_ref[...]
    o = acc[:ATT_V_DIM] * (1.0 / acc[ATT_V_DIM:ATT_V_DIM + 1])
    o = o[:, :tq] - lam * o[:, tq:]
    o = o * lax.rsqrt(jnp.mean(o * o, axis=0, keepdims=True) + ATT_NORM_EPS) * (sw_ref[...] * (1.0 - lam_init))
    o_ref[...] = o.T.astype(BF16)


def _diff_attn(qt, k, vt, lam_vecs, subln_col, *, nb, tq, tk, past, kv_len, lam_init, nbuf):
    nq = qt.shape[0] // nb
    nkt = vt.shape[0] // nb
    assert tk % CHUNK == 0 and kv_len % CHUNK == 0
    for i in range(nq):
        n_full, n_end = _attn_tile_counts(i, tq=tq, tk=tk, past=past, kv_len=kv_len)
        assert n_end - n_full == 1 and n_end <= nkt, (i, n_full, n_end)
    w = 2 * tq
    return pl.pallas_call(
        functools.partial(_diff_attn_kernel, tq=tq, tk=tk, past=past, kv_len=kv_len, lam_init=lam_init, nbuf=nbuf,
                          nkt=nkt),
        grid=(nb, ATT_HEADS, nq),
        in_specs=[pl.BlockSpec((1, ATT_V_DIM, tq), lambda b, h, i: (b * nq + i, h, 0)),
                  pl.BlockSpec((1, ATT_V_DIM, tq), lambda b, h, i: (b * nq + jnp.minimum(i + 1, nq - 1), h, 0)),
                  pl.BlockSpec((nkt * tk, ATT_V_DIM), lambda b, h, i: (b, h)),
                  pl.BlockSpec((nkt, ATT_VT_ROWS, tk), lambda b, h, i: (b, h, 0)),
                  _const_spec((4, ATT_HEAD_DIM)), _const_spec((ATT_V_DIM, 1))],
        out_specs=pl.BlockSpec((tq, ATT_V_DIM), lambda b, h, i: (b * nq + i, h)),
        out_shape=jax.ShapeDtypeStruct((nb * nq * tq, ATT_WIDTH), BF16),
        scratch_shapes=[pltpu.VMEM((ATT_V_DIM, w), BF16), pltpu.VMEM((nbuf, w // LANES, tk, LANES), F32),
                        pltpu.VMEM((nbuf, w // MXU_TILE, tk, MXU_TILE), BF16), pltpu.VMEM((1, w), F32),
                        pltpu.VMEM((nbuf, 1, w), F32),
                        pltpu.VMEM((ATT_VT_ROWS, w), F32)],
        compiler_params=pltpu.CompilerParams(dimension_semantics=("arbitrary", "arbitrary", "arbitrary"),
                                             vmem_limit_bytes=VMEM_LIMIT),
        name="diff_attn",
    )(qt, qt, k, vt, lam_vecs, subln_col)


def _decode_attn_kernel(q_ref, kn_ref, vn_ref, kc_ref, vc_ref, lam_ref, sw_ref, o_ref, *, seq, past, lam_init):
    lam_v = lam_ref[...]
    lam = (jnp.exp(jnp.sum(lam_v[0:1] * lam_v[1:2], axis=-1, keepdims=True))
           - jnp.exp(jnp.sum(lam_v[2:3] * lam_v[3:4], axis=-1, keepdims=True)) + lam_init)
    npad = LANES
    lane = lax.broadcasted_iota(jnp.int32, (seq, LANES), 1)
    qchunk = (past + lax.broadcasted_iota(jnp.int32, (2 * seq, 1), 0) % seq) // CHUNK
    kpos_p = lax.broadcasted_iota(jnp.int32, (2 * seq, past), 1)
    kpos_n = lax.broadcasted_iota(jnp.int32, (2 * seq, npad), 1)
    vis_p = kpos_p // CHUNK <= qchunk
    vis_n = ((past + kpos_n) // CHUNK <= qchunk) & (kpos_n < seq)
    pad_rows = jnp.zeros((npad - seq, LANES), BF16)
    for h in range(ATT_HEADS):
        sl = slice(h * LANES, (h + 1) * LANES)
        qh = q_ref[:, sl]
        zero = jnp.zeros_like(qh)
        q2 = jnp.concatenate([jnp.where(lane < ATT_HEAD_DIM, qh, zero), jnp.where(lane >= ATT_HEAD_DIM, qh, zero)], axis=0)
        s_p = jnp.where(vis_p, _dot(q2, kc_ref[0, sl, :].astype(BF16)), NEG_BIG)
        kn = jnp.concatenate([kn_ref[:, sl].astype(BF16), pad_rows], axis=0)
        s_n = lax.dot_general(q2, kn, (((1,), (1,)), ((), ())), preferred_element_type=F32)
        s_n = jnp.where(vis_n, s_n, NEG_BIG)
        m = jnp.maximum(jnp.max(s_p, axis=-1, keepdims=True), jnp.max(s_n, axis=-1, keepdims=True))
        p_p = jnp.exp2(s_p - m)
        p_n = jnp.exp2(s_n - m)
        l = jnp.sum(p_p, axis=-1, keepdims=True) + jnp.sum(p_n, axis=-1, keepdims=True)
        vh = vc_ref[0, pl.ds(h, past, stride=ATT_HEADS), :].astype(BF16)
        vn = jnp.concatenate([vn_ref[:, sl].astype(BF16), pad_rows], axis=0)
        o = (_dot(p_p.astype(BF16), vh) + _dot(p_n.astype(BF16), vn)) / l
        o = o[:seq] - lam * o[seq:]
        o = o * lax.rsqrt(jnp.mean(o * o, axis=-1, keepdims=True) + ATT_NORM_EPS) * (sw_ref[...] * (1.0 - lam_init))
        o_ref[:, sl] = o.astype(BF16)


def _decode_attn(q, k_new, v_new, kt_cache, v_cache, lam_vecs, subln_row, *, nb, seq, past, lam_init):
    row = pl.BlockSpec((seq, ATT_WIDTH), lambda b: (b, 0))
    return pl.pallas_call(
        functools.partial(_decode_attn_kernel, seq=seq, past=past, lam_init=lam_init),
        grid=(nb,),
        in_specs=[row, row, row,
                  pl.BlockSpec((1, ATT_WIDTH, past), lambda b: (b, 0, 0)),
                  pl.BlockSpec((1, past * ATT_HEADS, ATT_V_DIM), lambda b: (b, 0, 0)),
                  _const_spec((4, ATT_HEAD_DIM)), _const_spec((1, ATT_V_DIM))],
        out_specs=row,
        out_shape=jax.ShapeDtypeStruct((nb * seq, ATT_WIDTH), BF16),
        compiler_params=pltpu.CompilerParams(dimension_semantics=("arbitrary",), vmem_limit_bytes=VMEM_LIMIT),
        name="decode_attn",
    )(q, k_new, v_new, kt_cache, v_cache, lam_vecs, subln_row)


def _post_mix_kernel(x_ref, ys_ref, ya_ref, mk_ref, mv_ref, wo_ref, wq_ref, wox_ref,
                     g1_ref, g2_ref, g3_ref, h_ref, *, spt, rows, cached_mem):
    lane_blocks = MEM_HEAD_DIM // LANES

    def mem_head(ref, s_i, hd):
        if not cached_mem:
            return ref[s_i, :, hd * MEM_HEAD_DIM:(hd + 1) * MEM_HEAD_DIM]
        parts = [ref[s_i, pl.ds(cb * MEM_HEADS + hd, MEM_LEN, stride=lane_blocks * MEM_HEADS), :].astype(BF16)
                 for cb in range(lane_blocks)]
        return jnp.concatenate(parts, axis=1)

    mix = _dot(ys_ref[...], wo_ref[0:SSD_WIDTH, :]) + _dot(ya_ref[...], wo_ref[SSD_WIDTH:, :])
    h = x_ref[...] + _rms(mix, g1_ref[...], NORM_EPS)
    qn = _rms(h, g2_ref[...], NORM_EPS).astype(BF16)
    qx = (_dot(qn, wq_ref[...]) * (math.log2(math.e) / math.sqrt(MEM_HEAD_DIM))).astype(BF16)
    ox_seqs = []
    for s_i in range(spt):
        qs = qx[s_i * rows:(s_i + 1) * rows]
        oxs = []
        for hd in range(MEM_HEADS):
            sl = slice(hd * MEM_HEAD_DIM, (hd + 1) * MEM_HEAD_DIM)
            s = lax.dot_general(qs[:, sl], mem_head(mk_ref, s_i, hd), (((1,), (1,)), ((), ())),
                                preferred_element_type=F32)
            p = jnp.exp2(s - jnp.max(s, axis=-1, keepdims=True))
            ox = _dot(p.astype(BF16), mem_head(mv_ref, s_i, hd)) * (1.0 / jnp.sum(p, axis=-1, keepdims=True))
            oxs.append(ox.astype(BF16))
        ox_seqs.append(jnp.concatenate(oxs, axis=1))
    ox_all = ox_seqs[0] if spt == 1 else jnp.concatenate(ox_seqs, axis=0)
    o2 = _dot(ox_all, wox_ref[...])
    h_ref[...] = h + _rms(o2, g3_ref[...], NORM_EPS)


def _post_mix(x, ys, ya, mk, mv, w_out, wq, wox, g1, g2, g3, *, seq, tm):
    n = x.shape[0]
    spt = max(1, tm // seq)
    tiles_per_seq = max(1, seq // tm)
    row = lambda c: pl.BlockSpec((tm, c), lambda i: (i, 0))
    cached_mem = mk.dtype == F32
    mem = pl.BlockSpec((spt,) + mk.shape[1:], lambda i: (i // tiles_per_seq, 0, 0))
    wspec = _const_spec((D_MODEL, D_MODEL))
    gspec = _const_spec((1, D_MODEL))
    return pl.pallas_call(
        functools.partial(_post_mix_kernel, spt=spt, rows=tm // spt, cached_mem=cached_mem),
        grid=(n // tm,),
        in_specs=[row(D_MODEL), row(SSD_WIDTH), row(ATT_WIDTH), mem, mem, wspec, wspec, wspec, gspec, gspec, gspec],
        out_specs=row(D_MODEL),
        out_shape=jax.ShapeDtypeStruct((n, D_MODEL), F32),
        compiler_params=pltpu.CompilerParams(dimension_semantics=("arbitrary",), vmem_limit_bytes=VMEM_LIMIT),
        name="post_mix",
    )(x, ys, ya, mk, mv, w_out, wq, wox, g1, g2, g3)


def _ffn_kernel(h_ref, wg_ref, wu_ref, wd_ref, g1_ref, g2_ref, o_ref):
    h = h_ref[...]
    hn = _rms(h, g1_ref[...], NORM_EPS).astype(BF16)
    act = (_silu(_dot(hn, wg_ref[...])) * _dot(hn, wu_ref[...])).astype(BF16)
    f = _dot(act, wd_ref[...])
    o_ref[...] = h + _rms(f, g2_ref[...], NORM_EPS)


def _ffn(h, wg, wu, wd, g1, g2, *, tm):
    n = h.shape[0]
    row = pl.BlockSpec((tm, D_MODEL), lambda i: (i, 0))
    return pl.pallas_call(
        _ffn_kernel,
        grid=(n // tm,),
        in_specs=[row, _const_spec((D_MODEL, FFN_HIDDEN)), _const_spec((D_MODEL, FFN_HIDDEN)),
                  _const_spec((FFN_HIDDEN, D_MODEL)), _const_spec((1, D_MODEL)), _const_spec((1, D_MODEL))],
        out_specs=row,
        out_shape=jax.ShapeDtypeStruct((n, D_MODEL), F32),
        compiler_params=pltpu.CompilerParams(dimension_semantics=("arbitrary",), vmem_limit_bytes=VMEM_LIMIT),
        name="ffn",
    )(h, wg, wu, wd, g1, g2)


def _rope_tables(past, seq):
    inv = np.power(ROPE_THETA, -np.arange(0, ATT_HEAD_DIM, 2, dtype=np.float64) / ATT_HEAD_DIM)
    ang = (past + np.arange(seq, dtype=np.float64))[:, None] * inv[None, :]
    cos, sin = np.cos(ang), np.sin(ang)
    reps = LANES // ATT_HEAD_DIM
    cos_t = np.tile(np.concatenate([cos, cos], axis=-1), (1, reps)).astype(np.float32)
    sin_t = np.tile(np.concatenate([-sin, sin], axis=-1), (1, reps)).astype(np.float32)
    return jnp.asarray(cos_t), jnp.asarray(sin_t)


def _mem_cache_rows(c):
    b = c.shape[0]
    c = c.reshape(b, MEM_LEN, MEM_HEADS, MEM_HEAD_DIM // LANES, LANES)
    return c.transpose(0, 1, 3, 2, 4).reshape(b, MEM_LEN * (MEM_HEAD_DIM // LANES) * MEM_HEADS, LANES)


def _state_to_kernel_layout(s):
    b = s.shape[0]
    s = s.reshape(b, SSD_GROUPS, SSD_REP, SSD_HEAD_DIM, SSD_STATE)
    return s.transpose(0, 1, 4, 2, 3).reshape(b, SSD_GROUPS, SSD_STATE, SSD_REP * SSD_HEAD_DIM)


def _state_from_kernel_layout(s):
    b = s.shape[0]
    s = s.reshape(b, SSD_GROUPS, SSD_STATE, SSD_REP, SSD_HEAD_DIM)
    return s.transpose(0, 1, 3, 4, 2).reshape(b, SSD_HEADS, SSD_HEAD_DIM, SSD_STATE)


def _layer(x, conv_buf, ssm0, kt_past, v_past, mem_kb, mem_vb, lam_init, p, *, tm, tk, tq, ssd_q):
    nb, seq, _ = x.shape
    n = nb * seq
    past = 0 if kt_past is None else kt_past.shape[2]
    xf = x.reshape(n, D_MODEL)
    cos, sin = _rope_tables(past, seq)
    no_history = kt_past is None
    assert tq == tk or not no_history
    cbuf = jnp.pad(conv_buf.astype(F32), ((0, 0), (SUBLANES - (SSD_CONV - 1), 0), (0, 0)))
    gate, u, tail, dt, q, k, v, qt, kb, vt = _in_proj(xf, p["g_pre_mix"], p["w_in_ssd"], p["w_in_qkv"], cos, sin, cbuf,
                                                       p["conv_w"], p["conv_b"], seq=seq, tm=tm,
                                                       tt=tk if no_history else 0)

    seq_pad = -(-seq // ssd_q) * ssd_q
    if seq_pad != seq:
        pad = lambda a: jnp.pad(a.reshape(nb, seq, -1), ((0, 0), (0, seq_pad - seq), (0, 0))).reshape(nb * seq_pad, -1)
        gate_s, u_s, dt_s = pad(gate), pad(u), pad(dt)
    else:
        gate_s, u_s, dt_s = gate, u, dt
    y_ssd, h_new = _ssd(gate_s, u_s, dt_s, _state_to_kernel_layout(ssm0.astype(F32)),
                        p["dt_bias"], p["a_log"], p["d_skip"], p["ssm_norm_w"], p["e2"],
                        nb=nb, seq=seq_pad, q=ssd_q, valid=None if seq_pad == seq else seq)
    if seq_pad != seq:
        y_ssd = y_ssd.reshape(nb, seq_pad, SSD_WIDTH)[:, :seq].reshape(n, SSD_WIDTH)
    ssm_new = _state_from_kernel_layout(h_new)
    ext_tail = tail.reshape(nb, -1, SUBLANES, SSD_XBC)[:, -1]
    if seq >= SSD_CONV - 1:
        conv_new = ext_tail[:, SUBLANES - (SSD_CONV - 1):]
    else:
        conv_new = jnp.concatenate([conv_buf.astype(F32), ext_tail[:, SUBLANES - seq:]], axis=1)[:, -(SSD_CONV - 1):]

    if no_history:
        kv_len = seq
        y_att = _diff_attn(qt, kb, vt, p["lam_vecs"], p["subln_col"], nb=nb, tq=tq, tk=tk,
                           past=past, kv_len=kv_len, lam_init=lam_init, nbuf=ATT_PIPELINE_BUFFERS)
    else:
        y_att = _decode_attn(q, k, v, kt_past, v_past, p["lam_vecs"], p["subln_col"].reshape(1, ATT_V_DIM),
                             nb=nb, seq=seq, past=past, lam_init=lam_init)

    h = _post_mix(xf, y_ssd, y_att, mem_kb, mem_vb, p["w_out"], p["wq_x"], p["wo_x"],
                  p["g_post_mix"], p["g_pre_x"], p["g_post_x"], seq=seq, tm=tm)
    out = _ffn(h, p["w_gate"], p["w_up"], p["w_down"], p["g_pre_ffn"], p["g_post_ffn"], tm=tm)
    if no_history:
        k_out = k.reshape(nb, ATT_HEADS, 2, ATT_HEAD_DIM, seq).transpose(0, 4, 1, 2, 3)
    else:
        k_out = k.reshape(nb, seq, ATT_HEADS, 2, ATT_HEAD_DIM)
    return (out.reshape(nb, seq, D_MODEL), k_out,
            v.reshape(nb, seq, ATT_HEADS, ATT_V_DIM), ssm_new, conv_new)


def _prep_params(i, w_in, conv_w, conv_b, dt_bias, a_log, d_skip, ssm_norm_w, lam_q1, lam_k1, lam_q2, lam_k2, subln_w,
                 w_out, wq_x, wo_x, g_pre_mix, g_post_mix, g_pre_x, g_post_x, g_pre_ffn, g_post_ffn,
                 w_gate, w_up, w_down):
    w = w_in[i]
    s1 = SSD_WIDTH + SSD_XBC + SSD_HEADS
    w_ssd = jnp.pad(w[:, :s1].astype(BF16), ((0, 0), (0, DT_PAD - SSD_HEADS)))
    w_qkv = w[:, s1:].astype(BF16)
    head_pad = lambda a: jnp.pad(a[i].astype(F32), (0, DT_PAD - SSD_HEADS)).reshape(1, DT_PAD)
    e = (jnp.arange(DT_PAD)[:, None] == (jnp.arange(SSD_WIDTH)[None, :] // SSD_HEAD_DIM)).astype(BF16)
    row = lambda a: a[i].astype(F32).reshape(1, -1)
    w_out_b, wq_b, wo_b = _cast_bf16([w_out[i], wq_x[i], wo_x[i]], steps=4)
    w_gate_b, w_up_b, w_down_b = _cast_bf16([w_gate[i], w_up[i], w_down[i]], steps=8)
    return {
        "w_in_ssd": w_ssd, "w_in_qkv": w_qkv, "conv_w": conv_w[i].astype(F32), "conv_b": row(conv_b),
        "dt_bias": head_pad(dt_bias), "a_log": head_pad(a_log), "d_skip": head_pad(d_skip),
        "ssm_norm_w": row(ssm_norm_w), "e2": jnp.concatenate([e, e], axis=0),
        "lam_vecs": jnp.stack([lam_q1[i], lam_k1[i], lam_q2[i], lam_k2[i]]).astype(F32), "subln_col": subln_w[i].astype(F32).reshape(ATT_V_DIM, 1),
        "w_out": w_out_b, "wq_x": wq_b, "wo_x": wo_b,
        "g_pre_mix": row(g_pre_mix), "g_post_mix": row(g_post_mix), "g_pre_x": row(g_pre_x), "g_post_x": row(g_post_x),
        "g_pre_ffn": row(g_pre_ffn), "g_post_ffn": row(g_post_ffn),
        "w_gate": w_gate_b, "w_up": w_up_b, "w_down": w_down_b,
    }


def kernel(x_prompt, x_sample, cache_attn_k, cache_attn_v, state_ssm, state_conv, cache_mem_k, cache_mem_v, mem_prompt, w_in, conv_w, conv_b, dt_bias, a_log, d_skip, ssm_norm_w, lam_q1, lam_k1, lam_q2, lam_k2, subln_w, w_out, g_mem, wq_x, wk_x, wv_x, wo_x, g_pre_mix, g_post_mix, g_pre_x, g_post_x, g_pre_ffn, g_post_ffn, w_gate, w_up, w_down):
    depth = w_in.shape[0]
    bp, sp, _ = x_prompt.shape
    bs, ss, _ = x_sample.shape
    hp, hs = x_prompt, x_sample
    outs = [[] for _ in range(10)]
    for i in range(depth):
        lam_init = 0.8 - 0.6 * math.exp(-0.3 * i)
        p = _prep_params(i, w_in, conv_w, conv_b, dt_bias, a_log, d_skip, ssm_norm_w, lam_q1, lam_k1, lam_q2, lam_k2,
                         subln_w, w_out, wq_x, wo_x, g_pre_mix, g_post_mix, g_pre_x, g_post_x, g_pre_ffn, g_post_ffn,
                         w_gate, w_up, w_down)
        mk, mv, mkb, mvb = _mem_kv(mem_prompt.reshape(bp * MEM_LEN, D_MODEL), g_mem[i].reshape(1, D_MODEL),
                                   wk_x[i], wv_x[i])
        hp, k_new, v_new, ssm_new, conv_new = _layer(
            hp, jnp.zeros((bp, SSD_CONV - 1, SSD_XBC), F32), jnp.zeros((bp, SSD_HEADS, SSD_HEAD_DIM, SSD_STATE), F32),
            None, None, mkb.reshape(bp, MEM_LEN, D_MODEL), mvb.reshape(bp, MEM_LEN, D_MODEL), lam_init, p,
            tm=512, tk=512, tq=512, ssd_q=256)
        for lst, val in zip(outs[:6], (k_new, v_new, ssm_new, conv_new,
                                       mk.reshape(bp, MEM_LEN, MEM_HEADS, MEM_HEAD_DIM),
                                       mv.reshape(bp, MEM_LEN, MEM_HEADS, MEM_HEAD_DIM))):
            lst.append(val)
        past = cache_attn_k.shape[2]
        hs, k_new, v_new, ssm_new, conv_new = _layer(
            hs, state_conv[i], state_ssm[i],
            cache_attn_k[i].transpose(0, 2, 3, 4, 1).reshape(bs, ATT_WIDTH, past),
            cache_attn_v[i].reshape(bs, past * ATT_HEADS, ATT_V_DIM),
            _mem_cache_rows(cache_mem_k[i]), _mem_cache_rows(cache_mem_v[i]), lam_init, p,
            tm=bs * ss, tk=0, tq=0, ssd_q=ss)
        for lst, val in zip(outs[6:], (k_new, v_new, ssm_new, conv_new)):
            lst.append(val)
    return (hp, hs) + tuple(jnp.stack(o) for o in outs)
```

```python
import functools
import math

import jax
import jax.numpy as jnp
import numpy as np
from jax import lax
from jax.experimental import pallas as pl
from jax.experimental.pallas import tpu as pltpu

D_MODEL = 1024
CHUNK = 64
SSD_WIDTH = 512
SSD_HEAD_DIM = 64
SSD_HEADS = 8
SSD_GROUPS = 2
SSD_REP = 4
SSD_STATE = 128
SSD_CONV = 4
SSD_XBC = 1024
SSD_NORM_GROUP = 256
SSD_NORM_EPS = 1e-5
ATT_WIDTH = 512
ATT_HEAD_DIM = 64
ATT_HEADS = 4
ATT_V_DIM = 128
ATT_NORM_EPS = 1e-5
ROPE_THETA = 10000.0
MEM_LEN = 256
MEM_HEADS = 4
MEM_HEAD_DIM = 256
FFN_HIDDEN = 2816
NORM_EPS = 1e-6
LANES = 128
SUBLANES = 8
DT_PAD = LANES
IN_COLS_PADDED = SSD_WIDTH + SSD_XBC + 3 * ATT_WIDTH + DT_PAD
VMEM_LIMIT = 56 * 1024 * 1024
NEG_BIG = -1e30
MXU_TILE = 256
BF16_SUBLANES = 16
ATT_VT_ROWS = ATT_V_DIM + BF16_SUBLANES
SSD_CHUNKS_PER_STEP = 4
ATT_SOFTMAX_ROWS = 128
ATT_PIPELINE_BUFFERS = 2
Q_SCALE = math.log2(math.e) / math.sqrt(ATT_HEAD_DIM)

F32 = jnp.float32
BF16 = jnp.bfloat16


def _const_spec(shape):
    return pl.BlockSpec(shape, lambda *_: (0,) * len(shape), pipeline_mode=pl.Buffered(1))


def _rms(x, g, eps):
    return x * lax.rsqrt(jnp.mean(x * x, axis=-1, keepdims=True) + eps) * g


def _silu(x):
    h = 0.5 * x
    return h + h * jnp.tanh(h)


def _dot(a, b):
    return jnp.dot(a, b, preferred_element_type=F32)


def _cast_kernel(*refs):
    n = len(refs) // 2
    for x_ref, o_ref in zip(refs[:n], refs[n:]):
        o_ref[...] = x_ref[...].astype(BF16)


def _cast_bf16(arrays, steps):
    specs = [pl.BlockSpec((a.shape[0] // steps, a.shape[1]), lambda i: (i, 0)) for a in arrays]
    assert all(a.shape[0] % (steps * BF16_SUBLANES) == 0 for a in arrays)
    return pl.pallas_call(
        _cast_kernel,
        grid=(steps,),
        in_specs=specs,
        out_specs=specs,
        out_shape=[jax.ShapeDtypeStruct(a.shape, BF16) for a in arrays],
        compiler_params=pltpu.CompilerParams(dimension_semantics=("arbitrary",), vmem_limit_bytes=VMEM_LIMIT),
        name="cast_bf16",
    )(*arrays)


def _mem_kv_kernel(mem_ref, g_ref, wk_ref, wv_ref, mk_ref, mv_ref, mkb_ref, mvb_ref):
    mn = _rms(mem_ref[...], g_ref[...], NORM_EPS).astype(BF16)
    mk = _dot(mn, wk_ref[...].astype(BF16))
    mv = _dot(mn, wv_ref[...].astype(BF16))
    mk_ref[...] = mk
    mv_ref[...] = mv
    mkb_ref[...] = mk.astype(BF16)
    mvb_ref[...] = mv.astype(BF16)


def _mem_kv(mem, g_mem, wk, wv):
    n = mem.shape[0]
    tm = n
    row = pl.BlockSpec((tm, D_MODEL), lambda i: (i, 0))
    return pl.pallas_call(
        _mem_kv_kernel,
        grid=(n // tm,),
        in_specs=[row, _const_spec((1, D_MODEL)), _const_spec((D_MODEL, D_MODEL)), _const_spec((D_MODEL, D_MODEL))],
        out_specs=[row, row, row, row],
        out_shape=[jax.ShapeDtypeStruct((n, D_MODEL), F32), jax.ShapeDtypeStruct((n, D_MODEL), F32),
                   jax.ShapeDtypeStruct((n, D_MODEL), BF16), jax.ShapeDtypeStruct((n, D_MODEL), BF16)],
        compiler_params=pltpu.CompilerParams(dimension_semantics=("arbitrary",), vmem_limit_bytes=VMEM_LIMIT),
        name="mem_kv",
    )(mem, g_mem, wk, wv)


def _in_proj_kernel(x_ref, g_ref, w_ref, wqkv_ref, cos_ref, sin_ref, cbuf_ref, cw_ref, cb_ref,
                    z_ref, u_ref, tail_ref, dt_ref, q_ref, k_ref, v_ref, qt_ref, kb_ref, vt_ref, acc_ref, xtail,
                    *, tm, tt, spt, tiles_per_seq):
    step = pl.program_id(0)
    tile = step - 1

    @pl.when(step == 0)
    def _():
        acc_ref[...] = jnp.zeros_like(acc_ref)
        xtail[...] = jnp.zeros_like(xtail)

    c_xbc, c_dt = SSD_WIDTH, SSD_WIDTH + SSD_XBC
    c_q = c_dt + DT_PAD
    c_k, c_v = c_q + ATT_WIDTH, c_q + 2 * ATT_WIDTH
    z_ref[...] = _silu(acc_ref[:, :SSD_WIDTH]).astype(BF16)
    rows = tm // spt
    row8 = lax.broadcasted_iota(jnp.int32, (SUBLANES, LANES), 0)
    first_tile_of_seq = tile % tiles_per_seq == 0
    for s_i in range(spt):
        r0 = s_i * rows
        tail_ref[s_i] = acc_ref[r0 + rows - SUBLANES:r0 + rows, c_xbc:c_dt]
        for cb in range(SSD_XBC // LANES):
            sl = slice(cb * LANES, (cb + 1) * LANES)
            piece = acc_ref[r0:r0 + rows, c_xbc + cb * LANES:c_xbc + (cb + 1) * LANES]
            if spt > 1:
                prev = cbuf_ref[s_i, :, sl]
            else:
                prev = jnp.where(first_tile_of_seq, cbuf_ref[0, :, sl], xtail[:, sl])
            conv_top = cb_ref[:, sl] + piece[0:SUBLANES] * cw_ref[SSD_CONV - 1:SSD_CONV, sl]
            conv_rest = cb_ref[:, sl] + piece[SUBLANES:] * cw_ref[SSD_CONV - 1:SSD_CONV, sl]
            for d in range(1, SSD_CONV):
                sh = pltpu.roll(piece, d, 0)
                tap = cw_ref[SSD_CONV - 1 - d:SSD_CONV - d, sl]
                conv_top = conv_top + jnp.where(row8 < d, pltpu.roll(prev, d, 0), sh[0:SUBLANES]) * tap
                conv_rest = conv_rest + sh[SUBLANES:] * tap
            u_ref[r0:r0 + rows, sl] = _silu(jnp.concatenate([conv_top, conv_rest], axis=0)).astype(BF16)
    if spt == 1:
        xtail[...] = acc_ref[tm - SUBLANES:, c_xbc:c_dt]
    dt_ref[...] = acc_ref[:, c_dt:c_dt + DT_PAD]
    if tt:
        for j in range(ATT_HEADS):
            v_ref[pl.ds(j, tm, stride=ATT_HEADS), :] = acc_ref[:, c_v + j * ATT_V_DIM:c_v + (j + 1) * ATT_V_DIM]
    else:
        v_ref[...] = acc_ref[:, c_v:c_v + ATT_WIDTH]

    cos = cos_ref[...]
    sin = sin_ref[...]
    first_half = (lax.broadcasted_iota(jnp.int32, (tm, LANES), 1) % ATT_HEAD_DIM) < (ATT_HEAD_DIM // 2)

    def rope(t):
        swapped = jnp.where(first_half, pltpu.roll(t, LANES - ATT_HEAD_DIM // 2, 1),
                            pltpu.roll(t, ATT_HEAD_DIM // 2, 1))
        return t * cos + swapped * sin

    for j in range(ATT_WIDTH // LANES):
        sl = slice(j * LANES, (j + 1) * LANES)
        qr = rope(acc_ref[:, c_q + j * LANES:c_q + (j + 1) * LANES]) * Q_SCALE
        kr = rope(acc_ref[:, c_k + j * LANES:c_k + (j + 1) * LANES])
        q_ref[:, sl] = qr.astype(BF16)
        if tt:
            k_ref[0, sl, :] = kr.T
            kb_ref[:, sl] = kr.astype(BF16)
            for c in range(tm // tt):
                qt_ref[c, sl, :] = qr[c * tt:(c + 1) * tt, :].T.astype(BF16)
                vt_ref[c, j * ATT_VT_ROWS:j * ATT_VT_ROWS + LANES, :] = acc_ref[
                    c * tt:(c + 1) * tt, c_v + j * LANES:c_v + (j + 1) * LANES].T.astype(BF16)
                vt_ref[c, j * ATT_VT_ROWS + LANES:(j + 1) * ATT_VT_ROWS, :] = jnp.ones((ATT_VT_ROWS - LANES, tt), BF16)
        else:
            k_ref[:, sl] = kr

    hn = _rms(x_ref[...], g_ref[...], NORM_EPS).astype(BF16)
    acc_ref[:, :c_q] = _dot(hn, w_ref[...])
    acc_ref[:, c_q:] = _dot(hn, wqkv_ref[...])


def _in_proj(x, g, w, wqkv, cos, sin, cbuf, cw, cb, *, seq, tm, tt):
    n = x.shape[0]
    nt = n // tm
    spt = max(1, tm // seq)
    tiles_per_seq = max(1, seq // tm)
    if spt > 1:
        cos, sin = jnp.tile(cos, (spt, 1)), jnp.tile(sin, (spt, 1))
    done = lambda i: jnp.maximum(i - 1, 0)
    row = lambda c: pl.BlockSpec((tm, c), lambda i: (done(i), 0))
    tab = pl.BlockSpec((tm, LANES), lambda i: (done(i) % tiles_per_seq, 0))
    if tt:
        k_spec = pl.BlockSpec((1, ATT_WIDTH, tm), lambda i: (done(i) // tiles_per_seq, 0, done(i) % tiles_per_seq))
        k_shape = jax.ShapeDtypeStruct((n // seq, ATT_WIDTH, seq), F32)
        v_spec = pl.BlockSpec((tm * ATT_HEADS, ATT_V_DIM), lambda i: (done(i), 0))
        v_shape = jax.ShapeDtypeStruct((n * ATT_HEADS, ATT_V_DIM), F32)
    else:
        k_spec, k_shape = row(ATT_WIDTH), jax.ShapeDtypeStruct((n, ATT_WIDTH), F32)
        v_spec, v_shape = row(ATT_WIDTH), jax.ShapeDtypeStruct((n, ATT_WIDTH), F32)
    out_specs = [row(SSD_WIDTH), row(SSD_XBC), pl.BlockSpec((spt, SUBLANES, SSD_XBC), lambda i: (done(i), 0, 0)),
                 row(DT_PAD), row(ATT_WIDTH), k_spec, v_spec]
    out_shape = [jax.ShapeDtypeStruct((n, SSD_WIDTH), BF16), jax.ShapeDtypeStruct((n, SSD_XBC), BF16),
                 jax.ShapeDtypeStruct((nt * spt, SUBLANES, SSD_XBC), F32), jax.ShapeDtypeStruct((n, DT_PAD), F32),
                 jax.ShapeDtypeStruct((n, ATT_WIDTH), BF16), k_shape, v_shape]
    if tt:
        tr = lambda r: pl.BlockSpec((tm // tt, r, tt), lambda i: (done(i), 0, 0))
        vt_rows = ATT_HEADS * ATT_VT_ROWS
        out_specs += [tr(ATT_WIDTH), row(ATT_WIDTH), tr(vt_rows)]
        out_shape += [jax.ShapeDtypeStruct((n // tt, ATT_WIDTH, tt), BF16), jax.ShapeDtypeStruct((n, ATT_WIDTH), BF16),
                      jax.ShapeDtypeStruct((n // tt, vt_rows, tt), BF16)]

    def body(*refs):
        refs, scratch = refs[:-2], refs[-2:]
        refs = refs + (None,) * (19 - len(refs))
        _in_proj_kernel(*refs, *scratch, tm=tm, tt=tt, spt=spt, tiles_per_seq=tiles_per_seq)

    outs = pl.pallas_call(
        body,
        grid=(nt + 1,),
        in_specs=[pl.BlockSpec((tm, D_MODEL), lambda i: (jnp.minimum(i, nt - 1), 0)),
                  _const_spec((1, D_MODEL)), _const_spec(w.shape), _const_spec(wqkv.shape), tab, tab,
                  pl.BlockSpec((spt, SUBLANES, SSD_XBC), lambda i: (done(i) // tiles_per_seq, 0, 0)),
                  _const_spec((SSD_CONV, SSD_XBC)), _const_spec((1, SSD_XBC))],
        out_specs=out_specs,
        out_shape=out_shape,
        scratch_shapes=[pltpu.VMEM((tm, IN_COLS_PADDED), F32), pltpu.VMEM((SUBLANES, SSD_XBC), F32)],
        compiler_params=pltpu.CompilerParams(dimension_semantics=("arbitrary",), vmem_limit_bytes=VMEM_LIMIT),
        name="in_proj",
    )(x, g, w, wqkv, cos, sin, cbuf, cw, cb)
    return list(outs) + [None] * (10 - len(outs))


def _expand_heads(x, e2):
    hi = x.astype(BF16)
    lo = (x - hi.astype(F32)).astype(BF16)
    return _dot(jnp.concatenate([hi, lo], axis=1), e2)


def _ssd_kernel(gate_ref, u_ref, dt_ref, h0_ref, dtb_ref, alog_ref, dsk_ref, nw_ref, e2_ref,
                y_ref, hout_ref, state, *, q, valid, chunks):
    step = pl.program_id(1)

    @pl.when(step == 0)
    def _():
        state[...] = h0_ref[0]

    for sc in range(chunks):
        rs = pl.ds(sc * q, q)
        _ssd_chunk(gate_ref.at[rs], u_ref.at[rs], dt_ref.at[rs], dtb_ref, alog_ref, dsk_ref, nw_ref, e2_ref,
                   y_ref.at[rs], state, q=q, valid=valid, c=step * chunks + sc)

    @pl.when(step == pl.num_programs(1) - 1)
    def _():
        hout_ref[0] = state[...]


def _ssd_chunk(gate_ref, u_ref, dt_ref, dtb_ref, alog_ref, dsk_ref, nw_ref, e2_ref, y_ref, state, *, q, valid, c):
    xs = u_ref[:, :SSD_WIDTH].astype(F32)
    bm = u_ref[:, SSD_WIDTH:SSD_WIDTH + SSD_GROUPS * SSD_STATE]
    cmb = u_ref[:, SSD_WIDTH + SSD_GROUPS * SSD_STATE:]

    dtr = dt_ref[...] + dtb_ref[...]
    dt = jnp.maximum(dtr, 0.0) + jnp.log(1.0 + jnp.exp(-jnp.abs(dtr)))
    if valid is not None:
        row = lax.broadcasted_iota(jnp.int32, (q, DT_PAD), 0) + c * q
        dt = jnp.where(row < valid, dt, 0.0)
    a = -jnp.exp(alog_ref[...])
    ad = dt * a
    ri = lax.broadcasted_iota(jnp.int32, (q, q), 0)
    ci = lax.broadcasted_iota(jnp.int32, (q, q), 1)
    tril = ri >= ci
    tril_b = jnp.where(tril, 1.0, 0.0).astype(BF16)
    ad_hi = ad.astype(BF16)
    ad_r = ad - ad_hi.astype(F32)
    ad_mid = ad_r.astype(BF16)
    ad_lo = (ad_r - ad_mid.astype(F32)).astype(BF16)
    acum = _dot(tril_b, ad_hi) + _dot(tril_b, ad_mid) + _dot(tril_b, ad_lo)
    acum_t = acum.T
    tot = acum[q - 1:q, :]
    e2 = e2_ref[...]
    expanded = _expand_heads(jnp.concatenate([dt, dt * jnp.exp(tot - acum), jnp.exp(acum)], axis=0), e2)
    dtx = expanded[0:q]
    ddx = expanded[q:2 * q]
    eax = expanded[2 * q:3 * q]
    dsk = _expand_heads(jnp.broadcast_to(dsk_ref[...], (SUBLANES, DT_PAD)), e2)[0:1]

    xdt = (xs * dtx).astype(BF16)
    xdtd = (xs * ddx).astype(BF16)
    bm_t = bm.astype(F32).T.astype(BF16)
    gw = SSD_REP * SSD_HEAD_DIM
    stripe = lax.broadcasted_iota(jnp.int32, (q, gw), 1) // SSD_HEAD_DIM
    ys = []
    for g in range(SSD_GROUPS):
        cm_g = cmb[:, g * SSD_STATE:(g + 1) * SSD_STATE]
        bt_g = bm_t[g * SSD_STATE:(g + 1) * SSD_STATE, :]
        cbm = _dot(cm_g, bt_g)
        ms = []
        for r in range(SSD_REP):
            h = g * SSD_REP + r
            diff = acum[:, h:h + 1] - acum_t[h:h + 1, :]
            ms.append((cbm * jnp.exp(jnp.where(tril, diff, -jnp.inf))).astype(BF16))
        ydf = _dot(jnp.concatenate(ms, axis=0), xdt[:, g * gw:(g + 1) * gw])
        yd = ydf[0:q]
        for r in range(1, SSD_REP):
            yd = jnp.where(stripe == r, ydf[r * q:(r + 1) * q], yd)
        st = state[g]
        y_off = _dot(cm_g, st.astype(BF16)) * eax[:, g * gw:(g + 1) * gw]
        state[g] = st * eax[q - 1:q, g * gw:(g + 1) * gw] + _dot(bt_g, xdtd[:, g * gw:(g + 1) * gw])
        ys.append(yd + y_off)
    y = jnp.concatenate(ys, axis=1) + dsk * xs
    y = y * gate_ref[...].astype(F32)
    outs = []
    for g in range(SSD_WIDTH // SSD_NORM_GROUP):
        yg = y[:, g * SSD_NORM_GROUP:(g + 1) * SSD_NORM_GROUP]
        outs.append(yg * lax.rsqrt(jnp.mean(yg * yg, axis=-1, keepdims=True) + SSD_NORM_EPS))
    y_ref[...] = (jnp.concatenate(outs, axis=1) * nw_ref[...]).astype(BF16)


def _ssd(gate, u, dt, h0, dtb, alog, dsk, nw, e2, *, nb, seq, q, valid):
    chunks = SSD_CHUNKS_PER_STEP if (seq // q) % SSD_CHUNKS_PER_STEP == 0 else 1
    nc = seq // (q * chunks)
    row = lambda c: pl.BlockSpec((q * chunks, c), lambda b, i: (b * nc + i, 0))
    gw = SSD_REP * SSD_HEAD_DIM
    st_spec = pl.BlockSpec((1, SSD_GROUPS, SSD_STATE, gw), lambda b, i: (b, 0, 0, 0))
    return pl.pallas_call(
        functools.partial(_ssd_kernel, q=q, valid=valid, chunks=chunks),
        grid=(nb, nc),
        in_specs=[row(SSD_WIDTH), row(SSD_XBC), row(DT_PAD), st_spec,
                  _const_spec((1, DT_PAD)), _const_spec((1, DT_PAD)), _const_spec((1, DT_PAD)),
                  _const_spec((1, SSD_WIDTH)), _const_spec((2 * DT_PAD, SSD_WIDTH))],
        out_specs=[row(SSD_WIDTH), st_spec],
        out_shape=[jax.ShapeDtypeStruct((nb * seq, SSD_WIDTH), BF16),
                   jax.ShapeDtypeStruct((nb, SSD_GROUPS, SSD_STATE, gw), F32)],
        scratch_shapes=[pltpu.VMEM((SSD_GROUPS, SSD_STATE, gw), F32)],
        compiler_params=pltpu.CompilerParams(dimension_semantics=("arbitrary", "arbitrary"),
                                             vmem_limit_bytes=VMEM_LIMIT),
        name="ssd",
    )(gate, u, dt, h0, dtb, alog, dsk, nw, e2)


def _column_max(x):
    while x.shape[0] > SUBLANES and x.shape[0] % (2 * SUBLANES) == 0:
        half = x.shape[0] // 2
        x = jnp.maximum(x[:half], x[half:])
    return jnp.max(x, axis=0, keepdims=True)


def _attn_tile_counts(i, *, tq, tk, past, kv_len, minimum=min):
    q_lo = past + i * tq
    q_hi = q_lo + tq - 1
    lim_lo = minimum((q_lo // CHUNK + 1) * CHUNK, kv_len)
    lim_hi = minimum((q_hi // CHUNK + 1) * CHUNK, kv_len)
    return lim_lo // tk, (lim_hi + tk - 1) // tk


def _diff_attn_kernel(qt_ref, qt_next_ref, k_ref, vt_ref, lam_ref, sw_ref, o_ref,
                      q2t_ref, s_ref, p_ref, m_ref, alpha_ref, acc_ref, *, tq, tk, past, kv_len, lam_init, nbuf, nkt):
    i = pl.program_id(2)
    w = 2 * tq
    row = lax.broadcasted_iota(jnp.int32, (ATT_V_DIM, tq), 0)
    n_full, _ = _attn_tile_counts(i, tq=tq, tk=tk, past=past, kv_len=kv_len, minimum=jnp.minimum)
    n_visits = n_full + 1

    def scores(j):
        kt = k_ref[pl.ds(pl.multiple_of(j * tk, tk), tk), :]
        return _dot(kt, q2t_ref[...])

    def visited_tile(v):
        return jnp.where(v == 0, n_full, jnp.maximum(v - 1, 0))

    ncb = w // LANES
    pw = p_ref.shape[-1]

    def store_scores(buf, s):
        for cb in range(ncb):
            s_ref[buf, cb] = s[:, cb * LANES:(cb + 1) * LANES]

    def stage_a(v, buf):
        store_scores(buf, scores(jnp.minimum(v - 1, n_full)))

    rows = min(tk, ATT_SOFTMAX_ROWS)

    def stage_b(buf):
        for cb in range(ncb):
            sl = slice(cb * LANES, (cb + 1) * LANES)
            m_old = m_ref[:, sl]
            m_new = m_old
            for r0 in range(0, tk, rows):
                m_new = jnp.maximum(m_new, _column_max(s_ref[buf, cb, r0:r0 + rows, :]))
            m_ref[:, sl] = m_new
            alpha_ref[buf, :, sl] = jnp.exp2(m_old - m_new)
            pl0 = (cb * LANES) % pw
            for r0 in range(0, tk, rows):
                p_ref[buf, cb * LANES // pw, r0:r0 + rows, pl0:pl0 + LANES] = jnp.exp2(
                    (s_ref[buf, cb, r0:r0 + rows, :] - m_new).astype(BF16))

    def stage_c(v, buf):
        vt = vt_ref[visited_tile(v)]
        for pb in range(w // pw):
            sl = slice(pb * pw, (pb + 1) * pw)
            acc_ref[:, sl] = alpha_ref[buf, :, sl] * acc_ref[:, sl] + _dot(vt, p_ref[buf, pb])

    def first_visit(qt, tile):
        zero = jnp.zeros_like(qt)
        q2t_ref[:, 0:tq] = jnp.where(row < ATT_HEAD_DIM, qt, zero)
        q2t_ref[:, tq:w] = jnp.where(row >= ATT_HEAD_DIM, qt, zero)
        nf, _ = _attn_tile_counts(tile, tq=tq, tk=tk, past=past, kv_len=kv_len, minimum=jnp.minimum)
        nf = jnp.minimum(nf, nkt - 1)
        qchunk = (past + tile * tq + lax.broadcasted_iota(jnp.int32, (1, w), 1) % tq) // CHUNK
        s_part = scores(nf)
        s_masked = []
        for kb in range(tk // CHUNK):
            k0 = nf * tk + kb * CHUNK
            kchunk = jnp.where(k0 < kv_len, k0 // CHUNK, jnp.iinfo(jnp.int32).max)
            s_masked.append(jnp.where(kchunk <= qchunk, s_part[kb * CHUNK:(kb + 1) * CHUNK], NEG_BIG))
        store_scores(0, jnp.concatenate(s_masked, axis=0))

    @pl.when(i == 0)
    def _():
        first_visit(qt_ref[0], i)

    m_ref[...] = jnp.full_like(m_ref, NEG_BIG)
    acc_ref[...] = jnp.zeros_like(acc_ref)
    p_ref[nbuf - 1] = jnp.zeros(p_ref.shape[1:], BF16)
    alpha_ref[nbuf - 1] = jnp.ones(alpha_ref.shape[1:], F32)

    def visits(v0, count, prefetch_last):
        for r in range(count):
            prefetch = r + 1 < count or prefetch_last
            if prefetch and nbuf > 1:
                stage_a(v0 + r + 1, (r + 1) % nbuf)
            stage_c(v0 + r - 1, (r - 1) % nbuf)
            stage_b(r)
            if prefetch and nbuf == 1:
                stage_a(v0 + r + 1, (r + 1) % nbuf)

    def trip(u, carry):
        visits(nbuf * u, nbuf, True)
        return carry

    lax.fori_loop(0, n_visits // nbuf, trip, 0)
    for rem in range(nbuf):

        @pl.when(n_visits % nbuf == rem)
        def _():
            visits(n_visits - rem, rem, False)
            stage_c(n_visits - 1, (rem - 1) % nbuf)

    first_visit(qt_next_ref[0], i + 1)

    lam_v = lam_ref[...]
    lam = (jnp.exp(jnp.sum(lam_v[0:1] * lam_v[1:2], axis=-1, keepdims=True))
           - jnp.exp(jnp.sum(lam_v[2:3] * lam_v[3:4], axis=-1, keepdims=True)) + lam_init)
    acc = acc_ref[...]
    o = acc[:ATT_V_DIM] * (1.0 / acc[ATT_V_DIM:ATT_V_DIM + 1])
    o = o[:, :tq] - lam * o[:, tq:]
    o = o * lax.rsqrt(jnp.mean(o * o, axis=0, keepdims=True) + ATT_NORM_EPS) * (sw_ref[...] * (1.0 - lam_init))
    o_ref[...] = o.T.astype(BF16)


def _diff_attn(qt, k, vt, lam_vecs, subln_col, *, nb, tq, tk, past, kv_len, lam_init, nbuf):
    nq = qt.shape[0] // nb
    nkt = vt.shape[0] // nb
    assert tk % CHUNK == 0 and kv_len % CHUNK == 0
    for i in range(nq):
        n_full, n_end = _attn_tile_counts(i, tq=tq, tk=tk, past=past, kv_len=kv_len)
        assert n_end - n_full == 1 and n_end <= nkt, (i, n_full, n_end)
    w = 2 * tq
    return pl.pallas_call(
        functools.partial(_diff_attn_kernel, tq=tq, tk=tk, past=past, kv_len=kv_len, lam_init=lam_init, nbuf=nbuf,
                          nkt=nkt),
        grid=(nb, ATT_HEADS, nq),
        in_specs=[pl.BlockSpec((1, ATT_V_DIM, tq), lambda b, h, i: (b * nq + i, h, 0)),
                  pl.BlockSpec((1, ATT_V_DIM, tq), lambda b, h, i: (b * nq + jnp.minimum(i + 1, nq - 1), h, 0)),
                  pl.BlockSpec((nkt * tk, ATT_V_DIM), lambda b, h, i: (b, h)),
                  pl.BlockSpec((nkt, ATT_VT_ROWS, tk), lambda b, h, i: (b, h, 0)),
                  _const_spec((4, ATT_HEAD_DIM)), _const_spec((ATT_V_DIM, 1))],
        out_specs=pl.BlockSpec((tq, ATT_V_DIM), lambda b, h, i: (b * nq + i, h)),
        out_shape=jax.ShapeDtypeStruct((nb * nq * tq, ATT_WIDTH), BF16),
        scratch_shapes=[pltpu.VMEM((ATT_V_DIM, w), BF16), pltpu.VMEM((nbuf, w // LANES, tk, LANES), F32),
                        pltpu.VMEM((nbuf, w // MXU_TILE, tk, MXU_TILE), BF16), pltpu.VMEM((1, w), F32),
                        pltpu.VMEM((nbuf, 1, w), F32),
                        pltpu.VMEM((ATT_VT_ROWS, w), F32)],
        compiler_params=pltpu.CompilerParams(dimension_semantics=("arbitrary", "arbitrary", "arbitrary"),
                                             vmem_limit_bytes=VMEM_LIMIT),
        name="diff_attn",
    )(qt, qt, k, vt, lam_vecs, subln_col)


def _decode_attn_kernel(q_ref, kn_ref, vn_ref, kc_ref, vc_ref, lam_ref, sw_ref, o_ref, *, seq, past, lam_init):
    lam_v = lam_ref[...]
    lam = (jnp.exp(jnp.sum(lam_v[0:1] * lam_v[1:2], axis=-1, keepdims=True))
           - jnp.exp(jnp.sum(lam_v[2:3] * lam_v[3:4], axis=-1, keepdims=True)) + lam_init)
    npad = LANES
    lane = lax.broadcasted_iota(jnp.int32, (seq, LANES), 1)
    qchunk = (past + lax.broadcasted_iota(jnp.int32, (2 * seq, 1), 0) % seq) // CHUNK
    kpos_p = lax.broadcasted_iota(jnp.int32, (2 * seq, past), 1)
    kpos_n = lax.broadcasted_iota(jnp.int32, (2 * seq, npad), 1)
    vis_p = kpos_p // CHUNK <= qchunk
    vis_n = ((past + kpos_n) // CHUNK <= qchunk) & (kpos_n < seq)
    pad_rows = jnp.zeros((npad - seq, LANES), BF16)
    for h in range(ATT_HEADS):
        sl = slice(h * LANES, (h + 1) * LANES)
        qh = q_ref[:, sl]
        zero = jnp.zeros_like(qh)
        q2 = jnp.concatenate([jnp.where(lane < ATT_HEAD_DIM, qh, zero), jnp.where(lane >= ATT_HEAD_DIM, qh, zero)], axis=0)
        s_p = jnp.where(vis_p, _dot(q2, kc_ref[0, sl, :].astype(BF16)), NEG_BIG)
        kn = jnp.concatenate([kn_ref[:, sl].astype(BF16), pad_rows], axis=0)
        s_n = lax.dot_general(q2, kn, (((1,), (1,)), ((), ())), preferred_element_type=F32)
        s_n = jnp.where(vis_n, s_n, NEG_BIG)
        m = jnp.maximum(jnp.max(s_p, axis=-1, keepdims=True), jnp.max(s_n, axis=-1, keepdims=True))
        p_p = jnp.exp2(s_p - m)
        p_n = jnp.exp2(s_n - m)
        l = jnp.sum(p_p, axis=-1, keepdims=True) + jnp.sum(p_n, axis=-1, keepdims=True)
        vh = vc_ref[0, pl.ds(h, past, stride=ATT_HEADS), :].astype(BF16)
        vn = jnp.concatenate([vn_ref[:, sl].astype(BF16), pad_rows], axis=0)
        o = (_dot(p_p.astype(BF16), vh) + _dot(p_n.astype(BF16), vn)) / l
        o = o[:seq] - lam * o[seq:]
        o = o * lax.rsqrt(jnp.mean(o * o, axis=-1, keepdims=True) + ATT_NORM_EPS) * (sw_ref[...] * (1.0 - lam_init))
        o_ref[:, sl] = o.astype(BF16)


def _decode_attn(q, k_new, v_new, kt_cache, v_cache, lam_vecs, subln_row, *, nb, seq, past, lam_init):
    row = pl.BlockSpec((seq, ATT_WIDTH), lambda b: (b, 0))
    return pl.pallas_call(
        functools.partial(_decode_attn_kernel, seq=seq, past=past, lam_init=lam_init),
        grid=(nb,),
        in_specs=[row, row, row,
                  pl.BlockSpec((1, ATT_WIDTH, past), lambda b: (b, 0, 0)),
                  pl.BlockSpec((1, past * ATT_HEADS, ATT_V_DIM), lambda b: (b, 0, 0)),
                  _const_spec((4, ATT_HEAD_DIM)), _const_spec((1, ATT_V_DIM))],
        out_specs=row,
        out_shape=jax.ShapeDtypeStruct((nb * seq, ATT_WIDTH), BF16),
        compiler_params=pltpu.CompilerParams(dimension_semantics=("arbitrary",), vmem_limit_bytes=VMEM_LIMIT),
        name="decode_attn",
    )(q, k_new, v_new, kt_cache, v_cache, lam_vecs, subln_row)


def _post_mix_kernel(x_ref, ys_ref, ya_ref, mk_ref, mv_ref, wo_ref, wq_ref, wox_ref,
                     g1_ref, g2_ref, g3_ref, h_ref, *, spt, rows, cached_mem):
    lane_blocks = MEM_HEAD_DIM // LANES

    def mem_head(ref, s_i, hd):
        if not cached_mem:
            return ref[s_i, :, hd * MEM_HEAD_DIM:(hd + 1) * MEM_HEAD_DIM]
        parts = [ref[s_i, pl.ds(cb * MEM_HEADS + hd, MEM_LEN, stride=lane_blocks * MEM_HEADS), :].astype(BF16)
                 for cb in range(lane_blocks)]
        return jnp.concatenate(parts, axis=1)

    mix = _dot(ys_ref[...], wo_ref[0:SSD_WIDTH, :]) + _dot(ya_ref[...], wo_ref[SSD_WIDTH:, :])
    h = x_ref[...] + _rms(mix, g1_ref[...], NORM_EPS)
    qn = _rms(h, g2_ref[...], NORM_EPS).astype(BF16)
    qx = (_dot(qn, wq_ref[...]) * (math.log2(math.e) / math.sqrt(MEM_HEAD_DIM))).astype(BF16)
    ox_seqs = []
    for s_i in range(spt):
        qs = qx[s_i * rows:(s_i + 1) * rows]
        oxs = []
        for hd in range(MEM_HEADS):
            sl = slice(hd * MEM_HEAD_DIM, (hd + 1) * MEM_HEAD_DIM)
            s = lax.dot_general(qs[:, sl], mem_head(mk_ref, s_i, hd), (((1,), (1,)), ((), ())),
                                preferred_element_type=F32)
            p = jnp.exp2(s - jnp.max(s, axis=-1, keepdims=True))
            ox = _dot(p.astype(BF16), mem_head(mv_ref, s_i, hd)) * (1.0 / jnp.sum(p, axis=-1, keepdims=True))
            oxs.append(ox.astype(BF16))
        ox_seqs.append(jnp.concatenate(oxs, axis=1))
    ox_all = ox_seqs[0] if spt == 1 else jnp.concatenate(ox_seqs, axis=0)
    o2 = _dot(ox_all, wox_ref[...])
    h_ref[...] = h + _rms(o2, g3_ref[...], NORM_EPS)


def _post_mix(x, ys, ya, mk, mv, w_out, wq, wox, g1, g2, g3, *, seq, tm):
    n = x.shape[0]
    spt = max(1, tm // seq)
    tiles_per_seq = max(1, seq // tm)
    row = lambda c: pl.BlockSpec((tm, c), lambda i: (i, 0))
    cached_mem = mk.dtype == F32
    mem = pl.BlockSpec((spt,) + mk.shape[1:], lambda i: (i // tiles_per_seq, 0, 0))
    wspec = _const_spec((D_MODEL, D_MODEL))
    gspec = _const_spec((1, D_MODEL))
    return pl.pallas_call(
        functools.partial(_post_mix_kernel, spt=spt, rows=tm // spt, cached_mem=cached_mem),
        grid=(n // tm,),
        in_specs=[row(D_MODEL), row(SSD_WIDTH), row(ATT_WIDTH), mem, mem, wspec, wspec, wspec, gspec, gspec, gspec],
        out_specs=row(D_MODEL),
        out_shape=jax.ShapeDtypeStruct((n, D_MODEL), F32),
        compiler_params=pltpu.CompilerParams(dimension_semantics=("arbitrary",), vmem_limit_bytes=VMEM_LIMIT),
        name="post_mix",
    )(x, ys, ya, mk, mv, w_out, wq, wox, g1, g2, g3)


def _ffn_kernel(h_ref, wg_ref, wu_ref, wd_ref, g1_ref, g2_ref, o_ref):
    h = h_ref[...]
    hn = _rms(h, g1_ref[...], NORM_EPS).astype(BF16)
    act = (_silu(_dot(hn, wg_ref[...])) * _dot(hn, wu_ref[...])).astype(BF16)
    f = _dot(act, wd_ref[...])
    o_ref[...] = h + _rms(f, g2_ref[...], NORM_EPS)


def _ffn(h, wg, wu, wd, g1, g2, *, tm):
    n = h.shape[0]
    row = pl.BlockSpec((tm, D_MODEL), lambda i: (i, 0))
    return pl.pallas_call(
        _ffn_kernel,
        grid=(n // tm,),
        in_specs=[row, _const_spec((D_MODEL, FFN_HIDDEN)), _const_spec((D_MODEL, FFN_HIDDEN)),
                  _const_spec((FFN_HIDDEN, D_MODEL)), _const_spec((1, D_MODEL)), _const_spec((1, D_MODEL))],
        out_specs=row,
        out_shape=jax.ShapeDtypeStruct((n, D_MODEL), F32),
        compiler_params=pltpu.CompilerParams(dimension_semantics=("arbitrary",), vmem_limit_bytes=VMEM_LIMIT),
        name="ffn",
    )(h, wg, wu, wd, g1, g2)


def _rope_tables(past, seq):
    inv = np.power(ROPE_THETA, -np.arange(0, ATT_HEAD_DIM, 2, dtype=np.float64) / ATT_HEAD_DIM)
    ang = (past + np.arange(seq, dtype=np.float64))[:, None] * inv[None, :]
    cos, sin = np.cos(ang), np.sin(ang)
    reps = LANES // ATT_HEAD_DIM
    cos_t = np.tile(np.concatenate([cos, cos], axis=-1), (1, reps)).astype(np.float32)
    sin_t = np.tile(np.concatenate([-sin, sin], axis=-1), (1, reps)).astype(np.float32)
    return jnp.asarray(cos_t), jnp.asarray(sin_t)


def _mem_cache_rows(c):
    b = c.shape[0]
    c = c.reshape(b, MEM_LEN, MEM_HEADS, MEM_HEAD_DIM // LANES, LANES)
    return c.transpose(0, 1, 3, 2, 4).reshape(b, MEM_LEN * (MEM_HEAD_DIM // LANES) * MEM_HEADS, LANES)


def _state_to_kernel_layout(s):
    b = s.shape[0]
    s = s.reshape(b, SSD_GROUPS, SSD_REP, SSD_HEAD_DIM, SSD_STATE)
    return s.transpose(0, 1, 4, 2, 3).reshape(b, SSD_GROUPS, SSD_STATE, SSD_REP * SSD_HEAD_DIM)


def _state_from_kernel_layout(s):
    b = s.shape[0]
    s = s.reshape(b, SSD_GROUPS, SSD_STATE, SSD_REP, SSD_HEAD_DIM)
    return s.transpose(0, 1, 3, 4, 2).reshape(b, SSD_HEADS, SSD_HEAD_DIM, SSD_STATE)


def _layer(x, conv_buf, ssm0, kt_past, v_past, mem_kb, mem_vb, lam_init, p, *, tm, tk, tq, ssd_q):
    nb, seq, _ = x.shape
    n = nb * seq
    past = 0 if kt_past is None else kt_past.shape[2]
    xf = x.reshape(n, D_MODEL)
    cos, sin = _rope_tables(past, seq)
    no_history = kt_past is None
    assert tq == tk or not no_history
    cbuf = jnp.pad(conv_buf.astype(F32), ((0, 0), (SUBLANES - (SSD_CONV - 1), 0), (0, 0)))
    gate, u, tail, dt, q, k, v, qt, kb, vt = _in_proj(xf, p["g_pre_mix"], p["w_in_ssd"], p["w_in_qkv"], cos, sin, cbuf,
                                                       p["conv_w"], p["conv_b"], seq=seq, tm=tm,
                                                       tt=tk if no_history else 0)

    seq_pad = -(-seq // ssd_q) * ssd_q
    if seq_pad != seq:
        pad = lambda a: jnp.pad(a.reshape(nb, seq, -1), ((0, 0), (0, seq_pad - seq), (0, 0))).reshape(nb * seq_pad, -1)
        gate_s, u_s, dt_s = pad(gate), pad(u), pad(dt)
    else:
        gate_s, u_s, dt_s = gate, u, dt
    y_ssd, h_new = _ssd(gate_s, u_s, dt_s, _state_to_kernel_layout(ssm0.astype(F32)),
                        p["dt_bias"], p["a_log"], p["d_skip"], p["ssm_norm_w"], p["e2"],
                        nb=nb, seq=seq_pad, q=ssd_q, valid=None if seq_pad == seq else seq)
    if seq_pad != seq:
        y_ssd = y_ssd.reshape(nb, seq_pad, SSD_WIDTH)[:, :seq].reshape(n, SSD_WIDTH)
    ssm_new = _state_from_kernel_layout(h_new)
    ext_tail = tail.reshape(nb, -1, SUBLANES, SSD_XBC)[:, -1]
    if seq >= SSD_CONV - 1:
        conv_new = ext_tail[:, SUBLANES - (SSD_CONV - 1):]
    else:
        conv_new = jnp.concatenate([conv_buf.astype(F32), ext_tail[:, SUBLANES - seq:]], axis=1)[:, -(SSD_CONV - 1):]

    if no_history:
        kv_len = seq
        y_att = _diff_attn(qt, kb, vt, p["lam_vecs"], p["subln_col"], nb=nb, tq=tq, tk=tk,
                           past=past, kv_len=kv_len, lam_init=lam_init, nbuf=ATT_PIPELINE_BUFFERS)
    else:
        y_att = _decode_attn(q, k, v, kt_past, v_past, p["lam_vecs"], p["subln_col"].reshape(1, ATT_V_DIM),
                             nb=nb, seq=seq, past=past, lam_init=lam_init)

    h = _post_mix(xf, y_ssd, y_att, mem_kb, mem_vb, p["w_out"], p["wq_x"], p["wo_x"],
                  p["g_post_mix"], p["g_pre_x"], p["g_post_x"], seq=seq, tm=tm)
    out = _ffn(h, p["w_gate"], p["w_up"], p["w_down"], p["g_pre_ffn"], p["g_post_ffn"], tm=tm)
    if no_history:
        k_out = k.reshape(nb, ATT_HEADS, 2, ATT_HEAD_DIM, seq).transpose(0, 4, 1, 2, 3)
    else:
        k_out = k.reshape(nb, seq, ATT_HEADS, 2, ATT_HEAD_DIM)
    return (out.reshape(nb, seq, D_MODEL), k_out,
            v.reshape(nb, seq, ATT_HEADS, ATT_V_DIM), ssm_new, conv_new)


def _prep_params(i, w_in, conv_w, conv_b, dt_bias, a_log, d_skip, ssm_norm_w, lam_q1, lam_k1, lam_q2, lam_k2, subln_w,
                 w_out, wq_x, wo_x, g_pre_mix, g_post_mix, g_pre_x, g_post_x, g_pre_ffn, g_post_ffn,
                 w_gate, w_up, w_down):
    w = w_in[i]
    s1 = SSD_WIDTH + SSD_XBC + SSD_HEADS
    w_ssd = jnp.pad(w[:, :s1].astype(BF16), ((0, 0), (0, DT_PAD - SSD_HEADS)))
    w_qkv = w[:, s1:].astype(BF16)
    head_pad = lambda a: jnp.pad(a[i].astype(F32), (0, DT_PAD - SSD_HEADS)).reshape(1, DT_PAD)
    e = (jnp.arange(DT_PAD)[:, None] == (jnp.arange(SSD_WIDTH)[None, :] // SSD_HEAD_DIM)).astype(BF16)
    row = lambda a: a[i].astype(F32).reshape(1, -1)
    w_out_b, wq_b, wo_b = _cast_bf16([w_out[i], wq_x[i], wo_x[i]], steps=4)
    w_gate_b, w_up_b, w_down_b = _cast_bf16([w_gate[i], w_up[i], w_down[i]], steps=8)
    return {
        "w_in_ssd": w_ssd, "w_in_qkv": w_qkv, "conv_w": conv_w[i].astype(F32), "conv_b": row(conv_b),
        "dt_bias": head_pad(dt_bias), "a_log": head_pad(a_log), "d_skip": head_pad(d_skip),
        "ssm_norm_w": row(ssm_norm_w), "e2": jnp.concatenate([e, e], axis=0),
        "lam_vecs": jnp.stack([lam_q1[i], lam_k1[i], lam_q2[i], lam_k2[i]]).astype(F32), "subln_col": subln_w[i].astype(F32).reshape(ATT_V_DIM, 1),
        "w_out": w_out_b, "wq_x": wq_b, "wo_x": wo_b,
        "g_pre_mix": row(g_pre_mix), "g_post_mix": row(g_post_mix), "g_pre_x": row(g_pre_x), "g_post_x": row(g_post_x),
        "g_pre_ffn": row(g_pre_ffn), "g_post_ffn": row(g_post_ffn),
        "w_gate": w_gate_b, "w_up": w_up_b, "w_down": w_down_b,
    }


def kernel(x_prompt, x_sample, cache_attn_k, cache_attn_v, state_ssm, state_conv, cache_mem_k, cache_mem_v, mem_prompt, w_in, conv_w, conv_b, dt_bias, a_log, d_skip, ssm_norm_w, lam_q1, lam_k1, lam_q2, lam_k2, subln_w, w_out, g_mem, wq_x, wk_x, wv_x, wo_x, g_pre_mix, g_post_mix, g_pre_x, g_post_x, g_pre_ffn, g_post_ffn, w_gate, w_up, w_down):
    depth = w_in.shape[0]
    bp, sp, _ = x_prompt.shape
    bs, ss, _ = x_sample.shape
    hp, hs = x_prompt, x_sample
    outs = [[] for _ in range(10)]
    for i in range(depth):
        lam_init = 0.8 - 0.6 * math.exp(-0.3 * i)
        p = _prep_params(i, w_in, conv_w, conv_b, dt_bias, a_log, d_skip, ssm_norm_w, lam_q1, lam_k1, lam_q2, lam_k2,
                         subln_w, w_out, wq_x, wo_x, g_pre_mix, g_post_mix, g_pre_x, g_post_x, g_pre_ffn, g_post_ffn,
                         w_gate, w_up, w_down)
        mk, mv, mkb, mvb = _mem_kv(mem_prompt.reshape(bp * MEM_LEN, D_MODEL), g_mem[i].reshape(1, D_MODEL),
                                   wk_x[i], wv_x[i])
        hp, k_new, v_new, ssm_new, conv_new = _layer(
            hp, jnp.zeros((bp, SSD_CONV - 1, SSD_XBC), F32), jnp.zeros((bp, SSD_HEADS, SSD_HEAD_DIM, SSD_STATE), F32),
            None, None, mkb.reshape(bp, MEM_LEN, D_MODEL), mvb.reshape(bp, MEM_LEN, D_MODEL), lam_init, p,
            tm=512, tk=512, tq=512, ssd_q=256)
        for lst, val in zip(outs[:6], (k_new, v_new, ssm_new, conv_new,
                                       mk.reshape(bp, MEM_LEN, MEM_HEADS, MEM_HEAD_DIM),
                                       mv.reshape(bp, MEM_LEN, MEM_HEADS, MEM_HEAD_DIM))):
            lst.append(val)
        past = cache_attn_k.shape[2]
        hs, k_new, v_new, ssm_new, conv_new = _layer(
            hs, state_conv[i], state_ssm[i],
            cache_attn_k[i].transpose(0, 2, 3, 4, 1).reshape(bs, ATT_WIDTH, past),
            cache_attn_v[i].reshape(bs, past * ATT_HEADS, ATT_V_DIM),
            _mem_cache_rows(cache_mem_k[i]), _mem_cache_rows(cache_mem_v[i]), lam_init, p,
            tm=bs * ss, tk=0, tq=0, ssd_q=ss)
        for lst, val in zip(outs[6:], (k_new, v_new, ssm_new, conv_new)):
            lst.append(val)
    return (hp, hs) + tuple(jnp.stack(o) for o in outs)
```
